```python
import math
import jax, jax.numpy as jnp
from jax import lax
import numpy as np

D_MODEL = 1024
BATCH = 8
SEQ = 8192
DEPTH = 1

MLA_HEADS = 8
MLA_Q_RANK = 384
MLA_KV_RANK = 256
MLA_NOPE = 64
MLA_ROPE = 32
MLA_V = 64
MLA_SCALE = 1.0 / math.sqrt(MLA_NOPE + MLA_ROPE)
ROPE_THETA = 10000.0
FOX_HEADS = 8
FOX_DIM = 64
FOX_SCALE = 1.0 / math.sqrt(FOX_DIM)
D_FF = 2816
CONV_WIDTH = 3
BLOCK_Q = 128
EPS = 1e-6
N_ADA = 6

IN_SPLITS = (
    MLA_Q_RANK,
    MLA_KV_RANK,
    MLA_ROPE,
    FOX_HEADS * FOX_DIM,
    FOX_HEADS * FOX_DIM,
    FOX_HEADS * FOX_DIM,
    FOX_HEADS,
    D_MODEL,
    D_MODEL,
)
D_IN = sum(IN_SPLITS)
IN_OFFSETS = tuple(int(v) for v in np.cumsum(IN_SPLITS)[:-1])

kernel_name = "hybrid_mla_fox_convffn_adaln"


def rmsnorm(x, g):
    xf = x.astype(jnp.float32)
    y = xf * lax.rsqrt(jnp.mean(xf * xf, axis=-1, keepdims=True) + EPS)
    return (y * g.astype(jnp.float32)).astype(x.dtype)


def rope(x, positions):
    r = x.shape[-1]
    inv_freq = ROPE_THETA ** (-jnp.arange(0, r, 2, dtype=jnp.float32) / r)
    ang = positions.astype(jnp.float32)[..., None] * inv_freq
    cos = jnp.cos(ang)[:, :, None, :]
    sin = jnp.sin(ang)[:, :, None, :]
    xf = x.astype(jnp.float32)
    x1, x2 = xf[..., : r // 2], xf[..., r // 2:]
    out = jnp.concatenate([x1 * cos - x2 * sin, x2 * cos + x1 * sin], axis=-1)
    return out.astype(x.dtype)


def causal_block_attention(q, k, v, scale, log_cum=None):
    b, s, h, dk = q.shape
    nb = s // BLOCK_Q
    idx = jnp.arange(nb)
    qb = q.reshape(b, nb, BLOCK_Q, h, dk).transpose(1, 0, 2, 3, 4)
    key_pos = jnp.arange(s)
    if log_cum is not None:
        f_keys = log_cum.transpose(0, 2, 1)
        fb = log_cum.reshape(b, nb, BLOCK_Q, h).transpose(1, 0, 2, 3)
        xs = (idx, qb, fb)
    else:
        xs = (idx, qb)

    def one_block(args):
        i, qi = args[0], args[1]
        logits = jnp.einsum('bqhd,bkhd->bhqk', qi, k,
                            preferred_element_type=jnp.float32) * scale
        if log_cum is not None:
            fi = args[2].transpose(0, 2, 1)
            logits = logits + (fi[..., :, None] - f_keys[:, :, None, :])
        q_pos = i * BLOCK_Q + jnp.arange(BLOCK_Q)
        mask = key_pos[None, :] <= q_pos[:, None]
        logits = jnp.where(mask[None, None], logits, -jnp.inf)
        p = jax.nn.softmax(logits, axis=-1)
        return jnp.einsum('bhqk,bkhd->bqhd', p.astype(v.dtype), v)

    out = lax.map(one_block, xs)
    return out.transpose(1, 0, 2, 3, 4).reshape(b, s, h, v.shape[-1])


def hybrid_mixer(h, positions, w_in, q_norm_g, w_uq, kv_norm_g, w_ukv, b_forget,
                 w_o_mla, w_o_fox, w_out):
    b, s, _ = h.shape
    proj = h @ w_in
    cq, ckv, k_rope, fq, fk, fv, f_logit, g_mla, g_fox = jnp.split(proj, IN_OFFSETS, axis=-1)

    q = (rmsnorm(cq, q_norm_g) @ w_uq).reshape(b, s, MLA_HEADS, MLA_NOPE + MLA_ROPE)
    q_nope, q_rope = q[..., :MLA_NOPE], q[..., MLA_NOPE:]
    kv = (rmsnorm(ckv, kv_norm_g) @ w_ukv).reshape(b, s, MLA_HEADS, MLA_NOPE + MLA_V)
    k_nope, v_mla = kv[..., :MLA_NOPE], kv[..., MLA_NOPE:]
    q_rope = rope(q_rope, positions)
    k_rope = rope(k_rope[:, :, None, :], positions)
    q_mla = jnp.concatenate([q_nope, q_rope], axis=-1)
    k_mla = jnp.concatenate(
        [k_nope, jnp.broadcast_to(k_rope, (b, s, MLA_HEADS, MLA_ROPE))], axis=-1)
    o_mla = causal_block_attention(q_mla, k_mla, v_mla, MLA_SCALE).reshape(b, s, MLA_HEADS * MLA_V)

    log_f = jax.nn.log_sigmoid(f_logit.astype(jnp.float32) + b_forget.astype(jnp.float32))
    f_cum = jnp.cumsum(log_f, axis=1)
    o_fox = causal_block_attention(
        fq.reshape(b, s, FOX_HEADS, FOX_DIM), fk.reshape(b, s, FOX_HEADS, FOX_DIM),
        fv.reshape(b, s, FOX_HEADS, FOX_DIM), FOX_SCALE, f_cum).reshape(b, s, FOX_HEADS * FOX_DIM)

    y = jax.nn.sigmoid(g_mla) * (o_mla @ w_o_mla) + jax.nn.sigmoid(g_fox) * (o_fox @ w_o_fox)
    return y @ w_out


def conv_ffn(h, w_up, conv_w, conv_b, w_down):
    s = h.shape[1]
    u = h @ w_up
    up = jnp.pad(u, ((0, 0), (CONV_WIDTH - 1, 0), (0, 0)))
    u = conv_b + sum(conv_w[j] * up[:, j:j + s] for j in range(CONV_WIDTH))
    gate, val = u[..., :D_FF], u[..., D_FF:]
    return (jax.nn.silu(gate) * val) @ w_down


def modulate(hn, shift, scale):
    return hn * (1.0 + scale[:, None, :]) + shift[:, None, :]


def _fwd_setup_inputs(seed: int = 0) -> dict:
    key = jax.random.key(seed)
    ks = jax.random.split(key, 24)
    nrm = lambda k, shape, fan: jax.random.normal(k, shape, jnp.float32) * (fan ** -0.5)
    L = DEPTH
    x = jax.random.normal(ks[0], (BATCH, SEQ, D_MODEL), jnp.float32)
    c = jax.random.normal(ks[1], (BATCH, D_MODEL), jnp.float32)
    offsets = jax.random.randint(ks[2], (BATCH, 1), 0, 4096, dtype=jnp.int32)
    positions = offsets + jnp.arange(SEQ, dtype=jnp.int32)[None, :]
    return {
        "x": x,
        "c": c,
        "positions": positions,
        "w_ada": 0.5 * nrm(ks[3], (L, D_MODEL, N_ADA * D_MODEL), D_MODEL),
        "b_ada": 0.01 * jax.random.normal(ks[4], (L, N_ADA * D_MODEL), jnp.float32),
        "norm_mix_g": 1.0 + 0.1 * jax.random.normal(ks[5], (L, D_MODEL), jnp.float32),
        "w_in": nrm(ks[6], (L, D_MODEL, D_IN), D_MODEL),
        "q_norm_g": 1.0 + 0.1 * jax.random.normal(ks[7], (L, MLA_Q_RANK), jnp.float32),
        "w_uq": nrm(ks[8], (L, MLA_Q_RANK, MLA_HEADS * (MLA_NOPE + MLA_ROPE)), MLA_Q_RANK),
        "kv_norm_g": 1.0 + 0.1 * jax.random.normal(ks[9], (L, MLA_KV_RANK), jnp.float32),
        "w_ukv": nrm(ks[10], (L, MLA_KV_RANK, MLA_HEADS * (MLA_NOPE + MLA_V)), MLA_KV_RANK),
        "b_forget": jax.random.uniform(ks[11], (L, FOX_HEADS), jnp.float32, 1.0, 6.0),
        "w_o_mla": nrm(ks[12], (L, MLA_HEADS * MLA_V, D_MODEL), MLA_HEADS * MLA_V),
        "w_o_fox": nrm(ks[13], (L, FOX_HEADS * FOX_DIM, D_MODEL), FOX_HEADS * FOX_DIM),
        "w_out": nrm(ks[14], (L, D_MODEL, D_MODEL), D_MODEL),
        "norm_ffn_g": 1.0 + 0.1 * jax.random.normal(ks[15], (L, D_MODEL), jnp.float32),
        "w_up": nrm(ks[16], (L, D_MODEL, 2 * D_FF), D_MODEL),
        "conv_w": nrm(ks[17], (L, CONV_WIDTH, 2 * D_FF), CONV_WIDTH),
        "conv_b": 0.01 * jax.random.normal(ks[18], (L, 2 * D_FF), jnp.float32),
        "w_down": nrm(ks[19], (L, D_FF, D_MODEL), D_FF),
        "norm_final_g": 1.0 + 0.1 * jax.random.normal(ks[20], (D_MODEL,), jnp.float32),
    }


def _fwd_reference(x, c, positions, w_ada, b_ada, norm_mix_g, w_in, q_norm_g, w_uq, kv_norm_g,
              w_ukv, b_forget, w_o_mla, w_o_fox, w_out, norm_ffn_g, w_up, conv_w, conv_b,
              w_down, norm_final_g):
    c_act = jax.nn.silu(c)
    for l in range(DEPTH):
        ada = c_act @ w_ada[l] + b_ada[l]
        sh_m, sc_m, g_m, sh_f, sc_f, g_f = jnp.split(ada, N_ADA, axis=-1)
        h = modulate(rmsnorm(x, norm_mix_g[l]), sh_m, sc_m)
        mix = hybrid_mixer(h, positions, w_in[l], q_norm_g[l], w_uq[l], kv_norm_g[l], w_ukv[l],
                           b_forget[l], w_o_mla[l], w_o_fox[l], w_out[l])
        x = x + g_m[:, None, :] * mix
        h = modulate(rmsnorm(x, norm_ffn_g[l]), sh_f, sc_f)
        x = x + g_f[:, None, :] * conv_ffn(h, w_up[l], conv_w[l], conv_b[l], w_down[l])
    return rmsnorm(x, norm_final_g)


import jax as _jax
import jax.numpy as _jnp

TWIN_FORMAT = 'train_step'
FWD_PARAMS = ['x', 'c', 'positions', 'w_ada', 'b_ada', 'norm_mix_g', 'w_in', 'q_norm_g', 'w_uq', 'kv_norm_g', 'w_ukv', 'b_forget', 'w_o_mla', 'w_o_fox', 'w_out', 'norm_ffn_g', 'w_up', 'conv_w', 'conv_b', 'w_down', 'norm_final_g']
TWIN_WEIGHTS = ['w_ada', 'b_ada', 'norm_mix_g', 'w_in', 'q_norm_g', 'w_uq', 'kv_norm_g', 'w_ukv', 'b_forget', 'w_o_mla', 'w_o_fox', 'w_out', 'norm_ffn_g', 'w_up', 'conv_w', 'conv_b', 'w_down', 'norm_final_g']
TWIN_DIFF_INPUT = 'x'
TWIN_INPUTS = ['x', 'c', 'positions', 'w_ada', 'b_ada', 'norm_mix_g', 'w_in', 'q_norm_g', 'w_uq', 'kv_norm_g', 'w_ukv', 'b_forget', 'w_o_mla', 'w_o_fox', 'w_out', 'norm_ffn_g', 'w_up', 'conv_w', 'conv_b', 'w_down', 'norm_final_g', 'loss_target', 'm_w_ada', 'm_b_ada', 'm_norm_mix_g', 'm_w_in', 'm_q_norm_g', 'm_w_uq', 'm_kv_norm_g', 'm_w_ukv', 'm_b_forget', 'm_w_o_mla', 'm_w_o_fox', 'm_w_out', 'm_norm_ffn_g', 'm_w_up', 'm_conv_w', 'm_conv_b', 'm_w_down', 'm_norm_final_g', 'v_w_ada', 'v_b_ada', 'v_norm_mix_g', 'v_w_in', 'v_q_norm_g', 'v_w_uq', 'v_kv_norm_g', 'v_w_ukv', 'v_b_forget', 'v_w_o_mla', 'v_w_o_fox', 'v_w_out', 'v_norm_ffn_g', 'v_w_up', 'v_conv_w', 'v_conv_b', 'v_w_down', 'v_norm_final_g']
TWIN_OUTPUTS = ['loss', 'grad_x', 'grad_w_ada', 'grad_b_ada', 'grad_norm_mix_g', 'grad_w_in', 'grad_q_norm_g', 'grad_w_uq', 'grad_kv_norm_g', 'grad_w_ukv', 'grad_b_forget', 'grad_w_o_mla', 'grad_w_o_fox', 'grad_w_out', 'grad_norm_ffn_g', 'grad_w_up', 'grad_conv_w', 'grad_conv_b', 'grad_w_down', 'grad_norm_final_g', 'delta_w_ada', 'delta_b_ada', 'delta_norm_mix_g', 'delta_w_in', 'delta_q_norm_g', 'delta_w_uq', 'delta_kv_norm_g', 'delta_w_ukv', 'delta_b_forget', 'delta_w_o_mla', 'delta_w_o_fox', 'delta_w_out', 'delta_norm_ffn_g', 'delta_w_up', 'delta_conv_w', 'delta_conv_b', 'delta_w_down', 'delta_norm_final_g', 'new_m_w_ada', 'new_m_b_ada', 'new_m_norm_mix_g', 'new_m_w_in', 'new_m_q_norm_g', 'new_m_w_uq', 'new_m_kv_norm_g', 'new_m_w_ukv', 'new_m_b_forget', 'new_m_w_o_mla', 'new_m_w_o_fox', 'new_m_w_out', 'new_m_norm_ffn_g', 'new_m_w_up', 'new_m_conv_w', 'new_m_conv_b', 'new_m_w_down', 'new_m_norm_final_g', 'new_v_w_ada', 'new_v_b_ada', 'new_v_norm_mix_g', 'new_v_w_in', 'new_v_q_norm_g', 'new_v_w_uq', 'new_v_kv_norm_g', 'new_v_w_ukv', 'new_v_b_forget', 'new_v_w_o_mla', 'new_v_w_o_fox', 'new_v_w_out', 'new_v_norm_ffn_g', 'new_v_w_up', 'new_v_conv_w', 'new_v_conv_b', 'new_v_w_down', 'new_v_norm_final_g']
TWIN_LEAF_KINDS = {'loss': 'loss', 'grad_x': 'grad_x', 'grad_w_ada': 'grad_w', 'grad_b_ada': 'grad_w', 'grad_norm_mix_g': 'grad_w', 'grad_w_in': 'grad_w', 'grad_q_norm_g': 'grad_w', 'grad_w_uq': 'grad_w', 'grad_kv_norm_g': 'grad_w', 'grad_w_ukv': 'grad_w', 'grad_b_forget': 'grad_w', 'grad_w_o_mla': 'grad_w', 'grad_w_o_fox': 'grad_w', 'grad_w_out': 'grad_w', 'grad_norm_ffn_g': 'grad_w', 'grad_w_up': 'grad_w', 'grad_conv_w': 'grad_w', 'grad_conv_b': 'grad_w', 'grad_w_down': 'grad_w', 'grad_norm_final_g': 'grad_w', 'delta_w_ada': 'delta_w', 'delta_b_ada': 'delta_w', 'delta_norm_mix_g': 'delta_w', 'delta_w_in': 'delta_w', 'delta_q_norm_g': 'delta_w', 'delta_w_uq': 'delta_w', 'delta_kv_norm_g': 'delta_w', 'delta_w_ukv': 'delta_w', 'delta_b_forget': 'delta_w', 'delta_w_o_mla': 'delta_w', 'delta_w_o_fox': 'delta_w', 'delta_w_out': 'delta_w', 'delta_norm_ffn_g': 'delta_w', 'delta_w_up': 'delta_w', 'delta_conv_w': 'delta_w', 'delta_conv_b': 'delta_w', 'delta_w_down': 'delta_w', 'delta_norm_final_g': 'delta_w', 'new_m_w_ada': 'new_m', 'new_m_b_ada': 'new_m', 'new_m_norm_mix_g': 'new_m', 'new_m_w_in': 'new_m', 'new_m_q_norm_g': 'new_m', 'new_m_w_uq': 'new_m', 'new_m_kv_norm_g': 'new_m', 'new_m_w_ukv': 'new_m', 'new_m_b_forget': 'new_m', 'new_m_w_o_mla': 'new_m', 'new_m_w_o_fox': 'new_m', 'new_m_w_out': 'new_m', 'new_m_norm_ffn_g': 'new_m', 'new_m_w_up': 'new_m', 'new_m_conv_w': 'new_m', 'new_m_conv_b': 'new_m', 'new_m_w_down': 'new_m', 'new_m_norm_final_g': 'new_m', 'new_v_w_ada': 'new_v', 'new_v_b_ada': 'new_v', 'new_v_norm_mix_g': 'new_v', 'new_v_w_in': 'new_v', 'new_v_q_norm_g': 'new_v', 'new_v_w_uq': 'new_v', 'new_v_kv_norm_g': 'new_v', 'new_v_w_ukv': 'new_v', 'new_v_b_forget': 'new_v', 'new_v_w_o_mla': 'new_v', 'new_v_w_o_fox': 'new_v', 'new_v_w_out': 'new_v', 'new_v_norm_ffn_g': 'new_v', 'new_v_w_up': 'new_v', 'new_v_conv_w': 'new_v', 'new_v_conv_b': 'new_v', 'new_v_w_down': 'new_v', 'new_v_norm_final_g': 'new_v'}


def _forward(args):
    return _fwd_reference(*[args[k] for k in FWD_PARAMS])


def _output_shape():
    def fwd():
        inp = _fwd_setup_inputs(0)
        return _fwd_reference(*[inp[k] for k in FWD_PARAMS])
    out = _jax.eval_shape(fwd)
    return out.shape, out.dtype

N_MICROBATCH = 1
ADAM_LR = 0.001
ADAM_B1 = 0.9
ADAM_B2 = 0.999
ADAM_EPS = 1e-08
ADAM_WD = 0.01
ADAM_STEP = 10
PER_EXAMPLE_BATCH_AXIS = {'x': 0, 'c': 0, 'positions': 0, 'loss_target': 0}
SHARED_INPUTS = []
_WEIGHT_DTYPES = {'w_ada': _jnp.float32, 'b_ada': _jnp.float32, 'norm_mix_g': _jnp.float32, 'w_in': _jnp.float32, 'q_norm_g': _jnp.float32, 'w_uq': _jnp.float32, 'kv_norm_g': _jnp.float32, 'w_ukv': _jnp.float32, 'b_forget': _jnp.float32, 'w_o_mla': _jnp.float32, 'w_o_fox': _jnp.float32, 'w_out': _jnp.float32, 'norm_ffn_g': _jnp.float32, 'w_up': _jnp.float32, 'conv_w': _jnp.float32, 'conv_b': _jnp.float32, 'w_down': _jnp.float32, 'norm_final_g': _jnp.float32}
MOMENT_SCALE = {'w_ada': 1.249569e-01, 'b_ada': 2.827286e-01, 'norm_mix_g': 2.974671e-02, 'w_in': 1.717965e-02, 'q_norm_g': 1.160830e-02, 'w_uq': 8.617218e-03, 'kv_norm_g': 3.553503e-02, 'w_ukv': 1.665163e-02, 'b_forget': 9.583308e-02, 'w_o_mla': 1.534508e-02, 'w_o_fox': 2.037493e-02, 'w_out': 2.555484e-02, 'norm_ffn_g': 7.341569e-02, 'w_up': 3.441205e-02, 'conv_w': 3.482325e-02, 'conv_b': 3.235669e-02, 'w_down': 5.750577e-02, 'norm_final_g': 6.456968e+01}


def _to_microbatches(a, axis):
    t = _jnp.moveaxis(a, axis, 0)
    t = t.reshape((N_MICROBATCH, t.shape[0] // N_MICROBATCH) + t.shape[1:])
    return _jnp.moveaxis(t, 1, axis + 1)


def setup_inputs(seed: int = 0) -> dict:
    inp = _fwd_setup_inputs(seed)
    key = _jax.random.fold_in(_jax.random.key(seed), 7919)
    shape, _ = _output_shape()
    out = dict(inp)
    out["loss_target"] = _jax.random.normal(_jax.random.fold_in(key, 0), shape, _jnp.float32)
    for i, name in enumerate(TWIN_WEIGHTS):
        w = inp[name].astype(_jnp.float32)
        if MOMENT_SCALE is None:
            s = _jnp.sqrt(_jnp.mean(_jnp.square(w)) + 1e-30)
        else:
            s = MOMENT_SCALE[name]
        km, kv = _jax.random.split(_jax.random.fold_in(key, i + 1))
        out[name] = w
        out["m_" + name] = s * _jax.random.normal(km, w.shape, _jnp.float32)
        out["v_" + name] = (s * s) * _jax.random.uniform(kv, w.shape, _jnp.float32, 0.5, 1.5)
    if N_MICROBATCH > 1:
        for name, axis in PER_EXAMPLE_BATCH_AXIS.items():
            out[name] = _to_microbatches(out[name], axis)
    return {'x': out['x'], 'c': out['c'], 'positions': out['positions'], 'w_ada': out['w_ada'], 'b_ada': out['b_ada'], 'norm_mix_g': out['norm_mix_g'], 'w_in': out['w_in'], 'q_norm_g': out['q_norm_g'], 'w_uq': out['w_uq'], 'kv_norm_g': out['kv_norm_g'], 'w_ukv': out['w_ukv'], 'b_forget': out['b_forget'], 'w_o_mla': out['w_o_mla'], 'w_o_fox': out['w_o_fox'], 'w_out': out['w_out'], 'norm_ffn_g': out['norm_ffn_g'], 'w_up': out['w_up'], 'conv_w': out['conv_w'], 'conv_b': out['conv_b'], 'w_down': out['w_down'], 'norm_final_g': out['norm_final_g'], 'loss_target': out['loss_target'], 'm_w_ada': out['m_w_ada'], 'm_b_ada': out['m_b_ada'], 'm_norm_mix_g': out['m_norm_mix_g'], 'm_w_in': out['m_w_in'], 'm_q_norm_g': out['m_q_norm_g'], 'm_w_uq': out['m_w_uq'], 'm_kv_norm_g': out['m_kv_norm_g'], 'm_w_ukv': out['m_w_ukv'], 'm_b_forget': out['m_b_forget'], 'm_w_o_mla': out['m_w_o_mla'], 'm_w_o_fox': out['m_w_o_fox'], 'm_w_out': out['m_w_out'], 'm_norm_ffn_g': out['m_norm_ffn_g'], 'm_w_up': out['m_w_up'], 'm_conv_w': out['m_conv_w'], 'm_conv_b': out['m_conv_b'], 'm_w_down': out['m_w_down'], 'm_norm_final_g': out['m_norm_final_g'], 'v_w_ada': out['v_w_ada'], 'v_b_ada': out['v_b_ada'], 'v_norm_mix_g': out['v_norm_mix_g'], 'v_w_in': out['v_w_in'], 'v_q_norm_g': out['v_q_norm_g'], 'v_w_uq': out['v_w_uq'], 'v_kv_norm_g': out['v_kv_norm_g'], 'v_w_ukv': out['v_w_ukv'], 'v_b_forget': out['v_b_forget'], 'v_w_o_mla': out['v_w_o_mla'], 'v_w_o_fox': out['v_w_o_fox'], 'v_w_out': out['v_w_out'], 'v_norm_ffn_g': out['v_norm_ffn_g'], 'v_w_up': out['v_w_up'], 'v_conv_w': out['v_conv_w'], 'v_conv_b': out['v_conv_b'], 'v_w_down': out['v_w_down'], 'v_norm_final_g': out['v_norm_final_g']}


def _loss(weights, diff, rest, loss_target):
    with _jax.named_scope("forward"):
        args = {**rest, TWIN_DIFF_INPUT: diff, **{k: w.astype(_WEIGHT_DTYPES[k]) for k, w in weights.items()}}
        y = _forward(args)
    with _jax.named_scope("loss_head"):
        err = _jnp.square(y.astype(_jnp.float32) - loss_target)
        return 0.5 * _jnp.sum(_jnp.mean(err, axis=-1)) if err.ndim else 0.5 * err


def _adamw(w, g, m, v):
    m = ADAM_B1 * m + (1.0 - ADAM_B1) * g
    v = ADAM_B2 * v + (1.0 - ADAM_B2) * _jnp.square(g)
    m_hat = m / (1.0 - ADAM_B1 ** ADAM_STEP)
    v_hat = v / (1.0 - ADAM_B2 ** ADAM_STEP)
    delta = -ADAM_LR * (m_hat / (_jnp.sqrt(v_hat) + ADAM_EPS) + ADAM_WD * w)
    return delta, m, v


def reference(x, c, positions, w_ada, b_ada, norm_mix_g, w_in, q_norm_g, w_uq, kv_norm_g, w_ukv, b_forget, w_o_mla, w_o_fox, w_out, norm_ffn_g, w_up, conv_w, conv_b, w_down, norm_final_g, loss_target, m_w_ada, m_b_ada, m_norm_mix_g, m_w_in, m_q_norm_g, m_w_uq, m_kv_norm_g, m_w_ukv, m_b_forget, m_w_o_mla, m_w_o_fox, m_w_out, m_norm_ffn_g, m_w_up, m_conv_w, m_conv_b, m_w_down, m_norm_final_g, v_w_ada, v_b_ada, v_norm_mix_g, v_w_in, v_q_norm_g, v_w_uq, v_kv_norm_g, v_w_ukv, v_b_forget, v_w_o_mla, v_w_o_fox, v_w_out, v_norm_ffn_g, v_w_up, v_conv_w, v_conv_b, v_w_down, v_norm_final_g):
    given = dict(x=x, c=c, positions=positions, w_ada=w_ada, b_ada=b_ada, norm_mix_g=norm_mix_g, w_in=w_in, q_norm_g=q_norm_g, w_uq=w_uq, kv_norm_g=kv_norm_g, w_ukv=w_ukv, b_forget=b_forget, w_o_mla=w_o_mla, w_o_fox=w_o_fox, w_out=w_out, norm_ffn_g=norm_ffn_g, w_up=w_up, conv_w=conv_w, conv_b=conv_b, w_down=w_down, norm_final_g=norm_final_g, loss_target=loss_target, m_w_ada=m_w_ada, m_b_ada=m_b_ada, m_norm_mix_g=m_norm_mix_g, m_w_in=m_w_in, m_q_norm_g=m_q_norm_g, m_w_uq=m_w_uq, m_kv_norm_g=m_kv_norm_g, m_w_ukv=m_w_ukv, m_b_forget=m_b_forget, m_w_o_mla=m_w_o_mla, m_w_o_fox=m_w_o_fox, m_w_out=m_w_out, m_norm_ffn_g=m_norm_ffn_g, m_w_up=m_w_up, m_conv_w=m_conv_w, m_conv_b=m_conv_b, m_w_down=m_w_down, m_norm_final_g=m_norm_final_g, v_w_ada=v_w_ada, v_b_ada=v_b_ada, v_norm_mix_g=v_norm_mix_g, v_w_in=v_w_in, v_q_norm_g=v_q_norm_g, v_w_uq=v_w_uq, v_kv_norm_g=v_kv_norm_g, v_w_ukv=v_w_ukv, v_b_forget=v_b_forget, v_w_o_mla=v_w_o_mla, v_w_o_fox=v_w_o_fox, v_w_out=v_w_out, v_norm_ffn_g=v_norm_ffn_g, v_w_up=v_w_up, v_conv_w=v_conv_w, v_conv_b=v_conv_b, v_w_down=v_w_down, v_norm_final_g=v_norm_final_g)
    weights = {n: given[n] for n in TWIN_WEIGHTS}
    shared = {n: given[n] for n in SHARED_INPUTS}
    per_example = {n: given[n] for n in ['x', 'c', 'positions']}
    grad_fn = _jax.value_and_grad(_loss, argnums=(0, 1))

    def one_microbatch(ex, loss_target):
        ex = dict(ex)
        diff = ex.pop(TWIN_DIFF_INPUT)
        return grad_fn(weights, diff, {**shared, **ex}, loss_target)

    if N_MICROBATCH == 1:
        loss, (grad_w, grad_x) = one_microbatch(per_example, given["loss_target"])
    else:
        def body(carry, xs):
            loss_sum, grad_sum = carry
            l_k, (gw_k, gx_k) = one_microbatch(xs[0], xs[1])
            with _jax.named_scope("update"):
                return (loss_sum + l_k, _jax.tree.map(_jnp.add, grad_sum, gw_k)), gx_k

        init = (_jnp.zeros((), _jnp.float32), _jax.tree.map(_jnp.zeros_like, weights))
        (loss, grad_w), grad_x = _jax.lax.scan(body, init, (per_example, given["loss_target"]))
    with _jax.named_scope("update"):
        delta_w, new_m, new_v = {}, {}, {}
        for n in TWIN_WEIGHTS:
            delta_w[n], new_m[n], new_v[n] = _adamw(weights[n], grad_w[n], given["m_" + n], given["v_" + n])
    return (loss, grad_x, *[grad_w[n] for n in TWIN_WEIGHTS], *[delta_w[n] for n in TWIN_WEIGHTS],
            *[new_m[n] for n in TWIN_WEIGHTS], *[new_v[n] for n in TWIN_WEIGHTS])
```

```python
import math

import numpy as np
import jax
import jax.numpy as jnp
from jax import lax
from jax.experimental import pallas as pl
from jax.experimental.pallas import tpu as pltpu

F32 = jnp.float32
BF16 = jnp.bfloat16

N_DEV = 8
D_MODEL = 1024
N_HEADS = 8
HEAD_PAD = 128
MLA_Q_RANK = 384
MLA_KV_RANK = 256
MLA_NOPE = 64
MLA_ROPE = 32
MLA_V = 64
FOX_DIM = 64
D_FF = 2816
N_ADA = 6
EPS = 1e-6
ROPE_THETA = 10000.0
MLA_SCALE = 1.0 / math.sqrt(MLA_NOPE + MLA_ROPE)
FOX_SCALE = 1.0 / math.sqrt(FOX_DIM)
IN_SPLITS = (384, 256, 32, 512, 512, 512, 8, 1024, 1024)
D_IN = sum(IN_SPLITS)
IN_OFF = tuple(int(v) for v in np.cumsum((0,) + IN_SPLITS))
P_GM, P_GF, P_FQ, P_FK, P_FV, P_CQ, P_CKV, P_KR, P_FL, D_IN_P = 0, 1024, 2048, 2560, 3072, 3584, 3968, 4224, 4352, 4480

ADAM_LR, ADAM_B1, ADAM_B2, ADAM_EPS, ADAM_WD, ADAM_STEP = 0.001, 0.9, 0.999, 1e-08, 0.01, 10

VMEM_LIMIT_BYTES = 56 * 1024 * 1024
NEG_BIG = -1e30
ATT_T = 256
ROW_T = 256
SEQ_LANES = 128


def _params(sem):
    return pltpu.CompilerParams(dimension_semantics=sem, vmem_limit_bytes=VMEM_LIMIT_BYTES)


def _tile(n, target):
    if n <= target:
        return n
    t = (target // 128) * 128
    while t >= 128:
        if n % t == 0:
            return t
        t -= 128
    return n


def _vec_spec(w, nargs):
    if nargs == 1:
        return pl.BlockSpec((1, w), lambda i: (0, 0))
    return pl.BlockSpec((1, w), lambda i, j: (0, 0))


def _all_gather(x, name):
    def body(x_ref, out_ref, send_sems, recv_sems, local_sem):
        x_, y_, c_ = lax.axis_index("x"), lax.axis_index("y"), lax.axis_index("c")
        me, sibling = (x_, y_, c_), (x_, y_, 1 - c_)
        chips = [(1 - x_, y_), (x_, 1 - y_), (1 - x_, 1 - y_)]

        def slot(px, py, pc):
            return out_ref.at[4 * px + 2 * py + pc]

        def copy(k, block, to, src=None):
            return pltpu.make_async_remote_copy(
                src_ref=slot(*block) if src is None else src, dst_ref=slot(*block),
                send_sem=send_sems.at[k], recv_sem=recv_sems.at[k],
                device_id=to, device_id_type=pl.DeviceIdType.MESH)

        mine = pltpu.make_async_copy(x_ref, slot(*me), local_sem)
        mine.start()
        first = [copy(0, me, sibling, src=x_ref)]
        first += [copy(1 + j, me, (*chip, c_), src=x_ref) for j, chip in enumerate(chips)]
        for cp in first:
            cp.start()
        passed = [copy(4 + j, (*chip, c_), sibling) for j, chip in enumerate(chips)]
        for j, chip in enumerate(chips):
            copy(1 + j, (*chip, c_), me).wait_recv()
            passed[j].start()
        copy(0, sibling, me).wait_recv()
        for j, chip in enumerate(chips):
            copy(4 + j, (*chip, 1 - c_), me).wait_recv()
        for cp in first + passed:
            cp.wait_send()
        mine.wait()

    return pl.pallas_call(
        body, name=name,
        out_shape=jax.ShapeDtypeStruct((N_DEV,) + x.shape, x.dtype),
        in_specs=[pl.BlockSpec(memory_space=pl.ANY)],
        out_specs=pl.BlockSpec(memory_space=pl.ANY),
        scratch_shapes=[pltpu.SemaphoreType.DMA((7,)), pltpu.SemaphoreType.DMA((7,)), pltpu.SemaphoreType.DMA(())],
    )(x)


def _all_to_all(g, name):
    def body(g_ref, out_ref, send_sems, recv_sems, local_sem):
        x_, y_, c_ = lax.axis_index("x"), lax.axis_index("y"), lax.axis_index("c")
        me = 4 * x_ + 2 * y_ + c_

        def peer(k):
            return (x_ ^ ((k >> 2) & 1), y_ ^ ((k >> 1) & 1), c_ ^ (k & 1))

        def copy(k):
            px, py, pc = peer(k)
            return pltpu.make_async_remote_copy(
                src_ref=g_ref.at[4 * px + 2 * py + pc], dst_ref=out_ref.at[me],
                send_sem=send_sems.at[k - 1], recv_sem=recv_sems.at[k - 1],
                device_id=(px, py, pc), device_id_type=pl.DeviceIdType.MESH)

        def landed(k):
            px, py, pc = peer(k)
            return pltpu.make_async_remote_copy(
                src_ref=g_ref.at[me], dst_ref=out_ref.at[4 * px + 2 * py + pc],
                send_sem=send_sems.at[k - 1], recv_sem=recv_sems.at[k - 1],
                device_id=(px, py, pc), device_id_type=pl.DeviceIdType.MESH)

        mine = pltpu.make_async_copy(g_ref.at[me], out_ref.at[me], local_sem)
        mine.start()
        sends = [copy(k) for k in range(1, N_DEV)]
        for cp in sends:
            cp.start()
        for k in range(1, N_DEV):
            landed(k).wait_recv()
        for cp in sends:
            cp.wait_send()
        mine.wait()

    return pl.pallas_call(
        body, name=name,
        out_shape=jax.ShapeDtypeStruct(g.shape, g.dtype),
        in_specs=[pl.BlockSpec(memory_space=pl.ANY)],
        out_specs=pl.BlockSpec(memory_space=pl.ANY),
        scratch_shapes=[pltpu.SemaphoreType.DMA((7,)), pltpu.SemaphoreType.DMA((7,)), pltpu.SemaphoreType.DMA(())],
    )(g)


def _mm(a, b, mode, out_dtype, name, res=None, gvec=None, tm=1024, tn=512, tk=1024):
    m, k = a.shape
    n = b.shape[1] if mode == "nn" else b.shape[0]
    tm, tn, tk = _tile(m, tm), _tile(n, tn), _tile(k, tk)
    nk = k // tk
    dims = (((1,), (0,)), ((), ())) if mode == "nn" else (((1,), (1,)), ((), ()))
    fused = res is not None

    def body(*refs):
        if fused:
            a_ref, b_ref, res_ref, g_ref, o_ref, raw_ref, acc_ref = refs
        else:
            a_ref, b_ref, o_ref, acc_ref = refs
        kk = pl.program_id(2)
        part = lax.dot_general(a_ref[...], b_ref[...], dims, preferred_element_type=F32)

        @pl.when(kk == 0)
        def _():
            acc_ref[...] = part

        @pl.when(kk > 0)
        def _():
            acc_ref[...] += part

        @pl.when(kk == nk - 1)
        def _():
            acc = acc_ref[...]
            if fused:
                raw_ref[...] = acc
                o_ref[...] = (res_ref[...] + g_ref[...] * acc).astype(o_ref.dtype)
            else:
                o_ref[...] = acc.astype(o_ref.dtype)

    a_spec = pl.BlockSpec((tm, tk), lambda i, j, kk: (i, kk))
    if mode == "nn":
        b_spec = pl.BlockSpec((tk, tn), lambda i, j, kk: (kk, j))
    else:
        b_spec = pl.BlockSpec((tn, tk), lambda i, j, kk: (j, kk))
    o_spec = pl.BlockSpec((tm, tn), lambda i, j, kk: (i, j))
    in_specs, args = [a_spec, b_spec], [a, b]
    out_specs, out_shape = o_spec, jax.ShapeDtypeStruct((m, n), out_dtype)
    if fused:
        in_specs += [o_spec, pl.BlockSpec((1, tn), lambda i, j, kk: (0, j))]
        args += [res, gvec]
        out_specs = (o_spec, o_spec)
        out_shape = (out_shape, jax.ShapeDtypeStruct((m, n), F32))
    return pl.pallas_call(
        body, name=name, grid=(m // tm, n // tn, nk),
        in_specs=in_specs, out_specs=out_specs, out_shape=out_shape,
        scratch_shapes=[pltpu.VMEM((tm, tn), F32)],
        compiler_params=_params(("parallel", "parallel", "arbitrary")),
    )(*args)


def _rms_mod(x, g, sc, sh, name):
    s, w = x.shape
    tm = _tile(s, ROW_T)

    def body(x_ref, g_ref, sc_ref, sh_ref, o_ref):
        xv = x_ref[...]
        r = lax.rsqrt(jnp.mean(xv * xv, axis=-1, keepdims=True) + EPS)
        o_ref[...] = ((xv * r * g_ref[...]) * (1.0 + sc_ref[...]) + sh_ref[...]).astype(o_ref.dtype)

    row = pl.BlockSpec((tm, w), lambda i: (i, 0))
    return pl.pallas_call(
        body, name=name, grid=(s // tm,),
        in_specs=[row, _vec_spec(w, 1), _vec_spec(w, 1), _vec_spec(w, 1)],
        out_specs=row, out_shape=jax.ShapeDtypeStruct((s, w), BF16),
        compiler_params=_params(("parallel",)),
    )(x, g, sc, sh)


def _rms_mod_bwd(dh, x, g, sc, dres, name):
    s, w = x.shape
    tm = _tile(s, ROW_T)
    has_res = dres is not None

    def body(*refs):
        if has_res:
            dh_ref, x_ref, g_ref, sc_ref, dres_ref, dx_ref, sums_ref = refs
        else:
            dh_ref, x_ref, g_ref, sc_ref, dx_ref, sums_ref = refs
        xv, dhv, gv = x_ref[...], dh_ref[...], g_ref[...]
        r = lax.rsqrt(jnp.mean(xv * xv, axis=-1, keepdims=True) + EPS)
        xhat = xv * r
        dxn = dhv * (1.0 + sc_ref[...])
        dxhat = dxn * gv
        dx = r * (dxhat - xhat * jnp.mean(dxhat * xhat, axis=-1, keepdims=True))
        if has_res:
            dx = dx + dres_ref[...]
        dx_ref[...] = dx

        @pl.when(pl.program_id(0) == 0)
        def _():
            sums_ref[...] = jnp.zeros_like(sums_ref)

        sums_ref[0:1, :] += jnp.sum(dhv, axis=0, keepdims=True)
        sums_ref[1:2, :] += jnp.sum(dhv * (xhat * gv), axis=0, keepdims=True)
        sums_ref[2:3, :] += jnp.sum(dxn * xhat, axis=0, keepdims=True)

    row = pl.BlockSpec((tm, w), lambda i: (i, 0))
    in_specs = [row, row, _vec_spec(w, 1), _vec_spec(w, 1)] + ([row] if has_res else [])
    args = [dh, x, g, sc] + ([dres] if has_res else [])
    return pl.pallas_call(
        body, name=name, grid=(s // tm,),
        in_specs=in_specs,
        out_specs=(row, pl.BlockSpec((8, w), lambda i: (0, 0))),
        out_shape=(jax.ShapeDtypeStruct((s, w), F32), jax.ShapeDtypeStruct((8, w), F32)),
        compiler_params=_params(("arbitrary",)),
    )(*args)


def _scale_bwd(dx, val, gvec, name):
    s, w = dx.shape
    tm = _tile(s, ROW_T)

    def body(dx_ref, val_ref, g_ref, d_ref, sums_ref):
        dxv = dx_ref[...]
        d_ref[...] = (dxv * g_ref[...]).astype(d_ref.dtype)

        @pl.when(pl.program_id(0) == 0)
        def _():
            sums_ref[...] = jnp.zeros_like(sums_ref)

        sums_ref[0:1, :] += jnp.sum(dxv * val_ref[...], axis=0, keepdims=True)

    row = pl.BlockSpec((tm, w), lambda i: (i, 0))
    return pl.pallas_call(
        body, name=name, grid=(s // tm,),
        in_specs=[row, row, _vec_spec(w, 1)],
        out_specs=(row, pl.BlockSpec((8, w), lambda i: (0, 0))),
        out_shape=(jax.ShapeDtypeStruct((s, w), BF16), jax.ShapeDtypeStruct((8, w), F32)),
        compiler_params=_params(("arbitrary",)),
    )(dx, val, gvec)


def _final_loss(x3, target, g, name):
    s, w = x3.shape
    tm = _tile(s, ROW_T)

    def body(x_ref, t_ref, g_ref, dx_ref, sums_ref):
        xv, gv = x_ref[...], g_ref[...]
        r = lax.rsqrt(jnp.mean(xv * xv, axis=-1, keepdims=True) + EPS)
        xhat = xv * r
        err = xhat * gv - t_ref[...]
        dy = err * (1.0 / w)
        dxhat = dy * gv
        dx_ref[...] = r * (dxhat - xhat * jnp.mean(dxhat * xhat, axis=-1, keepdims=True))

        @pl.when(pl.program_id(0) == 0)
        def _():
            sums_ref[...] = jnp.zeros_like(sums_ref)

        sums_ref[0:1, :] += jnp.sum(dy * xhat, axis=0, keepdims=True)
        sums_ref[1:2, :] += jnp.zeros((1, w), F32) + (0.5 / w) * jnp.sum(err * err)

    row = pl.BlockSpec((tm, w), lambda i: (i, 0))
    return pl.pallas_call(
        body, name=name, grid=(s // tm,),
        in_specs=[row, row, _vec_spec(w, 1)],
        out_specs=(row, pl.BlockSpec((8, w), lambda i: (0, 0))),
        out_shape=(jax.ShapeDtypeStruct((s, w), F32), jax.ShapeDtypeStruct((8, w), F32)),
        compiler_params=_params(("arbitrary",)),
    )(x3, target, g)


def _rope_block(seg, cmul, smul):
    lane = lax.broadcasted_iota(jnp.int32, seg.shape, 1)
    swapped = jnp.where(lane < MLA_NOPE + MLA_ROPE // 2,
                        pltpu.roll(seg, HEAD_PAD - MLA_ROPE // 2, 1), pltpu.roll(seg, MLA_ROPE // 2, 1))
    return seg * cmul + swapped * smul


def _rope(t, cmul, smul, name, n_rot, out_dtypes):
    s, w = t.shape
    tm = _tile(s, ROW_T)
    n_out = len(out_dtypes)
    wo = w // n_out

    def body(t_ref, c_ref, s_ref, *o_refs):
        cv, sv = c_ref[...], s_ref[...]
        for hb in range(w // HEAD_PAD):
            seg = t_ref[:, hb * HEAD_PAD:(hb + 1) * HEAD_PAD]
            if hb < n_rot:
                seg = _rope_block(seg, cv, sv)
            o_ref = o_refs[(hb * HEAD_PAD) // wo]
            col = (hb * HEAD_PAD) % wo
            o_ref[:, col:col + HEAD_PAD] = seg.astype(o_ref.dtype)

    row = pl.BlockSpec((tm, w), lambda i: (i, 0))
    tab = pl.BlockSpec((tm, HEAD_PAD), lambda i: (i, 0))
    orow = pl.BlockSpec((tm, wo), lambda i: (i, 0))
    outs = pl.pallas_call(
        body, name=name, grid=(s // tm,),
        in_specs=[row, tab, tab],
        out_specs=tuple(orow for _ in out_dtypes),
        out_shape=tuple(jax.ShapeDtypeStruct((s, wo), dt) for dt in out_dtypes),
        compiler_params=_params(("parallel",)),
    )(t, cmul, smul)
    return outs


def _rope_bwd_kv(dk, dv, cmul, smul, name):
    s, w = dk.shape
    tm = _tile(s, ROW_T)

    def body(dk_ref, dv_ref, c_ref, s_ref, o_ref):
        cv, sv = c_ref[...], s_ref[...]
        for hb in range(N_HEADS):
            lo, hi = hb * HEAD_PAD, (hb + 1) * HEAD_PAD
            o_ref[:, lo:hi] = _rope_block(dk_ref[:, lo:hi], cv, sv).astype(o_ref.dtype)
        o_ref[:, w:2 * w] = dv_ref[...].astype(o_ref.dtype)

    row = pl.BlockSpec((tm, w), lambda i: (i, 0))
    tab = pl.BlockSpec((tm, HEAD_PAD), lambda i: (i, 0))
    return pl.pallas_call(
        body, name=name, grid=(s // tm,),
        in_specs=[row, row, tab, tab],
        out_specs=pl.BlockSpec((tm, 2 * w), lambda i: (i, 0)),
        out_shape=jax.ShapeDtypeStruct((s, 2 * w), BF16),
        compiler_params=_params(("parallel",)),
    )(dk, dv, cmul, smul)


def _attn_fwd(q, k, v, fcol, frow, scale, name):
    s = q.shape[0]
    t = ATT_T
    nq = s // t
    use_f = fcol is not None

    def body(*refs):
        if use_f:
            q_ref, k_ref, v_ref, fc_ref, fr_ref, o_ref, lse_ref, m_s, l_s, acc_s = refs
        else:
            q_ref, k_ref, v_ref, o_ref, lse_ref, m_s, l_s, acc_s = refs
        qi = pl.program_id(1)
        qv = q_ref[...]
        m_s[...] = jnp.full(m_s.shape, NEG_BIG, F32)
        l_s[...] = jnp.zeros(l_s.shape, F32)
        acc_s[...] = jnp.zeros(acc_s.shape, F32)

        def step(j, masked):
            off = pl.multiple_of(j * t, t)
            kv = k_ref[pl.ds(off, t), :]
            vv = v_ref[pl.ds(off, t), :]
            sc = lax.dot_general(qv, kv, (((1,), (1,)), ((), ())), preferred_element_type=F32) * scale
            if use_f:
                sc = sc + (fc_ref[0] - fr_ref[0, j])
            if masked:
                row = lax.broadcasted_iota(jnp.int32, (t, t), 0)
                col = lax.broadcasted_iota(jnp.int32, (t, t), 1)
                sc = jnp.where(row >= col, sc, NEG_BIG)
            m_prev = m_s[...]
            m_new = jnp.maximum(m_prev, jnp.max(sc, axis=-1, keepdims=True))
            p = jnp.exp(sc - m_new)
            alpha = jnp.exp(m_prev - m_new)
            l_s[...] = alpha * l_s[...] + jnp.sum(p, axis=-1, keepdims=True)
            acc_s[...] = alpha * acc_s[...] + jnp.dot(p.astype(BF16), vv, preferred_element_type=F32)
            m_s[...] = m_new

        def loop_body(j, carry):
            step(j, False)
            return carry

        lax.fori_loop(0, qi, loop_body, 0)
        step(qi, True)
        o_ref[...] = acc_s[...] / l_s[...]
        lse_ref[0] = m_s[...] + jnp.log(l_s[...])

    qspec = pl.BlockSpec((t, HEAD_PAD), lambda h, i: (i, h))
    kspec = pl.BlockSpec((s, HEAD_PAD), lambda h, i: (0, h))
    colspec = pl.BlockSpec((1, t, 1), lambda h, i: (h, i, 0))
    in_specs, args = [qspec, kspec, kspec], [q, k, v]
    if use_f:
        in_specs += [colspec, pl.BlockSpec((1, nq, 1, t), lambda h, i: (h, 0, 0, 0))]
        args += [fcol, frow]
    return pl.pallas_call(
        body, name=name, grid=(N_HEADS, nq),
        in_specs=in_specs,
        out_specs=(qspec, colspec),
        out_shape=(jax.ShapeDtypeStruct((s, N_HEADS * HEAD_PAD), F32), jax.ShapeDtypeStruct((N_HEADS, s, 1), F32)),
        scratch_shapes=[pltpu.VMEM((t, 1), F32), pltpu.VMEM((t, 1), F32), pltpu.VMEM((t, HEAD_PAD), F32)],
        compiler_params=_params(("parallel", "arbitrary")),
    )(*args)


def _attn_delta(o, do, name):
    s, w = o.shape
    tm = _tile(s, ROW_T)

    def body(o_ref, do_ref, d_ref):
        for hb in range(N_HEADS):
            lo, hi = hb * HEAD_PAD, (hb + 1) * HEAD_PAD
            d_ref[hb] = jnp.sum(o_ref[:, lo:hi] * do_ref[:, lo:hi], axis=-1, keepdims=True)

    row = pl.BlockSpec((tm, w), lambda i: (i, 0))
    return pl.pallas_call(
        body, name=name, grid=(s // tm,),
        in_specs=[row, row],
        out_specs=pl.BlockSpec((N_HEADS, tm, 1), lambda i: (0, i, 0)),
        out_shape=jax.ShapeDtypeStruct((N_HEADS, s, 1), F32),
        compiler_params=_params(("parallel",)),
    )(o, do)


def _attn_bwd_dq(q, k, v, do, lse, delta, fcol, frow, scale, name):
    s = q.shape[0]
    t = ATT_T
    nq = s // t
    use_f = fcol is not None

    def body(*refs):
        if use_f:
            q_ref, k_ref, v_ref, do_ref, lse_ref, dl_ref, fc_ref, fr_ref, dq_ref, dr_ref, acc_s, dr_s = refs
            dr_s[...] = jnp.zeros(dr_s.shape, F32)
        else:
            q_ref, k_ref, v_ref, do_ref, lse_ref, dl_ref, dq_ref, acc_s = refs
        qi = pl.program_id(1)
        qv, dov = q_ref[...], do_ref[...]
        lse_v, dl_v = lse_ref[0], dl_ref[0]
        acc_s[...] = jnp.zeros(acc_s.shape, F32)

        def step(j, masked):
            off = pl.multiple_of(j * t, t)
            kv = k_ref[pl.ds(off, t), :]
            vv = v_ref[pl.ds(off, t), :]
            sc = lax.dot_general(qv, kv, (((1,), (1,)), ((), ())), preferred_element_type=F32) * scale
            if use_f:
                sc = sc + (fc_ref[0] - fr_ref[0, j])
            if masked:
                row = lax.broadcasted_iota(jnp.int32, (t, t), 0)
                col = lax.broadcasted_iota(jnp.int32, (t, t), 1)
                sc = jnp.where(row >= col, sc, NEG_BIG)
            p = jnp.exp(sc - lse_v)
            dp = lax.dot_general(dov, vv, (((1,), (1,)), ((), ())), preferred_element_type=F32)
            ds = p * (dp - dl_v)
            acc_s[...] += jnp.dot(ds.astype(BF16), kv, preferred_element_type=F32)
            if use_f:
                dr_s[...] += jnp.sum(ds, axis=-1, keepdims=True)

        def loop_body(j, carry):
            step(j, False)
            return carry

        lax.fori_loop(0, qi, loop_body, 0)
        step(qi, True)
        dq_ref[...] = acc_s[...] * scale
        if use_f:
            dr_ref[0] = dr_s[...]

    qspec = pl.BlockSpec((t, HEAD_PAD), lambda h, i: (i, h))
    kspec = pl.BlockSpec((s, HEAD_PAD), lambda h, i: (0, h))
    colspec = pl.BlockSpec((1, t, 1), lambda h, i: (h, i, 0))
    in_specs, args = [qspec, kspec, kspec, qspec, colspec, colspec], [q, k, v, do, lse, delta]
    out_specs, out_shape = qspec, jax.ShapeDtypeStruct((s, N_HEADS * HEAD_PAD), F32)
    scratch = [pltpu.VMEM((t, HEAD_PAD), F32)]
    if use_f:
        in_specs += [colspec, pl.BlockSpec((1, nq, 1, t), lambda h, i: (h, 0, 0, 0))]
        args += [fcol, frow]
        out_specs, out_shape = (qspec, colspec), (out_shape, jax.ShapeDtypeStruct((N_HEADS, s, 1), F32))
        scratch += [pltpu.VMEM((t, 1), F32)]
    return pl.pallas_call(
        body, name=name, grid=(N_HEADS, nq),
        in_specs=in_specs, out_specs=out_specs, out_shape=out_shape,
        scratch_shapes=scratch,
        compiler_params=_params(("parallel", "arbitrary")),
    )(*args)


def _attn_bwd_dkv(q, k, v, do, lse_row, delta_row, fcol, frow, scale, name):
    s = q.shape[0]
    t = ATT_T
    nq = s // t
    use_f = fcol is not None

    def body(*refs):
        if use_f:
            q_ref, k_ref, v_ref, do_ref, lse_ref, dl_ref, fc_ref, fr_ref, dk_ref, dv_ref, df_ref, dk_s, dv_s, df_s = refs
        else:
            q_ref, k_ref, v_ref, do_ref, lse_ref, dl_ref, dk_ref, dv_ref, dk_s, dv_s = refs
        kj = pl.program_id(1)
        kv, vv = k_ref[...], v_ref[...]
        dk_s[...] = jnp.zeros(dk_s.shape, F32)
        dv_s[...] = jnp.zeros(dv_s.shape, F32)
        if use_f:
            df_s[...] = jnp.zeros(df_s.shape, F32)

        def step(i, masked):
            off = pl.multiple_of(i * t, t)
            qv = q_ref[pl.ds(off, t), :]
            dov = do_ref[pl.ds(off, t), :]
            st = lax.dot_general(kv, qv, (((1,), (1,)), ((), ())), preferred_element_type=F32) * scale
            if use_f:
                st = st + (fr_ref[0, i] - fc_ref[0])
            if masked:
                row = lax.broadcasted_iota(jnp.int32, (t, t), 0)
                col = lax.broadcasted_iota(jnp.int32, (t, t), 1)
                st = jnp.where(col >= row, st, NEG_BIG)
            pt = jnp.exp(st - lse_ref[0, i])
            dv_s[...] += jnp.dot(pt.astype(BF16), dov, preferred_element_type=F32)
            dpt = lax.dot_general(vv, dov, (((1,), (1,)), ((), ())), preferred_element_type=F32)
            dst = pt * (dpt - dl_ref[0, i])
            dk_s[...] += jnp.dot(dst.astype(BF16), qv, preferred_element_type=F32)
            if use_f:
                df_s[...] -= jnp.sum(dst, axis=-1, keepdims=True)

        step(kj, True)

        def loop_body(i, carry):
            step(i, False)
            return carry

        lax.fori_loop(kj + 1, nq, loop_body, 0)
        dk_ref[...] = dk_s[...] * scale
        dv_ref[...] = dv_s[...]
        if use_f:
            df_ref[0] = df_s[...]

    kspec = pl.BlockSpec((t, HEAD_PAD), lambda h, j: (j, h))
    qspec = pl.BlockSpec((s, HEAD_PAD), lambda h, j: (0, h))
    rowspec = pl.BlockSpec((1, nq, 1, t), lambda h, j: (h, 0, 0, 0))
    colspec = pl.BlockSpec((1, t, 1), lambda h, j: (h, j, 0))
    in_specs, args = [qspec, kspec, kspec, qspec, rowspec, rowspec], [q, k, v, do, lse_row, delta_row]
    out_specs = [kspec, kspec]
    out_shape = [jax.ShapeDtypeStruct((s, N_HEADS * HEAD_PAD), F32)] * 2
    scratch = [pltpu.VMEM((t, HEAD_PAD), F32), pltpu.VMEM((t, HEAD_PAD), F32)]
    if use_f:
        in_specs += [colspec, rowspec]
        args += [fcol, frow]
        out_specs += [colspec]
        out_shape += [jax.ShapeDtypeStruct((N_HEADS, s, 1), F32)]
        scratch += [pltpu.VMEM((t, 1), F32)]
    return pl.pallas_call(
        body, name=name, grid=(N_HEADS, nq),
        in_specs=in_specs, out_specs=tuple(out_specs), out_shape=tuple(out_shape),
        scratch_shapes=scratch,
        compiler_params=_params(("parallel", "arbitrary")),
    )(*args)


def _gate_fwd(pm, pf, proj, name):
    s, w = pm.shape
    tm = _tile(s, ROW_T)

    def body(pm_ref, pf_ref, gm_ref, gf_ref, y_ref):
        y = jax.nn.sigmoid(gm_ref[...]) * pm_ref[...] + jax.nn.sigmoid(gf_ref[...]) * pf_ref[...]
        y_ref[...] = y.astype(y_ref.dtype)

    row = pl.BlockSpec((tm, w), lambda i: (i, 0))
    return pl.pallas_call(
        body, name=name, grid=(s // tm,),
        in_specs=[row, row, pl.BlockSpec((tm, w), lambda i: (i, P_GM // D_MODEL)),
                  pl.BlockSpec((tm, w), lambda i: (i, P_GF // D_MODEL))],
        out_specs=row, out_shape=jax.ShapeDtypeStruct((s, w), BF16),
        compiler_params=_params(("parallel",)),
    )(pm, pf, proj, proj)


def _gate_bwd(dy, pm, pf, proj, name):
    s, w = pm.shape
    tm = _tile(s, ROW_T)

    def body(dy_ref, pm_ref, pf_ref, gm_ref, gf_ref, dpm_ref, dpf_ref, dgm_ref, dgf_ref):
        dyv = dy_ref[...]
        sm, sf = jax.nn.sigmoid(gm_ref[...]), jax.nn.sigmoid(gf_ref[...])
        dpm_ref[...] = (dyv * sm).astype(BF16)
        dpf_ref[...] = (dyv * sf).astype(BF16)
        dgm_ref[...] = (dyv * pm_ref[...] * (sm * (1.0 - sm))).astype(BF16)
        dgf_ref[...] = (dyv * pf_ref[...] * (sf * (1.0 - sf))).astype(BF16)

    row = pl.BlockSpec((tm, w), lambda i: (i, 0))
    out = jax.ShapeDtypeStruct((s, w), BF16)
    return pl.pallas_call(
        body, name=name, grid=(s // tm,),
        in_specs=[row, row, row, pl.BlockSpec((tm, w), lambda i: (i, P_GM // D_MODEL)),
                  pl.BlockSpec((tm, w), lambda i: (i, P_GF // D_MODEL))],
        out_specs=(row, row, row, row), out_shape=(out, out, out, out),
        compiler_params=_params(("parallel",)),
    )(dy, pm, pf, proj, proj)


CONV_TN = 256
CONV_TM = 512
HALO = 8


def _shift_down(u, prev, n):
    rolled = pltpu.roll(u, n, 0)
    prev_rolled = pltpu.roll(prev, n, 0)
    top = jnp.concatenate([prev_rolled, rolled[HALO:]], axis=0)
    row = lax.broadcasted_iota(jnp.int32, u.shape, 0)
    return jnp.where(row < n, top, rolled)


def _conv_tile(u, prev, w_ref, b_ref):
    um1 = _shift_down(u, prev, 1)
    um2 = _shift_down(u, prev, 2)
    uc = b_ref[...] + w_ref[0:1, :] * um2 + w_ref[1:2, :] * um1 + w_ref[2:3, :] * u
    return uc, um1, um2


def _conv_specs(tm, tn, ncol_off):
    blk = lambda off: pl.BlockSpec((tm, tn), lambda j, i: (i, j + off))
    halo = lambda off: pl.BlockSpec((HALO, tn), lambda j, i: (jnp.maximum(i * (tm // HALO) - 1, 0), j + off))
    wsp = lambda off: pl.BlockSpec((3, tn), lambda j, i: (0, j + off))
    bsp = lambda off: pl.BlockSpec((1, tn), lambda j, i: (0, j + off))
    return blk, halo, wsp, bsp


def _convglu_fwd(u, conv_w, conv_b, name):
    s = u.shape[0]
    tm, tn = _tile(s, CONV_TM), CONV_TN
    nj = D_FF // tn
    blk, halo, wsp, bsp = _conv_specs(tm, tn, nj)

    def body(ug_ref, pg_ref, uv_ref, pv_ref, wg_ref, wv_ref, bg_ref, bv_ref, a_ref):
        live = (pl.program_id(1) > 0).astype(F32)
        gate, _, _ = _conv_tile(ug_ref[...], pg_ref[...] * live, wg_ref, bg_ref)
        val, _, _ = _conv_tile(uv_ref[...], pv_ref[...] * live, wv_ref, bv_ref)
        a_ref[...] = (gate * jax.nn.sigmoid(gate) * val).astype(a_ref.dtype)

    return pl.pallas_call(
        body, name=name, grid=(nj, s // tm),
        in_specs=[blk(0), halo(0), blk(nj), halo(nj), wsp(0), wsp(nj), bsp(0), bsp(nj)],
        out_specs=blk(0), out_shape=jax.ShapeDtypeStruct((s, D_FF), BF16),
        compiler_params=_params(("parallel", "arbitrary")),
    )(u, u, u, u, conv_w, conv_w, conv_b, conv_b)


def _convglu_bwd(da, u, conv_w, conv_b, name):
    s = u.shape[0]
    tm, tn = _tile(s, CONV_TM), CONV_TN
    nj = D_FF // tn
    blk, halo, wsp, bsp = _conv_specs(tm, tn, nj)

    def body(da_ref, ug_ref, pg_ref, uv_ref, pv_ref, wg_ref, wv_ref, bg_ref, bv_ref,
             dg_ref, dv_ref, sg_ref, sv_ref):
        live = (pl.program_id(1) > 0).astype(F32)
        ug, uv = ug_ref[...], uv_ref[...]
        gate, ug1, ug2 = _conv_tile(ug, pg_ref[...] * live, wg_ref, bg_ref)
        val, uv1, uv2 = _conv_tile(uv, pv_ref[...] * live, wv_ref, bv_ref)
        dav = da_ref[...]
        sig = jax.nn.sigmoid(gate)
        dgate = dav * val * (sig * (1.0 + gate * (1.0 - sig)))
        dval = dav * (gate * sig)
        dg_ref[...] = dgate
        dv_ref[...] = dval

        @pl.when(pl.program_id(1) == 0)
        def _():
            sg_ref[...] = jnp.zeros_like(sg_ref)
            sv_ref[...] = jnp.zeros_like(sv_ref)

        for s_ref, d, taps in ((sg_ref, dgate, (ug2, ug1, ug)), (sv_ref, dval, (uv2, uv1, uv))):
            for r, tap in enumerate(taps):
                s_ref[r:r + 1, :] += jnp.sum(d * tap, axis=0, keepdims=True)
            s_ref[3:4, :] += jnp.sum(d, axis=0, keepdims=True)

    sums = lambda off: pl.BlockSpec((8, tn), lambda j, i: (0, j + off))
    return pl.pallas_call(
        body, name=name, grid=(nj, s // tm),
        in_specs=[blk(0), blk(0), halo(0), blk(nj), halo(nj), wsp(0), wsp(nj), bsp(0), bsp(nj)],
        out_specs=(blk(0), blk(0), sums(0), sums(0)),
        out_shape=(jax.ShapeDtypeStruct((s, D_FF), F32), jax.ShapeDtypeStruct((s, D_FF), F32),
                   jax.ShapeDtypeStruct((8, D_FF), F32), jax.ShapeDtypeStruct((8, D_FF), F32)),
        compiler_params=_params(("parallel", "arbitrary")),
    )(da, u, u, u, u, conv_w, conv_w, conv_b, conv_b)


def _conv_transpose(d, conv_w_half, name):
    s, w = d.shape
    tm, tn = _tile(s, CONV_TM), CONV_TN
    last = s // tm - 1

    def body(d_ref, nx_ref, w_ref, o_ref):
        dv = d_ref[...]
        nxt = nx_ref[...] * (pl.program_id(1) < last).astype(F32)
        row = lax.broadcasted_iota(jnp.int32, dv.shape, 0)

        def shift_up(n):
            rolled = pltpu.roll(dv, tm - n, 0)
            nxt_rolled = pltpu.roll(nxt, HALO - n, 0)
            bottom = jnp.concatenate([rolled[:tm - HALO], nxt_rolled], axis=0)
            return jnp.where(row >= tm - n, bottom, rolled)

        out = w_ref[2:3, :] * dv + w_ref[1:2, :] * shift_up(1) + w_ref[0:1, :] * shift_up(2)
        o_ref[...] = out.astype(o_ref.dtype)

    blk = pl.BlockSpec((tm, tn), lambda j, i: (i, j))
    nxt_spec = pl.BlockSpec((HALO, tn), lambda j, i: (jnp.minimum((i + 1) * (tm // HALO), s // HALO - 1), j))
    return pl.pallas_call(
        body, name=name, grid=(w // tn, s // tm),
        in_specs=[blk, nxt_spec, pl.BlockSpec((3, tn), lambda j, i: (0, j))],
        out_specs=blk, out_shape=jax.ShapeDtypeStruct((s, w), BF16),
        compiler_params=_params(("parallel", "arbitrary")),
    )(d, d, conv_w_half)


def _split3(a):
    a1 = a.astype(BF16)
    r1 = a - a1.astype(F32)
    a2 = r1.astype(BF16)
    a3 = (r1 - a2.astype(F32)).astype(BF16)
    return a1, a2, a3


def _ones_dot_right(a, mat):
    return sum(jnp.dot(p, mat, preferred_element_type=F32) for p in _split3(a))


def _ones_dot_left(mat, a):
    return sum(jnp.dot(mat, p, preferred_element_type=F32) for p in _split3(a))


def _tri(n, cmp):
    r = lax.broadcasted_iota(jnp.int32, (n, n), 0)
    c = lax.broadcasted_iota(jnp.int32, (n, n), 1)
    return cmp(r, c).astype(BF16)


def _forget_fwd(z, bias, name):
    nh, nr, nl = z.shape

    def body(z_ref, b_ref, f_ref):
        within = _tri(nl, lambda r, c: r <= c)
        before = _tri(nr, lambda r, c: c < r)
        for h in range(nh):
            x = z_ref[h] + b_ref[h]
            lf = jnp.minimum(x, 0.0) - jnp.log(1.0 + jnp.exp(-jnp.abs(x)))
            pre = _ones_dot_right(lf, within)
            tot = jnp.zeros((nr, nl), F32) + jnp.sum(lf, axis=1, keepdims=True)
            f_ref[h] = pre + _ones_dot_left(before, tot)

    return pl.pallas_call(
        body, name=name, out_shape=jax.ShapeDtypeStruct(z.shape, F32),
        compiler_params=pltpu.CompilerParams(vmem_limit_bytes=VMEM_LIMIT_BYTES),
    )(z, bias)


def _forget_bwd(df_rows, df_cols, z, bias, name):
    nh, nr, nl = z.shape

    def body(dfr_ref, dfc_ref, z_ref, b_ref, dz_ref, db_ref):
        within = _tri(nl, lambda r, c: r >= c)
        after = _tri(nr, lambda r, c: c > r)
        for h in range(nh):
            g = dfr_ref[h] + dfc_ref[h]
            suf = _ones_dot_right(g, within)
            tot = jnp.zeros((nr, nl), F32) + jnp.sum(g, axis=1, keepdims=True)
            dlf = suf + _ones_dot_left(after, tot)
            dz = dlf * jax.nn.sigmoid(-(z_ref[h] + b_ref[h]))
            dz_ref[h] = dz
            db_ref[h] = jnp.zeros((1, nl), F32) + jnp.sum(dz)

    return pl.pallas_call(
        body, name=name,
        out_shape=(jax.ShapeDtypeStruct(z.shape, F32), jax.ShapeDtypeStruct(bias.shape, F32)),
        compiler_params=pltpu.CompilerParams(vmem_limit_bytes=VMEM_LIMIT_BYTES),
    )(df_rows, df_cols, z, bias)


def _ada_fwd(c_col, w, b, name):
    kdim, n = w.shape

    def body(c_ref, w_ref, b_ref, ada_ref, act_ref):
        wv = w_ref[...]
        for e in range(N_DEV):
            cv = c_ref[e]
            act = cv * jax.nn.sigmoid(cv)
            act_ref[e] = act
            ada_ref[e:e + 1, :] = jnp.sum(act * wv, axis=0, keepdims=True) + b_ref[...]

    return pl.pallas_call(
        body, name=name,
        out_shape=(jax.ShapeDtypeStruct((N_DEV, n), F32), jax.ShapeDtypeStruct((N_DEV, kdim, 1), F32)),
        compiler_params=pltpu.CompilerParams(vmem_limit_bytes=VMEM_LIMIT_BYTES),
    )(c_col, w, b)


def _ada_bwd(act_col, dada, name):
    kdim = act_col.shape[1]
    n = dada.shape[1]

    def body(act_ref, d_ref, g_ref):
        acc = act_ref[0] * d_ref[0:1, :]
        for e in range(1, N_DEV):
            acc = acc + act_ref[e] * d_ref[e:e + 1, :]
        g_ref[...] = acc

    return pl.pallas_call(
        body, name=name, out_shape=jax.ShapeDtypeStruct((kdim, n), F32),
        compiler_params=pltpu.CompilerParams(vmem_limit_bytes=VMEM_LIMIT_BYTES),
    )(act_col, dada)


def _adamw(parts, w, m, v, name, tr=128):
    npart, r, c = parts.shape
    tr = _tile(r, tr) if r % 8 == 0 else r

    def body(p_ref, w_ref, m_ref, v_ref, g_ref, d_ref, nm_ref, nv_ref):
        g = p_ref[0].astype(F32)
        for e in range(1, npart):
            g = g + p_ref[e].astype(F32)
        nm = ADAM_B1 * m_ref[...] + (1.0 - ADAM_B1) * g
        nv = ADAM_B2 * v_ref[...] + (1.0 - ADAM_B2) * (g * g)
        m_hat = nm / (1.0 - ADAM_B1 ** ADAM_STEP)
        v_hat = nv / (1.0 - ADAM_B2 ** ADAM_STEP)
        g_ref[...] = g
        d_ref[...] = -ADAM_LR * (m_hat / (jnp.sqrt(v_hat) + ADAM_EPS) + ADAM_WD * w_ref[...])
        nm_ref[...] = nm
        nv_ref[...] = nv

    row = pl.BlockSpec((tr, c), lambda i: (i, 0))
    out = jax.ShapeDtypeStruct((r, c), F32)
    return pl.pallas_call(
        body, name=name, grid=(r // tr,),
        in_specs=[pl.BlockSpec((npart, tr, c), lambda i: (0, i, 0)), row, row, row],
        out_specs=(row, row, row, row), out_shape=(out, out, out, out),
        compiler_params=_params(("parallel",)),
    )(parts, w, m, v)


PACK_C = 1024
BIG = ("w_in", "w_uq", "w_ukv", "w_o_mla", "w_o_fox", "w_out", "w_up", "conv_w", "w_down")
SHARD_SHAPES = {"w_in": (1024, 533), "w_uq": (384, 96), "w_ukv": (256, 128), "w_o_mla": (512, 128),
                "w_o_fox": (512, 128), "w_out": (128, 1024), "w_up": (1024, 704), "conv_w": (3, 704),
                "w_down": (352, 1024)}
PACK_ROWS = 1920


def _rows_of(name, conv_rows):
    if name == "conv_w":
        return conv_rows
    r, c = SHARD_SHAPES[name]
    return (r * c) // PACK_C


def _offsets(conv_rows):
    offs, o = {}, 0
    for nme in BIG:
        offs[nme] = o
        o += _rows_of(nme, conv_rows)
    return offs


GATHER_CONV_ROWS = 5
GRAD_CONV_ROWS = 3
GATHER_OFF = _offsets(GATHER_CONV_ROWS)
GRAD_OFF = _offsets(GRAD_CONV_ROWS)


def _pad_rows(a, rows):
    return jnp.pad(a, ((0, rows - a.shape[0]), (0, 0)))


def _flat_rows(a, rows):
    lead = a.shape[:-2]
    flat = a.reshape(lead + (-1,))
    pad = rows * PACK_C - flat.shape[-1]
    flat = jnp.pad(flat, [(0, 0)] * len(lead) + [(0, pad)])
    return flat.reshape(lead + (rows, PACK_C))


def _pack_gather(shards):
    parts = []
    for nme in BIG:
        a = shards[nme]
        if nme == "conv_w":
            bits = lax.bitcast_convert_type(a, BF16)
            parts.append(_flat_rows(bits.reshape(3, 2 * 704), GATHER_CONV_ROWS))
        else:
            parts.append(_flat_rows(a.astype(BF16), _rows_of(nme, GATHER_CONV_ROWS)))
    return _pad_rows(jnp.concatenate(parts, axis=0), PACK_ROWS)


def _unpack_gather(wg):
    out = {}
    for nme in BIG:
        r, c = SHARD_SHAPES[nme]
        o = GATHER_OFF[nme]
        seg = wg[:, o:o + _rows_of(nme, GATHER_CONV_ROWS)].reshape(N_DEV, -1)
        if nme == "conv_w":
            bits = seg[:, :2 * r * c].reshape(N_DEV, r, c, 2)
            out[nme] = lax.bitcast_convert_type(bits, F32)
        else:
            out[nme] = seg[:, :r * c].reshape(N_DEV, r, c)
    return out


def _pack_f32(shards):
    parts = [_flat_rows(shards[nme], _rows_of(nme, GRAD_CONV_ROWS)) for nme in BIG]
    cat = jnp.concatenate(parts, axis=-2)
    pad = [(0, 0)] * (cat.ndim - 2) + [(0, PACK_ROWS - cat.shape[-2]), (0, 0)]
    return jnp.pad(cat, pad)


def _unpack_f32(p):
    out = {}
    for nme in BIG:
        r, c = SHARD_SHAPES[nme]
        o = GRAD_OFF[nme]
        seg = p[o:o + _rows_of(nme, GRAD_CONV_ROWS)].reshape(-1)
        out[nme] = seg[:r * c].reshape(1, r, c)
    return out


def _cols_to_full(stack):
    n, r, c = stack.shape
    return stack.transpose(1, 0, 2).reshape(r, n * c)


def _full_to_cols(full, c):
    r = full.shape[0]
    return full.reshape(r, N_DEV, c).transpose(1, 0, 2)


def _pad_heads(a, width):
    s = a.shape[0]
    a = a.reshape(s, N_HEADS, width)
    return jnp.pad(a, ((0, 0), (0, 0), (0, HEAD_PAD - width))).reshape(s, N_HEADS * HEAD_PAD)


def _unpad_heads(a, width):
    s = a.shape[0]
    return a.reshape(s, N_HEADS, HEAD_PAD)[:, :, :width].reshape(s, N_HEADS * width)


def _w_in_padded(w_in):
    seg = [w_in[:, IN_OFF[i]:IN_OFF[i + 1]] for i in range(9)]
    cq, ckv, kr, fq, fk, fv, fl, gm, gf = seg
    padc = lambda a, n: jnp.pad(a, ((0, 0), (0, n - a.shape[1])))
    return jnp.concatenate([gm, gf, fq, fk, fv, cq, ckv, padc(kr, 128), padc(fl, 128)], axis=1)


def _w_in_unpadded(g):
    return jnp.concatenate([
        g[:, P_CQ:P_CQ + 384], g[:, P_CKV:P_CKV + 256], g[:, P_KR:P_KR + 32], g[:, P_FQ:P_FQ + 512],
        g[:, P_FK:P_FK + 512], g[:, P_FV:P_FV + 512], g[:, P_FL:P_FL + 8], g[:, P_GM:P_GM + 1024],
        g[:, P_GF:P_GF + 1024]], axis=1)


SMALL = (("b_ada", 6144, 6144), ("norm_mix_g", 1024, 1024), ("q_norm_g", 384, 384), ("kv_norm_g", 256, 256),
         ("b_forget", 8, 128), ("norm_ffn_g", 1024, 1024), ("conv_b", 5632, 5632), ("norm_final_g", 1024, 1024),
         ("loss", 1, 128))
SMALL_OFF = {}
_o = 0
for _n, _real, _padded in SMALL:
    SMALL_OFF[_n] = _o
    _o += _padded
SMALL_W = _o


def _pack_small(vals):
    parts = []
    for nme, real, padded in SMALL:
        a = vals[nme].reshape(1, real).astype(F32)
        parts.append(jnp.pad(a, ((0, 0), (0, padded - real))))
    return jnp.concatenate(parts, axis=1)


def kernel(x, c, positions, w_ada, b_ada, norm_mix_g, w_in, q_norm_g, w_uq, kv_norm_g, w_ukv, b_forget, w_o_mla, w_o_fox, w_out, norm_ffn_g, w_up, conv_w, conv_b, w_down, norm_final_g, loss_target, m_w_ada, m_b_ada, m_norm_mix_g, m_w_in, m_q_norm_g, m_w_uq, m_kv_norm_g, m_w_ukv, m_b_forget, m_w_o_mla, m_w_o_fox, m_w_out, m_norm_ffn_g, m_w_up, m_conv_w, m_conv_b, m_w_down, m_norm_final_g, v_w_ada, v_b_ada, v_norm_mix_g, v_w_in, v_q_norm_g, v_w_uq, v_kv_norm_g, v_w_ukv, v_b_forget, v_w_o_mla, v_w_o_fox, v_w_out, v_norm_ffn_g, v_w_up, v_conv_w, v_conv_b, v_w_down, v_norm_final_g):
    me = 4 * lax.axis_index("x") + 2 * lax.axis_index("y") + lax.axis_index("c")
    x = x[0]
    target = loss_target[0]
    s = x.shape[0]
    nblk = s // ATT_T
    big_w = {"w_in": w_in, "w_uq": w_uq, "w_ukv": w_ukv, "w_o_mla": w_o_mla, "w_o_fox": w_o_fox,
             "w_out": w_out, "w_up": w_up, "conv_w": conv_w, "w_down": w_down}
    big_m = {"w_in": m_w_in, "w_uq": m_w_uq, "w_ukv": m_w_ukv, "w_o_mla": m_w_o_mla, "w_o_fox": m_w_o_fox,
             "w_out": m_w_out, "w_up": m_w_up, "conv_w": m_conv_w, "w_down": m_w_down}
    big_v = {"w_in": v_w_in, "w_uq": v_w_uq, "w_ukv": v_w_ukv, "w_o_mla": v_w_o_mla, "w_o_fox": v_w_o_fox,
             "w_out": v_w_out, "w_up": v_w_up, "conv_w": v_conv_w, "w_down": v_w_down}

    wg = _all_gather(_pack_gather({k: a[0] for k, a in big_w.items()}), "gather_weights")
    st = _unpack_gather(wg)
    w_in_p = _w_in_padded(_cols_to_full(st["w_in"]))
    uq = st["w_uq"]
    w_uq_p = jnp.pad(uq, ((0, 0), (0, 0), (0, HEAD_PAD - 96))).transpose(1, 0, 2).reshape(MLA_Q_RANK, 1024)
    ukv = st["w_ukv"]
    zeros64 = jnp.zeros((N_HEADS, MLA_KV_RANK, 64), BF16)
    w_uk_p = jnp.concatenate([ukv[:, :, :64], zeros64], axis=2).transpose(1, 0, 2).reshape(MLA_KV_RANK, 1024)
    w_uv_p = jnp.concatenate([ukv[:, :, 64:], zeros64], axis=2).transpose(1, 0, 2).reshape(MLA_KV_RANK, 1024)
    place = np.zeros((HEAD_PAD, N_HEADS, HEAD_PAD), np.float32)
    for j in range(MLA_ROPE):
        place[j, :, MLA_NOPE + j] = 1.0
    place = jnp.asarray(place.reshape(HEAD_PAD, 1024), BF16)
    w_kv_comb = jnp.concatenate([
        jnp.concatenate([w_uk_p, w_uv_p], axis=1),
        jnp.concatenate([place, jnp.zeros((HEAD_PAD, 1024), BF16)], axis=1)], axis=0)
    pad_o = lambda full: jnp.pad(full.reshape(N_HEADS, 64, 1024), ((0, 0), (0, 64), (0, 0))).reshape(1024, 1024)
    w_o_mla_p = pad_o(_cols_to_full(st["w_o_mla"]))
    w_o_fox_p = pad_o(_cols_to_full(st["w_o_fox"]))
    w_out_f = st["w_out"].reshape(1024, 1024)
    w_up_f = _cols_to_full(st["w_up"])
    conv_w_f = _cols_to_full(st["conv_w"])
    w_down_f = st["w_down"].reshape(D_FF, 1024)

    c_all = _all_gather(c, "gather_c").reshape(N_DEV, D_MODEL)
    b_ada_mine = lax.dynamic_slice(b_ada, (0, me * 768), (1, 768))
    ada_cols, act_col = _ada_fwd(c_all.reshape(N_DEV, D_MODEL, 1), w_ada[0], b_ada_mine, "ada_fwd")
    ada_all = _all_gather(ada_cols, "gather_ada")
    ada = lax.dynamic_slice(ada_all, (0, me, 0), (N_DEV, 1, 768)).reshape(1, N_ADA * D_MODEL)
    sh_m, sc_m, g_m, sh_f, sc_f, g_f = [ada[:, i * D_MODEL:(i + 1) * D_MODEL] for i in range(N_ADA)]

    inv_freq = ROPE_THETA ** (-jnp.arange(0, MLA_ROPE, 2, dtype=F32) / MLA_ROPE)
    ang = positions[0].astype(F32)[:, None] * inv_freq
    cos, sin = jnp.cos(ang), jnp.sin(ang)
    rope_c = jnp.concatenate([jnp.ones((s, 64), F32), cos, cos, jnp.zeros((s, 32), F32)], axis=1)
    rope_s = jnp.concatenate([jnp.zeros((s, 64), F32), -sin, sin, jnp.zeros((s, 32), F32)], axis=1)

    zero_d = jnp.zeros((1, D_MODEL), F32)

    h1 = _rms_mod(x, norm_mix_g, sc_m, sh_m, "norm_mix")
    proj = _mm(h1, w_in_p, "nn", F32, "proj_in", tn=640)
    cq = proj[:, P_CQ:P_CQ + 384]
    ckv = proj[:, P_CKV:P_CKV + 256]
    qn = _rms_mod(cq, q_norm_g, jnp.zeros((1, 384), F32), jnp.zeros((1, 384), F32), "q_norm")
    kvn = _rms_mod(ckv, kv_norm_g, jnp.zeros((1, 256), F32), jnp.zeros((1, 256), F32), "kv_norm")
    kv_in = jnp.concatenate([kvn, proj[:, P_KR:P_KR + 128].astype(BF16)], axis=1)
    q_pre = _mm(qn, w_uq_p, "nn", F32, "q_up")
    kv_pre = _mm(kv_in, w_kv_comb, "nn", F32, "kv_up")
    (q_att,) = _rope(q_pre, rope_c, rope_s, "rope_q", N_HEADS, (BF16,))
    k_att, v_att = _rope(kv_pre, rope_c, rope_s, "rope_kv", N_HEADS, (BF16, BF16))
    o_mla, lse_mla = _attn_fwd(q_att, k_att, v_att, None, None, MLA_SCALE, "mla_fwd")

    fq = _pad_heads(proj[:, P_FQ:P_FQ + 512], 64).astype(BF16)
    fk = _pad_heads(proj[:, P_FK:P_FK + 512], 64).astype(BF16)
    fv = _pad_heads(proj[:, P_FV:P_FV + 512], 64).astype(BF16)
    z = proj[:, P_FL:P_FL + 8].T.reshape(N_HEADS, s // SEQ_LANES, SEQ_LANES)
    bias_f = jnp.broadcast_to(b_forget.reshape(N_HEADS, 1, 1), (N_HEADS, 1, SEQ_LANES))
    f_cum = _forget_fwd(z, bias_f, "forget_fwd")
    f_col = f_cum.reshape(N_HEADS, s, 1)
    f_row = f_cum.reshape(N_HEADS, nblk, 1, ATT_T)
    o_fox, lse_fox = _attn_fwd(fq, fk, fv, f_col, f_row, FOX_SCALE, "fox_fwd")

    o_mla_b, o_fox_b = o_mla.astype(BF16), o_fox.astype(BF16)
    pm = _mm(o_mla_b, w_o_mla_p, "nn", F32, "o_mla_proj")
    pf = _mm(o_fox_b, w_o_fox_p, "nn", F32, "o_fox_proj")
    y = _gate_fwd(pm, pf, proj, "gate_fwd")
    x2, mix = _mm(y, w_out_f, "nn", F32, "out_proj", res=x, gvec=g_m)

    h2 = _rms_mod(x2, norm_ffn_g, sc_f, sh_f, "norm_ffn")
    u = _mm(h2, w_up_f, "nn", F32, "ffn_up")
    a = _convglu_fwd(u, conv_w_f, conv_b, "convglu_fwd")
    x3, ffn = _mm(a, w_down_f, "nn", F32, "ffn_down", res=x2, gvec=g_f, tk=2816)

    dx3, sums_final = _final_loss(x3, target, norm_final_g.reshape(1, D_MODEL), "final_loss")
    dffn, sums_gf = _scale_bwd(dx3, ffn, g_f, "ffn_scale_bwd")
    da = _mm(dffn, w_down_f, "nt", F32, "ffn_down_dx")
    g_w_down = _mm(a.T, dffn, "nn", F32, "ffn_down_dw", tm=256)
    dgate, dval, s_gate, s_val = _convglu_bwd(da, u, conv_w_f, conv_b, "convglu_bwd")
    du = jnp.concatenate([_conv_transpose(dgate, conv_w_f[:, :D_FF], "conv_t_gate"),
                          _conv_transpose(dval, conv_w_f[:, D_FF:], "conv_t_val")], axis=1)
    dh2 = _mm(du, w_up_f, "nt", F32, "ffn_up_dx", tk=512)
    g_w_up = _mm(h2.T, du, "nn", F32, "ffn_up_dw")
    dx2, sums_ffn = _rms_mod_bwd(dh2, x2, norm_ffn_g, sc_f, dx3, "norm_ffn_bwd")

    dmix, sums_gm = _scale_bwd(dx2, mix, g_m, "mix_scale_bwd")
    dy = _mm(dmix, w_out_f, "nt", F32, "out_proj_dx")
    g_w_out = _mm(y.T, dmix, "nn", F32, "out_proj_dw")
    dpm, dpf, dgm, dgf = _gate_bwd(dy, pm, pf, proj, "gate_bwd")
    do_mla = _mm(dpm, w_o_mla_p, "nt", F32, "o_mla_dx")
    do_fox = _mm(dpf, w_o_fox_p, "nt", F32, "o_fox_dx")
    g_w_o_mla_p = _mm(o_mla_b.T, dpm, "nn", F32, "o_mla_dw")
    g_w_o_fox_p = _mm(o_fox_b.T, dpf, "nn", F32, "o_fox_dw")

    rows = lambda col: col.reshape(N_HEADS, nblk, 1, ATT_T)
    delta_mla = _attn_delta(o_mla, do_mla, "mla_delta")
    do_mla_b = do_mla.astype(BF16)
    dq_rot = _attn_bwd_dq(q_att, k_att, v_att, do_mla_b, lse_mla, delta_mla, None, None, MLA_SCALE, "mla_bwd_dq")
    dk_rot, dv_mla = _attn_bwd_dkv(q_att, k_att, v_att, do_mla_b, rows(lse_mla), rows(delta_mla), None, None,
                                   MLA_SCALE, "mla_bwd_dkv")
    (dq_pre,) = _rope(dq_rot, rope_c, -rope_s, "rope_q_bwd", N_HEADS, (BF16,))
    dkv_pre = _rope_bwd_kv(dk_rot, dv_mla, rope_c, -rope_s, "rope_kv_bwd")
    dqn = _mm(dq_pre, w_uq_p, "nt", F32, "q_up_dx")
    g_w_uq_p = _mm(qn.T, dq_pre, "nn", F32, "q_up_dw")
    dkv_in = _mm(dkv_pre, w_kv_comb, "nt", F32, "kv_up_dx")
    g_w_kv_comb = _mm(kv_in.T, dkv_pre, "nn", F32, "kv_up_dw")
    dcq, sums_q = _rms_mod_bwd(dqn, cq, q_norm_g, jnp.zeros((1, 384), F32), None, "q_norm_bwd")
    dckv, sums_kv = _rms_mod_bwd(dkv_in[:, :256], ckv, kv_norm_g, jnp.zeros((1, 256), F32), None, "kv_norm_bwd")
    delta_fox = _attn_delta(o_fox, do_fox, "fox_delta")
    do_fox_b = do_fox.astype(BF16)
    dfq, dfr = _attn_bwd_dq(fq, fk, fv, do_fox_b, lse_fox, delta_fox, f_col, f_row, FOX_SCALE, "fox_bwd_dq")
    dfk, dfv, dfc = _attn_bwd_dkv(fq, fk, fv, do_fox_b, rows(lse_fox), rows(delta_fox), f_col, f_row,
                                  FOX_SCALE, "fox_bwd_dkv")
    df_rows = dfr.reshape(N_HEADS, s // SEQ_LANES, SEQ_LANES)
    df_cols = dfc.reshape(N_HEADS, s // SEQ_LANES, SEQ_LANES)
    dz, db_f = _forget_bwd(df_rows, df_cols, z, bias_f, "forget_bwd")
    dfl = jnp.pad(dz.reshape(N_HEADS, s).T, ((0, 0), (0, 128 - N_HEADS)))

    dproj = jnp.concatenate([
        dgm, dgf, _unpad_heads(dfq, 64).astype(BF16), _unpad_heads(dfk, 64).astype(BF16),
        _unpad_heads(dfv, 64).astype(BF16), dcq.astype(BF16), dckv.astype(BF16),
        dkv_in[:, 256:384].astype(BF16), dfl.astype(BF16)], axis=1)
    dh1 = _mm(dproj, w_in_p, "nt", F32, "proj_in_dx", tk=896)
    g_w_in_p = _mm(h1.T, dproj, "nn", F32, "proj_in_dw", tn=640)
    grad_x, sums_mix = _rms_mod_bwd(dh1, x, norm_mix_g, sc_m, dx2, "norm_mix_bwd")

    g_w_in = _w_in_unpadded(g_w_in_p)
    g_uq = g_w_uq_p.reshape(MLA_Q_RANK, N_HEADS, HEAD_PAD)[:, :, :96].transpose(1, 0, 2)
    g_uk = g_w_kv_comb[:256, :1024].reshape(256, N_HEADS, HEAD_PAD)[:, :, :64]
    g_uv = g_w_kv_comb[:256, 1024:].reshape(256, N_HEADS, HEAD_PAD)[:, :, :64]
    g_ukv = jnp.concatenate([g_uk, g_uv], axis=2).transpose(1, 0, 2)
    unpad_o = lambda g: g.reshape(N_HEADS, HEAD_PAD, 1024)[:, :64].reshape(512, 1024)
    g_conv_w = jnp.concatenate([s_gate[0:3], s_val[0:3]], axis=1)
    g_blocks = {
        "w_in": _full_to_cols(g_w_in, 533), "w_uq": g_uq, "w_ukv": g_ukv,
        "w_o_mla": _full_to_cols(unpad_o(g_w_o_mla_p), 128), "w_o_fox": _full_to_cols(unpad_o(g_w_o_fox_p), 128),
        "w_out": g_w_out.reshape(N_DEV, 128, 1024), "w_up": _full_to_cols(g_w_up, 704),
        "conv_w": _full_to_cols(g_conv_w, 704), "w_down": g_w_down.reshape(N_DEV, 352, 1024)}
    g_recv = _all_to_all(_pack_f32(g_blocks), "scatter_grads")
    pk = lambda dct: _pack_f32({k: a[0] for k, a in dct.items()})
    g_big, d_big, nm_big, nv_big = [_unpack_f32(t) for t in
                                    _adamw(g_recv, pk(big_w), pk(big_m), pk(big_v), "adamw_big")]

    dada = jnp.concatenate([sums_mix[0:1], sums_mix[1:2], sums_gm[0:1], sums_ffn[0:1], sums_ffn[1:2], sums_gf[0:1]],
                           axis=1)
    small_part = _pack_small({
        "b_ada": dada, "norm_mix_g": sums_mix[2:3], "q_norm_g": sums_q[2:3], "kv_norm_g": sums_kv[2:3],
        "b_forget": db_f[:, 0, 0], "norm_ffn_g": sums_ffn[2:3],
        "conv_b": jnp.concatenate([s_gate[3:4], s_val[3:4]], axis=1), "norm_final_g": sums_final[0:1],
        "loss": sums_final[1:2, 0:1]})
    small_all = _all_gather(small_part, "gather_small")
    zero1 = jnp.zeros((1,), F32)
    small_w = {"b_ada": b_ada, "norm_mix_g": norm_mix_g, "q_norm_g": q_norm_g, "kv_norm_g": kv_norm_g,
               "b_forget": b_forget, "norm_ffn_g": norm_ffn_g, "conv_b": conv_b, "norm_final_g": norm_final_g,
               "loss": zero1}
    small_m = {"b_ada": m_b_ada, "norm_mix_g": m_norm_mix_g, "q_norm_g": m_q_norm_g, "kv_norm_g": m_kv_norm_g,
               "b_forget": m_b_forget, "norm_ffn_g": m_norm_ffn_g, "conv_b": m_conv_b,
               "norm_final_g": m_norm_final_g, "loss": zero1}
    small_v = {"b_ada": v_b_ada, "norm_mix_g": v_norm_mix_g, "q_norm_g": v_q_norm_g, "kv_norm_g": v_kv_norm_g,
               "b_forget": v_b_forget, "norm_ffn_g": v_norm_ffn_g, "conv_b": v_conv_b,
               "norm_final_g": v_norm_final_g, "loss": zero1}
    g_sm, d_sm, nm_sm, nv_sm = _adamw(small_all, _pack_small(small_w), _pack_small(small_m), _pack_small(small_v),
                                      "adamw_small")
    loss = g_sm[0, SMALL_OFF["loss"]]

    dada_all = small_all[:, 0, SMALL_OFF["b_ada"]:SMALL_OFF["b_ada"] + N_ADA * D_MODEL]
    dada_mine = lax.dynamic_slice(dada_all, (0, me * 768), (N_DEV, 768))
    g_ada_local = _ada_bwd(act_col, dada_mine, "ada_bwd")
    g_ada, d_ada, nm_ada, nv_ada = _adamw(g_ada_local[None], w_ada[0], m_w_ada[0], v_w_ada[0], "adamw_ada")

    def small_out(t, nme, shape):
        real = dict((n_, r_) for n_, r_, _ in SMALL)[nme]
        o = SMALL_OFF[nme]
        return t[0, o:o + real].reshape(shape)

    order = ["w_ada", "b_ada", "norm_mix_g", "w_in", "q_norm_g", "w_uq", "kv_norm_g", "w_ukv", "b_forget",
             "w_o_mla", "w_o_fox", "w_out", "norm_ffn_g", "w_up", "conv_w", "conv_b", "w_down", "norm_final_g"]
    small_shapes = {"b_ada": (1, 6144), "norm_mix_g": (1, 1024), "q_norm_g": (1, 384), "kv_norm_g": (1, 256),
                    "b_forget": (1, 8), "norm_ffn_g": (1, 1024), "conv_b": (1, 5632), "norm_final_g": (1024,)}

    def family(big, small, ada_t):
        out = []
        for nme in order:
            if nme == "w_ada":
                out.append(ada_t[None])
            elif nme in small_shapes:
                out.append(small_out(small, nme, small_shapes[nme]))
            else:
                out.append(big[nme])
        return out

    return (loss, grad_x[None], *family(g_big, g_sm, g_ada), *family(d_big, d_sm, d_ada),
            *family(nm_big, nm_sm, nm_ada), *family(nv_big, nv_sm, nv_ada))
```

```python
import math

import numpy as np
import jax
import jax.numpy as jnp
from jax import lax
from jax.experimental import pallas as pl
from jax.experimental.pallas import tpu as pltpu

F32 = jnp.float32
BF16 = jnp.bfloat16

N_DEV = 8
D_MODEL = 1024
N_HEADS = 8
HEAD_PAD = 128
MLA_Q_RANK = 384
MLA_KV_RANK = 256
MLA_NOPE = 64
MLA_ROPE = 32
MLA_V = 64
FOX_DIM = 64
D_FF = 2816
N_ADA = 6
EPS = 1e-6
ROPE_THETA = 10000.0
MLA_SCALE = 1.0 / math.sqrt(MLA_NOPE + MLA_ROPE)
FOX_SCALE = 1.0 / math.sqrt(FOX_DIM)
IN_SPLITS = (384, 256, 32, 512, 512, 512, 8, 1024, 1024)
D_IN = sum(IN_SPLITS)
IN_OFF = tuple(int(v) for v in np.cumsum((0,) + IN_SPLITS))
P_GM, P_GF, P_FQ, P_FK, P_FV, P_CQ, P_CKV, P_KR, P_FL, D_IN_P = 0, 1024, 2048, 2560, 3072, 3584, 3968, 4224, 4352, 4480

ADAM_LR, ADAM_B1, ADAM_B2, ADAM_EPS, ADAM_WD, ADAM_STEP = 0.001, 0.9, 0.999, 1e-08, 0.01, 10

VMEM_LIMIT_BYTES = 56 * 1024 * 1024
NEG_BIG = -1e30
ATT_T = 512
LOG2E = 1.4426950408889634
SUM_LANE = 64
ROW_T = 256
SEQ_LANES = 128


def _params(sem):
    return pltpu.CompilerParams(dimension_semantics=sem, vmem_limit_bytes=VMEM_LIMIT_BYTES)


def _tile(n, target, step=128):
    if n <= target:
        return n
    t = (target // step) * step
    while t >= step:
        if n % t == 0:
            return t
        t -= step
    return n


def _vec_spec(w, nargs):
    if nargs == 1:
        return pl.BlockSpec((1, w), lambda i: (0, 0))
    return pl.BlockSpec((1, w), lambda i, j: (0, 0))


def _all_gather(x, name):
    def body(x_ref, out_ref, send_sems, recv_sems, local_sem):
        x_, y_, c_ = lax.axis_index("x"), lax.axis_index("y"), lax.axis_index("c")
        me, sibling = (x_, y_, c_), (x_, y_, 1 - c_)
        chips = [(1 - x_, y_), (x_, 1 - y_), (1 - x_, 1 - y_)]

        def slot(px, py, pc):
            return out_ref.at[4 * px + 2 * py + pc]

        def copy(k, block, to, src=None):
            return pltpu.make_async_remote_copy(
                src_ref=slot(*block) if src is None else src, dst_ref=slot(*block),
                send_sem=send_sems.at[k], recv_sem=recv_sems.at[k],
                device_id=to, device_id_type=pl.DeviceIdType.MESH)

        mine = pltpu.make_async_copy(x_ref, slot(*me), local_sem)
        mine.start()
        first = [copy(0, me, sibling, src=x_ref)]
        first += [copy(1 + j, me, (*chip, c_), src=x_ref) for j, chip in enumerate(chips)]
        for cp in first:
            cp.start()
        passed = [copy(4 + j, (*chip, c_), sibling) for j, chip in enumerate(chips)]
        for j, chip in enumerate(chips):
            copy(1 + j, (*chip, c_), me).wait_recv()
            passed[j].start()
        copy(0, sibling, me).wait_recv()
        for j, chip in enumerate(chips):
            copy(4 + j, (*chip, 1 - c_), me).wait_recv()
        for cp in first + passed:
            cp.wait_send()
        mine.wait()

    return pl.pallas_call(
        body, name=name,
        out_shape=jax.ShapeDtypeStruct((N_DEV,) + x.shape, x.dtype),
        in_specs=[pl.BlockSpec(memory_space=pl.ANY)],
        out_specs=pl.BlockSpec(memory_space=pl.ANY),
        scratch_shapes=[pltpu.SemaphoreType.DMA((7,)), pltpu.SemaphoreType.DMA((7,)), pltpu.SemaphoreType.DMA(())],
    )(x)


def _all_to_all(g, name):
    def body(g_ref, out_ref, send_sems, recv_sems, local_sem):
        x_, y_, c_ = lax.axis_index("x"), lax.axis_index("y"), lax.axis_index("c")
        me = 4 * x_ + 2 * y_ + c_

        def peer(k):
            return (x_ ^ ((k >> 2) & 1), y_ ^ ((k >> 1) & 1), c_ ^ (k & 1))

        def copy(k):
            px, py, pc = peer(k)
            return pltpu.make_async_remote_copy(
                src_ref=g_ref.at[4 * px + 2 * py + pc], dst_ref=out_ref.at[me],
                send_sem=send_sems.at[k - 1], recv_sem=recv_sems.at[k - 1],
                device_id=(px, py, pc), device_id_type=pl.DeviceIdType.MESH)

        def landed(k):
            px, py, pc = peer(k)
            return pltpu.make_async_remote_copy(
                src_ref=g_ref.at[me], dst_ref=out_ref.at[4 * px + 2 * py + pc],
                send_sem=send_sems.at[k - 1], recv_sem=recv_sems.at[k - 1],
                device_id=(px, py, pc), device_id_type=pl.DeviceIdType.MESH)

        mine = pltpu.make_async_copy(g_ref.at[me], out_ref.at[me], local_sem)
        mine.start()
        sends = [copy(k) for k in range(1, N_DEV)]
        for cp in sends:
            cp.start()
        for k in range(1, N_DEV):
            landed(k).wait_recv()
        for cp in sends:
            cp.wait_send()
        mine.wait()

    return pl.pallas_call(
        body, name=name,
        out_shape=jax.ShapeDtypeStruct(g.shape, g.dtype),
        in_specs=[pl.BlockSpec(memory_space=pl.ANY)],
        out_specs=pl.BlockSpec(memory_space=pl.ANY),
        scratch_shapes=[pltpu.SemaphoreType.DMA((7,)), pltpu.SemaphoreType.DMA((7,)), pltpu.SemaphoreType.DMA(())],
    )(g)


def _mm(a, b, mode, out_dtype, name, res=None, gvec=None, tm=1024, tn=512, tk=1024):
    m, k = a.shape
    n = b.shape[1] if mode == "nn" else b.shape[0]
    tm, tn, tk = _tile(m, tm), _tile(n, tn), _tile(k, tk)
    nk = k // tk
    dims = (((1,), (0,)), ((), ())) if mode == "nn" else (((1,), (1,)), ((), ()))
    fused = res is not None

    def body(*refs):
        if fused:
            a_ref, b_ref, res_ref, g_ref, o_ref, raw_ref, acc_ref = refs
        else:
            a_ref, b_ref, o_ref, acc_ref = refs
        kk = pl.program_id(2)
        part = lax.dot_general(a_ref[...], b_ref[...], dims, preferred_element_type=F32)

        @pl.when(kk == 0)
        def _():
            acc_ref[...] = part

        @pl.when(kk > 0)
        def _():
            acc_ref[...] += part

        @pl.when(kk == nk - 1)
        def _():
            acc = acc_ref[...]
            if fused:
                raw_ref[...] = acc
                o_ref[...] = (res_ref[...] + g_ref[...] * acc).astype(o_ref.dtype)
            else:
                o_ref[...] = acc.astype(o_ref.dtype)

    a_spec = pl.BlockSpec((tm, tk), lambda i, j, kk: (i, kk))
    if mode == "nn":
        b_spec = pl.BlockSpec((tk, tn), lambda i, j, kk: (kk, j))
    else:
        b_spec = pl.BlockSpec((tn, tk), lambda i, j, kk: (j, kk))
    o_spec = pl.BlockSpec((tm, tn), lambda i, j, kk: (i, j))
    in_specs, args = [a_spec, b_spec], [a, b]
    out_specs, out_shape = o_spec, jax.ShapeDtypeStruct((m, n), out_dtype)
    if fused:
        in_specs += [o_spec, pl.BlockSpec((1, tn), lambda i, j, kk: (0, j))]
        args += [res, gvec]
        out_specs = (o_spec, o_spec)
        out_shape = (out_shape, jax.ShapeDtypeStruct((m, n), F32))
    return pl.pallas_call(
        body, name=name, grid=(m // tm, n // tn, nk),
        in_specs=in_specs, out_specs=out_specs, out_shape=out_shape,
        scratch_shapes=[pltpu.VMEM((tm, tn), F32)],
        compiler_params=_params(("parallel", "parallel", "arbitrary")),
    )(*args)


def _rms_mod(x, g, sc, sh, name):
    s, w = x.shape
    tm = _tile(s, ROW_T)

    def body(x_ref, g_ref, sc_ref, sh_ref, o_ref):
        xv = x_ref[...]
        r = lax.rsqrt(jnp.mean(xv * xv, axis=-1, keepdims=True) + EPS)
        o_ref[...] = ((xv * r * g_ref[...]) * (1.0 + sc_ref[...]) + sh_ref[...]).astype(o_ref.dtype)

    row = pl.BlockSpec((tm, w), lambda i: (i, 0))
    return pl.pallas_call(
        body, name=name, grid=(s // tm,),
        in_specs=[row, _vec_spec(w, 1), _vec_spec(w, 1), _vec_spec(w, 1)],
        out_specs=row, out_shape=jax.ShapeDtypeStruct((s, w), BF16),
        compiler_params=_params(("parallel",)),
    )(x, g, sc, sh)


def _rms_mod_bwd(dh, x, g, sc, dres, name):
    s, w = x.shape
    tm = _tile(s, ROW_T)
    has_res = dres is not None

    def body(*refs):
        if has_res:
            dh_ref, x_ref, g_ref, sc_ref, dres_ref, dx_ref, sums_ref = refs
        else:
            dh_ref, x_ref, g_ref, sc_ref, dx_ref, sums_ref = refs
        xv, dhv, gv = x_ref[...], dh_ref[...], g_ref[...]
        r = lax.rsqrt(jnp.mean(xv * xv, axis=-1, keepdims=True) + EPS)
        xhat = xv * r
        dxn = dhv * (1.0 + sc_ref[...])
        dxhat = dxn * gv
        dx = r * (dxhat - xhat * jnp.mean(dxhat * xhat, axis=-1, keepdims=True))
        if has_res:
            dx = dx + dres_ref[...]
        dx_ref[...] = dx

        @pl.when(pl.program_id(0) == 0)
        def _():
            sums_ref[...] = jnp.zeros_like(sums_ref)

        sums_ref[0:1, :] += jnp.sum(dhv, axis=0, keepdims=True)
        sums_ref[1:2, :] += jnp.sum(dhv * (xhat * gv), axis=0, keepdims=True)
        sums_ref[2:3, :] += jnp.sum(dxn * xhat, axis=0, keepdims=True)

    row = pl.BlockSpec((tm, w), lambda i: (i, 0))
    in_specs = [row, row, _vec_spec(w, 1), _vec_spec(w, 1)] + ([row] if has_res else [])
    args = [dh, x, g, sc] + ([dres] if has_res else [])
    return pl.pallas_call(
        body, name=name, grid=(s // tm,),
        in_specs=in_specs,
        out_specs=(row, pl.BlockSpec((8, w), lambda i: (0, 0))),
        out_shape=(jax.ShapeDtypeStruct((s, w), F32), jax.ShapeDtypeStruct((8, w), F32)),
        compiler_params=_params(("arbitrary",)),
    )(*args)


def _scale_bwd(dx, val, gvec, name):
    s, w = dx.shape
    tm = _tile(s, ROW_T)

    def body(dx_ref, val_ref, g_ref, d_ref, sums_ref):
        dxv = dx_ref[...]
        d_ref[...] = (dxv * g_ref[...]).astype(d_ref.dtype)

        @pl.when(pl.program_id(0) == 0)
        def _():
            sums_ref[...] = jnp.zeros_like(sums_ref)

        sums_ref[0:1, :] += jnp.sum(dxv * val_ref[...], axis=0, keepdims=True)

    row = pl.BlockSpec((tm, w), lambda i: (i, 0))
    return pl.pallas_call(
        body, name=name, grid=(s // tm,),
        in_specs=[row, row, _vec_spec(w, 1)],
        out_specs=(row, pl.BlockSpec((8, w), lambda i: (0, 0))),
        out_shape=(jax.ShapeDtypeStruct((s, w), BF16), jax.ShapeDtypeStruct((8, w), F32)),
        compiler_params=_params(("arbitrary",)),
    )(dx, val, gvec)


def _final_loss(x3, target, g, name):
    s, w = x3.shape
    tm = _tile(s, ROW_T)

    def body(x_ref, t_ref, g_ref, dx_ref, sums_ref):
        xv, gv = x_ref[...], g_ref[...]
        r = lax.rsqrt(jnp.mean(xv * xv, axis=-1, keepdims=True) + EPS)
        xhat = xv * r
        err = xhat * gv - t_ref[...]
        dy = err * (1.0 / w)
        dxhat = dy * gv
        dx_ref[...] = r * (dxhat - xhat * jnp.mean(dxhat * xhat, axis=-1, keepdims=True))

        @pl.when(pl.program_id(0) == 0)
        def _():
            sums_ref[...] = jnp.zeros_like(sums_ref)

        sums_ref[0:1, :] += jnp.sum(dy * xhat, axis=0, keepdims=True)
        sums_ref[1:2, :] += jnp.zeros((1, w), F32) + (0.5 / w) * jnp.sum(err * err)

    row = pl.BlockSpec((tm, w), lambda i: (i, 0))
    return pl.pallas_call(
        body, name=name, grid=(s // tm,),
        in_specs=[row, row, _vec_spec(w, 1)],
        out_specs=(row, pl.BlockSpec((8, w), lambda i: (0, 0))),
        out_shape=(jax.ShapeDtypeStruct((s, w), F32), jax.ShapeDtypeStruct((8, w), F32)),
        compiler_params=_params(("arbitrary",)),
    )(x3, target, g)


def _rope_block(seg, cmul, smul):
    lane = lax.broadcasted_iota(jnp.int32, seg.shape, 1)
    swapped = jnp.where(lane < MLA_NOPE + MLA_ROPE // 2,
                        pltpu.roll(seg, HEAD_PAD - MLA_ROPE // 2, 1), pltpu.roll(seg, MLA_ROPE // 2, 1))
    return seg * cmul + swapped * smul


def _rope(t, cmul, smul, name, n_rot, out_dtypes):
    s, w = t.shape
    tm = _tile(s, ROW_T)
    n_out = len(out_dtypes)
    wo = w // n_out

    def body(t_ref, c_ref, s_ref, *o_refs):
        cv, sv = c_ref[...], s_ref[...]
        one = (lax.broadcasted_iota(jnp.int32, (tm, HEAD_PAD), 1) == SUM_LANE).astype(F32)
        for hb in range(w // HEAD_PAD):
            seg = t_ref[:, hb * HEAD_PAD:(hb + 1) * HEAD_PAD]
            if hb < n_rot:
                seg = _rope_block(seg, cv, sv)
            else:
                seg = seg + one
            o_ref = o_refs[(hb * HEAD_PAD) // wo]
            col = (hb * HEAD_PAD) % wo
            o_ref[:, col:col + HEAD_PAD] = seg.astype(o_ref.dtype)

    row = pl.BlockSpec((tm, w), lambda i: (i, 0))
    tab = pl.BlockSpec((tm, HEAD_PAD), lambda i: (i, 0))
    orow = pl.BlockSpec((tm, wo), lambda i: (i, 0))
    outs = pl.pallas_call(
        body, name=name, grid=(s // tm,),
        in_specs=[row, tab, tab],
        out_specs=tuple(orow for _ in out_dtypes),
        out_shape=tuple(jax.ShapeDtypeStruct((s, wo), dt) for dt in out_dtypes),
        compiler_params=_params(("parallel",)),
    )(t, cmul, smul)
    return outs


def _rope_bwd_kv(dk, dv, cmul, smul, name):
    s, w = dk.shape
    tm = _tile(s, ROW_T)

    def body(dk_ref, dv_ref, c_ref, s_ref, o_ref):
        cv, sv = c_ref[...], s_ref[...]
        for hb in range(N_HEADS):
            lo, hi = hb * HEAD_PAD, (hb + 1) * HEAD_PAD
            o_ref[:, lo:hi] = _rope_block(dk_ref[:, lo:hi], cv, sv).astype(o_ref.dtype)
        o_ref[:, w:2 * w] = dv_ref[...].astype(o_ref.dtype)

    row = pl.BlockSpec((tm, w), lambda i: (i, 0))
    tab = pl.BlockSpec((tm, HEAD_PAD), lambda i: (i, 0))
    return pl.pallas_call(
        body, name=name, grid=(s // tm,),
        in_specs=[row, row, tab, tab],
        out_specs=pl.BlockSpec((tm, 2 * w), lambda i: (i, 0)),
        out_shape=jax.ShapeDtypeStruct((s, 2 * w), BF16),
        compiler_params=_params(("parallel",)),
    )(dk, dv, cmul, smul)


def _lanes(col, width):
    if col.shape[1] == 1:
        col = jnp.broadcast_to(col, (col.shape[0], HEAD_PAD))
    return jnp.tile(col, (1, width // HEAD_PAD))


def _fold_lanes(a):
    out = a[:, 0:HEAD_PAD]
    for g in range(1, a.shape[1] // HEAD_PAD):
        out = out + a[:, g * HEAD_PAD:(g + 1) * HEAD_PAD]
    return out


def _causal(t, rows_are_queries):
    row = lax.broadcasted_iota(jnp.int32, (t, t), 0)
    col = lax.broadcasted_iota(jnp.int32, (t, t), 1)
    return row >= col if rows_are_queries else col >= row


def _attn_fwd(q, k, v, fcol, frow, name):
    s = q.shape[0]
    t = ATT_T
    nq = s // t
    use_f = fcol is not None

    def body(*refs):
        if use_f:
            q_ref, k_ref, v_ref, fc_ref, fr_ref, o_ref, ob_ref, lse_ref, m_s, acc_s = refs
            fc_b = jnp.broadcast_to(fc_ref[0], (t, HEAD_PAD))
        else:
            q_ref, k_ref, v_ref, o_ref, ob_ref, lse_ref, m_s, acc_s = refs
        qi = pl.program_id(1)
        qv = q_ref[...]
        m_s[...] = jnp.full(m_s.shape, NEG_BIG, F32)
        acc_s[...] = jnp.zeros(acc_s.shape, F32)

        def step(j, masked):
            off = pl.multiple_of(j * t, t)
            kv = k_ref[pl.ds(off, t), :]
            vv = v_ref[pl.ds(off, t), :]
            sc = lax.dot_general(qv, kv, (((1,), (1,)), ((), ())), preferred_element_type=F32)
            if use_f:
                sc = sc + (_lanes(fc_b, t) - fr_ref[0, j])
            if masked:
                sc = jnp.where(_causal(t, True), sc, NEG_BIG)
            m_prev = m_s[...]
            m_new = jnp.maximum(m_prev, jnp.max(sc, axis=-1, keepdims=True))
            p = jnp.exp2(sc - _lanes(m_new, t))
            acc_s[...] = jnp.exp2(m_prev - m_new) * acc_s[...] + jnp.dot(p.astype(BF16), vv,
                                                                         preferred_element_type=F32)
            m_s[...] = m_new

        def loop_body(j, carry):
            step(j, False)
            return carry

        lax.fori_loop(0, qi, loop_body, 0)
        step(qi, True)
        acc = acc_s[...]
        lane = lax.broadcasted_iota(jnp.int32, acc.shape, 1)
        denom = jnp.sum(jnp.where(lane == SUM_LANE, acc, 0.0), axis=-1, keepdims=True)
        o = acc * (1.0 / denom)
        o_ref[...] = o
        ob_ref[...] = o.astype(BF16)
        lse_ref[0] = jnp.max(m_s[...], axis=-1, keepdims=True) + jnp.log(denom) * LOG2E

    qspec = pl.BlockSpec((t, HEAD_PAD), lambda h, i: (i, h))
    kspec = pl.BlockSpec((s, HEAD_PAD), lambda h, i: (0, h))
    colspec = pl.BlockSpec((1, t, 1), lambda h, i: (h, i, 0))
    in_specs, args = [qspec, kspec, kspec], [q, k, v]
    if use_f:
        in_specs += [colspec, pl.BlockSpec((1, nq, 1, t), lambda h, i: (h, 0, 0, 0))]
        args += [fcol, frow]
    return pl.pallas_call(
        body, name=name, grid=(N_HEADS, nq),
        in_specs=in_specs,
        out_specs=(qspec, qspec, colspec),
        out_shape=(jax.ShapeDtypeStruct((s, N_HEADS * HEAD_PAD), F32), jax.ShapeDtypeStruct((s, N_HEADS * HEAD_PAD), BF16),
                   jax.ShapeDtypeStruct((N_HEADS, s, 1), F32)),
        scratch_shapes=[pltpu.VMEM((t, HEAD_PAD), F32), pltpu.VMEM((t, HEAD_PAD), F32)],
        compiler_params=_params(("parallel", "arbitrary")),
    )(*args)


def _attn_delta(o, do, name):
    s, w = o.shape
    tm = _tile(s, ROW_T)

    def body(o_ref, do_ref, d_ref):
        for hb in range(N_HEADS):
            lo, hi = hb * HEAD_PAD, (hb + 1) * HEAD_PAD
            d_ref[hb] = jnp.sum(o_ref[:, lo:hi] * do_ref[:, lo:hi], axis=-1, keepdims=True)

    row = pl.BlockSpec((tm, w), lambda i: (i, 0))
    return pl.pallas_call(
        body, name=name, grid=(s // tm,),
        in_specs=[row, row],
        out_specs=pl.BlockSpec((N_HEADS, tm, 1), lambda i: (0, i, 0)),
        out_shape=jax.ShapeDtypeStruct((N_HEADS, s, 1), F32),
        compiler_params=_params(("parallel",)),
    )(o, do)


def _attn_bwd_dq(q, k, v, do, lse, delta, fcol, frow, scale, name):
    s = q.shape[0]
    t = ATT_T
    nq = s // t
    use_f = fcol is not None

    def body(*refs):
        if use_f:
            q_ref, k_ref, v_ref, do_ref, lse_ref, dl_ref, fc_ref, fr_ref, dq_ref, dr_ref, acc_s, dr_s = refs
            dr_s[...] = jnp.zeros(dr_s.shape, F32)
            fc_b = jnp.broadcast_to(fc_ref[0], (t, HEAD_PAD))
        else:
            q_ref, k_ref, v_ref, do_ref, lse_ref, dl_ref, dq_ref, acc_s = refs
        qi = pl.program_id(1)
        qv, dov = q_ref[...], do_ref[...]
        lse_b = jnp.broadcast_to(lse_ref[0], (t, HEAD_PAD))
        dl_b = jnp.broadcast_to(dl_ref[0], (t, HEAD_PAD))
        acc_s[...] = jnp.zeros(acc_s.shape, F32)

        def step(j, masked):
            off = pl.multiple_of(j * t, t)
            kv = k_ref[pl.ds(off, t), :]
            vv = v_ref[pl.ds(off, t), :]
            sc = lax.dot_general(qv, kv, (((1,), (1,)), ((), ())), preferred_element_type=F32)
            if use_f:
                sc = sc + (_lanes(fc_b, t) - fr_ref[0, j])
            if masked:
                sc = jnp.where(_causal(t, True), sc, NEG_BIG)
            p = jnp.exp2(sc - _lanes(lse_b, t))
            dp = lax.dot_general(dov, vv, (((1,), (1,)), ((), ())), preferred_element_type=F32)
            ds = p * (dp - _lanes(dl_b, t))
            acc_s[...] += jnp.dot(ds.astype(BF16), kv, preferred_element_type=F32)
            if use_f:
                dr_s[...] += _fold_lanes(ds)

        def loop_body(j, carry):
            step(j, False)
            return carry

        lax.fori_loop(0, qi, loop_body, 0)
        step(qi, True)
        dq_ref[...] = acc_s[...] * scale
        if use_f:
            dr_ref[0] = jnp.sum(dr_s[...], axis=-1, keepdims=True)

    qspec = pl.BlockSpec((t, HEAD_PAD), lambda h, i: (i, h))
    kspec = pl.BlockSpec((s, HEAD_PAD), lambda h, i: (0, h))
    colspec = pl.BlockSpec((1, t, 1), lambda h, i: (h, i, 0))
    in_specs, args = [qspec, kspec, kspec, qspec, colspec, colspec], [q, k, v, do, lse, delta]
    out_specs, out_shape = qspec, jax.ShapeDtypeStruct((s, N_HEADS * HEAD_PAD), F32)
    scratch = [pltpu.VMEM((t, HEAD_PAD), F32)]
    if use_f:
        in_specs += [colspec, pl.BlockSpec((1, nq, 1, t), lambda h, i: (h, 0, 0, 0))]
        args += [fcol, frow]
        out_specs, out_shape = (qspec, colspec), (out_shape, jax.ShapeDtypeStruct((N_HEADS, s, 1), F32))
        scratch += [pltpu.VMEM((t, HEAD_PAD), F32)]
    return pl.pallas_call(
        body, name=name, grid=(N_HEADS, nq),
        in_specs=in_specs, out_specs=out_specs, out_shape=out_shape,
        scratch_shapes=scratch,
        compiler_params=_params(("parallel", "arbitrary")),
    )(*args)


def _attn_bwd_dkv(q, k, v, do, lse_row, delta_row, fcol, frow, scale, name):
    s = q.shape[0]
    t = ATT_T
    nq = s // t
    use_f = fcol is not None

    def body(*refs):
        if use_f:
            q_ref, k_ref, v_ref, do_ref, lse_ref, dl_ref, fc_ref, fr_ref, dk_ref, dv_ref, df_ref, dk_s, dv_s, df_s = refs
        else:
            q_ref, k_ref, v_ref, do_ref, lse_ref, dl_ref, dk_ref, dv_ref, dk_s, dv_s = refs
        kj = pl.program_id(1)
        kv, vv = k_ref[...], v_ref[...]
        dk_s[...] = jnp.zeros(dk_s.shape, F32)
        dv_s[...] = jnp.zeros(dv_s.shape, F32)
        if use_f:
            df_s[...] = jnp.zeros(df_s.shape, F32)
            fc_b = jnp.broadcast_to(fc_ref[0], (t, HEAD_PAD))

        def step(i, masked):
            off = pl.multiple_of(i * t, t)
            qv = q_ref[pl.ds(off, t), :]
            dov = do_ref[pl.ds(off, t), :]
            st = lax.dot_general(kv, qv, (((1,), (1,)), ((), ())), preferred_element_type=F32)
            if use_f:
                st = st + (fr_ref[0, i] - _lanes(fc_b, t))
            if masked:
                st = jnp.where(_causal(t, False), st, NEG_BIG)
            pt = jnp.exp2(st - lse_ref[0, i])
            dv_s[...] += jnp.dot(pt.astype(BF16), dov, preferred_element_type=F32)
            dpt = lax.dot_general(vv, dov, (((1,), (1,)), ((), ())), preferred_element_type=F32)
            dst = pt * (dpt - dl_ref[0, i])
            dk_s[...] += jnp.dot(dst.astype(BF16), qv, preferred_element_type=F32)
            if use_f:
                df_s[...] -= _fold_lanes(dst)

        step(kj, True)

        def loop_body(i, carry):
            step(i, False)
            return carry

        lax.fori_loop(kj + 1, nq, loop_body, 0)
        dk_ref[...] = dk_s[...] * scale
        dv_ref[...] = dv_s[...]
        if use_f:
            df_ref[0] = jnp.sum(df_s[...], axis=-1, keepdims=True)

    kspec = pl.BlockSpec((t, HEAD_PAD), lambda h, j: (j, h))
    qspec = pl.BlockSpec((s, HEAD_PAD), lambda h, j: (0, h))
    rowspec = pl.BlockSpec((1, nq, 1, t), lambda h, j: (h, 0, 0, 0))
    colspec = pl.BlockSpec((1, t, 1), lambda h, j: (h, j, 0))
    in_specs, args = [qspec, kspec, kspec, qspec, rowspec, rowspec], [q, k, v, do, lse_row, delta_row]
    out_specs = [kspec, kspec]
    out_shape = [jax.ShapeDtypeStruct((s, N_HEADS * HEAD_PAD), F32)] * 2
    scratch = [pltpu.VMEM((t, HEAD_PAD), F32), pltpu.VMEM((t, HEAD_PAD), F32)]
    if use_f:
        in_specs += [colspec, rowspec]
        args += [fcol, frow]
        out_specs += [colspec]
        out_shape += [jax.ShapeDtypeStruct((N_HEADS, s, 1), F32)]
        scratch += [pltpu.VMEM((t, HEAD_PAD), F32)]
    return pl.pallas_call(
        body, name=name, grid=(N_HEADS, nq),
        in_specs=in_specs, out_specs=tuple(out_specs), out_shape=tuple(out_shape),
        scratch_shapes=scratch,
        compiler_params=_params(("parallel", "arbitrary")),
    )(*args)


def _gate_fwd(pm, pf, proj, name):
    s, w = pm.shape
    tm = _tile(s, ROW_T)

    def body(pm_ref, pf_ref, gm_ref, gf_ref, y_ref):
        y = jax.nn.sigmoid(gm_ref[...]) * pm_ref[...] + jax.nn.sigmoid(gf_ref[...]) * pf_ref[...]
        y_ref[...] = y.astype(y_ref.dtype)

    row = pl.BlockSpec((tm, w), lambda i: (i, 0))
    return pl.pallas_call(
        body, name=name, grid=(s // tm,),
        in_specs=[row, row, pl.BlockSpec((tm, w), lambda i: (i, P_GM // D_MODEL)),
                  pl.BlockSpec((tm, w), lambda i: (i, P_GF // D_MODEL))],
        out_specs=row, out_shape=jax.ShapeDtypeStruct((s, w), BF16),
        compiler_params=_params(("parallel",)),
    )(pm, pf, proj, proj)


def _gate_bwd(dy, pm, pf, proj, name):
    s, w = pm.shape
    tm = _tile(s, ROW_T)

    def body(dy_ref, pm_ref, pf_ref, gm_ref, gf_ref, dpm_ref, dpf_ref, dgm_ref, dgf_ref):
        dyv = dy_ref[...]
        sm, sf = jax.nn.sigmoid(gm_ref[...]), jax.nn.sigmoid(gf_ref[...])
        dpm_ref[...] = (dyv * sm).astype(BF16)
        dpf_ref[...] = (dyv * sf).astype(BF16)
        dgm_ref[...] = (dyv * pm_ref[...] * (sm * (1.0 - sm))).astype(BF16)
        dgf_ref[...] = (dyv * pf_ref[...] * (sf * (1.0 - sf))).astype(BF16)

    row = pl.BlockSpec((tm, w), lambda i: (i, 0))
    out = jax.ShapeDtypeStruct((s, w), BF16)
    return pl.pallas_call(
        body, name=name, grid=(s // tm,),
        in_specs=[row, row, row, pl.BlockSpec((tm, w), lambda i: (i, P_GM // D_MODEL)),
                  pl.BlockSpec((tm, w), lambda i: (i, P_GF // D_MODEL))],
        out_specs=(row, row, row, row), out_shape=(out, out, out, out),
        compiler_params=_params(("parallel",)),
    )(dy, pm, pf, proj, proj)


CONV_TN = 256
CONV_TM = 512
HALO = 8


def _shift_down(u, prev, n):
    rolled = pltpu.roll(u, n, 0)
    prev_rolled = pltpu.roll(prev, n, 0)
    top = jnp.concatenate([prev_rolled, rolled[HALO:]], axis=0)
    row = lax.broadcasted_iota(jnp.int32, u.shape, 0)
    return jnp.where(row < n, top, rolled)


def _conv_tile(u, prev, w_ref, b_ref):
    um1 = _shift_down(u, prev, 1)
    um2 = _shift_down(u, prev, 2)
    uc = b_ref[...] + w_ref[0:1, :] * um2 + w_ref[1:2, :] * um1 + w_ref[2:3, :] * u
    return uc, um1, um2


def _conv_specs(tm, tn, ncol_off):
    blk = lambda off: pl.BlockSpec((tm, tn), lambda j, i: (i, j + off))
    halo = lambda off: pl.BlockSpec((HALO, tn), lambda j, i: (jnp.maximum(i * (tm // HALO) - 1, 0), j + off))
    wsp = lambda off: pl.BlockSpec((3, tn), lambda j, i: (0, j + off))
    bsp = lambda off: pl.BlockSpec((1, tn), lambda j, i: (0, j + off))
    return blk, halo, wsp, bsp


def _convglu_fwd(u, conv_w, conv_b, name):
    s = u.shape[0]
    tm, tn = _tile(s, CONV_TM), CONV_TN
    nj = D_FF // tn
    blk, halo, wsp, bsp = _conv_specs(tm, tn, nj)

    def body(ug_ref, pg_ref, uv_ref, pv_ref, wg_ref, wv_ref, bg_ref, bv_ref, a_ref):
        live = (pl.program_id(1) > 0).astype(F32)
        gate, _, _ = _conv_tile(ug_ref[...], pg_ref[...] * live, wg_ref, bg_ref)
        val, _, _ = _conv_tile(uv_ref[...], pv_ref[...] * live, wv_ref, bv_ref)
        a_ref[...] = (gate * jax.nn.sigmoid(gate) * val).astype(a_ref.dtype)

    return pl.pallas_call(
        body, name=name, grid=(nj, s // tm),
        in_specs=[blk(0), halo(0), blk(nj), halo(nj), wsp(0), wsp(nj), bsp(0), bsp(nj)],
        out_specs=blk(0), out_shape=jax.ShapeDtypeStruct((s, D_FF), BF16),
        compiler_params=_params(("parallel", "arbitrary")),
    )(u, u, u, u, conv_w, conv_w, conv_b, conv_b)


def _convglu_bwd(da, u, conv_w, conv_b, name):
    s = u.shape[0]
    tm, tn = _tile(s, CONV_TM), CONV_TN
    nj = D_FF // tn
    blk, halo, wsp, bsp = _conv_specs(tm, tn, nj)

    def body(da_ref, ug_ref, pg_ref, uv_ref, pv_ref, wg_ref, wv_ref, bg_ref, bv_ref,
             dg_ref, dv_ref, sg_ref, sv_ref):
        live = (pl.program_id(1) > 0).astype(F32)
        ug, uv = ug_ref[...], uv_ref[...]
        gate, ug1, ug2 = _conv_tile(ug, pg_ref[...] * live, wg_ref, bg_ref)
        val, uv1, uv2 = _conv_tile(uv, pv_ref[...] * live, wv_ref, bv_ref)
        dav = da_ref[...]
        sig = jax.nn.sigmoid(gate)
        dgate = dav * val * (sig * (1.0 + gate * (1.0 - sig)))
        dval = dav * (gate * sig)
        dg_ref[...] = dgate
        dv_ref[...] = dval

        @pl.when(pl.program_id(1) == 0)
        def _():
            sg_ref[...] = jnp.zeros_like(sg_ref)
            sv_ref[...] = jnp.zeros_like(sv_ref)

        for s_ref, d, taps in ((sg_ref, dgate, (ug2, ug1, ug)), (sv_ref, dval, (uv2, uv1, uv))):
            for r, tap in enumerate(taps):
                s_ref[r:r + 1, :] += jnp.sum(d * tap, axis=0, keepdims=True)
            s_ref[3:4, :] += jnp.sum(d, axis=0, keepdims=True)

    sums = lambda off: pl.BlockSpec((8, tn), lambda j, i: (0, j + off))
    return pl.pallas_call(
        body, name=name, grid=(nj, s // tm),
        in_specs=[blk(0), blk(0), halo(0), blk(nj), halo(nj), wsp(0), wsp(nj), bsp(0), bsp(nj)],
        out_specs=(blk(0), blk(0), sums(0), sums(0)),
        out_shape=(jax.ShapeDtypeStruct((s, D_FF), F32), jax.ShapeDtypeStruct((s, D_FF), F32),
                   jax.ShapeDtypeStruct((8, D_FF), F32), jax.ShapeDtypeStruct((8, D_FF), F32)),
        compiler_params=_params(("parallel", "arbitrary")),
    )(da, u, u, u, u, conv_w, conv_w, conv_b, conv_b)


def _conv_transpose(d, conv_w_half, name):
    s, w = d.shape
    tm, tn = _tile(s, CONV_TM), CONV_TN
    last = s // tm - 1

    def body(d_ref, nx_ref, w_ref, o_ref):
        dv = d_ref[...]
        nxt = nx_ref[...] * (pl.program_id(1) < last).astype(F32)
        row = lax.broadcasted_iota(jnp.int32, dv.shape, 0)

        def shift_up(n):
            rolled = pltpu.roll(dv, tm - n, 0)
            nxt_rolled = pltpu.roll(nxt, HALO - n, 0)
            bottom = jnp.concatenate([rolled[:tm - HALO], nxt_rolled], axis=0)
            return jnp.where(row >= tm - n, bottom, rolled)

        out = w_ref[2:3, :] * dv + w_ref[1:2, :] * shift_up(1) + w_ref[0:1, :] * shift_up(2)
        o_ref[...] = out.astype(o_ref.dtype)

    blk = pl.BlockSpec((tm, tn), lambda j, i: (i, j))
    nxt_spec = pl.BlockSpec((HALO, tn), lambda j, i: (jnp.minimum((i + 1) * (tm // HALO), s // HALO - 1), j))
    return pl.pallas_call(
        body, name=name, grid=(w // tn, s // tm),
        in_specs=[blk, nxt_spec, pl.BlockSpec((3, tn), lambda j, i: (0, j))],
        out_specs=blk, out_shape=jax.ShapeDtypeStruct((s, w), BF16),
        compiler_params=_params(("parallel", "arbitrary")),
    )(d, d, conv_w_half)


def _split3(a):
    a1 = a.astype(BF16)
    r1 = a - a1.astype(F32)
    a2 = r1.astype(BF16)
    a3 = (r1 - a2.astype(F32)).astype(BF16)
    return a1, a2, a3


def _ones_dot_right(a, mat):
    return sum(jnp.dot(p, mat, preferred_element_type=F32) for p in _split3(a))


def _ones_dot_left(mat, a):
    return sum(jnp.dot(mat, p, preferred_element_type=F32) for p in _split3(a))


def _tri(n, cmp):
    r = lax.broadcasted_iota(jnp.int32, (n, n), 0)
    c = lax.broadcasted_iota(jnp.int32, (n, n), 1)
    return cmp(r, c).astype(BF16)


def _forget_fwd(z, bias, name):
    nh, nr, nl = z.shape

    def body(z_ref, b_ref, f_ref):
        within = _tri(nl, lambda r, c: r <= c)
        before = _tri(nr, lambda r, c: c < r)
        for h in range(nh):
            x = z_ref[h] + b_ref[h]
            lf = jnp.minimum(x, 0.0) - jnp.log(1.0 + jnp.exp(-jnp.abs(x)))
            pre = _ones_dot_right(lf, within)
            tot = jnp.zeros((nr, nl), F32) + jnp.sum(lf, axis=1, keepdims=True)
            f_ref[h] = pre + _ones_dot_left(before, tot)

    return pl.pallas_call(
        body, name=name, out_shape=jax.ShapeDtypeStruct(z.shape, F32),
        compiler_params=pltpu.CompilerParams(vmem_limit_bytes=VMEM_LIMIT_BYTES),
    )(z, bias)


def _forget_bwd(df_rows, df_cols, z, bias, name):
    nh, nr, nl = z.shape

    def body(dfr_ref, dfc_ref, z_ref, b_ref, dz_ref, db_ref):
        within = _tri(nl, lambda r, c: r >= c)
        after = _tri(nr, lambda r, c: c > r)
        for h in range(nh):
            g = dfr_ref[h] + dfc_ref[h]
            suf = _ones_dot_right(g, within)
            tot = jnp.zeros((nr, nl), F32) + jnp.sum(g, axis=1, keepdims=True)
            dlf = suf + _ones_dot_left(after, tot)
            dz = dlf * jax.nn.sigmoid(-(z_ref[h] + b_ref[h]))
            dz_ref[h] = dz
            db_ref[h] = jnp.zeros((1, nl), F32) + jnp.sum(dz)

    return pl.pallas_call(
        body, name=name,
        out_shape=(jax.ShapeDtypeStruct(z.shape, F32), jax.ShapeDtypeStruct(bias.shape, F32)),
        compiler_params=pltpu.CompilerParams(vmem_limit_bytes=VMEM_LIMIT_BYTES),
    )(df_rows, df_cols, z, bias)


def _ada_fwd(c_col, w, b, name):
    kdim, n = w.shape

    def body(c_ref, w_ref, b_ref, ada_ref, act_ref):
        wv = w_ref[...]
        for e in range(N_DEV):
            cv = c_ref[e]
            act = cv * jax.nn.sigmoid(cv)
            act_ref[e] = act
            ada_ref[e:e + 1, :] = jnp.sum(act * wv, axis=0, keepdims=True) + b_ref[...]

    return pl.pallas_call(
        body, name=name,
        out_shape=(jax.ShapeDtypeStruct((N_DEV, n), F32), jax.ShapeDtypeStruct((N_DEV, kdim, 1), F32)),
        compiler_params=pltpu.CompilerParams(vmem_limit_bytes=VMEM_LIMIT_BYTES),
    )(c_col, w, b)


def _ada_bwd(act_col, dada, name):
    kdim = act_col.shape[1]
    n = dada.shape[1]

    def body(act_ref, d_ref, g_ref):
        acc = act_ref[0] * d_ref[0:1, :]
        for e in range(1, N_DEV):
            acc = acc + act_ref[e] * d_ref[e:e + 1, :]
        g_ref[...] = acc

    return pl.pallas_call(
        body, name=name, out_shape=jax.ShapeDtypeStruct((kdim, n), F32),
        compiler_params=pltpu.CompilerParams(vmem_limit_bytes=VMEM_LIMIT_BYTES),
    )(act_col, dada)


def _adamw(parts, w, m, v, name, tr=128):
    npart, r, c = parts.shape
    tr = _tile(r, tr, step=PACK_ALIGN) if r % PACK_ALIGN == 0 else r

    def body(p_ref, w_ref, m_ref, v_ref, g_ref, d_ref, nm_ref, nv_ref):
        g = p_ref[0].astype(F32)
        for e in range(1, npart):
            g = g + p_ref[e].astype(F32)
        nm = ADAM_B1 * m_ref[...] + (1.0 - ADAM_B1) * g
        nv = ADAM_B2 * v_ref[...] + (1.0 - ADAM_B2) * (g * g)
        m_hat = nm / (1.0 - ADAM_B1 ** ADAM_STEP)
        v_hat = nv / (1.0 - ADAM_B2 ** ADAM_STEP)
        g_ref[...] = g
        d_ref[...] = -ADAM_LR * (m_hat / (jnp.sqrt(v_hat) + ADAM_EPS) + ADAM_WD * w_ref[...])
        nm_ref[...] = nm
        nv_ref[...] = nv

    row = pl.BlockSpec((tr, c), lambda i: (i, 0))
    out = jax.ShapeDtypeStruct((r, c), F32)
    return pl.pallas_call(
        body, name=name, grid=(r // tr,),
        in_specs=[pl.BlockSpec((npart, tr, c), lambda i: (0, i, 0)), row, row, row],
        out_specs=(row, row, row, row), out_shape=(out, out, out, out),
        compiler_params=_params(("parallel",)),
    )(parts, w, m, v)


PACK_C = 1024
BIG = ("w_in", "w_uq", "w_ukv", "w_o_mla", "w_o_fox", "w_out", "w_up", "conv_w", "w_down")
SHARD_SHAPES = {"w_in": (1024, 533), "w_uq": (384, 96), "w_ukv": (256, 128), "w_o_mla": (512, 128),
                "w_o_fox": (512, 128), "w_out": (128, 1024), "w_up": (1024, 704), "conv_w": (3, 704),
                "w_down": (352, 1024)}
PACK_ALIGN = 16


def _rows_of(name):
    r, c = SHARD_SHAPES[name]
    n = 2 * r * c if name == "conv_w" else r * c
    rows = -(-n // PACK_C)
    return -(-rows // PACK_ALIGN) * PACK_ALIGN


PACK_OFF = {}
PACK_ROWS = 0
for _n in BIG:
    PACK_OFF[_n] = PACK_ROWS
    PACK_ROWS += _rows_of(_n)


def _flat_rows(a, rows):
    lead = a.shape[:-2]
    flat = a.reshape(lead + (-1,))
    pad = rows * PACK_C - flat.shape[-1]
    flat = jnp.pad(flat, [(0, 0)] * len(lead) + [(0, pad)])
    return flat.reshape(lead + (rows, PACK_C))


def _pack_gather(shards):
    parts = []
    for nme in BIG:
        a = shards[nme]
        if nme == "conv_w":
            bits = lax.bitcast_convert_type(a, BF16)
            parts.append(_flat_rows(bits.reshape(3, 2 * 704), _rows_of(nme)))
        else:
            parts.append(_flat_rows(a.astype(BF16), _rows_of(nme)))
    return jnp.concatenate(parts, axis=0)


def _unpack_gather(wg):
    out = {}
    for nme in BIG:
        r, c = SHARD_SHAPES[nme]
        o = PACK_OFF[nme]
        seg = wg[:, o:o + _rows_of(nme)].reshape(N_DEV, -1)
        if nme == "conv_w":
            bits = seg[:, :2 * r * c].reshape(N_DEV, r, c, 2)
            out[nme] = lax.bitcast_convert_type(bits, F32)
        else:
            out[nme] = seg[:, :r * c].reshape(N_DEV, r, c)
    return out


def _pack_f32(shards):
    return jnp.concatenate([_flat_rows(shards[nme], _rows_of(nme)) for nme in BIG], axis=-2)


def _unpack_f32(p):
    out = {}
    for nme in BIG:
        r, c = SHARD_SHAPES[nme]
        o = PACK_OFF[nme]
        seg = p[o:o + _rows_of(nme)].reshape(-1)
        out[nme] = seg[:r * c].reshape(1, r, c)
    return out


def _cols_to_full(stack):
    n, r, c = stack.shape
    return stack.transpose(1, 0, 2).reshape(r, n * c)


def _full_to_cols(full, c):
    r = full.shape[0]
    return full.reshape(r, N_DEV, c).transpose(1, 0, 2)


def _pad_heads(a, width, ones_lane=False):
    s = a.shape[0]
    a = a.reshape(s, N_HEADS, width)
    if ones_lane:
        assert width == SUM_LANE
        tail = jnp.zeros((s, N_HEADS, HEAD_PAD - width), a.dtype).at[:, :, 0].set(1.0)
        return jnp.concatenate([a, tail], axis=2).reshape(s, N_HEADS * HEAD_PAD)
    return jnp.pad(a, ((0, 0), (0, 0), (0, HEAD_PAD - width))).reshape(s, N_HEADS * HEAD_PAD)


def _unpad_heads(a, width):
    s = a.shape[0]
    return a.reshape(s, N_HEADS, HEAD_PAD)[:, :, :width].reshape(s, N_HEADS * width)


def _w_in_padded(w_in):
    seg = [w_in[:, IN_OFF[i]:IN_OFF[i + 1]] for i in range(9)]
    cq, ckv, kr, fq, fk, fv, fl, gm, gf = seg
    padc = lambda a, n: jnp.pad(a, ((0, 0), (0, n - a.shape[1])))
    return jnp.concatenate([gm, gf, fq, fk, fv, cq, ckv, padc(kr, 128), padc(fl, 128)], axis=1)


def _w_in_unpadded(g):
    return jnp.concatenate([
        g[:, P_CQ:P_CQ + 384], g[:, P_CKV:P_CKV + 256], g[:, P_KR:P_KR + 32], g[:, P_FQ:P_FQ + 512],
        g[:, P_FK:P_FK + 512], g[:, P_FV:P_FV + 512], g[:, P_FL:P_FL + 8], g[:, P_GM:P_GM + 1024],
        g[:, P_GF:P_GF + 1024]], axis=1)


SMALL = (("b_ada", 6144, 6144), ("norm_mix_g", 1024, 1024), ("q_norm_g", 384, 384), ("kv_norm_g", 256, 256),
         ("b_forget", 8, 128), ("norm_ffn_g", 1024, 1024), ("conv_b", 5632, 5632), ("norm_final_g", 1024, 1024),
         ("loss", 1, 128))
SMALL_OFF = {}
_o = 0
for _n, _real, _padded in SMALL:
    SMALL_OFF[_n] = _o
    _o += _padded
SMALL_W = _o


def _pack_small(vals):
    parts = []
    for nme, real, padded in SMALL:
        a = vals[nme].reshape(1, real).astype(F32)
        parts.append(jnp.pad(a, ((0, 0), (0, padded - real))))
    return jnp.concatenate(parts, axis=1)


def kernel(x, c, positions, w_ada, b_ada, norm_mix_g, w_in, q_norm_g, w_uq, kv_norm_g, w_ukv, b_forget, w_o_mla, w_o_fox, w_out, norm_ffn_g, w_up, conv_w, conv_b, w_down, norm_final_g, loss_target, m_w_ada, m_b_ada, m_norm_mix_g, m_w_in, m_q_norm_g, m_w_uq, m_kv_norm_g, m_w_ukv, m_b_forget, m_w_o_mla, m_w_o_fox, m_w_out, m_norm_ffn_g, m_w_up, m_conv_w, m_conv_b, m_w_down, m_norm_final_g, v_w_ada, v_b_ada, v_norm_mix_g, v_w_in, v_q_norm_g, v_w_uq, v_kv_norm_g, v_w_ukv, v_b_forget, v_w_o_mla, v_w_o_fox, v_w_out, v_norm_ffn_g, v_w_up, v_conv_w, v_conv_b, v_w_down, v_norm_final_g):
    me = 4 * lax.axis_index("x") + 2 * lax.axis_index("y") + lax.axis_index("c")
    x = x[0]
    target = loss_target[0]
    s = x.shape[0]
    nblk = s // ATT_T
    big_w = {"w_in": w_in, "w_uq": w_uq, "w_ukv": w_ukv, "w_o_mla": w_o_mla, "w_o_fox": w_o_fox,
             "w_out": w_out, "w_up": w_up, "conv_w": conv_w, "w_down": w_down}
    big_m = {"w_in": m_w_in, "w_uq": m_w_uq, "w_ukv": m_w_ukv, "w_o_mla": m_w_o_mla, "w_o_fox": m_w_o_fox,
             "w_out": m_w_out, "w_up": m_w_up, "conv_w": m_conv_w, "w_down": m_w_down}
    big_v = {"w_in": v_w_in, "w_uq": v_w_uq, "w_ukv": v_w_ukv, "w_o_mla": v_w_o_mla, "w_o_fox": v_w_o_fox,
             "w_out": v_w_out, "w_up": v_w_up, "conv_w": v_conv_w, "w_down": v_w_down}

    wg = _all_gather(_pack_gather({k: a[0] for k, a in big_w.items()}), "gather_weights")
    st = _unpack_gather(wg)
    w_in_p = _w_in_padded(_cols_to_full(st["w_in"]))
    uq = st["w_uq"]
    w_uq_p = jnp.pad(uq, ((0, 0), (0, 0), (0, HEAD_PAD - 96))).transpose(1, 0, 2).reshape(MLA_Q_RANK, 1024)
    ukv = st["w_ukv"]
    zeros64 = jnp.zeros((N_HEADS, MLA_KV_RANK, 64), BF16)
    w_uk_p = jnp.concatenate([ukv[:, :, :64], zeros64], axis=2).transpose(1, 0, 2).reshape(MLA_KV_RANK, 1024)
    w_uv_p = jnp.concatenate([ukv[:, :, 64:], zeros64], axis=2).transpose(1, 0, 2).reshape(MLA_KV_RANK, 1024)
    place = np.zeros((HEAD_PAD, N_HEADS, HEAD_PAD), np.float32)
    for j in range(MLA_ROPE):
        place[j, :, MLA_NOPE + j] = 1.0
    place = jnp.asarray(place.reshape(HEAD_PAD, 1024), BF16)
    w_kv_comb = jnp.concatenate([
        jnp.concatenate([w_uk_p, w_uv_p], axis=1),
        jnp.concatenate([place, jnp.zeros((HEAD_PAD, 1024), BF16)], axis=1)], axis=0)
    pad_o = lambda full: jnp.pad(full.reshape(N_HEADS, 64, 1024), ((0, 0), (0, 64), (0, 0))).reshape(1024, 1024)
    w_o_mla_p = pad_o(_cols_to_full(st["w_o_mla"]))
    w_o_fox_p = pad_o(_cols_to_full(st["w_o_fox"]))
    w_out_f = st["w_out"].reshape(1024, 1024)
    w_up_f = _cols_to_full(st["w_up"])
    conv_w_f = _cols_to_full(st["conv_w"])
    w_down_f = st["w_down"].reshape(D_FF, 1024)

    c_all = _all_gather(c, "gather_c").reshape(N_DEV, D_MODEL)
    b_ada_mine = lax.dynamic_slice(b_ada, (0, me * 768), (1, 768))
    ada_cols, act_col = _ada_fwd(c_all.reshape(N_DEV, D_MODEL, 1), w_ada[0], b_ada_mine, "ada_fwd")
    ada_all = _all_gather(ada_cols, "gather_ada")
    ada = lax.dynamic_slice(ada_all, (0, me, 0), (N_DEV, 1, 768)).reshape(1, N_ADA * D_MODEL)
    sh_m, sc_m, g_m, sh_f, sc_f, g_f = [ada[:, i * D_MODEL:(i + 1) * D_MODEL] for i in range(N_ADA)]

    inv_freq = ROPE_THETA ** (-jnp.arange(0, MLA_ROPE, 2, dtype=F32) / MLA_ROPE)
    ang = positions[0].astype(F32)[:, None] * inv_freq
    cos, sin = jnp.cos(ang), jnp.sin(ang)
    rope_c = jnp.concatenate([jnp.ones((s, 64), F32), cos, cos, jnp.zeros((s, 32), F32)], axis=1)
    rope_s = jnp.concatenate([jnp.zeros((s, 64), F32), -sin, sin, jnp.zeros((s, 32), F32)], axis=1)

    zero_d = jnp.zeros((1, D_MODEL), F32)

    h1 = _rms_mod(x, norm_mix_g, sc_m, sh_m, "norm_mix")
    proj = _mm(h1, w_in_p, "nn", F32, "proj_in", tn=640)
    cq = proj[:, P_CQ:P_CQ + 384]
    ckv = proj[:, P_CKV:P_CKV + 256]
    qn = _rms_mod(cq, q_norm_g, jnp.zeros((1, 384), F32), jnp.zeros((1, 384), F32), "q_norm")
    kvn = _rms_mod(ckv, kv_norm_g, jnp.zeros((1, 256), F32), jnp.zeros((1, 256), F32), "kv_norm")
    kv_in = jnp.concatenate([kvn, proj[:, P_KR:P_KR + 128].astype(BF16)], axis=1)
    q_pre = _mm(qn, w_uq_p, "nn", F32, "q_up")
    kv_pre = _mm(kv_in, w_kv_comb, "nn", F32, "kv_up")
    q_fold = MLA_SCALE * LOG2E
    (q_att,) = _rope(q_pre, rope_c * q_fold, rope_s * q_fold, "rope_q", N_HEADS, (BF16,))
    k_att, v_att = _rope(kv_pre, rope_c, rope_s, "rope_kv", N_HEADS, (BF16, BF16))
    o_mla, o_mla_b, lse_mla = _attn_fwd(q_att, k_att, v_att, None, None, "mla_fwd")

    fq = _pad_heads(proj[:, P_FQ:P_FQ + 512] * (FOX_SCALE * LOG2E), 64).astype(BF16)
    fk = _pad_heads(proj[:, P_FK:P_FK + 512], 64).astype(BF16)
    fv = _pad_heads(proj[:, P_FV:P_FV + 512], 64, ones_lane=True).astype(BF16)
    z = proj[:, P_FL:P_FL + 8].T.reshape(N_HEADS, s // SEQ_LANES, SEQ_LANES)
    bias_f = jnp.broadcast_to(b_forget.reshape(N_HEADS, 1, 1), (N_HEADS, 1, SEQ_LANES))
    f_cum = _forget_fwd(z, bias_f, "forget_fwd")
    f_col = (f_cum * LOG2E).reshape(N_HEADS, s, 1)
    f_row = f_col.reshape(N_HEADS, nblk, 1, ATT_T)
    o_fox, o_fox_b, lse_fox = _attn_fwd(fq, fk, fv, f_col, f_row, "fox_fwd")

    pm = _mm(o_mla_b, w_o_mla_p, "nn", F32, "o_mla_proj")
    pf = _mm(o_fox_b, w_o_fox_p, "nn", F32, "o_fox_proj")
    y = _gate_fwd(pm, pf, proj, "gate_fwd")
    x2, mix = _mm(y, w_out_f, "nn", F32, "out_proj", res=x, gvec=g_m)

    h2 = _rms_mod(x2, norm_ffn_g, sc_f, sh_f, "norm_ffn")
    u = _mm(h2, w_up_f, "nn", F32, "ffn_up")
    a = _convglu_fwd(u, conv_w_f, conv_b, "convglu_fwd")
    x3, ffn = _mm(a, w_down_f, "nn", F32, "ffn_down", res=x2, gvec=g_f, tk=2816)

    dx3, sums_final = _final_loss(x3, target, norm_final_g.reshape(1, D_MODEL), "final_loss")
    dffn, sums_gf = _scale_bwd(dx3, ffn, g_f, "ffn_scale_bwd")
    da = _mm(dffn, w_down_f, "nt", F32, "ffn_down_dx")
    g_w_down = _mm(a.T, dffn, "nn", F32, "ffn_down_dw", tm=256)
    dgate, dval, s_gate, s_val = _convglu_bwd(da, u, conv_w_f, conv_b, "convglu_bwd")
    du = jnp.concatenate([_conv_transpose(dgate, conv_w_f[:, :D_FF], "conv_t_gate"),
                          _conv_transpose(dval, conv_w_f[:, D_FF:], "conv_t_val")], axis=1)
    dh2 = _mm(du, w_up_f, "nt", F32, "ffn_up_dx", tk=512)
    g_w_up = _mm(h2.T, du, "nn", F32, "ffn_up_dw")
    dx2, sums_ffn = _rms_mod_bwd(dh2, x2, norm_ffn_g, sc_f, dx3, "norm_ffn_bwd")

    dmix, sums_gm = _scale_bwd(dx2, mix, g_m, "mix_scale_bwd")
    dy = _mm(dmix, w_out_f, "nt", F32, "out_proj_dx")
    g_w_out = _mm(y.T, dmix, "nn", F32, "out_proj_dw")
    dpm, dpf, dgm, dgf = _gate_bwd(dy, pm, pf, proj, "gate_bwd")
    do_mla = _mm(dpm, w_o_mla_p, "nt", F32, "o_mla_dx")
    do_fox = _mm(dpf, w_o_fox_p, "nt", F32, "o_fox_dx")
    g_w_o_mla_p = _mm(o_mla_b.T, dpm, "nn", F32, "o_mla_dw")
    g_w_o_fox_p = _mm(o_fox_b.T, dpf, "nn", F32, "o_fox_dw")

    rows = lambda col: col.reshape(N_HEADS, nblk, 1, ATT_T)
    delta_mla = _attn_delta(o_mla, do_mla, "mla_delta")
    do_mla_b = do_mla.astype(BF16)
    dq_rot = _attn_bwd_dq(q_att, k_att, v_att, do_mla_b, lse_mla, delta_mla, None, None, MLA_SCALE, "mla_bwd_dq")
    dk_rot, dv_mla = _attn_bwd_dkv(q_att, k_att, v_att, do_mla_b, rows(lse_mla), rows(delta_mla), None, None,
                                   1.0 / LOG2E, "mla_bwd_dkv")
    (dq_pre,) = _rope(dq_rot, rope_c, -rope_s, "rope_q_bwd", N_HEADS, (BF16,))
    dkv_pre = _rope_bwd_kv(dk_rot, dv_mla, rope_c, -rope_s, "rope_kv_bwd")
    dqn = _mm(dq_pre, w_uq_p, "nt", F32, "q_up_dx")
    g_w_uq_p = _mm(qn.T, dq_pre, "nn", F32, "q_up_dw")
    dkv_in = _mm(dkv_pre, w_kv_comb, "nt", F32, "kv_up_dx")
    g_w_kv_comb = _mm(kv_in.T, dkv_pre, "nn", F32, "kv_up_dw")
    dcq, sums_q = _rms_mod_bwd(dqn, cq, q_norm_g, jnp.zeros((1, 384), F32), None, "q_norm_bwd")
    dckv, sums_kv = _rms_mod_bwd(dkv_in[:, :256], ckv, kv_norm_g, jnp.zeros((1, 256), F32), None, "kv_norm_bwd")
    delta_fox = _attn_delta(o_fox, do_fox, "fox_delta")
    do_fox_b = do_fox.astype(BF16)
    dfq, dfr = _attn_bwd_dq(fq, fk, fv, do_fox_b, lse_fox, delta_fox, f_col, f_row, FOX_SCALE, "fox_bwd_dq")
    dfk, dfv, dfc = _attn_bwd_dkv(fq, fk, fv, do_fox_b, rows(lse_fox), rows(delta_fox), f_col, f_row,
                                  1.0 / LOG2E, "fox_bwd_dkv")
    df_rows = dfr.reshape(N_HEADS, s // SEQ_LANES, SEQ_LANES)
    df_cols = dfc.reshape(N_HEADS, s // SEQ_LANES, SEQ_LANES)
    dz, db_f = _forget_bwd(df_rows, df_cols, z, bias_f, "forget_bwd")
    dfl = jnp.pad(dz.reshape(N_HEADS, s).T, ((0, 0), (0, 128 - N_HEADS)))

    dproj = jnp.concatenate([
        dgm, dgf, _unpad_heads(dfq, 64).astype(BF16), _unpad_heads(dfk, 64).astype(BF16),
        _unpad_heads(dfv, 64).astype(BF16), dcq.astype(BF16), dckv.astype(BF16),
        dkv_in[:, 256:384].astype(BF16), dfl.astype(BF16)], axis=1)
    dh1 = _mm(dproj, w_in_p, "nt", F32, "proj_in_dx", tk=896)
    g_w_in_p = _mm(h1.T, dproj, "nn", F32, "proj_in_dw", tn=640)
    grad_x, sums_mix = _rms_mod_bwd(dh1, x, norm_mix_g, sc_m, dx2, "norm_mix_bwd")

    g_w_in = _w_in_unpadded(g_w_in_p)
    g_uq = g_w_uq_p.reshape(MLA_Q_RANK, N_HEADS, HEAD_PAD)[:, :, :96].transpose(1, 0, 2)
    g_uk = g_w_kv_comb[:256, :1024].reshape(256, N_HEADS, HEAD_PAD)[:, :, :64]
    g_uv = g_w_kv_comb[:256, 1024:].reshape(256, N_HEADS, HEAD_PAD)[:, :, :64]
    g_ukv = jnp.concatenate([g_uk, g_uv], axis=2).transpose(1, 0, 2)
    unpad_o = lambda g: g.reshape(N_HEADS, HEAD_PAD, 1024)[:, :64].reshape(512, 1024)
    g_conv_w = jnp.concatenate([s_gate[0:3], s_val[0:3]], axis=1)
    g_blocks = {
        "w_in": _full_to_cols(g_w_in, 533), "w_uq": g_uq, "w_ukv": g_ukv,
        "w_o_mla": _full_to_cols(unpad_o(g_w_o_mla_p), 128), "w_o_fox": _full_to_cols(unpad_o(g_w_o_fox_p), 128),
        "w_out": g_w_out.reshape(N_DEV, 128, 1024), "w_up": _full_to_cols(g_w_up, 704),
        "conv_w": _full_to_cols(g_conv_w, 704), "w_down": g_w_down.reshape(N_DEV, 352, 1024)}
    g_recv = _all_to_all(_pack_f32(g_blocks).astype(BF16), "scatter_grads")
    pk = lambda dct: _pack_f32({k: a[0] for k, a in dct.items()})
    g_big, d_big, nm_big, nv_big = [_unpack_f32(t) for t in
                                    _adamw(g_recv, pk(big_w), pk(big_m), pk(big_v), "adamw_big")]

    dada = jnp.concatenate([sums_mix[0:1], sums_mix[1:2], sums_gm[0:1], sums_ffn[0:1], sums_ffn[1:2], sums_gf[0:1]],
                           axis=1)
    small_part = _pack_small({
        "b_ada": dada, "norm_mix_g": sums_mix[2:3], "q_norm_g": sums_q[2:3], "kv_norm_g": sums_kv[2:3],
        "b_forget": db_f[:, 0, 0], "norm_ffn_g": sums_ffn[2:3],
        "conv_b": jnp.concatenate([s_gate[3:4], s_val[3:4]], axis=1), "norm_final_g": sums_final[0:1],
        "loss": sums_final[1:2, 0:1]})
    small_all = _all_gather(small_part, "gather_small")
    zero1 = jnp.zeros((1,), F32)
    small_w = {"b_ada": b_ada, "norm_mix_g": norm_mix_g, "q_norm_g": q_norm_g, "kv_norm_g": kv_norm_g,
               "b_forget": b_forget, "norm_ffn_g": norm_ffn_g, "conv_b": conv_b, "norm_final_g": norm_final_g,
               "loss": zero1}
    small_m = {"b_ada": m_b_ada, "norm_mix_g": m_norm_mix_g, "q_norm_g": m_q_norm_g, "kv_norm_g": m_kv_norm_g,
               "b_forget": m_b_forget, "norm_ffn_g": m_norm_ffn_g, "conv_b": m_conv_b,
               "norm_final_g": m_norm_final_g, "loss": zero1}
    small_v = {"b_ada": v_b_ada, "norm_mix_g": v_norm_mix_g, "q_norm_g": v_q_norm_g, "kv_norm_g": v_kv_norm_g,
               "b_forget": v_b_forget, "norm_ffn_g": v_norm_ffn_g, "conv_b": v_conv_b,
               "norm_final_g": v_norm_final_g, "loss": zero1}
    g_sm, d_sm, nm_sm, nv_sm = _adamw(small_all, _pack_small(small_w), _pack_small(small_m), _pack_small(small_v),
                                      "adamw_small")
    loss = g_sm[0, SMALL_OFF["loss"]]

    dada_all = small_all[:, 0, SMALL_OFF["b_ada"]:SMALL_OFF["b_ada"] + N_ADA * D_MODEL]
    dada_mine = lax.dynamic_slice(dada_all, (0, me * 768), (N_DEV, 768))
    g_ada_local = _ada_bwd(act_col, dada_mine, "ada_bwd")
    g_ada, d_ada, nm_ada, nv_ada = _adamw(g_ada_local[None], w_ada[0], m_w_ada[0], v_w_ada[0], "adamw_ada")

    def small_out(t, nme, shape):
        real = dict((n_, r_) for n_, r_, _ in SMALL)[nme]
        o = SMALL_OFF[nme]
        return t[0, o:o + real].reshape(shape)

    order = ["w_ada", "b_ada", "norm_mix_g", "w_in", "q_norm_g", "w_uq", "kv_norm_g", "w_ukv", "b_forget",
             "w_o_mla", "w_o_fox", "w_out", "norm_ffn_g", "w_up", "conv_w", "conv_b", "w_down", "norm_final_g"]
    small_shapes = {"b_ada": (1, 6144), "norm_mix_g": (1, 1024), "q_norm_g": (1, 384), "kv_norm_g": (1, 256),
                    "b_forget": (1, 8), "norm_ffn_g": (1, 1024), "conv_b": (1, 5632), "norm_final_g": (1024,)}

    def family(big, small, ada_t):
        out = []
        for nme in order:
            if nme == "w_ada":
                out.append(ada_t[None])
            elif nme in small_shapes:
                out.append(small_out(small, nme, small_shapes[nme]))
            else:
                out.append(big[nme])
        return out

    return (loss, grad_x[None], *family(g_big, g_sm, g_ada), *family(d_big, d_sm, d_ada),
            *family(nm_big, nm_sm, nm_ada), *family(nv_big, nv_sm, nv_ada))
```

```python
import math

import numpy as np
import jax
import jax.numpy as jnp
from jax import lax
from jax.experimental import pallas as pl
from jax.experimental.pallas import tpu as pltpu

F32 = jnp.float32
BF16 = jnp.bfloat16

N_DEV = 8
D_MODEL = 1024
N_HEADS = 8
HEAD_PAD = 128
MLA_Q_RANK = 384
MLA_KV_RANK = 256
MLA_NOPE = 64
MLA_ROPE = 32
MLA_V = 64
FOX_DIM = 64
D_FF = 2816
N_ADA = 6
EPS = 1e-6
ROPE_THETA = 10000.0
MLA_SCALE = 1.0 / math.sqrt(MLA_NOPE + MLA_ROPE)
FOX_SCALE = 1.0 / math.sqrt(FOX_DIM)
IN_SPLITS = (384, 256, 32, 512, 512, 512, 8, 1024, 1024)
D_IN = sum(IN_SPLITS)
IN_OFF = tuple(int(v) for v in np.cumsum((0,) + IN_SPLITS))
P_GM, P_GF, P_FQ, P_FK, P_FV, P_CQ, P_CKV, P_KR, P_FL, D_IN_P = 0, 1024, 2048, 2560, 3072, 3584, 3968, 4224, 4352, 4480

ADAM_LR, ADAM_B1, ADAM_B2, ADAM_EPS, ADAM_WD, ADAM_STEP = 0.001, 0.9, 0.999, 1e-08, 0.01, 10

VMEM_LIMIT_BYTES = 56 * 1024 * 1024
NEG_BIG = -1e30
ATT_T = 512
LOG2E = 1.4426950408889634
SUM_LANE = 64
ROW_T = 256
SEQ_LANES = 128
BF16_ROWS = 16


def _params(sem):
    return pltpu.CompilerParams(dimension_semantics=sem, vmem_limit_bytes=VMEM_LIMIT_BYTES)


def _tile(n, target, step=128):
    if n <= target:
        return n
    t = (target // step) * step
    while t >= step:
        if n % t == 0:
            return t
        t -= step
    return n


def _vec_spec(w, nargs):
    if nargs == 1:
        return pl.BlockSpec((1, w), lambda i: (0, 0))
    return pl.BlockSpec((1, w), lambda i, j: (0, 0))


def _comm_call(body, name, ins, out_shapes):
    n = len(ins)
    any_spec = pl.BlockSpec(memory_space=pl.ANY)
    return pl.pallas_call(
        body, name=name, out_shape=tuple(out_shapes),
        in_specs=[any_spec] * n, out_specs=tuple([any_spec] * n),
        scratch_shapes=[pltpu.SemaphoreType.DMA((n, 7)), pltpu.SemaphoreType.DMA((n, 7)),
                        pltpu.SemaphoreType.DMA((n,))],
    )(*ins)


def _all_gather(xs, name):
    n = len(xs)

    def body(*refs):
        x_refs, out_refs = refs[:n], refs[n:2 * n]
        send_sems, recv_sems, local_sems = refs[2 * n:]
        x_, y_, c_ = lax.axis_index("x"), lax.axis_index("y"), lax.axis_index("c")
        me, sibling = (x_, y_, c_), (x_, y_, 1 - c_)
        chips = [(1 - x_, y_), (x_, 1 - y_), (1 - x_, 1 - y_)]

        def slot(a, px, py, pc):
            return out_refs[a].at[4 * px + 2 * py + pc]

        def copy(a, k, block, to, src=None):
            return pltpu.make_async_remote_copy(
                src_ref=slot(a, *block) if src is None else src, dst_ref=slot(a, *block),
                send_sem=send_sems.at[a, k], recv_sem=recv_sems.at[a, k],
                device_id=to, device_id_type=pl.DeviceIdType.MESH)

        mine = [pltpu.make_async_copy(x_refs[a], slot(a, *me), local_sems.at[a]) for a in range(n)]
        for cp in mine:
            cp.start()
        first = []
        for a in range(n):
            first.append(copy(a, 0, me, sibling, src=x_refs[a]))
            first += [copy(a, 1 + j, me, (*chip, c_), src=x_refs[a]) for j, chip in enumerate(chips)]
        for cp in first:
            cp.start()
        passed = []
        for j, chip in enumerate(chips):
            for a in range(n):
                copy(a, 1 + j, (*chip, c_), me).wait_recv()
                passed.append(copy(a, 4 + j, (*chip, c_), sibling))
                passed[-1].start()
        for a in range(n):
            copy(a, 0, sibling, me).wait_recv()
            for j, chip in enumerate(chips):
                copy(a, 4 + j, (*chip, 1 - c_), me).wait_recv()
        for cp in first + passed:
            cp.wait_send()
        for cp in mine:
            cp.wait()

    return _comm_call(body, name, xs, [jax.ShapeDtypeStruct((N_DEV,) + x.shape, x.dtype) for x in xs])


def _all_to_all(gs, name):
    n = len(gs)

    def body(*refs):
        g_refs, out_refs = refs[:n], refs[n:2 * n]
        send_sems, recv_sems, local_sems = refs[2 * n:]
        x_, y_, c_ = lax.axis_index("x"), lax.axis_index("y"), lax.axis_index("c")
        me = 4 * x_ + 2 * y_ + c_

        def peer(k):
            return (x_ ^ ((k >> 2) & 1), y_ ^ ((k >> 1) & 1), c_ ^ (k & 1))

        def copy(a, k, sending):
            px, py, pc = peer(k)
            theirs = 4 * px + 2 * py + pc
            return pltpu.make_async_remote_copy(
                src_ref=g_refs[a].at[theirs if sending else me], dst_ref=out_refs[a].at[me if sending else theirs],
                send_sem=send_sems.at[a, k - 1], recv_sem=recv_sems.at[a, k - 1],
                device_id=(px, py, pc), device_id_type=pl.DeviceIdType.MESH)

        mine = [pltpu.make_async_copy(g_refs[a].at[me], out_refs[a].at[me], local_sems.at[a]) for a in range(n)]
        for cp in mine:
            cp.start()
        sends = [copy(a, k, True) for a in range(n) for k in range(1, N_DEV)]
        for cp in sends:
            cp.start()
        for a in range(n):
            for k in range(1, N_DEV):
                copy(a, k, False).wait_recv()
        for cp in sends:
            cp.wait_send()
        for cp in mine:
            cp.wait()

    return _comm_call(body, name, gs, [jax.ShapeDtypeStruct(g.shape, g.dtype) for g in gs])


def _mm(a, b, mode, out_dtype, name, res=None, gvec=None, tm=1024, tn=512, tk=1024):
    m, k = a.shape
    n = b.shape[1] if mode == "nn" else b.shape[0]
    tm, tn, tk = _tile(m, tm), _tile(n, tn), _tile(k, tk)
    nk = k // tk
    dims = (((1,), (0,)), ((), ())) if mode == "nn" else (((1,), (1,)), ((), ()))
    fused = res is not None

    def body(*refs):
        if fused:
            a_ref, b_ref, res_ref, g_ref, o_ref, raw_ref, acc_ref = refs
        else:
            a_ref, b_ref, o_ref, acc_ref = refs
        kk = pl.program_id(2)
        part = lax.dot_general(a_ref[...], b_ref[...], dims, preferred_element_type=F32)

        @pl.when(kk == 0)
        def _():
            acc_ref[...] = part

        @pl.when(kk > 0)
        def _():
            acc_ref[...] += part

        @pl.when(kk == nk - 1)
        def _():
            acc = acc_ref[...]
            if fused:
                raw_ref[...] = acc
                o_ref[...] = (res_ref[...] + g_ref[...] * acc).astype(o_ref.dtype)
            else:
                o_ref[...] = acc.astype(o_ref.dtype)

    a_spec = pl.BlockSpec((tm, tk), lambda i, j, kk: (i, kk))
    if mode == "nn":
        b_spec = pl.BlockSpec((tk, tn), lambda i, j, kk: (kk, j))
    else:
        b_spec = pl.BlockSpec((tn, tk), lambda i, j, kk: (j, kk))
    o_spec = pl.BlockSpec((tm, tn), lambda i, j, kk: (i, j))
    in_specs, args = [a_spec, b_spec], [a, b]
    out_specs, out_shape = o_spec, jax.ShapeDtypeStruct((m, n), out_dtype)
    if fused:
        in_specs += [o_spec, pl.BlockSpec((1, tn), lambda i, j, kk: (0, j))]
        args += [res, gvec]
        out_specs = (o_spec, o_spec)
        out_shape = (out_shape, jax.ShapeDtypeStruct((m, n), F32))
    return pl.pallas_call(
        body, name=name, grid=(m // tm, n // tn, nk),
        in_specs=in_specs, out_specs=out_specs, out_shape=out_shape,
        scratch_shapes=[pltpu.VMEM((tm, tn), F32)],
        compiler_params=_params(("parallel", "parallel", "arbitrary")),
    )(*args)


def _rms_mod(x, g, sc, sh, name):
    s, w = x.shape
    tm = _tile(s, ROW_T)

    def body(x_ref, g_ref, sc_ref, sh_ref, o_ref):
        xv = x_ref[...]
        r = lax.rsqrt(jnp.mean(xv * xv, axis=-1, keepdims=True) + EPS)
        o_ref[...] = ((xv * r * g_ref[...]) * (1.0 + sc_ref[...]) + sh_ref[...]).astype(o_ref.dtype)

    row = pl.BlockSpec((tm, w), lambda i: (i, 0))
    return pl.pallas_call(
        body, name=name, grid=(s // tm,),
        in_specs=[row, _vec_spec(w, 1), _vec_spec(w, 1), _vec_spec(w, 1)],
        out_specs=row, out_shape=jax.ShapeDtypeStruct((s, w), BF16),
        compiler_params=_params(("parallel",)),
    )(x, g, sc, sh)


def _rms_mod_bwd(dh, x, g, sc, dres, name):
    s, w = x.shape
    tm = _tile(s, ROW_T)
    has_res = dres is not None

    def body(*refs):
        if has_res:
            dh_ref, x_ref, g_ref, sc_ref, dres_ref, dx_ref, sums_ref = refs
        else:
            dh_ref, x_ref, g_ref, sc_ref, dx_ref, sums_ref = refs
        xv, dhv, gv = x_ref[...], dh_ref[...], g_ref[...]
        r = lax.rsqrt(jnp.mean(xv * xv, axis=-1, keepdims=True) + EPS)
        xhat = xv * r
        dxn = dhv * (1.0 + sc_ref[...])
        dxhat = dxn * gv
        dx = r * (dxhat - xhat * jnp.mean(dxhat * xhat, axis=-1, keepdims=True))
        if has_res:
            dx = dx + dres_ref[...]
        dx_ref[...] = dx

        @pl.when(pl.program_id(0) == 0)
        def _():
            sums_ref[...] = jnp.zeros_like(sums_ref)

        sums_ref[0:1, :] += jnp.sum(dhv, axis=0, keepdims=True)
        sums_ref[1:2, :] += jnp.sum(dhv * (xhat * gv), axis=0, keepdims=True)
        sums_ref[2:3, :] += jnp.sum(dxn * xhat, axis=0, keepdims=True)

    row = pl.BlockSpec((tm, w), lambda i: (i, 0))
    in_specs = [row, row, _vec_spec(w, 1), _vec_spec(w, 1)] + ([row] if has_res else [])
    args = [dh, x, g, sc] + ([dres] if has_res else [])
    return pl.pallas_call(
        body, name=name, grid=(s // tm,),
        in_specs=in_specs,
        out_specs=(row, pl.BlockSpec((8, w), lambda i: (0, 0))),
        out_shape=(jax.ShapeDtypeStruct((s, w), F32), jax.ShapeDtypeStruct((8, w), F32)),
        compiler_params=_params(("arbitrary",)),
    )(*args)


def _scale_bwd(dx, val, gvec, name):
    s, w = dx.shape
    tm = _tile(s, ROW_T)

    def body(dx_ref, val_ref, g_ref, d_ref, sums_ref):
        dxv = dx_ref[...]
        d_ref[...] = (dxv * g_ref[...]).astype(d_ref.dtype)

        @pl.when(pl.program_id(0) == 0)
        def _():
            sums_ref[...] = jnp.zeros_like(sums_ref)

        sums_ref[0:1, :] += jnp.sum(dxv * val_ref[...], axis=0, keepdims=True)

    row = pl.BlockSpec((tm, w), lambda i: (i, 0))
    return pl.pallas_call(
        body, name=name, grid=(s // tm,),
        in_specs=[row, row, _vec_spec(w, 1)],
        out_specs=(row, pl.BlockSpec((8, w), lambda i: (0, 0))),
        out_shape=(jax.ShapeDtypeStruct((s, w), BF16), jax.ShapeDtypeStruct((8, w), F32)),
        compiler_params=_params(("arbitrary",)),
    )(dx, val, gvec)


def _final_loss(x3, target, g, name):
    s, w = x3.shape
    tm = _tile(s, ROW_T)

    def body(x_ref, t_ref, g_ref, dx_ref, sums_ref):
        xv, gv = x_ref[...], g_ref[...]
        r = lax.rsqrt(jnp.mean(xv * xv, axis=-1, keepdims=True) + EPS)
        xhat = xv * r
        err = xhat * gv - t_ref[...]
        dy = err * (1.0 / w)
        dxhat = dy * gv
        dx_ref[...] = r * (dxhat - xhat * jnp.mean(dxhat * xhat, axis=-1, keepdims=True))

        @pl.when(pl.program_id(0) == 0)
        def _():
            sums_ref[...] = jnp.zeros_like(sums_ref)

        sums_ref[0:1, :] += jnp.sum(dy * xhat, axis=0, keepdims=True)
        sums_ref[1:2, :] += jnp.zeros((1, w), F32) + (0.5 / w) * jnp.sum(err * err)

    row = pl.BlockSpec((tm, w), lambda i: (i, 0))
    return pl.pallas_call(
        body, name=name, grid=(s // tm,),
        in_specs=[row, row, _vec_spec(w, 1)],
        out_specs=(row, pl.BlockSpec((8, w), lambda i: (0, 0))),
        out_shape=(jax.ShapeDtypeStruct((s, w), F32), jax.ShapeDtypeStruct((8, w), F32)),
        compiler_params=_params(("arbitrary",)),
    )(x3, target, g)


def _rope_block(seg, cmul, smul):
    lane = lax.broadcasted_iota(jnp.int32, seg.shape, 1)
    swapped = jnp.where(lane < MLA_NOPE + MLA_ROPE // 2,
                        pltpu.roll(seg, HEAD_PAD - MLA_ROPE // 2, 1), pltpu.roll(seg, MLA_ROPE // 2, 1))
    return seg * cmul + swapped * smul


def _rope(t, cmul, smul, name, n_rot, out_dtypes):
    s, w = t.shape
    tm = _tile(s, ROW_T)
    n_out = len(out_dtypes)
    wo = w // n_out

    def body(t_ref, c_ref, s_ref, *o_refs):
        cv, sv = c_ref[...], s_ref[...]
        one = (lax.broadcasted_iota(jnp.int32, (tm, HEAD_PAD), 1) == SUM_LANE).astype(F32)
        for hb in range(w // HEAD_PAD):
            seg = t_ref[:, hb * HEAD_PAD:(hb + 1) * HEAD_PAD]
            if hb < n_rot:
                seg = _rope_block(seg, cv, sv)
            else:
                seg = seg + one
            o_ref = o_refs[(hb * HEAD_PAD) // wo]
            col = (hb * HEAD_PAD) % wo
            o_ref[:, col:col + HEAD_PAD] = seg.astype(o_ref.dtype)

    row = pl.BlockSpec((tm, w), lambda i: (i, 0))
    tab = pl.BlockSpec((tm, HEAD_PAD), lambda i: (i, 0))
    orow = pl.BlockSpec((tm, wo), lambda i: (i, 0))
    outs = pl.pallas_call(
        body, name=name, grid=(s // tm,),
        in_specs=[row, tab, tab],
        out_specs=tuple(orow for _ in out_dtypes),
        out_shape=tuple(jax.ShapeDtypeStruct((s, wo), dt) for dt in out_dtypes),
        compiler_params=_params(("parallel",)),
    )(t, cmul, smul)
    return outs


def _rope_bwd_kv(dk, dv, cmul, smul, name):
    s, w = dk.shape
    tm = _tile(s, ROW_T)

    def body(dk_ref, dv_ref, c_ref, s_ref, o_ref):
        cv, sv = c_ref[...], s_ref[...]
        for hb in range(N_HEADS):
            lo, hi = hb * HEAD_PAD, (hb + 1) * HEAD_PAD
            o_ref[:, lo:hi] = _rope_block(dk_ref[:, lo:hi], cv, sv).astype(o_ref.dtype)
        o_ref[:, w:2 * w] = dv_ref[...].astype(o_ref.dtype)

    row = pl.BlockSpec((tm, w), lambda i: (i, 0))
    tab = pl.BlockSpec((tm, HEAD_PAD), lambda i: (i, 0))
    return pl.pallas_call(
        body, name=name, grid=(s // tm,),
        in_specs=[row, row, tab, tab],
        out_specs=pl.BlockSpec((tm, 2 * w), lambda i: (i, 0)),
        out_shape=jax.ShapeDtypeStruct((s, 2 * w), BF16),
        compiler_params=_params(("parallel",)),
    )(dk, dv, cmul, smul)


def _lanes(col, width):
    if col.shape[1] == 1:
        col = jnp.broadcast_to(col, (col.shape[0], HEAD_PAD))
    return jnp.tile(col, (1, width // HEAD_PAD))


def _fold_lanes(a):
    out = a[:, 0:HEAD_PAD]
    for g in range(1, a.shape[1] // HEAD_PAD):
        out = out + a[:, g * HEAD_PAD:(g + 1) * HEAD_PAD]
    return out


def _causal(t, rows_are_queries):
    row = lax.broadcasted_iota(jnp.int32, (t, t), 0)
    col = lax.broadcasted_iota(jnp.int32, (t, t), 1)
    return row >= col if rows_are_queries else col >= row


def _attn_fwd(q, k, v, fcol, frow, name):
    s = q.shape[0]
    t = ATT_T
    nq = s // t
    use_f = fcol is not None

    def body(*refs):
        if use_f:
            q_ref, k_ref, v_ref, fc_ref, fr_ref, o_ref, ob_ref, lse_ref, m_s, acc_s = refs
            fc_b = jnp.broadcast_to(fc_ref[0], (t, HEAD_PAD))
        else:
            q_ref, k_ref, v_ref, o_ref, ob_ref, lse_ref, m_s, acc_s = refs
        qi = pl.program_id(1)
        qv = q_ref[...]
        m_s[...] = jnp.full(m_s.shape, NEG_BIG, F32)
        acc_s[...] = jnp.zeros(acc_s.shape, F32)

        def step(j, masked):
            off = pl.multiple_of(j * t, t)
            kv = k_ref[pl.ds(off, t), :]
            vv = v_ref[pl.ds(off, t), :]
            sc = lax.dot_general(qv, kv, (((1,), (1,)), ((), ())), preferred_element_type=F32)
            if use_f:
                sc = sc + (_lanes(fc_b, t) - fr_ref[0, j])
            if masked:
                sc = jnp.where(_causal(t, True), sc, NEG_BIG)
            m_prev = m_s[...]
            m_new = jnp.maximum(m_prev, jnp.max(sc, axis=-1, keepdims=True))
            p = jnp.exp2(sc - _lanes(m_new, t))
            acc_s[...] = jnp.exp2(m_prev - m_new) * acc_s[...] + jnp.dot(p.astype(BF16), vv,
                                                                         preferred_element_type=F32)
            m_s[...] = m_new

        def loop_body(j, carry):
            step(j, False)
            return carry

        lax.fori_loop(0, qi, loop_body, 0)
        step(qi, True)
        acc = acc_s[...]
        lane = lax.broadcasted_iota(jnp.int32, acc.shape, 1)
        denom = jnp.sum(jnp.where(lane == SUM_LANE, acc, 0.0), axis=-1, keepdims=True)
        o = acc * (1.0 / denom)
        o_ref[...] = o
        ob_ref[...] = o.astype(BF16)
        lse_ref[0] = jnp.max(m_s[...], axis=-1, keepdims=True) + jnp.log(denom) * LOG2E

    qspec = pl.BlockSpec((t, HEAD_PAD), lambda h, i: (i, h))
    kspec = pl.BlockSpec((s, HEAD_PAD), lambda h, i: (0, h))
    colspec = pl.BlockSpec((1, t, 1), lambda h, i: (h, i, 0))
    in_specs, args = [qspec, kspec, kspec], [q, k, v]
    if use_f:
        in_specs += [colspec, pl.BlockSpec((1, nq, 1, t), lambda h, i: (h, 0, 0, 0))]
        args += [fcol, frow]
    return pl.pallas_call(
        body, name=name, grid=(N_HEADS, nq),
        in_specs=in_specs,
        out_specs=(qspec, qspec, colspec),
        out_shape=(jax.ShapeDtypeStruct((s, N_HEADS * HEAD_PAD), F32), jax.ShapeDtypeStruct((s, N_HEADS * HEAD_PAD), BF16),
                   jax.ShapeDtypeStruct((N_HEADS, s, 1), F32)),
        scratch_shapes=[pltpu.VMEM((t, HEAD_PAD), F32), pltpu.VMEM((t, HEAD_PAD), F32)],
        compiler_params=_params(("parallel", "arbitrary")),
    )(*args)


def _attn_delta(o, do, name):
    s, w = o.shape
    tm = _tile(s, ROW_T)

    def body(o_ref, do_ref, d_ref):
        for hb in range(N_HEADS):
            lo, hi = hb * HEAD_PAD, (hb + 1) * HEAD_PAD
            d_ref[hb] = jnp.sum(o_ref[:, lo:hi] * do_ref[:, lo:hi].astype(F32), axis=-1, keepdims=True)

    row = pl.BlockSpec((tm, w), lambda i: (i, 0))
    return pl.pallas_call(
        body, name=name, grid=(s // tm,),
        in_specs=[row, row],
        out_specs=pl.BlockSpec((N_HEADS, tm, 1), lambda i: (0, i, 0)),
        out_shape=jax.ShapeDtypeStruct((N_HEADS, s, 1), F32),
        compiler_params=_params(("parallel",)),
    )(o, do)


def _attn_bwd_dq(q, k, v, do, lse, delta, fcol, frow, scale, name, out_dtype=F32):
    s = q.shape[0]
    t = ATT_T
    nq = s // t
    use_f = fcol is not None

    def body(*refs):
        if use_f:
            q_ref, k_ref, v_ref, do_ref, lse_ref, dl_ref, fc_ref, fr_ref, dq_ref, dr_ref, acc_s, dr_s = refs
            dr_s[...] = jnp.zeros(dr_s.shape, F32)
            fc_b = jnp.broadcast_to(fc_ref[0], (t, HEAD_PAD))
        else:
            q_ref, k_ref, v_ref, do_ref, lse_ref, dl_ref, dq_ref, acc_s = refs
        qi = pl.program_id(1)
        qv, dov = q_ref[...], do_ref[...]
        lse_b = jnp.broadcast_to(lse_ref[0], (t, HEAD_PAD))
        dl_b = jnp.broadcast_to(dl_ref[0], (t, HEAD_PAD))
        acc_s[...] = jnp.zeros(acc_s.shape, F32)

        def step(j, masked):
            off = pl.multiple_of(j * t, t)
            kv = k_ref[pl.ds(off, t), :]
            vv = v_ref[pl.ds(off, t), :]
            sc = lax.dot_general(qv, kv, (((1,), (1,)), ((), ())), preferred_element_type=F32)
            if use_f:
                sc = sc + (_lanes(fc_b, t) - fr_ref[0, j])
            if masked:
                sc = jnp.where(_causal(t, True), sc, NEG_BIG)
            p = jnp.exp2(sc - _lanes(lse_b, t))
            dp = lax.dot_general(dov, vv, (((1,), (1,)), ((), ())), preferred_element_type=F32)
            ds = p * (dp - _lanes(dl_b, t))
            acc_s[...] += jnp.dot(ds.astype(BF16), kv, preferred_element_type=F32)
            if use_f:
                dr_s[...] += _fold_lanes(ds)

        def loop_body(j, carry):
            step(j, False)
            return carry

        lax.fori_loop(0, qi, loop_body, 0)
        step(qi, True)
        dq_ref[...] = (acc_s[...] * scale).astype(dq_ref.dtype)
        if use_f:
            dr_ref[0] = jnp.sum(dr_s[...], axis=-1, keepdims=True)

    qspec = pl.BlockSpec((t, HEAD_PAD), lambda h, i: (i, h))
    kspec = pl.BlockSpec((s, HEAD_PAD), lambda h, i: (0, h))
    colspec = pl.BlockSpec((1, t, 1), lambda h, i: (h, i, 0))
    in_specs, args = [qspec, kspec, kspec, qspec, colspec, colspec], [q, k, v, do, lse, delta]
    out_specs, out_shape = qspec, jax.ShapeDtypeStruct((s, N_HEADS * HEAD_PAD), out_dtype)
    scratch = [pltpu.VMEM((t, HEAD_PAD), F32)]
    if use_f:
        in_specs += [colspec, pl.BlockSpec((1, nq, 1, t), lambda h, i: (h, 0, 0, 0))]
        args += [fcol, frow]
        out_specs, out_shape = (qspec, colspec), (out_shape, jax.ShapeDtypeStruct((N_HEADS, s, 1), F32))
        scratch += [pltpu.VMEM((t, HEAD_PAD), F32)]
    return pl.pallas_call(
        body, name=name, grid=(N_HEADS, nq),
        in_specs=in_specs, out_specs=out_specs, out_shape=out_shape,
        scratch_shapes=scratch,
        compiler_params=_params(("parallel", "arbitrary")),
    )(*args)


def _attn_bwd_dkv(q, k, v, do, lse_row, delta_row, fcol, frow, scale, name, out_dtype=F32):
    s = q.shape[0]
    t = ATT_T
    nq = s // t
    use_f = fcol is not None

    def body(*refs):
        if use_f:
            q_ref, k_ref, v_ref, do_ref, lse_ref, dl_ref, fc_ref, fr_ref, dk_ref, dv_ref, df_ref, dk_s, dv_s, df_s = refs
        else:
            q_ref, k_ref, v_ref, do_ref, lse_ref, dl_ref, dk_ref, dv_ref, dk_s, dv_s = refs
        kj = pl.program_id(1)
        kv, vv = k_ref[...], v_ref[...]
        dk_s[...] = jnp.zeros(dk_s.shape, F32)
        dv_s[...] = jnp.zeros(dv_s.shape, F32)
        if use_f:
            df_s[...] = jnp.zeros(df_s.shape, F32)
            fc_b = jnp.broadcast_to(fc_ref[0], (t, HEAD_PAD))

        def step(i, masked):
            off = pl.multiple_of(i * t, t)
            qv = q_ref[pl.ds(off, t), :]
            dov = do_ref[pl.ds(off, t), :]
            st = lax.dot_general(kv, qv, (((1,), (1,)), ((), ())), preferred_element_type=F32)
            if use_f:
                st = st + (fr_ref[0, i] - _lanes(fc_b, t))
            if masked:
                st = jnp.where(_causal(t, False), st, NEG_BIG)
            pt = jnp.exp2(st - lse_ref[0, i])
            dv_s[...] += jnp.dot(pt.astype(BF16), dov, preferred_element_type=F32)
            dpt = lax.dot_general(vv, dov, (((1,), (1,)), ((), ())), preferred_element_type=F32)
            dst = pt * (dpt - dl_ref[0, i])
            dk_s[...] += jnp.dot(dst.astype(BF16), qv, preferred_element_type=F32)
            if use_f:
                df_s[...] -= _fold_lanes(dst)

        step(kj, True)

        def loop_body(i, carry):
            step(i, False)
            return carry

        lax.fori_loop(kj + 1, nq, loop_body, 0)
        dk_ref[...] = (dk_s[...] * scale).astype(dk_ref.dtype)
        dv_ref[...] = dv_s[...].astype(dv_ref.dtype)
        if use_f:
            df_ref[0] = jnp.sum(df_s[...], axis=-1, keepdims=True)

    kspec = pl.BlockSpec((t, HEAD_PAD), lambda h, j: (j, h))
    qspec = pl.BlockSpec((s, HEAD_PAD), lambda h, j: (0, h))
    rowspec = pl.BlockSpec((1, nq, 1, t), lambda h, j: (h, 0, 0, 0))
    colspec = pl.BlockSpec((1, t, 1), lambda h, j: (h, j, 0))
    in_specs, args = [qspec, kspec, kspec, qspec, rowspec, rowspec], [q, k, v, do, lse_row, delta_row]
    out_specs = [kspec, kspec]
    out_shape = [jax.ShapeDtypeStruct((s, N_HEADS * HEAD_PAD), out_dtype)] * 2
    scratch = [pltpu.VMEM((t, HEAD_PAD), F32), pltpu.VMEM((t, HEAD_PAD), F32)]
    if use_f:
        in_specs += [colspec, rowspec]
        args += [fcol, frow]
        out_specs += [colspec]
        out_shape += [jax.ShapeDtypeStruct((N_HEADS, s, 1), F32)]
        scratch += [pltpu.VMEM((t, HEAD_PAD), F32)]
    return pl.pallas_call(
        body, name=name, grid=(N_HEADS, nq),
        in_specs=in_specs, out_specs=tuple(out_specs), out_shape=tuple(out_shape),
        scratch_shapes=scratch,
        compiler_params=_params(("parallel", "arbitrary")),
    )(*args)


def _gate_fwd(pm, pf, proj, name):
    s, w = pm.shape
    tm = _tile(s, ROW_T)

    def body(pm_ref, pf_ref, gm_ref, gf_ref, y_ref):
        y = jax.nn.sigmoid(gm_ref[...]) * pm_ref[...] + jax.nn.sigmoid(gf_ref[...]) * pf_ref[...]
        y_ref[...] = y.astype(y_ref.dtype)

    row = pl.BlockSpec((tm, w), lambda i: (i, 0))
    return pl.pallas_call(
        body, name=name, grid=(s // tm,),
        in_specs=[row, row, pl.BlockSpec((tm, w), lambda i: (i, P_GM // D_MODEL)),
                  pl.BlockSpec((tm, w), lambda i: (i, P_GF // D_MODEL))],
        out_specs=row, out_shape=jax.ShapeDtypeStruct((s, w), BF16),
        compiler_params=_params(("parallel",)),
    )(pm, pf, proj, proj)


def _gate_bwd(dy, pm, pf, proj, name):
    s, w = pm.shape
    tm = _tile(s, ROW_T)

    def body(dy_ref, pm_ref, pf_ref, gm_ref, gf_ref, dpm_ref, dpf_ref, dgm_ref, dgf_ref):
        dyv = dy_ref[...]
        sm, sf = jax.nn.sigmoid(gm_ref[...]), jax.nn.sigmoid(gf_ref[...])
        dpm_ref[...] = (dyv * sm).astype(BF16)
        dpf_ref[...] = (dyv * sf).astype(BF16)
        dgm_ref[...] = (dyv * pm_ref[...] * (sm * (1.0 - sm))).astype(BF16)
        dgf_ref[...] = (dyv * pf_ref[...] * (sf * (1.0 - sf))).astype(BF16)

    row = pl.BlockSpec((tm, w), lambda i: (i, 0))
    out = jax.ShapeDtypeStruct((s, w), BF16)
    return pl.pallas_call(
        body, name=name, grid=(s // tm,),
        in_specs=[row, row, row, pl.BlockSpec((tm, w), lambda i: (i, P_GM // D_MODEL)),
                  pl.BlockSpec((tm, w), lambda i: (i, P_GF // D_MODEL))],
        out_specs=(row, row, row, row), out_shape=(out, out, out, out),
        compiler_params=_params(("parallel",)),
    )(dy, pm, pf, proj, proj)


CONV_TN = 256
CONV_TM = 512
HALO = BF16_ROWS


def _shift_down(u, prev, n):
    rolled = pltpu.roll(u, n, 0)
    prev_rolled = pltpu.roll(prev, n, 0)
    top = jnp.concatenate([prev_rolled, rolled[HALO:]], axis=0)
    row = lax.broadcasted_iota(jnp.int32, u.shape, 0)
    return jnp.where(row < n, top, rolled)


def _conv_tile(u, prev, w_ref, b_ref):
    um1 = _shift_down(u, prev, 1)
    um2 = _shift_down(u, prev, 2)
    uc = b_ref[...] + w_ref[0:1, :] * um2 + w_ref[1:2, :] * um1 + w_ref[2:3, :] * u
    return uc, um1, um2


def _conv_specs(tm, tn, ncol_off):
    blk = lambda off: pl.BlockSpec((tm, tn), lambda j, i: (i, j + off))
    halo = lambda off: pl.BlockSpec((HALO, tn), lambda j, i: (jnp.maximum(i * (tm // HALO) - 1, 0), j + off))
    wsp = lambda off: pl.BlockSpec((3, tn), lambda j, i: (0, j + off))
    bsp = lambda off: pl.BlockSpec((1, tn), lambda j, i: (0, j + off))
    return blk, halo, wsp, bsp


def _convglu_fwd(u, conv_w, conv_b, name):
    s = u.shape[0]
    tm, tn = _tile(s, CONV_TM), CONV_TN
    nj = D_FF // tn
    blk, halo, wsp, bsp = _conv_specs(tm, tn, nj)

    def body(ug_ref, pg_ref, uv_ref, pv_ref, wg_ref, wv_ref, bg_ref, bv_ref, a_ref):
        live = (pl.program_id(1) > 0).astype(F32)
        gate, _, _ = _conv_tile(ug_ref[...].astype(F32), pg_ref[...].astype(F32) * live, wg_ref, bg_ref)
        val, _, _ = _conv_tile(uv_ref[...].astype(F32), pv_ref[...].astype(F32) * live, wv_ref, bv_ref)
        a_ref[...] = (gate * jax.nn.sigmoid(gate) * val).astype(a_ref.dtype)

    return pl.pallas_call(
        body, name=name, grid=(nj, s // tm),
        in_specs=[blk(0), halo(0), blk(nj), halo(nj), wsp(0), wsp(nj), bsp(0), bsp(nj)],
        out_specs=blk(0), out_shape=jax.ShapeDtypeStruct((s, D_FF), BF16),
        compiler_params=_params(("parallel", "arbitrary")),
    )(u, u, u, u, conv_w, conv_w, conv_b, conv_b)


def _convglu_bwd(da, u, conv_w, conv_b, name):
    s = u.shape[0]
    tm, tn = _tile(s, CONV_TM), CONV_TN
    nj = D_FF // tn
    blk, halo, wsp, bsp = _conv_specs(tm, tn, nj)

    def body(da_ref, ug_ref, pg_ref, uv_ref, pv_ref, wg_ref, wv_ref, bg_ref, bv_ref,
             dg_ref, dv_ref, sg_ref, sv_ref):
        live = (pl.program_id(1) > 0).astype(F32)
        ug, uv = ug_ref[...].astype(F32), uv_ref[...].astype(F32)
        gate, ug1, ug2 = _conv_tile(ug, pg_ref[...].astype(F32) * live, wg_ref, bg_ref)
        val, uv1, uv2 = _conv_tile(uv, pv_ref[...].astype(F32) * live, wv_ref, bv_ref)
        dav = da_ref[...].astype(F32)
        sig = jax.nn.sigmoid(gate)
        dgate = dav * val * (sig * (1.0 + gate * (1.0 - sig)))
        dval = dav * (gate * sig)
        dg_ref[...] = dgate.astype(dg_ref.dtype)
        dv_ref[...] = dval.astype(dv_ref.dtype)

        @pl.when(pl.program_id(1) == 0)
        def _():
            sg_ref[...] = jnp.zeros_like(sg_ref)
            sv_ref[...] = jnp.zeros_like(sv_ref)

        for s_ref, d, taps in ((sg_ref, dgate, (ug2, ug1, ug)), (sv_ref, dval, (uv2, uv1, uv))):
            for r, tap in enumerate(taps):
                s_ref[r:r + 1, :] += jnp.sum(d * tap, axis=0, keepdims=True)
            s_ref[3:4, :] += jnp.sum(d, axis=0, keepdims=True)

    sums = lambda off: pl.BlockSpec((8, tn), lambda j, i: (0, j + off))
    return pl.pallas_call(
        body, name=name, grid=(nj, s // tm),
        in_specs=[blk(0), blk(0), halo(0), blk(nj), halo(nj), wsp(0), wsp(nj), bsp(0), bsp(nj)],
        out_specs=(blk(0), blk(0), sums(0), sums(0)),
        out_shape=(jax.ShapeDtypeStruct((s, D_FF), BF16), jax.ShapeDtypeStruct((s, D_FF), BF16),
                   jax.ShapeDtypeStruct((8, D_FF), F32), jax.ShapeDtypeStruct((8, D_FF), F32)),
        compiler_params=_params(("parallel", "arbitrary")),
    )(da, u, u, u, u, conv_w, conv_w, conv_b, conv_b)


def _conv_transpose(d, conv_w_half, name):
    s, w = d.shape
    tm, tn = _tile(s, CONV_TM), CONV_TN
    last = s // tm - 1

    def body(d_ref, nx_ref, w_ref, o_ref):
        dv = d_ref[...].astype(F32)
        nxt = nx_ref[...].astype(F32) * (pl.program_id(1) < last).astype(F32)
        row = lax.broadcasted_iota(jnp.int32, dv.shape, 0)

        def shift_up(n):
            rolled = pltpu.roll(dv, tm - n, 0)
            nxt_rolled = pltpu.roll(nxt, HALO - n, 0)
            bottom = jnp.concatenate([rolled[:tm - HALO], nxt_rolled], axis=0)
            return jnp.where(row >= tm - n, bottom, rolled)

        out = w_ref[2:3, :] * dv + w_ref[1:2, :] * shift_up(1) + w_ref[0:1, :] * shift_up(2)
        o_ref[...] = out.astype(o_ref.dtype)

    blk = pl.BlockSpec((tm, tn), lambda j, i: (i, j))
    nxt_spec = pl.BlockSpec((HALO, tn), lambda j, i: (jnp.minimum((i + 1) * (tm // HALO), s // HALO - 1), j))
    return pl.pallas_call(
        body, name=name, grid=(w // tn, s // tm),
        in_specs=[blk, nxt_spec, pl.BlockSpec((3, tn), lambda j, i: (0, j))],
        out_specs=blk, out_shape=jax.ShapeDtypeStruct((s, w), BF16),
        compiler_params=_params(("parallel", "arbitrary")),
    )(d, d, conv_w_half)


def _split3(a):
    a1 = a.astype(BF16)
    r1 = a - a1.astype(F32)
    a2 = r1.astype(BF16)
    a3 = (r1 - a2.astype(F32)).astype(BF16)
    return a1, a2, a3


def _ones_dot_right(a, mat):
    return sum(jnp.dot(p, mat, preferred_element_type=F32) for p in _split3(a))


def _ones_dot_left(mat, a):
    return sum(jnp.dot(mat, p, preferred_element_type=F32) for p in _split3(a))


def _tri(n, cmp):
    r = lax.broadcasted_iota(jnp.int32, (n, n), 0)
    c = lax.broadcasted_iota(jnp.int32, (n, n), 1)
    return cmp(r, c).astype(BF16)


def _forget_fwd(z, bias, name):
    nh, nr, nl = z.shape

    def body(z_ref, b_ref, f_ref):
        within = _tri(nl, lambda r, c: r <= c)
        before = _tri(nr, lambda r, c: c < r)
        for h in range(nh):
            x = z_ref[h] + b_ref[h]
            lf = jnp.minimum(x, 0.0) - jnp.log(1.0 + jnp.exp(-jnp.abs(x)))
            pre = _ones_dot_right(lf, within)
            tot = jnp.zeros((nr, nl), F32) + jnp.sum(lf, axis=1, keepdims=True)
            f_ref[h] = pre + _ones_dot_left(before, tot)

    return pl.pallas_call(
        body, name=name, out_shape=jax.ShapeDtypeStruct(z.shape, F32),
        compiler_params=pltpu.CompilerParams(vmem_limit_bytes=VMEM_LIMIT_BYTES),
    )(z, bias)


def _forget_bwd(df_rows, df_cols, z, bias, name):
    nh, nr, nl = z.shape

    def body(dfr_ref, dfc_ref, z_ref, b_ref, dz_ref, db_ref):
        within = _tri(nl, lambda r, c: r >= c)
        after = _tri(nr, lambda r, c: c > r)
        for h in range(nh):
            g = dfr_ref[h] + dfc_ref[h]
            suf = _ones_dot_right(g, within)
            tot = jnp.zeros((nr, nl), F32) + jnp.sum(g, axis=1, keepdims=True)
            dlf = suf + _ones_dot_left(after, tot)
            dz = dlf * jax.nn.sigmoid(-(z_ref[h] + b_ref[h]))
            dz_ref[h] = dz
            db_ref[h] = jnp.zeros((1, nl), F32) + jnp.sum(dz)

    return pl.pallas_call(
        body, name=name,
        out_shape=(jax.ShapeDtypeStruct(z.shape, F32), jax.ShapeDtypeStruct(bias.shape, F32)),
        compiler_params=pltpu.CompilerParams(vmem_limit_bytes=VMEM_LIMIT_BYTES),
    )(df_rows, df_cols, z, bias)


def _ada_fwd(c_col, w, b, name):
    kdim, n = w.shape

    def body(c_ref, w_ref, b_ref, ada_ref, act_ref):
        wv = w_ref[...]
        for e in range(N_DEV):
            cv = c_ref[e]
            act = cv * jax.nn.sigmoid(cv)
            act_ref[e] = act
            ada_ref[e:e + 1, :] = jnp.sum(act * wv, axis=0, keepdims=True) + b_ref[...]

    return pl.pallas_call(
        body, name=name,
        out_shape=(jax.ShapeDtypeStruct((N_DEV, n), F32), jax.ShapeDtypeStruct((N_DEV, kdim, 1), F32)),
        compiler_params=pltpu.CompilerParams(vmem_limit_bytes=VMEM_LIMIT_BYTES),
    )(c_col, w, b)


def _ada_bwd(act_col, dada, name):
    kdim = act_col.shape[1]
    n = dada.shape[1]

    def body(act_ref, d_ref, g_ref):
        acc = act_ref[0] * d_ref[0:1, :]
        for e in range(1, N_DEV):
            acc = acc + act_ref[e] * d_ref[e:e + 1, :]
        g_ref[...] = acc

    return pl.pallas_call(
        body, name=name, out_shape=jax.ShapeDtypeStruct((kdim, n), F32),
        compiler_params=pltpu.CompilerParams(vmem_limit_bytes=VMEM_LIMIT_BYTES),
    )(act_col, dada)


def _adamw(parts, w, m, v, name, tr=128):
    npart, r, c = parts.shape
    tr = _tile(r, tr, step=BF16_ROWS) if r % BF16_ROWS == 0 else r

    def body(p_ref, w_ref, m_ref, v_ref, g_ref, d_ref, nm_ref, nv_ref):
        g = p_ref[0].astype(F32)
        for e in range(1, npart):
            g = g + p_ref[e].astype(F32)
        nm = ADAM_B1 * m_ref[...] + (1.0 - ADAM_B1) * g
        nv = ADAM_B2 * v_ref[...] + (1.0 - ADAM_B2) * (g * g)
        m_hat = nm / (1.0 - ADAM_B1 ** ADAM_STEP)
        v_hat = nv / (1.0 - ADAM_B2 ** ADAM_STEP)
        g_ref[...] = g
        d_ref[...] = -ADAM_LR * (m_hat / (jnp.sqrt(v_hat) + ADAM_EPS) + ADAM_WD * w_ref[...])
        nm_ref[...] = nm
        nv_ref[...] = nv

    row = pl.BlockSpec((tr, c), lambda i: (i, 0))
    out = jax.ShapeDtypeStruct((r, c), F32)
    return pl.pallas_call(
        body, name=name, grid=(r // tr,),
        in_specs=[pl.BlockSpec((npart, tr, c), lambda i: (0, i, 0)), row, row, row],
        out_specs=(row, row, row, row), out_shape=(out, out, out, out),
        compiler_params=_params(("parallel",)),
    )(parts, w, m, v)


BIG = ("w_in", "w_uq", "w_ukv", "w_o_mla", "w_o_fox", "w_out", "w_up", "conv_w", "w_down")


def _cols_to_full(stack):
    n, r, c = stack.shape
    return stack.transpose(1, 0, 2).reshape(r, n * c)


def _full_to_cols(full, c):
    r = full.shape[0]
    return full.reshape(r, N_DEV, c).transpose(1, 0, 2)


def _pad_heads(a, width, ones_lane=False):
    s = a.shape[0]
    a = a.reshape(s, N_HEADS, width)
    if ones_lane:
        assert width == SUM_LANE
        tail = jnp.zeros((s, N_HEADS, HEAD_PAD - width), a.dtype).at[:, :, 0].set(1.0)
        return jnp.concatenate([a, tail], axis=2).reshape(s, N_HEADS * HEAD_PAD)
    return jnp.pad(a, ((0, 0), (0, 0), (0, HEAD_PAD - width))).reshape(s, N_HEADS * HEAD_PAD)


def _unpad_heads(a, width):
    s = a.shape[0]
    return a.reshape(s, N_HEADS, HEAD_PAD)[:, :, :width].reshape(s, N_HEADS * width)


def _w_in_padded(w_in):
    seg = [w_in[:, IN_OFF[i]:IN_OFF[i + 1]] for i in range(9)]
    cq, ckv, kr, fq, fk, fv, fl, gm, gf = seg
    padc = lambda a, n: jnp.pad(a, ((0, 0), (0, n - a.shape[1])))
    return jnp.concatenate([gm, gf, fq, fk, fv, cq, ckv, padc(kr, 128), padc(fl, 128)], axis=1)


def _w_in_unpadded(g):
    return jnp.concatenate([
        g[:, P_CQ:P_CQ + 384], g[:, P_CKV:P_CKV + 256], g[:, P_KR:P_KR + 32], g[:, P_FQ:P_FQ + 512],
        g[:, P_FK:P_FK + 512], g[:, P_FV:P_FV + 512], g[:, P_FL:P_FL + 8], g[:, P_GM:P_GM + 1024],
        g[:, P_GF:P_GF + 1024]], axis=1)


SMALL = (("b_ada", 6144, 6144), ("norm_mix_g", 1024, 1024), ("q_norm_g", 384, 384), ("kv_norm_g", 256, 256),
         ("b_forget", 8, 128), ("norm_ffn_g", 1024, 1024), ("conv_b", 5632, 5632), ("norm_final_g", 1024, 1024),
         ("loss", 1, 128))
SMALL_OFF = {}
_o = 0
for _n, _real, _padded in SMALL:
    SMALL_OFF[_n] = _o
    _o += _padded
SMALL_W = _o


def _pack_small(vals):
    parts = []
    for nme, real, padded in SMALL:
        a = vals[nme].reshape(1, real).astype(F32)
        parts.append(jnp.pad(a, ((0, 0), (0, padded - real))))
    return jnp.concatenate(parts, axis=1)


def kernel(x, c, positions, w_ada, b_ada, norm_mix_g, w_in, q_norm_g, w_uq, kv_norm_g, w_ukv, b_forget, w_o_mla, w_o_fox, w_out, norm_ffn_g, w_up, conv_w, conv_b, w_down, norm_final_g, loss_target, m_w_ada, m_b_ada, m_norm_mix_g, m_w_in, m_q_norm_g, m_w_uq, m_kv_norm_g, m_w_ukv, m_b_forget, m_w_o_mla, m_w_o_fox, m_w_out, m_norm_ffn_g, m_w_up, m_conv_w, m_conv_b, m_w_down, m_norm_final_g, v_w_ada, v_b_ada, v_norm_mix_g, v_w_in, v_q_norm_g, v_w_uq, v_kv_norm_g, v_w_ukv, v_b_forget, v_w_o_mla, v_w_o_fox, v_w_out, v_norm_ffn_g, v_w_up, v_conv_w, v_conv_b, v_w_down, v_norm_final_g):
    me = 4 * lax.axis_index("x") + 2 * lax.axis_index("y") + lax.axis_index("c")
    x = x[0]
    target = loss_target[0]
    s = x.shape[0]
    nblk = s // ATT_T
    big_w = {"w_in": w_in, "w_uq": w_uq, "w_ukv": w_ukv, "w_o_mla": w_o_mla, "w_o_fox": w_o_fox,
             "w_out": w_out, "w_up": w_up, "conv_w": conv_w, "w_down": w_down}
    big_m = {"w_in": m_w_in, "w_uq": m_w_uq, "w_ukv": m_w_ukv, "w_o_mla": m_w_o_mla, "w_o_fox": m_w_o_fox,
             "w_out": m_w_out, "w_up": m_w_up, "conv_w": m_conv_w, "w_down": m_w_down}
    big_v = {"w_in": v_w_in, "w_uq": v_w_uq, "w_ukv": v_w_ukv, "w_o_mla": v_w_o_mla, "w_o_fox": v_w_o_fox,
             "w_out": v_w_out, "w_up": v_w_up, "conv_w": v_conv_w, "w_down": v_w_down}

    gathered = _all_gather([big_w[k][0] if k == "conv_w" else big_w[k][0].astype(BF16) for k in BIG], "gather_weights")
    st = dict(zip(BIG, gathered))
    w_in_p = _w_in_padded(_cols_to_full(st["w_in"]))
    uq = st["w_uq"]
    w_uq_p = jnp.pad(uq, ((0, 0), (0, 0), (0, HEAD_PAD - 96))).transpose(1, 0, 2).reshape(MLA_Q_RANK, 1024)
    ukv = st["w_ukv"]
    zeros64 = jnp.zeros((N_HEADS, MLA_KV_RANK, 64), BF16)
    w_uk_p = jnp.concatenate([ukv[:, :, :64], zeros64], axis=2).transpose(1, 0, 2).reshape(MLA_KV_RANK, 1024)
    w_uv_p = jnp.concatenate([ukv[:, :, 64:], zeros64], axis=2).transpose(1, 0, 2).reshape(MLA_KV_RANK, 1024)
    place = np.zeros((HEAD_PAD, N_HEADS, HEAD_PAD), np.float32)
    for j in range(MLA_ROPE):
        place[j, :, MLA_NOPE + j] = 1.0
    place = jnp.asarray(place.reshape(HEAD_PAD, 1024), BF16)
    w_kv_comb = jnp.concatenate([
        jnp.concatenate([w_uk_p, w_uv_p], axis=1),
        jnp.concatenate([place, jnp.zeros((HEAD_PAD, 1024), BF16)], axis=1)], axis=0)
    pad_o = lambda full: jnp.pad(full.reshape(N_HEADS, 64, 1024), ((0, 0), (0, 64), (0, 0))).reshape(1024, 1024)
    w_o_mla_p = pad_o(_cols_to_full(st["w_o_mla"]))
    w_o_fox_p = pad_o(_cols_to_full(st["w_o_fox"]))
    w_out_f = st["w_out"].reshape(1024, 1024)
    w_up_f = _cols_to_full(st["w_up"])
    conv_w_f = _cols_to_full(st["conv_w"])
    w_down_f = st["w_down"].reshape(D_FF, 1024)

    (c_all,) = _all_gather([c], "gather_c")
    b_ada_mine = lax.dynamic_slice(b_ada, (0, me * 768), (1, 768))
    ada_cols, act_col = _ada_fwd(c_all.reshape(N_DEV, D_MODEL, 1), w_ada[0], b_ada_mine, "ada_fwd")
    (ada_all,) = _all_gather([ada_cols], "gather_ada")
    ada = lax.dynamic_slice(ada_all, (0, me, 0), (N_DEV, 1, 768)).reshape(1, N_ADA * D_MODEL)
    sh_m, sc_m, g_m, sh_f, sc_f, g_f = [ada[:, i * D_MODEL:(i + 1) * D_MODEL] for i in range(N_ADA)]

    inv_freq = ROPE_THETA ** (-jnp.arange(0, MLA_ROPE, 2, dtype=F32) / MLA_ROPE)
    ang = positions[0].astype(F32)[:, None] * inv_freq
    cos, sin = jnp.cos(ang), jnp.sin(ang)
    rope_c = jnp.concatenate([jnp.ones((s, 64), F32), cos, cos, jnp.zeros((s, 32), F32)], axis=1)
    rope_s = jnp.concatenate([jnp.zeros((s, 64), F32), -sin, sin, jnp.zeros((s, 32), F32)], axis=1)

    zero_d = jnp.zeros((1, D_MODEL), F32)

    h1 = _rms_mod(x, norm_mix_g, sc_m, sh_m, "norm_mix")
    proj = _mm(h1, w_in_p, "nn", F32, "proj_in", tn=640)
    cq = proj[:, P_CQ:P_CQ + 384]
    ckv = proj[:, P_CKV:P_CKV + 256]
    qn = _rms_mod(cq, q_norm_g, jnp.zeros((1, 384), F32), jnp.zeros((1, 384), F32), "q_norm")
    kvn = _rms_mod(ckv, kv_norm_g, jnp.zeros((1, 256), F32), jnp.zeros((1, 256), F32), "kv_norm")
    kv_in = jnp.concatenate([kvn, proj[:, P_KR:P_KR + 128].astype(BF16)], axis=1)
    q_pre = _mm(qn, w_uq_p, "nn", F32, "q_up")
    kv_pre = _mm(kv_in, w_kv_comb, "nn", F32, "kv_up")
    q_fold = MLA_SCALE * LOG2E
    (q_att,) = _rope(q_pre, rope_c * q_fold, rope_s * q_fold, "rope_q", N_HEADS, (BF16,))
    k_att, v_att = _rope(kv_pre, rope_c, rope_s, "rope_kv", N_HEADS, (BF16, BF16))
    o_mla, o_mla_b, lse_mla = _attn_fwd(q_att, k_att, v_att, None, None, "mla_fwd")

    fq = _pad_heads(proj[:, P_FQ:P_FQ + 512] * (FOX_SCALE * LOG2E), 64).astype(BF16)
    fk = _pad_heads(proj[:, P_FK:P_FK + 512], 64).astype(BF16)
    fv = _pad_heads(proj[:, P_FV:P_FV + 512], 64, ones_lane=True).astype(BF16)
    z = proj[:, P_FL:P_FL + 8].T.reshape(N_HEADS, s // SEQ_LANES, SEQ_LANES)
    bias_f = jnp.broadcast_to(b_forget.reshape(N_HEADS, 1, 1), (N_HEADS, 1, SEQ_LANES))
    f_cum = _forget_fwd(z, bias_f, "forget_fwd")
    f_col = (f_cum * LOG2E).reshape(N_HEADS, s, 1)
    f_row = f_col.reshape(N_HEADS, nblk, 1, ATT_T)
    o_fox, o_fox_b, lse_fox = _attn_fwd(fq, fk, fv, f_col, f_row, "fox_fwd")

    pm = _mm(o_mla_b, w_o_mla_p, "nn", F32, "o_mla_proj")
    pf = _mm(o_fox_b, w_o_fox_p, "nn", F32, "o_fox_proj")
    y = _gate_fwd(pm, pf, proj, "gate_fwd")
    x2, mix = _mm(y, w_out_f, "nn", F32, "out_proj", res=x, gvec=g_m)

    h2 = _rms_mod(x2, norm_ffn_g, sc_f, sh_f, "norm_ffn")
    u = _mm(h2, w_up_f, "nn", BF16, "ffn_up")
    a = _convglu_fwd(u, conv_w_f, conv_b, "convglu_fwd")
    x3, ffn = _mm(a, w_down_f, "nn", F32, "ffn_down", res=x2, gvec=g_f, tk=2816)

    dx3, sums_final = _final_loss(x3, target, norm_final_g.reshape(1, D_MODEL), "final_loss")
    dffn, sums_gf = _scale_bwd(dx3, ffn, g_f, "ffn_scale_bwd")
    da = _mm(dffn, w_down_f, "nt", BF16, "ffn_down_dx", tn=1408)
    g_w_down = _mm(a.T, dffn, "nn", F32, "ffn_down_dw", tm=1408, tn=1024)
    dgate, dval, s_gate, s_val = _convglu_bwd(da, u, conv_w_f, conv_b, "convglu_bwd")
    du = jnp.concatenate([_conv_transpose(dgate, conv_w_f[:, :D_FF], "conv_t_gate"),
                          _conv_transpose(dval, conv_w_f[:, D_FF:], "conv_t_val")], axis=1)
    dh2 = _mm(du, w_up_f, "nt", F32, "ffn_up_dx", tn=1024, tk=1408)
    g_w_up = _mm(h2.T, du, "nn", F32, "ffn_up_dw")
    dx2, sums_ffn = _rms_mod_bwd(dh2, x2, norm_ffn_g, sc_f, dx3, "norm_ffn_bwd")

    dmix, sums_gm = _scale_bwd(dx2, mix, g_m, "mix_scale_bwd")
    dy = _mm(dmix, w_out_f, "nt", F32, "out_proj_dx")
    g_w_out = _mm(y.T, dmix, "nn", F32, "out_proj_dw")
    dpm, dpf, dgm, dgf = _gate_bwd(dy, pm, pf, proj, "gate_bwd")
    do_mla_b = _mm(dpm, w_o_mla_p, "nt", BF16, "o_mla_dx", tn=1024)
    do_fox_b = _mm(dpf, w_o_fox_p, "nt", BF16, "o_fox_dx", tn=1024)
    g_w_o_mla_p = _mm(o_mla_b.T, dpm, "nn", F32, "o_mla_dw")
    g_w_o_fox_p = _mm(o_fox_b.T, dpf, "nn", F32, "o_fox_dw")

    rows = lambda col: col.reshape(N_HEADS, nblk, 1, ATT_T)
    delta_mla = _attn_delta(o_mla, do_mla_b, "mla_delta")
    dq_rot = _attn_bwd_dq(q_att, k_att, v_att, do_mla_b, lse_mla, delta_mla, None, None, MLA_SCALE, "mla_bwd_dq")
    dk_rot, dv_mla = _attn_bwd_dkv(q_att, k_att, v_att, do_mla_b, rows(lse_mla), rows(delta_mla), None, None,
                                   1.0 / LOG2E, "mla_bwd_dkv")
    (dq_pre,) = _rope(dq_rot, rope_c, -rope_s, "rope_q_bwd", N_HEADS, (BF16,))
    dkv_pre = _rope_bwd_kv(dk_rot, dv_mla, rope_c, -rope_s, "rope_kv_bwd")
    dqn = _mm(dq_pre, w_uq_p, "nt", F32, "q_up_dx")
    g_w_uq_p = _mm(qn.T, dq_pre, "nn", F32, "q_up_dw")
    dkv_in = _mm(dkv_pre, w_kv_comb, "nt", F32, "kv_up_dx")
    g_w_kv_comb = _mm(kv_in.T, dkv_pre, "nn", F32, "kv_up_dw")
    dcq, sums_q = _rms_mod_bwd(dqn, cq, q_norm_g, jnp.zeros((1, 384), F32), None, "q_norm_bwd")
    dckv, sums_kv = _rms_mod_bwd(dkv_in[:, :256], ckv, kv_norm_g, jnp.zeros((1, 256), F32), None, "kv_norm_bwd")
    delta_fox = _attn_delta(o_fox, do_fox_b, "fox_delta")
    dfq, dfr = _attn_bwd_dq(fq, fk, fv, do_fox_b, lse_fox, delta_fox, f_col, f_row, FOX_SCALE, "fox_bwd_dq", BF16)
    dfk, dfv, dfc = _attn_bwd_dkv(fq, fk, fv, do_fox_b, rows(lse_fox), rows(delta_fox), f_col, f_row,
                                  1.0 / LOG2E, "fox_bwd_dkv", BF16)
    df_rows = dfr.reshape(N_HEADS, s // SEQ_LANES, SEQ_LANES)
    df_cols = dfc.reshape(N_HEADS, s // SEQ_LANES, SEQ_LANES)
    dz, db_f = _forget_bwd(df_rows, df_cols, z, bias_f, "forget_bwd")
    dfl = jnp.pad(dz.reshape(N_HEADS, s).T, ((0, 0), (0, 128 - N_HEADS)))

    dproj = jnp.concatenate([
        dgm, dgf, _unpad_heads(dfq, 64), _unpad_heads(dfk, 64), _unpad_heads(dfv, 64),
        dcq.astype(BF16), dckv.astype(BF16), dkv_in[:, 256:384].astype(BF16), dfl.astype(BF16)], axis=1)
    dh1 = _mm(dproj, w_in_p, "nt", F32, "proj_in_dx", tn=1024, tk=896)
    g_w_in_p = _mm(h1.T, dproj, "nn", F32, "proj_in_dw", tn=640)
    grad_x, sums_mix = _rms_mod_bwd(dh1, x, norm_mix_g, sc_m, dx2, "norm_mix_bwd")

    g_w_in = _w_in_unpadded(g_w_in_p)
    g_uq = g_w_uq_p.reshape(MLA_Q_RANK, N_HEADS, HEAD_PAD)[:, :, :96].transpose(1, 0, 2)
    g_uk = g_w_kv_comb[:256, :1024].reshape(256, N_HEADS, HEAD_PAD)[:, :, :64]
    g_uv = g_w_kv_comb[:256, 1024:].reshape(256, N_HEADS, HEAD_PAD)[:, :, :64]
    g_ukv = jnp.concatenate([g_uk, g_uv], axis=2).transpose(1, 0, 2)
    unpad_o = lambda g: g.reshape(N_HEADS, HEAD_PAD, 1024)[:, :64].reshape(512, 1024)
    g_conv_w = jnp.concatenate([s_gate[0:3], s_val[0:3]], axis=1)
    g_blocks = {
        "w_in": _full_to_cols(g_w_in, 533), "w_uq": g_uq, "w_ukv": g_ukv,
        "w_o_mla": _full_to_cols(unpad_o(g_w_o_mla_p), 128), "w_o_fox": _full_to_cols(unpad_o(g_w_o_fox_p), 128),
        "w_out": g_w_out.reshape(N_DEV, 128, 1024), "w_up": _full_to_cols(g_w_up, 704),
        "conv_w": _full_to_cols(g_conv_w, 704), "w_down": g_w_down.reshape(N_DEV, 352, 1024)}
    g_recv = _all_to_all([g_blocks[k].astype(BF16) for k in BIG], "scatter_grads")
    g_big, d_big, nm_big, nv_big = {}, {}, {}, {}
    for k, parts in zip(BIG, g_recv):
        g_big[k], d_big[k], nm_big[k], nv_big[k] = [
            t[None] for t in _adamw(parts, big_w[k][0], big_m[k][0], big_v[k][0], "adamw_" + k)]

    dada = jnp.concatenate([sums_mix[0:1], sums_mix[1:2], sums_gm[0:1], sums_ffn[0:1], sums_ffn[1:2], sums_gf[0:1]],
                           axis=1)
    small_part = _pack_small({
        "b_ada": dada, "norm_mix_g": sums_mix[2:3], "q_norm_g": sums_q[2:3], "kv_norm_g": sums_kv[2:3],
        "b_forget": db_f[:, 0, 0], "norm_ffn_g": sums_ffn[2:3],
        "conv_b": jnp.concatenate([s_gate[3:4], s_val[3:4]], axis=1), "norm_final_g": sums_final[0:1],
        "loss": sums_final[1:2, 0:1]})
    (small_all,) = _all_gather([small_part], "gather_small")
    zero1 = jnp.zeros((1,), F32)
    small_w = {"b_ada": b_ada, "norm_mix_g": norm_mix_g, "q_norm_g": q_norm_g, "kv_norm_g": kv_norm_g,
               "b_forget": b_forget, "norm_ffn_g": norm_ffn_g, "conv_b": conv_b, "norm_final_g": norm_final_g,
               "loss": zero1}
    small_m = {"b_ada": m_b_ada, "norm_mix_g": m_norm_mix_g, "q_norm_g": m_q_norm_g, "kv_norm_g": m_kv_norm_g,
               "b_forget": m_b_forget, "norm_ffn_g": m_norm_ffn_g, "conv_b": m_conv_b,
               "norm_final_g": m_norm_final_g, "loss": zero1}
    small_v = {"b_ada": v_b_ada, "norm_mix_g": v_norm_mix_g, "q_norm_g": v_q_norm_g, "kv_norm_g": v_kv_norm_g,
               "b_forget": v_b_forget, "norm_ffn_g": v_norm_ffn_g, "conv_b": v_conv_b,
               "norm_final_g": v_norm_final_g, "loss": zero1}
    g_sm, d_sm, nm_sm, nv_sm = _adamw(small_all, _pack_small(small_w), _pack_small(small_m), _pack_small(small_v),
                                      "adamw_small")
    loss = g_sm[0, SMALL_OFF["loss"]]

    dada_all = small_all[:, 0, SMALL_OFF["b_ada"]:SMALL_OFF["b_ada"] + N_ADA * D_MODEL]
    dada_mine = lax.dynamic_slice(dada_all, (0, me * 768), (N_DEV, 768))
    g_ada_local = _ada_bwd(act_col, dada_mine, "ada_bwd")
    g_ada, d_ada, nm_ada, nv_ada = _adamw(g_ada_local[None], w_ada[0], m_w_ada[0], v_w_ada[0], "adamw_ada")

    def small_out(t, nme, shape):
        real = dict((n_, r_) for n_, r_, _ in SMALL)[nme]
        o = SMALL_OFF[nme]
        return t[0, o:o + real].reshape(shape)

    order = ["w_ada", "b_ada", "norm_mix_g", "w_in", "q_norm_g", "w_uq", "kv_norm_g", "w_ukv", "b_forget",
             "w_o_mla", "w_o_fox", "w_out", "norm_ffn_g", "w_up", "conv_w", "conv_b", "w_down", "norm_final_g"]
    small_shapes = {"b_ada": (1, 6144), "norm_mix_g": (1, 1024), "q_norm_g": (1, 384), "kv_norm_g": (1, 256),
                    "b_forget": (1, 8), "norm_ffn_g": (1, 1024), "conv_b": (1, 5632), "norm_final_g": (1024,)}

    def family(big, small, ada_t):
        out = []
        for nme in order:
            if nme == "w_ada":
                out.append(ada_t[None])
            elif nme in small_shapes:
                out.append(small_out(small, nme, small_shapes[nme]))
            else:
                out.append(big[nme])
        return out

    return (loss, grad_x[None], *family(g_big, g_sm, g_ada), *family(d_big, d_sm, d_ada),
            *family(nm_big, nm_sm, nm_ada), *family(nv_big, nv_sm, nv_ada))
```

```python
import math

import numpy as np
import jax
import jax.numpy as jnp
from jax import lax
from jax.experimental import pallas as pl
from jax.experimental.pallas import tpu as pltpu

F32 = jnp.float32
BF16 = jnp.bfloat16

N_DEV = 8
D_MODEL = 1024
N_HEADS = 8
HEAD_PAD = 128
MLA_Q_RANK = 384
MLA_KV_RANK = 256
MLA_NOPE = 64
MLA_ROPE = 32
MLA_V = 64
FOX_DIM = 64
D_FF = 2816
N_ADA = 6
EPS = 1e-6
ROPE_THETA = 10000.0
MLA_SCALE = 1.0 / math.sqrt(MLA_NOPE + MLA_ROPE)
FOX_SCALE = 1.0 / math.sqrt(FOX_DIM)
IN_SPLITS = (384, 256, 32, 512, 512, 512, 8, 1024, 1024)
D_IN = sum(IN_SPLITS)
IN_OFF = tuple(int(v) for v in np.cumsum((0,) + IN_SPLITS))
P_GM, P_GF, P_FQ, P_FK, P_FV, P_CQ, P_CKV, P_KR, P_FL, D_IN_P = 0, 1024, 2048, 2560, 3072, 3584, 3968, 4224, 4352, 4480

ADAM_LR, ADAM_B1, ADAM_B2, ADAM_EPS, ADAM_WD, ADAM_STEP = 0.001, 0.9, 0.999, 1e-08, 0.01, 10

VMEM_LIMIT_BYTES = 56 * 1024 * 1024
NEG_BIG = -1e30
ATT_T = 512
LOG2E = 1.4426950408889634
SUM_LANE = 64
ROW_T = 256
SEQ_LANES = 128
BF16_ROWS = 16
FWD_HEADS_PER_STEP = 2


def _params(sem):
    return pltpu.CompilerParams(dimension_semantics=sem, vmem_limit_bytes=VMEM_LIMIT_BYTES)


def _tile(n, target, step=128):
    if n <= target:
        return n
    t = (target // step) * step
    while t >= step:
        if n % t == 0:
            return t
        t -= step
    return n


def _vec_spec(w, nargs):
    if nargs == 1:
        return pl.BlockSpec((1, w), lambda i: (0, 0))
    return pl.BlockSpec((1, w), lambda i, j: (0, 0))


def _comm_call(body, name, ins, out_shapes):
    n = len(ins)
    any_spec = pl.BlockSpec(memory_space=pl.ANY)
    return pl.pallas_call(
        body, name=name, out_shape=tuple(out_shapes),
        in_specs=[any_spec] * n, out_specs=tuple([any_spec] * n),
        scratch_shapes=[pltpu.SemaphoreType.DMA((n, 7)), pltpu.SemaphoreType.DMA((n, 7)),
                        pltpu.SemaphoreType.DMA((n,))],
    )(*ins)


def _all_gather(xs, name):
    n = len(xs)

    def body(*refs):
        x_refs, out_refs = refs[:n], refs[n:2 * n]
        send_sems, recv_sems, local_sems = refs[2 * n:]
        x_, y_, c_ = lax.axis_index("x"), lax.axis_index("y"), lax.axis_index("c")
        me, sibling = (x_, y_, c_), (x_, y_, 1 - c_)
        chips = [(1 - x_, y_), (x_, 1 - y_), (1 - x_, 1 - y_)]

        def slot(a, px, py, pc):
            return out_refs[a].at[4 * px + 2 * py + pc]

        def copy(a, k, block, to, src=None):
            return pltpu.make_async_remote_copy(
                src_ref=slot(a, *block) if src is None else src, dst_ref=slot(a, *block),
                send_sem=send_sems.at[a, k], recv_sem=recv_sems.at[a, k],
                device_id=to, device_id_type=pl.DeviceIdType.MESH)

        mine = [pltpu.make_async_copy(x_refs[a], slot(a, *me), local_sems.at[a]) for a in range(n)]
        for cp in mine:
            cp.start()
        first = []
        for a in range(n):
            first.append(copy(a, 0, me, sibling, src=x_refs[a]))
            first += [copy(a, 1 + j, me, (*chip, c_), src=x_refs[a]) for j, chip in enumerate(chips)]
        for cp in first:
            cp.start()
        passed = []
        for j, chip in enumerate(chips):
            for a in range(n):
                copy(a, 1 + j, (*chip, c_), me).wait_recv()
                passed.append(copy(a, 4 + j, (*chip, c_), sibling))
                passed[-1].start()
        for a in range(n):
            copy(a, 0, sibling, me).wait_recv()
            for j, chip in enumerate(chips):
                copy(a, 4 + j, (*chip, 1 - c_), me).wait_recv()
        for cp in first + passed:
            cp.wait_send()
        for cp in mine:
            cp.wait()

    return _comm_call(body, name, xs, [jax.ShapeDtypeStruct((N_DEV,) + x.shape, x.dtype) for x in xs])


def _all_to_all(gs, name):
    n = len(gs)

    def body(*refs):
        g_refs, out_refs = refs[:n], refs[n:2 * n]
        send_sems, recv_sems, local_sems = refs[2 * n:]
        x_, y_, c_ = lax.axis_index("x"), lax.axis_index("y"), lax.axis_index("c")
        me = 4 * x_ + 2 * y_ + c_

        def peer(k):
            return (x_ ^ ((k >> 2) & 1), y_ ^ ((k >> 1) & 1), c_ ^ (k & 1))

        def copy(a, k, sending):
            px, py, pc = peer(k)
            theirs = 4 * px + 2 * py + pc
            return pltpu.make_async_remote_copy(
                src_ref=g_refs[a].at[theirs if sending else me], dst_ref=out_refs[a].at[me if sending else theirs],
                send_sem=send_sems.at[a, k - 1], recv_sem=recv_sems.at[a, k - 1],
                device_id=(px, py, pc), device_id_type=pl.DeviceIdType.MESH)

        mine = [pltpu.make_async_copy(g_refs[a].at[me], out_refs[a].at[me], local_sems.at[a]) for a in range(n)]
        for cp in mine:
            cp.start()
        sends = [copy(a, k, True) for a in range(n) for k in range(1, N_DEV)]
        for cp in sends:
            cp.start()
        for a in range(n):
            for k in range(1, N_DEV):
                copy(a, k, False).wait_recv()
        for cp in sends:
            cp.wait_send()
        for cp in mine:
            cp.wait()

    return _comm_call(body, name, gs, [jax.ShapeDtypeStruct(g.shape, g.dtype) for g in gs])


def _mm(a, b, mode, out_dtype, name, res=None, gvec=None, tm=1024, tn=512, tk=1024):
    m, k = a.shape
    n = b.shape[1] if mode == "nn" else b.shape[0]
    tm, tn, tk = _tile(m, tm), _tile(n, tn), _tile(k, tk)
    nk = k // tk
    dims = (((1,), (0,)), ((), ())) if mode == "nn" else (((1,), (1,)), ((), ()))
    fused = res is not None

    def body(*refs):
        if fused:
            a_ref, b_ref, res_ref, g_ref, o_ref, raw_ref, acc_ref = refs
        else:
            a_ref, b_ref, o_ref, acc_ref = refs
        kk = pl.program_id(2)
        part = lax.dot_general(a_ref[...], b_ref[...], dims, preferred_element_type=F32)

        @pl.when(kk == 0)
        def _():
            acc_ref[...] = part

        @pl.when(kk > 0)
        def _():
            acc_ref[...] += part

        @pl.when(kk == nk - 1)
        def _():
            acc = acc_ref[...]
            if fused:
                raw_ref[...] = acc
                o_ref[...] = (res_ref[...] + g_ref[...] * acc).astype(o_ref.dtype)
            else:
                o_ref[...] = acc.astype(o_ref.dtype)

    a_spec = pl.BlockSpec((tm, tk), lambda i, j, kk: (i, kk))
    if mode == "nn":
        b_spec = pl.BlockSpec((tk, tn), lambda i, j, kk: (kk, j))
    else:
        b_spec = pl.BlockSpec((tn, tk), lambda i, j, kk: (j, kk))
    o_spec = pl.BlockSpec((tm, tn), lambda i, j, kk: (i, j))
    in_specs, args = [a_spec, b_spec], [a, b]
    out_specs, out_shape = o_spec, jax.ShapeDtypeStruct((m, n), out_dtype)
    if fused:
        in_specs += [o_spec, pl.BlockSpec((1, tn), lambda i, j, kk: (0, j))]
        args += [res, gvec]
        out_specs = (o_spec, o_spec)
        out_shape = (out_shape, jax.ShapeDtypeStruct((m, n), F32))
    return pl.pallas_call(
        body, name=name, grid=(m // tm, n // tn, nk),
        in_specs=in_specs, out_specs=out_specs, out_shape=out_shape,
        scratch_shapes=[pltpu.VMEM((tm, tn), F32)],
        compiler_params=_params(("parallel", "parallel", "arbitrary")),
    )(*args)


def _rms_mod(x, g, sc, sh, name):
    s, w = x.shape
    tm = _tile(s, ROW_T)

    def body(x_ref, g_ref, sc_ref, sh_ref, o_ref):
        xv = x_ref[...]
        r = lax.rsqrt(jnp.mean(xv * xv, axis=-1, keepdims=True) + EPS)
        o_ref[...] = ((xv * r * g_ref[...]) * (1.0 + sc_ref[...]) + sh_ref[...]).astype(o_ref.dtype)

    row = pl.BlockSpec((tm, w), lambda i: (i, 0))
    return pl.pallas_call(
        body, name=name, grid=(s // tm,),
        in_specs=[row, _vec_spec(w, 1), _vec_spec(w, 1), _vec_spec(w, 1)],
        out_specs=row, out_shape=jax.ShapeDtypeStruct((s, w), BF16),
        compiler_params=_params(("parallel",)),
    )(x, g, sc, sh)


def _rms_mod_bwd(dh, x, g, sc, dres, name):
    s, w = x.shape
    tm = _tile(s, ROW_T)
    has_res = dres is not None

    def body(*refs):
        if has_res:
            dh_ref, x_ref, g_ref, sc_ref, dres_ref, dx_ref, sums_ref = refs
        else:
            dh_ref, x_ref, g_ref, sc_ref, dx_ref, sums_ref = refs
        xv, dhv, gv = x_ref[...], dh_ref[...], g_ref[...]
        r = lax.rsqrt(jnp.mean(xv * xv, axis=-1, keepdims=True) + EPS)
        xhat = xv * r
        dxn = dhv * (1.0 + sc_ref[...])
        dxhat = dxn * gv
        dx = r * (dxhat - xhat * jnp.mean(dxhat * xhat, axis=-1, keepdims=True))
        if has_res:
            dx = dx + dres_ref[...]
        dx_ref[...] = dx

        @pl.when(pl.program_id(0) == 0)
        def _():
            sums_ref[...] = jnp.zeros_like(sums_ref)

        sums_ref[0:1, :] += jnp.sum(dhv, axis=0, keepdims=True)
        sums_ref[1:2, :] += jnp.sum(dhv * (xhat * gv), axis=0, keepdims=True)
        sums_ref[2:3, :] += jnp.sum(dxn * xhat, axis=0, keepdims=True)

    row = pl.BlockSpec((tm, w), lambda i: (i, 0))
    in_specs = [row, row, _vec_spec(w, 1), _vec_spec(w, 1)] + ([row] if has_res else [])
    args = [dh, x, g, sc] + ([dres] if has_res else [])
    return pl.pallas_call(
        body, name=name, grid=(s // tm,),
        in_specs=in_specs,
        out_specs=(row, pl.BlockSpec((8, w), lambda i: (0, 0))),
        out_shape=(jax.ShapeDtypeStruct((s, w), F32), jax.ShapeDtypeStruct((8, w), F32)),
        compiler_params=_params(("arbitrary",)),
    )(*args)


def _scale_bwd(dx, val, gvec, name):
    s, w = dx.shape
    tm = _tile(s, ROW_T)

    def body(dx_ref, val_ref, g_ref, d_ref, sums_ref):
        dxv = dx_ref[...]
        d_ref[...] = (dxv * g_ref[...]).astype(d_ref.dtype)

        @pl.when(pl.program_id(0) == 0)
        def _():
            sums_ref[...] = jnp.zeros_like(sums_ref)

        sums_ref[0:1, :] += jnp.sum(dxv * val_ref[...], axis=0, keepdims=True)

    row = pl.BlockSpec((tm, w), lambda i: (i, 0))
    return pl.pallas_call(
        body, name=name, grid=(s // tm,),
        in_specs=[row, row, _vec_spec(w, 1)],
        out_specs=(row, pl.BlockSpec((8, w), lambda i: (0, 0))),
        out_shape=(jax.ShapeDtypeStruct((s, w), BF16), jax.ShapeDtypeStruct((8, w), F32)),
        compiler_params=_params(("arbitrary",)),
    )(dx, val, gvec)


def _final_loss(x3, target, g, name):
    s, w = x3.shape
    tm = _tile(s, ROW_T)

    def body(x_ref, t_ref, g_ref, dx_ref, sums_ref):
        xv, gv = x_ref[...], g_ref[...]
        r = lax.rsqrt(jnp.mean(xv * xv, axis=-1, keepdims=True) + EPS)
        xhat = xv * r
        err = xhat * gv - t_ref[...]
        dy = err * (1.0 / w)
        dxhat = dy * gv
        dx_ref[...] = r * (dxhat - xhat * jnp.mean(dxhat * xhat, axis=-1, keepdims=True))

        @pl.when(pl.program_id(0) == 0)
        def _():
            sums_ref[...] = jnp.zeros_like(sums_ref)

        sums_ref[0:1, :] += jnp.sum(dy * xhat, axis=0, keepdims=True)
        sums_ref[1:2, :] += jnp.zeros((1, w), F32) + (0.5 / w) * jnp.sum(err * err)

    row = pl.BlockSpec((tm, w), lambda i: (i, 0))
    return pl.pallas_call(
        body, name=name, grid=(s // tm,),
        in_specs=[row, row, _vec_spec(w, 1)],
        out_specs=(row, pl.BlockSpec((8, w), lambda i: (0, 0))),
        out_shape=(jax.ShapeDtypeStruct((s, w), F32), jax.ShapeDtypeStruct((8, w), F32)),
        compiler_params=_params(("arbitrary",)),
    )(x3, target, g)


def _rope_block(seg, cmul, smul):
    lane = lax.broadcasted_iota(jnp.int32, seg.shape, 1)
    swapped = jnp.where(lane < MLA_NOPE + MLA_ROPE // 2,
                        pltpu.roll(seg, HEAD_PAD - MLA_ROPE // 2, 1), pltpu.roll(seg, MLA_ROPE // 2, 1))
    return seg * cmul + swapped * smul


def _rope(t, cmul, smul, name, n_rot, out_dtypes):
    s, w = t.shape
    tm = _tile(s, ROW_T)
    n_out = len(out_dtypes)
    wo = w // n_out

    def body(t_ref, c_ref, s_ref, *o_refs):
        cv, sv = c_ref[...], s_ref[...]
        one = (lax.broadcasted_iota(jnp.int32, (tm, HEAD_PAD), 1) == SUM_LANE).astype(F32)
        for hb in range(w // HEAD_PAD):
            seg = t_ref[:, hb * HEAD_PAD:(hb + 1) * HEAD_PAD]
            if hb < n_rot:
                seg = _rope_block(seg, cv, sv)
            else:
                seg = seg + one
            o_ref = o_refs[(hb * HEAD_PAD) // wo]
            col = (hb * HEAD_PAD) % wo
            o_ref[:, col:col + HEAD_PAD] = seg.astype(o_ref.dtype)

    row = pl.BlockSpec((tm, w), lambda i: (i, 0))
    tab = pl.BlockSpec((tm, HEAD_PAD), lambda i: (i, 0))
    orow = pl.BlockSpec((tm, wo), lambda i: (i, 0))
    outs = pl.pallas_call(
        body, name=name, grid=(s // tm,),
        in_specs=[row, tab, tab],
        out_specs=tuple(orow for _ in out_dtypes),
        out_shape=tuple(jax.ShapeDtypeStruct((s, wo), dt) for dt in out_dtypes),
        compiler_params=_params(("parallel",)),
    )(t, cmul, smul)
    return outs


def _rope_bwd_kv(dk, dv, cmul, smul, name):
    s, w = dk.shape
    tm = _tile(s, ROW_T)

    def body(dk_ref, dv_ref, c_ref, s_ref, o_ref):
        cv, sv = c_ref[...], s_ref[...]
        for hb in range(N_HEADS):
            lo, hi = hb * HEAD_PAD, (hb + 1) * HEAD_PAD
            o_ref[:, lo:hi] = _rope_block(dk_ref[:, lo:hi], cv, sv).astype(o_ref.dtype)
        o_ref[:, w:2 * w] = dv_ref[...].astype(o_ref.dtype)

    row = pl.BlockSpec((tm, w), lambda i: (i, 0))
    tab = pl.BlockSpec((tm, HEAD_PAD), lambda i: (i, 0))
    return pl.pallas_call(
        body, name=name, grid=(s // tm,),
        in_specs=[row, row, tab, tab],
        out_specs=pl.BlockSpec((tm, 2 * w), lambda i: (i, 0)),
        out_shape=jax.ShapeDtypeStruct((s, 2 * w), BF16),
        compiler_params=_params(("parallel",)),
    )(dk, dv, cmul, smul)


def _lanes(col, width):
    if col.shape[1] == 1:
        col = jnp.broadcast_to(col, (col.shape[0], HEAD_PAD))
    return jnp.tile(col, (1, width // HEAD_PAD))


def _fold_lanes(a):
    out = a[:, 0:HEAD_PAD]
    for g in range(1, a.shape[1] // HEAD_PAD):
        out = out + a[:, g * HEAD_PAD:(g + 1) * HEAD_PAD]
    return out


def _causal(t, rows_are_queries):
    row = lax.broadcasted_iota(jnp.int32, (t, t), 0)
    col = lax.broadcasted_iota(jnp.int32, (t, t), 1)
    return row >= col if rows_are_queries else col >= row


def _attn_fwd(q, k, v, fcol, frow, name):
    s = q.shape[0]
    t = ATT_T
    nq = s // t
    use_f = fcol is not None

    hpb = FWD_HEADS_PER_STEP

    def body(*refs):
        if use_f:
            q_ref, k_ref, v_ref, fc_ref, fr_ref, o_ref, ob_ref, lse_ref, m_s, acc_s = refs
            fc_b = [jnp.broadcast_to(fc_ref[hh], (t, HEAD_PAD)) for hh in range(hpb)]
        else:
            q_ref, k_ref, v_ref, o_ref, ob_ref, lse_ref, m_s, acc_s = refs
        qi = pl.program_id(1)
        m_s[...] = jnp.full(m_s.shape, NEG_BIG, F32)
        acc_s[...] = jnp.zeros(acc_s.shape, F32)

        def step(j, masked):
            off = pl.multiple_of(j * t, t)
            for hh in range(hpb):
                lanes = slice(hh * HEAD_PAD, (hh + 1) * HEAD_PAD)
                kv = k_ref[pl.ds(off, t), lanes]
                vv = v_ref[pl.ds(off, t), lanes]
                sc = lax.dot_general(q_ref[:, lanes], kv, (((1,), (1,)), ((), ())), preferred_element_type=F32)
                if use_f:
                    sc = sc + (_lanes(fc_b[hh], t) - fr_ref[hh, j])
                if masked:
                    sc = jnp.where(_causal(t, True), sc, NEG_BIG)
                m_prev = m_s[hh]
                m_new = jnp.maximum(m_prev, jnp.max(sc, axis=-1, keepdims=True))
                p = jnp.exp2(sc - _lanes(m_new, t))
                acc_s[hh] = jnp.exp2(m_prev - m_new) * acc_s[hh] + jnp.dot(p.astype(BF16), vv,
                                                                           preferred_element_type=F32)
                m_s[hh] = m_new

        def loop_body(j, carry):
            step(j, False)
            return carry

        lax.fori_loop(0, qi, loop_body, 0)
        step(qi, True)
        for hh in range(hpb):
            lanes = slice(hh * HEAD_PAD, (hh + 1) * HEAD_PAD)
            acc = acc_s[hh]
            lane = lax.broadcasted_iota(jnp.int32, acc.shape, 1)
            denom = jnp.sum(jnp.where(lane == SUM_LANE, acc, 0.0), axis=-1, keepdims=True)
            o = acc * (1.0 / denom)
            o_ref[:, lanes] = o
            ob_ref[:, lanes] = o.astype(BF16)
            lse_ref[hh] = jnp.max(m_s[hh], axis=-1, keepdims=True) + jnp.log(denom) * LOG2E

    w = hpb * HEAD_PAD
    qspec = pl.BlockSpec((t, w), lambda h, i: (i, h))
    kspec = pl.BlockSpec((s, w), lambda h, i: (0, h))
    colspec = pl.BlockSpec((hpb, t, 1), lambda h, i: (h, i, 0))
    in_specs, args = [qspec, kspec, kspec], [q, k, v]
    if use_f:
        in_specs += [colspec, pl.BlockSpec((hpb, nq, 1, t), lambda h, i: (h, 0, 0, 0))]
        args += [fcol, frow]
    return pl.pallas_call(
        body, name=name, grid=(N_HEADS // hpb, nq),
        in_specs=in_specs,
        out_specs=(qspec, qspec, colspec),
        out_shape=(jax.ShapeDtypeStruct((s, N_HEADS * HEAD_PAD), F32), jax.ShapeDtypeStruct((s, N_HEADS * HEAD_PAD), BF16),
                   jax.ShapeDtypeStruct((N_HEADS, s, 1), F32)),
        scratch_shapes=[pltpu.VMEM((hpb, t, HEAD_PAD), F32), pltpu.VMEM((hpb, t, HEAD_PAD), F32)],
        compiler_params=_params(("parallel", "arbitrary")),
    )(*args)


def _attn_delta(o, do, name):
    s, w = o.shape
    tm = _tile(s, ROW_T)

    def body(o_ref, do_ref, d_ref):
        for hb in range(N_HEADS):
            lo, hi = hb * HEAD_PAD, (hb + 1) * HEAD_PAD
            d_ref[hb] = jnp.sum(o_ref[:, lo:hi] * do_ref[:, lo:hi].astype(F32), axis=-1, keepdims=True)

    row = pl.BlockSpec((tm, w), lambda i: (i, 0))
    return pl.pallas_call(
        body, name=name, grid=(s // tm,),
        in_specs=[row, row],
        out_specs=pl.BlockSpec((N_HEADS, tm, 1), lambda i: (0, i, 0)),
        out_shape=jax.ShapeDtypeStruct((N_HEADS, s, 1), F32),
        compiler_params=_params(("parallel",)),
    )(o, do)


def _attn_bwd_dq(q, k, v, do, lse, delta, fcol, frow, scale, name, out_dtype=F32):
    s = q.shape[0]
    t = ATT_T
    nq = s // t
    use_f = fcol is not None

    def body(*refs):
        if use_f:
            q_ref, k_ref, v_ref, do_ref, lse_ref, dl_ref, fc_ref, fr_ref, dq_ref, dr_ref, acc_s, dr_s = refs
            dr_s[...] = jnp.zeros(dr_s.shape, F32)
            fc_b = jnp.broadcast_to(fc_ref[0], (t, HEAD_PAD))
        else:
            q_ref, k_ref, v_ref, do_ref, lse_ref, dl_ref, dq_ref, acc_s = refs
        qi = pl.program_id(1)
        qv, dov = q_ref[...], do_ref[...]
        lse_b = jnp.broadcast_to(lse_ref[0], (t, HEAD_PAD))
        dl_b = jnp.broadcast_to(dl_ref[0], (t, HEAD_PAD))
        acc_s[...] = jnp.zeros(acc_s.shape, F32)

        def step(j, masked):
            off = pl.multiple_of(j * t, t)
            kv = k_ref[pl.ds(off, t), :]
            vv = v_ref[pl.ds(off, t), :]
            sc = lax.dot_general(qv, kv, (((1,), (1,)), ((), ())), preferred_element_type=F32)
            if use_f:
                sc = sc + (_lanes(fc_b, t) - fr_ref[0, j])
            if masked:
                sc = jnp.where(_causal(t, True), sc, NEG_BIG)
            p = jnp.exp2(sc - _lanes(lse_b, t))
            dp = lax.dot_general(dov, vv, (((1,), (1,)), ((), ())), preferred_element_type=F32)
            ds = p * (dp - _lanes(dl_b, t))
            acc_s[...] += jnp.dot(ds.astype(BF16), kv, preferred_element_type=F32)
            if use_f:
                dr_s[...] += _fold_lanes(ds)

        def loop_body(j, carry):
            step(j, False)
            return carry

        lax.fori_loop(0, qi, loop_body, 0)
        step(qi, True)
        dq_ref[...] = (acc_s[...] * scale).astype(dq_ref.dtype)
        if use_f:
            dr_ref[0] = jnp.sum(dr_s[...], axis=-1, keepdims=True)

    qspec = pl.BlockSpec((t, HEAD_PAD), lambda h, i: (i, h))
    kspec = pl.BlockSpec((s, HEAD_PAD), lambda h, i: (0, h))
    colspec = pl.BlockSpec((1, t, 1), lambda h, i: (h, i, 0))
    in_specs, args = [qspec, kspec, kspec, qspec, colspec, colspec], [q, k, v, do, lse, delta]
    out_specs, out_shape = qspec, jax.ShapeDtypeStruct((s, N_HEADS * HEAD_PAD), out_dtype)
    scratch = [pltpu.VMEM((t, HEAD_PAD), F32)]
    if use_f:
        in_specs += [colspec, pl.BlockSpec((1, nq, 1, t), lambda h, i: (h, 0, 0, 0))]
        args += [fcol, frow]
        out_specs, out_shape = (qspec, colspec), (out_shape, jax.ShapeDtypeStruct((N_HEADS, s, 1), F32))
        scratch += [pltpu.VMEM((t, HEAD_PAD), F32)]
    return pl.pallas_call(
        body, name=name, grid=(N_HEADS, nq),
        in_specs=in_specs, out_specs=out_specs, out_shape=out_shape,
        scratch_shapes=scratch,
        compiler_params=_params(("parallel", "arbitrary")),
    )(*args)


def _attn_bwd_dkv(q, k, v, do, lse_row, delta_row, fcol, frow, scale, name, out_dtype=F32):
    s = q.shape[0]
    t = ATT_T
    nq = s // t
    use_f = fcol is not None

    def body(*refs):
        if use_f:
            q_ref, k_ref, v_ref, do_ref, lse_ref, dl_ref, fc_ref, fr_ref, dk_ref, dv_ref, df_ref, dk_s, dv_s, df_s = refs
        else:
            q_ref, k_ref, v_ref, do_ref, lse_ref, dl_ref, dk_ref, dv_ref, dk_s, dv_s = refs
        kj = pl.program_id(1)
        kv, vv = k_ref[...], v_ref[...]
        dk_s[...] = jnp.zeros(dk_s.shape, F32)
        dv_s[...] = jnp.zeros(dv_s.shape, F32)
        if use_f:
            df_s[...] = jnp.zeros(df_s.shape, F32)
            fc_b = jnp.broadcast_to(fc_ref[0], (t, HEAD_PAD))

        def step(i, masked):
            off = pl.multiple_of(i * t, t)
            qv = q_ref[pl.ds(off, t), :]
            dov = do_ref[pl.ds(off, t), :]
            st = lax.dot_general(kv, qv, (((1,), (1,)), ((), ())), preferred_element_type=F32)
            if use_f:
                st = st + (fr_ref[0, i] - _lanes(fc_b, t))
            if masked:
                st = jnp.where(_causal(t, False), st, NEG_BIG)
            pt = jnp.exp2(st - lse_ref[0, i])
            dv_s[...] += jnp.dot(pt.astype(BF16), dov, preferred_element_type=F32)
            dpt = lax.dot_general(vv, dov, (((1,), (1,)), ((), ())), preferred_element_type=F32)
            dst = pt * (dpt - dl_ref[0, i])
            dk_s[...] += jnp.dot(dst.astype(BF16), qv, preferred_element_type=F32)
            if use_f:
                df_s[...] -= _fold_lanes(dst)

        step(kj, True)

        def loop_body(i, carry):
            step(i, False)
            return carry

        lax.fori_loop(kj + 1, nq, loop_body, 0)
        dk_ref[...] = (dk_s[...] * scale).astype(dk_ref.dtype)
        dv_ref[...] = dv_s[...].astype(dv_ref.dtype)
        if use_f:
            df_ref[0] = jnp.sum(df_s[...], axis=-1, keepdims=True)

    kspec = pl.BlockSpec((t, HEAD_PAD), lambda h, j: (j, h))
    qspec = pl.BlockSpec((s, HEAD_PAD), lambda h, j: (0, h))
    rowspec = pl.BlockSpec((1, nq, 1, t), lambda h, j: (h, 0, 0, 0))
    colspec = pl.BlockSpec((1, t, 1), lambda h, j: (h, j, 0))
    in_specs, args = [qspec, kspec, kspec, qspec, rowspec, rowspec], [q, k, v, do, lse_row, delta_row]
    out_specs = [kspec, kspec]
    out_shape = [jax.ShapeDtypeStruct((s, N_HEADS * HEAD_PAD), out_dtype)] * 2
    scratch = [pltpu.VMEM((t, HEAD_PAD), F32), pltpu.VMEM((t, HEAD_PAD), F32)]
    if use_f:
        in_specs += [colspec, rowspec]
        args += [fcol, frow]
        out_specs += [colspec]
        out_shape += [jax.ShapeDtypeStruct((N_HEADS, s, 1), F32)]
        scratch += [pltpu.VMEM((t, HEAD_PAD), F32)]
    return pl.pallas_call(
        body, name=name, grid=(N_HEADS, nq),
        in_specs=in_specs, out_specs=tuple(out_specs), out_shape=tuple(out_shape),
        scratch_shapes=scratch,
        compiler_params=_params(("parallel", "arbitrary")),
    )(*args)


def _attn_bwd(q, k, v, do, lse_row, delta_row, fcol, frow, scale_q, scale_k, name, out_dtype):
    s = q.shape[0]
    t = ATT_T
    nq = s // t
    use_f = fcol is not None

    def body(*refs):
        if use_f:
            (q_ref, k_ref, v_ref, do_ref, lse_ref, dl_ref, fc_ref, fr_ref,
             dq_ref, dk_ref, dv_ref, dr_ref, df_ref, dq_s, dk_s, dv_s, dr_s, df_s) = refs
        else:
            q_ref, k_ref, v_ref, do_ref, lse_ref, dl_ref, dq_ref, dk_ref, dv_ref, dq_s, dk_s, dv_s = refs
        kj = pl.program_id(1)

        @pl.when(kj == 0)
        def _():
            dq_s[...] = jnp.zeros(dq_s.shape, F32)
            if use_f:
                dr_s[...] = jnp.zeros(dr_s.shape, F32)

        kv, vv = k_ref[...], v_ref[...]
        dk_s[...] = jnp.zeros(dk_s.shape, F32)
        dv_s[...] = jnp.zeros(dv_s.shape, F32)
        if use_f:
            df_s[...] = jnp.zeros(df_s.shape, F32)
            fc_b = jnp.broadcast_to(fc_ref[0], (t, HEAD_PAD))

        def step(i, masked):
            off = pl.multiple_of(i * t, t)
            qv = q_ref[pl.ds(off, t), :]
            dov = do_ref[pl.ds(off, t), :]
            st = lax.dot_general(kv, qv, (((1,), (1,)), ((), ())), preferred_element_type=F32)
            if use_f:
                st = st + (fr_ref[0, i] - _lanes(fc_b, t))
            if masked:
                st = jnp.where(_causal(t, False), st, NEG_BIG)
            pt = jnp.exp2(st - lse_ref[0, i])
            dv_s[...] += jnp.dot(pt.astype(BF16), dov, preferred_element_type=F32)
            dpt = lax.dot_general(vv, dov, (((1,), (1,)), ((), ())), preferred_element_type=F32)
            dst = pt * (dpt - dl_ref[0, i])
            dsb = dst.astype(BF16)
            dk_s[...] += jnp.dot(dsb, qv, preferred_element_type=F32)
            dq_s[pl.ds(off, t), :] += lax.dot_general(dsb, kv, (((0,), (0,)), ((), ())), preferred_element_type=F32)
            if use_f:
                df_s[...] -= _fold_lanes(dst)
                dr_s[i] += jnp.sum(dst, axis=0, keepdims=True)

        step(kj, True)

        def loop_body(i, carry):
            step(i, False)
            return carry

        lax.fori_loop(kj + 1, nq, loop_body, 0)
        dk_ref[...] = (dk_s[...] * scale_k).astype(dk_ref.dtype)
        dv_ref[...] = dv_s[...].astype(dv_ref.dtype)
        if use_f:
            df_ref[0] = jnp.sum(df_s[...], axis=-1, keepdims=True)

        @pl.when(kj == nq - 1)
        def _():
            dq_ref[...] = (dq_s[...] * scale_q).astype(dq_ref.dtype)
            if use_f:
                dr_ref[0] = dr_s[...]

    kspec = pl.BlockSpec((t, HEAD_PAD), lambda h, j: (j, h))
    qspec = pl.BlockSpec((s, HEAD_PAD), lambda h, j: (0, h))
    rowspec = pl.BlockSpec((1, nq, 1, t), lambda h, j: (h, 0, 0, 0))
    colspec = pl.BlockSpec((1, t, 1), lambda h, j: (h, j, 0))
    in_specs, args = [qspec, kspec, kspec, qspec, rowspec, rowspec], [q, k, v, do, lse_row, delta_row]
    full = jax.ShapeDtypeStruct((s, N_HEADS * HEAD_PAD), out_dtype)
    out_specs, out_shape = [qspec, kspec, kspec], [full, full, full]
    scratch = [pltpu.VMEM((s, HEAD_PAD), F32), pltpu.VMEM((t, HEAD_PAD), F32), pltpu.VMEM((t, HEAD_PAD), F32)]
    if use_f:
        in_specs += [colspec, rowspec]
        args += [fcol, frow]
        out_specs += [rowspec, colspec]
        out_shape += [jax.ShapeDtypeStruct((N_HEADS, nq, 1, t), F32), jax.ShapeDtypeStruct((N_HEADS, s, 1), F32)]
        scratch += [pltpu.VMEM((nq, 1, t), F32), pltpu.VMEM((t, HEAD_PAD), F32)]
    return pl.pallas_call(
        body, name=name, grid=(N_HEADS, nq),
        in_specs=in_specs, out_specs=tuple(out_specs), out_shape=tuple(out_shape),
        scratch_shapes=scratch,
        compiler_params=_params(("parallel", "arbitrary")),
    )(*args)


def _gate_fwd(pm, pf, proj, name):
    s, w = pm.shape
    tm = _tile(s, ROW_T)

    def body(pm_ref, pf_ref, gm_ref, gf_ref, y_ref):
        y = jax.nn.sigmoid(gm_ref[...]) * pm_ref[...] + jax.nn.sigmoid(gf_ref[...]) * pf_ref[...]
        y_ref[...] = y.astype(y_ref.dtype)

    row = pl.BlockSpec((tm, w), lambda i: (i, 0))
    return pl.pallas_call(
        body, name=name, grid=(s // tm,),
        in_specs=[row, row, pl.BlockSpec((tm, w), lambda i: (i, P_GM // D_MODEL)),
                  pl.BlockSpec((tm, w), lambda i: (i, P_GF // D_MODEL))],
        out_specs=row, out_shape=jax.ShapeDtypeStruct((s, w), BF16),
        compiler_params=_params(("parallel",)),
    )(pm, pf, proj, proj)


def _gate_bwd(dy, pm, pf, proj, name):
    s, w = pm.shape
    tm = _tile(s, ROW_T)

    def body(dy_ref, pm_ref, pf_ref, gm_ref, gf_ref, dpm_ref, dpf_ref, dgm_ref, dgf_ref):
        dyv = dy_ref[...]
        sm, sf = jax.nn.sigmoid(gm_ref[...]), jax.nn.sigmoid(gf_ref[...])
        dpm_ref[...] = (dyv * sm).astype(BF16)
        dpf_ref[...] = (dyv * sf).astype(BF16)
        dgm_ref[...] = (dyv * pm_ref[...] * (sm * (1.0 - sm))).astype(BF16)
        dgf_ref[...] = (dyv * pf_ref[...] * (sf * (1.0 - sf))).astype(BF16)

    row = pl.BlockSpec((tm, w), lambda i: (i, 0))
    out = jax.ShapeDtypeStruct((s, w), BF16)
    return pl.pallas_call(
        body, name=name, grid=(s // tm,),
        in_specs=[row, row, row, pl.BlockSpec((tm, w), lambda i: (i, P_GM // D_MODEL)),
                  pl.BlockSpec((tm, w), lambda i: (i, P_GF // D_MODEL))],
        out_specs=(row, row, row, row), out_shape=(out, out, out, out),
        compiler_params=_params(("parallel",)),
    )(dy, pm, pf, proj, proj)


CONV_TN = 256
CONV_TM = 512
HALO = BF16_ROWS


def _shift_down(u, prev, n):
    rolled = pltpu.roll(u, n, 0)
    prev_rolled = pltpu.roll(prev, n, 0)
    top = jnp.concatenate([prev_rolled, rolled[HALO:]], axis=0)
    row = lax.broadcasted_iota(jnp.int32, u.shape, 0)
    return jnp.where(row < n, top, rolled)


def _conv_tile(u, prev, w_ref, b_ref):
    um1 = _shift_down(u, prev, 1)
    um2 = _shift_down(u, prev, 2)
    uc = b_ref[...] + w_ref[0:1, :] * um2 + w_ref[1:2, :] * um1 + w_ref[2:3, :] * u
    return uc, um1, um2


def _conv_specs(tm, tn, ncol_off):
    blk = lambda off: pl.BlockSpec((tm, tn), lambda j, i: (i, j + off))
    halo = lambda off: pl.BlockSpec((HALO, tn), lambda j, i: (jnp.maximum(i * (tm // HALO) - 1, 0), j + off))
    wsp = lambda off: pl.BlockSpec((3, tn), lambda j, i: (0, j + off))
    bsp = lambda off: pl.BlockSpec((1, tn), lambda j, i: (0, j + off))
    return blk, halo, wsp, bsp


def _convglu_fwd(u, conv_w, conv_b, name):
    s = u.shape[0]
    tm, tn = _tile(s, CONV_TM), CONV_TN
    nj = D_FF // tn
    blk, halo, wsp, bsp = _conv_specs(tm, tn, nj)

    def body(ug_ref, pg_ref, uv_ref, pv_ref, wg_ref, wv_ref, bg_ref, bv_ref, a_ref):
        live = (pl.program_id(1) > 0).astype(F32)
        gate, _, _ = _conv_tile(ug_ref[...].astype(F32), pg_ref[...].astype(F32) * live, wg_ref, bg_ref)
        val, _, _ = _conv_tile(uv_ref[...].astype(F32), pv_ref[...].astype(F32) * live, wv_ref, bv_ref)
        a_ref[...] = (gate * jax.nn.sigmoid(gate) * val).astype(a_ref.dtype)

    return pl.pallas_call(
        body, name=name, grid=(nj, s // tm),
        in_specs=[blk(0), halo(0), blk(nj), halo(nj), wsp(0), wsp(nj), bsp(0), bsp(nj)],
        out_specs=blk(0), out_shape=jax.ShapeDtypeStruct((s, D_FF), BF16),
        compiler_params=_params(("parallel", "arbitrary")),
    )(u, u, u, u, conv_w, conv_w, conv_b, conv_b)


def _convglu_bwd(da, u, conv_w, conv_b, name):
    s = u.shape[0]
    tm, tn = _tile(s, CONV_TM), CONV_TN
    nj = D_FF // tn
    blk, halo, wsp, bsp = _conv_specs(tm, tn, nj)

    def body(da_ref, ug_ref, pg_ref, uv_ref, pv_ref, wg_ref, wv_ref, bg_ref, bv_ref,
             dg_ref, dv_ref, sg_ref, sv_ref):
        live = (pl.program_id(1) > 0).astype(F32)
        ug, uv = ug_ref[...].astype(F32), uv_ref[...].astype(F32)
        gate, ug1, ug2 = _conv_tile(ug, pg_ref[...].astype(F32) * live, wg_ref, bg_ref)
        val, uv1, uv2 = _conv_tile(uv, pv_ref[...].astype(F32) * live, wv_ref, bv_ref)
        dav = da_ref[...].astype(F32)
        sig = jax.nn.sigmoid(gate)
        dgate = dav * val * (sig * (1.0 + gate * (1.0 - sig)))
        dval = dav * (gate * sig)
        dg_ref[...] = dgate.astype(dg_ref.dtype)
        dv_ref[...] = dval.astype(dv_ref.dtype)

        @pl.when(pl.program_id(1) == 0)
        def _():
            sg_ref[...] = jnp.zeros_like(sg_ref)
            sv_ref[...] = jnp.zeros_like(sv_ref)

        for s_ref, d, taps in ((sg_ref, dgate, (ug2, ug1, ug)), (sv_ref, dval, (uv2, uv1, uv))):
            for r, tap in enumerate(taps):
                s_ref[r:r + 1, :] += jnp.sum(d * tap, axis=0, keepdims=True)
            s_ref[3:4, :] += jnp.sum(d, axis=0, keepdims=True)

    sums = lambda off: pl.BlockSpec((8, tn), lambda j, i: (0, j + off))
    return pl.pallas_call(
        body, name=name, grid=(nj, s // tm),
        in_specs=[blk(0), blk(0), halo(0), blk(nj), halo(nj), wsp(0), wsp(nj), bsp(0), bsp(nj)],
        out_specs=(blk(0), blk(0), sums(0), sums(0)),
        out_shape=(jax.ShapeDtypeStruct((s, D_FF), BF16), jax.ShapeDtypeStruct((s, D_FF), BF16),
                   jax.ShapeDtypeStruct((8, D_FF), F32), jax.ShapeDtypeStruct((8, D_FF), F32)),
        compiler_params=_params(("parallel", "arbitrary")),
    )(da, u, u, u, u, conv_w, conv_w, conv_b, conv_b)


def _conv_transpose(d, conv_w_half, name):
    s, w = d.shape
    tm, tn = _tile(s, CONV_TM), CONV_TN
    last = s // tm - 1

    def body(d_ref, nx_ref, w_ref, o_ref):
        dv = d_ref[...].astype(F32)
        nxt = nx_ref[...].astype(F32) * (pl.program_id(1) < last).astype(F32)
        row = lax.broadcasted_iota(jnp.int32, dv.shape, 0)

        def shift_up(n):
            rolled = pltpu.roll(dv, tm - n, 0)
            nxt_rolled = pltpu.roll(nxt, HALO - n, 0)
            bottom = jnp.concatenate([rolled[:tm - HALO], nxt_rolled], axis=0)
            return jnp.where(row >= tm - n, bottom, rolled)

        out = w_ref[2:3, :] * dv + w_ref[1:2, :] * shift_up(1) + w_ref[0:1, :] * shift_up(2)
        o_ref[...] = out.astype(o_ref.dtype)

    blk = pl.BlockSpec((tm, tn), lambda j, i: (i, j))
    nxt_spec = pl.BlockSpec((HALO, tn), lambda j, i: (jnp.minimum((i + 1) * (tm // HALO), s // HALO - 1), j))
    return pl.pallas_call(
        body, name=name, grid=(w // tn, s // tm),
        in_specs=[blk, nxt_spec, pl.BlockSpec((3, tn), lambda j, i: (0, j))],
        out_specs=blk, out_shape=jax.ShapeDtypeStruct((s, w), BF16),
        compiler_params=_params(("parallel", "arbitrary")),
    )(d, d, conv_w_half)


def _split3(a):
    a1 = a.astype(BF16)
    r1 = a - a1.astype(F32)
    a2 = r1.astype(BF16)
    a3 = (r1 - a2.astype(F32)).astype(BF16)
    return a1, a2, a3


def _ones_dot_right(a, mat):
    return sum(jnp.dot(p, mat, preferred_element_type=F32) for p in _split3(a))


def _ones_dot_left(mat, a):
    return sum(jnp.dot(mat, p, preferred_element_type=F32) for p in _split3(a))


def _tri(n, cmp):
    r = lax.broadcasted_iota(jnp.int32, (n, n), 0)
    c = lax.broadcasted_iota(jnp.int32, (n, n), 1)
    return cmp(r, c).astype(BF16)


def _forget_fwd(z, bias, name):
    nh, nr, nl = z.shape

    def body(z_ref, b_ref, f_ref):
        within = _tri(nl, lambda r, c: r <= c)
        before = _tri(nr, lambda r, c: c < r)
        for h in range(nh):
            x = z_ref[h] + b_ref[h]
            lf = jnp.minimum(x, 0.0) - jnp.log(1.0 + jnp.exp(-jnp.abs(x)))
            pre = _ones_dot_right(lf, within)
            tot = jnp.zeros((nr, nl), F32) + jnp.sum(lf, axis=1, keepdims=True)
            f_ref[h] = pre + _ones_dot_left(before, tot)

    return pl.pallas_call(
        body, name=name, out_shape=jax.ShapeDtypeStruct(z.shape, F32),
        compiler_params=pltpu.CompilerParams(vmem_limit_bytes=VMEM_LIMIT_BYTES),
    )(z, bias)


def _forget_bwd(df_rows, df_cols, z, bias, name):
    nh, nr, nl = z.shape

    def body(dfr_ref, dfc_ref, z_ref, b_ref, dz_ref, db_ref):
        within = _tri(nl, lambda r, c: r >= c)
        after = _tri(nr, lambda r, c: c > r)
        for h in range(nh):
            g = dfr_ref[h] + dfc_ref[h]
            suf = _ones_dot_right(g, within)
            tot = jnp.zeros((nr, nl), F32) + jnp.sum(g, axis=1, keepdims=True)
            dlf = suf + _ones_dot_left(after, tot)
            dz = dlf * jax.nn.sigmoid(-(z_ref[h] + b_ref[h]))
            dz_ref[h] = dz
            db_ref[h] = jnp.zeros((1, nl), F32) + jnp.sum(dz)

    return pl.pallas_call(
        body, name=name,
        out_shape=(jax.ShapeDtypeStruct(z.shape, F32), jax.ShapeDtypeStruct(bias.shape, F32)),
        compiler_params=pltpu.CompilerParams(vmem_limit_bytes=VMEM_LIMIT_BYTES),
    )(df_rows, df_cols, z, bias)


def _ada_fwd(c_col, w, b, name):
    kdim, n = w.shape

    def body(c_ref, w_ref, b_ref, ada_ref, act_ref):
        wv = w_ref[...]
        for e in range(N_DEV):
            cv = c_ref[e]
            act = cv * jax.nn.sigmoid(cv)
            act_ref[e] = act
            ada_ref[e:e + 1, :] = jnp.sum(act * wv, axis=0, keepdims=True) + b_ref[...]

    return pl.pallas_call(
        body, name=name,
        out_shape=(jax.ShapeDtypeStruct((N_DEV, n), F32), jax.ShapeDtypeStruct((N_DEV, kdim, 1), F32)),
        compiler_params=pltpu.CompilerParams(vmem_limit_bytes=VMEM_LIMIT_BYTES),
    )(c_col, w, b)


def _ada_bwd(act_col, dada, name):
    kdim = act_col.shape[1]
    n = dada.shape[1]

    def body(act_ref, d_ref, g_ref):
        acc = act_ref[0] * d_ref[0:1, :]
        for e in range(1, N_DEV):
            acc = acc + act_ref[e] * d_ref[e:e + 1, :]
        g_ref[...] = acc

    return pl.pallas_call(
        body, name=name, out_shape=jax.ShapeDtypeStruct((kdim, n), F32),
        compiler_params=pltpu.CompilerParams(vmem_limit_bytes=VMEM_LIMIT_BYTES),
    )(act_col, dada)


def _adamw(parts, w, m, v, name, tr=128):
    npart, r, c = parts.shape
    tr = _tile(r, tr, step=BF16_ROWS) if r % BF16_ROWS == 0 else r

    def body(p_ref, w_ref, m_ref, v_ref, g_ref, d_ref, nm_ref, nv_ref):
        g = p_ref[0].astype(F32)
        for e in range(1, npart):
            g = g + p_ref[e].astype(F32)
        nm = ADAM_B1 * m_ref[...] + (1.0 - ADAM_B1) * g
        nv = ADAM_B2 * v_ref[...] + (1.0 - ADAM_B2) * (g * g)
        m_hat = nm / (1.0 - ADAM_B1 ** ADAM_STEP)
        v_hat = nv / (1.0 - ADAM_B2 ** ADAM_STEP)
        g_ref[...] = g
        d_ref[...] = -ADAM_LR * (m_hat / (jnp.sqrt(v_hat) + ADAM_EPS) + ADAM_WD * w_ref[...])
        nm_ref[...] = nm
        nv_ref[...] = nv

    row = pl.BlockSpec((tr, c), lambda i: (i, 0))
    out = jax.ShapeDtypeStruct((r, c), F32)
    return pl.pallas_call(
        body, name=name, grid=(r // tr,),
        in_specs=[pl.BlockSpec((npart, tr, c), lambda i: (0, i, 0)), row, row, row],
        out_specs=(row, row, row, row), out_shape=(out, out, out, out),
        compiler_params=_params(("parallel",)),
    )(parts, w, m, v)


BIG = ("w_in", "w_uq", "w_ukv", "w_o_mla", "w_o_fox", "w_out", "w_up", "conv_w", "w_down")


def _cols_to_full(stack):
    n, r, c = stack.shape
    return stack.transpose(1, 0, 2).reshape(r, n * c)


def _full_to_cols(full, c):
    r = full.shape[0]
    return full.reshape(r, N_DEV, c).transpose(1, 0, 2)


def _pad_heads(a, width, ones_lane=False):
    s = a.shape[0]
    a = a.reshape(s, N_HEADS, width)
    if ones_lane:
        assert width == SUM_LANE
        tail = jnp.zeros((s, N_HEADS, HEAD_PAD - width), a.dtype).at[:, :, 0].set(1.0)
        return jnp.concatenate([a, tail], axis=2).reshape(s, N_HEADS * HEAD_PAD)
    return jnp.pad(a, ((0, 0), (0, 0), (0, HEAD_PAD - width))).reshape(s, N_HEADS * HEAD_PAD)


def _unpad_heads(a, width):
    s = a.shape[0]
    return a.reshape(s, N_HEADS, HEAD_PAD)[:, :, :width].reshape(s, N_HEADS * width)


def _w_in_padded(w_in):
    seg = [w_in[:, IN_OFF[i]:IN_OFF[i + 1]] for i in range(9)]
    cq, ckv, kr, fq, fk, fv, fl, gm, gf = seg
    padc = lambda a, n: jnp.pad(a, ((0, 0), (0, n - a.shape[1])))
    return jnp.concatenate([gm, gf, fq, fk, fv, cq, ckv, padc(kr, 128), padc(fl, 128)], axis=1)


def _w_in_unpadded(g):
    return jnp.concatenate([
        g[:, P_CQ:P_CQ + 384], g[:, P_CKV:P_CKV + 256], g[:, P_KR:P_KR + 32], g[:, P_FQ:P_FQ + 512],
        g[:, P_FK:P_FK + 512], g[:, P_FV:P_FV + 512], g[:, P_FL:P_FL + 8], g[:, P_GM:P_GM + 1024],
        g[:, P_GF:P_GF + 1024]], axis=1)


SMALL = (("b_ada", 6144, 6144), ("norm_mix_g", 1024, 1024), ("q_norm_g", 384, 384), ("kv_norm_g", 256, 256),
         ("b_forget", 8, 128), ("norm_ffn_g", 1024, 1024), ("conv_b", 5632, 5632), ("norm_final_g", 1024, 1024),
         ("loss", 1, 128))
SMALL_OFF = {}
_o = 0
for _n, _real, _padded in SMALL:
    SMALL_OFF[_n] = _o
    _o += _padded
SMALL_W = _o


def _pack_small(vals):
    parts = []
    for nme, real, padded in SMALL:
        a = vals[nme].reshape(1, real).astype(F32)
        parts.append(jnp.pad(a, ((0, 0), (0, padded - real))))
    return jnp.concatenate(parts, axis=1)


def kernel(x, c, positions, w_ada, b_ada, norm_mix_g, w_in, q_norm_g, w_uq, kv_norm_g, w_ukv, b_forget, w_o_mla, w_o_fox, w_out, norm_ffn_g, w_up, conv_w, conv_b, w_down, norm_final_g, loss_target, m_w_ada, m_b_ada, m_norm_mix_g, m_w_in, m_q_norm_g, m_w_uq, m_kv_norm_g, m_w_ukv, m_b_forget, m_w_o_mla, m_w_o_fox, m_w_out, m_norm_ffn_g, m_w_up, m_conv_w, m_conv_b, m_w_down, m_norm_final_g, v_w_ada, v_b_ada, v_norm_mix_g, v_w_in, v_q_norm_g, v_w_uq, v_kv_norm_g, v_w_ukv, v_b_forget, v_w_o_mla, v_w_o_fox, v_w_out, v_norm_ffn_g, v_w_up, v_conv_w, v_conv_b, v_w_down, v_norm_final_g):
    me = 4 * lax.axis_index("x") + 2 * lax.axis_index("y") + lax.axis_index("c")
    x = x[0]
    target = loss_target[0]
    s = x.shape[0]
    nblk = s // ATT_T
    big_w = {"w_in": w_in, "w_uq": w_uq, "w_ukv": w_ukv, "w_o_mla": w_o_mla, "w_o_fox": w_o_fox,
             "w_out": w_out, "w_up": w_up, "conv_w": conv_w, "w_down": w_down}
    big_m = {"w_in": m_w_in, "w_uq": m_w_uq, "w_ukv": m_w_ukv, "w_o_mla": m_w_o_mla, "w_o_fox": m_w_o_fox,
             "w_out": m_w_out, "w_up": m_w_up, "conv_w": m_conv_w, "w_down": m_w_down}
    big_v = {"w_in": v_w_in, "w_uq": v_w_uq, "w_ukv": v_w_ukv, "w_o_mla": v_w_o_mla, "w_o_fox": v_w_o_fox,
             "w_out": v_w_out, "w_up": v_w_up, "conv_w": v_conv_w, "w_down": v_w_down}

    gathered = _all_gather([big_w[k][0] if k == "conv_w" else big_w[k][0].astype(BF16) for k in BIG], "gather_weights")
    st = dict(zip(BIG, gathered))
    w_in_p = _w_in_padded(_cols_to_full(st["w_in"]))
    uq = st["w_uq"]
    w_uq_p = jnp.pad(uq, ((0, 0), (0, 0), (0, HEAD_PAD - 96))).transpose(1, 0, 2).reshape(MLA_Q_RANK, 1024)
    ukv = st["w_ukv"]
    zeros64 = jnp.zeros((N_HEADS, MLA_KV_RANK, 64), BF16)
    w_uk_p = jnp.concatenate([ukv[:, :, :64], zeros64], axis=2).transpose(1, 0, 2).reshape(MLA_KV_RANK, 1024)
    w_uv_p = jnp.concatenate([ukv[:, :, 64:], zeros64], axis=2).transpose(1, 0, 2).reshape(MLA_KV_RANK, 1024)
    place = np.zeros((HEAD_PAD, N_HEADS, HEAD_PAD), np.float32)
    for j in range(MLA_ROPE):
        place[j, :, MLA_NOPE + j] = 1.0
    place = jnp.asarray(place.reshape(HEAD_PAD, 1024), BF16)
    w_kv_comb = jnp.concatenate([
        jnp.concatenate([w_uk_p, w_uv_p], axis=1),
        jnp.concatenate([place, jnp.zeros((HEAD_PAD, 1024), BF16)], axis=1)], axis=0)
    pad_o = lambda full: jnp.pad(full.reshape(N_HEADS, 64, 1024), ((0, 0), (0, 64), (0, 0))).reshape(1024, 1024)
    w_o_mla_p = pad_o(_cols_to_full(st["w_o_mla"]))
    w_o_fox_p = pad_o(_cols_to_full(st["w_o_fox"]))
    w_out_f = st["w_out"].reshape(1024, 1024)
    w_up_f = _cols_to_full(st["w_up"])
    conv_w_f = _cols_to_full(st["conv_w"])
    w_down_f = st["w_down"].reshape(D_FF, 1024)

    (c_all,) = _all_gather([c], "gather_c")
    b_ada_mine = lax.dynamic_slice(b_ada, (0, me * 768), (1, 768))
    ada_cols, act_col = _ada_fwd(c_all.reshape(N_DEV, D_MODEL, 1), w_ada[0], b_ada_mine, "ada_fwd")
    (ada_all,) = _all_gather([ada_cols], "gather_ada")
    ada = lax.dynamic_slice(ada_all, (0, me, 0), (N_DEV, 1, 768)).reshape(1, N_ADA * D_MODEL)
    sh_m, sc_m, g_m, sh_f, sc_f, g_f = [ada[:, i * D_MODEL:(i + 1) * D_MODEL] for i in range(N_ADA)]

    inv_freq = ROPE_THETA ** (-jnp.arange(0, MLA_ROPE, 2, dtype=F32) / MLA_ROPE)
    ang = positions[0].astype(F32)[:, None] * inv_freq
    cos, sin = jnp.cos(ang), jnp.sin(ang)
    rope_c = jnp.concatenate([jnp.ones((s, 64), F32), cos, cos, jnp.zeros((s, 32), F32)], axis=1)
    rope_s = jnp.concatenate([jnp.zeros((s, 64), F32), -sin, sin, jnp.zeros((s, 32), F32)], axis=1)

    zero_d = jnp.zeros((1, D_MODEL), F32)

    h1 = _rms_mod(x, norm_mix_g, sc_m, sh_m, "norm_mix")
    proj = _mm(h1, w_in_p, "nn", F32, "proj_in", tn=640)
    cq = proj[:, P_CQ:P_CQ + 384]
    ckv = proj[:, P_CKV:P_CKV + 256]
    qn = _rms_mod(cq, q_norm_g, jnp.zeros((1, 384), F32), jnp.zeros((1, 384), F32), "q_norm")
    kvn = _rms_mod(ckv, kv_norm_g, jnp.zeros((1, 256), F32), jnp.zeros((1, 256), F32), "kv_norm")
    kv_in = jnp.concatenate([kvn, proj[:, P_KR:P_KR + 128].astype(BF16)], axis=1)
    q_pre = _mm(qn, w_uq_p, "nn", F32, "q_up")
    kv_pre = _mm(kv_in, w_kv_comb, "nn", F32, "kv_up")
    q_fold = MLA_SCALE * LOG2E
    (q_att,) = _rope(q_pre, rope_c * q_fold, rope_s * q_fold, "rope_q", N_HEADS, (BF16,))
    k_att, v_att = _rope(kv_pre, rope_c, rope_s, "rope_kv", N_HEADS, (BF16, BF16))
    o_mla, o_mla_b, lse_mla = _attn_fwd(q_att, k_att, v_att, None, None, "mla_fwd")

    fq = _pad_heads(proj[:, P_FQ:P_FQ + 512] * (FOX_SCALE * LOG2E), 64).astype(BF16)
    fk = _pad_heads(proj[:, P_FK:P_FK + 512], 64).astype(BF16)
    fv = _pad_heads(proj[:, P_FV:P_FV + 512], 64, ones_lane=True).astype(BF16)
    z = proj[:, P_FL:P_FL + 8].T.reshape(N_HEADS, s // SEQ_LANES, SEQ_LANES)
    bias_f = jnp.broadcast_to(b_forget.reshape(N_HEADS, 1, 1), (N_HEADS, 1, SEQ_LANES))
    f_cum = _forget_fwd(z, bias_f, "forget_fwd")
    f_col = (f_cum * LOG2E).reshape(N_HEADS, s, 1)
    f_row = f_col.reshape(N_HEADS, nblk, 1, ATT_T)
    o_fox, o_fox_b, lse_fox = _attn_fwd(fq, fk, fv, f_col, f_row, "fox_fwd")

    pm = _mm(o_mla_b, w_o_mla_p, "nn", F32, "o_mla_proj")
    pf = _mm(o_fox_b, w_o_fox_p, "nn", F32, "o_fox_proj")
    y = _gate_fwd(pm, pf, proj, "gate_fwd")
    x2, mix = _mm(y, w_out_f, "nn", F32, "out_proj", res=x, gvec=g_m)

    h2 = _rms_mod(x2, norm_ffn_g, sc_f, sh_f, "norm_ffn")
    u = _mm(h2, w_up_f, "nn", BF16, "ffn_up")
    a = _convglu_fwd(u, conv_w_f, conv_b, "convglu_fwd")
    x3, ffn = _mm(a, w_down_f, "nn", F32, "ffn_down", res=x2, gvec=g_f, tk=2816)

    dx3, sums_final = _final_loss(x3, target, norm_final_g.reshape(1, D_MODEL), "final_loss")
    dffn, sums_gf = _scale_bwd(dx3, ffn, g_f, "ffn_scale_bwd")
    da = _mm(dffn, w_down_f, "nt", BF16, "ffn_down_dx", tn=1408)
    g_w_down = _mm(a.T, dffn, "nn", F32, "ffn_down_dw", tm=1408, tn=1024)
    dgate, dval, s_gate, s_val = _convglu_bwd(da, u, conv_w_f, conv_b, "convglu_bwd")
    du = jnp.concatenate([_conv_transpose(dgate, conv_w_f[:, :D_FF], "conv_t_gate"),
                          _conv_transpose(dval, conv_w_f[:, D_FF:], "conv_t_val")], axis=1)
    dh2 = _mm(du, w_up_f, "nt", F32, "ffn_up_dx", tn=1024, tk=1408)
    g_w_up = _mm(h2.T, du, "nn", F32, "ffn_up_dw")
    dx2, sums_ffn = _rms_mod_bwd(dh2, x2, norm_ffn_g, sc_f, dx3, "norm_ffn_bwd")

    dmix, sums_gm = _scale_bwd(dx2, mix, g_m, "mix_scale_bwd")
    dy = _mm(dmix, w_out_f, "nt", F32, "out_proj_dx")
    g_w_out = _mm(y.T, dmix, "nn", F32, "out_proj_dw")
    dpm, dpf, dgm, dgf = _gate_bwd(dy, pm, pf, proj, "gate_bwd")
    do_mla_b = _mm(dpm, w_o_mla_p, "nt", BF16, "o_mla_dx", tn=1024)
    do_fox_b = _mm(dpf, w_o_fox_p, "nt", BF16, "o_fox_dx", tn=1024)
    g_w_o_mla_p = _mm(o_mla_b.T, dpm, "nn", F32, "o_mla_dw")
    g_w_o_fox_p = _mm(o_fox_b.T, dpf, "nn", F32, "o_fox_dw")

    rows = lambda col: col.reshape(N_HEADS, nblk, 1, ATT_T)
    delta_mla = _attn_delta(o_mla, do_mla_b, "mla_delta")
    dq_rot, dk_rot, dv_mla = _attn_bwd(q_att, k_att, v_att, do_mla_b, rows(lse_mla), rows(delta_mla), None, None,
                                       MLA_SCALE, 1.0 / LOG2E, "mla_bwd", F32)
    (dq_pre,) = _rope(dq_rot, rope_c, -rope_s, "rope_q_bwd", N_HEADS, (BF16,))
    dkv_pre = _rope_bwd_kv(dk_rot, dv_mla, rope_c, -rope_s, "rope_kv_bwd")
    dqn = _mm(dq_pre, w_uq_p, "nt", F32, "q_up_dx")
    g_w_uq_p = _mm(qn.T, dq_pre, "nn", F32, "q_up_dw")
    dkv_in = _mm(dkv_pre, w_kv_comb, "nt", F32, "kv_up_dx")
    g_w_kv_comb = _mm(kv_in.T, dkv_pre, "nn", F32, "kv_up_dw")
    dcq, sums_q = _rms_mod_bwd(dqn, cq, q_norm_g, jnp.zeros((1, 384), F32), None, "q_norm_bwd")
    dckv, sums_kv = _rms_mod_bwd(dkv_in[:, :256], ckv, kv_norm_g, jnp.zeros((1, 256), F32), None, "kv_norm_bwd")
    delta_fox = _attn_delta(o_fox, do_fox_b, "fox_delta")
    dfq, dfk, dfv, dfr, dfc = _attn_bwd(fq, fk, fv, do_fox_b, rows(lse_fox), rows(delta_fox), f_col, f_row,
                                        FOX_SCALE, 1.0 / LOG2E, "fox_bwd", BF16)
    df_rows = dfr.reshape(N_HEADS, s // SEQ_LANES, SEQ_LANES)
    df_cols = dfc.reshape(N_HEADS, s // SEQ_LANES, SEQ_LANES)
    dz, db_f = _forget_bwd(df_rows, df_cols, z, bias_f, "forget_bwd")
    dfl = jnp.pad(dz.reshape(N_HEADS, s).T, ((0, 0), (0, 128 - N_HEADS)))

    dproj = jnp.concatenate([
        dgm, dgf, _unpad_heads(dfq, 64), _unpad_heads(dfk, 64), _unpad_heads(dfv, 64),
        dcq.astype(BF16), dckv.astype(BF16), dkv_in[:, 256:384].astype(BF16), dfl.astype(BF16)], axis=1)
    dh1 = _mm(dproj, w_in_p, "nt", F32, "proj_in_dx", tn=1024, tk=896)
    g_w_in_p = _mm(h1.T, dproj, "nn", F32, "proj_in_dw", tn=640)
    grad_x, sums_mix = _rms_mod_bwd(dh1, x, norm_mix_g, sc_m, dx2, "norm_mix_bwd")

    g_w_in = _w_in_unpadded(g_w_in_p)
    g_uq = g_w_uq_p.reshape(MLA_Q_RANK, N_HEADS, HEAD_PAD)[:, :, :96].transpose(1, 0, 2)
    g_uk = g_w_kv_comb[:256, :1024].reshape(256, N_HEADS, HEAD_PAD)[:, :, :64]
    g_uv = g_w_kv_comb[:256, 1024:].reshape(256, N_HEADS, HEAD_PAD)[:, :, :64]
    g_ukv = jnp.concatenate([g_uk, g_uv], axis=2).transpose(1, 0, 2)
    unpad_o = lambda g: g.reshape(N_HEADS, HEAD_PAD, 1024)[:, :64].reshape(512, 1024)
    g_conv_w = jnp.concatenate([s_gate[0:3], s_val[0:3]], axis=1)
    g_blocks = {
        "w_in": _full_to_cols(g_w_in, 533), "w_uq": g_uq, "w_ukv": g_ukv,
        "w_o_mla": _full_to_cols(unpad_o(g_w_o_mla_p), 128), "w_o_fox": _full_to_cols(unpad_o(g_w_o_fox_p), 128),
        "w_out": g_w_out.reshape(N_DEV, 128, 1024), "w_up": _full_to_cols(g_w_up, 704),
        "conv_w": _full_to_cols(g_conv_w, 704), "w_down": g_w_down.reshape(N_DEV, 352, 1024)}
    g_recv = _all_to_all([g_blocks[k].astype(BF16) for k in BIG], "scatter_grads")
    g_big, d_big, nm_big, nv_big = {}, {}, {}, {}
    for k, parts in zip(BIG, g_recv):
        g_big[k], d_big[k], nm_big[k], nv_big[k] = [
            t[None] for t in _adamw(parts, big_w[k][0], big_m[k][0], big_v[k][0], "adamw_" + k)]

    dada = jnp.concatenate([sums_mix[0:1], sums_mix[1:2], sums_gm[0:1], sums_ffn[0:1], sums_ffn[1:2], sums_gf[0:1]],
                           axis=1)
    small_part = _pack_small({
        "b_ada": dada, "norm_mix_g": sums_mix[2:3], "q_norm_g": sums_q[2:3], "kv_norm_g": sums_kv[2:3],
        "b_forget": db_f[:, 0, 0], "norm_ffn_g": sums_ffn[2:3],
        "conv_b": jnp.concatenate([s_gate[3:4], s_val[3:4]], axis=1), "norm_final_g": sums_final[0:1],
        "loss": sums_final[1:2, 0:1]})
    (small_all,) = _all_gather([small_part], "gather_small")
    zero1 = jnp.zeros((1,), F32)
    small_w = {"b_ada": b_ada, "norm_mix_g": norm_mix_g, "q_norm_g": q_norm_g, "kv_norm_g": kv_norm_g,
               "b_forget": b_forget, "norm_ffn_g": norm_ffn_g, "conv_b": conv_b, "norm_final_g": norm_final_g,
               "loss": zero1}
    small_m = {"b_ada": m_b_ada, "norm_mix_g": m_norm_mix_g, "q_norm_g": m_q_norm_g, "kv_norm_g": m_kv_norm_g,
               "b_forget": m_b_forget, "norm_ffn_g": m_norm_ffn_g, "conv_b": m_conv_b,
               "norm_final_g": m_norm_final_g, "loss": zero1}
    small_v = {"b_ada": v_b_ada, "norm_mix_g": v_norm_mix_g, "q_norm_g": v_q_norm_g, "kv_norm_g": v_kv_norm_g,
               "b_forget": v_b_forget, "norm_ffn_g": v_norm_ffn_g, "conv_b": v_conv_b,
               "norm_final_g": v_norm_final_g, "loss": zero1}
    g_sm, d_sm, nm_sm, nv_sm = _adamw(small_all, _pack_small(small_w), _pack_small(small_m), _pack_small(small_v),
                                      "adamw_small")
    loss = g_sm[0, SMALL_OFF["loss"]]

    dada_all = small_all[:, 0, SMALL_OFF["b_ada"]:SMALL_OFF["b_ada"] + N_ADA * D_MODEL]
    dada_mine = lax.dynamic_slice(dada_all, (0, me * 768), (N_DEV, 768))
    g_ada_local = _ada_bwd(act_col, dada_mine, "ada_bwd")
    g_ada, d_ada, nm_ada, nv_ada = _adamw(g_ada_local[None], w_ada[0], m_w_ada[0], v_w_ada[0], "adamw_ada")

    def small_out(t, nme, shape):
        real = dict((n_, r_) for n_, r_, _ in SMALL)[nme]
        o = SMALL_OFF[nme]
        return t[0, o:o + real].reshape(shape)

    order = ["w_ada", "b_ada", "norm_mix_g", "w_in", "q_norm_g", "w_uq", "kv_norm_g", "w_ukv", "b_forget",
             "w_o_mla", "w_o_fox", "w_out", "norm_ffn_g", "w_up", "conv_w", "conv_b", "w_down", "norm_final_g"]
    small_shapes = {"b_ada": (1, 6144), "norm_mix_g": (1, 1024), "q_norm_g": (1, 384), "kv_norm_g": (1, 256),
                    "b_forget": (1, 8), "norm_ffn_g": (1, 1024), "conv_b": (1, 5632), "norm_final_g": (1024,)}

    def family(big, small, ada_t):
        out = []
        for nme in order:
            if nme == "w_ada":
                out.append(ada_t[None])
            elif nme in small_shapes:
                out.append(small_out(small, nme, small_shapes[nme]))
            else:
                out.append(big[nme])
        return out

    return (loss, grad_x[None], *family(g_big, g_sm, g_ada), *family(d_big, d_sm, d_ada),
            *family(nm_big, nm_sm, nm_ada), *family(nv_big, nv_sm, nv_ada))
```

```python
import math

import numpy as np
import jax
import jax.numpy as jnp
from jax import lax
from jax.experimental import pallas as pl
from jax.experimental.pallas import tpu as pltpu

F32 = jnp.float32
BF16 = jnp.bfloat16

N_DEV = 8
D_MODEL = 1024
N_HEADS = 8
HEAD_PAD = 128
MLA_Q_RANK = 384
MLA_KV_RANK = 256
MLA_NOPE = 64
MLA_ROPE = 32
MLA_V = 64
FOX_DIM = 64
D_FF = 2816
N_ADA = 6
EPS = 1e-6
ROPE_THETA = 10000.0
MLA_SCALE = 1.0 / math.sqrt(MLA_NOPE + MLA_ROPE)
FOX_SCALE = 1.0 / math.sqrt(FOX_DIM)
IN_SPLITS = (384, 256, 32, 512, 512, 512, 8, 1024, 1024)
D_IN = sum(IN_SPLITS)
IN_OFF = tuple(int(v) for v in np.cumsum((0,) + IN_SPLITS))
P_GM, P_GF, P_FQ, P_FK, P_FV, P_CQ, P_CKV, P_KR, P_FL, D_IN_P = 0, 1024, 2048, 2560, 3072, 3584, 3968, 4224, 4352, 4480

ADAM_LR, ADAM_B1, ADAM_B2, ADAM_EPS, ADAM_WD, ADAM_STEP = 0.001, 0.9, 0.999, 1e-08, 0.01, 10

VMEM_LIMIT_BYTES = 56 * 1024 * 1024
NEG_BIG = -1e30
ATT_T = 512
LOG2E = 1.4426950408889634
SUM_LANE = 64
ROW_T = 256
SEQ_LANES = 128
BF16_ROWS = 16
FWD_HEADS_PER_STEP = 2


def _params(sem):
    return pltpu.CompilerParams(dimension_semantics=sem, vmem_limit_bytes=VMEM_LIMIT_BYTES)


def _tile(n, target, step=128):
    if n <= target:
        return n
    t = (target // step) * step
    while t >= step:
        if n % t == 0:
            return t
        t -= step
    return n


def _vec_spec(w, nargs):
    if nargs == 1:
        return pl.BlockSpec((1, w), lambda i: (0, 0))
    return pl.BlockSpec((1, w), lambda i, j: (0, 0))


def _comm_call(body, name, ins, out_shapes):
    n = len(ins)
    any_spec = pl.BlockSpec(memory_space=pl.ANY)
    return pl.pallas_call(
        body, name=name, out_shape=tuple(out_shapes),
        in_specs=[any_spec] * n, out_specs=tuple([any_spec] * n),
        scratch_shapes=[pltpu.SemaphoreType.DMA((n, 7)), pltpu.SemaphoreType.DMA((n, 7)),
                        pltpu.SemaphoreType.DMA((n,))],
    )(*ins)


def _all_gather(xs, name):
    n = len(xs)

    def body(*refs):
        x_refs, out_refs = refs[:n], refs[n:2 * n]
        send_sems, recv_sems, local_sems = refs[2 * n:]
        x_, y_, c_ = lax.axis_index("x"), lax.axis_index("y"), lax.axis_index("c")
        me, sibling = (x_, y_, c_), (x_, y_, 1 - c_)
        chips = [(1 - x_, y_), (x_, 1 - y_), (1 - x_, 1 - y_)]

        def slot(a, px, py, pc):
            return out_refs[a].at[4 * px + 2 * py + pc]

        def copy(a, k, block, to, src=None):
            return pltpu.make_async_remote_copy(
                src_ref=slot(a, *block) if src is None else src, dst_ref=slot(a, *block),
                send_sem=send_sems.at[a, k], recv_sem=recv_sems.at[a, k],
                device_id=to, device_id_type=pl.DeviceIdType.MESH)

        mine = [pltpu.make_async_copy(x_refs[a], slot(a, *me), local_sems.at[a]) for a in range(n)]
        for cp in mine:
            cp.start()
        first = []
        for a in range(n):
            first.append(copy(a, 0, me, sibling, src=x_refs[a]))
            first += [copy(a, 1 + j, me, (*chip, c_), src=x_refs[a]) for j, chip in enumerate(chips)]
        for cp in first:
            cp.start()
        passed = []
        for j, chip in enumerate(chips):
            for a in range(n):
                copy(a, 1 + j, (*chip, c_), me).wait_recv()
                passed.append(copy(a, 4 + j, (*chip, c_), sibling))
                passed[-1].start()
        for a in range(n):
            copy(a, 0, sibling, me).wait_recv()
            for j, chip in enumerate(chips):
                copy(a, 4 + j, (*chip, 1 - c_), me).wait_recv()
        for cp in first + passed:
            cp.wait_send()
        for cp in mine:
            cp.wait()

    return _comm_call(body, name, xs, [jax.ShapeDtypeStruct((N_DEV,) + x.shape, x.dtype) for x in xs])


def _all_to_all(gs, name):
    n = len(gs)

    def body(*refs):
        g_refs, out_refs = refs[:n], refs[n:2 * n]
        send_sems, recv_sems, local_sems = refs[2 * n:]
        x_, y_, c_ = lax.axis_index("x"), lax.axis_index("y"), lax.axis_index("c")
        me = 4 * x_ + 2 * y_ + c_

        def peer(k):
            return (x_ ^ ((k >> 2) & 1), y_ ^ ((k >> 1) & 1), c_ ^ (k & 1))

        def copy(a, k, sending):
            px, py, pc = peer(k)
            theirs = 4 * px + 2 * py + pc
            return pltpu.make_async_remote_copy(
                src_ref=g_refs[a].at[theirs if sending else me], dst_ref=out_refs[a].at[me if sending else theirs],
                send_sem=send_sems.at[a, k - 1], recv_sem=recv_sems.at[a, k - 1],
                device_id=(px, py, pc), device_id_type=pl.DeviceIdType.MESH)

        mine = [pltpu.make_async_copy(g_refs[a].at[me], out_refs[a].at[me], local_sems.at[a]) for a in range(n)]
        for cp in mine:
            cp.start()
        sends = [copy(a, k, True) for a in range(n) for k in range(1, N_DEV)]
        for cp in sends:
            cp.start()
        for a in range(n):
            for k in range(1, N_DEV):
                copy(a, k, False).wait_recv()
        for cp in sends:
            cp.wait_send()
        for cp in mine:
            cp.wait()

    return _comm_call(body, name, gs, [jax.ShapeDtypeStruct(g.shape, g.dtype) for g in gs])


def _mm(a, b, mode, out_dtype, name, res=None, gvec=None, tm=1024, tn=512, tk=1024):
    (k, m) = a.shape if mode == "tn" else a.shape[::-1]
    n = b.shape[0] if mode == "nt" else b.shape[1]
    tm, tn, tk = _tile(m, tm), _tile(n, tn), _tile(k, tk)
    nk = k // tk
    dims = {"nn": (((1,), (0,)), ((), ())), "nt": (((1,), (1,)), ((), ())), "tn": (((0,), (0,)), ((), ()))}[mode]
    fused = res is not None

    def body(*refs):
        acc_ref = refs[-1] if nk > 1 else None
        if fused:
            a_ref, b_ref, res_ref, g_ref, o_ref, raw_ref = refs[:6]
        else:
            a_ref, b_ref, o_ref = refs[:3]
        part = lax.dot_general(a_ref[...], b_ref[...], dims, preferred_element_type=F32)

        def finish(acc):
            if fused:
                raw_ref[...] = acc
                o_ref[...] = (res_ref[...] + g_ref[...] * acc).astype(o_ref.dtype)
            else:
                o_ref[...] = acc.astype(o_ref.dtype)

        if nk == 1:
            finish(part)
            return
        kk = pl.program_id(2)

        @pl.when(kk == 0)
        def _():
            acc_ref[...] = part

        @pl.when(kk > 0)
        def _():
            acc_ref[...] += part

        @pl.when(kk == nk - 1)
        def _():
            finish(acc_ref[...])

    if mode == "tn":
        a_spec = pl.BlockSpec((tk, tm), lambda i, j, kk: (kk, i))
    else:
        a_spec = pl.BlockSpec((tm, tk), lambda i, j, kk: (i, kk))
    if mode == "nt":
        b_spec = pl.BlockSpec((tn, tk), lambda i, j, kk: (j, kk))
    else:
        b_spec = pl.BlockSpec((tk, tn), lambda i, j, kk: (kk, j))
    o_spec = pl.BlockSpec((tm, tn), lambda i, j, kk: (i, j))
    in_specs, args = [a_spec, b_spec], [a, b]
    out_specs, out_shape = o_spec, jax.ShapeDtypeStruct((m, n), out_dtype)
    if fused:
        in_specs += [o_spec, pl.BlockSpec((1, tn), lambda i, j, kk: (0, j))]
        args += [res, gvec]
        out_specs = (o_spec, o_spec)
        out_shape = (out_shape, jax.ShapeDtypeStruct((m, n), F32))
    return pl.pallas_call(
        body, name=name, grid=(m // tm, n // tn, nk),
        in_specs=in_specs, out_specs=out_specs, out_shape=out_shape,
        scratch_shapes=[pltpu.VMEM((tm, tn), F32)] if nk > 1 else [],
        compiler_params=_params(("parallel", "parallel", "arbitrary")),
    )(*args)


def _rms_mod(x, g, sc, sh, name):
    s, w = x.shape
    tm = _tile(s, ROW_T)

    def body(x_ref, g_ref, sc_ref, sh_ref, o_ref):
        xv = x_ref[...]
        r = lax.rsqrt(jnp.mean(xv * xv, axis=-1, keepdims=True) + EPS)
        o_ref[...] = ((xv * r * g_ref[...]) * (1.0 + sc_ref[...]) + sh_ref[...]).astype(o_ref.dtype)

    row = pl.BlockSpec((tm, w), lambda i: (i, 0))
    return pl.pallas_call(
        body, name=name, grid=(s // tm,),
        in_specs=[row, _vec_spec(w, 1), _vec_spec(w, 1), _vec_spec(w, 1)],
        out_specs=row, out_shape=jax.ShapeDtypeStruct((s, w), BF16),
        compiler_params=_params(("parallel",)),
    )(x, g, sc, sh)


def _rms_mod_bwd(dh, x, g, sc, dres, name):
    s, w = x.shape
    tm = _tile(s, ROW_T)
    has_res = dres is not None

    def body(*refs):
        if has_res:
            dh_ref, x_ref, g_ref, sc_ref, dres_ref, dx_ref, sums_ref = refs
        else:
            dh_ref, x_ref, g_ref, sc_ref, dx_ref, sums_ref = refs
        xv, dhv, gv = x_ref[...], dh_ref[...], g_ref[...]
        r = lax.rsqrt(jnp.mean(xv * xv, axis=-1, keepdims=True) + EPS)
        xhat = xv * r
        dxn = dhv * (1.0 + sc_ref[...])
        dxhat = dxn * gv
        dx = r * (dxhat - xhat * jnp.mean(dxhat * xhat, axis=-1, keepdims=True))
        if has_res:
            dx = dx + dres_ref[...]
        dx_ref[...] = dx

        @pl.when(pl.program_id(0) == 0)
        def _():
            sums_ref[...] = jnp.zeros_like(sums_ref)

        sums_ref[0:1, :] += jnp.sum(dhv, axis=0, keepdims=True)
        sums_ref[1:2, :] += jnp.sum(dhv * (xhat * gv), axis=0, keepdims=True)
        sums_ref[2:3, :] += jnp.sum(dxn * xhat, axis=0, keepdims=True)

    row = pl.BlockSpec((tm, w), lambda i: (i, 0))
    in_specs = [row, row, _vec_spec(w, 1), _vec_spec(w, 1)] + ([row] if has_res else [])
    args = [dh, x, g, sc] + ([dres] if has_res else [])
    return pl.pallas_call(
        body, name=name, grid=(s // tm,),
        in_specs=in_specs,
        out_specs=(row, pl.BlockSpec((8, w), lambda i: (0, 0))),
        out_shape=(jax.ShapeDtypeStruct((s, w), F32), jax.ShapeDtypeStruct((8, w), F32)),
        compiler_params=_params(("arbitrary",)),
    )(*args)


def _scale_bwd(dx, val, gvec, name):
    s, w = dx.shape
    tm = _tile(s, ROW_T)

    def body(dx_ref, val_ref, g_ref, d_ref, sums_ref):
        dxv = dx_ref[...]
        d_ref[...] = (dxv * g_ref[...]).astype(d_ref.dtype)

        @pl.when(pl.program_id(0) == 0)
        def _():
            sums_ref[...] = jnp.zeros_like(sums_ref)

        sums_ref[0:1, :] += jnp.sum(dxv * val_ref[...], axis=0, keepdims=True)

    row = pl.BlockSpec((tm, w), lambda i: (i, 0))
    return pl.pallas_call(
        body, name=name, grid=(s // tm,),
        in_specs=[row, row, _vec_spec(w, 1)],
        out_specs=(row, pl.BlockSpec((8, w), lambda i: (0, 0))),
        out_shape=(jax.ShapeDtypeStruct((s, w), BF16), jax.ShapeDtypeStruct((8, w), F32)),
        compiler_params=_params(("arbitrary",)),
    )(dx, val, gvec)


def _final_loss(x3, target, g, name):
    s, w = x3.shape
    tm = _tile(s, ROW_T)

    def body(x_ref, t_ref, g_ref, dx_ref, sums_ref):
        xv, gv = x_ref[...], g_ref[...]
        r = lax.rsqrt(jnp.mean(xv * xv, axis=-1, keepdims=True) + EPS)
        xhat = xv * r
        err = xhat * gv - t_ref[...]
        dy = err * (1.0 / w)
        dxhat = dy * gv
        dx_ref[...] = r * (dxhat - xhat * jnp.mean(dxhat * xhat, axis=-1, keepdims=True))

        @pl.when(pl.program_id(0) == 0)
        def _():
            sums_ref[...] = jnp.zeros_like(sums_ref)

        sums_ref[0:1, :] += jnp.sum(dy * xhat, axis=0, keepdims=True)
        sums_ref[1:2, :] += jnp.zeros((1, w), F32) + (0.5 / w) * jnp.sum(err * err)

    row = pl.BlockSpec((tm, w), lambda i: (i, 0))
    return pl.pallas_call(
        body, name=name, grid=(s // tm,),
        in_specs=[row, row, _vec_spec(w, 1)],
        out_specs=(row, pl.BlockSpec((8, w), lambda i: (0, 0))),
        out_shape=(jax.ShapeDtypeStruct((s, w), F32), jax.ShapeDtypeStruct((8, w), F32)),
        compiler_params=_params(("arbitrary",)),
    )(x3, target, g)


def _rope_block(seg, cmul, smul):
    lane = lax.broadcasted_iota(jnp.int32, seg.shape, 1)
    swapped = jnp.where(lane < MLA_NOPE + MLA_ROPE // 2,
                        pltpu.roll(seg, HEAD_PAD - MLA_ROPE // 2, 1), pltpu.roll(seg, MLA_ROPE // 2, 1))
    return seg * cmul + swapped * smul


def _rope(t, cmul, smul, name, n_rot, out_dtypes):
    s, w = t.shape
    tm = _tile(s, ROW_T)
    n_out = len(out_dtypes)
    wo = w // n_out

    def body(t_ref, c_ref, s_ref, *o_refs):
        cv, sv = c_ref[...], s_ref[...]
        one = (lax.broadcasted_iota(jnp.int32, (tm, HEAD_PAD), 1) == SUM_LANE).astype(F32)
        for hb in range(w // HEAD_PAD):
            seg = t_ref[:, hb * HEAD_PAD:(hb + 1) * HEAD_PAD]
            if hb < n_rot:
                seg = _rope_block(seg, cv, sv)
            else:
                seg = seg + one
            o_ref = o_refs[(hb * HEAD_PAD) // wo]
            col = (hb * HEAD_PAD) % wo
            o_ref[:, col:col + HEAD_PAD] = seg.astype(o_ref.dtype)

    row = pl.BlockSpec((tm, w), lambda i: (i, 0))
    tab = pl.BlockSpec((tm, HEAD_PAD), lambda i: (i, 0))
    orow = pl.BlockSpec((tm, wo), lambda i: (i, 0))
    outs = pl.pallas_call(
        body, name=name, grid=(s // tm,),
        in_specs=[row, tab, tab],
        out_specs=tuple(orow for _ in out_dtypes),
        out_shape=tuple(jax.ShapeDtypeStruct((s, wo), dt) for dt in out_dtypes),
        compiler_params=_params(("parallel",)),
    )(t, cmul, smul)
    return outs


def _rope_bwd_kv(dk, dv, cmul, smul, name):
    s, w = dk.shape
    tm = _tile(s, ROW_T)

    def body(dk_ref, dv_ref, c_ref, s_ref, o_ref):
        cv, sv = c_ref[...], s_ref[...]
        for hb in range(N_HEADS):
            lo, hi = hb * HEAD_PAD, (hb + 1) * HEAD_PAD
            o_ref[:, lo:hi] = _rope_block(dk_ref[:, lo:hi], cv, sv).astype(o_ref.dtype)
        o_ref[:, w:2 * w] = dv_ref[...].astype(o_ref.dtype)

    row = pl.BlockSpec((tm, w), lambda i: (i, 0))
    tab = pl.BlockSpec((tm, HEAD_PAD), lambda i: (i, 0))
    return pl.pallas_call(
        body, name=name, grid=(s // tm,),
        in_specs=[row, row, tab, tab],
        out_specs=pl.BlockSpec((tm, 2 * w), lambda i: (i, 0)),
        out_shape=jax.ShapeDtypeStruct((s, 2 * w), BF16),
        compiler_params=_params(("parallel",)),
    )(dk, dv, cmul, smul)


def _lanes(col, width):
    if col.shape[1] == 1:
        col = jnp.broadcast_to(col, (col.shape[0], HEAD_PAD))
    return jnp.tile(col, (1, width // HEAD_PAD))


def _fold_lanes(a):
    out = a[:, 0:HEAD_PAD]
    for g in range(1, a.shape[1] // HEAD_PAD):
        out = out + a[:, g * HEAD_PAD:(g + 1) * HEAD_PAD]
    return out


def _causal(t, rows_are_queries):
    row = lax.broadcasted_iota(jnp.int32, (t, t), 0)
    col = lax.broadcasted_iota(jnp.int32, (t, t), 1)
    return row >= col if rows_are_queries else col >= row


def _attn_fwd(q, k, v, fcol, frow, name):
    s = q.shape[0]
    t = ATT_T
    nq = s // t
    use_f = fcol is not None

    hpb = FWD_HEADS_PER_STEP

    def body(*refs):
        if use_f:
            q_ref, k_ref, v_ref, fc_ref, fr_ref, o_ref, ob_ref, lse_ref, m_s, acc_s = refs
            fc_b = [jnp.broadcast_to(fc_ref[hh], (t, HEAD_PAD)) for hh in range(hpb)]
        else:
            q_ref, k_ref, v_ref, o_ref, ob_ref, lse_ref, m_s, acc_s = refs
        qi = pl.program_id(1)
        m_s[...] = jnp.full(m_s.shape, NEG_BIG, F32)
        acc_s[...] = jnp.zeros(acc_s.shape, F32)

        def step(j, masked):
            off = pl.multiple_of(j * t, t)
            for hh in range(hpb):
                lanes = slice(hh * HEAD_PAD, (hh + 1) * HEAD_PAD)
                kv = k_ref[pl.ds(off, t), lanes]
                vv = v_ref[pl.ds(off, t), lanes]
                sc = lax.dot_general(q_ref[:, lanes], kv, (((1,), (1,)), ((), ())), preferred_element_type=F32)
                if use_f:
                    sc = sc + (_lanes(fc_b[hh], t) - fr_ref[hh, j])
                if masked:
                    sc = jnp.where(_causal(t, True), sc, NEG_BIG)
                m_prev = m_s[hh]
                m_new = jnp.maximum(m_prev, jnp.max(sc, axis=-1, keepdims=True))
                p = jnp.exp2(sc - _lanes(m_new, t))
                acc_s[hh] = jnp.exp2(m_prev - m_new) * acc_s[hh] + jnp.dot(p.astype(BF16), vv,
                                                                           preferred_element_type=F32)
                m_s[hh] = m_new

        def loop_body(j, carry):
            step(j, False)
            return carry

        lax.fori_loop(0, qi, loop_body, 0)
        step(qi, True)
        for hh in range(hpb):
            lanes = slice(hh * HEAD_PAD, (hh + 1) * HEAD_PAD)
            acc = acc_s[hh]
            lane = lax.broadcasted_iota(jnp.int32, acc.shape, 1)
            denom = jnp.sum(jnp.where(lane == SUM_LANE, acc, 0.0), axis=-1, keepdims=True)
            o = acc * (1.0 / denom)
            o_ref[:, lanes] = o
            ob_ref[:, lanes] = o.astype(BF16)
            lse_ref[hh] = jnp.max(m_s[hh], axis=-1, keepdims=True) + jnp.log(denom) * LOG2E

    w = hpb * HEAD_PAD
    qspec = pl.BlockSpec((t, w), lambda h, i: (i, h))
    kspec = pl.BlockSpec((s, w), lambda h, i: (0, h))
    colspec = pl.BlockSpec((hpb, t, 1), lambda h, i: (h, i, 0))
    in_specs, args = [qspec, kspec, kspec], [q, k, v]
    if use_f:
        in_specs += [colspec, pl.BlockSpec((hpb, nq, 1, t), lambda h, i: (h, 0, 0, 0))]
        args += [fcol, frow]
    return pl.pallas_call(
        body, name=name, grid=(N_HEADS // hpb, nq),
        in_specs=in_specs,
        out_specs=(qspec, qspec, colspec),
        out_shape=(jax.ShapeDtypeStruct((s, N_HEADS * HEAD_PAD), F32), jax.ShapeDtypeStruct((s, N_HEADS * HEAD_PAD), BF16),
                   jax.ShapeDtypeStruct((N_HEADS, s, 1), F32)),
        scratch_shapes=[pltpu.VMEM((hpb, t, HEAD_PAD), F32), pltpu.VMEM((hpb, t, HEAD_PAD), F32)],
        compiler_params=_params(("parallel", "arbitrary")),
    )(*args)


def _attn_delta(o, do, name):
    s, w = o.shape
    tm = _tile(s, ROW_T)

    def body(o_ref, do_ref, d_ref):
        for hb in range(N_HEADS):
            lo, hi = hb * HEAD_PAD, (hb + 1) * HEAD_PAD
            d_ref[hb] = jnp.sum(o_ref[:, lo:hi] * do_ref[:, lo:hi].astype(F32), axis=-1, keepdims=True)

    row = pl.BlockSpec((tm, w), lambda i: (i, 0))
    return pl.pallas_call(
        body, name=name, grid=(s // tm,),
        in_specs=[row, row],
        out_specs=pl.BlockSpec((N_HEADS, tm, 1), lambda i: (0, i, 0)),
        out_shape=jax.ShapeDtypeStruct((N_HEADS, s, 1), F32),
        compiler_params=_params(("parallel",)),
    )(o, do)


def _attn_bwd(q, k, v, do, lse_row, delta_row, fcol, frow, scale_q, scale_k, name, out_dtype):
    s = q.shape[0]
    t = ATT_T
    nq = s // t
    use_f = fcol is not None

    def body(*refs):
        if use_f:
            (q_ref, k_ref, v_ref, do_ref, lse_ref, dl_ref, fc_ref, fr_ref,
             dq_ref, dk_ref, dv_ref, dr_ref, df_ref, dq_s, dk_s, dv_s, dr_s, df_s) = refs
        else:
            q_ref, k_ref, v_ref, do_ref, lse_ref, dl_ref, dq_ref, dk_ref, dv_ref, dq_s, dk_s, dv_s = refs
        kj = pl.program_id(1)

        @pl.when(kj == 0)
        def _():
            dq_s[...] = jnp.zeros(dq_s.shape, F32)
            if use_f:
                dr_s[...] = jnp.zeros(dr_s.shape, F32)

        kv, vv = k_ref[...], v_ref[...]
        dk_s[...] = jnp.zeros(dk_s.shape, F32)
        dv_s[...] = jnp.zeros(dv_s.shape, F32)
        if use_f:
            df_s[...] = jnp.zeros(df_s.shape, F32)
            fc_b = jnp.broadcast_to(fc_ref[0], (t, HEAD_PAD))

        def step(i, masked):
            off = pl.multiple_of(i * t, t)
            qv = q_ref[pl.ds(off, t), :]
            dov = do_ref[pl.ds(off, t), :]
            st = lax.dot_general(kv, qv, (((1,), (1,)), ((), ())), preferred_element_type=F32)
            if use_f:
                st = st + (fr_ref[0, i] - _lanes(fc_b, t))
            if masked:
                st = jnp.where(_causal(t, False), st, NEG_BIG)
            pt = jnp.exp2(st - lse_ref[0, i])
            dv_s[...] += jnp.dot(pt.astype(BF16), dov, preferred_element_type=F32)
            dpt = lax.dot_general(vv, dov, (((1,), (1,)), ((), ())), preferred_element_type=F32)
            dst = pt * (dpt - dl_ref[0, i])
            dsb = dst.astype(BF16)
            dk_s[...] += jnp.dot(dsb, qv, preferred_element_type=F32)
            dq_s[pl.ds(off, t), :] += lax.dot_general(dsb, kv, (((0,), (0,)), ((), ())), preferred_element_type=F32)
            if use_f:
                df_s[...] -= _fold_lanes(dst)
                dr_s[i] += jnp.sum(dst, axis=0, keepdims=True)

        step(kj, True)

        def loop_body(i, carry):
            step(i, False)
            return carry

        lax.fori_loop(kj + 1, nq, loop_body, 0)
        dk_ref[...] = (dk_s[...] * scale_k).astype(dk_ref.dtype)
        dv_ref[...] = dv_s[...].astype(dv_ref.dtype)
        if use_f:
            df_ref[0] = jnp.sum(df_s[...], axis=-1, keepdims=True)

        @pl.when(kj == nq - 1)
        def _():
            dq_ref[...] = (dq_s[...] * scale_q).astype(dq_ref.dtype)
            if use_f:
                dr_ref[0] = dr_s[...]

    kspec = pl.BlockSpec((t, HEAD_PAD), lambda h, j: (j, h))
    qspec = pl.BlockSpec((s, HEAD_PAD), lambda h, j: (0, h))
    rowspec = pl.BlockSpec((1, nq, 1, t), lambda h, j: (h, 0, 0, 0))
    colspec = pl.BlockSpec((1, t, 1), lambda h, j: (h, j, 0))
    in_specs, args = [qspec, kspec, kspec, qspec, rowspec, rowspec], [q, k, v, do, lse_row, delta_row]
    full = jax.ShapeDtypeStruct((s, N_HEADS * HEAD_PAD), out_dtype)
    out_specs, out_shape = [qspec, kspec, kspec], [full, full, full]
    scratch = [pltpu.VMEM((s, HEAD_PAD), F32), pltpu.VMEM((t, HEAD_PAD), F32), pltpu.VMEM((t, HEAD_PAD), F32)]
    if use_f:
        in_specs += [colspec, rowspec]
        args += [fcol, frow]
        out_specs += [rowspec, colspec]
        out_shape += [jax.ShapeDtypeStruct((N_HEADS, nq, 1, t), F32), jax.ShapeDtypeStruct((N_HEADS, s, 1), F32)]
        scratch += [pltpu.VMEM((nq, 1, t), F32), pltpu.VMEM((t, HEAD_PAD), F32)]
    return pl.pallas_call(
        body, name=name, grid=(N_HEADS, nq),
        in_specs=in_specs, out_specs=tuple(out_specs), out_shape=tuple(out_shape),
        scratch_shapes=scratch,
        compiler_params=_params(("parallel", "arbitrary")),
    )(*args)


def _gate_fwd(pm, pf, proj, name):
    s, w = pm.shape
    tm = _tile(s, ROW_T)

    def body(pm_ref, pf_ref, gm_ref, gf_ref, y_ref):
        y = jax.nn.sigmoid(gm_ref[...]) * pm_ref[...] + jax.nn.sigmoid(gf_ref[...]) * pf_ref[...]
        y_ref[...] = y.astype(y_ref.dtype)

    row = pl.BlockSpec((tm, w), lambda i: (i, 0))
    return pl.pallas_call(
        body, name=name, grid=(s // tm,),
        in_specs=[row, row, pl.BlockSpec((tm, w), lambda i: (i, P_GM // D_MODEL)),
                  pl.BlockSpec((tm, w), lambda i: (i, P_GF // D_MODEL))],
        out_specs=row, out_shape=jax.ShapeDtypeStruct((s, w), BF16),
        compiler_params=_params(("parallel",)),
    )(pm, pf, proj, proj)


def _gate_bwd(dy, pm, pf, proj, name):
    s, w = pm.shape
    tm = _tile(s, ROW_T)

    def body(dy_ref, pm_ref, pf_ref, gm_ref, gf_ref, dpm_ref, dpf_ref, dgm_ref, dgf_ref):
        dyv = dy_ref[...]
        sm, sf = jax.nn.sigmoid(gm_ref[...]), jax.nn.sigmoid(gf_ref[...])
        dpm_ref[...] = (dyv * sm).astype(BF16)
        dpf_ref[...] = (dyv * sf).astype(BF16)
        dgm_ref[...] = (dyv * pm_ref[...] * (sm * (1.0 - sm))).astype(BF16)
        dgf_ref[...] = (dyv * pf_ref[...] * (sf * (1.0 - sf))).astype(BF16)

    row = pl.BlockSpec((tm, w), lambda i: (i, 0))
    out = jax.ShapeDtypeStruct((s, w), BF16)
    return pl.pallas_call(
        body, name=name, grid=(s // tm,),
        in_specs=[row, row, row, pl.BlockSpec((tm, w), lambda i: (i, P_GM // D_MODEL)),
                  pl.BlockSpec((tm, w), lambda i: (i, P_GF // D_MODEL))],
        out_specs=(row, row, row, row), out_shape=(out, out, out, out),
        compiler_params=_params(("parallel",)),
    )(dy, pm, pf, proj, proj)


CONV_TN = 256
CONV_TM = 512
HALO = BF16_ROWS


def _shift_down(u, prev, n):
    rolled = pltpu.roll(u, n, 0)
    prev_rolled = pltpu.roll(prev, n, 0)
    top = jnp.concatenate([prev_rolled, rolled[HALO:]], axis=0)
    row = lax.broadcasted_iota(jnp.int32, u.shape, 0)
    return jnp.where(row < n, top, rolled)


def _conv_tile(u, prev, w_ref, b_ref):
    um1 = _shift_down(u, prev, 1)
    um2 = _shift_down(u, prev, 2)
    uc = b_ref[...] + w_ref[0:1, :] * um2 + w_ref[1:2, :] * um1 + w_ref[2:3, :] * u
    return uc, um1, um2


def _conv_specs(tm, tn, ncol_off):
    blk = lambda off: pl.BlockSpec((tm, tn), lambda j, i: (i, j + off))
    halo = lambda off: pl.BlockSpec((HALO, tn), lambda j, i: (jnp.maximum(i * (tm // HALO) - 1, 0), j + off))
    wsp = lambda off: pl.BlockSpec((3, tn), lambda j, i: (0, j + off))
    bsp = lambda off: pl.BlockSpec((1, tn), lambda j, i: (0, j + off))
    return blk, halo, wsp, bsp


def _convglu_fwd(u, conv_w, conv_b, name):
    s = u.shape[0]
    tm, tn = _tile(s, CONV_TM), CONV_TN
    nj = D_FF // tn
    blk, halo, wsp, bsp = _conv_specs(tm, tn, nj)

    def body(ug_ref, pg_ref, uv_ref, pv_ref, wg_ref, wv_ref, bg_ref, bv_ref, a_ref):
        live = (pl.program_id(1) > 0).astype(F32)
        gate, _, _ = _conv_tile(ug_ref[...].astype(F32), pg_ref[...].astype(F32) * live, wg_ref, bg_ref)
        val, _, _ = _conv_tile(uv_ref[...].astype(F32), pv_ref[...].astype(F32) * live, wv_ref, bv_ref)
        a_ref[...] = (gate * jax.nn.sigmoid(gate) * val).astype(a_ref.dtype)

    return pl.pallas_call(
        body, name=name, grid=(nj, s // tm),
        in_specs=[blk(0), halo(0), blk(nj), halo(nj), wsp(0), wsp(nj), bsp(0), bsp(nj)],
        out_specs=blk(0), out_shape=jax.ShapeDtypeStruct((s, D_FF), BF16),
        compiler_params=_params(("parallel", "arbitrary")),
    )(u, u, u, u, conv_w, conv_w, conv_b, conv_b)


def _convglu_bwd(da, u, conv_w, conv_b, name):
    s = u.shape[0]
    tm, tn = _tile(s, CONV_TM), CONV_TN
    nj = D_FF // tn
    blk, halo, wsp, bsp = _conv_specs(tm, tn, nj)

    def body(da_ref, ug_ref, pg_ref, uv_ref, pv_ref, wg_ref, wv_ref, bg_ref, bv_ref,
             dg_ref, dv_ref, sg_ref, sv_ref):
        live = (pl.program_id(1) > 0).astype(F32)
        ug, uv = ug_ref[...].astype(F32), uv_ref[...].astype(F32)
        gate, ug1, ug2 = _conv_tile(ug, pg_ref[...].astype(F32) * live, wg_ref, bg_ref)
        val, uv1, uv2 = _conv_tile(uv, pv_ref[...].astype(F32) * live, wv_ref, bv_ref)
        dav = da_ref[...].astype(F32)
        sig = jax.nn.sigmoid(gate)
        dgate = dav * val * (sig * (1.0 + gate * (1.0 - sig)))
        dval = dav * (gate * sig)
        dg_ref[...] = dgate.astype(dg_ref.dtype)
        dv_ref[...] = dval.astype(dv_ref.dtype)

        @pl.when(pl.program_id(1) == 0)
        def _():
            sg_ref[...] = jnp.zeros_like(sg_ref)
            sv_ref[...] = jnp.zeros_like(sv_ref)

        for s_ref, d, taps in ((sg_ref, dgate, (ug2, ug1, ug)), (sv_ref, dval, (uv2, uv1, uv))):
            for r, tap in enumerate(taps):
                s_ref[r:r + 1, :] += jnp.sum(d * tap, axis=0, keepdims=True)
            s_ref[3:4, :] += jnp.sum(d, axis=0, keepdims=True)

    sums = lambda off: pl.BlockSpec((8, tn), lambda j, i: (0, j + off))
    return pl.pallas_call(
        body, name=name, grid=(nj, s // tm),
        in_specs=[blk(0), blk(0), halo(0), blk(nj), halo(nj), wsp(0), wsp(nj), bsp(0), bsp(nj)],
        out_specs=(blk(0), blk(0), sums(0), sums(0)),
        out_shape=(jax.ShapeDtypeStruct((s, D_FF), BF16), jax.ShapeDtypeStruct((s, D_FF), BF16),
                   jax.ShapeDtypeStruct((8, D_FF), F32), jax.ShapeDtypeStruct((8, D_FF), F32)),
        compiler_params=_params(("parallel", "arbitrary")),
    )(da, u, u, u, u, conv_w, conv_w, conv_b, conv_b)


def _conv_transpose(d, conv_w_half, name):
    s, w = d.shape
    tm, tn = _tile(s, CONV_TM), CONV_TN
    last = s // tm - 1

    def body(d_ref, nx_ref, w_ref, o_ref):
        dv = d_ref[...].astype(F32)
        nxt = nx_ref[...].astype(F32) * (pl.program_id(1) < last).astype(F32)
        row = lax.broadcasted_iota(jnp.int32, dv.shape, 0)

        def shift_up(n):
            rolled = pltpu.roll(dv, tm - n, 0)
            nxt_rolled = pltpu.roll(nxt, HALO - n, 0)
            bottom = jnp.concatenate([rolled[:tm - HALO], nxt_rolled], axis=0)
            return jnp.where(row >= tm - n, bottom, rolled)

        out = w_ref[2:3, :] * dv + w_ref[1:2, :] * shift_up(1) + w_ref[0:1, :] * shift_up(2)
        o_ref[...] = out.astype(o_ref.dtype)

    blk = pl.BlockSpec((tm, tn), lambda j, i: (i, j))
    nxt_spec = pl.BlockSpec((HALO, tn), lambda j, i: (jnp.minimum((i + 1) * (tm // HALO), s // HALO - 1), j))
    return pl.pallas_call(
        body, name=name, grid=(w // tn, s // tm),
        in_specs=[blk, nxt_spec, pl.BlockSpec((3, tn), lambda j, i: (0, j))],
        out_specs=blk, out_shape=jax.ShapeDtypeStruct((s, w), BF16),
        compiler_params=_params(("parallel", "arbitrary")),
    )(d, d, conv_w_half)


def _split3(a):
    a1 = a.astype(BF16)
    r1 = a - a1.astype(F32)
    a2 = r1.astype(BF16)
    a3 = (r1 - a2.astype(F32)).astype(BF16)
    return a1, a2, a3


def _ones_dot_right(a, mat):
    return sum(jnp.dot(p, mat, preferred_element_type=F32) for p in _split3(a))


def _ones_dot_left(mat, a):
    return sum(jnp.dot(mat, p, preferred_element_type=F32) for p in _split3(a))


def _tri(n, cmp):
    r = lax.broadcasted_iota(jnp.int32, (n, n), 0)
    c = lax.broadcasted_iota(jnp.int32, (n, n), 1)
    return cmp(r, c).astype(BF16)


def _forget_fwd(z, bias, name):
    nh, nr, nl = z.shape

    def body(z_ref, b_ref, f_ref):
        within = _tri(nl, lambda r, c: r <= c)
        before = _tri(nr, lambda r, c: c < r)
        for h in range(nh):
            x = z_ref[h] + b_ref[h]
            lf = jnp.minimum(x, 0.0) - jnp.log(1.0 + jnp.exp(-jnp.abs(x)))
            pre = _ones_dot_right(lf, within)
            tot = jnp.zeros((nr, nl), F32) + jnp.sum(lf, axis=1, keepdims=True)
            f_ref[h] = pre + _ones_dot_left(before, tot)

    return pl.pallas_call(
        body, name=name, out_shape=jax.ShapeDtypeStruct(z.shape, F32),
        compiler_params=pltpu.CompilerParams(vmem_limit_bytes=VMEM_LIMIT_BYTES),
    )(z, bias)


def _forget_bwd(df_rows, df_cols, z, bias, name):
    nh, nr, nl = z.shape

    def body(dfr_ref, dfc_ref, z_ref, b_ref, dz_ref, db_ref):
        within = _tri(nl, lambda r, c: r >= c)
        after = _tri(nr, lambda r, c: c > r)
        for h in range(nh):
            g = dfr_ref[h] + dfc_ref[h]
            suf = _ones_dot_right(g, within)
            tot = jnp.zeros((nr, nl), F32) + jnp.sum(g, axis=1, keepdims=True)
            dlf = suf + _ones_dot_left(after, tot)
            dz = dlf * jax.nn.sigmoid(-(z_ref[h] + b_ref[h]))
            dz_ref[h] = dz
            db_ref[h] = jnp.zeros((1, nl), F32) + jnp.sum(dz)

    return pl.pallas_call(
        body, name=name,
        out_shape=(jax.ShapeDtypeStruct(z.shape, F32), jax.ShapeDtypeStruct(bias.shape, F32)),
        compiler_params=pltpu.CompilerParams(vmem_limit_bytes=VMEM_LIMIT_BYTES),
    )(df_rows, df_cols, z, bias)


def _ada_fwd(c_col, w, b, name):
    kdim, n = w.shape

    def body(c_ref, w_ref, b_ref, ada_ref, act_ref):
        wv = w_ref[...]
        for e in range(N_DEV):
            cv = c_ref[e]
            act = cv * jax.nn.sigmoid(cv)
            act_ref[e] = act
            ada_ref[e:e + 1, :] = jnp.sum(act * wv, axis=0, keepdims=True) + b_ref[...]

    return pl.pallas_call(
        body, name=name,
        out_shape=(jax.ShapeDtypeStruct((N_DEV, n), F32), jax.ShapeDtypeStruct((N_DEV, kdim, 1), F32)),
        compiler_params=pltpu.CompilerParams(vmem_limit_bytes=VMEM_LIMIT_BYTES),
    )(c_col, w, b)


def _ada_bwd(act_col, dada, name):
    kdim = act_col.shape[1]
    n = dada.shape[1]

    def body(act_ref, d_ref, g_ref):
        acc = act_ref[0] * d_ref[0:1, :]
        for e in range(1, N_DEV):
            acc = acc + act_ref[e] * d_ref[e:e + 1, :]
        g_ref[...] = acc

    return pl.pallas_call(
        body, name=name, out_shape=jax.ShapeDtypeStruct((kdim, n), F32),
        compiler_params=pltpu.CompilerParams(vmem_limit_bytes=VMEM_LIMIT_BYTES),
    )(act_col, dada)


def _adamw(parts, w, m, v, name, tr=128):
    npart, r, c = parts.shape
    tr = _tile(r, tr, step=BF16_ROWS) if r % BF16_ROWS == 0 else r

    def body(p_ref, w_ref, m_ref, v_ref, g_ref, d_ref, nm_ref, nv_ref):
        g = p_ref[0].astype(F32)
        for e in range(1, npart):
            g = g + p_ref[e].astype(F32)
        nm = ADAM_B1 * m_ref[...] + (1.0 - ADAM_B1) * g
        nv = ADAM_B2 * v_ref[...] + (1.0 - ADAM_B2) * (g * g)
        m_hat = nm / (1.0 - ADAM_B1 ** ADAM_STEP)
        v_hat = nv / (1.0 - ADAM_B2 ** ADAM_STEP)
        g_ref[...] = g
        d_ref[...] = -ADAM_LR * (m_hat / (jnp.sqrt(v_hat) + ADAM_EPS) + ADAM_WD * w_ref[...])
        nm_ref[...] = nm
        nv_ref[...] = nv

    row = pl.BlockSpec((tr, c), lambda i: (i, 0))
    out = jax.ShapeDtypeStruct((r, c), F32)
    return pl.pallas_call(
        body, name=name, grid=(r // tr,),
        in_specs=[pl.BlockSpec((npart, tr, c), lambda i: (0, i, 0)), row, row, row],
        out_specs=(row, row, row, row), out_shape=(out, out, out, out),
        compiler_params=_params(("parallel",)),
    )(parts, w, m, v)


BIG = ("w_in", "w_uq", "w_ukv", "w_o_mla", "w_o_fox", "w_out", "w_up", "conv_w", "w_down")


def _cols_to_full(stack):
    n, r, c = stack.shape
    return stack.transpose(1, 0, 2).reshape(r, n * c)


def _full_to_cols(full, c):
    r = full.shape[0]
    return full.reshape(r, N_DEV, c).transpose(1, 0, 2)


def _pad_heads(a, width, ones_lane=False):
    s = a.shape[0]
    a = a.reshape(s, N_HEADS, width)
    if ones_lane:
        assert width == SUM_LANE
        tail = jnp.zeros((s, N_HEADS, HEAD_PAD - width), a.dtype).at[:, :, 0].set(1.0)
        return jnp.concatenate([a, tail], axis=2).reshape(s, N_HEADS * HEAD_PAD)
    return jnp.pad(a, ((0, 0), (0, 0), (0, HEAD_PAD - width))).reshape(s, N_HEADS * HEAD_PAD)


def _unpad_heads(a, width):
    s = a.shape[0]
    return a.reshape(s, N_HEADS, HEAD_PAD)[:, :, :width].reshape(s, N_HEADS * width)


def _w_in_padded(w_in):
    seg = [w_in[:, IN_OFF[i]:IN_OFF[i + 1]] for i in range(9)]
    cq, ckv, kr, fq, fk, fv, fl, gm, gf = seg
    padc = lambda a, n: jnp.pad(a, ((0, 0), (0, n - a.shape[1])))
    return jnp.concatenate([gm, gf, fq, fk, fv, cq, ckv, padc(kr, 128), padc(fl, 128)], axis=1)


def _w_in_unpadded(g):
    return jnp.concatenate([
        g[:, P_CQ:P_CQ + 384], g[:, P_CKV:P_CKV + 256], g[:, P_KR:P_KR + 32], g[:, P_FQ:P_FQ + 512],
        g[:, P_FK:P_FK + 512], g[:, P_FV:P_FV + 512], g[:, P_FL:P_FL + 8], g[:, P_GM:P_GM + 1024],
        g[:, P_GF:P_GF + 1024]], axis=1)


SMALL = (("b_ada", 6144, 6144), ("norm_mix_g", 1024, 1024), ("q_norm_g", 384, 384), ("kv_norm_g", 256, 256),
         ("b_forget", 8, 128), ("norm_ffn_g", 1024, 1024), ("conv_b", 5632, 5632), ("norm_final_g", 1024, 1024),
         ("loss", 1, 128))
SMALL_OFF = {}
_o = 0
for _n, _real, _padded in SMALL:
    SMALL_OFF[_n] = _o
    _o += _padded
SMALL_W = _o


def _pack_small(vals):
    parts = []
    for nme, real, padded in SMALL:
        a = vals[nme].reshape(1, real).astype(F32)
        parts.append(jnp.pad(a, ((0, 0), (0, padded - real))))
    return jnp.concatenate(parts, axis=1)


def kernel(x, c, positions, w_ada, b_ada, norm_mix_g, w_in, q_norm_g, w_uq, kv_norm_g, w_ukv, b_forget, w_o_mla, w_o_fox, w_out, norm_ffn_g, w_up, conv_w, conv_b, w_down, norm_final_g, loss_target, m_w_ada, m_b_ada, m_norm_mix_g, m_w_in, m_q_norm_g, m_w_uq, m_kv_norm_g, m_w_ukv, m_b_forget, m_w_o_mla, m_w_o_fox, m_w_out, m_norm_ffn_g, m_w_up, m_conv_w, m_conv_b, m_w_down, m_norm_final_g, v_w_ada, v_b_ada, v_norm_mix_g, v_w_in, v_q_norm_g, v_w_uq, v_kv_norm_g, v_w_ukv, v_b_forget, v_w_o_mla, v_w_o_fox, v_w_out, v_norm_ffn_g, v_w_up, v_conv_w, v_conv_b, v_w_down, v_norm_final_g):
    me = 4 * lax.axis_index("x") + 2 * lax.axis_index("y") + lax.axis_index("c")
    x = x[0]
    target = loss_target[0]
    s = x.shape[0]
    nblk = s // ATT_T
    big_w = {"w_in": w_in, "w_uq": w_uq, "w_ukv": w_ukv, "w_o_mla": w_o_mla, "w_o_fox": w_o_fox,
             "w_out": w_out, "w_up": w_up, "conv_w": conv_w, "w_down": w_down}
    big_m = {"w_in": m_w_in, "w_uq": m_w_uq, "w_ukv": m_w_ukv, "w_o_mla": m_w_o_mla, "w_o_fox": m_w_o_fox,
             "w_out": m_w_out, "w_up": m_w_up, "conv_w": m_conv_w, "w_down": m_w_down}
    big_v = {"w_in": v_w_in, "w_uq": v_w_uq, "w_ukv": v_w_ukv, "w_o_mla": v_w_o_mla, "w_o_fox": v_w_o_fox,
             "w_out": v_w_out, "w_up": v_w_up, "conv_w": v_conv_w, "w_down": v_w_down}

    gathered = _all_gather([big_w[k][0] if k == "conv_w" else big_w[k][0].astype(BF16) for k in BIG], "gather_weights")
    st = dict(zip(BIG, gathered))
    w_in_p = _w_in_padded(_cols_to_full(st["w_in"]))
    uq = st["w_uq"]
    w_uq_p = jnp.pad(uq, ((0, 0), (0, 0), (0, HEAD_PAD - 96))).transpose(1, 0, 2).reshape(MLA_Q_RANK, 1024)
    ukv = st["w_ukv"]
    zeros64 = jnp.zeros((N_HEADS, MLA_KV_RANK, 64), BF16)
    w_uk_p = jnp.concatenate([ukv[:, :, :64], zeros64], axis=2).transpose(1, 0, 2).reshape(MLA_KV_RANK, 1024)
    w_uv_p = jnp.concatenate([ukv[:, :, 64:], zeros64], axis=2).transpose(1, 0, 2).reshape(MLA_KV_RANK, 1024)
    place = np.zeros((HEAD_PAD, N_HEADS, HEAD_PAD), np.float32)
    for j in range(MLA_ROPE):
        place[j, :, MLA_NOPE + j] = 1.0
    place = jnp.asarray(place.reshape(HEAD_PAD, 1024), BF16)
    w_kv_comb = jnp.concatenate([
        jnp.concatenate([w_uk_p, w_uv_p], axis=1),
        jnp.concatenate([place, jnp.zeros((HEAD_PAD, 1024), BF16)], axis=1)], axis=0)
    pad_o = lambda full: jnp.pad(full.reshape(N_HEADS, 64, 1024), ((0, 0), (0, 64), (0, 0))).reshape(1024, 1024)
    w_o_mla_p = pad_o(_cols_to_full(st["w_o_mla"]))
    w_o_fox_p = pad_o(_cols_to_full(st["w_o_fox"]))
    w_out_f = st["w_out"].reshape(1024, 1024)
    w_up_f = _cols_to_full(st["w_up"])
    conv_w_f = _cols_to_full(st["conv_w"])
    w_down_f = st["w_down"].reshape(D_FF, 1024)

    (c_all,) = _all_gather([c], "gather_c")
    b_ada_mine = lax.dynamic_slice(b_ada, (0, me * 768), (1, 768))
    ada_cols, act_col = _ada_fwd(c_all.reshape(N_DEV, D_MODEL, 1), w_ada[0], b_ada_mine, "ada_fwd")
    (ada_all,) = _all_gather([ada_cols], "gather_ada")
    ada = lax.dynamic_slice(ada_all, (0, me, 0), (N_DEV, 1, 768)).reshape(1, N_ADA * D_MODEL)
    sh_m, sc_m, g_m, sh_f, sc_f, g_f = [ada[:, i * D_MODEL:(i + 1) * D_MODEL] for i in range(N_ADA)]

    inv_freq = ROPE_THETA ** (-jnp.arange(0, MLA_ROPE, 2, dtype=F32) / MLA_ROPE)
    ang = positions[0].astype(F32)[:, None] * inv_freq
    cos, sin = jnp.cos(ang), jnp.sin(ang)
    rope_c = jnp.concatenate([jnp.ones((s, 64), F32), cos, cos, jnp.zeros((s, 32), F32)], axis=1)
    rope_s = jnp.concatenate([jnp.zeros((s, 64), F32), -sin, sin, jnp.zeros((s, 32), F32)], axis=1)

    zero_d = jnp.zeros((1, D_MODEL), F32)

    h1 = _rms_mod(x, norm_mix_g, sc_m, sh_m, "norm_mix")
    proj = _mm(h1, w_in_p, "nn", F32, "proj_in", tn=640)
    cq = proj[:, P_CQ:P_CQ + 384]
    ckv = proj[:, P_CKV:P_CKV + 256]
    qn = _rms_mod(cq, q_norm_g, jnp.zeros((1, 384), F32), jnp.zeros((1, 384), F32), "q_norm")
    kvn = _rms_mod(ckv, kv_norm_g, jnp.zeros((1, 256), F32), jnp.zeros((1, 256), F32), "kv_norm")
    kv_in = jnp.concatenate([kvn, proj[:, P_KR:P_KR + 128].astype(BF16)], axis=1)
    q_pre = _mm(qn, w_uq_p, "nn", F32, "q_up")
    kv_pre = _mm(kv_in, w_kv_comb, "nn", F32, "kv_up")
    q_fold = MLA_SCALE * LOG2E
    (q_att,) = _rope(q_pre, rope_c * q_fold, rope_s * q_fold, "rope_q", N_HEADS, (BF16,))
    k_att, v_att = _rope(kv_pre, rope_c, rope_s, "rope_kv", N_HEADS, (BF16, BF16))
    o_mla, o_mla_b, lse_mla = _attn_fwd(q_att, k_att, v_att, None, None, "mla_fwd")

    fq = _pad_heads(proj[:, P_FQ:P_FQ + 512] * (FOX_SCALE * LOG2E), 64).astype(BF16)
    fk = _pad_heads(proj[:, P_FK:P_FK + 512], 64).astype(BF16)
    fv = _pad_heads(proj[:, P_FV:P_FV + 512], 64, ones_lane=True).astype(BF16)
    z = proj[:, P_FL:P_FL + 8].T.reshape(N_HEADS, s // SEQ_LANES, SEQ_LANES)
    bias_f = jnp.broadcast_to(b_forget.reshape(N_HEADS, 1, 1), (N_HEADS, 1, SEQ_LANES))
    f_cum = _forget_fwd(z, bias_f, "forget_fwd")
    f_col = (f_cum * LOG2E).reshape(N_HEADS, s, 1)
    f_row = f_col.reshape(N_HEADS, nblk, 1, ATT_T)
    o_fox, o_fox_b, lse_fox = _attn_fwd(fq, fk, fv, f_col, f_row, "fox_fwd")

    pm = _mm(o_mla_b, w_o_mla_p, "nn", F32, "o_mla_proj")
    pf = _mm(o_fox_b, w_o_fox_p, "nn", F32, "o_fox_proj")
    y = _gate_fwd(pm, pf, proj, "gate_fwd")
    x2, mix = _mm(y, w_out_f, "nn", F32, "out_proj", res=x, gvec=g_m)

    h2 = _rms_mod(x2, norm_ffn_g, sc_f, sh_f, "norm_ffn")
    u = _mm(h2, w_up_f, "nn", BF16, "ffn_up")
    a = _convglu_fwd(u, conv_w_f, conv_b, "convglu_fwd")
    x3, ffn = _mm(a, w_down_f, "nn", F32, "ffn_down", res=x2, gvec=g_f, tk=2816)

    dx3, sums_final = _final_loss(x3, target, norm_final_g.reshape(1, D_MODEL), "final_loss")
    dffn, sums_gf = _scale_bwd(dx3, ffn, g_f, "ffn_scale_bwd")
    da = _mm(dffn, w_down_f, "nt", BF16, "ffn_down_dx", tn=1408)
    g_w_down = _mm(a, dffn, "tn", F32, "ffn_down_dw", tm=256, tn=1024, tk=s)
    dgate, dval, s_gate, s_val = _convglu_bwd(da, u, conv_w_f, conv_b, "convglu_bwd")
    du = jnp.concatenate([_conv_transpose(dgate, conv_w_f[:, :D_FF], "conv_t_gate"),
                          _conv_transpose(dval, conv_w_f[:, D_FF:], "conv_t_val")], axis=1)
    dh2 = _mm(du, w_up_f, "nt", F32, "ffn_up_dx", tn=512, tk=2 * D_FF)
    g_w_up = _mm(h2, du, "tn", F32, "ffn_up_dw", tn=256, tk=s)
    dx2, sums_ffn = _rms_mod_bwd(dh2, x2, norm_ffn_g, sc_f, dx3, "norm_ffn_bwd")

    dmix, sums_gm = _scale_bwd(dx2, mix, g_m, "mix_scale_bwd")
    dy = _mm(dmix, w_out_f, "nt", F32, "out_proj_dx")
    g_w_out = _mm(y, dmix, "tn", F32, "out_proj_dw", tn=256, tk=s)
    dpm, dpf, dgm, dgf = _gate_bwd(dy, pm, pf, proj, "gate_bwd")
    do_mla_b = _mm(dpm, w_o_mla_p, "nt", BF16, "o_mla_dx", tn=1024)
    do_fox_b = _mm(dpf, w_o_fox_p, "nt", BF16, "o_fox_dx", tn=1024)
    g_w_o_mla_p = _mm(o_mla_b, dpm, "tn", F32, "o_mla_dw", tn=256, tk=s)
    g_w_o_fox_p = _mm(o_fox_b, dpf, "tn", F32, "o_fox_dw", tn=256, tk=s)

    rows = lambda col: col.reshape(N_HEADS, nblk, 1, ATT_T)
    delta_mla = _attn_delta(o_mla, do_mla_b, "mla_delta")
    dq_rot, dk_rot, dv_mla = _attn_bwd(q_att, k_att, v_att, do_mla_b, rows(lse_mla), rows(delta_mla), None, None,
                                       MLA_SCALE, 1.0 / LOG2E, "mla_bwd", F32)
    (dq_pre,) = _rope(dq_rot, rope_c, -rope_s, "rope_q_bwd", N_HEADS, (BF16,))
    dkv_pre = _rope_bwd_kv(dk_rot, dv_mla, rope_c, -rope_s, "rope_kv_bwd")
    dqn = _mm(dq_pre, w_uq_p, "nt", F32, "q_up_dx")
    g_w_uq_p = _mm(qn, dq_pre, "tn", F32, "q_up_dw", tk=s)
    dkv_in = _mm(dkv_pre, w_kv_comb, "nt", F32, "kv_up_dx")
    g_w_kv_comb = _mm(kv_in, dkv_pre, "tn", F32, "kv_up_dw", tk=s)
    dcq, sums_q = _rms_mod_bwd(dqn, cq, q_norm_g, jnp.zeros((1, 384), F32), None, "q_norm_bwd")
    dckv, sums_kv = _rms_mod_bwd(dkv_in[:, :256], ckv, kv_norm_g, jnp.zeros((1, 256), F32), None, "kv_norm_bwd")
    delta_fox = _attn_delta(o_fox, do_fox_b, "fox_delta")
    dfq, dfk, dfv, dfr, dfc = _attn_bwd(fq, fk, fv, do_fox_b, rows(lse_fox), rows(delta_fox), f_col, f_row,
                                        FOX_SCALE, 1.0 / LOG2E, "fox_bwd", BF16)
    df_rows = dfr.reshape(N_HEADS, s // SEQ_LANES, SEQ_LANES)
    df_cols = dfc.reshape(N_HEADS, s // SEQ_LANES, SEQ_LANES)
    dz, db_f = _forget_bwd(df_rows, df_cols, z, bias_f, "forget_bwd")
    dfl = jnp.pad(dz.reshape(N_HEADS, s).T, ((0, 0), (0, 128 - N_HEADS)))

    dproj = jnp.concatenate([
        dgm, dgf, _unpad_heads(dfq, 64), _unpad_heads(dfk, 64), _unpad_heads(dfv, 64),
        dcq.astype(BF16), dckv.astype(BF16), dkv_in[:, 256:384].astype(BF16), dfl.astype(BF16)], axis=1)
    dh1 = _mm(dproj, w_in_p, "nt", F32, "proj_in_dx", tn=512, tk=D_IN_P)
    g_w_in_p = _mm(h1, dproj, "tn", F32, "proj_in_dw", tm=512, tn=640, tk=s)
    grad_x, sums_mix = _rms_mod_bwd(dh1, x, norm_mix_g, sc_m, dx2, "norm_mix_bwd")

    g_w_in = _w_in_unpadded(g_w_in_p)
    g_uq = g_w_uq_p.reshape(MLA_Q_RANK, N_HEADS, HEAD_PAD)[:, :, :96].transpose(1, 0, 2)
    g_uk = g_w_kv_comb[:256, :1024].reshape(256, N_HEADS, HEAD_PAD)[:, :, :64]
    g_uv = g_w_kv_comb[:256, 1024:].reshape(256, N_HEADS, HEAD_PAD)[:, :, :64]
    g_ukv = jnp.concatenate([g_uk, g_uv], axis=2).transpose(1, 0, 2)
    unpad_o = lambda g: g.reshape(N_HEADS, HEAD_PAD, 1024)[:, :64].reshape(512, 1024)
    g_conv_w = jnp.concatenate([s_gate[0:3], s_val[0:3]], axis=1)
    g_blocks = {
        "w_in": _full_to_cols(g_w_in, 533), "w_uq": g_uq, "w_ukv": g_ukv,
        "w_o_mla": _full_to_cols(unpad_o(g_w_o_mla_p), 128), "w_o_fox": _full_to_cols(unpad_o(g_w_o_fox_p), 128),
        "w_out": g_w_out.reshape(N_DEV, 128, 1024), "w_up": _full_to_cols(g_w_up, 704),
        "conv_w": _full_to_cols(g_conv_w, 704), "w_down": g_w_down.reshape(N_DEV, 352, 1024)}
    g_recv = _all_to_all([g_blocks[k].astype(BF16) for k in BIG], "scatter_grads")
    g_big, d_big, nm_big, nv_big = {}, {}, {}, {}
    for k, parts in zip(BIG, g_recv):
        g_big[k], d_big[k], nm_big[k], nv_big[k] = [
            t[None] for t in _adamw(parts, big_w[k][0], big_m[k][0], big_v[k][0], "adamw_" + k)]

    dada = jnp.concatenate([sums_mix[0:1], sums_mix[1:2], sums_gm[0:1], sums_ffn[0:1], sums_ffn[1:2], sums_gf[0:1]],
                           axis=1)
    small_part = _pack_small({
        "b_ada": dada, "norm_mix_g": sums_mix[2:3], "q_norm_g": sums_q[2:3], "kv_norm_g": sums_kv[2:3],
        "b_forget": db_f[:, 0, 0], "norm_ffn_g": sums_ffn[2:3],
        "conv_b": jnp.concatenate([s_gate[3:4], s_val[3:4]], axis=1), "norm_final_g": sums_final[0:1],
        "loss": sums_final[1:2, 0:1]})
    (small_all,) = _all_gather([small_part], "gather_small")
    zero1 = jnp.zeros((1,), F32)
    small_w = {"b_ada": b_ada, "norm_mix_g": norm_mix_g, "q_norm_g": q_norm_g, "kv_norm_g": kv_norm_g,
               "b_forget": b_forget, "norm_ffn_g": norm_ffn_g, "conv_b": conv_b, "norm_final_g": norm_final_g,
               "loss": zero1}
    small_m = {"b_ada": m_b_ada, "norm_mix_g": m_norm_mix_g, "q_norm_g": m_q_norm_g, "kv_norm_g": m_kv_norm_g,
               "b_forget": m_b_forget, "norm_ffn_g": m_norm_ffn_g, "conv_b": m_conv_b,
               "norm_final_g": m_norm_final_g, "loss": zero1}
    small_v = {"b_ada": v_b_ada, "norm_mix_g": v_norm_mix_g, "q_norm_g": v_q_norm_g, "kv_norm_g": v_kv_norm_g,
               "b_forget": v_b_forget, "norm_ffn_g": v_norm_ffn_g, "conv_b": v_conv_b,
               "norm_final_g": v_norm_final_g, "loss": zero1}
    g_sm, d_sm, nm_sm, nv_sm = _adamw(small_all, _pack_small(small_w), _pack_small(small_m), _pack_small(small_v),
                                      "adamw_small")
    loss = g_sm[0, SMALL_OFF["loss"]]

    dada_all = small_all[:, 0, SMALL_OFF["b_ada"]:SMALL_OFF["b_ada"] + N_ADA * D_MODEL]
    dada_mine = lax.dynamic_slice(dada_all, (0, me * 768), (N_DEV, 768))
    g_ada_local = _ada_bwd(act_col, dada_mine, "ada_bwd")
    g_ada, d_ada, nm_ada, nv_ada = _adamw(g_ada_local[None], w_ada[0], m_w_ada[0], v_w_ada[0], "adamw_ada")

    def small_out(t, nme, shape):
        real = dict((n_, r_) for n_, r_, _ in SMALL)[nme]
        o = SMALL_OFF[nme]
        return t[0, o:o + real].reshape(shape)

    order = ["w_ada", "b_ada", "norm_mix_g", "w_in", "q_norm_g", "w_uq", "kv_norm_g", "w_ukv", "b_forget",
             "w_o_mla", "w_o_fox", "w_out", "norm_ffn_g", "w_up", "conv_w", "conv_b", "w_down", "norm_final_g"]
    small_shapes = {"b_ada": (1, 6144), "norm_mix_g": (1, 1024), "q_norm_g": (1, 384), "kv_norm_g": (1, 256),
                    "b_forget": (1, 8), "norm_ffn_g": (1, 1024), "conv_b": (1, 5632), "norm_final_g": (1024,)}

    def family(big, small, ada_t):
        out = []
        for nme in order:
            if nme == "w_ada":
                out.append(ada_t[None])
            elif nme in small_shapes:
                out.append(small_out(small, nme, small_shapes[nme]))
            else:
                out.append(big[nme])
        return out

    return (loss, grad_x[None], *family(g_big, g_sm, g_ada), *family(d_big, d_sm, d_ada),
            *family(nm_big, nm_sm, nm_ada), *family(nv_big, nv_sm, nv_ada))
```

```python
import math

import numpy as np
import jax
import jax.numpy as jnp
from jax import lax
from jax.experimental import pallas as pl
from jax.experimental.pallas import tpu as pltpu

F32 = jnp.float32
BF16 = jnp.bfloat16

N_DEV = 8
D_MODEL = 1024
N_HEADS = 8
HEAD_PAD = 128
MLA_Q_RANK = 384
MLA_KV_RANK = 256
MLA_NOPE = 64
MLA_ROPE = 32
MLA_V = 64
FOX_DIM = 64
D_FF = 2816
N_ADA = 6
EPS = 1e-6
ROPE_THETA = 10000.0
MLA_SCALE = 1.0 / math.sqrt(MLA_NOPE + MLA_ROPE)
FOX_SCALE = 1.0 / math.sqrt(FOX_DIM)
IN_SPLITS = (384, 256, 32, 512, 512, 512, 8, 1024, 1024)
D_IN = sum(IN_SPLITS)
IN_OFF = tuple(int(v) for v in np.cumsum((0,) + IN_SPLITS))
P_GM, P_GF, P_FQ, P_FK, P_FV, P_CQ, P_CKV, P_KR, P_FL, D_IN_P = 0, 1024, 2048, 2560, 3072, 3584, 3968, 4224, 4352, 4480

ADAM_LR, ADAM_B1, ADAM_B2, ADAM_EPS, ADAM_WD, ADAM_STEP = 0.001, 0.9, 0.999, 1e-08, 0.01, 10

VMEM_LIMIT_BYTES = 56 * 1024 * 1024
NEG_BIG = -1e30
ATT_T = 512
LOG2E = 1.4426950408889634
SUM_LANE = 64
ROW_T = 256
SEQ_LANES = 128
BF16_ROWS = 16
FWD_HEADS_PER_STEP = 2
BWD_HEADS_PER_STEP = 2


def _params(sem):
    return pltpu.CompilerParams(dimension_semantics=sem, vmem_limit_bytes=VMEM_LIMIT_BYTES)


def _tile(n, target, step=128):
    if n <= target:
        return n
    t = (target // step) * step
    while t >= step:
        if n % t == 0:
            return t
        t -= step
    return n


def _vec_spec(w, nargs):
    if nargs == 1:
        return pl.BlockSpec((1, w), lambda i: (0, 0))
    return pl.BlockSpec((1, w), lambda i, j: (0, 0))


def _comm_call(body, name, ins, out_shapes):
    n = len(ins)
    any_spec = pl.BlockSpec(memory_space=pl.ANY)
    return pl.pallas_call(
        body, name=name, out_shape=tuple(out_shapes),
        in_specs=[any_spec] * n, out_specs=tuple([any_spec] * n),
        scratch_shapes=[pltpu.SemaphoreType.DMA((n, 7)), pltpu.SemaphoreType.DMA((n, 7)),
                        pltpu.SemaphoreType.DMA((n,))],
    )(*ins)


def _all_gather(xs, name):
    n = len(xs)

    def body(*refs):
        x_refs, out_refs = refs[:n], refs[n:2 * n]
        send_sems, recv_sems, local_sems = refs[2 * n:]
        x_, y_, c_ = lax.axis_index("x"), lax.axis_index("y"), lax.axis_index("c")
        me, sibling = (x_, y_, c_), (x_, y_, 1 - c_)
        chips = [(1 - x_, y_), (x_, 1 - y_), (1 - x_, 1 - y_)]

        def slot(a, px, py, pc):
            return out_refs[a].at[4 * px + 2 * py + pc]

        def copy(a, k, block, to, src=None):
            return pltpu.make_async_remote_copy(
                src_ref=slot(a, *block) if src is None else src, dst_ref=slot(a, *block),
                send_sem=send_sems.at[a, k], recv_sem=recv_sems.at[a, k],
                device_id=to, device_id_type=pl.DeviceIdType.MESH)

        mine = [pltpu.make_async_copy(x_refs[a], slot(a, *me), local_sems.at[a]) for a in range(n)]
        for cp in mine:
            cp.start()
        first = []
        for a in range(n):
            first.append(copy(a, 0, me, sibling, src=x_refs[a]))
            first += [copy(a, 1 + j, me, (*chip, c_), src=x_refs[a]) for j, chip in enumerate(chips)]
        for cp in first:
            cp.start()
        passed = []
        for j, chip in enumerate(chips):
            for a in range(n):
                copy(a, 1 + j, (*chip, c_), me).wait_recv()
                passed.append(copy(a, 4 + j, (*chip, c_), sibling))
                passed[-1].start()
        for a in range(n):
            copy(a, 0, sibling, me).wait_recv()
            for j, chip in enumerate(chips):
                copy(a, 4 + j, (*chip, 1 - c_), me).wait_recv()
        for cp in first + passed:
            cp.wait_send()
        for cp in mine:
            cp.wait()

    return _comm_call(body, name, xs, [jax.ShapeDtypeStruct((N_DEV,) + x.shape, x.dtype) for x in xs])


def _all_to_all(gs, name):
    n = len(gs)

    def body(*refs):
        g_refs, out_refs = refs[:n], refs[n:2 * n]
        send_sems, recv_sems, local_sems = refs[2 * n:]
        x_, y_, c_ = lax.axis_index("x"), lax.axis_index("y"), lax.axis_index("c")
        me = 4 * x_ + 2 * y_ + c_

        def peer(k):
            return (x_ ^ ((k >> 2) & 1), y_ ^ ((k >> 1) & 1), c_ ^ (k & 1))

        def copy(a, k, sending):
            px, py, pc = peer(k)
            theirs = 4 * px + 2 * py + pc
            return pltpu.make_async_remote_copy(
                src_ref=g_refs[a].at[theirs if sending else me], dst_ref=out_refs[a].at[me if sending else theirs],
                send_sem=send_sems.at[a, k - 1], recv_sem=recv_sems.at[a, k - 1],
                device_id=(px, py, pc), device_id_type=pl.DeviceIdType.MESH)

        mine = [pltpu.make_async_copy(g_refs[a].at[me], out_refs[a].at[me], local_sems.at[a]) for a in range(n)]
        for cp in mine:
            cp.start()
        sends = [copy(a, k, True) for a in range(n) for k in range(1, N_DEV)]
        for cp in sends:
            cp.start()
        for a in range(n):
            for k in range(1, N_DEV):
                copy(a, k, False).wait_recv()
        for cp in sends:
            cp.wait_send()
        for cp in mine:
            cp.wait()

    return _comm_call(body, name, gs, [jax.ShapeDtypeStruct(g.shape, g.dtype) for g in gs])


def _mm(a, b, mode, out_dtype, name, res=None, gvec=None, tm=1024, tn=512, tk=1024):
    (k, m) = a.shape if mode == "tn" else a.shape[::-1]
    n = b.shape[0] if mode == "nt" else b.shape[1]
    tm, tn, tk = _tile(m, tm), _tile(n, tn), _tile(k, tk)
    nk = k // tk
    dims = {"nn": (((1,), (0,)), ((), ())), "nt": (((1,), (1,)), ((), ())), "tn": (((0,), (0,)), ((), ()))}[mode]
    fused = res is not None

    def body(*refs):
        acc_ref = refs[-1] if nk > 1 else None
        if fused:
            a_ref, b_ref, res_ref, g_ref, o_ref, raw_ref = refs[:6]
        else:
            a_ref, b_ref, o_ref = refs[:3]
        part = lax.dot_general(a_ref[...], b_ref[...], dims, preferred_element_type=F32)

        def finish(acc):
            if fused:
                raw_ref[...] = acc
                o_ref[...] = (res_ref[...] + g_ref[...] * acc).astype(o_ref.dtype)
            else:
                o_ref[...] = acc.astype(o_ref.dtype)

        if nk == 1:
            finish(part)
            return
        kk = pl.program_id(2)

        @pl.when(kk == 0)
        def _():
            acc_ref[...] = part

        @pl.when(kk > 0)
        def _():
            acc_ref[...] += part

        @pl.when(kk == nk - 1)
        def _():
            finish(acc_ref[...])

    if mode == "tn":
        a_spec = pl.BlockSpec((tk, tm), lambda i, j, kk: (kk, i))
    else:
        a_spec = pl.BlockSpec((tm, tk), lambda i, j, kk: (i, kk))
    if mode == "nt":
        b_spec = pl.BlockSpec((tn, tk), lambda i, j, kk: (j, kk))
    else:
        b_spec = pl.BlockSpec((tk, tn), lambda i, j, kk: (kk, j))
    o_spec = pl.BlockSpec((tm, tn), lambda i, j, kk: (i, j))
    in_specs, args = [a_spec, b_spec], [a, b]
    out_specs, out_shape = o_spec, jax.ShapeDtypeStruct((m, n), out_dtype)
    if fused:
        in_specs += [o_spec, pl.BlockSpec((1, tn), lambda i, j, kk: (0, j))]
        args += [res, gvec]
        out_specs = (o_spec, o_spec)
        out_shape = (out_shape, jax.ShapeDtypeStruct((m, n), F32))
    return pl.pallas_call(
        body, name=name, grid=(m // tm, n // tn, nk),
        in_specs=in_specs, out_specs=out_specs, out_shape=out_shape,
        scratch_shapes=[pltpu.VMEM((tm, tn), F32)] if nk > 1 else [],
        compiler_params=_params(("parallel", "parallel", "arbitrary")),
    )(*args)


def _rms_mod(x, g, sc, sh, name):
    s, w = x.shape
    tm = _tile(s, ROW_T)

    def body(x_ref, g_ref, sc_ref, sh_ref, o_ref):
        xv = x_ref[...]
        r = lax.rsqrt(jnp.mean(xv * xv, axis=-1, keepdims=True) + EPS)
        o_ref[...] = ((xv * r * g_ref[...]) * (1.0 + sc_ref[...]) + sh_ref[...]).astype(o_ref.dtype)

    row = pl.BlockSpec((tm, w), lambda i: (i, 0))
    return pl.pallas_call(
        body, name=name, grid=(s // tm,),
        in_specs=[row, _vec_spec(w, 1), _vec_spec(w, 1), _vec_spec(w, 1)],
        out_specs=row, out_shape=jax.ShapeDtypeStruct((s, w), BF16),
        compiler_params=_params(("parallel",)),
    )(x, g, sc, sh)


def _rms_mod_bwd(dh, x, g, sc, dres, name):
    s, w = x.shape
    tm = _tile(s, ROW_T)
    has_res = dres is not None

    def body(*refs):
        if has_res:
            dh_ref, x_ref, g_ref, sc_ref, dres_ref, dx_ref, sums_ref = refs
        else:
            dh_ref, x_ref, g_ref, sc_ref, dx_ref, sums_ref = refs
        xv, dhv, gv = x_ref[...], dh_ref[...], g_ref[...]
        r = lax.rsqrt(jnp.mean(xv * xv, axis=-1, keepdims=True) + EPS)
        xhat = xv * r
        dxn = dhv * (1.0 + sc_ref[...])
        dxhat = dxn * gv
        dx = r * (dxhat - xhat * jnp.mean(dxhat * xhat, axis=-1, keepdims=True))
        if has_res:
            dx = dx + dres_ref[...]
        dx_ref[...] = dx

        @pl.when(pl.program_id(0) == 0)
        def _():
            sums_ref[...] = jnp.zeros_like(sums_ref)

        sums_ref[0:1, :] += jnp.sum(dhv, axis=0, keepdims=True)
        sums_ref[1:2, :] += jnp.sum(dhv * (xhat * gv), axis=0, keepdims=True)
        sums_ref[2:3, :] += jnp.sum(dxn * xhat, axis=0, keepdims=True)

    row = pl.BlockSpec((tm, w), lambda i: (i, 0))
    in_specs = [row, row, _vec_spec(w, 1), _vec_spec(w, 1)] + ([row] if has_res else [])
    args = [dh, x, g, sc] + ([dres] if has_res else [])
    return pl.pallas_call(
        body, name=name, grid=(s // tm,),
        in_specs=in_specs,
        out_specs=(row, pl.BlockSpec((8, w), lambda i: (0, 0))),
        out_shape=(jax.ShapeDtypeStruct((s, w), F32), jax.ShapeDtypeStruct((8, w), F32)),
        compiler_params=_params(("arbitrary",)),
    )(*args)


def _scale_bwd(dx, val, gvec, name):
    s, w = dx.shape
    tm = _tile(s, ROW_T)

    def body(dx_ref, val_ref, g_ref, d_ref, sums_ref):
        dxv = dx_ref[...]
        d_ref[...] = (dxv * g_ref[...]).astype(d_ref.dtype)

        @pl.when(pl.program_id(0) == 0)
        def _():
            sums_ref[...] = jnp.zeros_like(sums_ref)

        sums_ref[0:1, :] += jnp.sum(dxv * val_ref[...], axis=0, keepdims=True)

    row = pl.BlockSpec((tm, w), lambda i: (i, 0))
    return pl.pallas_call(
        body, name=name, grid=(s // tm,),
        in_specs=[row, row, _vec_spec(w, 1)],
        out_specs=(row, pl.BlockSpec((8, w), lambda i: (0, 0))),
        out_shape=(jax.ShapeDtypeStruct((s, w), BF16), jax.ShapeDtypeStruct((8, w), F32)),
        compiler_params=_params(("arbitrary",)),
    )(dx, val, gvec)


def _final_loss(x3, target, g, name):
    s, w = x3.shape
    tm = _tile(s, ROW_T)

    def body(x_ref, t_ref, g_ref, dx_ref, sums_ref):
        xv, gv = x_ref[...], g_ref[...]
        r = lax.rsqrt(jnp.mean(xv * xv, axis=-1, keepdims=True) + EPS)
        xhat = xv * r
        err = xhat * gv - t_ref[...]
        dy = err * (1.0 / w)
        dxhat = dy * gv
        dx_ref[...] = r * (dxhat - xhat * jnp.mean(dxhat * xhat, axis=-1, keepdims=True))

        @pl.when(pl.program_id(0) == 0)
        def _():
            sums_ref[...] = jnp.zeros_like(sums_ref)

        sums_ref[0:1, :] += jnp.sum(dy * xhat, axis=0, keepdims=True)
        sums_ref[1:2, :] += jnp.zeros((1, w), F32) + (0.5 / w) * jnp.sum(err * err)

    row = pl.BlockSpec((tm, w), lambda i: (i, 0))
    return pl.pallas_call(
        body, name=name, grid=(s // tm,),
        in_specs=[row, row, _vec_spec(w, 1)],
        out_specs=(row, pl.BlockSpec((8, w), lambda i: (0, 0))),
        out_shape=(jax.ShapeDtypeStruct((s, w), F32), jax.ShapeDtypeStruct((8, w), F32)),
        compiler_params=_params(("arbitrary",)),
    )(x3, target, g)


def _rope_block(seg, cmul, smul):
    lane = lax.broadcasted_iota(jnp.int32, seg.shape, 1)
    swapped = jnp.where(lane < MLA_NOPE + MLA_ROPE // 2,
                        pltpu.roll(seg, HEAD_PAD - MLA_ROPE // 2, 1), pltpu.roll(seg, MLA_ROPE // 2, 1))
    return seg * cmul + swapped * smul


def _rope(t, cmul, smul, name, n_rot, out_dtypes):
    s, w = t.shape
    tm = _tile(s, ROW_T)
    n_out = len(out_dtypes)
    wo = w // n_out

    def body(t_ref, c_ref, s_ref, *o_refs):
        cv, sv = c_ref[...], s_ref[...]
        one = (lax.broadcasted_iota(jnp.int32, (tm, HEAD_PAD), 1) == SUM_LANE).astype(F32)
        for hb in range(w // HEAD_PAD):
            seg = t_ref[:, hb * HEAD_PAD:(hb + 1) * HEAD_PAD].astype(F32)
            if hb < n_rot:
                seg = _rope_block(seg, cv, sv)
            else:
                seg = seg + one
            o_ref = o_refs[(hb * HEAD_PAD) // wo]
            col = (hb * HEAD_PAD) % wo
            o_ref[:, col:col + HEAD_PAD] = seg.astype(o_ref.dtype)

    row = pl.BlockSpec((tm, w), lambda i: (i, 0))
    tab = pl.BlockSpec((tm, HEAD_PAD), lambda i: (i, 0))
    orow = pl.BlockSpec((tm, wo), lambda i: (i, 0))
    outs = pl.pallas_call(
        body, name=name, grid=(s // tm,),
        in_specs=[row, tab, tab],
        out_specs=tuple(orow for _ in out_dtypes),
        out_shape=tuple(jax.ShapeDtypeStruct((s, wo), dt) for dt in out_dtypes),
        compiler_params=_params(("parallel",)),
    )(t, cmul, smul)
    return outs


def _rope_bwd_kv(dk, dv, cmul, smul, name):
    s, w = dk.shape
    tm = _tile(s, ROW_T)

    def body(dk_ref, dv_ref, c_ref, s_ref, o_ref):
        cv, sv = c_ref[...], s_ref[...]
        for hb in range(N_HEADS):
            lo, hi = hb * HEAD_PAD, (hb + 1) * HEAD_PAD
            o_ref[:, lo:hi] = _rope_block(dk_ref[:, lo:hi].astype(F32), cv, sv).astype(o_ref.dtype)
        o_ref[:, w:2 * w] = dv_ref[...].astype(o_ref.dtype)

    row = pl.BlockSpec((tm, w), lambda i: (i, 0))
    tab = pl.BlockSpec((tm, HEAD_PAD), lambda i: (i, 0))
    return pl.pallas_call(
        body, name=name, grid=(s // tm,),
        in_specs=[row, row, tab, tab],
        out_specs=pl.BlockSpec((tm, 2 * w), lambda i: (i, 0)),
        out_shape=jax.ShapeDtypeStruct((s, 2 * w), BF16),
        compiler_params=_params(("parallel",)),
    )(dk, dv, cmul, smul)


def _lanes(col, width):
    if col.shape[1] == 1:
        col = jnp.broadcast_to(col, (col.shape[0], HEAD_PAD))
    return jnp.tile(col, (1, width // HEAD_PAD))


def _fold_lanes(a):
    out = a[:, 0:HEAD_PAD]
    for g in range(1, a.shape[1] // HEAD_PAD):
        out = out + a[:, g * HEAD_PAD:(g + 1) * HEAD_PAD]
    return out


def _as_row(rep):
    return rep.T[0:1, :]


def _causal(t, rows_are_queries):
    row = lax.broadcasted_iota(jnp.int32, (t, t), 0)
    col = lax.broadcasted_iota(jnp.int32, (t, t), 1)
    return row >= col if rows_are_queries else col >= row


def _attn_fwd(q, k, v, fcol, frow, name):
    s = q.shape[0]
    t = ATT_T
    nq = s // t
    use_f = fcol is not None

    hpb = FWD_HEADS_PER_STEP

    def body(*refs):
        if use_f:
            q_ref, k_ref, v_ref, fc_ref, fr_ref, o_ref, ob_ref, lse_ref, m_s, acc_s = refs
            fc_b = [jnp.broadcast_to(fc_ref[hh], (t, HEAD_PAD)) for hh in range(hpb)]
        else:
            q_ref, k_ref, v_ref, o_ref, ob_ref, lse_ref, m_s, acc_s = refs
        qi = pl.program_id(1)
        m_s[...] = jnp.full(m_s.shape, NEG_BIG, F32)
        acc_s[...] = jnp.zeros(acc_s.shape, F32)

        def step(j, masked):
            off = pl.multiple_of(j * t, t)
            for hh in range(hpb):
                lanes = slice(hh * HEAD_PAD, (hh + 1) * HEAD_PAD)
                kv = k_ref[pl.ds(off, t), lanes]
                vv = v_ref[pl.ds(off, t), lanes]
                sc = lax.dot_general(q_ref[:, lanes], kv, (((1,), (1,)), ((), ())), preferred_element_type=F32)
                if use_f:
                    sc = sc + (_lanes(fc_b[hh], t) - fr_ref[hh, j])
                if masked:
                    sc = jnp.where(_causal(t, True), sc, NEG_BIG)
                m_prev = m_s[hh]
                m_new = jnp.maximum(m_prev, jnp.max(sc, axis=-1, keepdims=True))
                p = jnp.exp2(sc - _lanes(m_new, t))
                acc_s[hh] = jnp.exp2(m_prev - m_new) * acc_s[hh] + jnp.dot(p.astype(BF16), vv,
                                                                           preferred_element_type=F32)
                m_s[hh] = m_new

        def loop_body(j, carry):
            step(j, False)
            return carry

        lax.fori_loop(0, qi, loop_body, 0)
        step(qi, True)
        for hh in range(hpb):
            lanes = slice(hh * HEAD_PAD, (hh + 1) * HEAD_PAD)
            acc = acc_s[hh]
            lane = lax.broadcasted_iota(jnp.int32, acc.shape, 1)
            denom = jnp.sum(jnp.where(lane == SUM_LANE, acc, 0.0), axis=-1, keepdims=True)
            o = acc * (1.0 / denom)
            o_ref[:, lanes] = o
            ob_ref[:, lanes] = o.astype(BF16)
            lse_ref[hh, 0] = _as_row(m_s[hh] + jnp.log(denom) * LOG2E)

    w = hpb * HEAD_PAD
    qspec = pl.BlockSpec((t, w), lambda h, i: (i, h))
    kspec = pl.BlockSpec((s, w), lambda h, i: (0, h))
    colspec = pl.BlockSpec((hpb, t, 1), lambda h, i: (h, i, 0))
    in_specs, args = [qspec, kspec, kspec], [q, k, v]
    if use_f:
        in_specs += [colspec, pl.BlockSpec((hpb, nq, 1, t), lambda h, i: (h, 0, 0, 0))]
        args += [fcol, frow]
    return pl.pallas_call(
        body, name=name, grid=(N_HEADS // hpb, nq),
        in_specs=in_specs,
        out_specs=(qspec, qspec, pl.BlockSpec((hpb, 1, 1, t), lambda h, i: (h, i, 0, 0))),
        out_shape=(jax.ShapeDtypeStruct((s, N_HEADS * HEAD_PAD), F32), jax.ShapeDtypeStruct((s, N_HEADS * HEAD_PAD), BF16),
                   jax.ShapeDtypeStruct((N_HEADS, nq, 1, t), F32)),
        scratch_shapes=[pltpu.VMEM((hpb, t, HEAD_PAD), F32), pltpu.VMEM((hpb, t, HEAD_PAD), F32)],
        compiler_params=_params(("parallel", "arbitrary")),
    )(*args)


def _attn_delta(o, do, name):
    s, w = o.shape
    t = ATT_T

    def body(o_ref, do_ref, d_ref):
        for hb in range(N_HEADS):
            lo, hi = hb * HEAD_PAD, (hb + 1) * HEAD_PAD
            prod = o_ref[:, lo:hi] * do_ref[:, lo:hi].astype(F32)
            d_ref[hb, 0] = jnp.sum(prod.T, axis=0, keepdims=True)

    row = pl.BlockSpec((t, w), lambda i: (i, 0))
    return pl.pallas_call(
        body, name=name, grid=(s // t,),
        in_specs=[row, row],
        out_specs=pl.BlockSpec((N_HEADS, 1, 1, t), lambda i: (0, i, 0, 0)),
        out_shape=jax.ShapeDtypeStruct((N_HEADS, s // t, 1, t), F32),
        compiler_params=_params(("parallel",)),
    )(o, do)


def _attn_bwd(q, k, v, do, lse_row, delta_row, fcol, frow, scale_q, scale_k, name, out_dtype):
    s = q.shape[0]
    t = ATT_T
    nq = s // t
    use_f = fcol is not None
    hpb = BWD_HEADS_PER_STEP

    def body(*refs):
        if use_f:
            (q_ref, k_ref, v_ref, do_ref, lse_ref, dl_ref, fc_ref, fr_ref,
             dq_ref, dk_ref, dv_ref, dr_ref, df_ref, dq_s, dk_s, dv_s, dr_s, df_s) = refs
            fc_b = [jnp.broadcast_to(fc_ref[hh], (t, HEAD_PAD)) for hh in range(hpb)]
        else:
            q_ref, k_ref, v_ref, do_ref, lse_ref, dl_ref, dq_ref, dk_ref, dv_ref, dq_s, dk_s, dv_s = refs
        kj = pl.program_id(1)

        @pl.when(kj == 0)
        def _():
            dq_s[...] = jnp.zeros(dq_s.shape, F32)
            if use_f:
                dr_s[...] = jnp.zeros(dr_s.shape, F32)

        dk_s[...] = jnp.zeros(dk_s.shape, F32)
        dv_s[...] = jnp.zeros(dv_s.shape, F32)
        if use_f:
            df_s[...] = jnp.zeros(df_s.shape, F32)

        def step(i, masked):
            off = pl.multiple_of(i * t, t)
            for hh in range(hpb):
                lanes = slice(hh * HEAD_PAD, (hh + 1) * HEAD_PAD)
                kv, vv = k_ref[:, lanes], v_ref[:, lanes]
                qv = q_ref[pl.ds(off, t), lanes]
                dov = do_ref[pl.ds(off, t), lanes]
                st = lax.dot_general(kv, qv, (((1,), (1,)), ((), ())), preferred_element_type=F32)
                if use_f:
                    st = st + (fr_ref[hh, i] - _lanes(fc_b[hh], t))
                if masked:
                    st = jnp.where(_causal(t, False), st, NEG_BIG)
                pt = jnp.exp2(st - lse_ref[hh, i])
                dv_s[hh] += jnp.dot(pt.astype(BF16), dov, preferred_element_type=F32)
                dpt = lax.dot_general(vv, dov, (((1,), (1,)), ((), ())), preferred_element_type=F32)
                dst = pt * (dpt - dl_ref[hh, i])
                dsb = dst.astype(BF16)
                dk_s[hh] += jnp.dot(dsb, qv, preferred_element_type=F32)
                dq_s[hh, pl.ds(off, t), :] += lax.dot_general(dsb, kv, (((0,), (0,)), ((), ())),
                                                              preferred_element_type=F32)
                if use_f:
                    df_s[hh] -= _fold_lanes(dst)
                    dr_s[hh, i] += jnp.sum(dst, axis=0, keepdims=True)

        step(kj, True)

        def loop_body(i, carry):
            step(i, False)
            return carry

        lax.fori_loop(kj + 1, nq, loop_body, 0)
        for hh in range(hpb):
            lanes = slice(hh * HEAD_PAD, (hh + 1) * HEAD_PAD)
            dk_ref[:, lanes] = (dk_s[hh] * scale_k).astype(dk_ref.dtype)
            dv_ref[:, lanes] = dv_s[hh].astype(dv_ref.dtype)
            if use_f:
                df_ref[hh, 0] = jnp.sum(df_s[hh].T, axis=0, keepdims=True)

        @pl.when(kj == nq - 1)
        def _():
            for hh in range(hpb):
                dq_ref[:, hh * HEAD_PAD:(hh + 1) * HEAD_PAD] = (dq_s[hh] * scale_q).astype(dq_ref.dtype)
            if use_f:
                dr_ref[...] = dr_s[...]

    w = hpb * HEAD_PAD
    kspec = pl.BlockSpec((t, w), lambda h, j: (j, h))
    qspec = pl.BlockSpec((s, w), lambda h, j: (0, h))
    rowspec = pl.BlockSpec((hpb, nq, 1, t), lambda h, j: (h, 0, 0, 0))
    in_specs, args = [qspec, kspec, kspec, qspec, rowspec, rowspec], [q, k, v, do, lse_row, delta_row]
    full = jax.ShapeDtypeStruct((s, N_HEADS * HEAD_PAD), out_dtype)
    out_specs, out_shape = [qspec, kspec, kspec], [full, full, full]
    scratch = [pltpu.VMEM((hpb, s, HEAD_PAD), F32), pltpu.VMEM((hpb, t, HEAD_PAD), F32),
               pltpu.VMEM((hpb, t, HEAD_PAD), F32)]
    if use_f:
        in_specs += [pl.BlockSpec((hpb, t, 1), lambda h, j: (h, j, 0)), rowspec]
        args += [fcol, frow]
        out_specs += [rowspec, pl.BlockSpec((hpb, 1, 1, t), lambda h, j: (h, j, 0, 0))]
        out_shape += [jax.ShapeDtypeStruct((N_HEADS, nq, 1, t), F32)] * 2
        scratch += [pltpu.VMEM((hpb, nq, 1, t), F32), pltpu.VMEM((hpb, t, HEAD_PAD), F32)]
    return pl.pallas_call(
        body, name=name, grid=(N_HEADS // hpb, nq),
        in_specs=in_specs, out_specs=tuple(out_specs), out_shape=tuple(out_shape),
        scratch_shapes=scratch,
        compiler_params=_params(("parallel", "arbitrary")),
    )(*args)


def _gate_fwd(pm, pf, proj, name):
    s, w = pm.shape
    tm = _tile(s, ROW_T)

    def body(pm_ref, pf_ref, gm_ref, gf_ref, y_ref):
        y = jax.nn.sigmoid(gm_ref[...]) * pm_ref[...] + jax.nn.sigmoid(gf_ref[...]) * pf_ref[...]
        y_ref[...] = y.astype(y_ref.dtype)

    row = pl.BlockSpec((tm, w), lambda i: (i, 0))
    return pl.pallas_call(
        body, name=name, grid=(s // tm,),
        in_specs=[row, row, pl.BlockSpec((tm, w), lambda i: (i, P_GM // D_MODEL)),
                  pl.BlockSpec((tm, w), lambda i: (i, P_GF // D_MODEL))],
        out_specs=row, out_shape=jax.ShapeDtypeStruct((s, w), BF16),
        compiler_params=_params(("parallel",)),
    )(pm, pf, proj, proj)


def _gate_bwd(dy, pm, pf, proj, name):
    s, w = pm.shape
    tm = _tile(s, ROW_T)

    def body(dy_ref, pm_ref, pf_ref, gm_ref, gf_ref, dpm_ref, dpf_ref, dgm_ref, dgf_ref):
        dyv = dy_ref[...]
        sm, sf = jax.nn.sigmoid(gm_ref[...]), jax.nn.sigmoid(gf_ref[...])
        dpm_ref[...] = (dyv * sm).astype(BF16)
        dpf_ref[...] = (dyv * sf).astype(BF16)
        dgm_ref[...] = (dyv * pm_ref[...] * (sm * (1.0 - sm))).astype(BF16)
        dgf_ref[...] = (dyv * pf_ref[...] * (sf * (1.0 - sf))).astype(BF16)

    row = pl.BlockSpec((tm, w), lambda i: (i, 0))
    out = jax.ShapeDtypeStruct((s, w), BF16)
    return pl.pallas_call(
        body, name=name, grid=(s // tm,),
        in_specs=[row, row, row, pl.BlockSpec((tm, w), lambda i: (i, P_GM // D_MODEL)),
                  pl.BlockSpec((tm, w), lambda i: (i, P_GF // D_MODEL))],
        out_specs=(row, row, row, row), out_shape=(out, out, out, out),
        compiler_params=_params(("parallel",)),
    )(dy, pm, pf, proj, proj)


CONV_TN = 256
CONV_TM = 512
HALO = BF16_ROWS


def _shift_down(u, prev, n):
    rolled = pltpu.roll(u, n, 0)
    prev_rolled = pltpu.roll(prev, n, 0)
    top = jnp.concatenate([prev_rolled, rolled[HALO:]], axis=0)
    row = lax.broadcasted_iota(jnp.int32, u.shape, 0)
    return jnp.where(row < n, top, rolled)


def _conv_tile(u, prev, w_ref, b_ref):
    um1 = _shift_down(u, prev, 1)
    um2 = _shift_down(u, prev, 2)
    uc = b_ref[...] + w_ref[0:1, :] * um2 + w_ref[1:2, :] * um1 + w_ref[2:3, :] * u
    return uc, um1, um2


def _conv_specs(tm, tn, ncol_off):
    blk = lambda off: pl.BlockSpec((tm, tn), lambda j, i: (i, j + off))
    halo = lambda off: pl.BlockSpec((HALO, tn), lambda j, i: (jnp.maximum(i * (tm // HALO) - 1, 0), j + off))
    wsp = lambda off: pl.BlockSpec((3, tn), lambda j, i: (0, j + off))
    bsp = lambda off: pl.BlockSpec((1, tn), lambda j, i: (0, j + off))
    return blk, halo, wsp, bsp


def _convglu_fwd(u, conv_w, conv_b, name):
    s = u.shape[0]
    tm, tn = _tile(s, CONV_TM), CONV_TN
    nj = D_FF // tn
    blk, halo, wsp, bsp = _conv_specs(tm, tn, nj)

    def body(ug_ref, pg_ref, uv_ref, pv_ref, wg_ref, wv_ref, bg_ref, bv_ref, a_ref):
        live = (pl.program_id(1) > 0).astype(F32)
        gate, _, _ = _conv_tile(ug_ref[...].astype(F32), pg_ref[...].astype(F32) * live, wg_ref, bg_ref)
        val, _, _ = _conv_tile(uv_ref[...].astype(F32), pv_ref[...].astype(F32) * live, wv_ref, bv_ref)
        a_ref[...] = (gate * jax.nn.sigmoid(gate) * val).astype(a_ref.dtype)

    return pl.pallas_call(
        body, name=name, grid=(nj, s // tm),
        in_specs=[blk(0), halo(0), blk(nj), halo(nj), wsp(0), wsp(nj), bsp(0), bsp(nj)],
        out_specs=blk(0), out_shape=jax.ShapeDtypeStruct((s, D_FF), BF16),
        compiler_params=_params(("parallel", "arbitrary")),
    )(u, u, u, u, conv_w, conv_w, conv_b, conv_b)


def _convglu_bwd(da, u, conv_w, conv_b, name):
    s = u.shape[0]
    tm, tn = _tile(s, CONV_TM), CONV_TN
    nj = D_FF // tn
    blk, halo, wsp, bsp = _conv_specs(tm, tn, nj)

    def body(da_ref, ug_ref, pg_ref, uv_ref, pv_ref, wg_ref, wv_ref, bg_ref, bv_ref,
             dg_ref, dv_ref, sg_ref, sv_ref):
        live = (pl.program_id(1) > 0).astype(F32)
        ug, uv = ug_ref[...].astype(F32), uv_ref[...].astype(F32)
        gate, ug1, ug2 = _conv_tile(ug, pg_ref[...].astype(F32) * live, wg_ref, bg_ref)
        val, uv1, uv2 = _conv_tile(uv, pv_ref[...].astype(F32) * live, wv_ref, bv_ref)
        dav = da_ref[...].astype(F32)
        sig = jax.nn.sigmoid(gate)
        dgate = dav * val * (sig * (1.0 + gate * (1.0 - sig)))
        dval = dav * (gate * sig)
        dg_ref[...] = dgate.astype(dg_ref.dtype)
        dv_ref[...] = dval.astype(dv_ref.dtype)

        @pl.when(pl.program_id(1) == 0)
        def _():
            sg_ref[...] = jnp.zeros_like(sg_ref)
            sv_ref[...] = jnp.zeros_like(sv_ref)

        for s_ref, d, taps in ((sg_ref, dgate, (ug2, ug1, ug)), (sv_ref, dval, (uv2, uv1, uv))):
            for r, tap in enumerate(taps):
                s_ref[r:r + 1, :] += jnp.sum(d * tap, axis=0, keepdims=True)
            s_ref[3:4, :] += jnp.sum(d, axis=0, keepdims=True)

    sums = lambda off: pl.BlockSpec((8, tn), lambda j, i: (0, j + off))
    return pl.pallas_call(
        body, name=name, grid=(nj, s // tm),
        in_specs=[blk(0), blk(0), halo(0), blk(nj), halo(nj), wsp(0), wsp(nj), bsp(0), bsp(nj)],
        out_specs=(blk(0), blk(0), sums(0), sums(0)),
        out_shape=(jax.ShapeDtypeStruct((s, D_FF), BF16), jax.ShapeDtypeStruct((s, D_FF), BF16),
                   jax.ShapeDtypeStruct((8, D_FF), F32), jax.ShapeDtypeStruct((8, D_FF), F32)),
        compiler_params=_params(("parallel", "arbitrary")),
    )(da, u, u, u, u, conv_w, conv_w, conv_b, conv_b)


def _conv_transpose(d, conv_w_half, name):
    s, w = d.shape
    tm, tn = _tile(s, CONV_TM), CONV_TN
    last = s // tm - 1

    def body(d_ref, nx_ref, w_ref, o_ref):
        dv = d_ref[...].astype(F32)
        nxt = nx_ref[...].astype(F32) * (pl.program_id(1) < last).astype(F32)
        row = lax.broadcasted_iota(jnp.int32, dv.shape, 0)

        def shift_up(n):
            rolled = pltpu.roll(dv, tm - n, 0)
            nxt_rolled = pltpu.roll(nxt, HALO - n, 0)
            bottom = jnp.concatenate([rolled[:tm - HALO], nxt_rolled], axis=0)
            return jnp.where(row >= tm - n, bottom, rolled)

        out = w_ref[2:3, :] * dv + w_ref[1:2, :] * shift_up(1) + w_ref[0:1, :] * shift_up(2)
        o_ref[...] = out.astype(o_ref.dtype)

    blk = pl.BlockSpec((tm, tn), lambda j, i: (i, j))
    nxt_spec = pl.BlockSpec((HALO, tn), lambda j, i: (jnp.minimum((i + 1) * (tm // HALO), s // HALO - 1), j))
    return pl.pallas_call(
        body, name=name, grid=(w // tn, s // tm),
        in_specs=[blk, nxt_spec, pl.BlockSpec((3, tn), lambda j, i: (0, j))],
        out_specs=blk, out_shape=jax.ShapeDtypeStruct((s, w), BF16),
        compiler_params=_params(("parallel", "arbitrary")),
    )(d, d, conv_w_half)


def _split3(a):
    a1 = a.astype(BF16)
    r1 = a - a1.astype(F32)
    a2 = r1.astype(BF16)
    a3 = (r1 - a2.astype(F32)).astype(BF16)
    return a1, a2, a3


def _ones_dot_right(a, mat):
    return sum(jnp.dot(p, mat, preferred_element_type=F32) for p in _split3(a))


def _ones_dot_left(mat, a):
    return sum(jnp.dot(mat, p, preferred_element_type=F32) for p in _split3(a))


def _tri(n, cmp):
    r = lax.broadcasted_iota(jnp.int32, (n, n), 0)
    c = lax.broadcasted_iota(jnp.int32, (n, n), 1)
    return cmp(r, c).astype(BF16)


def _forget_fwd(z, bias, name):
    nh, nr, nl = z.shape

    def body(z_ref, b_ref, f_ref):
        within = _tri(nl, lambda r, c: r <= c)
        before = _tri(nr, lambda r, c: c < r)
        for h in range(nh):
            x = z_ref[h] + b_ref[h]
            lf = jnp.minimum(x, 0.0) - jnp.log(1.0 + jnp.exp(-jnp.abs(x)))
            pre = _ones_dot_right(lf, within)
            tot = jnp.zeros((nr, nl), F32) + jnp.sum(lf, axis=1, keepdims=True)
            f_ref[h] = pre + _ones_dot_left(before, tot)

    return pl.pallas_call(
        body, name=name, out_shape=jax.ShapeDtypeStruct(z.shape, F32),
        compiler_params=pltpu.CompilerParams(vmem_limit_bytes=VMEM_LIMIT_BYTES),
    )(z, bias)


def _forget_bwd(df_rows, df_cols, z, bias, name):
    nh, nr, nl = z.shape

    def body(dfr_ref, dfc_ref, z_ref, b_ref, dz_ref, db_ref):
        within = _tri(nl, lambda r, c: r >= c)
        after = _tri(nr, lambda r, c: c > r)
        for h in range(nh):
            g = dfr_ref[h] + dfc_ref[h]
            suf = _ones_dot_right(g, within)
            tot = jnp.zeros((nr, nl), F32) + jnp.sum(g, axis=1, keepdims=True)
            dlf = suf + _ones_dot_left(after, tot)
            dz = dlf * jax.nn.sigmoid(-(z_ref[h] + b_ref[h]))
            dz_ref[h] = dz
            db_ref[h] = jnp.zeros((1, nl), F32) + jnp.sum(dz)

    return pl.pallas_call(
        body, name=name,
        out_shape=(jax.ShapeDtypeStruct(z.shape, F32), jax.ShapeDtypeStruct(bias.shape, F32)),
        compiler_params=pltpu.CompilerParams(vmem_limit_bytes=VMEM_LIMIT_BYTES),
    )(df_rows, df_cols, z, bias)


def _ada_fwd(c_col, w, b, name):
    kdim, n = w.shape

    def body(c_ref, w_ref, b_ref, ada_ref, act_ref):
        wv = w_ref[...]
        for e in range(N_DEV):
            cv = c_ref[e]
            act = cv * jax.nn.sigmoid(cv)
            act_ref[e] = act
            ada_ref[e:e + 1, :] = jnp.sum(act * wv, axis=0, keepdims=True) + b_ref[...]

    return pl.pallas_call(
        body, name=name,
        out_shape=(jax.ShapeDtypeStruct((N_DEV, n), F32), jax.ShapeDtypeStruct((N_DEV, kdim, 1), F32)),
        compiler_params=pltpu.CompilerParams(vmem_limit_bytes=VMEM_LIMIT_BYTES),
    )(c_col, w, b)


def _ada_bwd(act_col, dada, name):
    kdim = act_col.shape[1]
    n = dada.shape[1]

    def body(act_ref, d_ref, g_ref):
        acc = act_ref[0] * d_ref[0:1, :]
        for e in range(1, N_DEV):
            acc = acc + act_ref[e] * d_ref[e:e + 1, :]
        g_ref[...] = acc

    return pl.pallas_call(
        body, name=name, out_shape=jax.ShapeDtypeStruct((kdim, n), F32),
        compiler_params=pltpu.CompilerParams(vmem_limit_bytes=VMEM_LIMIT_BYTES),
    )(act_col, dada)


def _adamw(parts, w, m, v, name, tr=128):
    npart, r, c = parts.shape
    tr = _tile(r, tr, step=BF16_ROWS) if r % BF16_ROWS == 0 else r

    def body(p_ref, w_ref, m_ref, v_ref, g_ref, d_ref, nm_ref, nv_ref):
        g = p_ref[0].astype(F32)
        for e in range(1, npart):
            g = g + p_ref[e].astype(F32)
        nm = ADAM_B1 * m_ref[...] + (1.0 - ADAM_B1) * g
        nv = ADAM_B2 * v_ref[...] + (1.0 - ADAM_B2) * (g * g)
        m_hat = nm / (1.0 - ADAM_B1 ** ADAM_STEP)
        v_hat = nv / (1.0 - ADAM_B2 ** ADAM_STEP)
        g_ref[...] = g
        d_ref[...] = -ADAM_LR * (m_hat / (jnp.sqrt(v_hat) + ADAM_EPS) + ADAM_WD * w_ref[...])
        nm_ref[...] = nm
        nv_ref[...] = nv

    row = pl.BlockSpec((tr, c), lambda i: (i, 0))
    out = jax.ShapeDtypeStruct((r, c), F32)
    return pl.pallas_call(
        body, name=name, grid=(r // tr,),
        in_specs=[pl.BlockSpec((npart, tr, c), lambda i: (0, i, 0)), row, row, row],
        out_specs=(row, row, row, row), out_shape=(out, out, out, out),
        compiler_params=_params(("parallel",)),
    )(parts, w, m, v)


BIG = ("w_in", "w_uq", "w_ukv", "w_o_mla", "w_o_fox", "w_out", "w_up", "conv_w", "w_down")


def _cols_to_full(stack):
    n, r, c = stack.shape
    return stack.transpose(1, 0, 2).reshape(r, n * c)


def _full_to_cols(full, c):
    r = full.shape[0]
    return full.reshape(r, N_DEV, c).transpose(1, 0, 2)


def _pad_heads(a, width, ones_lane=False):
    s = a.shape[0]
    a = a.reshape(s, N_HEADS, width)
    if ones_lane:
        assert width == SUM_LANE
        tail = jnp.zeros((s, N_HEADS, HEAD_PAD - width), a.dtype).at[:, :, 0].set(1.0)
        return jnp.concatenate([a, tail], axis=2).reshape(s, N_HEADS * HEAD_PAD)
    return jnp.pad(a, ((0, 0), (0, 0), (0, HEAD_PAD - width))).reshape(s, N_HEADS * HEAD_PAD)


def _unpad_heads(a, width):
    s = a.shape[0]
    return a.reshape(s, N_HEADS, HEAD_PAD)[:, :, :width].reshape(s, N_HEADS * width)


def _w_in_padded(w_in):
    seg = [w_in[:, IN_OFF[i]:IN_OFF[i + 1]] for i in range(9)]
    cq, ckv, kr, fq, fk, fv, fl, gm, gf = seg
    padc = lambda a, n: jnp.pad(a, ((0, 0), (0, n - a.shape[1])))
    return jnp.concatenate([gm, gf, fq, fk, fv, cq, ckv, padc(kr, 128), padc(fl, 128)], axis=1)


def _w_in_unpadded(g):
    return jnp.concatenate([
        g[:, P_CQ:P_CQ + 384], g[:, P_CKV:P_CKV + 256], g[:, P_KR:P_KR + 32], g[:, P_FQ:P_FQ + 512],
        g[:, P_FK:P_FK + 512], g[:, P_FV:P_FV + 512], g[:, P_FL:P_FL + 8], g[:, P_GM:P_GM + 1024],
        g[:, P_GF:P_GF + 1024]], axis=1)


SMALL = (("b_ada", 6144, 6144), ("norm_mix_g", 1024, 1024), ("q_norm_g", 384, 384), ("kv_norm_g", 256, 256),
         ("b_forget", 8, 128), ("norm_ffn_g", 1024, 1024), ("conv_b", 5632, 5632), ("norm_final_g", 1024, 1024),
         ("loss", 1, 128))
SMALL_OFF = {}
_o = 0
for _n, _real, _padded in SMALL:
    SMALL_OFF[_n] = _o
    _o += _padded
SMALL_W = _o


def _pack_small(vals):
    parts = []
    for nme, real, padded in SMALL:
        a = vals[nme].reshape(1, real).astype(F32)
        parts.append(jnp.pad(a, ((0, 0), (0, padded - real))))
    return jnp.concatenate(parts, axis=1)


def kernel(x, c, positions, w_ada, b_ada, norm_mix_g, w_in, q_norm_g, w_uq, kv_norm_g, w_ukv, b_forget, w_o_mla, w_o_fox, w_out, norm_ffn_g, w_up, conv_w, conv_b, w_down, norm_final_g, loss_target, m_w_ada, m_b_ada, m_norm_mix_g, m_w_in, m_q_norm_g, m_w_uq, m_kv_norm_g, m_w_ukv, m_b_forget, m_w_o_mla, m_w_o_fox, m_w_out, m_norm_ffn_g, m_w_up, m_conv_w, m_conv_b, m_w_down, m_norm_final_g, v_w_ada, v_b_ada, v_norm_mix_g, v_w_in, v_q_norm_g, v_w_uq, v_kv_norm_g, v_w_ukv, v_b_forget, v_w_o_mla, v_w_o_fox, v_w_out, v_norm_ffn_g, v_w_up, v_conv_w, v_conv_b, v_w_down, v_norm_final_g):
    me = 4 * lax.axis_index("x") + 2 * lax.axis_index("y") + lax.axis_index("c")
    x = x[0]
    target = loss_target[0]
    s = x.shape[0]
    nblk = s // ATT_T
    big_w = {"w_in": w_in, "w_uq": w_uq, "w_ukv": w_ukv, "w_o_mla": w_o_mla, "w_o_fox": w_o_fox,
             "w_out": w_out, "w_up": w_up, "conv_w": conv_w, "w_down": w_down}
    big_m = {"w_in": m_w_in, "w_uq": m_w_uq, "w_ukv": m_w_ukv, "w_o_mla": m_w_o_mla, "w_o_fox": m_w_o_fox,
             "w_out": m_w_out, "w_up": m_w_up, "conv_w": m_conv_w, "w_down": m_w_down}
    big_v = {"w_in": v_w_in, "w_uq": v_w_uq, "w_ukv": v_w_ukv, "w_o_mla": v_w_o_mla, "w_o_fox": v_w_o_fox,
             "w_out": v_w_out, "w_up": v_w_up, "conv_w": v_conv_w, "w_down": v_w_down}

    gathered = _all_gather([big_w[k][0] if k == "conv_w" else big_w[k][0].astype(BF16) for k in BIG], "gather_weights")
    st = dict(zip(BIG, gathered))
    w_in_p = _w_in_padded(_cols_to_full(st["w_in"]))
    uq = st["w_uq"]
    w_uq_p = jnp.pad(uq, ((0, 0), (0, 0), (0, HEAD_PAD - 96))).transpose(1, 0, 2).reshape(MLA_Q_RANK, 1024)
    ukv = st["w_ukv"]
    zeros64 = jnp.zeros((N_HEADS, MLA_KV_RANK, 64), BF16)
    w_uk_p = jnp.concatenate([ukv[:, :, :64], zeros64], axis=2).transpose(1, 0, 2).reshape(MLA_KV_RANK, 1024)
    w_uv_p = jnp.concatenate([ukv[:, :, 64:], zeros64], axis=2).transpose(1, 0, 2).reshape(MLA_KV_RANK, 1024)
    place = np.zeros((HEAD_PAD, N_HEADS, HEAD_PAD), np.float32)
    for j in range(MLA_ROPE):
        place[j, :, MLA_NOPE + j] = 1.0
    place = jnp.asarray(place.reshape(HEAD_PAD, 1024), BF16)
    w_kv_comb = jnp.concatenate([
        jnp.concatenate([w_uk_p, w_uv_p], axis=1),
        jnp.concatenate([place, jnp.zeros((HEAD_PAD, 1024), BF16)], axis=1)], axis=0)
    pad_o = lambda full: jnp.pad(full.reshape(N_HEADS, 64, 1024), ((0, 0), (0, 64), (0, 0))).reshape(1024, 1024)
    w_o_mla_p = pad_o(_cols_to_full(st["w_o_mla"]))
    w_o_fox_p = pad_o(_cols_to_full(st["w_o_fox"]))
    w_out_f = st["w_out"].reshape(1024, 1024)
    w_up_f = _cols_to_full(st["w_up"])
    conv_w_f = _cols_to_full(st["conv_w"])
    w_down_f = st["w_down"].reshape(D_FF, 1024)

    (c_all,) = _all_gather([c], "gather_c")
    b_ada_mine = lax.dynamic_slice(b_ada, (0, me * 768), (1, 768))
    ada_cols, act_col = _ada_fwd(c_all.reshape(N_DEV, D_MODEL, 1), w_ada[0], b_ada_mine, "ada_fwd")
    (ada_all,) = _all_gather([ada_cols], "gather_ada")
    ada = lax.dynamic_slice(ada_all, (0, me, 0), (N_DEV, 1, 768)).reshape(1, N_ADA * D_MODEL)
    sh_m, sc_m, g_m, sh_f, sc_f, g_f = [ada[:, i * D_MODEL:(i + 1) * D_MODEL] for i in range(N_ADA)]

    inv_freq = ROPE_THETA ** (-jnp.arange(0, MLA_ROPE, 2, dtype=F32) / MLA_ROPE)
    ang = positions[0].astype(F32)[:, None] * inv_freq
    cos, sin = jnp.cos(ang), jnp.sin(ang)
    rope_c = jnp.concatenate([jnp.ones((s, 64), F32), cos, cos, jnp.zeros((s, 32), F32)], axis=1)
    rope_s = jnp.concatenate([jnp.zeros((s, 64), F32), -sin, sin, jnp.zeros((s, 32), F32)], axis=1)

    zero_d = jnp.zeros((1, D_MODEL), F32)

    h1 = _rms_mod(x, norm_mix_g, sc_m, sh_m, "norm_mix")
    proj = _mm(h1, w_in_p, "nn", F32, "proj_in", tn=640)
    cq = proj[:, P_CQ:P_CQ + 384]
    ckv = proj[:, P_CKV:P_CKV + 256]
    qn = _rms_mod(cq, q_norm_g, jnp.zeros((1, 384), F32), jnp.zeros((1, 384), F32), "q_norm")
    kvn = _rms_mod(ckv, kv_norm_g, jnp.zeros((1, 256), F32), jnp.zeros((1, 256), F32), "kv_norm")
    kv_in = jnp.concatenate([kvn, proj[:, P_KR:P_KR + 128].astype(BF16)], axis=1)
    q_pre = _mm(qn, w_uq_p, "nn", F32, "q_up")
    kv_pre = _mm(kv_in, w_kv_comb, "nn", F32, "kv_up")
    q_fold = MLA_SCALE * LOG2E
    (q_att,) = _rope(q_pre, rope_c * q_fold, rope_s * q_fold, "rope_q", N_HEADS, (BF16,))
    k_att, v_att = _rope(kv_pre, rope_c, rope_s, "rope_kv", N_HEADS, (BF16, BF16))
    o_mla, o_mla_b, lse_mla = _attn_fwd(q_att, k_att, v_att, None, None, "mla_fwd")

    fq = _pad_heads(proj[:, P_FQ:P_FQ + 512] * (FOX_SCALE * LOG2E), 64).astype(BF16)
    fk = _pad_heads(proj[:, P_FK:P_FK + 512], 64).astype(BF16)
    fv = _pad_heads(proj[:, P_FV:P_FV + 512], 64, ones_lane=True).astype(BF16)
    z = proj[:, P_FL:P_FL + 8].T.reshape(N_HEADS, s // SEQ_LANES, SEQ_LANES)
    bias_f = jnp.broadcast_to(b_forget.reshape(N_HEADS, 1, 1), (N_HEADS, 1, SEQ_LANES))
    f_cum = _forget_fwd(z, bias_f, "forget_fwd")
    f_col = (f_cum * LOG2E).reshape(N_HEADS, s, 1)
    f_row = f_col.reshape(N_HEADS, nblk, 1, ATT_T)
    o_fox, o_fox_b, lse_fox = _attn_fwd(fq, fk, fv, f_col, f_row, "fox_fwd")

    pm = _mm(o_mla_b, w_o_mla_p, "nn", F32, "o_mla_proj")
    pf = _mm(o_fox_b, w_o_fox_p, "nn", F32, "o_fox_proj")
    y = _gate_fwd(pm, pf, proj, "gate_fwd")
    x2, mix = _mm(y, w_out_f, "nn", F32, "out_proj", res=x, gvec=g_m)

    h2 = _rms_mod(x2, norm_ffn_g, sc_f, sh_f, "norm_ffn")
    u = _mm(h2, w_up_f, "nn", BF16, "ffn_up")
    a = _convglu_fwd(u, conv_w_f, conv_b, "convglu_fwd")
    x3, ffn = _mm(a, w_down_f, "nn", F32, "ffn_down", res=x2, gvec=g_f, tk=2816)

    dx3, sums_final = _final_loss(x3, target, norm_final_g.reshape(1, D_MODEL), "final_loss")
    dffn, sums_gf = _scale_bwd(dx3, ffn, g_f, "ffn_scale_bwd")
    da = _mm(dffn, w_down_f, "nt", BF16, "ffn_down_dx", tn=1408)
    g_w_down = _mm(a, dffn, "tn", F32, "ffn_down_dw", tm=256, tn=1024, tk=s)
    dgate, dval, s_gate, s_val = _convglu_bwd(da, u, conv_w_f, conv_b, "convglu_bwd")
    du = jnp.concatenate([_conv_transpose(dgate, conv_w_f[:, :D_FF], "conv_t_gate"),
                          _conv_transpose(dval, conv_w_f[:, D_FF:], "conv_t_val")], axis=1)
    dh2 = _mm(du, w_up_f, "nt", F32, "ffn_up_dx", tn=512, tk=2 * D_FF)
    g_w_up = _mm(h2, du, "tn", F32, "ffn_up_dw", tn=256, tk=s)
    dx2, sums_ffn = _rms_mod_bwd(dh2, x2, norm_ffn_g, sc_f, dx3, "norm_ffn_bwd")

    dmix, sums_gm = _scale_bwd(dx2, mix, g_m, "mix_scale_bwd")
    dy = _mm(dmix, w_out_f, "nt", F32, "out_proj_dx")
    g_w_out = _mm(y, dmix, "tn", F32, "out_proj_dw", tn=256, tk=s)
    dpm, dpf, dgm, dgf = _gate_bwd(dy, pm, pf, proj, "gate_bwd")
    do_mla_b = _mm(dpm, w_o_mla_p, "nt", BF16, "o_mla_dx", tn=1024)
    do_fox_b = _mm(dpf, w_o_fox_p, "nt", BF16, "o_fox_dx", tn=1024)
    g_w_o_mla_p = _mm(o_mla_b, dpm, "tn", F32, "o_mla_dw", tn=256, tk=s)
    g_w_o_fox_p = _mm(o_fox_b, dpf, "tn", F32, "o_fox_dw", tn=256, tk=s)

    delta_mla = _attn_delta(o_mla, do_mla_b, "mla_delta")
    dq_rot, dk_rot, dv_mla = _attn_bwd(q_att, k_att, v_att, do_mla_b, lse_mla, delta_mla, None, None,
                                       MLA_SCALE, 1.0 / LOG2E, "mla_bwd", BF16)
    (dq_pre,) = _rope(dq_rot, rope_c, -rope_s, "rope_q_bwd", N_HEADS, (BF16,))
    dkv_pre = _rope_bwd_kv(dk_rot, dv_mla, rope_c, -rope_s, "rope_kv_bwd")
    dqn = _mm(dq_pre, w_uq_p, "nt", F32, "q_up_dx")
    g_w_uq_p = _mm(qn, dq_pre, "tn", F32, "q_up_dw", tk=s)
    dkv_in = _mm(dkv_pre, w_kv_comb, "nt", F32, "kv_up_dx")
    g_w_kv_comb = _mm(kv_in, dkv_pre, "tn", F32, "kv_up_dw", tk=s)
    dcq, sums_q = _rms_mod_bwd(dqn, cq, q_norm_g, jnp.zeros((1, 384), F32), None, "q_norm_bwd")
    dckv, sums_kv = _rms_mod_bwd(dkv_in[:, :256], ckv, kv_norm_g, jnp.zeros((1, 256), F32), None, "kv_norm_bwd")
    delta_fox = _attn_delta(o_fox, do_fox_b, "fox_delta")
    dfq, dfk, dfv, dfr, dfc = _attn_bwd(fq, fk, fv, do_fox_b, lse_fox, delta_fox, f_col, f_row,
                                        FOX_SCALE, 1.0 / LOG2E, "fox_bwd", BF16)
    df_rows = dfr.reshape(N_HEADS, s // SEQ_LANES, SEQ_LANES)
    df_cols = dfc.reshape(N_HEADS, s // SEQ_LANES, SEQ_LANES)
    dz, db_f = _forget_bwd(df_rows, df_cols, z, bias_f, "forget_bwd")
    dfl = jnp.pad(dz.reshape(N_HEADS, s).T, ((0, 0), (0, 128 - N_HEADS)))

    dproj = jnp.concatenate([
        dgm, dgf, _unpad_heads(dfq, 64), _unpad_heads(dfk, 64), _unpad_heads(dfv, 64),
        dcq.astype(BF16), dckv.astype(BF16), dkv_in[:, 256:384].astype(BF16), dfl.astype(BF16)], axis=1)
    dh1 = _mm(dproj, w_in_p, "nt", F32, "proj_in_dx", tn=512, tk=D_IN_P)
    g_w_in_p = _mm(h1, dproj, "tn", F32, "proj_in_dw", tm=512, tn=640, tk=s)
    grad_x, sums_mix = _rms_mod_bwd(dh1, x, norm_mix_g, sc_m, dx2, "norm_mix_bwd")

    g_w_in = _w_in_unpadded(g_w_in_p)
    g_uq = g_w_uq_p.reshape(MLA_Q_RANK, N_HEADS, HEAD_PAD)[:, :, :96].transpose(1, 0, 2)
    g_uk = g_w_kv_comb[:256, :1024].reshape(256, N_HEADS, HEAD_PAD)[:, :, :64]
    g_uv = g_w_kv_comb[:256, 1024:].reshape(256, N_HEADS, HEAD_PAD)[:, :, :64]
    g_ukv = jnp.concatenate([g_uk, g_uv], axis=2).transpose(1, 0, 2)
    unpad_o = lambda g: g.reshape(N_HEADS, HEAD_PAD, 1024)[:, :64].reshape(512, 1024)
    g_conv_w = jnp.concatenate([s_gate[0:3], s_val[0:3]], axis=1)
    g_blocks = {
        "w_in": _full_to_cols(g_w_in, 533), "w_uq": g_uq, "w_ukv": g_ukv,
        "w_o_mla": _full_to_cols(unpad_o(g_w_o_mla_p), 128), "w_o_fox": _full_to_cols(unpad_o(g_w_o_fox_p), 128),
        "w_out": g_w_out.reshape(N_DEV, 128, 1024), "w_up": _full_to_cols(g_w_up, 704),
        "conv_w": _full_to_cols(g_conv_w, 704), "w_down": g_w_down.reshape(N_DEV, 352, 1024)}
    g_recv = _all_to_all([g_blocks[k].astype(BF16) for k in BIG], "scatter_grads")
    g_big, d_big, nm_big, nv_big = {}, {}, {}, {}
    for k, parts in zip(BIG, g_recv):
        g_big[k], d_big[k], nm_big[k], nv_big[k] = [
            t[None] for t in _adamw(parts, big_w[k][0], big_m[k][0], big_v[k][0], "adamw_" + k)]

    dada = jnp.concatenate([sums_mix[0:1], sums_mix[1:2], sums_gm[0:1], sums_ffn[0:1], sums_ffn[1:2], sums_gf[0:1]],
                           axis=1)
    small_part = _pack_small({
        "b_ada": dada, "norm_mix_g": sums_mix[2:3], "q_norm_g": sums_q[2:3], "kv_norm_g": sums_kv[2:3],
        "b_forget": db_f[:, 0, 0], "norm_ffn_g": sums_ffn[2:3],
        "conv_b": jnp.concatenate([s_gate[3:4], s_val[3:4]], axis=1), "norm_final_g": sums_final[0:1],
        "loss": sums_final[1:2, 0:1]})
    (small_all,) = _all_gather([small_part], "gather_small")
    zero1 = jnp.zeros((1,), F32)
    small_w = {"b_ada": b_ada, "norm_mix_g": norm_mix_g, "q_norm_g": q_norm_g, "kv_norm_g": kv_norm_g,
               "b_forget": b_forget, "norm_ffn_g": norm_ffn_g, "conv_b": conv_b, "norm_final_g": norm_final_g,
               "loss": zero1}
    small_m = {"b_ada": m_b_ada, "norm_mix_g": m_norm_mix_g, "q_norm_g": m_q_norm_g, "kv_norm_g": m_kv_norm_g,
               "b_forget": m_b_forget, "norm_ffn_g": m_norm_ffn_g, "conv_b": m_conv_b,
               "norm_final_g": m_norm_final_g, "loss": zero1}
    small_v = {"b_ada": v_b_ada, "norm_mix_g": v_norm_mix_g, "q_norm_g": v_q_norm_g, "kv_norm_g": v_kv_norm_g,
               "b_forget": v_b_forget, "norm_ffn_g": v_norm_ffn_g, "conv_b": v_conv_b,
               "norm_final_g": v_norm_final_g, "loss": zero1}
    g_sm, d_sm, nm_sm, nv_sm = _adamw(small_all, _pack_small(small_w), _pack_small(small_m), _pack_small(small_v),
                                      "adamw_small")
    loss = g_sm[0, SMALL_OFF["loss"]]

    dada_all = small_all[:, 0, SMALL_OFF["b_ada"]:SMALL_OFF["b_ada"] + N_ADA * D_MODEL]
    dada_mine = lax.dynamic_slice(dada_all, (0, me * 768), (N_DEV, 768))
    g_ada_local = _ada_bwd(act_col, dada_mine, "ada_bwd")
    g_ada, d_ada, nm_ada, nv_ada = _adamw(g_ada_local[None], w_ada[0], m_w_ada[0], v_w_ada[0], "adamw_ada")

    def small_out(t, nme, shape):
        real = dict((n_, r_) for n_, r_, _ in SMALL)[nme]
        o = SMALL_OFF[nme]
        return t[0, o:o + real].reshape(shape)

    order = ["w_ada", "b_ada", "norm_mix_g", "w_in", "q_norm_g", "w_uq", "kv_norm_g", "w_ukv", "b_forget",
             "w_o_mla", "w_o_fox", "w_out", "norm_ffn_g", "w_up", "conv_w", "conv_b", "w_down", "norm_final_g"]
    small_shapes = {"b_ada": (1, 6144), "norm_mix_g": (1, 1024), "q_norm_g": (1, 384), "kv_norm_g": (1, 256),
                    "b_forget": (1, 8), "norm_ffn_g": (1, 1024), "conv_b": (1, 5632), "norm_final_g": (1024,)}

    def family(big, small, ada_t):
        out = []
        for nme in order:
            if nme == "w_ada":
                out.append(ada_t[None])
            elif nme in small_shapes:
                out.append(small_out(small, nme, small_shapes[nme]))
            else:
                out.append(big[nme])
        return out

    return (loss, grad_x[None], *family(g_big, g_sm, g_ada), *family(d_big, d_sm, d_ada),
            *family(nm_big, nm_sm, nm_ada), *family(nv_big, nv_sm, nv_ada))
```

```python
import math

import numpy as np
import jax
import jax.numpy as jnp
from jax import lax
from jax.experimental import pallas as pl
from jax.experimental.pallas import tpu as pltpu

F32 = jnp.float32
BF16 = jnp.bfloat16

N_DEV = 8
D_MODEL = 1024
N_HEADS = 8
HEAD_PAD = 128
MLA_Q_RANK = 384
MLA_KV_RANK = 256
MLA_NOPE = 64
MLA_ROPE = 32
MLA_V = 64
FOX_DIM = 64
D_FF = 2816
N_ADA = 6
EPS = 1e-6
ROPE_THETA = 10000.0
MLA_SCALE = 1.0 / math.sqrt(MLA_NOPE + MLA_ROPE)
FOX_SCALE = 1.0 / math.sqrt(FOX_DIM)
IN_SPLITS = (384, 256, 32, 512, 512, 512, 8, 1024, 1024)
D_IN = sum(IN_SPLITS)
IN_OFF = tuple(int(v) for v in np.cumsum((0,) + IN_SPLITS))
P_GM, P_GF, P_FQ, P_FK, P_FV, P_CQ, P_CKV, P_KR, P_FL, D_IN_P = 0, 1024, 2048, 2560, 3072, 3584, 3968, 4224, 4352, 4480

ADAM_LR, ADAM_B1, ADAM_B2, ADAM_EPS, ADAM_WD, ADAM_STEP = 0.001, 0.9, 0.999, 1e-08, 0.01, 10

VMEM_LIMIT_BYTES = 56 * 1024 * 1024
NEG_BIG = -1e30
ATT_T = 512
LOG2E = 1.4426950408889634
SUM_LANE = 64
ROW_T = 256
SEQ_LANES = 128
BF16_ROWS = 16
FWD_HEADS_PER_STEP = 2
BWD_HEADS_PER_STEP = 2


def _params(sem):
    return pltpu.CompilerParams(dimension_semantics=sem, vmem_limit_bytes=VMEM_LIMIT_BYTES)


def _tile(n, target, step=128):
    if n <= target:
        return n
    t = (target // step) * step
    while t >= step:
        if n % t == 0:
            return t
        t -= step
    return n


def _vec_spec(w, nargs):
    if nargs == 1:
        return pl.BlockSpec((1, w), lambda i: (0, 0))
    return pl.BlockSpec((1, w), lambda i, j: (0, 0))


def _comm_call(body, name, ins, out_shapes):
    n = len(ins)
    any_spec = pl.BlockSpec(memory_space=pl.ANY)
    return pl.pallas_call(
        body, name=name, out_shape=tuple(out_shapes),
        in_specs=[any_spec] * n, out_specs=tuple([any_spec] * n),
        scratch_shapes=[pltpu.SemaphoreType.DMA((n, 7)), pltpu.SemaphoreType.DMA((n, 7)),
                        pltpu.SemaphoreType.DMA((n,))],
    )(*ins)


def _all_gather(xs, name):
    n = len(xs)

    def body(*refs):
        x_refs, out_refs = refs[:n], refs[n:2 * n]
        send_sems, recv_sems, local_sems = refs[2 * n:]
        x_, y_, c_ = lax.axis_index("x"), lax.axis_index("y"), lax.axis_index("c")
        me, sibling = (x_, y_, c_), (x_, y_, 1 - c_)
        chips = [(1 - x_, y_), (x_, 1 - y_), (1 - x_, 1 - y_)]

        def slot(a, px, py, pc):
            return out_refs[a].at[4 * px + 2 * py + pc]

        def copy(a, k, block, to, src=None):
            return pltpu.make_async_remote_copy(
                src_ref=slot(a, *block) if src is None else src, dst_ref=slot(a, *block),
                send_sem=send_sems.at[a, k], recv_sem=recv_sems.at[a, k],
                device_id=to, device_id_type=pl.DeviceIdType.MESH)

        mine = [pltpu.make_async_copy(x_refs[a], slot(a, *me), local_sems.at[a]) for a in range(n)]
        for cp in mine:
            cp.start()
        first = []
        for a in range(n):
            first.append(copy(a, 0, me, sibling, src=x_refs[a]))
            first += [copy(a, 1 + j, me, (*chip, c_), src=x_refs[a]) for j, chip in enumerate(chips)]
        for cp in first:
            cp.start()
        passed = []
        for j, chip in enumerate(chips):
            for a in range(n):
                copy(a, 1 + j, (*chip, c_), me).wait_recv()
                passed.append(copy(a, 4 + j, (*chip, c_), sibling))
                passed[-1].start()
        for a in range(n):
            copy(a, 0, sibling, me).wait_recv()
            for j, chip in enumerate(chips):
                copy(a, 4 + j, (*chip, 1 - c_), me).wait_recv()
        for cp in first + passed:
            cp.wait_send()
        for cp in mine:
            cp.wait()

    return _comm_call(body, name, xs, [jax.ShapeDtypeStruct((N_DEV,) + x.shape, x.dtype) for x in xs])


def _direct_exchange(src_refs, out_refs, send_sems, recv_sems, local_sems, scatter):
    n = len(src_refs)
    x_, y_, c_ = lax.axis_index("x"), lax.axis_index("y"), lax.axis_index("c")
    me = 4 * x_ + 2 * y_ + c_

    def peer(k):
        return (x_ ^ ((k >> 2) & 1), y_ ^ ((k >> 1) & 1), c_ ^ (k & 1))

    def src(a, slot):
        return src_refs[a].at[slot] if scatter else src_refs[a]

    def copy(a, k, sending):
        px, py, pc = peer(k)
        theirs = 4 * px + 2 * py + pc
        return pltpu.make_async_remote_copy(
            src_ref=src(a, theirs if sending else me), dst_ref=out_refs[a].at[me if sending else theirs],
            send_sem=send_sems.at[a, k - 1], recv_sem=recv_sems.at[a, k - 1],
            device_id=(px, py, pc), device_id_type=pl.DeviceIdType.MESH)

    mine = [pltpu.make_async_copy(src(a, me), out_refs[a].at[me], local_sems.at[a]) for a in range(n)]
    sends = [copy(a, k, True) for a in range(n) for k in range(1, N_DEV)]

    def start():
        for cp in mine + sends:
            cp.start()

    def wait():
        for a in range(n):
            for k in range(1, N_DEV):
                copy(a, k, False).wait_recv()
        for cp in sends:
            cp.wait_send()
        for cp in mine:
            cp.wait()

    return start, wait


def _exchange_scratch(n):
    return [pltpu.SemaphoreType.DMA((n, 7)), pltpu.SemaphoreType.DMA((n, 7)), pltpu.SemaphoreType.DMA((n,))]


def _all_to_all(gs, name):
    n = len(gs)

    def body(*refs):
        start, wait = _direct_exchange(refs[:n], refs[n:2 * n], *refs[2 * n:], scatter=True)
        start()
        wait()

    return _comm_call(body, name, gs, [jax.ShapeDtypeStruct(g.shape, g.dtype) for g in gs])


def _mm(a, b, mode, out_dtype, name, res=None, gvec=None, tm=1024, tn=512, tk=1024):
    (k, m) = a.shape if mode == "tn" else a.shape[::-1]
    n = b.shape[0] if mode == "nt" else b.shape[1]
    tm, tn, tk = _tile(m, tm), _tile(n, tn), _tile(k, tk)
    nk = k // tk
    dims = {"nn": (((1,), (0,)), ((), ())), "nt": (((1,), (1,)), ((), ())), "tn": (((0,), (0,)), ((), ()))}[mode]
    fused = res is not None

    def body(*refs):
        acc_ref = refs[-1] if nk > 1 else None
        if fused:
            a_ref, b_ref, res_ref, g_ref, o_ref, raw_ref = refs[:6]
        else:
            a_ref, b_ref, o_ref = refs[:3]
        part = lax.dot_general(a_ref[...], b_ref[...], dims, preferred_element_type=F32)

        def finish(acc):
            if fused:
                raw_ref[...] = acc
                o_ref[...] = (res_ref[...] + g_ref[...] * acc).astype(o_ref.dtype)
            else:
                o_ref[...] = acc.astype(o_ref.dtype)

        if nk == 1:
            finish(part)
            return
        kk = pl.program_id(2)

        @pl.when(kk == 0)
        def _():
            acc_ref[...] = part

        @pl.when(kk > 0)
        def _():
            acc_ref[...] += part

        @pl.when(kk == nk - 1)
        def _():
            finish(acc_ref[...])

    if mode == "tn":
        a_spec = pl.BlockSpec((tk, tm), lambda i, j, kk: (kk, i))
    else:
        a_spec = pl.BlockSpec((tm, tk), lambda i, j, kk: (i, kk))
    if mode == "nt":
        b_spec = pl.BlockSpec((tn, tk), lambda i, j, kk: (j, kk))
    else:
        b_spec = pl.BlockSpec((tk, tn), lambda i, j, kk: (kk, j))
    o_spec = pl.BlockSpec((tm, tn), lambda i, j, kk: (i, j))
    in_specs, args = [a_spec, b_spec], [a, b]
    out_specs, out_shape = o_spec, jax.ShapeDtypeStruct((m, n), out_dtype)
    if fused:
        in_specs += [o_spec, pl.BlockSpec((1, tn), lambda i, j, kk: (0, j))]
        args += [res, gvec]
        out_specs = (o_spec, o_spec)
        out_shape = (out_shape, jax.ShapeDtypeStruct((m, n), F32))
    return pl.pallas_call(
        body, name=name, grid=(m // tm, n // tn, nk),
        in_specs=in_specs, out_specs=out_specs, out_shape=out_shape,
        scratch_shapes=[pltpu.VMEM((tm, tn), F32)] if nk > 1 else [],
        compiler_params=_params(("parallel", "parallel", "arbitrary")),
    )(*args)


def _rms_mod(x, g, sc, sh, name):
    s, w = x.shape
    tm = _tile(s, ROW_T)

    def body(x_ref, g_ref, sc_ref, sh_ref, o_ref):
        xv = x_ref[...]
        r = lax.rsqrt(jnp.mean(xv * xv, axis=-1, keepdims=True) + EPS)
        o_ref[...] = ((xv * r * g_ref[...]) * (1.0 + sc_ref[...]) + sh_ref[...]).astype(o_ref.dtype)

    row = pl.BlockSpec((tm, w), lambda i: (i, 0))
    return pl.pallas_call(
        body, name=name, grid=(s // tm,),
        in_specs=[row, _vec_spec(w, 1), _vec_spec(w, 1), _vec_spec(w, 1)],
        out_specs=row, out_shape=jax.ShapeDtypeStruct((s, w), BF16),
        compiler_params=_params(("parallel",)),
    )(x, g, sc, sh)


def _rms_mod_bwd(dh, x, g, sc, dres, name):
    s, w = x.shape
    tm = _tile(s, ROW_T)
    has_res = dres is not None

    def body(*refs):
        if has_res:
            dh_ref, x_ref, g_ref, sc_ref, dres_ref, dx_ref, sums_ref = refs
        else:
            dh_ref, x_ref, g_ref, sc_ref, dx_ref, sums_ref = refs
        xv, dhv, gv = x_ref[...], dh_ref[...], g_ref[...]
        r = lax.rsqrt(jnp.mean(xv * xv, axis=-1, keepdims=True) + EPS)
        xhat = xv * r
        dxn = dhv * (1.0 + sc_ref[...])
        dxhat = dxn * gv
        dx = r * (dxhat - xhat * jnp.mean(dxhat * xhat, axis=-1, keepdims=True))
        if has_res:
            dx = dx + dres_ref[...]
        dx_ref[...] = dx

        @pl.when(pl.program_id(0) == 0)
        def _():
            sums_ref[...] = jnp.zeros_like(sums_ref)

        sums_ref[0:1, :] += jnp.sum(dhv, axis=0, keepdims=True)
        sums_ref[1:2, :] += jnp.sum(dhv * (xhat * gv), axis=0, keepdims=True)
        sums_ref[2:3, :] += jnp.sum(dxn * xhat, axis=0, keepdims=True)

    row = pl.BlockSpec((tm, w), lambda i: (i, 0))
    in_specs = [row, row, _vec_spec(w, 1), _vec_spec(w, 1)] + ([row] if has_res else [])
    args = [dh, x, g, sc] + ([dres] if has_res else [])
    return pl.pallas_call(
        body, name=name, grid=(s // tm,),
        in_specs=in_specs,
        out_specs=(row, pl.BlockSpec((8, w), lambda i: (0, 0))),
        out_shape=(jax.ShapeDtypeStruct((s, w), F32), jax.ShapeDtypeStruct((8, w), F32)),
        compiler_params=_params(("arbitrary",)),
    )(*args)


def _scale_bwd(dx, val, gvec, name):
    s, w = dx.shape
    tm = _tile(s, ROW_T)

    def body(dx_ref, val_ref, g_ref, d_ref, sums_ref):
        dxv = dx_ref[...]
        d_ref[...] = (dxv * g_ref[...]).astype(d_ref.dtype)

        @pl.when(pl.program_id(0) == 0)
        def _():
            sums_ref[...] = jnp.zeros_like(sums_ref)

        sums_ref[0:1, :] += jnp.sum(dxv * val_ref[...], axis=0, keepdims=True)

    row = pl.BlockSpec((tm, w), lambda i: (i, 0))
    return pl.pallas_call(
        body, name=name, grid=(s // tm,),
        in_specs=[row, row, _vec_spec(w, 1)],
        out_specs=(row, pl.BlockSpec((8, w), lambda i: (0, 0))),
        out_shape=(jax.ShapeDtypeStruct((s, w), BF16), jax.ShapeDtypeStruct((8, w), F32)),
        compiler_params=_params(("arbitrary",)),
    )(dx, val, gvec)


def _final_loss(x3, target, g, name):
    s, w = x3.shape
    tm = _tile(s, ROW_T)

    def body(x_ref, t_ref, g_ref, dx_ref, sums_ref):
        xv, gv = x_ref[...], g_ref[...]
        r = lax.rsqrt(jnp.mean(xv * xv, axis=-1, keepdims=True) + EPS)
        xhat = xv * r
        err = xhat * gv - t_ref[...]
        dy = err * (1.0 / w)
        dxhat = dy * gv
        dx_ref[...] = r * (dxhat - xhat * jnp.mean(dxhat * xhat, axis=-1, keepdims=True))

        @pl.when(pl.program_id(0) == 0)
        def _():
            sums_ref[...] = jnp.zeros_like(sums_ref)

        sums_ref[0:1, :] += jnp.sum(dy * xhat, axis=0, keepdims=True)
        sums_ref[1:2, :] += jnp.zeros((1, w), F32) + (0.5 / w) * jnp.sum(err * err)

    row = pl.BlockSpec((tm, w), lambda i: (i, 0))
    return pl.pallas_call(
        body, name=name, grid=(s // tm,),
        in_specs=[row, row, _vec_spec(w, 1)],
        out_specs=(row, pl.BlockSpec((8, w), lambda i: (0, 0))),
        out_shape=(jax.ShapeDtypeStruct((s, w), F32), jax.ShapeDtypeStruct((8, w), F32)),
        compiler_params=_params(("arbitrary",)),
    )(x3, target, g)


def _rope_block(seg, cmul, smul):
    lane = lax.broadcasted_iota(jnp.int32, seg.shape, 1)
    swapped = jnp.where(lane < MLA_NOPE + MLA_ROPE // 2,
                        pltpu.roll(seg, HEAD_PAD - MLA_ROPE // 2, 1), pltpu.roll(seg, MLA_ROPE // 2, 1))
    return seg * cmul + swapped * smul


def _rope(t, cmul, smul, name, n_rot, out_dtypes):
    s, w = t.shape
    tm = _tile(s, ROW_T)
    n_out = len(out_dtypes)
    wo = w // n_out

    def body(t_ref, c_ref, s_ref, *o_refs):
        cv, sv = c_ref[...], s_ref[...]
        one = (lax.broadcasted_iota(jnp.int32, (tm, HEAD_PAD), 1) == SUM_LANE).astype(F32)
        for hb in range(w // HEAD_PAD):
            seg = t_ref[:, hb * HEAD_PAD:(hb + 1) * HEAD_PAD].astype(F32)
            if hb < n_rot:
                seg = _rope_block(seg, cv, sv)
            else:
                seg = seg + one
            o_ref = o_refs[(hb * HEAD_PAD) // wo]
            col = (hb * HEAD_PAD) % wo
            o_ref[:, col:col + HEAD_PAD] = seg.astype(o_ref.dtype)

    row = pl.BlockSpec((tm, w), lambda i: (i, 0))
    tab = pl.BlockSpec((tm, HEAD_PAD), lambda i: (i, 0))
    orow = pl.BlockSpec((tm, wo), lambda i: (i, 0))
    outs = pl.pallas_call(
        body, name=name, grid=(s // tm,),
        in_specs=[row, tab, tab],
        out_specs=tuple(orow for _ in out_dtypes),
        out_shape=tuple(jax.ShapeDtypeStruct((s, wo), dt) for dt in out_dtypes),
        compiler_params=_params(("parallel",)),
    )(t, cmul, smul)
    return outs


def _rope_bwd_kv(dk, dv, cmul, smul, name):
    s, w = dk.shape
    tm = _tile(s, ROW_T)

    def body(dk_ref, dv_ref, c_ref, s_ref, o_ref):
        cv, sv = c_ref[...], s_ref[...]
        for hb in range(N_HEADS):
            lo, hi = hb * HEAD_PAD, (hb + 1) * HEAD_PAD
            o_ref[:, lo:hi] = _rope_block(dk_ref[:, lo:hi].astype(F32), cv, sv).astype(o_ref.dtype)
        o_ref[:, w:2 * w] = dv_ref[...].astype(o_ref.dtype)

    row = pl.BlockSpec((tm, w), lambda i: (i, 0))
    tab = pl.BlockSpec((tm, HEAD_PAD), lambda i: (i, 0))
    return pl.pallas_call(
        body, name=name, grid=(s // tm,),
        in_specs=[row, row, tab, tab],
        out_specs=pl.BlockSpec((tm, 2 * w), lambda i: (i, 0)),
        out_shape=jax.ShapeDtypeStruct((s, 2 * w), BF16),
        compiler_params=_params(("parallel",)),
    )(dk, dv, cmul, smul)


def _lanes(col, width):
    if col.shape[1] == 1:
        col = jnp.broadcast_to(col, (col.shape[0], HEAD_PAD))
    return jnp.tile(col, (1, width // HEAD_PAD))


def _fold_lanes(a):
    out = a[:, 0:HEAD_PAD]
    for g in range(1, a.shape[1] // HEAD_PAD):
        out = out + a[:, g * HEAD_PAD:(g + 1) * HEAD_PAD]
    return out


def _as_row(rep):
    return rep.T[0:1, :]


def _causal(t, rows_are_queries):
    row = lax.broadcasted_iota(jnp.int32, (t, t), 0)
    col = lax.broadcasted_iota(jnp.int32, (t, t), 1)
    return row >= col if rows_are_queries else col >= row


def _split_refs(refs, n_in, n_out, n_scratch, n_x):
    pos = [n_in, n_x, n_out, n_x, n_scratch, 3 if n_x else 0]
    out, at = [], 0
    for cnt in pos:
        out.append(refs[at:at + cnt])
        at += cnt
    return out


def _first_last_step(n0, n1):
    i0, i1 = pl.program_id(0), pl.program_id(1)
    return jnp.logical_and(i0 == 0, i1 == 0), jnp.logical_and(i0 == n0 - 1, i1 == n1 - 1)


def _attn_fwd(q, k, v, fcol, frow, name, gather=()):
    s = q.shape[0]
    t = ATT_T
    nq = s // t
    use_f = fcol is not None
    nx = len(gather)

    hpb = FWD_HEADS_PER_STEP

    def body(*refs):
        ins, x_src, outs, x_out, scr, x_sems = _split_refs(refs, 5 if use_f else 3, 3, 2, nx)
        if use_f:
            q_ref, k_ref, v_ref, fc_ref, fr_ref = ins
            fc_b = [jnp.broadcast_to(fc_ref[hh], (t, HEAD_PAD)) for hh in range(hpb)]
        else:
            q_ref, k_ref, v_ref = ins
        o_ref, ob_ref, lse_ref = outs
        m_s, acc_s = scr
        if nx:
            first, last = _first_last_step(N_HEADS // hpb, nq)
            x_start, x_wait = _direct_exchange(x_src, x_out, *x_sems, scatter=False)
            pl.when(first)(x_start)
        qi = pl.program_id(1)
        m_s[...] = jnp.full(m_s.shape, NEG_BIG, F32)
        acc_s[...] = jnp.zeros(acc_s.shape, F32)

        def step(j, masked):
            off = pl.multiple_of(j * t, t)
            for hh in range(hpb):
                lanes = slice(hh * HEAD_PAD, (hh + 1) * HEAD_PAD)
                kv = k_ref[pl.ds(off, t), lanes]
                vv = v_ref[pl.ds(off, t), lanes]
                sc = lax.dot_general(q_ref[:, lanes], kv, (((1,), (1,)), ((), ())), preferred_element_type=F32)
                if use_f:
                    sc = sc + (_lanes(fc_b[hh], t) - fr_ref[hh, j])
                if masked:
                    sc = jnp.where(_causal(t, True), sc, NEG_BIG)
                m_prev = m_s[hh]
                m_new = jnp.maximum(m_prev, jnp.max(sc, axis=-1, keepdims=True))
                p = jnp.exp2(sc - _lanes(m_new, t))
                acc_s[hh] = jnp.exp2(m_prev - m_new) * acc_s[hh] + jnp.dot(p.astype(BF16), vv,
                                                                           preferred_element_type=F32)
                m_s[hh] = m_new

        def loop_body(j, carry):
            step(j, False)
            return carry

        lax.fori_loop(0, qi, loop_body, 0)
        step(qi, True)
        for hh in range(hpb):
            lanes = slice(hh * HEAD_PAD, (hh + 1) * HEAD_PAD)
            acc = acc_s[hh]
            lane = lax.broadcasted_iota(jnp.int32, acc.shape, 1)
            denom = jnp.sum(jnp.where(lane == SUM_LANE, acc, 0.0), axis=-1, keepdims=True)
            o = acc * (1.0 / denom)
            o_ref[:, lanes] = o
            ob_ref[:, lanes] = o.astype(BF16)
            lse_ref[hh, 0] = _as_row(m_s[hh] + jnp.log(denom) * LOG2E)
        if nx:
            pl.when(last)(x_wait)

    w = hpb * HEAD_PAD
    qspec = pl.BlockSpec((t, w), lambda h, i: (i, h))
    kspec = pl.BlockSpec((s, w), lambda h, i: (0, h))
    colspec = pl.BlockSpec((hpb, t, 1), lambda h, i: (h, i, 0))
    any_spec = pl.BlockSpec(memory_space=pl.ANY)
    in_specs, args = [qspec, kspec, kspec], [q, k, v]
    if use_f:
        in_specs += [colspec, pl.BlockSpec((hpb, nq, 1, t), lambda h, i: (h, 0, 0, 0))]
        args += [fcol, frow]
    out_specs = [qspec, qspec, pl.BlockSpec((hpb, 1, 1, t), lambda h, i: (h, i, 0, 0))]
    out_shape = [jax.ShapeDtypeStruct((s, N_HEADS * HEAD_PAD), F32), jax.ShapeDtypeStruct((s, N_HEADS * HEAD_PAD), BF16),
                 jax.ShapeDtypeStruct((N_HEADS, nq, 1, t), F32)]
    scratch = [pltpu.VMEM((hpb, t, HEAD_PAD), F32), pltpu.VMEM((hpb, t, HEAD_PAD), F32)]
    if nx:
        in_specs += [any_spec] * nx
        args += list(gather)
        out_specs += [any_spec] * nx
        out_shape += [jax.ShapeDtypeStruct((N_DEV,) + g.shape, g.dtype) for g in gather]
        scratch += _exchange_scratch(nx)
    return pl.pallas_call(
        body, name=name, grid=(N_HEADS // hpb, nq),
        in_specs=in_specs, out_specs=tuple(out_specs), out_shape=tuple(out_shape),
        scratch_shapes=scratch,
        compiler_params=_params(("arbitrary", "arbitrary") if nx else ("parallel", "arbitrary")),
    )(*args)


def _attn_delta(o, do, name):
    s, w = o.shape
    t = ATT_T

    def body(o_ref, do_ref, d_ref):
        for hb in range(N_HEADS):
            lo, hi = hb * HEAD_PAD, (hb + 1) * HEAD_PAD
            prod = o_ref[:, lo:hi] * do_ref[:, lo:hi].astype(F32)
            d_ref[hb, 0] = jnp.sum(prod.T, axis=0, keepdims=True)

    row = pl.BlockSpec((t, w), lambda i: (i, 0))
    return pl.pallas_call(
        body, name=name, grid=(s // t,),
        in_specs=[row, row],
        out_specs=pl.BlockSpec((N_HEADS, 1, 1, t), lambda i: (0, i, 0, 0)),
        out_shape=jax.ShapeDtypeStruct((N_HEADS, s // t, 1, t), F32),
        compiler_params=_params(("parallel",)),
    )(o, do)


def _attn_bwd(q, k, v, do, lse_row, delta_row, fcol, frow, scale_q, scale_k, name, out_dtype, scatter=()):
    s = q.shape[0]
    t = ATT_T
    nq = s // t
    use_f = fcol is not None
    nx = len(scatter)
    hpb = BWD_HEADS_PER_STEP

    def body(*refs):
        ins, x_src, outs, x_out, scr, x_sems = _split_refs(refs, 8 if use_f else 6, 5 if use_f else 3,
                                                           5 if use_f else 3, nx)
        if use_f:
            q_ref, k_ref, v_ref, do_ref, lse_ref, dl_ref, fc_ref, fr_ref = ins
            dq_ref, dk_ref, dv_ref, dr_ref, df_ref = outs
            dq_s, dk_s, dv_s, dr_s, df_s = scr
            fc_b = [jnp.broadcast_to(fc_ref[hh], (t, HEAD_PAD)) for hh in range(hpb)]
        else:
            q_ref, k_ref, v_ref, do_ref, lse_ref, dl_ref = ins
            dq_ref, dk_ref, dv_ref = outs
            dq_s, dk_s, dv_s = scr
        if nx:
            first, last = _first_last_step(N_HEADS // hpb, nq)
            x_start, x_wait = _direct_exchange(x_src, x_out, *x_sems, scatter=True)
            pl.when(first)(x_start)
        kj = pl.program_id(1)

        @pl.when(kj == 0)
        def _():
            dq_s[...] = jnp.zeros(dq_s.shape, F32)
            if use_f:
                dr_s[...] = jnp.zeros(dr_s.shape, F32)

        dk_s[...] = jnp.zeros(dk_s.shape, F32)
        dv_s[...] = jnp.zeros(dv_s.shape, F32)
        if use_f:
            df_s[...] = jnp.zeros(df_s.shape, F32)

        def step(i, masked):
            off = pl.multiple_of(i * t, t)
            for hh in range(hpb):
                lanes = slice(hh * HEAD_PAD, (hh + 1) * HEAD_PAD)
                kv, vv = k_ref[:, lanes], v_ref[:, lanes]
                qv = q_ref[pl.ds(off, t), lanes]
                dov = do_ref[pl.ds(off, t), lanes]
                st = lax.dot_general(kv, qv, (((1,), (1,)), ((), ())), preferred_element_type=F32)
                if use_f:
                    st = st + (fr_ref[hh, i] - _lanes(fc_b[hh], t))
                if masked:
                    st = jnp.where(_causal(t, False), st, NEG_BIG)
                pt = jnp.exp2(st - lse_ref[hh, i])
                dv_s[hh] += jnp.dot(pt.astype(BF16), dov, preferred_element_type=F32)
                dpt = lax.dot_general(vv, dov, (((1,), (1,)), ((), ())), preferred_element_type=F32)
                dst = pt * (dpt - dl_ref[hh, i])
                dsb = dst.astype(BF16)
                dk_s[hh] += jnp.dot(dsb, qv, preferred_element_type=F32)
                dq_s[hh, pl.ds(off, t), :] += lax.dot_general(dsb, kv, (((0,), (0,)), ((), ())),
                                                              preferred_element_type=F32)
                if use_f:
                    df_s[hh] -= _fold_lanes(dst)
                    dr_s[hh, i] += jnp.sum(dst, axis=0, keepdims=True)

        step(kj, True)

        def loop_body(i, carry):
            step(i, False)
            return carry

        lax.fori_loop(kj + 1, nq, loop_body, 0)
        for hh in range(hpb):
            lanes = slice(hh * HEAD_PAD, (hh + 1) * HEAD_PAD)
            dk_ref[:, lanes] = (dk_s[hh] * scale_k).astype(dk_ref.dtype)
            dv_ref[:, lanes] = dv_s[hh].astype(dv_ref.dtype)
            if use_f:
                df_ref[hh, 0] = jnp.sum(df_s[hh].T, axis=0, keepdims=True)

        @pl.when(kj == nq - 1)
        def _():
            for hh in range(hpb):
                dq_ref[:, hh * HEAD_PAD:(hh + 1) * HEAD_PAD] = (dq_s[hh] * scale_q).astype(dq_ref.dtype)
            if use_f:
                dr_ref[...] = dr_s[...]

        if nx:
            pl.when(last)(x_wait)

    w = hpb * HEAD_PAD
    kspec = pl.BlockSpec((t, w), lambda h, j: (j, h))
    qspec = pl.BlockSpec((s, w), lambda h, j: (0, h))
    rowspec = pl.BlockSpec((hpb, nq, 1, t), lambda h, j: (h, 0, 0, 0))
    any_spec = pl.BlockSpec(memory_space=pl.ANY)
    in_specs, args = [qspec, kspec, kspec, qspec, rowspec, rowspec], [q, k, v, do, lse_row, delta_row]
    full = jax.ShapeDtypeStruct((s, N_HEADS * HEAD_PAD), out_dtype)
    out_specs, out_shape = [qspec, kspec, kspec], [full, full, full]
    scratch = [pltpu.VMEM((hpb, s, HEAD_PAD), F32), pltpu.VMEM((hpb, t, HEAD_PAD), F32),
               pltpu.VMEM((hpb, t, HEAD_PAD), F32)]
    if use_f:
        in_specs += [pl.BlockSpec((hpb, t, 1), lambda h, j: (h, j, 0)), rowspec]
        args += [fcol, frow]
        out_specs += [rowspec, pl.BlockSpec((hpb, 1, 1, t), lambda h, j: (h, j, 0, 0))]
        out_shape += [jax.ShapeDtypeStruct((N_HEADS, nq, 1, t), F32)] * 2
        scratch += [pltpu.VMEM((hpb, nq, 1, t), F32), pltpu.VMEM((hpb, t, HEAD_PAD), F32)]
    if nx:
        in_specs += [any_spec] * nx
        args += list(scatter)
        out_specs += [any_spec] * nx
        out_shape += [jax.ShapeDtypeStruct(g.shape, g.dtype) for g in scatter]
        scratch += _exchange_scratch(nx)
    return pl.pallas_call(
        body, name=name, grid=(N_HEADS // hpb, nq),
        in_specs=in_specs, out_specs=tuple(out_specs), out_shape=tuple(out_shape),
        scratch_shapes=scratch,
        compiler_params=_params(("arbitrary", "arbitrary") if nx else ("parallel", "arbitrary")),
    )(*args)


def _gate_fwd(pm, pf, proj, name):
    s, w = pm.shape
    tm = _tile(s, ROW_T)

    def body(pm_ref, pf_ref, gm_ref, gf_ref, y_ref):
        y = jax.nn.sigmoid(gm_ref[...]) * pm_ref[...] + jax.nn.sigmoid(gf_ref[...]) * pf_ref[...]
        y_ref[...] = y.astype(y_ref.dtype)

    row = pl.BlockSpec((tm, w), lambda i: (i, 0))
    return pl.pallas_call(
        body, name=name, grid=(s // tm,),
        in_specs=[row, row, pl.BlockSpec((tm, w), lambda i: (i, P_GM // D_MODEL)),
                  pl.BlockSpec((tm, w), lambda i: (i, P_GF // D_MODEL))],
        out_specs=row, out_shape=jax.ShapeDtypeStruct((s, w), BF16),
        compiler_params=_params(("parallel",)),
    )(pm, pf, proj, proj)


def _gate_bwd(dy, pm, pf, proj, name):
    s, w = pm.shape
    tm = _tile(s, ROW_T)

    def body(dy_ref, pm_ref, pf_ref, gm_ref, gf_ref, dpm_ref, dpf_ref, dgm_ref, dgf_ref):
        dyv = dy_ref[...]
        sm, sf = jax.nn.sigmoid(gm_ref[...]), jax.nn.sigmoid(gf_ref[...])
        dpm_ref[...] = (dyv * sm).astype(BF16)
        dpf_ref[...] = (dyv * sf).astype(BF16)
        dgm_ref[...] = (dyv * pm_ref[...] * (sm * (1.0 - sm))).astype(BF16)
        dgf_ref[...] = (dyv * pf_ref[...] * (sf * (1.0 - sf))).astype(BF16)

    row = pl.BlockSpec((tm, w), lambda i: (i, 0))
    out = jax.ShapeDtypeStruct((s, w), BF16)
    return pl.pallas_call(
        body, name=name, grid=(s // tm,),
        in_specs=[row, row, row, pl.BlockSpec((tm, w), lambda i: (i, P_GM // D_MODEL)),
                  pl.BlockSpec((tm, w), lambda i: (i, P_GF // D_MODEL))],
        out_specs=(row, row, row, row), out_shape=(out, out, out, out),
        compiler_params=_params(("parallel",)),
    )(dy, pm, pf, proj, proj)


CONV_TN = 256
CONV_TM = 512
HALO = BF16_ROWS


def _shift_down(u, prev, n):
    rolled = pltpu.roll(u, n, 0)
    prev_rolled = pltpu.roll(prev, n, 0)
    top = jnp.concatenate([prev_rolled, rolled[HALO:]], axis=0)
    row = lax.broadcasted_iota(jnp.int32, u.shape, 0)
    return jnp.where(row < n, top, rolled)


def _conv_tile(u, prev, w_ref, b_ref):
    um1 = _shift_down(u, prev, 1)
    um2 = _shift_down(u, prev, 2)
    uc = b_ref[...] + w_ref[0:1, :] * um2 + w_ref[1:2, :] * um1 + w_ref[2:3, :] * u
    return uc, um1, um2


def _conv_specs(tm, tn, ncol_off):
    blk = lambda off: pl.BlockSpec((tm, tn), lambda j, i: (i, j + off))
    halo = lambda off: pl.BlockSpec((HALO, tn), lambda j, i: (jnp.maximum(i * (tm // HALO) - 1, 0), j + off))
    wsp = lambda off: pl.BlockSpec((3, tn), lambda j, i: (0, j + off))
    bsp = lambda off: pl.BlockSpec((1, tn), lambda j, i: (0, j + off))
    return blk, halo, wsp, bsp


def _convglu_fwd(u, conv_w, conv_b, name):
    s = u.shape[0]
    tm, tn = _tile(s, CONV_TM), CONV_TN
    nj = D_FF // tn
    blk, halo, wsp, bsp = _conv_specs(tm, tn, nj)

    def body(ug_ref, pg_ref, uv_ref, pv_ref, wg_ref, wv_ref, bg_ref, bv_ref, a_ref):
        live = (pl.program_id(1) > 0).astype(F32)
        gate, _, _ = _conv_tile(ug_ref[...].astype(F32), pg_ref[...].astype(F32) * live, wg_ref, bg_ref)
        val, _, _ = _conv_tile(uv_ref[...].astype(F32), pv_ref[...].astype(F32) * live, wv_ref, bv_ref)
        a_ref[...] = (gate * jax.nn.sigmoid(gate) * val).astype(a_ref.dtype)

    return pl.pallas_call(
        body, name=name, grid=(nj, s // tm),
        in_specs=[blk(0), halo(0), blk(nj), halo(nj), wsp(0), wsp(nj), bsp(0), bsp(nj)],
        out_specs=blk(0), out_shape=jax.ShapeDtypeStruct((s, D_FF), BF16),
        compiler_params=_params(("parallel", "arbitrary")),
    )(u, u, u, u, conv_w, conv_w, conv_b, conv_b)


def _convglu_bwd(da, u, conv_w, conv_b, name):
    s = u.shape[0]
    tm, tn = _tile(s, CONV_TM), CONV_TN
    nj = D_FF // tn
    blk, halo, wsp, bsp = _conv_specs(tm, tn, nj)

    def body(da_ref, ug_ref, pg_ref, uv_ref, pv_ref, wg_ref, wv_ref, bg_ref, bv_ref,
             dg_ref, dv_ref, sg_ref, sv_ref):
        live = (pl.program_id(1) > 0).astype(F32)
        ug, uv = ug_ref[...].astype(F32), uv_ref[...].astype(F32)
        gate, ug1, ug2 = _conv_tile(ug, pg_ref[...].astype(F32) * live, wg_ref, bg_ref)
        val, uv1, uv2 = _conv_tile(uv, pv_ref[...].astype(F32) * live, wv_ref, bv_ref)
        dav = da_ref[...].astype(F32)
        sig = jax.nn.sigmoid(gate)
        dgate = dav * val * (sig * (1.0 + gate * (1.0 - sig)))
        dval = dav * (gate * sig)
        dg_ref[...] = dgate.astype(dg_ref.dtype)
        dv_ref[...] = dval.astype(dv_ref.dtype)

        @pl.when(pl.program_id(1) == 0)
        def _():
            sg_ref[...] = jnp.zeros_like(sg_ref)
            sv_ref[...] = jnp.zeros_like(sv_ref)

        for s_ref, d, taps in ((sg_ref, dgate, (ug2, ug1, ug)), (sv_ref, dval, (uv2, uv1, uv))):
            for r, tap in enumerate(taps):
                s_ref[r:r + 1, :] += jnp.sum(d * tap, axis=0, keepdims=True)
            s_ref[3:4, :] += jnp.sum(d, axis=0, keepdims=True)

    sums = lambda off: pl.BlockSpec((8, tn), lambda j, i: (0, j + off))
    return pl.pallas_call(
        body, name=name, grid=(nj, s // tm),
        in_specs=[blk(0), blk(0), halo(0), blk(nj), halo(nj), wsp(0), wsp(nj), bsp(0), bsp(nj)],
        out_specs=(blk(0), blk(0), sums(0), sums(0)),
        out_shape=(jax.ShapeDtypeStruct((s, D_FF), BF16), jax.ShapeDtypeStruct((s, D_FF), BF16),
                   jax.ShapeDtypeStruct((8, D_FF), F32), jax.ShapeDtypeStruct((8, D_FF), F32)),
        compiler_params=_params(("parallel", "arbitrary")),
    )(da, u, u, u, u, conv_w, conv_w, conv_b, conv_b)


def _conv_transpose(d, conv_w_half, name):
    s, w = d.shape
    tm, tn = _tile(s, CONV_TM), CONV_TN
    last = s // tm - 1

    def body(d_ref, nx_ref, w_ref, o_ref):
        dv = d_ref[...].astype(F32)
        nxt = nx_ref[...].astype(F32) * (pl.program_id(1) < last).astype(F32)
        row = lax.broadcasted_iota(jnp.int32, dv.shape, 0)

        def shift_up(n):
            rolled = pltpu.roll(dv, tm - n, 0)
            nxt_rolled = pltpu.roll(nxt, HALO - n, 0)
            bottom = jnp.concatenate([rolled[:tm - HALO], nxt_rolled], axis=0)
            return jnp.where(row >= tm - n, bottom, rolled)

        out = w_ref[2:3, :] * dv + w_ref[1:2, :] * shift_up(1) + w_ref[0:1, :] * shift_up(2)
        o_ref[...] = out.astype(o_ref.dtype)

    blk = pl.BlockSpec((tm, tn), lambda j, i: (i, j))
    nxt_spec = pl.BlockSpec((HALO, tn), lambda j, i: (jnp.minimum((i + 1) * (tm // HALO), s // HALO - 1), j))
    return pl.pallas_call(
        body, name=name, grid=(w // tn, s // tm),
        in_specs=[blk, nxt_spec, pl.BlockSpec((3, tn), lambda j, i: (0, j))],
        out_specs=blk, out_shape=jax.ShapeDtypeStruct((s, w), BF16),
        compiler_params=_params(("parallel", "arbitrary")),
    )(d, d, conv_w_half)


def _split3(a):
    a1 = a.astype(BF16)
    r1 = a - a1.astype(F32)
    a2 = r1.astype(BF16)
    a3 = (r1 - a2.astype(F32)).astype(BF16)
    return a1, a2, a3


def _ones_dot_right(a, mat):
    return sum(jnp.dot(p, mat, preferred_element_type=F32) for p in _split3(a))


def _ones_dot_left(mat, a):
    return sum(jnp.dot(mat, p, preferred_element_type=F32) for p in _split3(a))


def _tri(n, cmp):
    r = lax.broadcasted_iota(jnp.int32, (n, n), 0)
    c = lax.broadcasted_iota(jnp.int32, (n, n), 1)
    return cmp(r, c).astype(BF16)


def _forget_fwd(z, bias, name):
    nh, nr, nl = z.shape

    def body(z_ref, b_ref, f_ref):
        within = _tri(nl, lambda r, c: r <= c)
        before = _tri(nr, lambda r, c: c < r)
        for h in range(nh):
            x = z_ref[h] + b_ref[h]
            lf = jnp.minimum(x, 0.0) - jnp.log(1.0 + jnp.exp(-jnp.abs(x)))
            pre = _ones_dot_right(lf, within)
            tot = jnp.zeros((nr, nl), F32) + jnp.sum(lf, axis=1, keepdims=True)
            f_ref[h] = pre + _ones_dot_left(before, tot)

    return pl.pallas_call(
        body, name=name, out_shape=jax.ShapeDtypeStruct(z.shape, F32),
        compiler_params=pltpu.CompilerParams(vmem_limit_bytes=VMEM_LIMIT_BYTES),
    )(z, bias)


def _forget_bwd(df_rows, df_cols, z, bias, name):
    nh, nr, nl = z.shape

    def body(dfr_ref, dfc_ref, z_ref, b_ref, dz_ref, db_ref):
        within = _tri(nl, lambda r, c: r >= c)
        after = _tri(nr, lambda r, c: c > r)
        for h in range(nh):
            g = dfr_ref[h] + dfc_ref[h]
            suf = _ones_dot_right(g, within)
            tot = jnp.zeros((nr, nl), F32) + jnp.sum(g, axis=1, keepdims=True)
            dlf = suf + _ones_dot_left(after, tot)
            dz = dlf * jax.nn.sigmoid(-(z_ref[h] + b_ref[h]))
            dz_ref[h] = dz
            db_ref[h] = jnp.zeros((1, nl), F32) + jnp.sum(dz)

    return pl.pallas_call(
        body, name=name,
        out_shape=(jax.ShapeDtypeStruct(z.shape, F32), jax.ShapeDtypeStruct(bias.shape, F32)),
        compiler_params=pltpu.CompilerParams(vmem_limit_bytes=VMEM_LIMIT_BYTES),
    )(df_rows, df_cols, z, bias)


def _ada_fwd(c_col, w, b, name):
    kdim, n = w.shape

    def body(c_ref, w_ref, b_ref, ada_ref, act_ref):
        wv = w_ref[...]
        for e in range(N_DEV):
            cv = c_ref[e]
            act = cv * jax.nn.sigmoid(cv)
            act_ref[e] = act
            ada_ref[e:e + 1, :] = jnp.sum(act * wv, axis=0, keepdims=True) + b_ref[...]

    return pl.pallas_call(
        body, name=name,
        out_shape=(jax.ShapeDtypeStruct((N_DEV, n), F32), jax.ShapeDtypeStruct((N_DEV, kdim, 1), F32)),
        compiler_params=pltpu.CompilerParams(vmem_limit_bytes=VMEM_LIMIT_BYTES),
    )(c_col, w, b)


def _ada_bwd(act_col, dada, name):
    kdim = act_col.shape[1]
    n = dada.shape[1]

    def body(act_ref, d_ref, g_ref):
        acc = act_ref[0] * d_ref[0:1, :]
        for e in range(1, N_DEV):
            acc = acc + act_ref[e] * d_ref[e:e + 1, :]
        g_ref[...] = acc

    return pl.pallas_call(
        body, name=name, out_shape=jax.ShapeDtypeStruct((kdim, n), F32),
        compiler_params=pltpu.CompilerParams(vmem_limit_bytes=VMEM_LIMIT_BYTES),
    )(act_col, dada)


def _adamw(parts, w, m, v, name, tr=128):
    npart, r, c = parts.shape
    tr = _tile(r, tr, step=BF16_ROWS) if r % BF16_ROWS == 0 else r

    def body(p_ref, w_ref, m_ref, v_ref, g_ref, d_ref, nm_ref, nv_ref):
        g = p_ref[0].astype(F32)
        for e in range(1, npart):
            g = g + p_ref[e].astype(F32)
        nm = ADAM_B1 * m_ref[...] + (1.0 - ADAM_B1) * g
        nv = ADAM_B2 * v_ref[...] + (1.0 - ADAM_B2) * (g * g)
        m_hat = nm / (1.0 - ADAM_B1 ** ADAM_STEP)
        v_hat = nv / (1.0 - ADAM_B2 ** ADAM_STEP)
        g_ref[...] = g
        d_ref[...] = -ADAM_LR * (m_hat / (jnp.sqrt(v_hat) + ADAM_EPS) + ADAM_WD * w_ref[...])
        nm_ref[...] = nm
        nv_ref[...] = nv

    row = pl.BlockSpec((tr, c), lambda i: (i, 0))
    out = jax.ShapeDtypeStruct((r, c), F32)
    return pl.pallas_call(
        body, name=name, grid=(r // tr,),
        in_specs=[pl.BlockSpec((npart, tr, c), lambda i: (0, i, 0)), row, row, row],
        out_specs=(row, row, row, row), out_shape=(out, out, out, out),
        compiler_params=_params(("parallel",)),
    )(parts, w, m, v)


EARLY = ("w_in", "w_uq", "w_ukv")
LATE = ("w_o_mla", "w_o_fox", "w_out", "w_up", "conv_w", "w_down")
BIG = EARLY + LATE


def _cols_to_full(stack):
    n, r, c = stack.shape
    return stack.transpose(1, 0, 2).reshape(r, n * c)


def _full_to_cols(full, c):
    r = full.shape[0]
    return full.reshape(r, N_DEV, c).transpose(1, 0, 2)


def _pad_heads(a, width, ones_lane=False):
    s = a.shape[0]
    a = a.reshape(s, N_HEADS, width)
    if ones_lane:
        assert width == SUM_LANE
        tail = jnp.zeros((s, N_HEADS, HEAD_PAD - width), a.dtype).at[:, :, 0].set(1.0)
        return jnp.concatenate([a, tail], axis=2).reshape(s, N_HEADS * HEAD_PAD)
    return jnp.pad(a, ((0, 0), (0, 0), (0, HEAD_PAD - width))).reshape(s, N_HEADS * HEAD_PAD)


def _unpad_heads(a, width):
    s = a.shape[0]
    return a.reshape(s, N_HEADS, HEAD_PAD)[:, :, :width].reshape(s, N_HEADS * width)


def _w_in_padded(w_in):
    seg = [w_in[:, IN_OFF[i]:IN_OFF[i + 1]] for i in range(9)]
    cq, ckv, kr, fq, fk, fv, fl, gm, gf = seg
    padc = lambda a, n: jnp.pad(a, ((0, 0), (0, n - a.shape[1])))
    return jnp.concatenate([gm, gf, fq, fk, fv, cq, ckv, padc(kr, 128), padc(fl, 128)], axis=1)


def _w_in_unpadded(g):
    return jnp.concatenate([
        g[:, P_CQ:P_CQ + 384], g[:, P_CKV:P_CKV + 256], g[:, P_KR:P_KR + 32], g[:, P_FQ:P_FQ + 512],
        g[:, P_FK:P_FK + 512], g[:, P_FV:P_FV + 512], g[:, P_FL:P_FL + 8], g[:, P_GM:P_GM + 1024],
        g[:, P_GF:P_GF + 1024]], axis=1)


SMALL = (("b_ada", 6144, 6144), ("norm_mix_g", 1024, 1024), ("q_norm_g", 384, 384), ("kv_norm_g", 256, 256),
         ("b_forget", 8, 128), ("norm_ffn_g", 1024, 1024), ("conv_b", 5632, 5632), ("norm_final_g", 1024, 1024),
         ("loss", 1, 128))
SMALL_OFF = {}
_o = 0
for _n, _real, _padded in SMALL:
    SMALL_OFF[_n] = _o
    _o += _padded
SMALL_W = _o


def _pack_small(vals):
    parts = []
    for nme, real, padded in SMALL:
        a = vals[nme].reshape(1, real).astype(F32)
        parts.append(jnp.pad(a, ((0, 0), (0, padded - real))))
    return jnp.concatenate(parts, axis=1)


def kernel(x, c, positions, w_ada, b_ada, norm_mix_g, w_in, q_norm_g, w_uq, kv_norm_g, w_ukv, b_forget, w_o_mla, w_o_fox, w_out, norm_ffn_g, w_up, conv_w, conv_b, w_down, norm_final_g, loss_target, m_w_ada, m_b_ada, m_norm_mix_g, m_w_in, m_q_norm_g, m_w_uq, m_kv_norm_g, m_w_ukv, m_b_forget, m_w_o_mla, m_w_o_fox, m_w_out, m_norm_ffn_g, m_w_up, m_conv_w, m_conv_b, m_w_down, m_norm_final_g, v_w_ada, v_b_ada, v_norm_mix_g, v_w_in, v_q_norm_g, v_w_uq, v_kv_norm_g, v_w_ukv, v_b_forget, v_w_o_mla, v_w_o_fox, v_w_out, v_norm_ffn_g, v_w_up, v_conv_w, v_conv_b, v_w_down, v_norm_final_g):
    me = 4 * lax.axis_index("x") + 2 * lax.axis_index("y") + lax.axis_index("c")
    x = x[0]
    target = loss_target[0]
    s = x.shape[0]
    nblk = s // ATT_T
    big_w = {"w_in": w_in, "w_uq": w_uq, "w_ukv": w_ukv, "w_o_mla": w_o_mla, "w_o_fox": w_o_fox,
             "w_out": w_out, "w_up": w_up, "conv_w": conv_w, "w_down": w_down}
    big_m = {"w_in": m_w_in, "w_uq": m_w_uq, "w_ukv": m_w_ukv, "w_o_mla": m_w_o_mla, "w_o_fox": m_w_o_fox,
             "w_out": m_w_out, "w_up": m_w_up, "conv_w": m_conv_w, "w_down": m_w_down}
    big_v = {"w_in": v_w_in, "w_uq": v_w_uq, "w_ukv": v_w_ukv, "w_o_mla": v_w_o_mla, "w_o_fox": v_w_o_fox,
             "w_out": v_w_out, "w_up": v_w_up, "conv_w": v_conv_w, "w_down": v_w_down}

    shard = lambda k: big_w[k][0] if k == "conv_w" else big_w[k][0].astype(BF16)
    st = dict(zip(EARLY, _all_gather([shard(k) for k in EARLY], "gather_weights")))
    w_in_p = _w_in_padded(_cols_to_full(st["w_in"]))
    uq = st["w_uq"]
    w_uq_p = jnp.pad(uq, ((0, 0), (0, 0), (0, HEAD_PAD - 96))).transpose(1, 0, 2).reshape(MLA_Q_RANK, 1024)
    ukv = st["w_ukv"]
    zeros64 = jnp.zeros((N_HEADS, MLA_KV_RANK, 64), BF16)
    w_uk_p = jnp.concatenate([ukv[:, :, :64], zeros64], axis=2).transpose(1, 0, 2).reshape(MLA_KV_RANK, 1024)
    w_uv_p = jnp.concatenate([ukv[:, :, 64:], zeros64], axis=2).transpose(1, 0, 2).reshape(MLA_KV_RANK, 1024)
    place = np.zeros((HEAD_PAD, N_HEADS, HEAD_PAD), np.float32)
    for j in range(MLA_ROPE):
        place[j, :, MLA_NOPE + j] = 1.0
    place = jnp.asarray(place.reshape(HEAD_PAD, 1024), BF16)
    w_kv_comb = jnp.concatenate([
        jnp.concatenate([w_uk_p, w_uv_p], axis=1),
        jnp.concatenate([place, jnp.zeros((HEAD_PAD, 1024), BF16)], axis=1)], axis=0)

    (c_all,) = _all_gather([c], "gather_c")
    b_ada_mine = lax.dynamic_slice(b_ada, (0, me * 768), (1, 768))
    ada_cols, act_col = _ada_fwd(c_all.reshape(N_DEV, D_MODEL, 1), w_ada[0], b_ada_mine, "ada_fwd")
    (ada_all,) = _all_gather([ada_cols], "gather_ada")
    ada = lax.dynamic_slice(ada_all, (0, me, 0), (N_DEV, 1, 768)).reshape(1, N_ADA * D_MODEL)
    sh_m, sc_m, g_m, sh_f, sc_f, g_f = [ada[:, i * D_MODEL:(i + 1) * D_MODEL] for i in range(N_ADA)]

    inv_freq = ROPE_THETA ** (-jnp.arange(0, MLA_ROPE, 2, dtype=F32) / MLA_ROPE)
    ang = positions[0].astype(F32)[:, None] * inv_freq
    cos, sin = jnp.cos(ang), jnp.sin(ang)
    rope_c = jnp.concatenate([jnp.ones((s, 64), F32), cos, cos, jnp.zeros((s, 32), F32)], axis=1)
    rope_s = jnp.concatenate([jnp.zeros((s, 64), F32), -sin, sin, jnp.zeros((s, 32), F32)], axis=1)

    zero_d = jnp.zeros((1, D_MODEL), F32)

    h1 = _rms_mod(x, norm_mix_g, sc_m, sh_m, "norm_mix")
    proj = _mm(h1, w_in_p, "nn", F32, "proj_in", tn=640)
    cq = proj[:, P_CQ:P_CQ + 384]
    ckv = proj[:, P_CKV:P_CKV + 256]
    qn = _rms_mod(cq, q_norm_g, jnp.zeros((1, 384), F32), jnp.zeros((1, 384), F32), "q_norm")
    kvn = _rms_mod(ckv, kv_norm_g, jnp.zeros((1, 256), F32), jnp.zeros((1, 256), F32), "kv_norm")
    kv_in = jnp.concatenate([kvn, proj[:, P_KR:P_KR + 128].astype(BF16)], axis=1)
    q_pre = _mm(qn, w_uq_p, "nn", F32, "q_up")
    kv_pre = _mm(kv_in, w_kv_comb, "nn", F32, "kv_up")
    q_fold = MLA_SCALE * LOG2E
    (q_att,) = _rope(q_pre, rope_c * q_fold, rope_s * q_fold, "rope_q", N_HEADS, (BF16,))
    k_att, v_att = _rope(kv_pre, rope_c, rope_s, "rope_kv", N_HEADS, (BF16, BF16))
    o_mla, o_mla_b, lse_mla, *late = _attn_fwd(q_att, k_att, v_att, None, None, "mla_fwd",
                                               gather=[shard(k) for k in LATE])
    st.update(zip(LATE, late))
    pad_o = lambda full: jnp.pad(full.reshape(N_HEADS, 64, 1024), ((0, 0), (0, 64), (0, 0))).reshape(1024, 1024)
    w_o_mla_p = pad_o(_cols_to_full(st["w_o_mla"]))
    w_o_fox_p = pad_o(_cols_to_full(st["w_o_fox"]))
    w_out_f = st["w_out"].reshape(1024, 1024)
    w_up_f = _cols_to_full(st["w_up"])
    conv_w_f = _cols_to_full(st["conv_w"])
    w_down_f = st["w_down"].reshape(D_FF, 1024)

    fq = _pad_heads(proj[:, P_FQ:P_FQ + 512] * (FOX_SCALE * LOG2E), 64).astype(BF16)
    fk = _pad_heads(proj[:, P_FK:P_FK + 512], 64).astype(BF16)
    fv = _pad_heads(proj[:, P_FV:P_FV + 512], 64, ones_lane=True).astype(BF16)
    z = proj[:, P_FL:P_FL + 8].T.reshape(N_HEADS, s // SEQ_LANES, SEQ_LANES)
    bias_f = jnp.broadcast_to(b_forget.reshape(N_HEADS, 1, 1), (N_HEADS, 1, SEQ_LANES))
    f_cum = _forget_fwd(z, bias_f, "forget_fwd")
    f_col = (f_cum * LOG2E).reshape(N_HEADS, s, 1)
    f_row = f_col.reshape(N_HEADS, nblk, 1, ATT_T)
    o_fox, o_fox_b, lse_fox = _attn_fwd(fq, fk, fv, f_col, f_row, "fox_fwd")

    pm = _mm(o_mla_b, w_o_mla_p, "nn", F32, "o_mla_proj")
    pf = _mm(o_fox_b, w_o_fox_p, "nn", F32, "o_fox_proj")
    y = _gate_fwd(pm, pf, proj, "gate_fwd")
    x2, mix = _mm(y, w_out_f, "nn", F32, "out_proj", res=x, gvec=g_m)

    h2 = _rms_mod(x2, norm_ffn_g, sc_f, sh_f, "norm_ffn")
    u = _mm(h2, w_up_f, "nn", BF16, "ffn_up")
    a = _convglu_fwd(u, conv_w_f, conv_b, "convglu_fwd")
    x3, ffn = _mm(a, w_down_f, "nn", F32, "ffn_down", res=x2, gvec=g_f, tk=2816)

    dx3, sums_final = _final_loss(x3, target, norm_final_g.reshape(1, D_MODEL), "final_loss")
    dffn, sums_gf = _scale_bwd(dx3, ffn, g_f, "ffn_scale_bwd")
    da = _mm(dffn, w_down_f, "nt", BF16, "ffn_down_dx", tn=1408)
    g_w_down = _mm(a, dffn, "tn", F32, "ffn_down_dw", tm=256, tn=1024, tk=s)
    dgate, dval, s_gate, s_val = _convglu_bwd(da, u, conv_w_f, conv_b, "convglu_bwd")
    du = jnp.concatenate([_conv_transpose(dgate, conv_w_f[:, :D_FF], "conv_t_gate"),
                          _conv_transpose(dval, conv_w_f[:, D_FF:], "conv_t_val")], axis=1)
    dh2 = _mm(du, w_up_f, "nt", F32, "ffn_up_dx", tn=512, tk=2 * D_FF)
    g_w_up = _mm(h2, du, "tn", F32, "ffn_up_dw", tn=256, tk=s)
    dx2, sums_ffn = _rms_mod_bwd(dh2, x2, norm_ffn_g, sc_f, dx3, "norm_ffn_bwd")

    dmix, sums_gm = _scale_bwd(dx2, mix, g_m, "mix_scale_bwd")
    dy = _mm(dmix, w_out_f, "nt", F32, "out_proj_dx")
    g_w_out = _mm(y, dmix, "tn", F32, "out_proj_dw", tn=256, tk=s)
    dpm, dpf, dgm, dgf = _gate_bwd(dy, pm, pf, proj, "gate_bwd")
    do_mla_b = _mm(dpm, w_o_mla_p, "nt", BF16, "o_mla_dx", tn=1024)
    do_fox_b = _mm(dpf, w_o_fox_p, "nt", BF16, "o_fox_dx", tn=1024)
    g_w_o_mla_p = _mm(o_mla_b, dpm, "tn", F32, "o_mla_dw", tn=256, tk=s)
    g_w_o_fox_p = _mm(o_fox_b, dpf, "tn", F32, "o_fox_dw", tn=256, tk=s)

    unpad_o = lambda g: g.reshape(N_HEADS, HEAD_PAD, 1024)[:, :64].reshape(512, 1024)
    g_conv_w = jnp.concatenate([s_gate[0:3], s_val[0:3]], axis=1)
    g_blocks = {
        "w_o_mla": _full_to_cols(unpad_o(g_w_o_mla_p), 128), "w_o_fox": _full_to_cols(unpad_o(g_w_o_fox_p), 128),
        "w_out": g_w_out.reshape(N_DEV, 128, 1024), "w_up": _full_to_cols(g_w_up, 704),
        "conv_w": _full_to_cols(g_conv_w, 704), "w_down": g_w_down.reshape(N_DEV, 352, 1024)}

    delta_mla = _attn_delta(o_mla, do_mla_b, "mla_delta")
    dq_rot, dk_rot, dv_mla, *late_recv = _attn_bwd(
        q_att, k_att, v_att, do_mla_b, lse_mla, delta_mla, None, None, MLA_SCALE, 1.0 / LOG2E, "mla_bwd", BF16,
        scatter=[g_blocks[k].astype(BF16) for k in LATE])
    (dq_pre,) = _rope(dq_rot, rope_c, -rope_s, "rope_q_bwd", N_HEADS, (BF16,))
    dkv_pre = _rope_bwd_kv(dk_rot, dv_mla, rope_c, -rope_s, "rope_kv_bwd")
    dqn = _mm(dq_pre, w_uq_p, "nt", F32, "q_up_dx")
    g_w_uq_p = _mm(qn, dq_pre, "tn", F32, "q_up_dw", tk=s)
    dkv_in = _mm(dkv_pre, w_kv_comb, "nt", F32, "kv_up_dx")
    g_w_kv_comb = _mm(kv_in, dkv_pre, "tn", F32, "kv_up_dw", tk=s)
    dcq, sums_q = _rms_mod_bwd(dqn, cq, q_norm_g, jnp.zeros((1, 384), F32), None, "q_norm_bwd")
    dckv, sums_kv = _rms_mod_bwd(dkv_in[:, :256], ckv, kv_norm_g, jnp.zeros((1, 256), F32), None, "kv_norm_bwd")
    delta_fox = _attn_delta(o_fox, do_fox_b, "fox_delta")
    dfq, dfk, dfv, dfr, dfc = _attn_bwd(fq, fk, fv, do_fox_b, lse_fox, delta_fox, f_col, f_row,
                                        FOX_SCALE, 1.0 / LOG2E, "fox_bwd", BF16)
    df_rows = dfr.reshape(N_HEADS, s // SEQ_LANES, SEQ_LANES)
    df_cols = dfc.reshape(N_HEADS, s // SEQ_LANES, SEQ_LANES)
    dz, db_f = _forget_bwd(df_rows, df_cols, z, bias_f, "forget_bwd")
    dfl = jnp.pad(dz.reshape(N_HEADS, s).T, ((0, 0), (0, 128 - N_HEADS)))

    dproj = jnp.concatenate([
        dgm, dgf, _unpad_heads(dfq, 64), _unpad_heads(dfk, 64), _unpad_heads(dfv, 64),
        dcq.astype(BF16), dckv.astype(BF16), dkv_in[:, 256:384].astype(BF16), dfl.astype(BF16)], axis=1)
    dh1 = _mm(dproj, w_in_p, "nt", F32, "proj_in_dx", tn=512, tk=D_IN_P)
    g_w_in_p = _mm(h1, dproj, "tn", F32, "proj_in_dw", tm=512, tn=640, tk=s)
    grad_x, sums_mix = _rms_mod_bwd(dh1, x, norm_mix_g, sc_m, dx2, "norm_mix_bwd")

    g_w_in = _w_in_unpadded(g_w_in_p)
    g_uq = g_w_uq_p.reshape(MLA_Q_RANK, N_HEADS, HEAD_PAD)[:, :, :96].transpose(1, 0, 2)
    g_uk = g_w_kv_comb[:256, :1024].reshape(256, N_HEADS, HEAD_PAD)[:, :, :64]
    g_uv = g_w_kv_comb[:256, 1024:].reshape(256, N_HEADS, HEAD_PAD)[:, :, :64]
    g_ukv = jnp.concatenate([g_uk, g_uv], axis=2).transpose(1, 0, 2)
    g_blocks.update({"w_in": _full_to_cols(g_w_in, 533), "w_uq": g_uq, "w_ukv": g_ukv})
    early_recv = _all_to_all([g_blocks[k].astype(BF16) for k in EARLY], "scatter_grads")
    g_big, d_big, nm_big, nv_big = {}, {}, {}, {}
    for k, parts in zip(BIG, list(early_recv) + list(late_recv)):
        g_big[k], d_big[k], nm_big[k], nv_big[k] = [
            t[None] for t in _adamw(parts, big_w[k][0], big_m[k][0], big_v[k][0], "adamw_" + k)]

    dada = jnp.concatenate([sums_mix[0:1], sums_mix[1:2], sums_gm[0:1], sums_ffn[0:1], sums_ffn[1:2], sums_gf[0:1]],
                           axis=1)
    small_part = _pack_small({
        "b_ada": dada, "norm_mix_g": sums_mix[2:3], "q_norm_g": sums_q[2:3], "kv_norm_g": sums_kv[2:3],
        "b_forget": db_f[:, 0, 0], "norm_ffn_g": sums_ffn[2:3],
        "conv_b": jnp.concatenate([s_gate[3:4], s_val[3:4]], axis=1), "norm_final_g": sums_final[0:1],
        "loss": sums_final[1:2, 0:1]})
    (small_all,) = _all_gather([small_part], "gather_small")
    zero1 = jnp.zeros((1,), F32)
    small_w = {"b_ada": b_ada, "norm_mix_g": norm_mix_g, "q_norm_g": q_norm_g, "kv_norm_g": kv_norm_g,
               "b_forget": b_forget, "norm_ffn_g": norm_ffn_g, "conv_b": conv_b, "norm_final_g": norm_final_g,
               "loss": zero1}
    small_m = {"b_ada": m_b_ada, "norm_mix_g": m_norm_mix_g, "q_norm_g": m_q_norm_g, "kv_norm_g": m_kv_norm_g,
               "b_forget": m_b_forget, "norm_ffn_g": m_norm_ffn_g, "conv_b": m_conv_b,
               "norm_final_g": m_norm_final_g, "loss": zero1}
    small_v = {"b_ada": v_b_ada, "norm_mix_g": v_norm_mix_g, "q_norm_g": v_q_norm_g, "kv_norm_g": v_kv_norm_g,
               "b_forget": v_b_forget, "norm_ffn_g": v_norm_ffn_g, "conv_b": v_conv_b,
               "norm_final_g": v_norm_final_g, "loss": zero1}
    g_sm, d_sm, nm_sm, nv_sm = _adamw(small_all, _pack_small(small_w), _pack_small(small_m), _pack_small(small_v),
                                      "adamw_small")
    loss = g_sm[0, SMALL_OFF["loss"]]

    dada_all = small_all[:, 0, SMALL_OFF["b_ada"]:SMALL_OFF["b_ada"] + N_ADA * D_MODEL]
    dada_mine = lax.dynamic_slice(dada_all, (0, me * 768), (N_DEV, 768))
    g_ada_local = _ada_bwd(act_col, dada_mine, "ada_bwd")
    g_ada, d_ada, nm_ada, nv_ada = _adamw(g_ada_local[None], w_ada[0], m_w_ada[0], v_w_ada[0], "adamw_ada")

    def small_out(t, nme, shape):
        real = dict((n_, r_) for n_, r_, _ in SMALL)[nme]
        o = SMALL_OFF[nme]
        return t[0, o:o + real].reshape(shape)

    order = ["w_ada", "b_ada", "norm_mix_g", "w_in", "q_norm_g", "w_uq", "kv_norm_g", "w_ukv", "b_forget",
             "w_o_mla", "w_o_fox", "w_out", "norm_ffn_g", "w_up", "conv_w", "conv_b", "w_down", "norm_final_g"]
    small_shapes = {"b_ada": (1, 6144), "norm_mix_g": (1, 1024), "q_norm_g": (1, 384), "kv_norm_g": (1, 256),
                    "b_forget": (1, 8), "norm_ffn_g": (1, 1024), "conv_b": (1, 5632), "norm_final_g": (1024,)}

    def family(big, small, ada_t):
        out = []
        for nme in order:
            if nme == "w_ada":
                out.append(ada_t[None])
            elif nme in small_shapes:
                out.append(small_out(small, nme, small_shapes[nme]))
            else:
                out.append(big[nme])
        return out

    return (loss, grad_x[None], *family(g_big, g_sm, g_ada), *family(d_big, d_sm, d_ada),
            *family(nm_big, nm_sm, nm_ada), *family(nv_big, nv_sm, nv_ada))
```

```python
import math

import numpy as np
import jax
import jax.numpy as jnp
from jax import lax
from jax.experimental import pallas as pl
from jax.experimental.pallas import tpu as pltpu

F32 = jnp.float32
BF16 = jnp.bfloat16

N_DEV = 8
D_MODEL = 1024
N_HEADS = 8
HEAD_PAD = 128
MLA_Q_RANK = 384
MLA_KV_RANK = 256
MLA_NOPE = 64
MLA_ROPE = 32
MLA_V = 64
FOX_DIM = 64
D_FF = 2816
N_ADA = 6
EPS = 1e-6
ROPE_THETA = 10000.0
MLA_SCALE = 1.0 / math.sqrt(MLA_NOPE + MLA_ROPE)
FOX_SCALE = 1.0 / math.sqrt(FOX_DIM)
IN_SPLITS = (384, 256, 32, 512, 512, 512, 8, 1024, 1024)
D_IN = sum(IN_SPLITS)
IN_OFF = tuple(int(v) for v in np.cumsum((0,) + IN_SPLITS))
P_GM, P_GF, P_FQ, P_FK, P_FV, P_CQ, P_CKV, P_KR, P_FL, D_IN_P = 0, 1024, 2048, 2560, 3072, 3584, 3968, 4224, 4352, 4480

ADAM_LR, ADAM_B1, ADAM_B2, ADAM_EPS, ADAM_WD, ADAM_STEP = 0.001, 0.9, 0.999, 1e-08, 0.01, 10

VMEM_LIMIT_BYTES = 56 * 1024 * 1024
NEG_BIG = -1e30
ATT_T = 512
LOG2E = 1.4426950408889634
SUM_LANE = 64
ROW_T = 256
SEQ_LANES = 128
BF16_ROWS = 16
FWD_HEADS_PER_STEP = 4
BWD_HEADS_PER_STEP = 2


def _params(sem):
    return pltpu.CompilerParams(dimension_semantics=sem, vmem_limit_bytes=VMEM_LIMIT_BYTES)


def _tile(n, target, step=128):
    if n <= target:
        return n
    t = (target // step) * step
    while t >= step:
        if n % t == 0:
            return t
        t -= step
    return n


def _vec_spec(w, nargs):
    if nargs == 1:
        return pl.BlockSpec((1, w), lambda i: (0, 0))
    return pl.BlockSpec((1, w), lambda i, j: (0, 0))


def _comm_call(body, name, ins, out_shapes):
    n = len(ins)
    any_spec = pl.BlockSpec(memory_space=pl.ANY)
    return pl.pallas_call(
        body, name=name, out_shape=tuple(out_shapes),
        in_specs=[any_spec] * n, out_specs=tuple([any_spec] * n),
        scratch_shapes=[pltpu.SemaphoreType.DMA((n, 7)), pltpu.SemaphoreType.DMA((n, 7)),
                        pltpu.SemaphoreType.DMA((n,))],
    )(*ins)


def _all_gather(xs, name):
    n = len(xs)

    def body(*refs):
        x_refs, out_refs = refs[:n], refs[n:2 * n]
        send_sems, recv_sems, local_sems = refs[2 * n:]
        x_, y_, c_ = lax.axis_index("x"), lax.axis_index("y"), lax.axis_index("c")
        me, sibling = (x_, y_, c_), (x_, y_, 1 - c_)
        chips = [(1 - x_, y_), (x_, 1 - y_), (1 - x_, 1 - y_)]

        def slot(a, px, py, pc):
            return out_refs[a].at[4 * px + 2 * py + pc]

        def copy(a, k, block, to, src=None):
            return pltpu.make_async_remote_copy(
                src_ref=slot(a, *block) if src is None else src, dst_ref=slot(a, *block),
                send_sem=send_sems.at[a, k], recv_sem=recv_sems.at[a, k],
                device_id=to, device_id_type=pl.DeviceIdType.MESH)

        mine = [pltpu.make_async_copy(x_refs[a], slot(a, *me), local_sems.at[a]) for a in range(n)]
        for cp in mine:
            cp.start()
        first = []
        for a in range(n):
            first.append(copy(a, 0, me, sibling, src=x_refs[a]))
            first += [copy(a, 1 + j, me, (*chip, c_), src=x_refs[a]) for j, chip in enumerate(chips)]
        for cp in first:
            cp.start()
        passed = []
        for j, chip in enumerate(chips):
            for a in range(n):
                copy(a, 1 + j, (*chip, c_), me).wait_recv()
                passed.append(copy(a, 4 + j, (*chip, c_), sibling))
                passed[-1].start()
        for a in range(n):
            copy(a, 0, sibling, me).wait_recv()
            for j, chip in enumerate(chips):
                copy(a, 4 + j, (*chip, 1 - c_), me).wait_recv()
        for cp in first + passed:
            cp.wait_send()
        for cp in mine:
            cp.wait()

    return _comm_call(body, name, xs, [jax.ShapeDtypeStruct((N_DEV,) + x.shape, x.dtype) for x in xs])


def _direct_exchange(src_refs, out_refs, send_sems, recv_sems, local_sems, scatter):
    n = len(src_refs)
    x_, y_, c_ = lax.axis_index("x"), lax.axis_index("y"), lax.axis_index("c")
    me = 4 * x_ + 2 * y_ + c_

    def peer(k):
        return (x_ ^ ((k >> 2) & 1), y_ ^ ((k >> 1) & 1), c_ ^ (k & 1))

    def src(a, slot):
        return src_refs[a].at[slot] if scatter else src_refs[a]

    def copy(a, k, sending):
        px, py, pc = peer(k)
        theirs = 4 * px + 2 * py + pc
        return pltpu.make_async_remote_copy(
            src_ref=src(a, theirs if sending else me), dst_ref=out_refs[a].at[me if sending else theirs],
            send_sem=send_sems.at[a, k - 1], recv_sem=recv_sems.at[a, k - 1],
            device_id=(px, py, pc), device_id_type=pl.DeviceIdType.MESH)

    mine = [pltpu.make_async_copy(src(a, me), out_refs[a].at[me], local_sems.at[a]) for a in range(n)]
    sends = [copy(a, k, True) for a in range(n) for k in range(1, N_DEV)]

    def start():
        for cp in mine + sends:
            cp.start()

    def wait():
        for a in range(n):
            for k in range(1, N_DEV):
                copy(a, k, False).wait_recv()
        for cp in sends:
            cp.wait_send()
        for cp in mine:
            cp.wait()

    return start, wait


def _exchange_scratch(n):
    return [pltpu.SemaphoreType.DMA((n, 7)), pltpu.SemaphoreType.DMA((n, 7)), pltpu.SemaphoreType.DMA((n,))]


def _all_to_all(gs, name):
    n = len(gs)

    def body(*refs):
        start, wait = _direct_exchange(refs[:n], refs[n:2 * n], *refs[2 * n:], scatter=True)
        start()
        wait()

    return _comm_call(body, name, gs, [jax.ShapeDtypeStruct(g.shape, g.dtype) for g in gs])


def _mm(a, b, mode, out_dtype, name, res=None, gvec=None, tm=1024, tn=512, tk=1024):
    (k, m) = a.shape if mode == "tn" else a.shape[::-1]
    n = b.shape[0] if mode == "nt" else b.shape[1]
    tm, tn, tk = _tile(m, tm), _tile(n, tn), _tile(k, tk)
    nk = k // tk
    dims = {"nn": (((1,), (0,)), ((), ())), "nt": (((1,), (1,)), ((), ())), "tn": (((0,), (0,)), ((), ()))}[mode]
    has_res, has_g = res is not None, gvec is not None
    fused = has_res and has_g

    def body(*refs):
        acc_ref = refs[-1] if nk > 1 else None
        refs = list(refs[:2 + has_res + has_g + 1 + fused])
        a_ref, b_ref = refs[:2]
        res_ref = refs[2] if has_res else None
        g_ref = refs[2 + has_res] if has_g else None
        o_ref = refs[2 + has_res + has_g]
        part = lax.dot_general(a_ref[...], b_ref[...], dims, preferred_element_type=F32)

        def finish(acc):
            if fused:
                refs[-1][...] = acc
            out = g_ref[...] * acc if has_g else acc
            if has_res:
                out = res_ref[...] + out
            o_ref[...] = out.astype(o_ref.dtype)

        if nk == 1:
            finish(part)
            return
        kk = pl.program_id(2)

        @pl.when(kk == 0)
        def _():
            acc_ref[...] = part

        @pl.when(kk > 0)
        def _():
            acc_ref[...] += part

        @pl.when(kk == nk - 1)
        def _():
            finish(acc_ref[...])

    if mode == "tn":
        a_spec = pl.BlockSpec((tk, tm), lambda i, j, kk: (kk, i))
    else:
        a_spec = pl.BlockSpec((tm, tk), lambda i, j, kk: (i, kk))
    if mode == "nt":
        b_spec = pl.BlockSpec((tn, tk), lambda i, j, kk: (j, kk))
    else:
        b_spec = pl.BlockSpec((tk, tn), lambda i, j, kk: (kk, j))
    o_spec = pl.BlockSpec((tm, tn), lambda i, j, kk: (i, j))
    in_specs, args = [a_spec, b_spec], [a, b]
    out_specs, out_shape = o_spec, jax.ShapeDtypeStruct((m, n), out_dtype)
    if has_res:
        in_specs.append(o_spec)
        args.append(res)
    if has_g:
        in_specs.append(pl.BlockSpec((1, tn), lambda i, j, kk: (0, j)))
        args.append(gvec)
    if fused:
        out_specs = (o_spec, o_spec)
        out_shape = (out_shape, jax.ShapeDtypeStruct((m, n), F32))
    return pl.pallas_call(
        body, name=name, grid=(m // tm, n // tn, nk),
        in_specs=in_specs, out_specs=out_specs, out_shape=out_shape,
        scratch_shapes=[pltpu.VMEM((tm, tn), F32)] if nk > 1 else [],
        compiler_params=_params(("parallel", "parallel", "arbitrary")),
    )(*args)


def _rms_mod(x, g, sc, sh, name):
    s, w = x.shape
    tm = _tile(s, ROW_T)

    def body(x_ref, g_ref, sc_ref, sh_ref, o_ref):
        xv = x_ref[...]
        r = lax.rsqrt(jnp.mean(xv * xv, axis=-1, keepdims=True) + EPS)
        o_ref[...] = ((xv * r * g_ref[...]) * (1.0 + sc_ref[...]) + sh_ref[...]).astype(o_ref.dtype)

    row = pl.BlockSpec((tm, w), lambda i: (i, 0))
    return pl.pallas_call(
        body, name=name, grid=(s // tm,),
        in_specs=[row, _vec_spec(w, 1), _vec_spec(w, 1), _vec_spec(w, 1)],
        out_specs=row, out_shape=jax.ShapeDtypeStruct((s, w), BF16),
        compiler_params=_params(("parallel",)),
    )(x, g, sc, sh)


def _rms_mod_bwd(dh, x, g, sc, dres, name):
    s, w = x.shape
    tm = _tile(s, ROW_T)
    has_res = dres is not None

    def body(*refs):
        if has_res:
            dh_ref, x_ref, g_ref, sc_ref, dres_ref, dx_ref, sums_ref = refs
        else:
            dh_ref, x_ref, g_ref, sc_ref, dx_ref, sums_ref = refs
        xv, dhv, gv = x_ref[...], dh_ref[...], g_ref[...]
        r = lax.rsqrt(jnp.mean(xv * xv, axis=-1, keepdims=True) + EPS)
        xhat = xv * r
        dxn = dhv * (1.0 + sc_ref[...])
        dxhat = dxn * gv
        dx = r * (dxhat - xhat * jnp.mean(dxhat * xhat, axis=-1, keepdims=True))
        if has_res:
            dx = dx + dres_ref[...]
        dx_ref[...] = dx

        @pl.when(pl.program_id(0) == 0)
        def _():
            sums_ref[...] = jnp.zeros_like(sums_ref)

        sums_ref[0:1, :] += jnp.sum(dhv, axis=0, keepdims=True)
        sums_ref[1:2, :] += jnp.sum(dhv * (xhat * gv), axis=0, keepdims=True)
        sums_ref[2:3, :] += jnp.sum(dxn * xhat, axis=0, keepdims=True)

    row = pl.BlockSpec((tm, w), lambda i: (i, 0))
    in_specs = [row, row, _vec_spec(w, 1), _vec_spec(w, 1)] + ([row] if has_res else [])
    args = [dh, x, g, sc] + ([dres] if has_res else [])
    return pl.pallas_call(
        body, name=name, grid=(s // tm,),
        in_specs=in_specs,
        out_specs=(row, pl.BlockSpec((8, w), lambda i: (0, 0))),
        out_shape=(jax.ShapeDtypeStruct((s, w), F32), jax.ShapeDtypeStruct((8, w), F32)),
        compiler_params=_params(("arbitrary",)),
    )(*args)


def _scale_bwd(dx, val, gvec, name):
    s, w = dx.shape
    tm = _tile(s, ROW_T)

    def body(dx_ref, val_ref, g_ref, d_ref, sums_ref):
        dxv = dx_ref[...]
        d_ref[...] = (dxv * g_ref[...]).astype(d_ref.dtype)

        @pl.when(pl.program_id(0) == 0)
        def _():
            sums_ref[...] = jnp.zeros_like(sums_ref)

        sums_ref[0:1, :] += jnp.sum(dxv * val_ref[...], axis=0, keepdims=True)

    row = pl.BlockSpec((tm, w), lambda i: (i, 0))
    return pl.pallas_call(
        body, name=name, grid=(s // tm,),
        in_specs=[row, row, _vec_spec(w, 1)],
        out_specs=(row, pl.BlockSpec((8, w), lambda i: (0, 0))),
        out_shape=(jax.ShapeDtypeStruct((s, w), BF16), jax.ShapeDtypeStruct((8, w), F32)),
        compiler_params=_params(("arbitrary",)),
    )(dx, val, gvec)


def _final_loss(x3, target, g, name):
    s, w = x3.shape
    tm = _tile(s, ROW_T)

    def body(x_ref, t_ref, g_ref, dx_ref, sums_ref):
        xv, gv = x_ref[...], g_ref[...]
        r = lax.rsqrt(jnp.mean(xv * xv, axis=-1, keepdims=True) + EPS)
        xhat = xv * r
        err = xhat * gv - t_ref[...]
        dy = err * (1.0 / w)
        dxhat = dy * gv
        dx_ref[...] = r * (dxhat - xhat * jnp.mean(dxhat * xhat, axis=-1, keepdims=True))

        @pl.when(pl.program_id(0) == 0)
        def _():
            sums_ref[...] = jnp.zeros_like(sums_ref)

        sums_ref[0:1, :] += jnp.sum(dy * xhat, axis=0, keepdims=True)
        sums_ref[1:2, :] += jnp.zeros((1, w), F32) + (0.5 / w) * jnp.sum(err * err)

    row = pl.BlockSpec((tm, w), lambda i: (i, 0))
    return pl.pallas_call(
        body, name=name, grid=(s // tm,),
        in_specs=[row, row, _vec_spec(w, 1)],
        out_specs=(row, pl.BlockSpec((8, w), lambda i: (0, 0))),
        out_shape=(jax.ShapeDtypeStruct((s, w), F32), jax.ShapeDtypeStruct((8, w), F32)),
        compiler_params=_params(("arbitrary",)),
    )(x3, target, g)


def _rope_block(seg, cmul, smul):
    lane = lax.broadcasted_iota(jnp.int32, seg.shape, 1)
    swapped = jnp.where(lane < MLA_NOPE + MLA_ROPE // 2,
                        pltpu.roll(seg, HEAD_PAD - MLA_ROPE // 2, 1), pltpu.roll(seg, MLA_ROPE // 2, 1))
    return seg * cmul + swapped * smul


def _rope(t, cmul, smul, name, n_rot, out_dtypes):
    s, w = t.shape
    tm = _tile(s, ROW_T)
    n_out = len(out_dtypes)
    wo = w // n_out

    def body(t_ref, c_ref, s_ref, *o_refs):
        cv, sv = c_ref[...], s_ref[...]
        one = (lax.broadcasted_iota(jnp.int32, (tm, HEAD_PAD), 1) == SUM_LANE).astype(F32)
        for hb in range(w // HEAD_PAD):
            seg = t_ref[:, hb * HEAD_PAD:(hb + 1) * HEAD_PAD].astype(F32)
            if hb < n_rot:
                seg = _rope_block(seg, cv, sv)
            else:
                seg = seg + one
            o_ref = o_refs[(hb * HEAD_PAD) // wo]
            col = (hb * HEAD_PAD) % wo
            o_ref[:, col:col + HEAD_PAD] = seg.astype(o_ref.dtype)

    row = pl.BlockSpec((tm, w), lambda i: (i, 0))
    tab = pl.BlockSpec((tm, HEAD_PAD), lambda i: (i, 0))
    orow = pl.BlockSpec((tm, wo), lambda i: (i, 0))
    outs = pl.pallas_call(
        body, name=name, grid=(s // tm,),
        in_specs=[row, tab, tab],
        out_specs=tuple(orow for _ in out_dtypes),
        out_shape=tuple(jax.ShapeDtypeStruct((s, wo), dt) for dt in out_dtypes),
        compiler_params=_params(("parallel",)),
    )(t, cmul, smul)
    return outs


def _rope_bwd_kv(dk, dv, cmul, smul, name):
    s, w = dk.shape
    tm = _tile(s, ROW_T)

    def body(dk_ref, dv_ref, c_ref, s_ref, o_ref):
        cv, sv = c_ref[...], s_ref[...]
        for hb in range(N_HEADS):
            lo, hi = hb * HEAD_PAD, (hb + 1) * HEAD_PAD
            o_ref[:, lo:hi] = _rope_block(dk_ref[:, lo:hi].astype(F32), cv, sv).astype(o_ref.dtype)
        o_ref[:, w:2 * w] = dv_ref[...].astype(o_ref.dtype)

    row = pl.BlockSpec((tm, w), lambda i: (i, 0))
    tab = pl.BlockSpec((tm, HEAD_PAD), lambda i: (i, 0))
    return pl.pallas_call(
        body, name=name, grid=(s // tm,),
        in_specs=[row, row, tab, tab],
        out_specs=pl.BlockSpec((tm, 2 * w), lambda i: (i, 0)),
        out_shape=jax.ShapeDtypeStruct((s, 2 * w), BF16),
        compiler_params=_params(("parallel",)),
    )(dk, dv, cmul, smul)


def _lanes(col, width):
    if col.shape[1] == 1:
        col = jnp.broadcast_to(col, (col.shape[0], HEAD_PAD))
    return jnp.tile(col, (1, width // HEAD_PAD))


def _fold_lanes(a):
    out = a[:, 0:HEAD_PAD]
    for g in range(1, a.shape[1] // HEAD_PAD):
        out = out + a[:, g * HEAD_PAD:(g + 1) * HEAD_PAD]
    return out


def _as_row(rep):
    return rep.T[0:1, :]


def _causal(t, rows_are_queries):
    row = lax.broadcasted_iota(jnp.int32, (t, t), 0)
    col = lax.broadcasted_iota(jnp.int32, (t, t), 1)
    return row >= col if rows_are_queries else col >= row


def _split_refs(refs, n_in, n_out, n_scratch, n_x):
    pos = [n_in, n_x, n_out, n_x, n_scratch, 3 if n_x else 0]
    out, at = [], 0
    for cnt in pos:
        out.append(refs[at:at + cnt])
        at += cnt
    return out


def _first_last_step(n0, n1):
    i0, i1 = pl.program_id(0), pl.program_id(1)
    return jnp.logical_and(i0 == 0, i1 == 0), jnp.logical_and(i0 == n0 - 1, i1 == n1 - 1)


def _attn_fwd(q, k, v, fcol, frow, name, gather=()):
    s = q.shape[0]
    t = ATT_T
    nq = s // t
    use_f = fcol is not None
    nx = len(gather)

    hpb = FWD_HEADS_PER_STEP

    def body(*refs):
        ins, x_src, outs, x_out, scr, x_sems = _split_refs(refs, 5 if use_f else 3, 3, 2, nx)
        if use_f:
            q_ref, k_ref, v_ref, fc_ref, fr_ref = ins
            fc_b = [jnp.broadcast_to(fc_ref[hh], (t, HEAD_PAD)) for hh in range(hpb)]
        else:
            q_ref, k_ref, v_ref = ins
        o_ref, ob_ref, lse_ref = outs
        m_s, acc_s = scr
        if nx:
            first, last = _first_last_step(N_HEADS // hpb, nq)
            x_start, x_wait = _direct_exchange(x_src, x_out, *x_sems, scatter=False)
            pl.when(first)(x_start)
        qi = pl.program_id(1)
        m_s[...] = jnp.full(m_s.shape, NEG_BIG, F32)
        acc_s[...] = jnp.zeros(acc_s.shape, F32)

        def step(j, masked):
            off = pl.multiple_of(j * t, t)
            for hh in range(hpb):
                lanes = slice(hh * HEAD_PAD, (hh + 1) * HEAD_PAD)
                kv = k_ref[pl.ds(off, t), lanes]
                vv = v_ref[pl.ds(off, t), lanes]
                sc = lax.dot_general(q_ref[:, lanes], kv, (((1,), (1,)), ((), ())), preferred_element_type=F32)
                if use_f:
                    sc = sc + (_lanes(fc_b[hh], t) - fr_ref[hh, j])
                if masked:
                    sc = jnp.where(_causal(t, True), sc, NEG_BIG)
                m_prev = m_s[hh]
                m_new = jnp.maximum(m_prev, jnp.max(sc, axis=-1, keepdims=True))
                p = jnp.exp2(sc - _lanes(m_new, t))
                acc_s[hh] = jnp.exp2(m_prev - m_new) * acc_s[hh] + jnp.dot(p.astype(BF16), vv,
                                                                           preferred_element_type=F32)
                m_s[hh] = m_new

        def loop_body(j, carry):
            step(j, False)
            return carry

        lax.fori_loop(0, qi, loop_body, 0)
        step(qi, True)
        for hh in range(hpb):
            lanes = slice(hh * HEAD_PAD, (hh + 1) * HEAD_PAD)
            acc = acc_s[hh]
            lane = lax.broadcasted_iota(jnp.int32, acc.shape, 1)
            denom = jnp.sum(jnp.where(lane == SUM_LANE, acc, 0.0), axis=-1, keepdims=True)
            o = acc * (1.0 / denom)
            o_ref[:, lanes] = o
            ob_ref[:, lanes] = o.astype(BF16)
            lse_ref[hh, 0] = _as_row(m_s[hh] + jnp.log(denom) * LOG2E)
        if nx:
            pl.when(last)(x_wait)

    w = hpb * HEAD_PAD
    qspec = pl.BlockSpec((t, w), lambda h, i: (i, h))
    kspec = pl.BlockSpec((s, w), lambda h, i: (0, h))
    colspec = pl.BlockSpec((hpb, t, 1), lambda h, i: (h, i, 0))
    any_spec = pl.BlockSpec(memory_space=pl.ANY)
    in_specs, args = [qspec, kspec, kspec], [q, k, v]
    if use_f:
        in_specs += [colspec, pl.BlockSpec((hpb, nq, 1, t), lambda h, i: (h, 0, 0, 0))]
        args += [fcol, frow]
    out_specs = [qspec, qspec, pl.BlockSpec((hpb, 1, 1, t), lambda h, i: (h, i, 0, 0))]
    out_shape = [jax.ShapeDtypeStruct((s, N_HEADS * HEAD_PAD), F32), jax.ShapeDtypeStruct((s, N_HEADS * HEAD_PAD), BF16),
                 jax.ShapeDtypeStruct((N_HEADS, nq, 1, t), F32)]
    scratch = [pltpu.VMEM((hpb, t, HEAD_PAD), F32), pltpu.VMEM((hpb, t, HEAD_PAD), F32)]
    if nx:
        in_specs += [any_spec] * nx
        args += list(gather)
        out_specs += [any_spec] * nx
        out_shape += [jax.ShapeDtypeStruct((N_DEV,) + g.shape, g.dtype) for g in gather]
        scratch += _exchange_scratch(nx)
    return pl.pallas_call(
        body, name=name, grid=(N_HEADS // hpb, nq),
        in_specs=in_specs, out_specs=tuple(out_specs), out_shape=tuple(out_shape),
        scratch_shapes=scratch,
        compiler_params=_params(("arbitrary", "arbitrary") if nx else ("parallel", "arbitrary")),
    )(*args)


def _attn_delta(o, do, name):
    s, w = o.shape
    t = ATT_T

    def body(o_ref, do_ref, d_ref):
        for hb in range(N_HEADS):
            lo, hi = hb * HEAD_PAD, (hb + 1) * HEAD_PAD
            prod = o_ref[:, lo:hi] * do_ref[:, lo:hi].astype(F32)
            d_ref[hb, 0] = jnp.sum(prod.T, axis=0, keepdims=True)

    row = pl.BlockSpec((t, w), lambda i: (i, 0))
    return pl.pallas_call(
        body, name=name, grid=(s // t,),
        in_specs=[row, row],
        out_specs=pl.BlockSpec((N_HEADS, 1, 1, t), lambda i: (0, i, 0, 0)),
        out_shape=jax.ShapeDtypeStruct((N_HEADS, s // t, 1, t), F32),
        compiler_params=_params(("parallel",)),
    )(o, do)


def _attn_bwd(q, k, v, do, lse_row, delta_row, fcol, frow, scale_q, scale_k, name, out_dtype, scatter=()):
    s = q.shape[0]
    t = ATT_T
    nq = s // t
    use_f = fcol is not None
    nx = len(scatter)
    hpb = BWD_HEADS_PER_STEP

    def body(*refs):
        ins, x_src, outs, x_out, scr, x_sems = _split_refs(refs, 8 if use_f else 6, 5 if use_f else 3,
                                                           5 if use_f else 3, nx)
        if use_f:
            q_ref, k_ref, v_ref, do_ref, lse_ref, dl_ref, fc_ref, fr_ref = ins
            dq_ref, dk_ref, dv_ref, dr_ref, df_ref = outs
            dq_s, dk_s, dv_s, dr_s, df_s = scr
            fc_b = [jnp.broadcast_to(fc_ref[hh], (t, HEAD_PAD)) for hh in range(hpb)]
        else:
            q_ref, k_ref, v_ref, do_ref, lse_ref, dl_ref = ins
            dq_ref, dk_ref, dv_ref = outs
            dq_s, dk_s, dv_s = scr
        if nx:
            first, last = _first_last_step(N_HEADS // hpb, nq)
            x_start, x_wait = _direct_exchange(x_src, x_out, *x_sems, scatter=True)
            pl.when(first)(x_start)
        kj = pl.program_id(1)

        @pl.when(kj == 0)
        def _():
            dq_s[...] = jnp.zeros(dq_s.shape, F32)
            if use_f:
                dr_s[...] = jnp.zeros(dr_s.shape, F32)

        dk_s[...] = jnp.zeros(dk_s.shape, F32)
        dv_s[...] = jnp.zeros(dv_s.shape, F32)
        if use_f:
            df_s[...] = jnp.zeros(df_s.shape, F32)

        def step(i, masked):
            off = pl.multiple_of(i * t, t)
            for hh in range(hpb):
                lanes = slice(hh * HEAD_PAD, (hh + 1) * HEAD_PAD)
                kv, vv = k_ref[:, lanes], v_ref[:, lanes]
                qv = q_ref[pl.ds(off, t), lanes]
                dov = do_ref[pl.ds(off, t), lanes]
                st = lax.dot_general(kv, qv, (((1,), (1,)), ((), ())), preferred_element_type=F32)
                if use_f:
                    st = st + (fr_ref[hh, i] - _lanes(fc_b[hh], t))
                if masked:
                    st = jnp.where(_causal(t, False), st, NEG_BIG)
                pt = jnp.exp2(st - lse_ref[hh, i])
                dv_s[hh] += jnp.dot(pt.astype(BF16), dov, preferred_element_type=F32)
                dpt = lax.dot_general(vv, dov, (((1,), (1,)), ((), ())), preferred_element_type=F32)
                dst = pt * (dpt - dl_ref[hh, i])
                dsb = dst.astype(BF16)
                dk_s[hh] += jnp.dot(dsb, qv, preferred_element_type=F32)
                dq_s[hh, pl.ds(off, t), :] += lax.dot_general(dsb, kv, (((0,), (0,)), ((), ())),
                                                              preferred_element_type=F32)
                if use_f:
                    df_s[hh] -= _fold_lanes(dst)
                    dr_s[hh, i] += jnp.sum(dst, axis=0, keepdims=True)

        step(kj, True)

        def loop_body(i, carry):
            step(i, False)
            return carry

        lax.fori_loop(kj + 1, nq, loop_body, 0)
        for hh in range(hpb):
            lanes = slice(hh * HEAD_PAD, (hh + 1) * HEAD_PAD)
            dk_ref[:, lanes] = (dk_s[hh] * scale_k).astype(dk_ref.dtype)
            dv_ref[:, lanes] = dv_s[hh].astype(dv_ref.dtype)
            if use_f:
                df_ref[hh, 0] = jnp.sum(df_s[hh].T, axis=0, keepdims=True)

        @pl.when(kj == nq - 1)
        def _():
            for hh in range(hpb):
                dq_ref[:, hh * HEAD_PAD:(hh + 1) * HEAD_PAD] = (dq_s[hh] * scale_q).astype(dq_ref.dtype)
            if use_f:
                dr_ref[...] = dr_s[...]

        if nx:
            pl.when(last)(x_wait)

    w = hpb * HEAD_PAD
    kspec = pl.BlockSpec((t, w), lambda h, j: (j, h))
    qspec = pl.BlockSpec((s, w), lambda h, j: (0, h))
    rowspec = pl.BlockSpec((hpb, nq, 1, t), lambda h, j: (h, 0, 0, 0))
    any_spec = pl.BlockSpec(memory_space=pl.ANY)
    in_specs, args = [qspec, kspec, kspec, qspec, rowspec, rowspec], [q, k, v, do, lse_row, delta_row]
    full = jax.ShapeDtypeStruct((s, N_HEADS * HEAD_PAD), out_dtype)
    out_specs, out_shape = [qspec, kspec, kspec], [full, full, full]
    scratch = [pltpu.VMEM((hpb, s, HEAD_PAD), F32), pltpu.VMEM((hpb, t, HEAD_PAD), F32),
               pltpu.VMEM((hpb, t, HEAD_PAD), F32)]
    if use_f:
        in_specs += [pl.BlockSpec((hpb, t, 1), lambda h, j: (h, j, 0)), rowspec]
        args += [fcol, frow]
        out_specs += [rowspec, pl.BlockSpec((hpb, 1, 1, t), lambda h, j: (h, j, 0, 0))]
        out_shape += [jax.ShapeDtypeStruct((N_HEADS, nq, 1, t), F32)] * 2
        scratch += [pltpu.VMEM((hpb, nq, 1, t), F32), pltpu.VMEM((hpb, t, HEAD_PAD), F32)]
    if nx:
        in_specs += [any_spec] * nx
        args += list(scatter)
        out_specs += [any_spec] * nx
        out_shape += [jax.ShapeDtypeStruct(g.shape, g.dtype) for g in scatter]
        scratch += _exchange_scratch(nx)
    return pl.pallas_call(
        body, name=name, grid=(N_HEADS // hpb, nq),
        in_specs=in_specs, out_specs=tuple(out_specs), out_shape=tuple(out_shape),
        scratch_shapes=scratch,
        compiler_params=_params(("arbitrary", "arbitrary") if nx else ("parallel", "arbitrary")),
    )(*args)


def _gate_fwd(pm, pf, gates, name):
    s, w = pm.shape
    tm = _tile(s, ROW_T)

    def body(pm_ref, pf_ref, g_ref, y_ref):
        y = (jax.nn.sigmoid(g_ref[:, 0:w]) * pm_ref[...].astype(F32)
             + jax.nn.sigmoid(g_ref[:, w:2 * w]) * pf_ref[...].astype(F32))
        y_ref[...] = y.astype(y_ref.dtype)

    row = pl.BlockSpec((tm, w), lambda i: (i, 0))
    return pl.pallas_call(
        body, name=name, grid=(s // tm,),
        in_specs=[row, row, pl.BlockSpec((tm, 2 * w), lambda i: (i, 0))],
        out_specs=row, out_shape=jax.ShapeDtypeStruct((s, w), BF16),
        compiler_params=_params(("parallel",)),
    )(pm, pf, gates)


def _gate_bwd(dy, pm, pf, gates, name):
    s, w = pm.shape
    tm = _tile(s, ROW_T)

    def body(dy_ref, pm_ref, pf_ref, g_ref, dpm_ref, dpf_ref, dg_ref):
        dyv = dy_ref[...]
        sm, sf = jax.nn.sigmoid(g_ref[:, 0:w]), jax.nn.sigmoid(g_ref[:, w:2 * w])
        dpm_ref[...] = (dyv * sm).astype(BF16)
        dpf_ref[...] = (dyv * sf).astype(BF16)
        dg_ref[:, 0:w] = (dyv * pm_ref[...].astype(F32) * (sm * (1.0 - sm))).astype(BF16)
        dg_ref[:, w:2 * w] = (dyv * pf_ref[...].astype(F32) * (sf * (1.0 - sf))).astype(BF16)

    row = pl.BlockSpec((tm, w), lambda i: (i, 0))
    wide = pl.BlockSpec((tm, 2 * w), lambda i: (i, 0))
    out = jax.ShapeDtypeStruct((s, w), BF16)
    return pl.pallas_call(
        body, name=name, grid=(s // tm,),
        in_specs=[row, row, row, wide],
        out_specs=(row, row, wide), out_shape=(out, out, jax.ShapeDtypeStruct((s, 2 * w), BF16)),
        compiler_params=_params(("parallel",)),
    )(dy, pm, pf, gates)


CONV_TN = 256
CONV_TM = 512
HALO = BF16_ROWS


def _shift_down(u, prev, n):
    rolled = pltpu.roll(u, n, 0)
    prev_rolled = pltpu.roll(prev, n, 0)
    top = jnp.concatenate([prev_rolled, rolled[HALO:]], axis=0)
    row = lax.broadcasted_iota(jnp.int32, u.shape, 0)
    return jnp.where(row < n, top, rolled)


def _conv_tile(u, prev, w_ref, b_ref):
    um1 = _shift_down(u, prev, 1)
    um2 = _shift_down(u, prev, 2)
    uc = b_ref[...] + w_ref[0:1, :] * um2 + w_ref[1:2, :] * um1 + w_ref[2:3, :] * u
    return uc, um1, um2


def _conv_specs(tm, tn, ncol_off):
    blk = lambda off: pl.BlockSpec((tm, tn), lambda j, i: (i, j + off))
    halo = lambda off: pl.BlockSpec((HALO, tn), lambda j, i: (jnp.maximum(i * (tm // HALO) - 1, 0), j + off))
    wsp = lambda off: pl.BlockSpec((3, tn), lambda j, i: (0, j + off))
    bsp = lambda off: pl.BlockSpec((1, tn), lambda j, i: (0, j + off))
    return blk, halo, wsp, bsp


def _convglu_fwd(u, conv_w, conv_b, name):
    s = u.shape[0]
    tm, tn = _tile(s, CONV_TM), CONV_TN
    nj = D_FF // tn
    blk, halo, wsp, bsp = _conv_specs(tm, tn, nj)

    def body(ug_ref, pg_ref, uv_ref, pv_ref, wg_ref, wv_ref, bg_ref, bv_ref, a_ref):
        live = (pl.program_id(1) > 0).astype(F32)
        gate, _, _ = _conv_tile(ug_ref[...].astype(F32), pg_ref[...].astype(F32) * live, wg_ref, bg_ref)
        val, _, _ = _conv_tile(uv_ref[...].astype(F32), pv_ref[...].astype(F32) * live, wv_ref, bv_ref)
        a_ref[...] = (gate * jax.nn.sigmoid(gate) * val).astype(a_ref.dtype)

    return pl.pallas_call(
        body, name=name, grid=(nj, s // tm),
        in_specs=[blk(0), halo(0), blk(nj), halo(nj), wsp(0), wsp(nj), bsp(0), bsp(nj)],
        out_specs=blk(0), out_shape=jax.ShapeDtypeStruct((s, D_FF), BF16),
        compiler_params=_params(("parallel", "arbitrary")),
    )(u, u, u, u, conv_w, conv_w, conv_b, conv_b)


def _convglu_bwd(da, u, conv_w, conv_b, name):
    s = u.shape[0]
    tm, tn = _tile(s, CONV_TM), CONV_TN
    nj = D_FF // tn
    blk, halo, wsp, bsp = _conv_specs(tm, tn, nj)

    def body(da_ref, ug_ref, pg_ref, uv_ref, pv_ref, wg_ref, wv_ref, bg_ref, bv_ref,
             dg_ref, dv_ref, sg_ref, sv_ref):
        live = (pl.program_id(1) > 0).astype(F32)
        ug, uv = ug_ref[...].astype(F32), uv_ref[...].astype(F32)
        gate, ug1, ug2 = _conv_tile(ug, pg_ref[...].astype(F32) * live, wg_ref, bg_ref)
        val, uv1, uv2 = _conv_tile(uv, pv_ref[...].astype(F32) * live, wv_ref, bv_ref)
        dav = da_ref[...].astype(F32)
        sig = jax.nn.sigmoid(gate)
        dgate = dav * val * (sig * (1.0 + gate * (1.0 - sig)))
        dval = dav * (gate * sig)
        dg_ref[...] = dgate.astype(dg_ref.dtype)
        dv_ref[...] = dval.astype(dv_ref.dtype)

        @pl.when(pl.program_id(1) == 0)
        def _():
            sg_ref[...] = jnp.zeros_like(sg_ref)
            sv_ref[...] = jnp.zeros_like(sv_ref)

        for s_ref, d, taps in ((sg_ref, dgate, (ug2, ug1, ug)), (sv_ref, dval, (uv2, uv1, uv))):
            for r, tap in enumerate(taps):
                s_ref[r:r + 1, :] += jnp.sum(d * tap, axis=0, keepdims=True)
            s_ref[3:4, :] += jnp.sum(d, axis=0, keepdims=True)

    sums = lambda off: pl.BlockSpec((8, tn), lambda j, i: (0, j + off))
    return pl.pallas_call(
        body, name=name, grid=(nj, s // tm),
        in_specs=[blk(0), blk(0), halo(0), blk(nj), halo(nj), wsp(0), wsp(nj), bsp(0), bsp(nj)],
        out_specs=(blk(0), blk(0), sums(0), sums(0)),
        out_shape=(jax.ShapeDtypeStruct((s, D_FF), BF16), jax.ShapeDtypeStruct((s, D_FF), BF16),
                   jax.ShapeDtypeStruct((8, D_FF), F32), jax.ShapeDtypeStruct((8, D_FF), F32)),
        compiler_params=_params(("parallel", "arbitrary")),
    )(da, u, u, u, u, conv_w, conv_w, conv_b, conv_b)


def _conv_transpose(d, conv_w, name, col0, into=None):
    s, w = d.shape
    tm, tn = _tile(s, CONV_TM), CONV_TN
    last = s // tm - 1
    jo = col0 // tn

    def body(d_ref, nx_ref, w_ref, *rest):
        o_ref = rest[-1]
        dv = d_ref[...].astype(F32)
        nxt = nx_ref[...].astype(F32) * (pl.program_id(1) < last).astype(F32)
        row = lax.broadcasted_iota(jnp.int32, dv.shape, 0)

        def shift_up(n):
            rolled = pltpu.roll(dv, tm - n, 0)
            nxt_rolled = pltpu.roll(nxt, HALO - n, 0)
            bottom = jnp.concatenate([rolled[:tm - HALO], nxt_rolled], axis=0)
            return jnp.where(row >= tm - n, bottom, rolled)

        out = w_ref[2:3, :] * dv + w_ref[1:2, :] * shift_up(1) + w_ref[0:1, :] * shift_up(2)
        o_ref[...] = out.astype(o_ref.dtype)

    blk = pl.BlockSpec((tm, tn), lambda j, i: (i, j))
    nxt_spec = pl.BlockSpec((HALO, tn), lambda j, i: (jnp.minimum((i + 1) * (tm // HALO), s // HALO - 1), j))
    in_specs = [blk, nxt_spec, pl.BlockSpec((3, tn), lambda j, i: (0, j + jo))]
    args = [d, d, conv_w]
    if into is not None:
        in_specs.append(pl.BlockSpec(memory_space=pl.ANY))
        args.append(into)
    return pl.pallas_call(
        body, name=name, grid=(w // tn, s // tm),
        in_specs=in_specs,
        out_specs=pl.BlockSpec((tm, tn), lambda j, i: (i, j + jo)),
        out_shape=jax.ShapeDtypeStruct((s, 2 * D_FF), BF16),
        input_output_aliases={3: 0} if into is not None else {},
        compiler_params=_params(("parallel", "arbitrary")),
    )(*args)


def _split3(a):
    a1 = a.astype(BF16)
    r1 = a - a1.astype(F32)
    a2 = r1.astype(BF16)
    a3 = (r1 - a2.astype(F32)).astype(BF16)
    return a1, a2, a3


def _ones_dot_right(a, mat):
    return sum(jnp.dot(p, mat, preferred_element_type=F32) for p in _split3(a))


def _ones_dot_left(mat, a):
    return sum(jnp.dot(mat, p, preferred_element_type=F32) for p in _split3(a))


def _tri(n, cmp):
    r = lax.broadcasted_iota(jnp.int32, (n, n), 0)
    c = lax.broadcasted_iota(jnp.int32, (n, n), 1)
    return cmp(r, c).astype(BF16)


def _forget_fwd(z, bias, name):
    nh, nr, nl = z.shape

    def body(z_ref, b_ref, f_ref):
        within = _tri(nl, lambda r, c: r <= c)
        before = _tri(nr, lambda r, c: c < r)
        for h in range(nh):
            x = z_ref[h] + b_ref[h]
            lf = jnp.minimum(x, 0.0) - jnp.log(1.0 + jnp.exp(-jnp.abs(x)))
            pre = _ones_dot_right(lf, within)
            tot = jnp.zeros((nr, nl), F32) + jnp.sum(lf, axis=1, keepdims=True)
            f_ref[h] = pre + _ones_dot_left(before, tot)

    return pl.pallas_call(
        body, name=name, out_shape=jax.ShapeDtypeStruct(z.shape, F32),
        compiler_params=pltpu.CompilerParams(vmem_limit_bytes=VMEM_LIMIT_BYTES),
    )(z, bias)


def _forget_bwd(df_rows, df_cols, z, bias, name):
    nh, nr, nl = z.shape

    def body(dfr_ref, dfc_ref, z_ref, b_ref, dz_ref, db_ref):
        within = _tri(nl, lambda r, c: r >= c)
        after = _tri(nr, lambda r, c: c > r)
        for h in range(nh):
            g = dfr_ref[h] + dfc_ref[h]
            suf = _ones_dot_right(g, within)
            tot = jnp.zeros((nr, nl), F32) + jnp.sum(g, axis=1, keepdims=True)
            dlf = suf + _ones_dot_left(after, tot)
            dz = dlf * jax.nn.sigmoid(-(z_ref[h] + b_ref[h]))
            dz_ref[h] = dz
            db_ref[h] = jnp.zeros((1, nl), F32) + jnp.sum(dz)

    return pl.pallas_call(
        body, name=name,
        out_shape=(jax.ShapeDtypeStruct(z.shape, F32), jax.ShapeDtypeStruct(bias.shape, F32)),
        compiler_params=pltpu.CompilerParams(vmem_limit_bytes=VMEM_LIMIT_BYTES),
    )(df_rows, df_cols, z, bias)


def _ada_fwd(c_col, w, b, name):
    kdim, n = w.shape

    def body(c_ref, w_ref, b_ref, ada_ref, act_ref):
        wv = w_ref[...]
        for e in range(N_DEV):
            cv = c_ref[e]
            act = cv * jax.nn.sigmoid(cv)
            act_ref[e] = act
            ada_ref[e:e + 1, :] = jnp.sum(act * wv, axis=0, keepdims=True) + b_ref[...]

    return pl.pallas_call(
        body, name=name,
        out_shape=(jax.ShapeDtypeStruct((N_DEV, n), F32), jax.ShapeDtypeStruct((N_DEV, kdim, 1), F32)),
        compiler_params=pltpu.CompilerParams(vmem_limit_bytes=VMEM_LIMIT_BYTES),
    )(c_col, w, b)


def _ada_bwd(act_col, dada, name):
    kdim = act_col.shape[1]
    n = dada.shape[1]

    def body(act_ref, d_ref, g_ref):
        acc = act_ref[0] * d_ref[0:1, :]
        for e in range(1, N_DEV):
            acc = acc + act_ref[e] * d_ref[e:e + 1, :]
        g_ref[...] = acc

    return pl.pallas_call(
        body, name=name, out_shape=jax.ShapeDtypeStruct((kdim, n), F32),
        compiler_params=pltpu.CompilerParams(vmem_limit_bytes=VMEM_LIMIT_BYTES),
    )(act_col, dada)


def _adamw(parts, w, m, v, name, tr=128):
    npart, r, c = parts.shape
    tr = _tile(r, tr, step=BF16_ROWS) if r % BF16_ROWS == 0 else r

    def body(p_ref, w_ref, m_ref, v_ref, g_ref, d_ref, nm_ref, nv_ref):
        g = p_ref[0].astype(F32)
        for e in range(1, npart):
            g = g + p_ref[e].astype(F32)
        nm = ADAM_B1 * m_ref[...] + (1.0 - ADAM_B1) * g
        nv = ADAM_B2 * v_ref[...] + (1.0 - ADAM_B2) * (g * g)
        m_hat = nm / (1.0 - ADAM_B1 ** ADAM_STEP)
        v_hat = nv / (1.0 - ADAM_B2 ** ADAM_STEP)
        g_ref[...] = g
        d_ref[...] = -ADAM_LR * (m_hat / (jnp.sqrt(v_hat) + ADAM_EPS) + ADAM_WD * w_ref[...])
        nm_ref[...] = nm
        nv_ref[...] = nv

    row = pl.BlockSpec((tr, c), lambda i: (i, 0))
    out = jax.ShapeDtypeStruct((r, c), F32)
    return pl.pallas_call(
        body, name=name, grid=(r // tr,),
        in_specs=[pl.BlockSpec((npart, tr, c), lambda i: (0, i, 0)), row, row, row],
        out_specs=(row, row, row, row), out_shape=(out, out, out, out),
        compiler_params=_params(("parallel",)),
    )(parts, w, m, v)


EARLY = ("w_in", "w_uq", "w_ukv")
LATE = ("w_o_mla", "w_o_fox", "w_out", "w_up", "conv_w", "w_down")
BIG = EARLY + LATE


def _cols_to_full(stack):
    n, r, c = stack.shape
    return stack.transpose(1, 0, 2).reshape(r, n * c)


def _full_to_cols(full, c):
    r = full.shape[0]
    return full.reshape(r, N_DEV, c).transpose(1, 0, 2)


def _pad_heads(a, width, ones_lane=False):
    s = a.shape[0]
    a = a.reshape(s, N_HEADS, width)
    if ones_lane:
        assert width == SUM_LANE
        tail = jnp.zeros((s, N_HEADS, HEAD_PAD - width), a.dtype).at[:, :, 0].set(1.0)
        return jnp.concatenate([a, tail], axis=2).reshape(s, N_HEADS * HEAD_PAD)
    return jnp.pad(a, ((0, 0), (0, 0), (0, HEAD_PAD - width))).reshape(s, N_HEADS * HEAD_PAD)


def _unpad_heads(a, width):
    s = a.shape[0]
    return a.reshape(s, N_HEADS, HEAD_PAD)[:, :, :width].reshape(s, N_HEADS * width)


def _w_in_padded(w_in):
    seg = [w_in[:, IN_OFF[i]:IN_OFF[i + 1]] for i in range(9)]
    cq, ckv, kr, fq, fk, fv, fl, gm, gf = seg
    padc = lambda a, n: jnp.pad(a, ((0, 0), (0, n - a.shape[1])))
    return jnp.concatenate([gm, gf, fq, fk, fv, cq, ckv, padc(kr, 128), padc(fl, 128)], axis=1)


def _w_in_unpadded(g):
    return jnp.concatenate([
        g[:, P_CQ:P_CQ + 384], g[:, P_CKV:P_CKV + 256], g[:, P_KR:P_KR + 32], g[:, P_FQ:P_FQ + 512],
        g[:, P_FK:P_FK + 512], g[:, P_FV:P_FV + 512], g[:, P_FL:P_FL + 8], g[:, P_GM:P_GM + 1024],
        g[:, P_GF:P_GF + 1024]], axis=1)


SMALL = (("b_ada", 6144, 6144), ("norm_mix_g", 1024, 1024), ("q_norm_g", 384, 384), ("kv_norm_g", 256, 256),
         ("b_forget", 8, 128), ("norm_ffn_g", 1024, 1024), ("conv_b", 5632, 5632), ("norm_final_g", 1024, 1024),
         ("loss", 1, 128))
SMALL_OFF = {}
_o = 0
for _n, _real, _padded in SMALL:
    SMALL_OFF[_n] = _o
    _o += _padded
SMALL_W = _o


def _pack_small(vals):
    parts = []
    for nme, real, padded in SMALL:
        a = vals[nme].reshape(1, real).astype(F32)
        parts.append(jnp.pad(a, ((0, 0), (0, padded - real))))
    return jnp.concatenate(parts, axis=1)


def kernel(x, c, positions, w_ada, b_ada, norm_mix_g, w_in, q_norm_g, w_uq, kv_norm_g, w_ukv, b_forget, w_o_mla, w_o_fox, w_out, norm_ffn_g, w_up, conv_w, conv_b, w_down, norm_final_g, loss_target, m_w_ada, m_b_ada, m_norm_mix_g, m_w_in, m_q_norm_g, m_w_uq, m_kv_norm_g, m_w_ukv, m_b_forget, m_w_o_mla, m_w_o_fox, m_w_out, m_norm_ffn_g, m_w_up, m_conv_w, m_conv_b, m_w_down, m_norm_final_g, v_w_ada, v_b_ada, v_norm_mix_g, v_w_in, v_q_norm_g, v_w_uq, v_kv_norm_g, v_w_ukv, v_b_forget, v_w_o_mla, v_w_o_fox, v_w_out, v_norm_ffn_g, v_w_up, v_conv_w, v_conv_b, v_w_down, v_norm_final_g):
    me = 4 * lax.axis_index("x") + 2 * lax.axis_index("y") + lax.axis_index("c")
    x = x[0]
    target = loss_target[0]
    s = x.shape[0]
    nblk = s // ATT_T
    big_w = {"w_in": w_in, "w_uq": w_uq, "w_ukv": w_ukv, "w_o_mla": w_o_mla, "w_o_fox": w_o_fox,
             "w_out": w_out, "w_up": w_up, "conv_w": conv_w, "w_down": w_down}
    big_m = {"w_in": m_w_in, "w_uq": m_w_uq, "w_ukv": m_w_ukv, "w_o_mla": m_w_o_mla, "w_o_fox": m_w_o_fox,
             "w_out": m_w_out, "w_up": m_w_up, "conv_w": m_conv_w, "w_down": m_w_down}
    big_v = {"w_in": v_w_in, "w_uq": v_w_uq, "w_ukv": v_w_ukv, "w_o_mla": v_w_o_mla, "w_o_fox": v_w_o_fox,
             "w_out": v_w_out, "w_up": v_w_up, "conv_w": v_conv_w, "w_down": v_w_down}

    shard = lambda k: big_w[k][0] if k == "conv_w" else big_w[k][0].astype(BF16)
    st = dict(zip(EARLY, _all_gather([shard(k) for k in EARLY], "gather_weights")))
    w_in_p = _w_in_padded(_cols_to_full(st["w_in"]))
    uq = st["w_uq"]
    w_uq_p = jnp.pad(uq, ((0, 0), (0, 0), (0, HEAD_PAD - 96))).transpose(1, 0, 2).reshape(MLA_Q_RANK, 1024)
    ukv = st["w_ukv"]
    zeros64 = jnp.zeros((N_HEADS, MLA_KV_RANK, 64), BF16)
    w_uk_p = jnp.concatenate([ukv[:, :, :64], zeros64], axis=2).transpose(1, 0, 2).reshape(MLA_KV_RANK, 1024)
    w_uv_p = jnp.concatenate([ukv[:, :, 64:], zeros64], axis=2).transpose(1, 0, 2).reshape(MLA_KV_RANK, 1024)
    place = np.zeros((HEAD_PAD, N_HEADS, HEAD_PAD), np.float32)
    for j in range(MLA_ROPE):
        place[j, :, MLA_NOPE + j] = 1.0
    place = jnp.asarray(place.reshape(HEAD_PAD, 1024), BF16)
    w_kv_comb = jnp.concatenate([
        jnp.concatenate([w_uk_p, w_uv_p], axis=1),
        jnp.concatenate([place, jnp.zeros((HEAD_PAD, 1024), BF16)], axis=1)], axis=0)

    (c_all,) = _all_gather([c], "gather_c")
    b_ada_mine = lax.dynamic_slice(b_ada, (0, me * 768), (1, 768))
    ada_cols, act_col = _ada_fwd(c_all.reshape(N_DEV, D_MODEL, 1), w_ada[0], b_ada_mine, "ada_fwd")
    (ada_all,) = _all_gather([ada_cols], "gather_ada")
    ada = lax.dynamic_slice(ada_all, (0, me, 0), (N_DEV, 1, 768)).reshape(1, N_ADA * D_MODEL)
    sh_m, sc_m, g_m, sh_f, sc_f, g_f = [ada[:, i * D_MODEL:(i + 1) * D_MODEL] for i in range(N_ADA)]

    inv_freq = ROPE_THETA ** (-jnp.arange(0, MLA_ROPE, 2, dtype=F32) / MLA_ROPE)
    ang = positions[0].astype(F32)[:, None] * inv_freq
    cos, sin = jnp.cos(ang), jnp.sin(ang)
    rope_c = jnp.concatenate([jnp.ones((s, 64), F32), cos, cos, jnp.zeros((s, 32), F32)], axis=1)
    rope_s = jnp.concatenate([jnp.zeros((s, 64), F32), -sin, sin, jnp.zeros((s, 32), F32)], axis=1)

    zero_d = jnp.zeros((1, D_MODEL), F32)

    h1 = _rms_mod(x, norm_mix_g, sc_m, sh_m, "norm_mix")
    gates = _mm(h1, w_in_p[:, P_GM:P_FQ], "nn", F32, "proj_gates", tn=1024)
    fox_fold = jnp.concatenate([jnp.full((1, 512), FOX_SCALE * LOG2E, F32), jnp.ones((1, 1024), F32)], axis=1)
    fqkv = _mm(h1, w_in_p[:, P_FQ:P_CQ], "nn", BF16, "proj_fox", gvec=fox_fold)
    lat = _mm(h1, w_in_p[:, P_CQ:], "nn", F32, "proj_latent", tn=D_IN_P - P_CQ)
    cq = lat[:, 0:384]
    ckv = lat[:, P_CKV - P_CQ:P_CKV - P_CQ + 256]
    qn = _rms_mod(cq, q_norm_g, jnp.zeros((1, 384), F32), jnp.zeros((1, 384), F32), "q_norm")
    kvn = _rms_mod(ckv, kv_norm_g, jnp.zeros((1, 256), F32), jnp.zeros((1, 256), F32), "kv_norm")
    kv_in = jnp.concatenate([kvn, lat[:, P_KR - P_CQ:P_KR - P_CQ + 128].astype(BF16)], axis=1)
    q_pre = _mm(qn, w_uq_p, "nn", F32, "q_up")
    kv_pre = _mm(kv_in, w_kv_comb, "nn", F32, "kv_up")
    q_fold = MLA_SCALE * LOG2E
    (q_att,) = _rope(q_pre, rope_c * q_fold, rope_s * q_fold, "rope_q", N_HEADS, (BF16,))
    k_att, v_att = _rope(kv_pre, rope_c, rope_s, "rope_kv", N_HEADS, (BF16, BF16))
    o_mla, o_mla_b, lse_mla, *late = _attn_fwd(q_att, k_att, v_att, None, None, "mla_fwd",
                                               gather=[shard(k) for k in LATE])
    st.update(zip(LATE, late))
    pad_o = lambda full: jnp.pad(full.reshape(N_HEADS, 64, 1024), ((0, 0), (0, 64), (0, 0))).reshape(1024, 1024)
    w_o_mla_p = pad_o(_cols_to_full(st["w_o_mla"]))
    w_o_fox_p = pad_o(_cols_to_full(st["w_o_fox"]))
    w_out_f = st["w_out"].reshape(1024, 1024)
    w_up_f = _cols_to_full(st["w_up"])
    conv_w_f = _cols_to_full(st["conv_w"])
    w_down_f = st["w_down"].reshape(D_FF, 1024)

    fq = _pad_heads(fqkv[:, 0:512], 64)
    fk = _pad_heads(fqkv[:, 512:1024], 64)
    fv = _pad_heads(fqkv[:, 1024:1536], 64, ones_lane=True)
    z = lat[:, P_FL - P_CQ:P_FL - P_CQ + 8].T.reshape(N_HEADS, s // SEQ_LANES, SEQ_LANES)
    bias_f = jnp.broadcast_to(b_forget.reshape(N_HEADS, 1, 1), (N_HEADS, 1, SEQ_LANES))
    f_cum = _forget_fwd(z, bias_f, "forget_fwd")
    f_col = (f_cum * LOG2E).reshape(N_HEADS, s, 1)
    f_row = f_col.reshape(N_HEADS, nblk, 1, ATT_T)
    o_fox, o_fox_b, lse_fox = _attn_fwd(fq, fk, fv, f_col, f_row, "fox_fwd")

    pm = _mm(o_mla_b, w_o_mla_p, "nn", BF16, "o_mla_proj")
    pf = _mm(o_fox_b, w_o_fox_p, "nn", BF16, "o_fox_proj")
    y = _gate_fwd(pm, pf, gates, "gate_fwd")
    x2, mix = _mm(y, w_out_f, "nn", F32, "out_proj", res=x, gvec=g_m)

    h2 = _rms_mod(x2, norm_ffn_g, sc_f, sh_f, "norm_ffn")
    u = _mm(h2, w_up_f, "nn", BF16, "ffn_up")
    a = _convglu_fwd(u, conv_w_f, conv_b, "convglu_fwd")
    x3, ffn = _mm(a, w_down_f, "nn", F32, "ffn_down", res=x2, gvec=g_f, tk=2816)

    dx3, sums_final = _final_loss(x3, target, norm_final_g.reshape(1, D_MODEL), "final_loss")
    dffn, sums_gf = _scale_bwd(dx3, ffn, g_f, "ffn_scale_bwd")
    da = _mm(dffn, w_down_f, "nt", BF16, "ffn_down_dx", tn=1408)
    g_w_down = _mm(a, dffn, "tn", F32, "ffn_down_dw", tm=256, tn=1024, tk=s)
    dgate, dval, s_gate, s_val = _convglu_bwd(da, u, conv_w_f, conv_b, "convglu_bwd")
    du = _conv_transpose(dgate, conv_w_f, "conv_t_gate", 0)
    du = _conv_transpose(dval, conv_w_f, "conv_t_val", D_FF, into=du)
    dh2 = _mm(du, w_up_f, "nt", F32, "ffn_up_dx", tn=512, tk=2 * D_FF)
    g_w_up = _mm(h2, du, "tn", F32, "ffn_up_dw", tn=256, tk=s)
    dx2, sums_ffn = _rms_mod_bwd(dh2, x2, norm_ffn_g, sc_f, dx3, "norm_ffn_bwd")

    dmix, sums_gm = _scale_bwd(dx2, mix, g_m, "mix_scale_bwd")
    dy = _mm(dmix, w_out_f, "nt", F32, "out_proj_dx")
    g_w_out = _mm(y, dmix, "tn", F32, "out_proj_dw", tn=256, tk=s)
    dpm, dpf, dgates = _gate_bwd(dy, pm, pf, gates, "gate_bwd")
    do_mla_b = _mm(dpm, w_o_mla_p, "nt", BF16, "o_mla_dx", tn=1024)
    do_fox_b = _mm(dpf, w_o_fox_p, "nt", BF16, "o_fox_dx", tn=1024)
    g_w_o_mla_p = _mm(o_mla_b, dpm, "tn", F32, "o_mla_dw", tn=256, tk=s)
    g_w_o_fox_p = _mm(o_fox_b, dpf, "tn", F32, "o_fox_dw", tn=256, tk=s)

    unpad_o = lambda g: g.reshape(N_HEADS, HEAD_PAD, 1024)[:, :64].reshape(512, 1024)
    g_conv_w = jnp.concatenate([s_gate[0:3], s_val[0:3]], axis=1)
    g_blocks = {
        "w_o_mla": _full_to_cols(unpad_o(g_w_o_mla_p), 128), "w_o_fox": _full_to_cols(unpad_o(g_w_o_fox_p), 128),
        "w_out": g_w_out.reshape(N_DEV, 128, 1024), "w_up": _full_to_cols(g_w_up, 704),
        "conv_w": _full_to_cols(g_conv_w, 704), "w_down": g_w_down.reshape(N_DEV, 352, 1024)}

    delta_mla = _attn_delta(o_mla, do_mla_b, "mla_delta")
    dq_rot, dk_rot, dv_mla, *late_recv = _attn_bwd(
        q_att, k_att, v_att, do_mla_b, lse_mla, delta_mla, None, None, MLA_SCALE, 1.0 / LOG2E, "mla_bwd", BF16,
        scatter=[g_blocks[k].astype(BF16) for k in LATE])
    (dq_pre,) = _rope(dq_rot, rope_c, -rope_s, "rope_q_bwd", N_HEADS, (BF16,))
    dkv_pre = _rope_bwd_kv(dk_rot, dv_mla, rope_c, -rope_s, "rope_kv_bwd")
    dqn = _mm(dq_pre, w_uq_p, "nt", F32, "q_up_dx")
    g_w_uq_p = _mm(qn, dq_pre, "tn", F32, "q_up_dw", tk=s)
    dkv_in = _mm(dkv_pre, w_kv_comb, "nt", F32, "kv_up_dx")
    g_w_kv_comb = _mm(kv_in, dkv_pre, "tn", F32, "kv_up_dw", tk=s)
    dcq, sums_q = _rms_mod_bwd(dqn, cq, q_norm_g, jnp.zeros((1, 384), F32), None, "q_norm_bwd")
    dckv, sums_kv = _rms_mod_bwd(dkv_in[:, :256], ckv, kv_norm_g, jnp.zeros((1, 256), F32), None, "kv_norm_bwd")
    delta_fox = _attn_delta(o_fox, do_fox_b, "fox_delta")
    dfq, dfk, dfv, dfr, dfc = _attn_bwd(fq, fk, fv, do_fox_b, lse_fox, delta_fox, f_col, f_row,
                                        FOX_SCALE, 1.0 / LOG2E, "fox_bwd", BF16)
    df_rows = dfr.reshape(N_HEADS, s // SEQ_LANES, SEQ_LANES)
    df_cols = dfc.reshape(N_HEADS, s // SEQ_LANES, SEQ_LANES)
    dz, db_f = _forget_bwd(df_rows, df_cols, z, bias_f, "forget_bwd")
    dfl = jnp.pad(dz.reshape(N_HEADS, s).T, ((0, 0), (0, 128 - N_HEADS)))

    dproj = jnp.concatenate([
        dgates, _unpad_heads(dfq, 64), _unpad_heads(dfk, 64), _unpad_heads(dfv, 64),
        dcq.astype(BF16), dckv.astype(BF16), dkv_in[:, 256:384].astype(BF16), dfl.astype(BF16)], axis=1)
    dh1 = _mm(dproj, w_in_p, "nt", F32, "proj_in_dx", tn=512, tk=D_IN_P)
    g_w_in_p = _mm(h1, dproj, "tn", F32, "proj_in_dw", tm=512, tn=640, tk=s)
    grad_x, sums_mix = _rms_mod_bwd(dh1, x, norm_mix_g, sc_m, dx2, "norm_mix_bwd")

    g_w_in = _w_in_unpadded(g_w_in_p)
    g_uq = g_w_uq_p.reshape(MLA_Q_RANK, N_HEADS, HEAD_PAD)[:, :, :96].transpose(1, 0, 2)
    g_uk = g_w_kv_comb[:256, :1024].reshape(256, N_HEADS, HEAD_PAD)[:, :, :64]
    g_uv = g_w_kv_comb[:256, 1024:].reshape(256, N_HEADS, HEAD_PAD)[:, :, :64]
    g_ukv = jnp.concatenate([g_uk, g_uv], axis=2).transpose(1, 0, 2)
    g_blocks.update({"w_in": _full_to_cols(g_w_in, 533), "w_uq": g_uq, "w_ukv": g_ukv})
    early_recv = _all_to_all([g_blocks[k].astype(BF16) for k in EARLY], "scatter_grads")
    g_big, d_big, nm_big, nv_big = {}, {}, {}, {}
    for k, parts in zip(BIG, list(early_recv) + list(late_recv)):
        g_big[k], d_big[k], nm_big[k], nv_big[k] = [
            t[None] for t in _adamw(parts, big_w[k][0], big_m[k][0], big_v[k][0], "adamw_" + k)]

    dada = jnp.concatenate([sums_mix[0:1], sums_mix[1:2], sums_gm[0:1], sums_ffn[0:1], sums_ffn[1:2], sums_gf[0:1]],
                           axis=1)
    small_part = _pack_small({
        "b_ada": dada, "norm_mix_g": sums_mix[2:3], "q_norm_g": sums_q[2:3], "kv_norm_g": sums_kv[2:3],
        "b_forget": db_f[:, 0, 0], "norm_ffn_g": sums_ffn[2:3],
        "conv_b": jnp.concatenate([s_gate[3:4], s_val[3:4]], axis=1), "norm_final_g": sums_final[0:1],
        "loss": sums_final[1:2, 0:1]})
    (small_all,) = _all_gather([small_part], "gather_small")
    zero1 = jnp.zeros((1,), F32)
    small_w = {"b_ada": b_ada, "norm_mix_g": norm_mix_g, "q_norm_g": q_norm_g, "kv_norm_g": kv_norm_g,
               "b_forget": b_forget, "norm_ffn_g": norm_ffn_g, "conv_b": conv_b, "norm_final_g": norm_final_g,
               "loss": zero1}
    small_m = {"b_ada": m_b_ada, "norm_mix_g": m_norm_mix_g, "q_norm_g": m_q_norm_g, "kv_norm_g": m_kv_norm_g,
               "b_forget": m_b_forget, "norm_ffn_g": m_norm_ffn_g, "conv_b": m_conv_b,
               "norm_final_g": m_norm_final_g, "loss": zero1}
    small_v = {"b_ada": v_b_ada, "norm_mix_g": v_norm_mix_g, "q_norm_g": v_q_norm_g, "kv_norm_g": v_kv_norm_g,
               "b_forget": v_b_forget, "norm_ffn_g": v_norm_ffn_g, "conv_b": v_conv_b,
               "norm_final_g": v_norm_final_g, "loss": zero1}
    g_sm, d_sm, nm_sm, nv_sm = _adamw(small_all, _pack_small(small_w), _pack_small(small_m), _pack_small(small_v),
                                      "adamw_small")
    loss = g_sm[0, SMALL_OFF["loss"]]

    dada_all = small_all[:, 0, SMALL_OFF["b_ada"]:SMALL_OFF["b_ada"] + N_ADA * D_MODEL]
    dada_mine = lax.dynamic_slice(dada_all, (0, me * 768), (N_DEV, 768))
    g_ada_local = _ada_bwd(act_col, dada_mine, "ada_bwd")
    g_ada, d_ada, nm_ada, nv_ada = _adamw(g_ada_local[None], w_ada[0], m_w_ada[0], v_w_ada[0], "adamw_ada")

    def small_out(t, nme, shape):
        real = dict((n_, r_) for n_, r_, _ in SMALL)[nme]
        o = SMALL_OFF[nme]
        return t[0, o:o + real].reshape(shape)

    order = ["w_ada", "b_ada", "norm_mix_g", "w_in", "q_norm_g", "w_uq", "kv_norm_g", "w_ukv", "b_forget",
             "w_o_mla", "w_o_fox", "w_out", "norm_ffn_g", "w_up", "conv_w", "conv_b", "w_down", "norm_final_g"]
    small_shapes = {"b_ada": (1, 6144), "norm_mix_g": (1, 1024), "q_norm_g": (1, 384), "kv_norm_g": (1, 256),
                    "b_forget": (1, 8), "norm_ffn_g": (1, 1024), "conv_b": (1, 5632), "norm_final_g": (1024,)}

    def family(big, small, ada_t):
        out = []
        for nme in order:
            if nme == "w_ada":
                out.append(ada_t[None])
            elif nme in small_shapes:
                out.append(small_out(small, nme, small_shapes[nme]))
            else:
                out.append(big[nme])
        return out

    return (loss, grad_x[None], *family(g_big, g_sm, g_ada), *family(d_big, d_sm, d_ada),
            *family(nm_big, nm_sm, nm_ada), *family(nv_big, nv_sm, nv_ada))
```

```python
import math

import numpy as np
import jax
import jax.numpy as jnp
from jax import lax
from jax.experimental import pallas as pl
from jax.experimental.pallas import tpu as pltpu

F32 = jnp.float32
BF16 = jnp.bfloat16

N_DEV = 8
D_MODEL = 1024
N_HEADS = 8
HEAD_PAD = 128
MLA_Q_RANK = 384
MLA_KV_RANK = 256
MLA_NOPE = 64
MLA_ROPE = 32
MLA_V = 64
FOX_DIM = 64
D_FF = 2816
N_ADA = 6
EPS = 1e-6
ROPE_THETA = 10000.0
MLA_SCALE = 1.0 / math.sqrt(MLA_NOPE + MLA_ROPE)
FOX_SCALE = 1.0 / math.sqrt(FOX_DIM)
IN_SPLITS = (384, 256, 32, 512, 512, 512, 8, 1024, 1024)
D_IN = sum(IN_SPLITS)
IN_OFF = tuple(int(v) for v in np.cumsum((0,) + IN_SPLITS))
P_GM, P_GF, P_FQ, P_FK, P_FV, P_CQ, P_CKV, P_KR, P_FL, D_IN_P = 0, 1024, 2048, 2560, 3072, 3584, 3968, 4224, 4352, 4480

ADAM_LR, ADAM_B1, ADAM_B2, ADAM_EPS, ADAM_WD, ADAM_STEP = 0.001, 0.9, 0.999, 1e-08, 0.01, 10

VMEM_LIMIT_BYTES = 56 * 1024 * 1024
NEG_BIG = -1e30
ATT_T = 512
LOG2E = 1.4426950408889634
SUM_LANE = 64
ROW_T = 256
SEQ_LANES = 128
BF16_ROWS = 16
FWD_HEADS_PER_STEP = 4
BWD_HEADS_PER_STEP = 2


def _params(sem):
    return pltpu.CompilerParams(dimension_semantics=sem, vmem_limit_bytes=VMEM_LIMIT_BYTES)


def _tile(n, target, step=128):
    if n <= target:
        return n
    t = (target // step) * step
    while t >= step:
        if n % t == 0:
            return t
        t -= step
    return n


def _vec_spec(w, nargs):
    if nargs == 1:
        return pl.BlockSpec((1, w), lambda i: (0, 0))
    return pl.BlockSpec((1, w), lambda i, j: (0, 0))


def _comm_call(body, name, ins, out_shapes):
    n = len(ins)
    any_spec = pl.BlockSpec(memory_space=pl.ANY)
    return pl.pallas_call(
        body, name=name, out_shape=tuple(out_shapes),
        in_specs=[any_spec] * n, out_specs=tuple([any_spec] * n),
        scratch_shapes=[pltpu.SemaphoreType.DMA((n, 7)), pltpu.SemaphoreType.DMA((n, 7)),
                        pltpu.SemaphoreType.DMA((n,))],
    )(*ins)


def _all_gather(xs, name):
    n = len(xs)

    def body(*refs):
        x_refs, out_refs = refs[:n], refs[n:2 * n]
        send_sems, recv_sems, local_sems = refs[2 * n:]
        x_, y_, c_ = lax.axis_index("x"), lax.axis_index("y"), lax.axis_index("c")
        me, sibling = (x_, y_, c_), (x_, y_, 1 - c_)
        chips = [(1 - x_, y_), (x_, 1 - y_), (1 - x_, 1 - y_)]

        def slot(a, px, py, pc):
            return out_refs[a].at[4 * px + 2 * py + pc]

        def copy(a, k, block, to, src=None):
            return pltpu.make_async_remote_copy(
                src_ref=slot(a, *block) if src is None else src, dst_ref=slot(a, *block),
                send_sem=send_sems.at[a, k], recv_sem=recv_sems.at[a, k],
                device_id=to, device_id_type=pl.DeviceIdType.MESH)

        mine = [pltpu.make_async_copy(x_refs[a], slot(a, *me), local_sems.at[a]) for a in range(n)]
        for cp in mine:
            cp.start()
        first = []
        for a in range(n):
            first.append(copy(a, 0, me, sibling, src=x_refs[a]))
            first += [copy(a, 1 + j, me, (*chip, c_), src=x_refs[a]) for j, chip in enumerate(chips)]
        for cp in first:
            cp.start()
        passed = []
        for j, chip in enumerate(chips):
            for a in range(n):
                copy(a, 1 + j, (*chip, c_), me).wait_recv()
                passed.append(copy(a, 4 + j, (*chip, c_), sibling))
                passed[-1].start()
        for a in range(n):
            copy(a, 0, sibling, me).wait_recv()
            for j, chip in enumerate(chips):
                copy(a, 4 + j, (*chip, 1 - c_), me).wait_recv()
        for cp in first + passed:
            cp.wait_send()
        for cp in mine:
            cp.wait()

    return _comm_call(body, name, xs, [jax.ShapeDtypeStruct((N_DEV,) + x.shape, x.dtype) for x in xs])


def _direct_exchange(src_refs, out_refs, send_sems, recv_sems, local_sems, scatter):
    n = len(src_refs)
    x_, y_, c_ = lax.axis_index("x"), lax.axis_index("y"), lax.axis_index("c")
    me = 4 * x_ + 2 * y_ + c_

    def peer(k):
        return (x_ ^ ((k >> 2) & 1), y_ ^ ((k >> 1) & 1), c_ ^ (k & 1))

    def src(a, slot):
        return src_refs[a].at[slot] if scatter else src_refs[a]

    def copy(a, k, sending):
        px, py, pc = peer(k)
        theirs = 4 * px + 2 * py + pc
        return pltpu.make_async_remote_copy(
            src_ref=src(a, theirs if sending else me), dst_ref=out_refs[a].at[me if sending else theirs],
            send_sem=send_sems.at[a, k - 1], recv_sem=recv_sems.at[a, k - 1],
            device_id=(px, py, pc), device_id_type=pl.DeviceIdType.MESH)

    mine = [pltpu.make_async_copy(src(a, me), out_refs[a].at[me], local_sems.at[a]) for a in range(n)]
    sends = [copy(a, k, True) for a in range(n) for k in range(1, N_DEV)]

    def start():
        for cp in mine + sends:
            cp.start()

    def wait():
        for a in range(n):
            for k in range(1, N_DEV):
                copy(a, k, False).wait_recv()
        for cp in sends:
            cp.wait_send()
        for cp in mine:
            cp.wait()

    return start, wait


def _exchange_scratch(n):
    return [pltpu.SemaphoreType.DMA((n, 7)), pltpu.SemaphoreType.DMA((n, 7)), pltpu.SemaphoreType.DMA((n,))]


def _all_to_all(gs, name):
    n = len(gs)

    def body(*refs):
        start, wait = _direct_exchange(refs[:n], refs[n:2 * n], *refs[2 * n:], scatter=True)
        start()
        wait()

    return _comm_call(body, name, gs, [jax.ShapeDtypeStruct(g.shape, g.dtype) for g in gs])


def _mm(a, b, mode, out_dtype, name, res=None, gvec=None, tm=1024, tn=512, tk=1024, rope=None, ones_lane=False):
    (k, m) = a.shape if mode == "tn" else a.shape[::-1]
    n = b.shape[0] if mode == "nt" else b.shape[1]
    tm, tn, tk = _tile(m, tm), _tile(n, tn), _tile(k, tk)
    nk = k // tk
    dims = {"nn": (((1,), (0,)), ((), ())), "nt": (((1,), (1,)), ((), ())), "tn": (((0,), (0,)), ((), ()))}[mode]
    has_res, has_g = res is not None, gvec is not None
    fused = has_res and has_g

    n_rope = 2 if rope is not None else 0

    def body(*refs):
        acc_ref = refs[-1] if nk > 1 else None
        n_in = 2 + has_res + has_g + n_rope
        refs = list(refs[:n_in + 1 + fused])
        a_ref, b_ref = refs[:2]
        res_ref = refs[2] if has_res else None
        g_ref = refs[2 + has_res] if has_g else None
        o_ref = refs[n_in]
        part = lax.dot_general(a_ref[...], b_ref[...], dims, preferred_element_type=F32)

        def finish(acc):
            if fused:
                refs[-1][...] = acc
            out = g_ref[...] * acc if has_g else acc
            if has_res:
                out = res_ref[...] + out
            if n_rope or ones_lane:
                one = (lax.broadcasted_iota(jnp.int32, (tm, HEAD_PAD), 1) == SUM_LANE).astype(F32)
                for hb in range(tn // HEAD_PAD):
                    lanes = slice(hb * HEAD_PAD, (hb + 1) * HEAD_PAD)
                    seg = out[:, lanes]
                    seg = _rope_block(seg, refs[n_in - 2][...], refs[n_in - 1][...]) if n_rope else seg + one
                    o_ref[:, lanes] = seg.astype(o_ref.dtype)
            else:
                o_ref[...] = out.astype(o_ref.dtype)

        if nk == 1:
            finish(part)
            return
        kk = pl.program_id(2)

        @pl.when(kk == 0)
        def _():
            acc_ref[...] = part

        @pl.when(kk > 0)
        def _():
            acc_ref[...] += part

        @pl.when(kk == nk - 1)
        def _():
            finish(acc_ref[...])

    if mode == "tn":
        a_spec = pl.BlockSpec((tk, tm), lambda i, j, kk: (kk, i))
    else:
        a_spec = pl.BlockSpec((tm, tk), lambda i, j, kk: (i, kk))
    if mode == "nt":
        b_spec = pl.BlockSpec((tn, tk), lambda i, j, kk: (j, kk))
    else:
        b_spec = pl.BlockSpec((tk, tn), lambda i, j, kk: (kk, j))
    o_spec = pl.BlockSpec((tm, tn), lambda i, j, kk: (i, j))
    in_specs, args = [a_spec, b_spec], [a, b]
    out_specs, out_shape = o_spec, jax.ShapeDtypeStruct((m, n), out_dtype)
    if has_res:
        in_specs.append(o_spec)
        args.append(res)
    if has_g:
        in_specs.append(pl.BlockSpec((1, tn), lambda i, j, kk: (0, j)))
        args.append(gvec)
    if n_rope:
        in_specs += [pl.BlockSpec((tm, HEAD_PAD), lambda i, j, kk: (i, 0))] * 2
        args += list(rope)
    if fused:
        out_specs = (o_spec, o_spec)
        out_shape = (out_shape, jax.ShapeDtypeStruct((m, n), F32))
    return pl.pallas_call(
        body, name=name, grid=(m // tm, n // tn, nk),
        in_specs=in_specs, out_specs=out_specs, out_shape=out_shape,
        scratch_shapes=[pltpu.VMEM((tm, tn), F32)] if nk > 1 else [],
        compiler_params=_params(("parallel", "parallel", "arbitrary")),
    )(*args)


def _rms_mod(x, g, sc, sh, name):
    s, w = x.shape
    tm = _tile(s, ROW_T)

    def body(x_ref, g_ref, sc_ref, sh_ref, o_ref):
        xv = x_ref[...]
        r = lax.rsqrt(jnp.mean(xv * xv, axis=-1, keepdims=True) + EPS)
        o_ref[...] = ((xv * r * g_ref[...]) * (1.0 + sc_ref[...]) + sh_ref[...]).astype(o_ref.dtype)

    row = pl.BlockSpec((tm, w), lambda i: (i, 0))
    return pl.pallas_call(
        body, name=name, grid=(s // tm,),
        in_specs=[row, _vec_spec(w, 1), _vec_spec(w, 1), _vec_spec(w, 1)],
        out_specs=row, out_shape=jax.ShapeDtypeStruct((s, w), BF16),
        compiler_params=_params(("parallel",)),
    )(x, g, sc, sh)


def _latent_norm(lat, q_g, kv_g, name):
    s, w = lat.shape
    tm = _tile(s, ROW_T)
    nq_, nkv = MLA_Q_RANK, MLA_KV_RANK

    def norm(xv, gv):
        return xv * lax.rsqrt(jnp.mean(xv * xv, axis=-1, keepdims=True) + EPS) * gv

    def body(lat_ref, qg_ref, kg_ref, qn_ref, kv_ref):
        qn_ref[...] = norm(lat_ref[:, 0:nq_], qg_ref[...]).astype(BF16)
        kv_ref[:, 0:nkv] = norm(lat_ref[:, nq_:nq_ + nkv], kg_ref[...]).astype(BF16)
        kv_ref[:, nkv:nkv + HEAD_PAD] = lat_ref[:, nq_ + nkv:nq_ + nkv + HEAD_PAD].astype(BF16)

    out = lambda n: pl.BlockSpec((tm, n), lambda i: (i, 0))
    return pl.pallas_call(
        body, name=name, grid=(s // tm,),
        in_specs=[out(w), _vec_spec(nq_, 1), _vec_spec(nkv, 1)],
        out_specs=(out(nq_), out(nkv + HEAD_PAD)),
        out_shape=(jax.ShapeDtypeStruct((s, nq_), BF16), jax.ShapeDtypeStruct((s, nkv + HEAD_PAD), BF16)),
        compiler_params=_params(("parallel",)),
    )(lat, q_g, kv_g)


def _rms_mod_bwd(dh, x, g, sc, dres, name, branch=None):
    s, w = x.shape
    tm = _tile(s, ROW_T)
    has_res, has_br = dres is not None, branch is not None

    def body(*refs):
        dh_ref, x_ref, g_ref, sc_ref = refs[:4]
        rest = list(refs[4:])
        dres_ref = rest.pop(0) if has_res else None
        val_ref, bg_ref = (rest.pop(0), rest.pop(0)) if has_br else (None, None)
        dx_ref = rest.pop(0)
        db_ref = rest.pop(0) if has_br else None
        sums_ref = rest.pop(0)
        xv, dhv, gv = x_ref[...], dh_ref[...], g_ref[...]
        r = lax.rsqrt(jnp.mean(xv * xv, axis=-1, keepdims=True) + EPS)
        xhat = xv * r
        dxn = dhv * (1.0 + sc_ref[...])
        dxhat = dxn * gv
        dx = r * (dxhat - xhat * jnp.mean(dxhat * xhat, axis=-1, keepdims=True))
        if has_res:
            dx = dx + dres_ref[...]
        dx_ref[...] = dx

        @pl.when(pl.program_id(0) == 0)
        def _():
            sums_ref[...] = jnp.zeros_like(sums_ref)

        sums_ref[0:1, :] += jnp.sum(dhv, axis=0, keepdims=True)
        sums_ref[1:2, :] += jnp.sum(dhv * (xhat * gv), axis=0, keepdims=True)
        sums_ref[2:3, :] += jnp.sum(dxn * xhat, axis=0, keepdims=True)
        if has_br:
            db_ref[...] = (dx * bg_ref[...]).astype(db_ref.dtype)
            sums_ref[3:4, :] += jnp.sum(dx * val_ref[...], axis=0, keepdims=True)

    row = pl.BlockSpec((tm, w), lambda i: (i, 0))
    in_specs = [row, row, _vec_spec(w, 1), _vec_spec(w, 1)] + ([row] if has_res else [])
    args = [dh, x, g, sc] + ([dres] if has_res else [])
    out_specs, out_shape = [row], [jax.ShapeDtypeStruct((s, w), F32)]
    if has_br:
        in_specs += [row, _vec_spec(w, 1)]
        args += list(branch)
        out_specs.append(row)
        out_shape.append(jax.ShapeDtypeStruct((s, w), BF16))
    out_specs.append(pl.BlockSpec((8, w), lambda i: (0, 0)))
    out_shape.append(jax.ShapeDtypeStruct((8, w), F32))
    return pl.pallas_call(
        body, name=name, grid=(s // tm,),
        in_specs=in_specs, out_specs=tuple(out_specs), out_shape=tuple(out_shape),
        compiler_params=_params(("arbitrary",)),
    )(*args)


def _final_loss(x3, target, g, ffn, gvec, name):
    s, w = x3.shape
    tm = _tile(s, ROW_T)

    def body(x_ref, t_ref, g_ref, ffn_ref, bg_ref, dx_ref, db_ref, sums_ref):
        xv, gv = x_ref[...], g_ref[...]
        r = lax.rsqrt(jnp.mean(xv * xv, axis=-1, keepdims=True) + EPS)
        xhat = xv * r
        err = xhat * gv - t_ref[...]
        dy = err * (1.0 / w)
        dxhat = dy * gv
        dx = r * (dxhat - xhat * jnp.mean(dxhat * xhat, axis=-1, keepdims=True))
        dx_ref[...] = dx
        db_ref[...] = (dx * bg_ref[...]).astype(db_ref.dtype)

        @pl.when(pl.program_id(0) == 0)
        def _():
            sums_ref[...] = jnp.zeros_like(sums_ref)

        sums_ref[0:1, :] += jnp.sum(dy * xhat, axis=0, keepdims=True)
        sums_ref[1:2, :] += jnp.zeros((1, w), F32) + (0.5 / w) * jnp.sum(err * err)
        sums_ref[2:3, :] += jnp.sum(dx * ffn_ref[...], axis=0, keepdims=True)

    row = pl.BlockSpec((tm, w), lambda i: (i, 0))
    return pl.pallas_call(
        body, name=name, grid=(s // tm,),
        in_specs=[row, row, _vec_spec(w, 1), row, _vec_spec(w, 1)],
        out_specs=(row, row, pl.BlockSpec((8, w), lambda i: (0, 0))),
        out_shape=(jax.ShapeDtypeStruct((s, w), F32), jax.ShapeDtypeStruct((s, w), BF16),
                   jax.ShapeDtypeStruct((8, w), F32)),
        compiler_params=_params(("arbitrary",)),
    )(x3, target, g, ffn, gvec)


def _rope_block(seg, cmul, smul):
    lane = lax.broadcasted_iota(jnp.int32, seg.shape, 1)
    swapped = jnp.where(lane < MLA_NOPE + MLA_ROPE // 2,
                        pltpu.roll(seg, HEAD_PAD - MLA_ROPE // 2, 1), pltpu.roll(seg, MLA_ROPE // 2, 1))
    return seg * cmul + swapped * smul


def _rope(t, cmul, smul, name):
    s, w = t.shape
    tm = _tile(s, ROW_T)

    def body(t_ref, c_ref, s_ref, o_ref):
        cv, sv = c_ref[...], s_ref[...]
        for hb in range(w // HEAD_PAD):
            lanes = slice(hb * HEAD_PAD, (hb + 1) * HEAD_PAD)
            o_ref[:, lanes] = _rope_block(t_ref[:, lanes].astype(F32), cv, sv).astype(o_ref.dtype)

    row = pl.BlockSpec((tm, w), lambda i: (i, 0))
    tab = pl.BlockSpec((tm, HEAD_PAD), lambda i: (i, 0))
    return pl.pallas_call(
        body, name=name, grid=(s // tm,),
        in_specs=[row, tab, tab], out_specs=row, out_shape=jax.ShapeDtypeStruct((s, w), BF16),
        compiler_params=_params(("parallel",)),
    )(t, cmul, smul)


def _rope_bwd_kv(dk, dv, cmul, smul, name):
    s, w = dk.shape
    tm = _tile(s, ROW_T)

    def body(dk_ref, dv_ref, c_ref, s_ref, o_ref):
        cv, sv = c_ref[...], s_ref[...]
        for hb in range(N_HEADS):
            lo, hi = hb * HEAD_PAD, (hb + 1) * HEAD_PAD
            o_ref[:, lo:hi] = _rope_block(dk_ref[:, lo:hi].astype(F32), cv, sv).astype(o_ref.dtype)
        o_ref[:, w:2 * w] = dv_ref[...].astype(o_ref.dtype)

    row = pl.BlockSpec((tm, w), lambda i: (i, 0))
    tab = pl.BlockSpec((tm, HEAD_PAD), lambda i: (i, 0))
    return pl.pallas_call(
        body, name=name, grid=(s // tm,),
        in_specs=[row, row, tab, tab],
        out_specs=pl.BlockSpec((tm, 2 * w), lambda i: (i, 0)),
        out_shape=jax.ShapeDtypeStruct((s, 2 * w), BF16),
        compiler_params=_params(("parallel",)),
    )(dk, dv, cmul, smul)


def _lanes(col, width):
    if col.shape[1] == 1:
        col = jnp.broadcast_to(col, (col.shape[0], HEAD_PAD))
    return jnp.tile(col, (1, width // HEAD_PAD))


def _fold_lanes(a):
    out = a[:, 0:HEAD_PAD]
    for g in range(1, a.shape[1] // HEAD_PAD):
        out = out + a[:, g * HEAD_PAD:(g + 1) * HEAD_PAD]
    return out


def _as_row(rep):
    return rep.T[0:1, :]


def _causal(t, rows_are_queries):
    row = lax.broadcasted_iota(jnp.int32, (t, t), 0)
    col = lax.broadcasted_iota(jnp.int32, (t, t), 1)
    return row >= col if rows_are_queries else col >= row


def _split_refs(refs, n_in, n_out, n_scratch, n_x):
    pos = [n_in, n_x, n_out, n_x, n_scratch, 3 if n_x else 0]
    out, at = [], 0
    for cnt in pos:
        out.append(refs[at:at + cnt])
        at += cnt
    return out


def _first_last_step(n0, n1):
    i0, i1 = pl.program_id(0), pl.program_id(1)
    return jnp.logical_and(i0 == 0, i1 == 0), jnp.logical_and(i0 == n0 - 1, i1 == n1 - 1)


def _as_lanes(row):
    return jnp.broadcast_to(row, (HEAD_PAD, row.shape[1])).T


def _attn_fwd(q, k, v, frow, name, gather=()):
    s = q.shape[0]
    t = ATT_T
    nq = s // t
    use_f = frow is not None
    nx = len(gather)

    hpb = FWD_HEADS_PER_STEP

    def body(*refs):
        ins, x_src, outs, x_out, scr, x_sems = _split_refs(refs, 4 if use_f else 3, 3, 2, nx)
        if use_f:
            q_ref, k_ref, v_ref, fr_ref = ins
            fc_b = [_as_lanes(fr_ref[hh, pl.program_id(1)]) for hh in range(hpb)]
        else:
            q_ref, k_ref, v_ref = ins
        o_ref, ob_ref, lse_ref = outs
        m_s, acc_s = scr
        if nx:
            first, last = _first_last_step(N_HEADS // hpb, nq)
            x_start, x_wait = _direct_exchange(x_src, x_out, *x_sems, scatter=False)
            pl.when(first)(x_start)
        qi = pl.program_id(1)
        m_s[...] = jnp.full(m_s.shape, NEG_BIG, F32)
        acc_s[...] = jnp.zeros(acc_s.shape, F32)

        def step(j, masked):
            off = pl.multiple_of(j * t, t)
            for hh in range(hpb):
                lanes = slice(hh * HEAD_PAD, (hh + 1) * HEAD_PAD)
                kv = k_ref[pl.ds(off, t), lanes]
                vv = v_ref[pl.ds(off, t), lanes]
                sc = lax.dot_general(q_ref[:, lanes], kv, (((1,), (1,)), ((), ())), preferred_element_type=F32)
                if use_f:
                    sc = sc + (_lanes(fc_b[hh], t) - fr_ref[hh, j])
                if masked:
                    sc = jnp.where(_causal(t, True), sc, NEG_BIG)
                m_prev = m_s[hh]
                m_new = jnp.maximum(m_prev, jnp.max(sc, axis=-1, keepdims=True))
                p = jnp.exp2(sc - _lanes(m_new, t))
                acc_s[hh] = jnp.exp2(m_prev - m_new) * acc_s[hh] + jnp.dot(p.astype(BF16), vv,
                                                                           preferred_element_type=F32)
                m_s[hh] = m_new

        def loop_body(j, carry):
            step(j, False)
            return carry

        lax.fori_loop(0, qi, loop_body, 0)
        step(qi, True)
        for hh in range(hpb):
            lanes = slice(hh * HEAD_PAD, (hh + 1) * HEAD_PAD)
            acc = acc_s[hh]
            lane = lax.broadcasted_iota(jnp.int32, acc.shape, 1)
            denom = jnp.sum(jnp.where(lane == SUM_LANE, acc, 0.0), axis=-1, keepdims=True)
            o = acc * (1.0 / denom)
            o_ref[:, lanes] = o
            ob_ref[:, lanes] = o.astype(BF16)
            lse_ref[hh, 0] = _as_row(m_s[hh] + jnp.log(denom) * LOG2E)
        if nx:
            pl.when(last)(x_wait)

    w = hpb * HEAD_PAD
    qspec = pl.BlockSpec((t, w), lambda h, i: (i, h))
    kspec = pl.BlockSpec((s, w), lambda h, i: (0, h))
    any_spec = pl.BlockSpec(memory_space=pl.ANY)
    in_specs, args = [qspec, kspec, kspec], [q, k, v]
    if use_f:
        in_specs += [pl.BlockSpec((hpb, nq, 1, t), lambda h, i: (h, 0, 0, 0))]
        args += [frow]
    out_specs = [qspec, qspec, pl.BlockSpec((hpb, 1, 1, t), lambda h, i: (h, i, 0, 0))]
    out_shape = [jax.ShapeDtypeStruct((s, N_HEADS * HEAD_PAD), F32), jax.ShapeDtypeStruct((s, N_HEADS * HEAD_PAD), BF16),
                 jax.ShapeDtypeStruct((N_HEADS, nq, 1, t), F32)]
    scratch = [pltpu.VMEM((hpb, t, HEAD_PAD), F32), pltpu.VMEM((hpb, t, HEAD_PAD), F32)]
    if nx:
        in_specs += [any_spec] * nx
        args += list(gather)
        out_specs += [any_spec] * nx
        out_shape += [jax.ShapeDtypeStruct((N_DEV,) + g.shape, g.dtype) for g in gather]
        scratch += _exchange_scratch(nx)
    return pl.pallas_call(
        body, name=name, grid=(N_HEADS // hpb, nq),
        in_specs=in_specs, out_specs=tuple(out_specs), out_shape=tuple(out_shape),
        scratch_shapes=scratch,
        compiler_params=_params(("arbitrary", "arbitrary") if nx else ("parallel", "arbitrary")),
    )(*args)


def _attn_delta(o, do, name):
    s, w = o.shape
    t = ATT_T

    def body(o_ref, do_ref, d_ref):
        for hb in range(N_HEADS):
            lo, hi = hb * HEAD_PAD, (hb + 1) * HEAD_PAD
            prod = o_ref[:, lo:hi] * do_ref[:, lo:hi].astype(F32)
            d_ref[hb, 0] = jnp.sum(prod.T, axis=0, keepdims=True)

    row = pl.BlockSpec((t, w), lambda i: (i, 0))
    return pl.pallas_call(
        body, name=name, grid=(s // t,),
        in_specs=[row, row],
        out_specs=pl.BlockSpec((N_HEADS, 1, 1, t), lambda i: (0, i, 0, 0)),
        out_shape=jax.ShapeDtypeStruct((N_HEADS, s // t, 1, t), F32),
        compiler_params=_params(("parallel",)),
    )(o, do)


def _attn_bwd(q, k, v, do, lse_row, delta_row, frow, scale_q, scale_k, name, out_dtype, scatter=()):
    s = q.shape[0]
    t = ATT_T
    nq = s // t
    use_f = frow is not None
    nx = len(scatter)
    hpb = BWD_HEADS_PER_STEP

    def body(*refs):
        ins, x_src, outs, x_out, scr, x_sems = _split_refs(refs, 7 if use_f else 6, 5 if use_f else 3,
                                                           5 if use_f else 3, nx)
        if use_f:
            q_ref, k_ref, v_ref, do_ref, lse_ref, dl_ref, fr_ref = ins
            dq_ref, dk_ref, dv_ref, dr_ref, df_ref = outs
            dq_s, dk_s, dv_s, dr_s, df_s = scr
            fc_b = [_as_lanes(fr_ref[hh, pl.program_id(1)]) for hh in range(hpb)]
        else:
            q_ref, k_ref, v_ref, do_ref, lse_ref, dl_ref = ins
            dq_ref, dk_ref, dv_ref = outs
            dq_s, dk_s, dv_s = scr
        if nx:
            first, last = _first_last_step(N_HEADS // hpb, nq)
            x_start, x_wait = _direct_exchange(x_src, x_out, *x_sems, scatter=True)
            pl.when(first)(x_start)
        kj = pl.program_id(1)

        @pl.when(kj == 0)
        def _():
            dq_s[...] = jnp.zeros(dq_s.shape, F32)
            if use_f:
                dr_s[...] = jnp.zeros(dr_s.shape, F32)

        dk_s[...] = jnp.zeros(dk_s.shape, F32)
        dv_s[...] = jnp.zeros(dv_s.shape, F32)
        if use_f:
            df_s[...] = jnp.zeros(df_s.shape, F32)

        def step(i, masked):
            off = pl.multiple_of(i * t, t)
            for hh in range(hpb):
                lanes = slice(hh * HEAD_PAD, (hh + 1) * HEAD_PAD)
                kv, vv = k_ref[:, lanes], v_ref[:, lanes]
                qv = q_ref[pl.ds(off, t), lanes]
                dov = do_ref[pl.ds(off, t), lanes]
                st = lax.dot_general(kv, qv, (((1,), (1,)), ((), ())), preferred_element_type=F32)
                if use_f:
                    st = st + (fr_ref[hh, i] - _lanes(fc_b[hh], t))
                if masked:
                    st = jnp.where(_causal(t, False), st, NEG_BIG)
                pt = jnp.exp2(st - lse_ref[hh, i])
                dv_s[hh] += jnp.dot(pt.astype(BF16), dov, preferred_element_type=F32)
                dpt = lax.dot_general(vv, dov, (((1,), (1,)), ((), ())), preferred_element_type=F32)
                dst = pt * (dpt - dl_ref[hh, i])
                dsb = dst.astype(BF16)
                dk_s[hh] += jnp.dot(dsb, qv, preferred_element_type=F32)
                dq_s[hh, pl.ds(off, t), :] += lax.dot_general(dsb, kv, (((0,), (0,)), ((), ())),
                                                              preferred_element_type=F32)
                if use_f:
                    df_s[hh] -= _fold_lanes(dst)
                    dr_s[hh, i] += jnp.sum(dst, axis=0, keepdims=True)

        step(kj, True)

        def loop_body(i, carry):
            step(i, False)
            return carry

        lax.fori_loop(kj + 1, nq, loop_body, 0)
        for hh in range(hpb):
            lanes = slice(hh * HEAD_PAD, (hh + 1) * HEAD_PAD)
            dk_ref[:, lanes] = (dk_s[hh] * scale_k).astype(dk_ref.dtype)
            dv_ref[:, lanes] = dv_s[hh].astype(dv_ref.dtype)
            if use_f:
                df_ref[hh, 0] = jnp.sum(df_s[hh].T, axis=0, keepdims=True)

        @pl.when(kj == nq - 1)
        def _():
            for hh in range(hpb):
                dq_ref[:, hh * HEAD_PAD:(hh + 1) * HEAD_PAD] = (dq_s[hh] * scale_q).astype(dq_ref.dtype)
            if use_f:
                dr_ref[...] = dr_s[...]

        if nx:
            pl.when(last)(x_wait)

    w = hpb * HEAD_PAD
    kspec = pl.BlockSpec((t, w), lambda h, j: (j, h))
    qspec = pl.BlockSpec((s, w), lambda h, j: (0, h))
    rowspec = pl.BlockSpec((hpb, nq, 1, t), lambda h, j: (h, 0, 0, 0))
    any_spec = pl.BlockSpec(memory_space=pl.ANY)
    in_specs, args = [qspec, kspec, kspec, qspec, rowspec, rowspec], [q, k, v, do, lse_row, delta_row]
    full = jax.ShapeDtypeStruct((s, N_HEADS * HEAD_PAD), out_dtype)
    out_specs, out_shape = [qspec, kspec, kspec], [full, full, full]
    scratch = [pltpu.VMEM((hpb, s, HEAD_PAD), F32), pltpu.VMEM((hpb, t, HEAD_PAD), F32),
               pltpu.VMEM((hpb, t, HEAD_PAD), F32)]
    if use_f:
        in_specs += [rowspec]
        args += [frow]
        out_specs += [rowspec, pl.BlockSpec((hpb, 1, 1, t), lambda h, j: (h, j, 0, 0))]
        out_shape += [jax.ShapeDtypeStruct((N_HEADS, nq, 1, t), F32)] * 2
        scratch += [pltpu.VMEM((hpb, nq, 1, t), F32), pltpu.VMEM((hpb, t, HEAD_PAD), F32)]
    if nx:
        in_specs += [any_spec] * nx
        args += list(scatter)
        out_specs += [any_spec] * nx
        out_shape += [jax.ShapeDtypeStruct(g.shape, g.dtype) for g in scatter]
        scratch += _exchange_scratch(nx)
    return pl.pallas_call(
        body, name=name, grid=(N_HEADS // hpb, nq),
        in_specs=in_specs, out_specs=tuple(out_specs), out_shape=tuple(out_shape),
        scratch_shapes=scratch,
        compiler_params=_params(("arbitrary", "arbitrary") if nx else ("parallel", "arbitrary")),
    )(*args)


def _gate_fwd(pm, pf, gates, name):
    s, w = pm.shape
    tm = _tile(s, ROW_T)

    def body(pm_ref, pf_ref, g_ref, y_ref):
        y = (jax.nn.sigmoid(g_ref[:, 0:w]) * pm_ref[...].astype(F32)
             + jax.nn.sigmoid(g_ref[:, w:2 * w]) * pf_ref[...].astype(F32))
        y_ref[...] = y.astype(y_ref.dtype)

    row = pl.BlockSpec((tm, w), lambda i: (i, 0))
    return pl.pallas_call(
        body, name=name, grid=(s // tm,),
        in_specs=[row, row, pl.BlockSpec((tm, 2 * w), lambda i: (i, 0))],
        out_specs=row, out_shape=jax.ShapeDtypeStruct((s, w), BF16),
        compiler_params=_params(("parallel",)),
    )(pm, pf, gates)


def _gate_bwd(dy, pm, pf, gates, name):
    s, w = pm.shape
    tm = _tile(s, ROW_T)

    def body(dy_ref, pm_ref, pf_ref, g_ref, dpm_ref, dpf_ref, dg_ref):
        dyv = dy_ref[...]
        sm, sf = jax.nn.sigmoid(g_ref[:, 0:w]), jax.nn.sigmoid(g_ref[:, w:2 * w])
        dpm_ref[...] = (dyv * sm).astype(BF16)
        dpf_ref[...] = (dyv * sf).astype(BF16)
        dg_ref[:, 0:w] = (dyv * pm_ref[...].astype(F32) * (sm * (1.0 - sm))).astype(BF16)
        dg_ref[:, w:2 * w] = (dyv * pf_ref[...].astype(F32) * (sf * (1.0 - sf))).astype(BF16)

    row = pl.BlockSpec((tm, w), lambda i: (i, 0))
    wide = pl.BlockSpec((tm, 2 * w), lambda i: (i, 0))
    out = jax.ShapeDtypeStruct((s, w), BF16)
    return pl.pallas_call(
        body, name=name, grid=(s // tm,),
        in_specs=[row, row, row, wide],
        out_specs=(row, row, wide), out_shape=(out, out, jax.ShapeDtypeStruct((s, 2 * w), BF16)),
        compiler_params=_params(("parallel",)),
    )(dy, pm, pf, gates)


CONV_TN = 256
CONV_TM = 512
HALO = BF16_ROWS


def _shift_down(u, prev, n):
    rolled = pltpu.roll(u, n, 0)
    prev_rolled = pltpu.roll(prev, n, 0)
    top = jnp.concatenate([prev_rolled, rolled[HALO:]], axis=0)
    row = lax.broadcasted_iota(jnp.int32, u.shape, 0)
    return jnp.where(row < n, top, rolled)


def _conv_tile(u, prev, w_ref, b_ref):
    um1 = _shift_down(u, prev, 1)
    um2 = _shift_down(u, prev, 2)
    uc = b_ref[...] + w_ref[0:1, :] * um2 + w_ref[1:2, :] * um1 + w_ref[2:3, :] * u
    return uc, um1, um2


def _conv_specs(tm, tn, ncol_off):
    blk = lambda off: pl.BlockSpec((tm, tn), lambda j, i: (i, j + off))
    halo = lambda off: pl.BlockSpec((HALO, tn), lambda j, i: (jnp.maximum(i * (tm // HALO) - 1, 0), j + off))
    wsp = lambda off: pl.BlockSpec((3, tn), lambda j, i: (0, j + off))
    bsp = lambda off: pl.BlockSpec((1, tn), lambda j, i: (0, j + off))
    return blk, halo, wsp, bsp


def _convglu_fwd(u, conv_w, conv_b, name):
    s = u.shape[0]
    tm, tn = _tile(s, CONV_TM), CONV_TN
    nj = D_FF // tn
    blk, halo, wsp, bsp = _conv_specs(tm, tn, nj)

    def body(ug_ref, pg_ref, uv_ref, pv_ref, wg_ref, wv_ref, bg_ref, bv_ref, a_ref):
        live = (pl.program_id(1) > 0).astype(F32)
        gate, _, _ = _conv_tile(ug_ref[...].astype(F32), pg_ref[...].astype(F32) * live, wg_ref, bg_ref)
        val, _, _ = _conv_tile(uv_ref[...].astype(F32), pv_ref[...].astype(F32) * live, wv_ref, bv_ref)
        a_ref[...] = (gate * jax.nn.sigmoid(gate) * val).astype(a_ref.dtype)

    return pl.pallas_call(
        body, name=name, grid=(nj, s // tm),
        in_specs=[blk(0), halo(0), blk(nj), halo(nj), wsp(0), wsp(nj), bsp(0), bsp(nj)],
        out_specs=blk(0), out_shape=jax.ShapeDtypeStruct((s, D_FF), BF16),
        compiler_params=_params(("parallel", "arbitrary")),
    )(u, u, u, u, conv_w, conv_w, conv_b, conv_b)


def _convglu_bwd(da, u, conv_w, conv_b, name):
    s = u.shape[0]
    tm, tn = _tile(s, CONV_TM), CONV_TN
    nj = D_FF // tn
    blk, halo, wsp, bsp = _conv_specs(tm, tn, nj)

    def body(da_ref, ug_ref, pg_ref, uv_ref, pv_ref, wg_ref, wv_ref, bg_ref, bv_ref,
             dg_ref, dv_ref, sg_ref, sv_ref):
        live = (pl.program_id(1) > 0).astype(F32)
        ug, uv = ug_ref[...].astype(F32), uv_ref[...].astype(F32)
        gate, ug1, ug2 = _conv_tile(ug, pg_ref[...].astype(F32) * live, wg_ref, bg_ref)
        val, uv1, uv2 = _conv_tile(uv, pv_ref[...].astype(F32) * live, wv_ref, bv_ref)
        dav = da_ref[...].astype(F32)
        sig = jax.nn.sigmoid(gate)
        dgate = dav * val * (sig * (1.0 + gate * (1.0 - sig)))
        dval = dav * (gate * sig)
        dg_ref[...] = dgate.astype(dg_ref.dtype)
        dv_ref[...] = dval.astype(dv_ref.dtype)

        @pl.when(pl.program_id(1) == 0)
        def _():
            sg_ref[...] = jnp.zeros_like(sg_ref)
            sv_ref[...] = jnp.zeros_like(sv_ref)

        for s_ref, d, taps in ((sg_ref, dgate, (ug2, ug1, ug)), (sv_ref, dval, (uv2, uv1, uv))):
            for r, tap in enumerate(taps):
                s_ref[r:r + 1, :] += jnp.sum(d * tap, axis=0, keepdims=True)
            s_ref[3:4, :] += jnp.sum(d, axis=0, keepdims=True)

    sums = lambda off: pl.BlockSpec((8, tn), lambda j, i: (0, j + off))
    return pl.pallas_call(
        body, name=name, grid=(nj, s // tm),
        in_specs=[blk(0), blk(0), halo(0), blk(nj), halo(nj), wsp(0), wsp(nj), bsp(0), bsp(nj)],
        out_specs=(blk(0), blk(0), sums(0), sums(0)),
        out_shape=(jax.ShapeDtypeStruct((s, D_FF), BF16), jax.ShapeDtypeStruct((s, D_FF), BF16),
                   jax.ShapeDtypeStruct((8, D_FF), F32), jax.ShapeDtypeStruct((8, D_FF), F32)),
        compiler_params=_params(("parallel", "arbitrary")),
    )(da, u, u, u, u, conv_w, conv_w, conv_b, conv_b)


def _conv_transpose(d, conv_w, name, col0, into=None):
    s, w = d.shape
    tm, tn = _tile(s, CONV_TM), CONV_TN
    last = s // tm - 1
    jo = col0 // tn

    def body(d_ref, nx_ref, w_ref, *rest):
        o_ref = rest[-1]
        dv = d_ref[...].astype(F32)
        nxt = nx_ref[...].astype(F32) * (pl.program_id(1) < last).astype(F32)
        row = lax.broadcasted_iota(jnp.int32, dv.shape, 0)

        def shift_up(n):
            rolled = pltpu.roll(dv, tm - n, 0)
            nxt_rolled = pltpu.roll(nxt, HALO - n, 0)
            bottom = jnp.concatenate([rolled[:tm - HALO], nxt_rolled], axis=0)
            return jnp.where(row >= tm - n, bottom, rolled)

        out = w_ref[2:3, :] * dv + w_ref[1:2, :] * shift_up(1) + w_ref[0:1, :] * shift_up(2)
        o_ref[...] = out.astype(o_ref.dtype)

    blk = pl.BlockSpec((tm, tn), lambda j, i: (i, j))
    nxt_spec = pl.BlockSpec((HALO, tn), lambda j, i: (jnp.minimum((i + 1) * (tm // HALO), s // HALO - 1), j))
    in_specs = [blk, nxt_spec, pl.BlockSpec((3, tn), lambda j, i: (0, j + jo))]
    args = [d, d, conv_w]
    if into is not None:
        in_specs.append(pl.BlockSpec(memory_space=pl.ANY))
        args.append(into)
    return pl.pallas_call(
        body, name=name, grid=(w // tn, s // tm),
        in_specs=in_specs,
        out_specs=pl.BlockSpec((tm, tn), lambda j, i: (i, j + jo)),
        out_shape=jax.ShapeDtypeStruct((s, 2 * D_FF), BF16),
        input_output_aliases={3: 0} if into is not None else {},
        compiler_params=_params(("parallel", "arbitrary")),
    )(*args)


def _split3(a):
    a1 = a.astype(BF16)
    r1 = a - a1.astype(F32)
    a2 = r1.astype(BF16)
    a3 = (r1 - a2.astype(F32)).astype(BF16)
    return a1, a2, a3


def _ones_dot_right(a, mat):
    return sum(jnp.dot(p, mat, preferred_element_type=F32) for p in _split3(a))


def _ones_dot_left(mat, a):
    return sum(jnp.dot(mat, p, preferred_element_type=F32) for p in _split3(a))


def _tri(n, cmp):
    r = lax.broadcasted_iota(jnp.int32, (n, n), 0)
    c = lax.broadcasted_iota(jnp.int32, (n, n), 1)
    return cmp(r, c).astype(BF16)


def _forget_fwd(z, bias, name):
    nh, nr, nl = z.shape

    def body(z_ref, b_ref, f_ref):
        within = _tri(nl, lambda r, c: r <= c)
        before = _tri(nr, lambda r, c: c < r)
        for h in range(nh):
            x = z_ref[h] + b_ref[h]
            lf = jnp.minimum(x, 0.0) - jnp.log(1.0 + jnp.exp(-jnp.abs(x)))
            pre = _ones_dot_right(lf, within)
            tot = jnp.zeros((nr, nl), F32) + jnp.sum(lf, axis=1, keepdims=True)
            f_ref[h] = pre + _ones_dot_left(before, tot)

    return pl.pallas_call(
        body, name=name, out_shape=jax.ShapeDtypeStruct(z.shape, F32),
        compiler_params=pltpu.CompilerParams(vmem_limit_bytes=VMEM_LIMIT_BYTES),
    )(z, bias)


def _forget_bwd(df_rows, df_cols, z, bias, name):
    nh, nr, nl = z.shape

    def body(dfr_ref, dfc_ref, z_ref, b_ref, dz_ref, db_ref):
        within = _tri(nl, lambda r, c: r >= c)
        after = _tri(nr, lambda r, c: c > r)
        for h in range(nh):
            g = dfr_ref[h] + dfc_ref[h]
            suf = _ones_dot_right(g, within)
            tot = jnp.zeros((nr, nl), F32) + jnp.sum(g, axis=1, keepdims=True)
            dlf = suf + _ones_dot_left(after, tot)
            dz = dlf * jax.nn.sigmoid(-(z_ref[h] + b_ref[h]))
            dz_ref[h] = dz
            db_ref[h] = jnp.zeros((1, nl), F32) + jnp.sum(dz)

    return pl.pallas_call(
        body, name=name,
        out_shape=(jax.ShapeDtypeStruct(z.shape, F32), jax.ShapeDtypeStruct(bias.shape, F32)),
        compiler_params=pltpu.CompilerParams(vmem_limit_bytes=VMEM_LIMIT_BYTES),
    )(df_rows, df_cols, z, bias)


def _ada_fwd(c_col, w, b, name):
    kdim, n = w.shape

    def body(c_ref, w_ref, b_ref, ada_ref, act_ref):
        wv = w_ref[...]
        for e in range(N_DEV):
            cv = c_ref[e]
            act = cv * jax.nn.sigmoid(cv)
            act_ref[e] = act
            ada_ref[e:e + 1, :] = jnp.sum(act * wv, axis=0, keepdims=True) + b_ref[...]

    return pl.pallas_call(
        body, name=name,
        out_shape=(jax.ShapeDtypeStruct((N_DEV, n), F32), jax.ShapeDtypeStruct((N_DEV, kdim, 1), F32)),
        compiler_params=pltpu.CompilerParams(vmem_limit_bytes=VMEM_LIMIT_BYTES),
    )(c_col, w, b)


def _ada_bwd(act_col, dada, name):
    kdim = act_col.shape[1]
    n = dada.shape[1]

    def body(act_ref, d_ref, g_ref):
        acc = act_ref[0] * d_ref[0:1, :]
        for e in range(1, N_DEV):
            acc = acc + act_ref[e] * d_ref[e:e + 1, :]
        g_ref[...] = acc

    return pl.pallas_call(
        body, name=name, out_shape=jax.ShapeDtypeStruct((kdim, n), F32),
        compiler_params=pltpu.CompilerParams(vmem_limit_bytes=VMEM_LIMIT_BYTES),
    )(act_col, dada)


def _adamw(parts, w, m, v, name, tr=128):
    npart, r, c = parts.shape
    tr = _tile(r, tr, step=BF16_ROWS) if r % BF16_ROWS == 0 else r

    def body(p_ref, w_ref, m_ref, v_ref, g_ref, d_ref, nm_ref, nv_ref):
        g = p_ref[0].astype(F32)
        for e in range(1, npart):
            g = g + p_ref[e].astype(F32)
        nm = ADAM_B1 * m_ref[...] + (1.0 - ADAM_B1) * g
        nv = ADAM_B2 * v_ref[...] + (1.0 - ADAM_B2) * (g * g)
        m_hat = nm / (1.0 - ADAM_B1 ** ADAM_STEP)
        v_hat = nv / (1.0 - ADAM_B2 ** ADAM_STEP)
        g_ref[...] = g
        d_ref[...] = -ADAM_LR * (m_hat / (jnp.sqrt(v_hat) + ADAM_EPS) + ADAM_WD * w_ref[...])
        nm_ref[...] = nm
        nv_ref[...] = nv

    row = pl.BlockSpec((tr, c), lambda i: (i, 0))
    out = jax.ShapeDtypeStruct((r, c), F32)
    return pl.pallas_call(
        body, name=name, grid=(r // tr,),
        in_specs=[pl.BlockSpec((npart, tr, c), lambda i: (0, i, 0)), row, row, row],
        out_specs=(row, row, row, row), out_shape=(out, out, out, out),
        compiler_params=_params(("parallel",)),
    )(parts, w, m, v)


EARLY = ("w_in", "w_uq", "w_ukv")
LATE = ("w_o_mla", "w_o_fox", "w_out", "w_up", "conv_w", "w_down")
BIG = EARLY + LATE


def _cols_to_full(stack):
    n, r, c = stack.shape
    return stack.transpose(1, 0, 2).reshape(r, n * c)


def _full_to_cols(full, c):
    r = full.shape[0]
    return full.reshape(r, N_DEV, c).transpose(1, 0, 2)


def _pad_heads(a, width, ones_lane=False):
    s = a.shape[0]
    a = a.reshape(s, N_HEADS, width)
    if ones_lane:
        assert width == SUM_LANE
        tail = jnp.zeros((s, N_HEADS, HEAD_PAD - width), a.dtype).at[:, :, 0].set(1.0)
        return jnp.concatenate([a, tail], axis=2).reshape(s, N_HEADS * HEAD_PAD)
    return jnp.pad(a, ((0, 0), (0, 0), (0, HEAD_PAD - width))).reshape(s, N_HEADS * HEAD_PAD)


def _unpad_heads(a, width):
    s = a.shape[0]
    return a.reshape(s, N_HEADS, HEAD_PAD)[:, :, :width].reshape(s, N_HEADS * width)


def _w_in_padded(w_in):
    seg = [w_in[:, IN_OFF[i]:IN_OFF[i + 1]] for i in range(9)]
    cq, ckv, kr, fq, fk, fv, fl, gm, gf = seg
    padc = lambda a, n: jnp.pad(a, ((0, 0), (0, n - a.shape[1])))
    return jnp.concatenate([gm, gf, fq, fk, fv, cq, ckv, padc(kr, 128), padc(fl, 128)], axis=1)


def _w_in_unpadded(g):
    return jnp.concatenate([
        g[:, P_CQ:P_CQ + 384], g[:, P_CKV:P_CKV + 256], g[:, P_KR:P_KR + 32], g[:, P_FQ:P_FQ + 512],
        g[:, P_FK:P_FK + 512], g[:, P_FV:P_FV + 512], g[:, P_FL:P_FL + 8], g[:, P_GM:P_GM + 1024],
        g[:, P_GF:P_GF + 1024]], axis=1)


SMALL = (("b_ada", 6144, 6144), ("norm_mix_g", 1024, 1024), ("q_norm_g", 384, 384), ("kv_norm_g", 256, 256),
         ("b_forget", 8, 128), ("norm_ffn_g", 1024, 1024), ("conv_b", 5632, 5632), ("norm_final_g", 1024, 1024),
         ("loss", 1, 128))
SMALL_OFF = {}
_o = 0
for _n, _real, _padded in SMALL:
    SMALL_OFF[_n] = _o
    _o += _padded
SMALL_W = _o


def _pack_small(vals):
    parts = []
    for nme, real, padded in SMALL:
        a = vals[nme].reshape(1, real).astype(F32)
        parts.append(jnp.pad(a, ((0, 0), (0, padded - real))))
    return jnp.concatenate(parts, axis=1)


def kernel(x, c, positions, w_ada, b_ada, norm_mix_g, w_in, q_norm_g, w_uq, kv_norm_g, w_ukv, b_forget, w_o_mla, w_o_fox, w_out, norm_ffn_g, w_up, conv_w, conv_b, w_down, norm_final_g, loss_target, m_w_ada, m_b_ada, m_norm_mix_g, m_w_in, m_q_norm_g, m_w_uq, m_kv_norm_g, m_w_ukv, m_b_forget, m_w_o_mla, m_w_o_fox, m_w_out, m_norm_ffn_g, m_w_up, m_conv_w, m_conv_b, m_w_down, m_norm_final_g, v_w_ada, v_b_ada, v_norm_mix_g, v_w_in, v_q_norm_g, v_w_uq, v_kv_norm_g, v_w_ukv, v_b_forget, v_w_o_mla, v_w_o_fox, v_w_out, v_norm_ffn_g, v_w_up, v_conv_w, v_conv_b, v_w_down, v_norm_final_g):
    me = 4 * lax.axis_index("x") + 2 * lax.axis_index("y") + lax.axis_index("c")
    x = x[0]
    target = loss_target[0]
    s = x.shape[0]
    nblk = s // ATT_T
    big_w = {"w_in": w_in, "w_uq": w_uq, "w_ukv": w_ukv, "w_o_mla": w_o_mla, "w_o_fox": w_o_fox,
             "w_out": w_out, "w_up": w_up, "conv_w": conv_w, "w_down": w_down}
    big_m = {"w_in": m_w_in, "w_uq": m_w_uq, "w_ukv": m_w_ukv, "w_o_mla": m_w_o_mla, "w_o_fox": m_w_o_fox,
             "w_out": m_w_out, "w_up": m_w_up, "conv_w": m_conv_w, "w_down": m_w_down}
    big_v = {"w_in": v_w_in, "w_uq": v_w_uq, "w_ukv": v_w_ukv, "w_o_mla": v_w_o_mla, "w_o_fox": v_w_o_fox,
             "w_out": v_w_out, "w_up": v_w_up, "conv_w": v_conv_w, "w_down": v_w_down}

    shard = lambda k: big_w[k][0] if k == "conv_w" else big_w[k][0].astype(BF16)
    st = dict(zip(EARLY, _all_gather([shard(k) for k in EARLY], "gather_weights")))
    w_in_p = _w_in_padded(_cols_to_full(st["w_in"]))
    uq = st["w_uq"]
    w_uq_p = jnp.pad(uq, ((0, 0), (0, 0), (0, HEAD_PAD - 96))).transpose(1, 0, 2).reshape(MLA_Q_RANK, 1024)
    ukv = st["w_ukv"]
    zeros64 = jnp.zeros((N_HEADS, MLA_KV_RANK, 64), BF16)
    w_uk_p = jnp.concatenate([ukv[:, :, :64], zeros64], axis=2).transpose(1, 0, 2).reshape(MLA_KV_RANK, 1024)
    w_uv_p = jnp.concatenate([ukv[:, :, 64:], zeros64], axis=2).transpose(1, 0, 2).reshape(MLA_KV_RANK, 1024)
    place = np.zeros((HEAD_PAD, N_HEADS, HEAD_PAD), np.float32)
    for j in range(MLA_ROPE):
        place[j, :, MLA_NOPE + j] = 1.0
    place = jnp.asarray(place.reshape(HEAD_PAD, 1024), BF16)
    w_kv_comb = jnp.concatenate([
        jnp.concatenate([w_uk_p, w_uv_p], axis=1),
        jnp.concatenate([place, jnp.zeros((HEAD_PAD, 1024), BF16)], axis=1)], axis=0)

    (c_all,) = _all_gather([c], "gather_c")
    b_ada_mine = lax.dynamic_slice(b_ada, (0, me * 768), (1, 768))
    ada_cols, act_col = _ada_fwd(c_all.reshape(N_DEV, D_MODEL, 1), w_ada[0], b_ada_mine, "ada_fwd")
    (ada_all,) = _all_gather([ada_cols], "gather_ada")
    ada = lax.dynamic_slice(ada_all, (0, me, 0), (N_DEV, 1, 768)).reshape(1, N_ADA * D_MODEL)
    sh_m, sc_m, g_m, sh_f, sc_f, g_f = [ada[:, i * D_MODEL:(i + 1) * D_MODEL] for i in range(N_ADA)]

    inv_freq = ROPE_THETA ** (-jnp.arange(0, MLA_ROPE, 2, dtype=F32) / MLA_ROPE)
    ang = positions[0].astype(F32)[:, None] * inv_freq
    cos, sin = jnp.cos(ang), jnp.sin(ang)
    rope_c = jnp.concatenate([jnp.ones((s, 64), F32), cos, cos, jnp.zeros((s, 32), F32)], axis=1)
    rope_s = jnp.concatenate([jnp.zeros((s, 64), F32), -sin, sin, jnp.zeros((s, 32), F32)], axis=1)

    h1 = _rms_mod(x, norm_mix_g, sc_m, sh_m, "norm_mix")
    gates = _mm(h1, w_in_p[:, P_GM:P_FQ], "nn", F32, "proj_gates", tn=1024)
    fox_fold = jnp.concatenate([jnp.full((1, 512), FOX_SCALE * LOG2E, F32), jnp.ones((1, 1024), F32)], axis=1)
    fqkv = _mm(h1, w_in_p[:, P_FQ:P_CQ], "nn", BF16, "proj_fox", gvec=fox_fold)
    lat = _mm(h1, w_in_p[:, P_CQ:], "nn", F32, "proj_latent", tn=D_IN_P - P_CQ)
    cq = lat[:, 0:384]
    ckv = lat[:, P_CKV - P_CQ:P_CKV - P_CQ + 256]
    qn, kv_in = _latent_norm(lat, q_norm_g, kv_norm_g, "latent_norm")
    q_fold = MLA_SCALE * LOG2E
    q_att = _mm(qn, w_uq_p, "nn", BF16, "q_up", rope=(rope_c * q_fold, rope_s * q_fold))
    k_att = _mm(kv_in, w_kv_comb[:, :1024], "nn", BF16, "k_up", rope=(rope_c, rope_s))
    v_att = _mm(kv_in, w_kv_comb[:, 1024:], "nn", BF16, "v_up", ones_lane=True)
    o_mla, o_mla_b, lse_mla, *late = _attn_fwd(q_att, k_att, v_att, None, "mla_fwd",
                                               gather=[shard(k) for k in LATE])
    st.update(zip(LATE, late))
    pad_o = lambda full: jnp.pad(full.reshape(N_HEADS, 64, 1024), ((0, 0), (0, 64), (0, 0))).reshape(1024, 1024)
    w_o_mla_p = pad_o(_cols_to_full(st["w_o_mla"]))
    w_o_fox_p = pad_o(_cols_to_full(st["w_o_fox"]))
    w_out_f = st["w_out"].reshape(1024, 1024)
    w_up_f = _cols_to_full(st["w_up"])
    conv_w_f = _cols_to_full(st["conv_w"])
    w_down_f = st["w_down"].reshape(D_FF, 1024)

    fq = _pad_heads(fqkv[:, 0:512], 64)
    fk = _pad_heads(fqkv[:, 512:1024], 64)
    fv = _pad_heads(fqkv[:, 1024:1536], 64, ones_lane=True)
    z = lat[:, P_FL - P_CQ:P_FL - P_CQ + 8].T.reshape(N_HEADS, s // SEQ_LANES, SEQ_LANES)
    bias_f = jnp.broadcast_to(b_forget.reshape(N_HEADS, 1, 1), (N_HEADS, 1, SEQ_LANES))
    f_cum = _forget_fwd(z, bias_f, "forget_fwd")
    f_row = (f_cum * LOG2E).reshape(N_HEADS, nblk, 1, ATT_T)
    o_fox, o_fox_b, lse_fox = _attn_fwd(fq, fk, fv, f_row, "fox_fwd")

    pm = _mm(o_mla_b, w_o_mla_p, "nn", BF16, "o_mla_proj")
    pf = _mm(o_fox_b, w_o_fox_p, "nn", BF16, "o_fox_proj")
    y = _gate_fwd(pm, pf, gates, "gate_fwd")
    x2, mix = _mm(y, w_out_f, "nn", F32, "out_proj", res=x, gvec=g_m)

    h2 = _rms_mod(x2, norm_ffn_g, sc_f, sh_f, "norm_ffn")
    u = _mm(h2, w_up_f, "nn", BF16, "ffn_up")
    a = _convglu_fwd(u, conv_w_f, conv_b, "convglu_fwd")
    x3, ffn = _mm(a, w_down_f, "nn", F32, "ffn_down", res=x2, gvec=g_f, tk=2816)

    dx3, dffn, sums_final = _final_loss(x3, target, norm_final_g.reshape(1, D_MODEL), ffn, g_f, "final_loss")
    da = _mm(dffn, w_down_f, "nt", BF16, "ffn_down_dx", tn=1408)
    g_w_down = _mm(a, dffn, "tn", F32, "ffn_down_dw", tm=256, tn=1024, tk=s)
    dgate, dval, s_gate, s_val = _convglu_bwd(da, u, conv_w_f, conv_b, "convglu_bwd")
    du = _conv_transpose(dgate, conv_w_f, "conv_t_gate", 0)
    du = _conv_transpose(dval, conv_w_f, "conv_t_val", D_FF, into=du)
    dh2 = _mm(du, w_up_f, "nt", F32, "ffn_up_dx", tn=512, tk=2 * D_FF)
    g_w_up = _mm(h2, du, "tn", F32, "ffn_up_dw", tn=256, tk=s)
    dx2, dmix, sums_ffn = _rms_mod_bwd(dh2, x2, norm_ffn_g, sc_f, dx3, "norm_ffn_bwd", branch=(mix, g_m))

    dy = _mm(dmix, w_out_f, "nt", F32, "out_proj_dx")
    g_w_out = _mm(y, dmix, "tn", F32, "out_proj_dw", tn=256, tk=s)
    dpm, dpf, dgates = _gate_bwd(dy, pm, pf, gates, "gate_bwd")
    do_mla_b = _mm(dpm, w_o_mla_p, "nt", BF16, "o_mla_dx", tn=1024)
    do_fox_b = _mm(dpf, w_o_fox_p, "nt", BF16, "o_fox_dx", tn=1024)
    g_w_o_mla_p = _mm(o_mla_b, dpm, "tn", F32, "o_mla_dw", tn=256, tk=s)
    g_w_o_fox_p = _mm(o_fox_b, dpf, "tn", F32, "o_fox_dw", tn=256, tk=s)

    unpad_o = lambda g: g.reshape(N_HEADS, HEAD_PAD, 1024)[:, :64].reshape(512, 1024)
    g_conv_w = jnp.concatenate([s_gate[0:3], s_val[0:3]], axis=1)
    g_blocks = {
        "w_o_mla": _full_to_cols(unpad_o(g_w_o_mla_p), 128), "w_o_fox": _full_to_cols(unpad_o(g_w_o_fox_p), 128),
        "w_out": g_w_out.reshape(N_DEV, 128, 1024), "w_up": _full_to_cols(g_w_up, 704),
        "conv_w": _full_to_cols(g_conv_w, 704), "w_down": g_w_down.reshape(N_DEV, 352, 1024)}

    delta_mla = _attn_delta(o_mla, do_mla_b, "mla_delta")
    dq_rot, dk_rot, dv_mla, *late_recv = _attn_bwd(
        q_att, k_att, v_att, do_mla_b, lse_mla, delta_mla, None, MLA_SCALE, 1.0 / LOG2E, "mla_bwd", BF16,
        scatter=[g_blocks[k].astype(BF16) for k in LATE])
    dq_pre = _rope(dq_rot, rope_c, -rope_s, "rope_q_bwd")
    dkv_pre = _rope_bwd_kv(dk_rot, dv_mla, rope_c, -rope_s, "rope_kv_bwd")
    dqn = _mm(dq_pre, w_uq_p, "nt", F32, "q_up_dx")
    g_w_uq_p = _mm(qn, dq_pre, "tn", F32, "q_up_dw", tk=s)
    dkv_in = _mm(dkv_pre, w_kv_comb, "nt", F32, "kv_up_dx")
    g_w_kv_comb = _mm(kv_in, dkv_pre, "tn", F32, "kv_up_dw", tk=s)
    dcq, sums_q = _rms_mod_bwd(dqn, cq, q_norm_g, jnp.zeros((1, 384), F32), None, "q_norm_bwd")
    dckv, sums_kv = _rms_mod_bwd(dkv_in[:, :256], ckv, kv_norm_g, jnp.zeros((1, 256), F32), None, "kv_norm_bwd")
    delta_fox = _attn_delta(o_fox, do_fox_b, "fox_delta")
    dfq, dfk, dfv, dfr, dfc = _attn_bwd(fq, fk, fv, do_fox_b, lse_fox, delta_fox, f_row,
                                        FOX_SCALE, 1.0 / LOG2E, "fox_bwd", BF16)
    df_rows = dfr.reshape(N_HEADS, s // SEQ_LANES, SEQ_LANES)
    df_cols = dfc.reshape(N_HEADS, s // SEQ_LANES, SEQ_LANES)
    dz, db_f = _forget_bwd(df_rows, df_cols, z, bias_f, "forget_bwd")
    dfl = jnp.pad(dz.reshape(N_HEADS, s).T, ((0, 0), (0, 128 - N_HEADS)))

    dproj = jnp.concatenate([
        dgates, _unpad_heads(dfq, 64), _unpad_heads(dfk, 64), _unpad_heads(dfv, 64),
        dcq.astype(BF16), dckv.astype(BF16), dkv_in[:, 256:384].astype(BF16), dfl.astype(BF16)], axis=1)
    dh1 = _mm(dproj, w_in_p, "nt", F32, "proj_in_dx", tn=512, tk=D_IN_P)
    g_w_in_p = _mm(h1, dproj, "tn", F32, "proj_in_dw", tm=512, tn=640, tk=s)
    grad_x, sums_mix = _rms_mod_bwd(dh1, x, norm_mix_g, sc_m, dx2, "norm_mix_bwd")

    g_w_in = _w_in_unpadded(g_w_in_p)
    g_uq = g_w_uq_p.reshape(MLA_Q_RANK, N_HEADS, HEAD_PAD)[:, :, :96].transpose(1, 0, 2)
    g_uk = g_w_kv_comb[:256, :1024].reshape(256, N_HEADS, HEAD_PAD)[:, :, :64]
    g_uv = g_w_kv_comb[:256, 1024:].reshape(256, N_HEADS, HEAD_PAD)[:, :, :64]
    g_ukv = jnp.concatenate([g_uk, g_uv], axis=2).transpose(1, 0, 2)
    g_blocks.update({"w_in": _full_to_cols(g_w_in, 533), "w_uq": g_uq, "w_ukv": g_ukv})
    early_recv = _all_to_all([g_blocks[k].astype(BF16) for k in EARLY], "scatter_grads")
    g_big, d_big, nm_big, nv_big = {}, {}, {}, {}
    for k, parts in zip(BIG, list(early_recv) + list(late_recv)):
        g_big[k], d_big[k], nm_big[k], nv_big[k] = [
            t[None] for t in _adamw(parts, big_w[k][0], big_m[k][0], big_v[k][0], "adamw_" + k)]

    dada = jnp.concatenate([sums_mix[0:1], sums_mix[1:2], sums_ffn[3:4], sums_ffn[0:1], sums_ffn[1:2], sums_final[2:3]],
                           axis=1)
    small_part = _pack_small({
        "b_ada": dada, "norm_mix_g": sums_mix[2:3], "q_norm_g": sums_q[2:3], "kv_norm_g": sums_kv[2:3],
        "b_forget": db_f[:, 0, 0], "norm_ffn_g": sums_ffn[2:3],
        "conv_b": jnp.concatenate([s_gate[3:4], s_val[3:4]], axis=1), "norm_final_g": sums_final[0:1],
        "loss": sums_final[1:2, 0:1]})
    (small_all,) = _all_gather([small_part], "gather_small")
    zero1 = jnp.zeros((1,), F32)
    small_w = {"b_ada": b_ada, "norm_mix_g": norm_mix_g, "q_norm_g": q_norm_g, "kv_norm_g": kv_norm_g,
               "b_forget": b_forget, "norm_ffn_g": norm_ffn_g, "conv_b": conv_b, "norm_final_g": norm_final_g,
               "loss": zero1}
    small_m = {"b_ada": m_b_ada, "norm_mix_g": m_norm_mix_g, "q_norm_g": m_q_norm_g, "kv_norm_g": m_kv_norm_g,
               "b_forget": m_b_forget, "norm_ffn_g": m_norm_ffn_g, "conv_b": m_conv_b,
               "norm_final_g": m_norm_final_g, "loss": zero1}
    small_v = {"b_ada": v_b_ada, "norm_mix_g": v_norm_mix_g, "q_norm_g": v_q_norm_g, "kv_norm_g": v_kv_norm_g,
               "b_forget": v_b_forget, "norm_ffn_g": v_norm_ffn_g, "conv_b": v_conv_b,
               "norm_final_g": v_norm_final_g, "loss": zero1}
    g_sm, d_sm, nm_sm, nv_sm = _adamw(small_all, _pack_small(small_w), _pack_small(small_m), _pack_small(small_v),
                                      "adamw_small")
    loss = g_sm[0, SMALL_OFF["loss"]]

    dada_all = small_all[:, 0, SMALL_OFF["b_ada"]:SMALL_OFF["b_ada"] + N_ADA * D_MODEL]
    dada_mine = lax.dynamic_slice(dada_all, (0, me * 768), (N_DEV, 768))
    g_ada_local = _ada_bwd(act_col, dada_mine, "ada_bwd")
    g_ada, d_ada, nm_ada, nv_ada = _adamw(g_ada_local[None], w_ada[0], m_w_ada[0], v_w_ada[0], "adamw_ada")

    def small_out(t, nme, shape):
        real = dict((n_, r_) for n_, r_, _ in SMALL)[nme]
        o = SMALL_OFF[nme]
        return t[0, o:o + real].reshape(shape)

    order = ["w_ada", "b_ada", "norm_mix_g", "w_in", "q_norm_g", "w_uq", "kv_norm_g", "w_ukv", "b_forget",
             "w_o_mla", "w_o_fox", "w_out", "norm_ffn_g", "w_up", "conv_w", "conv_b", "w_down", "norm_final_g"]
    small_shapes = {"b_ada": (1, 6144), "norm_mix_g": (1, 1024), "q_norm_g": (1, 384), "kv_norm_g": (1, 256),
                    "b_forget": (1, 8), "norm_ffn_g": (1, 1024), "conv_b": (1, 5632), "norm_final_g": (1024,)}

    def family(big, small, ada_t):
        out = []
        for nme in order:
            if nme == "w_ada":
                out.append(ada_t[None])
            elif nme in small_shapes:
                out.append(small_out(small, nme, small_shapes[nme]))
            else:
                out.append(big[nme])
        return out

    return (loss, grad_x[None], *family(g_big, g_sm, g_ada), *family(d_big, d_sm, d_ada),
            *family(nm_big, nm_sm, nm_ada), *family(nv_big, nv_sm, nv_ada))
```

```python
import math

import numpy as np
import jax
import jax.numpy as jnp
from jax import lax
from jax.experimental import pallas as pl
from jax.experimental.pallas import tpu as pltpu

F32 = jnp.float32
BF16 = jnp.bfloat16

N_DEV = 8
D_MODEL = 1024
N_HEADS = 8
HEAD_PAD = 128
MLA_Q_RANK = 384
MLA_KV_RANK = 256
MLA_NOPE = 64
MLA_ROPE = 32
MLA_V = 64
FOX_DIM = 64
D_FF = 2816
N_ADA = 6
EPS = 1e-6
ROPE_THETA = 10000.0
MLA_SCALE = 1.0 / math.sqrt(MLA_NOPE + MLA_ROPE)
FOX_SCALE = 1.0 / math.sqrt(FOX_DIM)
IN_SPLITS = (384, 256, 32, 512, 512, 512, 8, 1024, 1024)
D_IN = sum(IN_SPLITS)
IN_OFF = tuple(int(v) for v in np.cumsum((0,) + IN_SPLITS))
P_GM, P_GF, P_FQ, P_FK, P_FV, P_CQ, P_CKV, P_KR, P_FL, D_IN_P = 0, 1024, 2048, 2560, 3072, 3584, 3968, 4224, 4352, 4480

ADAM_LR, ADAM_B1, ADAM_B2, ADAM_EPS, ADAM_WD, ADAM_STEP = 0.001, 0.9, 0.999, 1e-08, 0.01, 10

VMEM_LIMIT_BYTES = 56 * 1024 * 1024
NEG_BIG = -1e30
ATT_T = 512
LOG2E = 1.4426950408889634
SUM_LANE = 64
ROW_T = 256
SEQ_LANES = 128
BF16_ROWS = 16
FWD_HEADS_PER_STEP = 4
BWD_HEADS_PER_STEP = 2


def _params(sem):
    return pltpu.CompilerParams(dimension_semantics=sem, vmem_limit_bytes=VMEM_LIMIT_BYTES)


def _tile(n, target, step=128):
    if n <= target:
        return n
    t = (target // step) * step
    while t >= step:
        if n % t == 0:
            return t
        t -= step
    return n


def _vec_spec(w, nargs):
    if nargs == 1:
        return pl.BlockSpec((1, w), lambda i: (0, 0))
    return pl.BlockSpec((1, w), lambda i, j: (0, 0))


def _comm_call(body, name, ins, out_shapes):
    n = len(ins)
    any_spec = pl.BlockSpec(memory_space=pl.ANY)
    return pl.pallas_call(
        body, name=name, out_shape=tuple(out_shapes),
        in_specs=[any_spec] * n, out_specs=tuple([any_spec] * n),
        scratch_shapes=[pltpu.SemaphoreType.DMA((n, 7)), pltpu.SemaphoreType.DMA((n, 7)),
                        pltpu.SemaphoreType.DMA((n,))],
    )(*ins)


def _all_gather(xs, name):
    n = len(xs)

    def body(*refs):
        x_refs, out_refs = refs[:n], refs[n:2 * n]
        send_sems, recv_sems, local_sems = refs[2 * n:]
        x_, y_, c_ = lax.axis_index("x"), lax.axis_index("y"), lax.axis_index("c")
        me, sibling = (x_, y_, c_), (x_, y_, 1 - c_)
        chips = [(1 - x_, y_), (x_, 1 - y_), (1 - x_, 1 - y_)]

        def slot(a, px, py, pc):
            return out_refs[a].at[4 * px + 2 * py + pc]

        def copy(a, k, block, to, src=None):
            return pltpu.make_async_remote_copy(
                src_ref=slot(a, *block) if src is None else src, dst_ref=slot(a, *block),
                send_sem=send_sems.at[a, k], recv_sem=recv_sems.at[a, k],
                device_id=to, device_id_type=pl.DeviceIdType.MESH)

        mine = [pltpu.make_async_copy(x_refs[a], slot(a, *me), local_sems.at[a]) for a in range(n)]
        for cp in mine:
            cp.start()
        first = []
        for a in range(n):
            first.append(copy(a, 0, me, sibling, src=x_refs[a]))
            first += [copy(a, 1 + j, me, (*chip, c_), src=x_refs[a]) for j, chip in enumerate(chips)]
        for cp in first:
            cp.start()
        passed = []
        for j, chip in enumerate(chips):
            for a in range(n):
                copy(a, 1 + j, (*chip, c_), me).wait_recv()
                passed.append(copy(a, 4 + j, (*chip, c_), sibling))
                passed[-1].start()
        for a in range(n):
            copy(a, 0, sibling, me).wait_recv()
            for j, chip in enumerate(chips):
                copy(a, 4 + j, (*chip, 1 - c_), me).wait_recv()
        for cp in first + passed:
            cp.wait_send()
        for cp in mine:
            cp.wait()

    return _comm_call(body, name, xs, [jax.ShapeDtypeStruct((N_DEV,) + x.shape, x.dtype) for x in xs])


def _direct_exchange(src_refs, out_refs, send_sems, recv_sems, local_sems, scatter):
    n = len(src_refs)
    x_, y_, c_ = lax.axis_index("x"), lax.axis_index("y"), lax.axis_index("c")
    me = 4 * x_ + 2 * y_ + c_

    def peer(k):
        return (x_ ^ ((k >> 2) & 1), y_ ^ ((k >> 1) & 1), c_ ^ (k & 1))

    def src(a, slot):
        return src_refs[a].at[slot] if scatter else src_refs[a]

    def copy(a, k, sending):
        px, py, pc = peer(k)
        theirs = 4 * px + 2 * py + pc
        return pltpu.make_async_remote_copy(
            src_ref=src(a, theirs if sending else me), dst_ref=out_refs[a].at[me if sending else theirs],
            send_sem=send_sems.at[a, k - 1], recv_sem=recv_sems.at[a, k - 1],
            device_id=(px, py, pc), device_id_type=pl.DeviceIdType.MESH)

    mine = [pltpu.make_async_copy(src(a, me), out_refs[a].at[me], local_sems.at[a]) for a in range(n)]
    sends = [copy(a, k, True) for a in range(n) for k in range(1, N_DEV)]

    def start():
        for cp in mine + sends:
            cp.start()

    def wait():
        for a in range(n):
            for k in range(1, N_DEV):
                copy(a, k, False).wait_recv()
        for cp in sends:
            cp.wait_send()
        for cp in mine:
            cp.wait()

    return start, wait


def _exchange_scratch(n):
    return [pltpu.SemaphoreType.DMA((n, 7)), pltpu.SemaphoreType.DMA((n, 7)), pltpu.SemaphoreType.DMA((n,))]


def _mm(a, b, mode, out_dtype, name, res=None, gvec=None, tm=1024, tn=512, tk=1024, rope=None, ones_lane=False,
        scatter=()):
    (k, m) = a.shape if mode == "tn" else a.shape[::-1]
    n = b.shape[0] if mode == "nt" else b.shape[1]
    tm, tn, tk = _tile(m, tm), _tile(n, tn), _tile(k, tk)
    nk = k // tk
    dims = {"nn": (((1,), (0,)), ((), ())), "nt": (((1,), (1,)), ((), ())), "tn": (((0,), (0,)), ((), ()))}[mode]
    has_res, has_g = res is not None, gvec is not None
    fused = has_res and has_g

    n_rope = 2 if rope is not None else 0
    nx = len(scatter)
    assert not nx or nk == 1, "the exchange rides only on a matmul with one K step"
    n_in = 2 + has_res + has_g + n_rope

    def body(*refs):
        acc_ref = refs[-1] if nk > 1 else None
        if nx:
            ins, x_src, outs, x_out, _, x_sems = _split_refs(refs, n_in, 1 + fused, 0, nx)
            refs = list(ins) + list(outs)
            i, j = pl.program_id(0), pl.program_id(1)
            x_start, x_wait = _direct_exchange(x_src, x_out, *x_sems, scatter=True)
            pl.when(jnp.logical_and(i == 0, j == 0))(x_start)
        else:
            refs = list(refs[:n_in + 1 + fused])
        a_ref, b_ref = refs[:2]
        res_ref = refs[2] if has_res else None
        g_ref = refs[2 + has_res] if has_g else None
        o_ref = refs[n_in]
        part = lax.dot_general(a_ref[...], b_ref[...], dims, preferred_element_type=F32)

        def finish(acc):
            if fused:
                refs[-1][...] = acc
            out = g_ref[...] * acc if has_g else acc
            if has_res:
                out = res_ref[...] + out
            if n_rope or ones_lane:
                one = (lax.broadcasted_iota(jnp.int32, (tm, HEAD_PAD), 1) == SUM_LANE).astype(F32)
                for hb in range(tn // HEAD_PAD):
                    lanes = slice(hb * HEAD_PAD, (hb + 1) * HEAD_PAD)
                    seg = out[:, lanes]
                    seg = _rope_block(seg, refs[n_in - 2][...], refs[n_in - 1][...]) if n_rope else seg + one
                    o_ref[:, lanes] = seg.astype(o_ref.dtype)
            else:
                o_ref[...] = out.astype(o_ref.dtype)

        if nk == 1:
            finish(part)
            if nx:
                pl.when(jnp.logical_and(i == m // tm - 1, j == n // tn - 1))(x_wait)
            return
        kk = pl.program_id(2)

        @pl.when(kk == 0)
        def _():
            acc_ref[...] = part

        @pl.when(kk > 0)
        def _():
            acc_ref[...] += part

        @pl.when(kk == nk - 1)
        def _():
            finish(acc_ref[...])

    if mode == "tn":
        a_spec = pl.BlockSpec((tk, tm), lambda i, j, kk: (kk, i))
    else:
        a_spec = pl.BlockSpec((tm, tk), lambda i, j, kk: (i, kk))
    if mode == "nt":
        b_spec = pl.BlockSpec((tn, tk), lambda i, j, kk: (j, kk))
    else:
        b_spec = pl.BlockSpec((tk, tn), lambda i, j, kk: (kk, j))
    o_spec = pl.BlockSpec((tm, tn), lambda i, j, kk: (i, j))
    in_specs, args = [a_spec, b_spec], [a, b]
    out_specs, out_shape = o_spec, jax.ShapeDtypeStruct((m, n), out_dtype)
    if has_res:
        in_specs.append(o_spec)
        args.append(res)
    if has_g:
        in_specs.append(pl.BlockSpec((1, tn), lambda i, j, kk: (0, j)))
        args.append(gvec)
    if n_rope:
        in_specs += [pl.BlockSpec((tm, HEAD_PAD), lambda i, j, kk: (i, 0))] * 2
        args += list(rope)
    if fused:
        out_specs = (o_spec, o_spec)
        out_shape = (out_shape, jax.ShapeDtypeStruct((m, n), F32))
    scratch = [pltpu.VMEM((tm, tn), F32)] if nk > 1 else []
    if nx:
        any_spec = pl.BlockSpec(memory_space=pl.ANY)
        in_specs += [any_spec] * nx
        args += list(scatter)
        out_specs = tuple(out_specs if fused else (out_specs,)) + (any_spec,) * nx
        out_shape = tuple(out_shape if fused else (out_shape,)) + tuple(
            jax.ShapeDtypeStruct(g.shape, g.dtype) for g in scatter)
        scratch += _exchange_scratch(nx)
    return pl.pallas_call(
        body, name=name, grid=(m // tm, n // tn, nk),
        in_specs=in_specs, out_specs=out_specs, out_shape=out_shape,
        scratch_shapes=scratch,
        compiler_params=_params(("arbitrary",) * 3 if nx else ("parallel", "parallel", "arbitrary")),
    )(*args)


def _rms_mod(x, g, sc, sh, name):
    s, w = x.shape
    tm = _tile(s, ROW_T)

    def body(x_ref, g_ref, sc_ref, sh_ref, o_ref):
        xv = x_ref[...]
        r = lax.rsqrt(jnp.mean(xv * xv, axis=-1, keepdims=True) + EPS)
        o_ref[...] = ((xv * r * g_ref[...]) * (1.0 + sc_ref[...]) + sh_ref[...]).astype(o_ref.dtype)

    row = pl.BlockSpec((tm, w), lambda i: (i, 0))
    return pl.pallas_call(
        body, name=name, grid=(s // tm,),
        in_specs=[row, _vec_spec(w, 1), _vec_spec(w, 1), _vec_spec(w, 1)],
        out_specs=row, out_shape=jax.ShapeDtypeStruct((s, w), BF16),
        compiler_params=_params(("parallel",)),
    )(x, g, sc, sh)


def _latent_norm(lat, q_g, kv_g, name):
    s, w = lat.shape
    tm = _tile(s, ROW_T)
    nq_, nkv = MLA_Q_RANK, MLA_KV_RANK

    def norm(xv, gv):
        return xv * lax.rsqrt(jnp.mean(xv * xv, axis=-1, keepdims=True) + EPS) * gv

    def body(lat_ref, qg_ref, kg_ref, qn_ref, kv_ref):
        qn_ref[...] = norm(lat_ref[:, 0:nq_], qg_ref[...]).astype(BF16)
        kv_ref[:, 0:nkv] = norm(lat_ref[:, nq_:nq_ + nkv], kg_ref[...]).astype(BF16)
        kv_ref[:, nkv:nkv + HEAD_PAD] = lat_ref[:, nq_ + nkv:nq_ + nkv + HEAD_PAD].astype(BF16)

    out = lambda n: pl.BlockSpec((tm, n), lambda i: (i, 0))
    return pl.pallas_call(
        body, name=name, grid=(s // tm,),
        in_specs=[out(w), _vec_spec(nq_, 1), _vec_spec(nkv, 1)],
        out_specs=(out(nq_), out(nkv + HEAD_PAD)),
        out_shape=(jax.ShapeDtypeStruct((s, nq_), BF16), jax.ShapeDtypeStruct((s, nkv + HEAD_PAD), BF16)),
        compiler_params=_params(("parallel",)),
    )(lat, q_g, kv_g)


def _rms_mod_bwd(dh, x, g, sc, dres, name, branch=None):
    s, w = x.shape
    tm = _tile(s, ROW_T)
    has_res, has_br = dres is not None, branch is not None

    def body(*refs):
        dh_ref, x_ref, g_ref, sc_ref = refs[:4]
        rest = list(refs[4:])
        dres_ref = rest.pop(0) if has_res else None
        val_ref, bg_ref = (rest.pop(0), rest.pop(0)) if has_br else (None, None)
        dx_ref = rest.pop(0)
        db_ref = rest.pop(0) if has_br else None
        sums_ref = rest.pop(0)
        xv, dhv, gv = x_ref[...], dh_ref[...], g_ref[...]
        r = lax.rsqrt(jnp.mean(xv * xv, axis=-1, keepdims=True) + EPS)
        xhat = xv * r
        dxn = dhv * (1.0 + sc_ref[...])
        dxhat = dxn * gv
        dx = r * (dxhat - xhat * jnp.mean(dxhat * xhat, axis=-1, keepdims=True))
        if has_res:
            dx = dx + dres_ref[...]
        dx_ref[...] = dx

        @pl.when(pl.program_id(0) == 0)
        def _():
            sums_ref[...] = jnp.zeros_like(sums_ref)

        sums_ref[0:1, :] += jnp.sum(dhv, axis=0, keepdims=True)
        sums_ref[1:2, :] += jnp.sum(dhv * (xhat * gv), axis=0, keepdims=True)
        sums_ref[2:3, :] += jnp.sum(dxn * xhat, axis=0, keepdims=True)
        if has_br:
            db_ref[...] = (dx * bg_ref[...]).astype(db_ref.dtype)
            sums_ref[3:4, :] += jnp.sum(dx * val_ref[...], axis=0, keepdims=True)

    row = pl.BlockSpec((tm, w), lambda i: (i, 0))
    in_specs = [row, row, _vec_spec(w, 1), _vec_spec(w, 1)] + ([row] if has_res else [])
    args = [dh, x, g, sc] + ([dres] if has_res else [])
    out_specs, out_shape = [row], [jax.ShapeDtypeStruct((s, w), F32)]
    if has_br:
        in_specs += [row, _vec_spec(w, 1)]
        args += list(branch)
        out_specs.append(row)
        out_shape.append(jax.ShapeDtypeStruct((s, w), BF16))
    out_specs.append(pl.BlockSpec((8, w), lambda i: (0, 0)))
    out_shape.append(jax.ShapeDtypeStruct((8, w), F32))
    return pl.pallas_call(
        body, name=name, grid=(s // tm,),
        in_specs=in_specs, out_specs=tuple(out_specs), out_shape=tuple(out_shape),
        compiler_params=_params(("arbitrary",)),
    )(*args)


def _final_loss(x3, target, g, ffn, gvec, name):
    s, w = x3.shape
    tm = _tile(s, ROW_T)

    def body(x_ref, t_ref, g_ref, ffn_ref, bg_ref, dx_ref, db_ref, sums_ref):
        xv, gv = x_ref[...], g_ref[...]
        r = lax.rsqrt(jnp.mean(xv * xv, axis=-1, keepdims=True) + EPS)
        xhat = xv * r
        err = xhat * gv - t_ref[...]
        dy = err * (1.0 / w)
        dxhat = dy * gv
        dx = r * (dxhat - xhat * jnp.mean(dxhat * xhat, axis=-1, keepdims=True))
        dx_ref[...] = dx
        db_ref[...] = (dx * bg_ref[...]).astype(db_ref.dtype)

        @pl.when(pl.program_id(0) == 0)
        def _():
            sums_ref[...] = jnp.zeros_like(sums_ref)

        sums_ref[0:1, :] += jnp.sum(dy * xhat, axis=0, keepdims=True)
        sums_ref[1:2, :] += jnp.zeros((1, w), F32) + (0.5 / w) * jnp.sum(err * err)
        sums_ref[2:3, :] += jnp.sum(dx * ffn_ref[...], axis=0, keepdims=True)

    row = pl.BlockSpec((tm, w), lambda i: (i, 0))
    return pl.pallas_call(
        body, name=name, grid=(s // tm,),
        in_specs=[row, row, _vec_spec(w, 1), row, _vec_spec(w, 1)],
        out_specs=(row, row, pl.BlockSpec((8, w), lambda i: (0, 0))),
        out_shape=(jax.ShapeDtypeStruct((s, w), F32), jax.ShapeDtypeStruct((s, w), BF16),
                   jax.ShapeDtypeStruct((8, w), F32)),
        compiler_params=_params(("arbitrary",)),
    )(x3, target, g, ffn, gvec)


def _rope_block(seg, cmul, smul):
    lane = lax.broadcasted_iota(jnp.int32, seg.shape, 1)
    swapped = jnp.where(lane < MLA_NOPE + MLA_ROPE // 2,
                        pltpu.roll(seg, HEAD_PAD - MLA_ROPE // 2, 1), pltpu.roll(seg, MLA_ROPE // 2, 1))
    return seg * cmul + swapped * smul


def _rope(t, cmul, smul, name):
    s, w = t.shape
    tm = _tile(s, ROW_T)

    def body(t_ref, c_ref, s_ref, o_ref):
        cv, sv = c_ref[...], s_ref[...]
        for hb in range(w // HEAD_PAD):
            lanes = slice(hb * HEAD_PAD, (hb + 1) * HEAD_PAD)
            o_ref[:, lanes] = _rope_block(t_ref[:, lanes].astype(F32), cv, sv).astype(o_ref.dtype)

    row = pl.BlockSpec((tm, w), lambda i: (i, 0))
    tab = pl.BlockSpec((tm, HEAD_PAD), lambda i: (i, 0))
    return pl.pallas_call(
        body, name=name, grid=(s // tm,),
        in_specs=[row, tab, tab], out_specs=row, out_shape=jax.ShapeDtypeStruct((s, w), BF16),
        compiler_params=_params(("parallel",)),
    )(t, cmul, smul)


def _rope_bwd_kv(dk, dv, cmul, smul, name):
    s, w = dk.shape
    tm = _tile(s, ROW_T)

    def body(dk_ref, dv_ref, c_ref, s_ref, o_ref):
        cv, sv = c_ref[...], s_ref[...]
        for hb in range(N_HEADS):
            lo, hi = hb * HEAD_PAD, (hb + 1) * HEAD_PAD
            o_ref[:, lo:hi] = _rope_block(dk_ref[:, lo:hi].astype(F32), cv, sv).astype(o_ref.dtype)
        o_ref[:, w:2 * w] = dv_ref[...].astype(o_ref.dtype)

    row = pl.BlockSpec((tm, w), lambda i: (i, 0))
    tab = pl.BlockSpec((tm, HEAD_PAD), lambda i: (i, 0))
    return pl.pallas_call(
        body, name=name, grid=(s // tm,),
        in_specs=[row, row, tab, tab],
        out_specs=pl.BlockSpec((tm, 2 * w), lambda i: (i, 0)),
        out_shape=jax.ShapeDtypeStruct((s, 2 * w), BF16),
        compiler_params=_params(("parallel",)),
    )(dk, dv, cmul, smul)


def _lanes(col, width):
    if col.shape[1] == 1:
        col = jnp.broadcast_to(col, (col.shape[0], HEAD_PAD))
    return jnp.tile(col, (1, width // HEAD_PAD))


def _fold_lanes(a):
    out = a[:, 0:HEAD_PAD]
    for g in range(1, a.shape[1] // HEAD_PAD):
        out = out + a[:, g * HEAD_PAD:(g + 1) * HEAD_PAD]
    return out


def _as_row(rep):
    return rep.T[0:1, :]


def _causal(t, rows_are_queries):
    row = lax.broadcasted_iota(jnp.int32, (t, t), 0)
    col = lax.broadcasted_iota(jnp.int32, (t, t), 1)
    return row >= col if rows_are_queries else col >= row


def _split_refs(refs, n_in, n_out, n_scratch, n_x):
    pos = [n_in, n_x, n_out, n_x, n_scratch, 3 if n_x else 0]
    out, at = [], 0
    for cnt in pos:
        out.append(refs[at:at + cnt])
        at += cnt
    return out


def _first_last_step(n0, n1):
    i0, i1 = pl.program_id(0), pl.program_id(1)
    return jnp.logical_and(i0 == 0, i1 == 0), jnp.logical_and(i0 == n0 - 1, i1 == n1 - 1)


def _as_lanes(row):
    return jnp.broadcast_to(row, (HEAD_PAD, row.shape[1])).T


def _attn_fwd(q, k, v, frow, name, gather=()):
    s = q.shape[0]
    t = ATT_T
    nq = s // t
    use_f = frow is not None
    nx = len(gather)

    hpb = FWD_HEADS_PER_STEP

    def body(*refs):
        ins, x_src, outs, x_out, scr, x_sems = _split_refs(refs, 4 if use_f else 3, 3, 2, nx)
        if use_f:
            q_ref, k_ref, v_ref, fr_ref = ins
            fc_b = [_as_lanes(fr_ref[hh, pl.program_id(1)]) for hh in range(hpb)]
        else:
            q_ref, k_ref, v_ref = ins
        o_ref, ob_ref, lse_ref = outs
        m_s, acc_s = scr
        if nx:
            first, last = _first_last_step(N_HEADS // hpb, nq)
            x_start, x_wait = _direct_exchange(x_src, x_out, *x_sems, scatter=False)
            pl.when(first)(x_start)
        qi = pl.program_id(1)
        m_s[...] = jnp.full(m_s.shape, NEG_BIG, F32)
        acc_s[...] = jnp.zeros(acc_s.shape, F32)

        def step(j, masked):
            off = pl.multiple_of(j * t, t)
            for hh in range(hpb):
                lanes = slice(hh * HEAD_PAD, (hh + 1) * HEAD_PAD)
                kv = k_ref[pl.ds(off, t), lanes]
                vv = v_ref[pl.ds(off, t), lanes]
                sc = lax.dot_general(q_ref[:, lanes], kv, (((1,), (1,)), ((), ())), preferred_element_type=F32)
                if use_f:
                    sc = sc + (_lanes(fc_b[hh], t) - fr_ref[hh, j])
                if masked:
                    sc = jnp.where(_causal(t, True), sc, NEG_BIG)
                m_prev = m_s[hh]
                m_new = jnp.maximum(m_prev, jnp.max(sc, axis=-1, keepdims=True))
                p = jnp.exp2(sc - _lanes(m_new, t))
                acc_s[hh] = jnp.exp2(m_prev - m_new) * acc_s[hh] + jnp.dot(p.astype(BF16), vv,
                                                                           preferred_element_type=F32)
                m_s[hh] = m_new

        def loop_body(j, carry):
            step(j, False)
            return carry

        lax.fori_loop(0, qi, loop_body, 0)
        step(qi, True)
        for hh in range(hpb):
            lanes = slice(hh * HEAD_PAD, (hh + 1) * HEAD_PAD)
            acc = acc_s[hh]
            lane = lax.broadcasted_iota(jnp.int32, acc.shape, 1)
            denom = jnp.sum(jnp.where(lane == SUM_LANE, acc, 0.0), axis=-1, keepdims=True)
            o = acc * (1.0 / denom)
            o_ref[:, lanes] = o
            ob_ref[:, lanes] = o.astype(BF16)
            lse_ref[hh, 0] = _as_row(m_s[hh] + jnp.log(denom) * LOG2E)
        if nx:
            pl.when(last)(x_wait)

    w = hpb * HEAD_PAD
    qspec = pl.BlockSpec((t, w), lambda h, i: (i, h))
    kspec = pl.BlockSpec((s, w), lambda h, i: (0, h))
    any_spec = pl.BlockSpec(memory_space=pl.ANY)
    in_specs, args = [qspec, kspec, kspec], [q, k, v]
    if use_f:
        in_specs += [pl.BlockSpec((hpb, nq, 1, t), lambda h, i: (h, 0, 0, 0))]
        args += [frow]
    out_specs = [qspec, qspec, pl.BlockSpec((hpb, 1, 1, t), lambda h, i: (h, i, 0, 0))]
    out_shape = [jax.ShapeDtypeStruct((s, N_HEADS * HEAD_PAD), F32), jax.ShapeDtypeStruct((s, N_HEADS * HEAD_PAD), BF16),
                 jax.ShapeDtypeStruct((N_HEADS, nq, 1, t), F32)]
    scratch = [pltpu.VMEM((hpb, t, HEAD_PAD), F32), pltpu.VMEM((hpb, t, HEAD_PAD), F32)]
    if nx:
        in_specs += [any_spec] * nx
        args += list(gather)
        out_specs += [any_spec] * nx
        out_shape += [jax.ShapeDtypeStruct((N_DEV,) + g.shape, g.dtype) for g in gather]
        scratch += _exchange_scratch(nx)
    return pl.pallas_call(
        body, name=name, grid=(N_HEADS // hpb, nq),
        in_specs=in_specs, out_specs=tuple(out_specs), out_shape=tuple(out_shape),
        scratch_shapes=scratch,
        compiler_params=_params(("arbitrary", "arbitrary") if nx else ("parallel", "arbitrary")),
    )(*args)


def _attn_delta(o, do, name):
    s, w = o.shape
    t = ATT_T

    def body(o_ref, do_ref, d_ref):
        for hb in range(N_HEADS):
            lo, hi = hb * HEAD_PAD, (hb + 1) * HEAD_PAD
            prod = o_ref[:, lo:hi] * do_ref[:, lo:hi].astype(F32)
            d_ref[hb, 0] = jnp.sum(prod.T, axis=0, keepdims=True)

    row = pl.BlockSpec((t, w), lambda i: (i, 0))
    return pl.pallas_call(
        body, name=name, grid=(s // t,),
        in_specs=[row, row],
        out_specs=pl.BlockSpec((N_HEADS, 1, 1, t), lambda i: (0, i, 0, 0)),
        out_shape=jax.ShapeDtypeStruct((N_HEADS, s // t, 1, t), F32),
        compiler_params=_params(("parallel",)),
    )(o, do)


def _attn_bwd(q, k, v, do, lse_row, delta_row, frow, scale_q, scale_k, name, out_dtype, scatter=()):
    s = q.shape[0]
    t = ATT_T
    nq = s // t
    use_f = frow is not None
    nx = len(scatter)
    hpb = BWD_HEADS_PER_STEP

    def body(*refs):
        ins, x_src, outs, x_out, scr, x_sems = _split_refs(refs, 7 if use_f else 6, 5 if use_f else 3,
                                                           5 if use_f else 3, nx)
        if use_f:
            q_ref, k_ref, v_ref, do_ref, lse_ref, dl_ref, fr_ref = ins
            dq_ref, dk_ref, dv_ref, dr_ref, df_ref = outs
            dq_s, dk_s, dv_s, dr_s, df_s = scr
            fc_b = [_as_lanes(fr_ref[hh, pl.program_id(1)]) for hh in range(hpb)]
        else:
            q_ref, k_ref, v_ref, do_ref, lse_ref, dl_ref = ins
            dq_ref, dk_ref, dv_ref = outs
            dq_s, dk_s, dv_s = scr
        if nx:
            first, last = _first_last_step(N_HEADS // hpb, nq)
            x_start, x_wait = _direct_exchange(x_src, x_out, *x_sems, scatter=True)
            pl.when(first)(x_start)
        kj = pl.program_id(1)

        @pl.when(kj == 0)
        def _():
            dq_s[...] = jnp.zeros(dq_s.shape, F32)
            if use_f:
                dr_s[...] = jnp.zeros(dr_s.shape, F32)

        dk_s[...] = jnp.zeros(dk_s.shape, F32)
        dv_s[...] = jnp.zeros(dv_s.shape, F32)
        if use_f:
            df_s[...] = jnp.zeros(df_s.shape, F32)

        def step(i, masked):
            off = pl.multiple_of(i * t, t)
            for hh in range(hpb):
                lanes = slice(hh * HEAD_PAD, (hh + 1) * HEAD_PAD)
                kv, vv = k_ref[:, lanes], v_ref[:, lanes]
                qv = q_ref[pl.ds(off, t), lanes]
                dov = do_ref[pl.ds(off, t), lanes]
                st = lax.dot_general(kv, qv, (((1,), (1,)), ((), ())), preferred_element_type=F32)
                if use_f:
                    st = st + (fr_ref[hh, i] - _lanes(fc_b[hh], t))
                if masked:
                    st = jnp.where(_causal(t, False), st, NEG_BIG)
                pt = jnp.exp2(st - lse_ref[hh, i])
                dv_s[hh] += jnp.dot(pt.astype(BF16), dov, preferred_element_type=F32)
                dpt = lax.dot_general(vv, dov, (((1,), (1,)), ((), ())), preferred_element_type=F32)
                dst = pt * (dpt - dl_ref[hh, i])
                dsb = dst.astype(BF16)
                dk_s[hh] += jnp.dot(dsb, qv, preferred_element_type=F32)
                dq_s[hh, pl.ds(off, t), :] += lax.dot_general(dsb, kv, (((0,), (0,)), ((), ())),
                                                              preferred_element_type=F32)
                if use_f:
                    df_s[hh] -= _fold_lanes(dst)
                    dr_s[hh, i] += jnp.sum(dst, axis=0, keepdims=True)

        step(kj, True)

        def loop_body(i, carry):
            step(i, False)
            return carry

        lax.fori_loop(kj + 1, nq, loop_body, 0)
        for hh in range(hpb):
            lanes = slice(hh * HEAD_PAD, (hh + 1) * HEAD_PAD)
            dk_ref[:, lanes] = (dk_s[hh] * scale_k).astype(dk_ref.dtype)
            dv_ref[:, lanes] = dv_s[hh].astype(dv_ref.dtype)
            if use_f:
                df_ref[hh, 0] = jnp.sum(df_s[hh].T, axis=0, keepdims=True)

        @pl.when(kj == nq - 1)
        def _():
            for hh in range(hpb):
                dq_ref[:, hh * HEAD_PAD:(hh + 1) * HEAD_PAD] = (dq_s[hh] * scale_q).astype(dq_ref.dtype)
            if use_f:
                dr_ref[...] = dr_s[...]

        if nx:
            pl.when(last)(x_wait)

    w = hpb * HEAD_PAD
    kspec = pl.BlockSpec((t, w), lambda h, j: (j, h))
    qspec = pl.BlockSpec((s, w), lambda h, j: (0, h))
    rowspec = pl.BlockSpec((hpb, nq, 1, t), lambda h, j: (h, 0, 0, 0))
    any_spec = pl.BlockSpec(memory_space=pl.ANY)
    in_specs, args = [qspec, kspec, kspec, qspec, rowspec, rowspec], [q, k, v, do, lse_row, delta_row]
    full = jax.ShapeDtypeStruct((s, N_HEADS * HEAD_PAD), out_dtype)
    out_specs, out_shape = [qspec, kspec, kspec], [full, full, full]
    scratch = [pltpu.VMEM((hpb, s, HEAD_PAD), F32), pltpu.VMEM((hpb, t, HEAD_PAD), F32),
               pltpu.VMEM((hpb, t, HEAD_PAD), F32)]
    if use_f:
        in_specs += [rowspec]
        args += [frow]
        out_specs += [rowspec, pl.BlockSpec((hpb, 1, 1, t), lambda h, j: (h, j, 0, 0))]
        out_shape += [jax.ShapeDtypeStruct((N_HEADS, nq, 1, t), F32)] * 2
        scratch += [pltpu.VMEM((hpb, nq, 1, t), F32), pltpu.VMEM((hpb, t, HEAD_PAD), F32)]
    if nx:
        in_specs += [any_spec] * nx
        args += list(scatter)
        out_specs += [any_spec] * nx
        out_shape += [jax.ShapeDtypeStruct(g.shape, g.dtype) for g in scatter]
        scratch += _exchange_scratch(nx)
    return pl.pallas_call(
        body, name=name, grid=(N_HEADS // hpb, nq),
        in_specs=in_specs, out_specs=tuple(out_specs), out_shape=tuple(out_shape),
        scratch_shapes=scratch,
        compiler_params=_params(("arbitrary", "arbitrary") if nx else ("parallel", "arbitrary")),
    )(*args)


def _gate_fwd(pm, pf, gates, name):
    s, w = pm.shape
    tm = _tile(s, ROW_T)

    def body(pm_ref, pf_ref, g_ref, y_ref):
        y = (jax.nn.sigmoid(g_ref[:, 0:w]) * pm_ref[...].astype(F32)
             + jax.nn.sigmoid(g_ref[:, w:2 * w]) * pf_ref[...].astype(F32))
        y_ref[...] = y.astype(y_ref.dtype)

    row = pl.BlockSpec((tm, w), lambda i: (i, 0))
    return pl.pallas_call(
        body, name=name, grid=(s // tm,),
        in_specs=[row, row, pl.BlockSpec((tm, 2 * w), lambda i: (i, 0))],
        out_specs=row, out_shape=jax.ShapeDtypeStruct((s, w), BF16),
        compiler_params=_params(("parallel",)),
    )(pm, pf, gates)


def _gate_bwd(dy, pm, pf, gates, name):
    s, w = pm.shape
    tm = _tile(s, ROW_T)

    def body(dy_ref, pm_ref, pf_ref, g_ref, dpm_ref, dpf_ref, dg_ref):
        dyv = dy_ref[...]
        sm, sf = jax.nn.sigmoid(g_ref[:, 0:w]), jax.nn.sigmoid(g_ref[:, w:2 * w])
        dpm_ref[...] = (dyv * sm).astype(BF16)
        dpf_ref[...] = (dyv * sf).astype(BF16)
        dg_ref[:, 0:w] = (dyv * pm_ref[...].astype(F32) * (sm * (1.0 - sm))).astype(BF16)
        dg_ref[:, w:2 * w] = (dyv * pf_ref[...].astype(F32) * (sf * (1.0 - sf))).astype(BF16)

    row = pl.BlockSpec((tm, w), lambda i: (i, 0))
    wide = pl.BlockSpec((tm, 2 * w), lambda i: (i, 0))
    out = jax.ShapeDtypeStruct((s, w), BF16)
    return pl.pallas_call(
        body, name=name, grid=(s // tm,),
        in_specs=[row, row, row, wide],
        out_specs=(row, row, wide), out_shape=(out, out, jax.ShapeDtypeStruct((s, 2 * w), BF16)),
        compiler_params=_params(("parallel",)),
    )(dy, pm, pf, gates)


CONV_TN = 256
CONV_TM = 512
HALO = BF16_ROWS


def _shift_down(u, prev, n):
    rolled = pltpu.roll(u, n, 0)
    prev_rolled = pltpu.roll(prev, n, 0)
    top = jnp.concatenate([prev_rolled, rolled[HALO:]], axis=0)
    row = lax.broadcasted_iota(jnp.int32, u.shape, 0)
    return jnp.where(row < n, top, rolled)


def _conv_tile(u, prev, w_ref, b_ref):
    um1 = _shift_down(u, prev, 1)
    um2 = _shift_down(u, prev, 2)
    uc = b_ref[...] + w_ref[0:1, :] * um2 + w_ref[1:2, :] * um1 + w_ref[2:3, :] * u
    return uc, um1, um2


def _conv_specs(tm, tn, ncol_off):
    blk = lambda off: pl.BlockSpec((tm, tn), lambda j, i: (i, j + off))
    halo = lambda off: pl.BlockSpec((HALO, tn), lambda j, i: (jnp.maximum(i * (tm // HALO) - 1, 0), j + off))
    wsp = lambda off: pl.BlockSpec((3, tn), lambda j, i: (0, j + off))
    bsp = lambda off: pl.BlockSpec((1, tn), lambda j, i: (0, j + off))
    return blk, halo, wsp, bsp


def _convglu_fwd(u, conv_w, conv_b, name):
    s = u.shape[0]
    tm, tn = _tile(s, CONV_TM), CONV_TN
    nj = D_FF // tn
    blk, halo, wsp, bsp = _conv_specs(tm, tn, nj)

    def body(ug_ref, pg_ref, uv_ref, pv_ref, wg_ref, wv_ref, bg_ref, bv_ref, a_ref):
        live = (pl.program_id(1) > 0).astype(F32)
        gate, _, _ = _conv_tile(ug_ref[...].astype(F32), pg_ref[...].astype(F32) * live, wg_ref, bg_ref)
        val, _, _ = _conv_tile(uv_ref[...].astype(F32), pv_ref[...].astype(F32) * live, wv_ref, bv_ref)
        a_ref[...] = (gate * jax.nn.sigmoid(gate) * val).astype(a_ref.dtype)

    return pl.pallas_call(
        body, name=name, grid=(nj, s // tm),
        in_specs=[blk(0), halo(0), blk(nj), halo(nj), wsp(0), wsp(nj), bsp(0), bsp(nj)],
        out_specs=blk(0), out_shape=jax.ShapeDtypeStruct((s, D_FF), BF16),
        compiler_params=_params(("parallel", "arbitrary")),
    )(u, u, u, u, conv_w, conv_w, conv_b, conv_b)


def _convglu_bwd(da, u, conv_w, conv_b, name):
    s = u.shape[0]
    tm, tn = _tile(s, CONV_TM), CONV_TN
    nj = D_FF // tn
    blk, halo, wsp, bsp = _conv_specs(tm, tn, nj)

    def body(da_ref, ug_ref, pg_ref, uv_ref, pv_ref, wg_ref, wv_ref, bg_ref, bv_ref,
             dg_ref, dv_ref, sg_ref, sv_ref):
        live = (pl.program_id(1) > 0).astype(F32)
        ug, uv = ug_ref[...].astype(F32), uv_ref[...].astype(F32)
        gate, ug1, ug2 = _conv_tile(ug, pg_ref[...].astype(F32) * live, wg_ref, bg_ref)
        val, uv1, uv2 = _conv_tile(uv, pv_ref[...].astype(F32) * live, wv_ref, bv_ref)
        dav = da_ref[...].astype(F32)
        sig = jax.nn.sigmoid(gate)
        dgate = dav * val * (sig * (1.0 + gate * (1.0 - sig)))
        dval = dav * (gate * sig)
        dg_ref[...] = dgate.astype(dg_ref.dtype)
        dv_ref[...] = dval.astype(dv_ref.dtype)

        @pl.when(pl.program_id(1) == 0)
        def _():
            sg_ref[...] = jnp.zeros_like(sg_ref)
            sv_ref[...] = jnp.zeros_like(sv_ref)

        for s_ref, d, taps in ((sg_ref, dgate, (ug2, ug1, ug)), (sv_ref, dval, (uv2, uv1, uv))):
            for r, tap in enumerate(taps):
                s_ref[r:r + 1, :] += jnp.sum(d * tap, axis=0, keepdims=True)
            s_ref[3:4, :] += jnp.sum(d, axis=0, keepdims=True)

    sums = lambda off: pl.BlockSpec((8, tn), lambda j, i: (0, j + off))
    return pl.pallas_call(
        body, name=name, grid=(nj, s // tm),
        in_specs=[blk(0), blk(0), halo(0), blk(nj), halo(nj), wsp(0), wsp(nj), bsp(0), bsp(nj)],
        out_specs=(blk(0), blk(0), sums(0), sums(0)),
        out_shape=(jax.ShapeDtypeStruct((s, D_FF), BF16), jax.ShapeDtypeStruct((s, D_FF), BF16),
                   jax.ShapeDtypeStruct((8, D_FF), F32), jax.ShapeDtypeStruct((8, D_FF), F32)),
        compiler_params=_params(("parallel", "arbitrary")),
    )(da, u, u, u, u, conv_w, conv_w, conv_b, conv_b)


def _conv_transpose(d, conv_w, name, col0, into=None):
    s, w = d.shape
    tm, tn = _tile(s, CONV_TM), CONV_TN
    last = s // tm - 1
    jo = col0 // tn

    def body(d_ref, nx_ref, w_ref, *rest):
        o_ref = rest[-1]
        dv = d_ref[...].astype(F32)
        nxt = nx_ref[...].astype(F32) * (pl.program_id(1) < last).astype(F32)
        row = lax.broadcasted_iota(jnp.int32, dv.shape, 0)

        def shift_up(n):
            rolled = pltpu.roll(dv, tm - n, 0)
            nxt_rolled = pltpu.roll(nxt, HALO - n, 0)
            bottom = jnp.concatenate([rolled[:tm - HALO], nxt_rolled], axis=0)
            return jnp.where(row >= tm - n, bottom, rolled)

        out = w_ref[2:3, :] * dv + w_ref[1:2, :] * shift_up(1) + w_ref[0:1, :] * shift_up(2)
        o_ref[...] = out.astype(o_ref.dtype)

    blk = pl.BlockSpec((tm, tn), lambda j, i: (i, j))
    nxt_spec = pl.BlockSpec((HALO, tn), lambda j, i: (jnp.minimum((i + 1) * (tm // HALO), s // HALO - 1), j))
    in_specs = [blk, nxt_spec, pl.BlockSpec((3, tn), lambda j, i: (0, j + jo))]
    args = [d, d, conv_w]
    if into is not None:
        in_specs.append(pl.BlockSpec(memory_space=pl.ANY))
        args.append(into)
    return pl.pallas_call(
        body, name=name, grid=(w // tn, s // tm),
        in_specs=in_specs,
        out_specs=pl.BlockSpec((tm, tn), lambda j, i: (i, j + jo)),
        out_shape=jax.ShapeDtypeStruct((s, 2 * D_FF), BF16),
        input_output_aliases={3: 0} if into is not None else {},
        compiler_params=_params(("parallel", "arbitrary")),
    )(*args)


def _split3(a):
    a1 = a.astype(BF16)
    r1 = a - a1.astype(F32)
    a2 = r1.astype(BF16)
    a3 = (r1 - a2.astype(F32)).astype(BF16)
    return a1, a2, a3


def _ones_dot_right(a, mat):
    return sum(jnp.dot(p, mat, preferred_element_type=F32) for p in _split3(a))


def _ones_dot_left(mat, a):
    return sum(jnp.dot(mat, p, preferred_element_type=F32) for p in _split3(a))


def _tri(n, cmp):
    r = lax.broadcasted_iota(jnp.int32, (n, n), 0)
    c = lax.broadcasted_iota(jnp.int32, (n, n), 1)
    return cmp(r, c).astype(BF16)


def _forget_fwd(z, bias, name):
    nh, nr, nl = z.shape

    def body(z_ref, b_ref, f_ref):
        within = _tri(nl, lambda r, c: r <= c)
        before = _tri(nr, lambda r, c: c < r)
        for h in range(nh):
            x = z_ref[h] + b_ref[h]
            lf = jnp.minimum(x, 0.0) - jnp.log(1.0 + jnp.exp(-jnp.abs(x)))
            pre = _ones_dot_right(lf, within)
            tot = jnp.zeros((nr, nl), F32) + jnp.sum(lf, axis=1, keepdims=True)
            f_ref[h] = pre + _ones_dot_left(before, tot)

    return pl.pallas_call(
        body, name=name, out_shape=jax.ShapeDtypeStruct(z.shape, F32),
        compiler_params=pltpu.CompilerParams(vmem_limit_bytes=VMEM_LIMIT_BYTES),
    )(z, bias)


def _forget_bwd(df_rows, df_cols, z, bias, name):
    nh, nr, nl = z.shape

    def body(dfr_ref, dfc_ref, z_ref, b_ref, dz_ref, db_ref):
        within = _tri(nl, lambda r, c: r >= c)
        after = _tri(nr, lambda r, c: c > r)
        for h in range(nh):
            g = dfr_ref[h] + dfc_ref[h]
            suf = _ones_dot_right(g, within)
            tot = jnp.zeros((nr, nl), F32) + jnp.sum(g, axis=1, keepdims=True)
            dlf = suf + _ones_dot_left(after, tot)
            dz = dlf * jax.nn.sigmoid(-(z_ref[h] + b_ref[h]))
            dz_ref[h] = dz
            db_ref[h] = jnp.zeros((1, nl), F32) + jnp.sum(dz)

    return pl.pallas_call(
        body, name=name,
        out_shape=(jax.ShapeDtypeStruct(z.shape, F32), jax.ShapeDtypeStruct(bias.shape, F32)),
        compiler_params=pltpu.CompilerParams(vmem_limit_bytes=VMEM_LIMIT_BYTES),
    )(df_rows, df_cols, z, bias)


def _ada_fwd(c_col, w, b, name):
    kdim, n = w.shape

    def body(c_ref, w_ref, b_ref, ada_ref, act_ref):
        wv = w_ref[...]
        for e in range(N_DEV):
            cv = c_ref[e]
            act = cv * jax.nn.sigmoid(cv)
            act_ref[e] = act
            ada_ref[e:e + 1, :] = jnp.sum(act * wv, axis=0, keepdims=True) + b_ref[...]

    return pl.pallas_call(
        body, name=name,
        out_shape=(jax.ShapeDtypeStruct((N_DEV, n), F32), jax.ShapeDtypeStruct((N_DEV, kdim, 1), F32)),
        compiler_params=pltpu.CompilerParams(vmem_limit_bytes=VMEM_LIMIT_BYTES),
    )(c_col, w, b)


def _ada_bwd(act_col, dada, name):
    kdim = act_col.shape[1]
    n = dada.shape[1]

    def body(act_ref, d_ref, g_ref):
        acc = act_ref[0] * d_ref[0:1, :]
        for e in range(1, N_DEV):
            acc = acc + act_ref[e] * d_ref[e:e + 1, :]
        g_ref[...] = acc

    return pl.pallas_call(
        body, name=name, out_shape=jax.ShapeDtypeStruct((kdim, n), F32),
        compiler_params=pltpu.CompilerParams(vmem_limit_bytes=VMEM_LIMIT_BYTES),
    )(act_col, dada)


def _adamw(parts, w, m, v, name, tr=128):
    npart, r, c = parts.shape
    tr = _tile(r, tr, step=BF16_ROWS) if r % BF16_ROWS == 0 else r

    def body(p_ref, w_ref, m_ref, v_ref, g_ref, d_ref, nm_ref, nv_ref):
        g = p_ref[0].astype(F32)
        for e in range(1, npart):
            g = g + p_ref[e].astype(F32)
        nm = ADAM_B1 * m_ref[...] + (1.0 - ADAM_B1) * g
        nv = ADAM_B2 * v_ref[...] + (1.0 - ADAM_B2) * (g * g)
        m_hat = nm / (1.0 - ADAM_B1 ** ADAM_STEP)
        v_hat = nv / (1.0 - ADAM_B2 ** ADAM_STEP)
        g_ref[...] = g
        d_ref[...] = -ADAM_LR * (m_hat / (jnp.sqrt(v_hat) + ADAM_EPS) + ADAM_WD * w_ref[...])
        nm_ref[...] = nm
        nv_ref[...] = nv

    row = pl.BlockSpec((tr, c), lambda i: (i, 0))
    out = jax.ShapeDtypeStruct((r, c), F32)
    return pl.pallas_call(
        body, name=name, grid=(r // tr,),
        in_specs=[pl.BlockSpec((npart, tr, c), lambda i: (0, i, 0)), row, row, row],
        out_specs=(row, row, row, row), out_shape=(out, out, out, out),
        compiler_params=_params(("parallel",)),
    )(parts, w, m, v)


EARLY = ("w_in", "w_uq", "w_ukv")
LATE = ("w_o_mla", "w_o_fox", "w_out", "w_up", "conv_w", "w_down")
BIG = EARLY + LATE


def _cols_to_full(stack):
    n, r, c = stack.shape
    return stack.transpose(1, 0, 2).reshape(r, n * c)


def _full_to_cols(full, c):
    r = full.shape[0]
    return full.reshape(r, N_DEV, c).transpose(1, 0, 2)


def _pad_heads(a, width):
    r = a.shape[0]
    a = a.reshape(r, N_HEADS, width)
    return jnp.pad(a, ((0, 0), (0, 0), (0, HEAD_PAD - width))).reshape(r, N_HEADS * HEAD_PAD)


def _unpad_heads(a, width):
    s = a.shape[0]
    return a.reshape(s, N_HEADS, HEAD_PAD)[:, :, :width].reshape(s, N_HEADS * width)


def _w_in_padded(w_in):
    seg = [w_in[:, IN_OFF[i]:IN_OFF[i + 1]] for i in range(9)]
    cq, ckv, kr, fq, fk, fv, fl, gm, gf = seg
    padc = lambda a, n: jnp.pad(a, ((0, 0), (0, n - a.shape[1])))
    return jnp.concatenate([gm, gf, fq, fk, fv, cq, ckv, padc(kr, 128), padc(fl, 128)], axis=1)


def _w_in_unpadded(g):
    return jnp.concatenate([
        g[:, P_CQ:P_CQ + 384], g[:, P_CKV:P_CKV + 256], g[:, P_KR:P_KR + 32], g[:, P_FQ:P_FQ + 512],
        g[:, P_FK:P_FK + 512], g[:, P_FV:P_FV + 512], g[:, P_FL:P_FL + 8], g[:, P_GM:P_GM + 1024],
        g[:, P_GF:P_GF + 1024]], axis=1)


SMALL = (("b_ada", 6144, 6144), ("norm_mix_g", 1024, 1024), ("q_norm_g", 384, 384), ("kv_norm_g", 256, 256),
         ("b_forget", 8, 128), ("norm_ffn_g", 1024, 1024), ("conv_b", 5632, 5632), ("norm_final_g", 1024, 1024),
         ("loss", 1, 128))
SMALL_OFF = {}
_o = 0
for _n, _real, _padded in SMALL:
    SMALL_OFF[_n] = _o
    _o += _padded
SMALL_W = _o


def _pack_small(vals):
    parts = []
    for nme, real, padded in SMALL:
        a = vals[nme].reshape(1, real).astype(F32)
        parts.append(jnp.pad(a, ((0, 0), (0, padded - real))))
    return jnp.concatenate(parts, axis=1)


def kernel(x, c, positions, w_ada, b_ada, norm_mix_g, w_in, q_norm_g, w_uq, kv_norm_g, w_ukv, b_forget, w_o_mla, w_o_fox, w_out, norm_ffn_g, w_up, conv_w, conv_b, w_down, norm_final_g, loss_target, m_w_ada, m_b_ada, m_norm_mix_g, m_w_in, m_q_norm_g, m_w_uq, m_kv_norm_g, m_w_ukv, m_b_forget, m_w_o_mla, m_w_o_fox, m_w_out, m_norm_ffn_g, m_w_up, m_conv_w, m_conv_b, m_w_down, m_norm_final_g, v_w_ada, v_b_ada, v_norm_mix_g, v_w_in, v_q_norm_g, v_w_uq, v_kv_norm_g, v_w_ukv, v_b_forget, v_w_o_mla, v_w_o_fox, v_w_out, v_norm_ffn_g, v_w_up, v_conv_w, v_conv_b, v_w_down, v_norm_final_g):
    me = 4 * lax.axis_index("x") + 2 * lax.axis_index("y") + lax.axis_index("c")
    x = x[0]
    target = loss_target[0]
    s = x.shape[0]
    nblk = s // ATT_T
    big_w = {"w_in": w_in, "w_uq": w_uq, "w_ukv": w_ukv, "w_o_mla": w_o_mla, "w_o_fox": w_o_fox,
             "w_out": w_out, "w_up": w_up, "conv_w": conv_w, "w_down": w_down}
    big_m = {"w_in": m_w_in, "w_uq": m_w_uq, "w_ukv": m_w_ukv, "w_o_mla": m_w_o_mla, "w_o_fox": m_w_o_fox,
             "w_out": m_w_out, "w_up": m_w_up, "conv_w": m_conv_w, "w_down": m_w_down}
    big_v = {"w_in": v_w_in, "w_uq": v_w_uq, "w_ukv": v_w_ukv, "w_o_mla": v_w_o_mla, "w_o_fox": v_w_o_fox,
             "w_out": v_w_out, "w_up": v_w_up, "conv_w": v_conv_w, "w_down": v_w_down}

    shard = lambda k: big_w[k][0] if k == "conv_w" else big_w[k][0].astype(BF16)
    st = dict(zip(EARLY, _all_gather([shard(k) for k in EARLY], "gather_weights")))
    w_in_p = _w_in_padded(_cols_to_full(st["w_in"]))
    uq = st["w_uq"]
    w_uq_p = jnp.pad(uq, ((0, 0), (0, 0), (0, HEAD_PAD - 96))).transpose(1, 0, 2).reshape(MLA_Q_RANK, 1024)
    ukv = st["w_ukv"]
    zeros64 = jnp.zeros((N_HEADS, MLA_KV_RANK, 64), BF16)
    w_uk_p = jnp.concatenate([ukv[:, :, :64], zeros64], axis=2).transpose(1, 0, 2).reshape(MLA_KV_RANK, 1024)
    w_uv_p = jnp.concatenate([ukv[:, :, 64:], zeros64], axis=2).transpose(1, 0, 2).reshape(MLA_KV_RANK, 1024)
    place = np.zeros((HEAD_PAD, N_HEADS, HEAD_PAD), np.float32)
    for j in range(MLA_ROPE):
        place[j, :, MLA_NOPE + j] = 1.0
    place = jnp.asarray(place.reshape(HEAD_PAD, 1024), BF16)
    w_kv_comb = jnp.concatenate([
        jnp.concatenate([w_uk_p, w_uv_p], axis=1),
        jnp.concatenate([place, jnp.zeros((HEAD_PAD, 1024), BF16)], axis=1)], axis=0)

    (c_all,) = _all_gather([c], "gather_c")
    b_ada_mine = lax.dynamic_slice(b_ada, (0, me * 768), (1, 768))
    ada_cols, act_col = _ada_fwd(c_all.reshape(N_DEV, D_MODEL, 1), w_ada[0], b_ada_mine, "ada_fwd")
    (ada_all,) = _all_gather([ada_cols], "gather_ada")
    ada = lax.dynamic_slice(ada_all, (0, me, 0), (N_DEV, 1, 768)).reshape(1, N_ADA * D_MODEL)
    sh_m, sc_m, g_m, sh_f, sc_f, g_f = [ada[:, i * D_MODEL:(i + 1) * D_MODEL] for i in range(N_ADA)]

    inv_freq = ROPE_THETA ** (-jnp.arange(0, MLA_ROPE, 2, dtype=F32) / MLA_ROPE)
    ang = positions[0].astype(F32)[:, None] * inv_freq
    cos, sin = jnp.cos(ang), jnp.sin(ang)
    rope_c = jnp.concatenate([jnp.ones((s, 64), F32), cos, cos, jnp.zeros((s, 32), F32)], axis=1)
    rope_s = jnp.concatenate([jnp.zeros((s, 64), F32), -sin, sin, jnp.zeros((s, 32), F32)], axis=1)

    h1 = _rms_mod(x, norm_mix_g, sc_m, sh_m, "norm_mix")
    gates = _mm(h1, w_in_p[:, P_GM:P_FQ], "nn", F32, "proj_gates", tn=1024)
    fq = _mm(h1, _pad_heads(w_in_p[:, P_FQ:P_FK], 64), "nn", BF16, "proj_fq", tn=1024,
             gvec=jnp.full((1, 1024), FOX_SCALE * LOG2E, F32))
    fk = _mm(h1, _pad_heads(w_in_p[:, P_FK:P_FV], 64), "nn", BF16, "proj_fk", tn=1024)
    fv = _mm(h1, _pad_heads(w_in_p[:, P_FV:P_CQ], 64), "nn", BF16, "proj_fv", tn=1024, ones_lane=True)
    lat =_mm(h1, w_in_p[:, P_CQ:], "nn", F32, "proj_latent", tn=D_IN_P - P_CQ)
    cq = lat[:, 0:384]
    ckv = lat[:, P_CKV - P_CQ:P_CKV - P_CQ + 256]
    qn, kv_in = _latent_norm(lat, q_norm_g, kv_norm_g, "latent_norm")
    q_fold = MLA_SCALE * LOG2E
    q_att = _mm(qn, w_uq_p, "nn", BF16, "q_up", rope=(rope_c * q_fold, rope_s * q_fold))
    k_att = _mm(kv_in, w_kv_comb[:, :1024], "nn", BF16, "k_up", rope=(rope_c, rope_s))
    v_att = _mm(kv_in, w_kv_comb[:, 1024:], "nn", BF16, "v_up", ones_lane=True)
    o_mla, o_mla_b, lse_mla, *late = _attn_fwd(q_att, k_att, v_att, None, "mla_fwd",
                                               gather=[shard(k) for k in LATE])
    st.update(zip(LATE, late))
    pad_o = lambda full: jnp.pad(full.reshape(N_HEADS, 64, 1024), ((0, 0), (0, 64), (0, 0))).reshape(1024, 1024)
    w_o_mla_p = pad_o(_cols_to_full(st["w_o_mla"]))
    w_o_fox_p = pad_o(_cols_to_full(st["w_o_fox"]))
    w_out_f = st["w_out"].reshape(1024, 1024)
    w_up_f = _cols_to_full(st["w_up"])
    conv_w_f = _cols_to_full(st["conv_w"])
    w_down_f = st["w_down"].reshape(D_FF, 1024)

    z = lat[:, P_FL - P_CQ:P_FL - P_CQ + 8].T.reshape(N_HEADS, s // SEQ_LANES, SEQ_LANES)
    bias_f = jnp.broadcast_to(b_forget.reshape(N_HEADS, 1, 1), (N_HEADS, 1, SEQ_LANES))
    f_cum = _forget_fwd(z, bias_f, "forget_fwd")
    f_row = (f_cum * LOG2E).reshape(N_HEADS, nblk, 1, ATT_T)
    o_fox, o_fox_b, lse_fox = _attn_fwd(fq, fk, fv, f_row, "fox_fwd")

    pm = _mm(o_mla_b, w_o_mla_p, "nn", BF16, "o_mla_proj")
    pf = _mm(o_fox_b, w_o_fox_p, "nn", BF16, "o_fox_proj")
    y = _gate_fwd(pm, pf, gates, "gate_fwd")
    x2, mix = _mm(y, w_out_f, "nn", F32, "out_proj", res=x, gvec=g_m)

    h2 = _rms_mod(x2, norm_ffn_g, sc_f, sh_f, "norm_ffn")
    u = _mm(h2, w_up_f, "nn", BF16, "ffn_up")
    a = _convglu_fwd(u, conv_w_f, conv_b, "convglu_fwd")
    x3, ffn = _mm(a, w_down_f, "nn", F32, "ffn_down", res=x2, gvec=g_f, tk=2816)

    dx3, dffn, sums_final = _final_loss(x3, target, norm_final_g.reshape(1, D_MODEL), ffn, g_f, "final_loss")
    da = _mm(dffn, w_down_f, "nt", BF16, "ffn_down_dx", tn=1408)
    g_w_down = _mm(a, dffn, "tn", F32, "ffn_down_dw", tm=256, tn=1024, tk=s)
    dgate, dval, s_gate, s_val = _convglu_bwd(da, u, conv_w_f, conv_b, "convglu_bwd")
    du = _conv_transpose(dgate, conv_w_f, "conv_t_gate", 0)
    du = _conv_transpose(dval, conv_w_f, "conv_t_val", D_FF, into=du)
    dh2 = _mm(du, w_up_f, "nt", F32, "ffn_up_dx", tn=512, tk=2 * D_FF)
    g_w_up = _mm(h2, du, "tn", F32, "ffn_up_dw", tn=256, tk=s)
    dx2, dmix, sums_ffn = _rms_mod_bwd(dh2, x2, norm_ffn_g, sc_f, dx3, "norm_ffn_bwd", branch=(mix, g_m))

    dy = _mm(dmix, w_out_f, "nt", F32, "out_proj_dx")
    g_w_out = _mm(y, dmix, "tn", F32, "out_proj_dw", tn=256, tk=s)
    dpm, dpf, dgates = _gate_bwd(dy, pm, pf, gates, "gate_bwd")
    do_mla_b = _mm(dpm, w_o_mla_p, "nt", BF16, "o_mla_dx", tn=1024)
    do_fox_b = _mm(dpf, w_o_fox_p, "nt", BF16, "o_fox_dx", tn=1024)
    g_w_o_mla_p = _mm(o_mla_b, dpm, "tn", F32, "o_mla_dw", tn=256, tk=s)
    g_w_o_fox_p = _mm(o_fox_b, dpf, "tn", F32, "o_fox_dw", tn=256, tk=s)

    unpad_o = lambda g: g.reshape(N_HEADS, HEAD_PAD, 1024)[:, :64].reshape(512, 1024)
    g_conv_w = jnp.concatenate([s_gate[0:3], s_val[0:3]], axis=1)
    g_blocks = {
        "w_o_mla": _full_to_cols(unpad_o(g_w_o_mla_p), 128), "w_o_fox": _full_to_cols(unpad_o(g_w_o_fox_p), 128),
        "w_out": g_w_out.reshape(N_DEV, 128, 1024), "w_up": _full_to_cols(g_w_up, 704),
        "conv_w": _full_to_cols(g_conv_w, 704), "w_down": g_w_down.reshape(N_DEV, 352, 1024)}

    delta_mla = _attn_delta(o_mla, do_mla_b, "mla_delta")
    dq_rot, dk_rot, dv_mla, *late_recv = _attn_bwd(
        q_att, k_att, v_att, do_mla_b, lse_mla, delta_mla, None, MLA_SCALE, 1.0 / LOG2E, "mla_bwd", BF16,
        scatter=[g_blocks[k].astype(BF16) for k in LATE])
    dq_pre = _rope(dq_rot, rope_c, -rope_s, "rope_q_bwd")
    dkv_pre = _rope_bwd_kv(dk_rot, dv_mla, rope_c, -rope_s, "rope_kv_bwd")
    dqn = _mm(dq_pre, w_uq_p, "nt", F32, "q_up_dx")
    g_w_uq_p = _mm(qn, dq_pre, "tn", F32, "q_up_dw", tk=s)
    dkv_in = _mm(dkv_pre, w_kv_comb, "nt", F32, "kv_up_dx")
    g_w_kv_comb = _mm(kv_in, dkv_pre, "tn", F32, "kv_up_dw", tk=s)
    dcq, sums_q = _rms_mod_bwd(dqn, cq, q_norm_g, jnp.zeros((1, 384), F32), None, "q_norm_bwd")
    dckv, sums_kv = _rms_mod_bwd(dkv_in[:, :256], ckv, kv_norm_g, jnp.zeros((1, 256), F32), None, "kv_norm_bwd")
    delta_fox = _attn_delta(o_fox, do_fox_b, "fox_delta")
    dfq, dfk, dfv, dfr, dfc = _attn_bwd(fq, fk, fv, do_fox_b, lse_fox, delta_fox, f_row,
                                        FOX_SCALE, 1.0 / LOG2E, "fox_bwd", BF16)
    df_rows = dfr.reshape(N_HEADS, s // SEQ_LANES, SEQ_LANES)
    df_cols = dfc.reshape(N_HEADS, s // SEQ_LANES, SEQ_LANES)
    dz, db_f = _forget_bwd(df_rows, df_cols, z, bias_f, "forget_bwd")
    dfl = jnp.pad(dz.reshape(N_HEADS, s).T, ((0, 0), (0, 128 - N_HEADS)))

    dproj = jnp.concatenate([
        dgates, _unpad_heads(dfq, 64), _unpad_heads(dfk, 64), _unpad_heads(dfv, 64),
        dcq.astype(BF16), dckv.astype(BF16), dkv_in[:, 256:384].astype(BF16), dfl.astype(BF16)], axis=1)
    g_w_in_p = _mm(h1, dproj, "tn", F32, "proj_in_dw", tm=512, tn=640, tk=s)

    g_w_in = _w_in_unpadded(g_w_in_p)
    g_uq = g_w_uq_p.reshape(MLA_Q_RANK, N_HEADS, HEAD_PAD)[:, :, :96].transpose(1, 0, 2)
    g_uk = g_w_kv_comb[:256, :1024].reshape(256, N_HEADS, HEAD_PAD)[:, :, :64]
    g_uv = g_w_kv_comb[:256, 1024:].reshape(256, N_HEADS, HEAD_PAD)[:, :, :64]
    g_ukv = jnp.concatenate([g_uk, g_uv], axis=2).transpose(1, 0, 2)
    g_blocks.update({"w_in": _full_to_cols(g_w_in, 533), "w_uq": g_uq, "w_ukv": g_ukv})
    dh1, *early_recv = _mm(dproj, w_in_p, "nt", F32, "proj_in_dx", tn=512, tk=D_IN_P,
                           scatter=[g_blocks[k].astype(BF16) for k in EARLY])
    grad_x, sums_mix = _rms_mod_bwd(dh1, x, norm_mix_g, sc_m, dx2, "norm_mix_bwd")
    g_big, d_big, nm_big, nv_big = {}, {}, {}, {}
    for k, parts in zip(BIG, list(early_recv) + list(late_recv)):
        g_big[k], d_big[k], nm_big[k], nv_big[k] = [
            t[None] for t in _adamw(parts, big_w[k][0], big_m[k][0], big_v[k][0], "adamw_" + k)]

    dada = jnp.concatenate([sums_mix[0:1], sums_mix[1:2], sums_ffn[3:4], sums_ffn[0:1], sums_ffn[1:2], sums_final[2:3]],
                           axis=1)
    small_part = _pack_small({
        "b_ada": dada, "norm_mix_g": sums_mix[2:3], "q_norm_g": sums_q[2:3], "kv_norm_g": sums_kv[2:3],
        "b_forget": db_f[:, 0, 0], "norm_ffn_g": sums_ffn[2:3],
        "conv_b": jnp.concatenate([s_gate[3:4], s_val[3:4]], axis=1), "norm_final_g": sums_final[0:1],
        "loss": sums_final[1:2, 0:1]})
    (small_all,) = _all_gather([small_part], "gather_small")
    zero1 = jnp.zeros((1,), F32)
    small_w = {"b_ada": b_ada, "norm_mix_g": norm_mix_g, "q_norm_g": q_norm_g, "kv_norm_g": kv_norm_g,
               "b_forget": b_forget, "norm_ffn_g": norm_ffn_g, "conv_b": conv_b, "norm_final_g": norm_final_g,
               "loss": zero1}
    small_m = {"b_ada": m_b_ada, "norm_mix_g": m_norm_mix_g, "q_norm_g": m_q_norm_g, "kv_norm_g": m_kv_norm_g,
               "b_forget": m_b_forget, "norm_ffn_g": m_norm_ffn_g, "conv_b": m_conv_b,
               "norm_final_g": m_norm_final_g, "loss": zero1}
    small_v = {"b_ada": v_b_ada, "norm_mix_g": v_norm_mix_g, "q_norm_g": v_q_norm_g, "kv_norm_g": v_kv_norm_g,
               "b_forget": v_b_forget, "norm_ffn_g": v_norm_ffn_g, "conv_b": v_conv_b,
               "norm_final_g": v_norm_final_g, "loss": zero1}
    g_sm, d_sm, nm_sm, nv_sm = _adamw(small_all, _pack_small(small_w), _pack_small(small_m), _pack_small(small_v),
                                      "adamw_small")
    loss = g_sm[0, SMALL_OFF["loss"]]

    dada_all = small_all[:, 0, SMALL_OFF["b_ada"]:SMALL_OFF["b_ada"] + N_ADA * D_MODEL]
    dada_mine = lax.dynamic_slice(dada_all, (0, me * 768), (N_DEV, 768))
    g_ada_local = _ada_bwd(act_col, dada_mine, "ada_bwd")
    g_ada, d_ada, nm_ada, nv_ada = _adamw(g_ada_local[None], w_ada[0], m_w_ada[0], v_w_ada[0], "adamw_ada")

    def small_out(t, nme, shape):
        real = dict((n_, r_) for n_, r_, _ in SMALL)[nme]
        o = SMALL_OFF[nme]
        return t[0, o:o + real].reshape(shape)

    order = ["w_ada", "b_ada", "norm_mix_g", "w_in", "q_norm_g", "w_uq", "kv_norm_g", "w_ukv", "b_forget",
             "w_o_mla", "w_o_fox", "w_out", "norm_ffn_g", "w_up", "conv_w", "conv_b", "w_down", "norm_final_g"]
    small_shapes = {"b_ada": (1, 6144), "norm_mix_g": (1, 1024), "q_norm_g": (1, 384), "kv_norm_g": (1, 256),
                    "b_forget": (1, 8), "norm_ffn_g": (1, 1024), "conv_b": (1, 5632), "norm_final_g": (1024,)}

    def family(big, small, ada_t):
        out = []
        for nme in order:
            if nme == "w_ada":
                out.append(ada_t[None])
            elif nme in small_shapes:
                out.append(small_out(small, nme, small_shapes[nme]))
            else:
                out.append(big[nme])
        return out

    return (loss, grad_x[None], *family(g_big, g_sm, g_ada), *family(d_big, d_sm, d_ada),
            *family(nm_big, nm_sm, nm_ada), *family(nv_big, nv_sm, nv_ada))
```

```python
import math

import numpy as np
import jax
import jax.numpy as jnp
from jax import lax
from jax.experimental import pallas as pl
from jax.experimental.pallas import tpu as pltpu

F32 = jnp.float32
BF16 = jnp.bfloat16

N_DEV = 8
D_MODEL = 1024
N_HEADS = 8
HEAD_PAD = 128
MLA_Q_RANK = 384
MLA_KV_RANK = 256
MLA_NOPE = 64
MLA_ROPE = 32
MLA_V = 64
FOX_DIM = 64
D_FF = 2816
N_ADA = 6
EPS = 1e-6
ROPE_THETA = 10000.0
MLA_SCALE = 1.0 / math.sqrt(MLA_NOPE + MLA_ROPE)
FOX_SCALE = 1.0 / math.sqrt(FOX_DIM)
IN_SPLITS = (384, 256, 32, 512, 512, 512, 8, 1024, 1024)
D_IN = sum(IN_SPLITS)
IN_OFF = tuple(int(v) for v in np.cumsum((0,) + IN_SPLITS))
P_GM, P_GF, P_FQ, P_FK, P_FV, P_CQ, P_CKV, P_KR, P_FL, D_IN_P = 0, 1024, 2048, 2560, 3072, 3584, 3968, 4224, 4352, 4480

ADAM_LR, ADAM_B1, ADAM_B2, ADAM_EPS, ADAM_WD, ADAM_STEP = 0.001, 0.9, 0.999, 1e-08, 0.01, 10

VMEM_LIMIT_BYTES = 56 * 1024 * 1024
NEG_BIG = -1e30
ATT_T = 512
LOG2E = 1.4426950408889634
SUM_LANE = 64
ROW_T = 512
SEQ_LANES = 128
BF16_ROWS = 16
FWD_HEADS_PER_STEP = 4
BWD_HEADS_PER_STEP = 2


def _params(sem):
    return pltpu.CompilerParams(dimension_semantics=sem, vmem_limit_bytes=VMEM_LIMIT_BYTES)


def _tile(n, target, step=128):
    if n <= target:
        return n
    t = (target // step) * step
    while t >= step:
        if n % t == 0:
            return t
        t -= step
    return n


def _vec_spec(w, nargs):
    if nargs == 1:
        return pl.BlockSpec((1, w), lambda i: (0, 0))
    return pl.BlockSpec((1, w), lambda i, j: (0, 0))


def _comm_call(body, name, ins, out_shapes):
    n = len(ins)
    any_spec = pl.BlockSpec(memory_space=pl.ANY)
    return pl.pallas_call(
        body, name=name, out_shape=tuple(out_shapes),
        in_specs=[any_spec] * n, out_specs=tuple([any_spec] * n),
        scratch_shapes=[pltpu.SemaphoreType.DMA((n, 7)), pltpu.SemaphoreType.DMA((n, 7)),
                        pltpu.SemaphoreType.DMA((n,))],
    )(*ins)


def _all_gather(xs, name):
    n = len(xs)

    def body(*refs):
        x_refs, out_refs = refs[:n], refs[n:2 * n]
        send_sems, recv_sems, local_sems = refs[2 * n:]
        x_, y_, c_ = lax.axis_index("x"), lax.axis_index("y"), lax.axis_index("c")
        me, sibling = (x_, y_, c_), (x_, y_, 1 - c_)
        chips = [(1 - x_, y_), (x_, 1 - y_), (1 - x_, 1 - y_)]

        def slot(a, px, py, pc):
            return out_refs[a].at[4 * px + 2 * py + pc]

        def copy(a, k, block, to, src=None):
            return pltpu.make_async_remote_copy(
                src_ref=slot(a, *block) if src is None else src, dst_ref=slot(a, *block),
                send_sem=send_sems.at[a, k], recv_sem=recv_sems.at[a, k],
                device_id=to, device_id_type=pl.DeviceIdType.MESH)

        mine = [pltpu.make_async_copy(x_refs[a], slot(a, *me), local_sems.at[a]) for a in range(n)]
        for cp in mine:
            cp.start()
        first = []
        for a in range(n):
            first.append(copy(a, 0, me, sibling, src=x_refs[a]))
            first += [copy(a, 1 + j, me, (*chip, c_), src=x_refs[a]) for j, chip in enumerate(chips)]
        for cp in first:
            cp.start()
        passed = []
        for j, chip in enumerate(chips):
            for a in range(n):
                copy(a, 1 + j, (*chip, c_), me).wait_recv()
                passed.append(copy(a, 4 + j, (*chip, c_), sibling))
                passed[-1].start()
        for a in range(n):
            copy(a, 0, sibling, me).wait_recv()
            for j, chip in enumerate(chips):
                copy(a, 4 + j, (*chip, 1 - c_), me).wait_recv()
        for cp in first + passed:
            cp.wait_send()
        for cp in mine:
            cp.wait()

    return _comm_call(body, name, xs, [jax.ShapeDtypeStruct((N_DEV,) + x.shape, x.dtype) for x in xs])


def _direct_exchange(src_refs, out_refs, send_sems, recv_sems, local_sems, scatter):
    n = len(src_refs)
    x_, y_, c_ = lax.axis_index("x"), lax.axis_index("y"), lax.axis_index("c")
    me = 4 * x_ + 2 * y_ + c_

    def peer(k):
        return (x_ ^ ((k >> 2) & 1), y_ ^ ((k >> 1) & 1), c_ ^ (k & 1))

    def src(a, slot):
        return src_refs[a].at[slot] if scatter else src_refs[a]

    def copy(a, k, sending):
        px, py, pc = peer(k)
        theirs = 4 * px + 2 * py + pc
        return pltpu.make_async_remote_copy(
            src_ref=src(a, theirs if sending else me), dst_ref=out_refs[a].at[me if sending else theirs],
            send_sem=send_sems.at[a, k - 1], recv_sem=recv_sems.at[a, k - 1],
            device_id=(px, py, pc), device_id_type=pl.DeviceIdType.MESH)

    mine = [pltpu.make_async_copy(src(a, me), out_refs[a].at[me], local_sems.at[a]) for a in range(n)]
    sends = [copy(a, k, True) for a in range(n) for k in range(1, N_DEV)]

    def start():
        for cp in mine + sends:
            cp.start()

    def wait():
        for a in range(n):
            for k in range(1, N_DEV):
                copy(a, k, False).wait_recv()
        for cp in sends:
            cp.wait_send()
        for cp in mine:
            cp.wait()

    return start, wait


def _exchange_scratch(n):
    return [pltpu.SemaphoreType.DMA((n, 7)), pltpu.SemaphoreType.DMA((n, 7)), pltpu.SemaphoreType.DMA((n,))]


def _mm(a, b, mode, out_dtype, name, res=None, gvec=None, tm=1024, tn=512, tk=1024, rope=None, ones_lane=False,
        scatter=()):
    (k, m) = a.shape if mode == "tn" else a.shape[::-1]
    n = b.shape[0] if mode == "nt" else b.shape[1]
    tm, tn, tk = _tile(m, tm), _tile(n, tn), _tile(k, tk)
    nk = k // tk
    dims = {"nn": (((1,), (0,)), ((), ())), "nt": (((1,), (1,)), ((), ())), "tn": (((0,), (0,)), ((), ()))}[mode]
    has_res, has_g = res is not None, gvec is not None
    fused = has_res and has_g

    n_rope = 2 if rope is not None else 0
    nx = len(scatter)
    assert not nx or nk == 1, "the exchange rides only on a matmul with one K step"
    n_in = 2 + has_res + has_g + n_rope

    def body(*refs):
        acc_ref = refs[-1] if nk > 1 else None
        if nx:
            ins, x_src, outs, x_out, _, x_sems = _split_refs(refs, n_in, 1 + fused, 0, nx)
            refs = list(ins) + list(outs)
            i, j = pl.program_id(0), pl.program_id(1)
            x_start, x_wait = _direct_exchange(x_src, x_out, *x_sems, scatter=True)
            pl.when(jnp.logical_and(i == 0, j == 0))(x_start)
        else:
            refs = list(refs[:n_in + 1 + fused])
        a_ref, b_ref = refs[:2]
        res_ref = refs[2] if has_res else None
        g_ref = refs[2 + has_res] if has_g else None
        o_ref = refs[n_in]
        part = lax.dot_general(a_ref[...], b_ref[...], dims, preferred_element_type=F32)

        def finish(acc):
            if fused:
                refs[-1][...] = acc
            out = g_ref[...] * acc if has_g else acc
            if has_res:
                out = res_ref[...] + out
            if n_rope or ones_lane:
                one = (lax.broadcasted_iota(jnp.int32, (tm, HEAD_PAD), 1) == SUM_LANE).astype(F32)
                for hb in range(tn // HEAD_PAD):
                    lanes = slice(hb * HEAD_PAD, (hb + 1) * HEAD_PAD)
                    seg = out[:, lanes]
                    seg = _rope_block(seg, refs[n_in - 2][...], refs[n_in - 1][...]) if n_rope else seg + one
                    o_ref[:, lanes] = seg.astype(o_ref.dtype)
            else:
                o_ref[...] = out.astype(o_ref.dtype)

        if nk == 1:
            finish(part)
            if nx:
                pl.when(jnp.logical_and(i == m // tm - 1, j == n // tn - 1))(x_wait)
            return
        kk = pl.program_id(2)

        @pl.when(kk == 0)
        def _():
            acc_ref[...] = part

        @pl.when(kk > 0)
        def _():
            acc_ref[...] += part

        @pl.when(kk == nk - 1)
        def _():
            finish(acc_ref[...])

    if mode == "tn":
        a_spec = pl.BlockSpec((tk, tm), lambda i, j, kk: (kk, i))
    else:
        a_spec = pl.BlockSpec((tm, tk), lambda i, j, kk: (i, kk))
    if mode == "nt":
        b_spec = pl.BlockSpec((tn, tk), lambda i, j, kk: (j, kk))
    else:
        b_spec = pl.BlockSpec((tk, tn), lambda i, j, kk: (kk, j))
    o_spec = pl.BlockSpec((tm, tn), lambda i, j, kk: (i, j))
    in_specs, args = [a_spec, b_spec], [a, b]
    out_specs, out_shape = o_spec, jax.ShapeDtypeStruct((m, n), out_dtype)
    if has_res:
        in_specs.append(o_spec)
        args.append(res)
    if has_g:
        in_specs.append(pl.BlockSpec((1, tn), lambda i, j, kk: (0, j)))
        args.append(gvec)
    if n_rope:
        in_specs += [pl.BlockSpec((tm, HEAD_PAD), lambda i, j, kk: (i, 0))] * 2
        args += list(rope)
    if fused:
        out_specs = (o_spec, o_spec)
        out_shape = (out_shape, jax.ShapeDtypeStruct((m, n), F32))
    scratch = [pltpu.VMEM((tm, tn), F32)] if nk > 1 else []
    if nx:
        any_spec = pl.BlockSpec(memory_space=pl.ANY)
        in_specs += [any_spec] * nx
        args += list(scatter)
        out_specs = tuple(out_specs if fused else (out_specs,)) + (any_spec,) * nx
        out_shape = tuple(out_shape if fused else (out_shape,)) + tuple(
            jax.ShapeDtypeStruct(g.shape, g.dtype) for g in scatter)
        scratch += _exchange_scratch(nx)
    return pl.pallas_call(
        body, name=name, grid=(m // tm, n // tn, nk),
        in_specs=in_specs, out_specs=out_specs, out_shape=out_shape,
        scratch_shapes=scratch,
        compiler_params=_params(("arbitrary",) * 3 if nx else ("parallel", "parallel", "arbitrary")),
    )(*args)


def _rms_mod(x, g, sc, sh, name):
    s, w = x.shape
    tm = _tile(s, ROW_T)

    def body(x_ref, g_ref, sc_ref, sh_ref, o_ref):
        xv = x_ref[...]
        r = lax.rsqrt(jnp.mean(xv * xv, axis=-1, keepdims=True) + EPS)
        o_ref[...] = ((xv * r * g_ref[...]) * (1.0 + sc_ref[...]) + sh_ref[...]).astype(o_ref.dtype)

    row = pl.BlockSpec((tm, w), lambda i: (i, 0))
    return pl.pallas_call(
        body, name=name, grid=(s // tm,),
        in_specs=[row, _vec_spec(w, 1), _vec_spec(w, 1), _vec_spec(w, 1)],
        out_specs=row, out_shape=jax.ShapeDtypeStruct((s, w), BF16),
        compiler_params=_params(("parallel",)),
    )(x, g, sc, sh)


def _latent_norm(lat, q_g, kv_g, name):
    s, w = lat.shape
    tm = _tile(s, ROW_T)
    nq_, nkv = MLA_Q_RANK, MLA_KV_RANK

    def norm(xv, gv):
        return xv * lax.rsqrt(jnp.mean(xv * xv, axis=-1, keepdims=True) + EPS) * gv

    def body(lat_ref, qg_ref, kg_ref, qn_ref, kv_ref):
        qn_ref[...] = norm(lat_ref[:, 0:nq_], qg_ref[...]).astype(BF16)
        kv_ref[:, 0:nkv] = norm(lat_ref[:, nq_:nq_ + nkv], kg_ref[...]).astype(BF16)
        kv_ref[:, nkv:nkv + HEAD_PAD] = lat_ref[:, nq_ + nkv:nq_ + nkv + HEAD_PAD].astype(BF16)

    out = lambda n: pl.BlockSpec((tm, n), lambda i: (i, 0))
    return pl.pallas_call(
        body, name=name, grid=(s // tm,),
        in_specs=[out(w), _vec_spec(nq_, 1), _vec_spec(nkv, 1)],
        out_specs=(out(nq_), out(nkv + HEAD_PAD)),
        out_shape=(jax.ShapeDtypeStruct((s, nq_), BF16), jax.ShapeDtypeStruct((s, nkv + HEAD_PAD), BF16)),
        compiler_params=_params(("parallel",)),
    )(lat, q_g, kv_g)


def _rms_mod_bwd(dh, x, g, sc, dres, name, branch=None):
    s, w = x.shape
    tm = _tile(s, ROW_T)
    has_res, has_br = dres is not None, branch is not None

    def body(*refs):
        dh_ref, x_ref, g_ref, sc_ref = refs[:4]
        rest = list(refs[4:])
        dres_ref = rest.pop(0) if has_res else None
        val_ref, bg_ref = (rest.pop(0), rest.pop(0)) if has_br else (None, None)
        dx_ref = rest.pop(0)
        db_ref = rest.pop(0) if has_br else None
        sums_ref = rest.pop(0)
        xv, dhv, gv = x_ref[...], dh_ref[...], g_ref[...]
        r = lax.rsqrt(jnp.mean(xv * xv, axis=-1, keepdims=True) + EPS)
        xhat = xv * r
        dxn = dhv * (1.0 + sc_ref[...])
        dxhat = dxn * gv
        dx = r * (dxhat - xhat * jnp.mean(dxhat * xhat, axis=-1, keepdims=True))
        if has_res:
            dx = dx + dres_ref[...]
        dx_ref[...] = dx

        @pl.when(pl.program_id(0) == 0)
        def _():
            sums_ref[...] = jnp.zeros_like(sums_ref)

        sums_ref[0:1, :] += jnp.sum(dhv, axis=0, keepdims=True)
        sums_ref[1:2, :] += jnp.sum(dhv * (xhat * gv), axis=0, keepdims=True)
        sums_ref[2:3, :] += jnp.sum(dxn * xhat, axis=0, keepdims=True)
        if has_br:
            db_ref[...] = (dx * bg_ref[...]).astype(db_ref.dtype)
            sums_ref[3:4, :] += jnp.sum(dx * val_ref[...], axis=0, keepdims=True)

    row = pl.BlockSpec((tm, w), lambda i: (i, 0))
    in_specs = [row, row, _vec_spec(w, 1), _vec_spec(w, 1)] + ([row] if has_res else [])
    args = [dh, x, g, sc] + ([dres] if has_res else [])
    out_specs, out_shape = [row], [jax.ShapeDtypeStruct((s, w), F32)]
    if has_br:
        in_specs += [row, _vec_spec(w, 1)]
        args += list(branch)
        out_specs.append(row)
        out_shape.append(jax.ShapeDtypeStruct((s, w), BF16))
    out_specs.append(pl.BlockSpec((8, w), lambda i: (0, 0)))
    out_shape.append(jax.ShapeDtypeStruct((8, w), F32))
    return pl.pallas_call(
        body, name=name, grid=(s // tm,),
        in_specs=in_specs, out_specs=tuple(out_specs), out_shape=tuple(out_shape),
        compiler_params=_params(("arbitrary",)),
    )(*args)


def _final_loss(x3, target, g, ffn, gvec, name):
    s, w = x3.shape
    tm = _tile(s, ROW_T)

    def body(x_ref, t_ref, g_ref, ffn_ref, bg_ref, dx_ref, db_ref, sums_ref):
        xv, gv = x_ref[...], g_ref[...]
        r = lax.rsqrt(jnp.mean(xv * xv, axis=-1, keepdims=True) + EPS)
        xhat = xv * r
        err = xhat * gv - t_ref[...]
        dy = err * (1.0 / w)
        dxhat = dy * gv
        dx = r * (dxhat - xhat * jnp.mean(dxhat * xhat, axis=-1, keepdims=True))
        dx_ref[...] = dx
        db_ref[...] = (dx * bg_ref[...]).astype(db_ref.dtype)

        @pl.when(pl.program_id(0) == 0)
        def _():
            sums_ref[...] = jnp.zeros_like(sums_ref)

        sums_ref[0:1, :] += jnp.sum(dy * xhat, axis=0, keepdims=True)
        sums_ref[1:2, :] += jnp.zeros((1, w), F32) + (0.5 / w) * jnp.sum(err * err)
        sums_ref[2:3, :] += jnp.sum(dx * ffn_ref[...], axis=0, keepdims=True)

    row = pl.BlockSpec((tm, w), lambda i: (i, 0))
    return pl.pallas_call(
        body, name=name, grid=(s // tm,),
        in_specs=[row, row, _vec_spec(w, 1), row, _vec_spec(w, 1)],
        out_specs=(row, row, pl.BlockSpec((8, w), lambda i: (0, 0))),
        out_shape=(jax.ShapeDtypeStruct((s, w), F32), jax.ShapeDtypeStruct((s, w), BF16),
                   jax.ShapeDtypeStruct((8, w), F32)),
        compiler_params=_params(("arbitrary",)),
    )(x3, target, g, ffn, gvec)


def _rope_block(seg, cmul, smul):
    lane = lax.broadcasted_iota(jnp.int32, seg.shape, 1)
    swapped = jnp.where(lane < MLA_NOPE + MLA_ROPE // 2,
                        pltpu.roll(seg, HEAD_PAD - MLA_ROPE // 2, 1), pltpu.roll(seg, MLA_ROPE // 2, 1))
    return seg * cmul + swapped * smul


def _rope(t, cmul, smul, name):
    s, w = t.shape
    tm = _tile(s, ROW_T)

    def body(t_ref, c_ref, s_ref, o_ref):
        cv, sv = c_ref[...], s_ref[...]
        for hb in range(w // HEAD_PAD):
            lanes = slice(hb * HEAD_PAD, (hb + 1) * HEAD_PAD)
            o_ref[:, lanes] = _rope_block(t_ref[:, lanes].astype(F32), cv, sv).astype(o_ref.dtype)

    row = pl.BlockSpec((tm, w), lambda i: (i, 0))
    tab = pl.BlockSpec((tm, HEAD_PAD), lambda i: (i, 0))
    return pl.pallas_call(
        body, name=name, grid=(s // tm,),
        in_specs=[row, tab, tab], out_specs=row, out_shape=jax.ShapeDtypeStruct((s, w), BF16),
        compiler_params=_params(("parallel",)),
    )(t, cmul, smul)


def _rope_bwd_kv(dk, dv, cmul, smul, name):
    s, w = dk.shape
    tm = _tile(s, ROW_T)

    def body(dk_ref, dv_ref, c_ref, s_ref, o_ref):
        cv, sv = c_ref[...], s_ref[...]
        for hb in range(N_HEADS):
            lo, hi = hb * HEAD_PAD, (hb + 1) * HEAD_PAD
            o_ref[:, lo:hi] = _rope_block(dk_ref[:, lo:hi].astype(F32), cv, sv).astype(o_ref.dtype)
        o_ref[:, w:2 * w] = dv_ref[...].astype(o_ref.dtype)

    row = pl.BlockSpec((tm, w), lambda i: (i, 0))
    tab = pl.BlockSpec((tm, HEAD_PAD), lambda i: (i, 0))
    return pl.pallas_call(
        body, name=name, grid=(s // tm,),
        in_specs=[row, row, tab, tab],
        out_specs=pl.BlockSpec((tm, 2 * w), lambda i: (i, 0)),
        out_shape=jax.ShapeDtypeStruct((s, 2 * w), BF16),
        compiler_params=_params(("parallel",)),
    )(dk, dv, cmul, smul)


def _lanes(col, width):
    if col.shape[1] == 1:
        col = jnp.broadcast_to(col, (col.shape[0], HEAD_PAD))
    return jnp.tile(col, (1, width // HEAD_PAD))


def _fold_lanes(a):
    out = a[:, 0:HEAD_PAD]
    for g in range(1, a.shape[1] // HEAD_PAD):
        out = out + a[:, g * HEAD_PAD:(g + 1) * HEAD_PAD]
    return out


def _as_row(rep):
    return rep.T[0:1, :]


def _causal(t, rows_are_queries):
    row = lax.broadcasted_iota(jnp.int32, (t, t), 0)
    col = lax.broadcasted_iota(jnp.int32, (t, t), 1)
    return row >= col if rows_are_queries else col >= row


def _split_refs(refs, n_in, n_out, n_scratch, n_x):
    pos = [n_in, n_x, n_out, n_x, n_scratch, 3 if n_x else 0]
    out, at = [], 0
    for cnt in pos:
        out.append(refs[at:at + cnt])
        at += cnt
    return out


def _first_last_step(n0, n1):
    i0, i1 = pl.program_id(0), pl.program_id(1)
    return jnp.logical_and(i0 == 0, i1 == 0), jnp.logical_and(i0 == n0 - 1, i1 == n1 - 1)


def _as_lanes(row):
    return jnp.broadcast_to(row, (HEAD_PAD, row.shape[1])).T


def _attn_fwd(q, k, v, frow, name, gather=()):
    s = q.shape[0]
    t = ATT_T
    nq = s // t
    use_f = frow is not None
    nx = len(gather)

    hpb = FWD_HEADS_PER_STEP

    def body(*refs):
        ins, x_src, outs, x_out, scr, x_sems = _split_refs(refs, 4 if use_f else 3, 3, 2, nx)
        if use_f:
            q_ref, k_ref, v_ref, fr_ref = ins
            fc_b = [_as_lanes(fr_ref[hh, pl.program_id(1)]) for hh in range(hpb)]
        else:
            q_ref, k_ref, v_ref = ins
        o_ref, ob_ref, lse_ref = outs
        m_s, acc_s = scr
        if nx:
            first, last = _first_last_step(N_HEADS // hpb, nq)
            x_start, x_wait = _direct_exchange(x_src, x_out, *x_sems, scatter=False)
            pl.when(first)(x_start)
        qi = pl.program_id(1)
        m_s[...] = jnp.full(m_s.shape, NEG_BIG, F32)
        acc_s[...] = jnp.zeros(acc_s.shape, F32)

        def step(j, masked):
            off = pl.multiple_of(j * t, t)
            for hh in range(hpb):
                lanes = slice(hh * HEAD_PAD, (hh + 1) * HEAD_PAD)
                kv = k_ref[pl.ds(off, t), lanes]
                vv = v_ref[pl.ds(off, t), lanes]
                sc = lax.dot_general(q_ref[:, lanes], kv, (((1,), (1,)), ((), ())), preferred_element_type=F32)
                if use_f:
                    sc = sc + (_lanes(fc_b[hh], t) - fr_ref[hh, j])
                if masked:
                    sc = jnp.where(_causal(t, True), sc, NEG_BIG)
                m_prev = m_s[hh]
                m_new = jnp.maximum(m_prev, jnp.max(sc, axis=-1, keepdims=True))
                p = jnp.exp2(sc - _lanes(m_new, t))
                acc_s[hh] = jnp.exp2(m_prev - m_new) * acc_s[hh] + jnp.dot(p.astype(BF16), vv,
                                                                           preferred_element_type=F32)
                m_s[hh] = m_new

        def loop_body(j, carry):
            step(j, False)
            return carry

        lax.fori_loop(0, qi, loop_body, 0)
        step(qi, True)
        for hh in range(hpb):
            lanes = slice(hh * HEAD_PAD, (hh + 1) * HEAD_PAD)
            acc = acc_s[hh]
            lane = lax.broadcasted_iota(jnp.int32, acc.shape, 1)
            denom = jnp.sum(jnp.where(lane == SUM_LANE, acc, 0.0), axis=-1, keepdims=True)
            o = acc * (1.0 / denom)
            o_ref[:, lanes] = o
            ob_ref[:, lanes] = o.astype(BF16)
            lse_ref[hh, 0] = _as_row(m_s[hh] + jnp.log(denom) * LOG2E)
        if nx:
            pl.when(last)(x_wait)

    w = hpb * HEAD_PAD
    qspec = pl.BlockSpec((t, w), lambda h, i: (i, h))
    kspec = pl.BlockSpec((s, w), lambda h, i: (0, h))
    any_spec = pl.BlockSpec(memory_space=pl.ANY)
    in_specs, args = [qspec, kspec, kspec], [q, k, v]
    if use_f:
        in_specs += [pl.BlockSpec((hpb, nq, 1, t), lambda h, i: (h, 0, 0, 0))]
        args += [frow]
    out_specs = [qspec, qspec, pl.BlockSpec((hpb, 1, 1, t), lambda h, i: (h, i, 0, 0))]
    out_shape = [jax.ShapeDtypeStruct((s, N_HEADS * HEAD_PAD), F32), jax.ShapeDtypeStruct((s, N_HEADS * HEAD_PAD), BF16),
                 jax.ShapeDtypeStruct((N_HEADS, nq, 1, t), F32)]
    scratch = [pltpu.VMEM((hpb, t, HEAD_PAD), F32), pltpu.VMEM((hpb, t, HEAD_PAD), F32)]
    if nx:
        in_specs += [any_spec] * nx
        args += list(gather)
        out_specs += [any_spec] * nx
        out_shape += [jax.ShapeDtypeStruct((N_DEV,) + g.shape, g.dtype) for g in gather]
        scratch += _exchange_scratch(nx)
    return pl.pallas_call(
        body, name=name, grid=(N_HEADS // hpb, nq),
        in_specs=in_specs, out_specs=tuple(out_specs), out_shape=tuple(out_shape),
        scratch_shapes=scratch,
        compiler_params=_params(("arbitrary", "arbitrary") if nx else ("parallel", "arbitrary")),
    )(*args)


def _attn_delta(o, do, name):
    s, w = o.shape
    t = ATT_T

    def body(o_ref, do_ref, d_ref):
        for hb in range(N_HEADS):
            lo, hi = hb * HEAD_PAD, (hb + 1) * HEAD_PAD
            prod = o_ref[:, lo:hi] * do_ref[:, lo:hi].astype(F32)
            d_ref[hb, 0] = jnp.sum(prod.T, axis=0, keepdims=True)

    row = pl.BlockSpec((t, w), lambda i: (i, 0))
    return pl.pallas_call(
        body, name=name, grid=(s // t,),
        in_specs=[row, row],
        out_specs=pl.BlockSpec((N_HEADS, 1, 1, t), lambda i: (0, i, 0, 0)),
        out_shape=jax.ShapeDtypeStruct((N_HEADS, s // t, 1, t), F32),
        compiler_params=_params(("parallel",)),
    )(o, do)


def _attn_bwd(q, k, v, do, lse_row, delta_row, frow, scale_q, scale_k, name, out_dtype, scatter=()):
    s = q.shape[0]
    t = ATT_T
    nq = s // t
    use_f = frow is not None
    nx = len(scatter)
    hpb = BWD_HEADS_PER_STEP

    def body(*refs):
        ins, x_src, outs, x_out, scr, x_sems = _split_refs(refs, 7 if use_f else 6, 5 if use_f else 3,
                                                           5 if use_f else 3, nx)
        if use_f:
            q_ref, k_ref, v_ref, do_ref, lse_ref, dl_ref, fr_ref = ins
            dq_ref, dk_ref, dv_ref, dr_ref, df_ref = outs
            dq_s, dk_s, dv_s, dr_s, df_s = scr
            fc_b = [_as_lanes(fr_ref[hh, pl.program_id(1)]) for hh in range(hpb)]
        else:
            q_ref, k_ref, v_ref, do_ref, lse_ref, dl_ref = ins
            dq_ref, dk_ref, dv_ref = outs
            dq_s, dk_s, dv_s = scr
        if nx:
            first, last = _first_last_step(N_HEADS // hpb, nq)
            x_start, x_wait = _direct_exchange(x_src, x_out, *x_sems, scatter=True)
            pl.when(first)(x_start)
        kj = pl.program_id(1)

        @pl.when(kj == 0)
        def _():
            dq_s[...] = jnp.zeros(dq_s.shape, F32)
            if use_f:
                dr_s[...] = jnp.zeros(dr_s.shape, F32)

        dk_s[...] = jnp.zeros(dk_s.shape, F32)
        dv_s[...] = jnp.zeros(dv_s.shape, F32)
        if use_f:
            df_s[...] = jnp.zeros(df_s.shape, F32)

        def step(i, masked):
            off = pl.multiple_of(i * t, t)
            for hh in range(hpb):
                lanes = slice(hh * HEAD_PAD, (hh + 1) * HEAD_PAD)
                kv, vv = k_ref[:, lanes], v_ref[:, lanes]
                qv = q_ref[pl.ds(off, t), lanes]
                dov = do_ref[pl.ds(off, t), lanes]
                st = lax.dot_general(kv, qv, (((1,), (1,)), ((), ())), preferred_element_type=F32)
                if use_f:
                    st = st + (fr_ref[hh, i] - _lanes(fc_b[hh], t))
                if masked:
                    st = jnp.where(_causal(t, False), st, NEG_BIG)
                pt = jnp.exp2(st - lse_ref[hh, i])
                dv_s[hh] += jnp.dot(pt.astype(BF16), dov, preferred_element_type=F32)
                dpt = lax.dot_general(vv, dov, (((1,), (1,)), ((), ())), preferred_element_type=F32)
                dst = pt * (dpt - dl_ref[hh, i])
                dsb = dst.astype(BF16)
                dk_s[hh] += jnp.dot(dsb, qv, preferred_element_type=F32)
                dq_s[hh, pl.ds(off, t), :] += lax.dot_general(dsb, kv, (((0,), (0,)), ((), ())),
                                                              preferred_element_type=F32)
                if use_f:
                    df_s[hh] -= _fold_lanes(dst)
                    dr_s[hh, i] += jnp.sum(dst, axis=0, keepdims=True)

        step(kj, True)

        def loop_body(i, carry):
            step(i, False)
            return carry

        lax.fori_loop(kj + 1, nq, loop_body, 0)
        for hh in range(hpb):
            lanes = slice(hh * HEAD_PAD, (hh + 1) * HEAD_PAD)
            dk_ref[:, lanes] = (dk_s[hh] * scale_k).astype(dk_ref.dtype)
            dv_ref[:, lanes] = dv_s[hh].astype(dv_ref.dtype)
            if use_f:
                df_ref[hh, 0] = jnp.sum(df_s[hh].T, axis=0, keepdims=True)

        @pl.when(kj == nq - 1)
        def _():
            for hh in range(hpb):
                dq_ref[:, hh * HEAD_PAD:(hh + 1) * HEAD_PAD] = (dq_s[hh] * scale_q).astype(dq_ref.dtype)
            if use_f:
                dr_ref[...] = dr_s[...]

        if nx:
            pl.when(last)(x_wait)

    w = hpb * HEAD_PAD
    kspec = pl.BlockSpec((t, w), lambda h, j: (j, h))
    qspec = pl.BlockSpec((s, w), lambda h, j: (0, h))
    rowspec = pl.BlockSpec((hpb, nq, 1, t), lambda h, j: (h, 0, 0, 0))
    any_spec = pl.BlockSpec(memory_space=pl.ANY)
    in_specs, args = [qspec, kspec, kspec, qspec, rowspec, rowspec], [q, k, v, do, lse_row, delta_row]
    full = jax.ShapeDtypeStruct((s, N_HEADS * HEAD_PAD), out_dtype)
    out_specs, out_shape = [qspec, kspec, kspec], [full, full, full]
    scratch = [pltpu.VMEM((hpb, s, HEAD_PAD), F32), pltpu.VMEM((hpb, t, HEAD_PAD), F32),
               pltpu.VMEM((hpb, t, HEAD_PAD), F32)]
    if use_f:
        in_specs += [rowspec]
        args += [frow]
        out_specs += [rowspec, pl.BlockSpec((hpb, 1, 1, t), lambda h, j: (h, j, 0, 0))]
        out_shape += [jax.ShapeDtypeStruct((N_HEADS, nq, 1, t), F32)] * 2
        scratch += [pltpu.VMEM((hpb, nq, 1, t), F32), pltpu.VMEM((hpb, t, HEAD_PAD), F32)]
    if nx:
        in_specs += [any_spec] * nx
        args += list(scatter)
        out_specs += [any_spec] * nx
        out_shape += [jax.ShapeDtypeStruct(g.shape, g.dtype) for g in scatter]
        scratch += _exchange_scratch(nx)
    return pl.pallas_call(
        body, name=name, grid=(N_HEADS // hpb, nq),
        in_specs=in_specs, out_specs=tuple(out_specs), out_shape=tuple(out_shape),
        scratch_shapes=scratch,
        compiler_params=_params(("arbitrary", "arbitrary") if nx else ("parallel", "arbitrary")),
    )(*args)


def _gate_fwd(pm, pf, gates, name):
    s, w = pm.shape
    tm = _tile(s, ROW_T)

    def body(pm_ref, pf_ref, g_ref, y_ref):
        y = (jax.nn.sigmoid(g_ref[:, 0:w]) * pm_ref[...].astype(F32)
             + jax.nn.sigmoid(g_ref[:, w:2 * w]) * pf_ref[...].astype(F32))
        y_ref[...] = y.astype(y_ref.dtype)

    row = pl.BlockSpec((tm, w), lambda i: (i, 0))
    return pl.pallas_call(
        body, name=name, grid=(s // tm,),
        in_specs=[row, row, pl.BlockSpec((tm, 2 * w), lambda i: (i, 0))],
        out_specs=row, out_shape=jax.ShapeDtypeStruct((s, w), BF16),
        compiler_params=_params(("parallel",)),
    )(pm, pf, gates)


def _gate_bwd(dy, pm, pf, gates, name):
    s, w = pm.shape
    tm = _tile(s, ROW_T)

    def body(dy_ref, pm_ref, pf_ref, g_ref, dpm_ref, dpf_ref, dg_ref):
        dyv = dy_ref[...]
        sm, sf = jax.nn.sigmoid(g_ref[:, 0:w]), jax.nn.sigmoid(g_ref[:, w:2 * w])
        dpm_ref[...] = (dyv * sm).astype(BF16)
        dpf_ref[...] = (dyv * sf).astype(BF16)
        dg_ref[:, 0:w] = (dyv * pm_ref[...].astype(F32) * (sm * (1.0 - sm))).astype(BF16)
        dg_ref[:, w:2 * w] = (dyv * pf_ref[...].astype(F32) * (sf * (1.0 - sf))).astype(BF16)

    row = pl.BlockSpec((tm, w), lambda i: (i, 0))
    wide = pl.BlockSpec((tm, 2 * w), lambda i: (i, 0))
    out = jax.ShapeDtypeStruct((s, w), BF16)
    return pl.pallas_call(
        body, name=name, grid=(s // tm,),
        in_specs=[row, row, row, wide],
        out_specs=(row, row, wide), out_shape=(out, out, jax.ShapeDtypeStruct((s, 2 * w), BF16)),
        compiler_params=_params(("parallel",)),
    )(dy, pm, pf, gates)


CONV_TN = 256
CONV_TM = 512
CONV_T_TN = D_FF // 2
HALO = BF16_ROWS


def _shift_down(u, prev, n):
    rolled = pltpu.roll(u, n, 0)
    prev_rolled = pltpu.roll(prev, n, 0)
    top = jnp.concatenate([prev_rolled, rolled[HALO:]], axis=0)
    row = lax.broadcasted_iota(jnp.int32, u.shape, 0)
    return jnp.where(row < n, top, rolled)


def _conv_tile(u, prev, w_ref, b_ref):
    um1 = _shift_down(u, prev, 1)
    um2 = _shift_down(u, prev, 2)
    uc = b_ref[...] + w_ref[0:1, :] * um2 + w_ref[1:2, :] * um1 + w_ref[2:3, :] * u
    return uc, um1, um2


def _conv_specs(tm, tn, ncol_off):
    blk = lambda off: pl.BlockSpec((tm, tn), lambda j, i: (i, j + off))
    halo = lambda off: pl.BlockSpec((HALO, tn), lambda j, i: (jnp.maximum(i * (tm // HALO) - 1, 0), j + off))
    wsp = lambda off: pl.BlockSpec((3, tn), lambda j, i: (0, j + off))
    bsp = lambda off: pl.BlockSpec((1, tn), lambda j, i: (0, j + off))
    return blk, halo, wsp, bsp


def _convglu_fwd(u, conv_w, conv_b, name):
    s = u.shape[0]
    tm, tn = _tile(s, CONV_TM), CONV_TN
    nj = D_FF // tn
    blk, halo, wsp, bsp = _conv_specs(tm, tn, nj)

    def body(ug_ref, pg_ref, uv_ref, pv_ref, wg_ref, wv_ref, bg_ref, bv_ref, a_ref):
        live = (pl.program_id(1) > 0).astype(F32)
        gate, _, _ = _conv_tile(ug_ref[...].astype(F32), pg_ref[...].astype(F32) * live, wg_ref, bg_ref)
        val, _, _ = _conv_tile(uv_ref[...].astype(F32), pv_ref[...].astype(F32) * live, wv_ref, bv_ref)
        a_ref[...] = (gate * jax.nn.sigmoid(gate) * val).astype(a_ref.dtype)

    return pl.pallas_call(
        body, name=name, grid=(nj, s // tm),
        in_specs=[blk(0), halo(0), blk(nj), halo(nj), wsp(0), wsp(nj), bsp(0), bsp(nj)],
        out_specs=blk(0), out_shape=jax.ShapeDtypeStruct((s, D_FF), BF16),
        compiler_params=_params(("parallel", "arbitrary")),
    )(u, u, u, u, conv_w, conv_w, conv_b, conv_b)


def _convglu_bwd(da, u, conv_w, conv_b, name):
    s = u.shape[0]
    tm, tn = _tile(s, CONV_TM), CONV_TN
    nj = D_FF // tn
    blk, halo, wsp, bsp = _conv_specs(tm, tn, nj)

    def body(da_ref, ug_ref, pg_ref, uv_ref, pv_ref, wg_ref, wv_ref, bg_ref, bv_ref,
             dg_ref, dv_ref, sg_ref, sv_ref):
        live = (pl.program_id(1) > 0).astype(F32)
        ug, uv = ug_ref[...].astype(F32), uv_ref[...].astype(F32)
        gate, ug1, ug2 = _conv_tile(ug, pg_ref[...].astype(F32) * live, wg_ref, bg_ref)
        val, uv1, uv2 = _conv_tile(uv, pv_ref[...].astype(F32) * live, wv_ref, bv_ref)
        dav = da_ref[...].astype(F32)
        sig = jax.nn.sigmoid(gate)
        dgate = dav * val * (sig * (1.0 + gate * (1.0 - sig)))
        dval = dav * (gate * sig)
        dg_ref[...] = dgate.astype(dg_ref.dtype)
        dv_ref[...] = dval.astype(dv_ref.dtype)

        @pl.when(pl.program_id(1) == 0)
        def _():
            sg_ref[...] = jnp.zeros_like(sg_ref)
            sv_ref[...] = jnp.zeros_like(sv_ref)

        for s_ref, d, taps in ((sg_ref, dgate, (ug2, ug1, ug)), (sv_ref, dval, (uv2, uv1, uv))):
            for r, tap in enumerate(taps):
                s_ref[r:r + 1, :] += jnp.sum(d * tap, axis=0, keepdims=True)
            s_ref[3:4, :] += jnp.sum(d, axis=0, keepdims=True)

    sums = lambda off: pl.BlockSpec((8, tn), lambda j, i: (0, j + off))
    return pl.pallas_call(
        body, name=name, grid=(nj, s // tm),
        in_specs=[blk(0), blk(0), halo(0), blk(nj), halo(nj), wsp(0), wsp(nj), bsp(0), bsp(nj)],
        out_specs=(blk(0), blk(0), sums(0), sums(0)),
        out_shape=(jax.ShapeDtypeStruct((s, D_FF), BF16), jax.ShapeDtypeStruct((s, D_FF), BF16),
                   jax.ShapeDtypeStruct((8, D_FF), F32), jax.ShapeDtypeStruct((8, D_FF), F32)),
        compiler_params=_params(("parallel", "arbitrary")),
    )(da, u, u, u, u, conv_w, conv_w, conv_b, conv_b)


def _conv_transpose(d, conv_w, name, col0, into=None):
    s, w = d.shape
    tm, tn = _tile(s, CONV_TM), CONV_T_TN
    last = s // tm - 1
    jo = col0 // tn

    def body(d_ref, nx_ref, w_ref, *rest):
        o_ref = rest[-1]
        dv = d_ref[...].astype(F32)
        nxt = nx_ref[...].astype(F32) * (pl.program_id(1) < last).astype(F32)
        row = lax.broadcasted_iota(jnp.int32, dv.shape, 0)

        def shift_up(n):
            rolled = pltpu.roll(dv, tm - n, 0)
            nxt_rolled = pltpu.roll(nxt, HALO - n, 0)
            bottom = jnp.concatenate([rolled[:tm - HALO], nxt_rolled], axis=0)
            return jnp.where(row >= tm - n, bottom, rolled)

        out = w_ref[2:3, :] * dv + w_ref[1:2, :] * shift_up(1) + w_ref[0:1, :] * shift_up(2)
        o_ref[...] = out.astype(o_ref.dtype)

    blk = pl.BlockSpec((tm, tn), lambda j, i: (i, j))
    nxt_spec = pl.BlockSpec((HALO, tn), lambda j, i: (jnp.minimum((i + 1) * (tm // HALO), s // HALO - 1), j))
    in_specs = [blk, nxt_spec, pl.BlockSpec((3, tn), lambda j, i: (0, j + jo))]
    args = [d, d, conv_w]
    if into is not None:
        in_specs.append(pl.BlockSpec(memory_space=pl.ANY))
        args.append(into)
    return pl.pallas_call(
        body, name=name, grid=(w // tn, s // tm),
        in_specs=in_specs,
        out_specs=pl.BlockSpec((tm, tn), lambda j, i: (i, j + jo)),
        out_shape=jax.ShapeDtypeStruct((s, 2 * D_FF), BF16),
        input_output_aliases={3: 0} if into is not None else {},
        compiler_params=_params(("parallel", "arbitrary")),
    )(*args)


def _split3(a):
    a1 = a.astype(BF16)
    r1 = a - a1.astype(F32)
    a2 = r1.astype(BF16)
    a3 = (r1 - a2.astype(F32)).astype(BF16)
    return a1, a2, a3


def _ones_dot_right(a, mat):
    return sum(jnp.dot(p, mat, preferred_element_type=F32) for p in _split3(a))


def _ones_dot_left(mat, a):
    return sum(jnp.dot(mat, p, preferred_element_type=F32) for p in _split3(a))


def _tri(n, cmp):
    r = lax.broadcasted_iota(jnp.int32, (n, n), 0)
    c = lax.broadcasted_iota(jnp.int32, (n, n), 1)
    return cmp(r, c).astype(BF16)


def _forget_fwd(z, bias, name):
    nh, nr, nl = z.shape

    def body(z_ref, b_ref, f_ref):
        within = _tri(nl, lambda r, c: r <= c)
        before = _tri(nr, lambda r, c: c < r)
        for h in range(nh):
            x = z_ref[h] + b_ref[h]
            lf = jnp.minimum(x, 0.0) - jnp.log(1.0 + jnp.exp(-jnp.abs(x)))
            pre = _ones_dot_right(lf, within)
            tot = jnp.zeros((nr, nl), F32) + jnp.sum(lf, axis=1, keepdims=True)
            f_ref[h] = pre + _ones_dot_left(before, tot)

    return pl.pallas_call(
        body, name=name, out_shape=jax.ShapeDtypeStruct(z.shape, F32),
        compiler_params=pltpu.CompilerParams(vmem_limit_bytes=VMEM_LIMIT_BYTES),
    )(z, bias)


def _forget_bwd(df_rows, df_cols, z, bias, name):
    nh, nr, nl = z.shape

    def body(dfr_ref, dfc_ref, z_ref, b_ref, dz_ref, db_ref):
        within = _tri(nl, lambda r, c: r >= c)
        after = _tri(nr, lambda r, c: c > r)
        for h in range(nh):
            g = dfr_ref[h] + dfc_ref[h]
            suf = _ones_dot_right(g, within)
            tot = jnp.zeros((nr, nl), F32) + jnp.sum(g, axis=1, keepdims=True)
            dlf = suf + _ones_dot_left(after, tot)
            dz = dlf * jax.nn.sigmoid(-(z_ref[h] + b_ref[h]))
            dz_ref[h] = dz
            db_ref[h] = jnp.zeros((1, nl), F32) + jnp.sum(dz)

    return pl.pallas_call(
        body, name=name,
        out_shape=(jax.ShapeDtypeStruct(z.shape, F32), jax.ShapeDtypeStruct(bias.shape, F32)),
        compiler_params=pltpu.CompilerParams(vmem_limit_bytes=VMEM_LIMIT_BYTES),
    )(df_rows, df_cols, z, bias)


def _ada_fwd(c_col, w, b, name):
    kdim, n = w.shape

    def body(c_ref, w_ref, b_ref, ada_ref, act_ref):
        wv = w_ref[...]
        for e in range(N_DEV):
            cv = c_ref[e]
            act = cv * jax.nn.sigmoid(cv)
            act_ref[e] = act
            ada_ref[e:e + 1, :] = jnp.sum(act * wv, axis=0, keepdims=True) + b_ref[...]

    return pl.pallas_call(
        body, name=name,
        out_shape=(jax.ShapeDtypeStruct((N_DEV, n), F32), jax.ShapeDtypeStruct((N_DEV, kdim, 1), F32)),
        compiler_params=pltpu.CompilerParams(vmem_limit_bytes=VMEM_LIMIT_BYTES),
    )(c_col, w, b)


def _ada_bwd(act_col, dada, name):
    kdim = act_col.shape[1]
    n = dada.shape[1]

    def body(act_ref, d_ref, g_ref):
        acc = act_ref[0] * d_ref[0:1, :]
        for e in range(1, N_DEV):
            acc = acc + act_ref[e] * d_ref[e:e + 1, :]
        g_ref[...] = acc

    return pl.pallas_call(
        body, name=name, out_shape=jax.ShapeDtypeStruct((kdim, n), F32),
        compiler_params=pltpu.CompilerParams(vmem_limit_bytes=VMEM_LIMIT_BYTES),
    )(act_col, dada)


def _adamw(parts, w, m, v, name, tr=128):
    npart, r, c = parts.shape
    tr = _tile(r, tr, step=BF16_ROWS) if r % BF16_ROWS == 0 else r

    def body(p_ref, w_ref, m_ref, v_ref, g_ref, d_ref, nm_ref, nv_ref):
        g = p_ref[0].astype(F32)
        for e in range(1, npart):
            g = g + p_ref[e].astype(F32)
        nm = ADAM_B1 * m_ref[...] + (1.0 - ADAM_B1) * g
        nv = ADAM_B2 * v_ref[...] + (1.0 - ADAM_B2) * (g * g)
        m_hat = nm / (1.0 - ADAM_B1 ** ADAM_STEP)
        v_hat = nv / (1.0 - ADAM_B2 ** ADAM_STEP)
        g_ref[...] = g
        d_ref[...] = -ADAM_LR * (m_hat / (jnp.sqrt(v_hat) + ADAM_EPS) + ADAM_WD * w_ref[...])
        nm_ref[...] = nm
        nv_ref[...] = nv

    row = pl.BlockSpec((tr, c), lambda i: (i, 0))
    out = jax.ShapeDtypeStruct((r, c), F32)
    return pl.pallas_call(
        body, name=name, grid=(r // tr,),
        in_specs=[pl.BlockSpec((npart, tr, c), lambda i: (0, i, 0)), row, row, row],
        out_specs=(row, row, row, row), out_shape=(out, out, out, out),
        compiler_params=_params(("parallel",)),
    )(parts, w, m, v)


EARLY = ("w_in", "w_uq", "w_ukv")
LATE = ("w_o_mla", "w_o_fox", "w_out", "w_up", "conv_w", "w_down")
BIG = EARLY + LATE


def _cols_to_full(stack):
    n, r, c = stack.shape
    return stack.transpose(1, 0, 2).reshape(r, n * c)


def _full_to_cols(full, c):
    r = full.shape[0]
    return full.reshape(r, N_DEV, c).transpose(1, 0, 2)


def _pad_heads(a, width):
    r = a.shape[0]
    a = a.reshape(r, N_HEADS, width)
    return jnp.pad(a, ((0, 0), (0, 0), (0, HEAD_PAD - width))).reshape(r, N_HEADS * HEAD_PAD)


def _unpad_heads(a, width):
    s = a.shape[0]
    return a.reshape(s, N_HEADS, HEAD_PAD)[:, :, :width].reshape(s, N_HEADS * width)


def _w_in_padded(w_in):
    seg = [w_in[:, IN_OFF[i]:IN_OFF[i + 1]] for i in range(9)]
    cq, ckv, kr, fq, fk, fv, fl, gm, gf = seg
    padc = lambda a, n: jnp.pad(a, ((0, 0), (0, n - a.shape[1])))
    return jnp.concatenate([gm, gf, fq, fk, fv, cq, ckv, padc(kr, 128), padc(fl, 128)], axis=1)


def _w_in_unpadded(g):
    return jnp.concatenate([
        g[:, P_CQ:P_CQ + 384], g[:, P_CKV:P_CKV + 256], g[:, P_KR:P_KR + 32], g[:, P_FQ:P_FQ + 512],
        g[:, P_FK:P_FK + 512], g[:, P_FV:P_FV + 512], g[:, P_FL:P_FL + 8], g[:, P_GM:P_GM + 1024],
        g[:, P_GF:P_GF + 1024]], axis=1)


SMALL = (("b_ada", 6144, 6144), ("norm_mix_g", 1024, 1024), ("q_norm_g", 384, 384), ("kv_norm_g", 256, 256),
         ("b_forget", 8, 128), ("norm_ffn_g", 1024, 1024), ("conv_b", 5632, 5632), ("norm_final_g", 1024, 1024),
         ("loss", 1, 128))
SMALL_OFF = {}
_o = 0
for _n, _real, _padded in SMALL:
    SMALL_OFF[_n] = _o
    _o += _padded
SMALL_W = _o


def _pack_small(vals):
    parts = []
    for nme, real, padded in SMALL:
        a = vals[nme].reshape(1, real).astype(F32)
        parts.append(jnp.pad(a, ((0, 0), (0, padded - real))))
    return jnp.concatenate(parts, axis=1)


def kernel(x, c, positions, w_ada, b_ada, norm_mix_g, w_in, q_norm_g, w_uq, kv_norm_g, w_ukv, b_forget, w_o_mla, w_o_fox, w_out, norm_ffn_g, w_up, conv_w, conv_b, w_down, norm_final_g, loss_target, m_w_ada, m_b_ada, m_norm_mix_g, m_w_in, m_q_norm_g, m_w_uq, m_kv_norm_g, m_w_ukv, m_b_forget, m_w_o_mla, m_w_o_fox, m_w_out, m_norm_ffn_g, m_w_up, m_conv_w, m_conv_b, m_w_down, m_norm_final_g, v_w_ada, v_b_ada, v_norm_mix_g, v_w_in, v_q_norm_g, v_w_uq, v_kv_norm_g, v_w_ukv, v_b_forget, v_w_o_mla, v_w_o_fox, v_w_out, v_norm_ffn_g, v_w_up, v_conv_w, v_conv_b, v_w_down, v_norm_final_g):
    me = 4 * lax.axis_index("x") + 2 * lax.axis_index("y") + lax.axis_index("c")
    x = x[0]
    target = loss_target[0]
    s = x.shape[0]
    nblk = s // ATT_T
    big_w = {"w_in": w_in, "w_uq": w_uq, "w_ukv": w_ukv, "w_o_mla": w_o_mla, "w_o_fox": w_o_fox,
             "w_out": w_out, "w_up": w_up, "conv_w": conv_w, "w_down": w_down}
    big_m = {"w_in": m_w_in, "w_uq": m_w_uq, "w_ukv": m_w_ukv, "w_o_mla": m_w_o_mla, "w_o_fox": m_w_o_fox,
             "w_out": m_w_out, "w_up": m_w_up, "conv_w": m_conv_w, "w_down": m_w_down}
    big_v = {"w_in": v_w_in, "w_uq": v_w_uq, "w_ukv": v_w_ukv, "w_o_mla": v_w_o_mla, "w_o_fox": v_w_o_fox,
             "w_out": v_w_out, "w_up": v_w_up, "conv_w": v_conv_w, "w_down": v_w_down}

    shard = lambda k: big_w[k][0] if k == "conv_w" else big_w[k][0].astype(BF16)
    st = dict(zip(EARLY, _all_gather([shard(k) for k in EARLY], "gather_weights")))
    w_in_p = _w_in_padded(_cols_to_full(st["w_in"]))
    uq = st["w_uq"]
    w_uq_p = jnp.pad(uq, ((0, 0), (0, 0), (0, HEAD_PAD - 96))).transpose(1, 0, 2).reshape(MLA_Q_RANK, 1024)
    ukv = st["w_ukv"]
    zeros64 = jnp.zeros((N_HEADS, MLA_KV_RANK, 64), BF16)
    w_uk_p = jnp.concatenate([ukv[:, :, :64], zeros64], axis=2).transpose(1, 0, 2).reshape(MLA_KV_RANK, 1024)
    w_uv_p = jnp.concatenate([ukv[:, :, 64:], zeros64], axis=2).transpose(1, 0, 2).reshape(MLA_KV_RANK, 1024)
    place = np.zeros((HEAD_PAD, N_HEADS, HEAD_PAD), np.float32)
    for j in range(MLA_ROPE):
        place[j, :, MLA_NOPE + j] = 1.0
    place = jnp.asarray(place.reshape(HEAD_PAD, 1024), BF16)
    w_kv_comb = jnp.concatenate([
        jnp.concatenate([w_uk_p, w_uv_p], axis=1),
        jnp.concatenate([place, jnp.zeros((HEAD_PAD, 1024), BF16)], axis=1)], axis=0)

    (c_all,) = _all_gather([c], "gather_c")
    b_ada_mine = lax.dynamic_slice(b_ada, (0, me * 768), (1, 768))
    ada_cols, act_col = _ada_fwd(c_all.reshape(N_DEV, D_MODEL, 1), w_ada[0], b_ada_mine, "ada_fwd")
    (ada_all,) = _all_gather([ada_cols], "gather_ada")
    ada = lax.dynamic_slice(ada_all, (0, me, 0), (N_DEV, 1, 768)).reshape(1, N_ADA * D_MODEL)
    sh_m, sc_m, g_m, sh_f, sc_f, g_f = [ada[:, i * D_MODEL:(i + 1) * D_MODEL] for i in range(N_ADA)]

    inv_freq = ROPE_THETA ** (-jnp.arange(0, MLA_ROPE, 2, dtype=F32) / MLA_ROPE)
    ang = positions[0].astype(F32)[:, None] * inv_freq
    cos, sin = jnp.cos(ang), jnp.sin(ang)
    rope_c = jnp.concatenate([jnp.ones((s, 64), F32), cos, cos, jnp.zeros((s, 32), F32)], axis=1)
    rope_s = jnp.concatenate([jnp.zeros((s, 64), F32), -sin, sin, jnp.zeros((s, 32), F32)], axis=1)

    h1 = _rms_mod(x, norm_mix_g, sc_m, sh_m, "norm_mix")
    gates = _mm(h1, w_in_p[:, P_GM:P_FQ], "nn", F32, "proj_gates", tn=1024)
    fq = _mm(h1, _pad_heads(w_in_p[:, P_FQ:P_FK], 64), "nn", BF16, "proj_fq", tn=1024,
             gvec=jnp.full((1, 1024), FOX_SCALE * LOG2E, F32))
    fk = _mm(h1, _pad_heads(w_in_p[:, P_FK:P_FV], 64), "nn", BF16, "proj_fk", tn=1024)
    fv = _mm(h1, _pad_heads(w_in_p[:, P_FV:P_CQ], 64), "nn", BF16, "proj_fv", tn=1024, ones_lane=True)
    lat =_mm(h1, w_in_p[:, P_CQ:], "nn", F32, "proj_latent", tn=D_IN_P - P_CQ)
    cq = lat[:, 0:384]
    ckv = lat[:, P_CKV - P_CQ:P_CKV - P_CQ + 256]
    qn, kv_in = _latent_norm(lat, q_norm_g, kv_norm_g, "latent_norm")
    q_fold = MLA_SCALE * LOG2E
    q_att = _mm(qn, w_uq_p, "nn", BF16, "q_up", rope=(rope_c * q_fold, rope_s * q_fold))
    k_att = _mm(kv_in, w_kv_comb[:, :1024], "nn", BF16, "k_up", rope=(rope_c, rope_s))
    v_att = _mm(kv_in, w_kv_comb[:, 1024:], "nn", BF16, "v_up", ones_lane=True)
    o_mla, o_mla_b, lse_mla, *late = _attn_fwd(q_att, k_att, v_att, None, "mla_fwd",
                                               gather=[shard(k) for k in LATE])
    st.update(zip(LATE, late))
    pad_o = lambda full: jnp.pad(full.reshape(N_HEADS, 64, 1024), ((0, 0), (0, 64), (0, 0))).reshape(1024, 1024)
    w_o_mla_p = pad_o(_cols_to_full(st["w_o_mla"]))
    w_o_fox_p = pad_o(_cols_to_full(st["w_o_fox"]))
    w_out_f = st["w_out"].reshape(1024, 1024)
    w_up_f = _cols_to_full(st["w_up"])
    conv_w_f = _cols_to_full(st["conv_w"])
    w_down_f = st["w_down"].reshape(D_FF, 1024)

    z = lat[:, P_FL - P_CQ:P_FL - P_CQ + 8].T.reshape(N_HEADS, s // SEQ_LANES, SEQ_LANES)
    bias_f = jnp.broadcast_to(b_forget.reshape(N_HEADS, 1, 1), (N_HEADS, 1, SEQ_LANES))
    f_cum = _forget_fwd(z, bias_f, "forget_fwd")
    f_row = (f_cum * LOG2E).reshape(N_HEADS, nblk, 1, ATT_T)
    o_fox, o_fox_b, lse_fox = _attn_fwd(fq, fk, fv, f_row, "fox_fwd")

    pm = _mm(o_mla_b, w_o_mla_p, "nn", BF16, "o_mla_proj")
    pf = _mm(o_fox_b, w_o_fox_p, "nn", BF16, "o_fox_proj")
    y = _gate_fwd(pm, pf, gates, "gate_fwd")
    x2, mix = _mm(y, w_out_f, "nn", F32, "out_proj", res=x, gvec=g_m)

    h2 = _rms_mod(x2, norm_ffn_g, sc_f, sh_f, "norm_ffn")
    u = _mm(h2, w_up_f, "nn", BF16, "ffn_up", tn=D_FF // 2)
    a = _convglu_fwd(u, conv_w_f, conv_b, "convglu_fwd")
    x3, ffn = _mm(a, w_down_f, "nn", F32, "ffn_down", res=x2, gvec=g_f, tk=2816)

    dx3, dffn, sums_final = _final_loss(x3, target, norm_final_g.reshape(1, D_MODEL), ffn, g_f, "final_loss")
    da = _mm(dffn, w_down_f, "nt", BF16, "ffn_down_dx", tn=1408)
    g_w_down = _mm(a, dffn, "tn", F32, "ffn_down_dw", tm=256, tn=1024, tk=s)
    dgate, dval, s_gate, s_val = _convglu_bwd(da, u, conv_w_f, conv_b, "convglu_bwd")
    du = _conv_transpose(dgate, conv_w_f, "conv_t_gate", 0)
    du = _conv_transpose(dval, conv_w_f, "conv_t_val", D_FF, into=du)
    dh2 = _mm(du, w_up_f, "nt", F32, "ffn_up_dx", tn=512, tk=2 * D_FF)
    g_w_up = _mm(h2, du, "tn", F32, "ffn_up_dw", tn=256, tk=s)
    dx2, dmix, sums_ffn = _rms_mod_bwd(dh2, x2, norm_ffn_g, sc_f, dx3, "norm_ffn_bwd", branch=(mix, g_m))

    dy = _mm(dmix, w_out_f, "nt", F32, "out_proj_dx")
    g_w_out = _mm(y, dmix, "tn", F32, "out_proj_dw", tn=256, tk=s)
    dpm, dpf, dgates = _gate_bwd(dy, pm, pf, gates, "gate_bwd")
    do_mla_b = _mm(dpm, w_o_mla_p, "nt", BF16, "o_mla_dx", tn=1024)
    do_fox_b = _mm(dpf, w_o_fox_p, "nt", BF16, "o_fox_dx", tn=1024)
    g_w_o_mla_p = _mm(o_mla_b, dpm, "tn", F32, "o_mla_dw", tn=256, tk=s)
    g_w_o_fox_p = _mm(o_fox_b, dpf, "tn", F32, "o_fox_dw", tn=256, tk=s)

    unpad_o = lambda g: g.reshape(N_HEADS, HEAD_PAD, 1024)[:, :64].reshape(512, 1024)
    g_conv_w = jnp.concatenate([s_gate[0:3], s_val[0:3]], axis=1)
    g_blocks = {
        "w_o_mla": _full_to_cols(unpad_o(g_w_o_mla_p), 128), "w_o_fox": _full_to_cols(unpad_o(g_w_o_fox_p), 128),
        "w_out": g_w_out.reshape(N_DEV, 128, 1024), "w_up": _full_to_cols(g_w_up, 704),
        "conv_w": _full_to_cols(g_conv_w, 704), "w_down": g_w_down.reshape(N_DEV, 352, 1024)}

    delta_mla = _attn_delta(o_mla, do_mla_b, "mla_delta")
    dq_rot, dk_rot, dv_mla, *late_recv = _attn_bwd(
        q_att, k_att, v_att, do_mla_b, lse_mla, delta_mla, None, MLA_SCALE, 1.0 / LOG2E, "mla_bwd", BF16,
        scatter=[g_blocks[k].astype(BF16) for k in LATE])
    dq_pre = _rope(dq_rot, rope_c, -rope_s, "rope_q_bwd")
    dkv_pre = _rope_bwd_kv(dk_rot, dv_mla, rope_c, -rope_s, "rope_kv_bwd")
    dqn = _mm(dq_pre, w_uq_p, "nt", F32, "q_up_dx")
    g_w_uq_p = _mm(qn, dq_pre, "tn", F32, "q_up_dw", tk=s)
    dkv_in = _mm(dkv_pre, w_kv_comb, "nt", F32, "kv_up_dx")
    g_w_kv_comb = _mm(kv_in, dkv_pre, "tn", F32, "kv_up_dw", tk=s)
    dcq, sums_q = _rms_mod_bwd(dqn, cq, q_norm_g, jnp.zeros((1, 384), F32), None, "q_norm_bwd")
    dckv, sums_kv = _rms_mod_bwd(dkv_in[:, :256], ckv, kv_norm_g, jnp.zeros((1, 256), F32), None, "kv_norm_bwd")
    delta_fox = _attn_delta(o_fox, do_fox_b, "fox_delta")
    dfq, dfk, dfv, dfr, dfc = _attn_bwd(fq, fk, fv, do_fox_b, lse_fox, delta_fox, f_row,
                                        FOX_SCALE, 1.0 / LOG2E, "fox_bwd", BF16)
    df_rows = dfr.reshape(N_HEADS, s // SEQ_LANES, SEQ_LANES)
    df_cols = dfc.reshape(N_HEADS, s // SEQ_LANES, SEQ_LANES)
    dz, db_f = _forget_bwd(df_rows, df_cols, z, bias_f, "forget_bwd")
    dfl = jnp.pad(dz.reshape(N_HEADS, s).T, ((0, 0), (0, 128 - N_HEADS)))

    dproj = jnp.concatenate([
        dgates, _unpad_heads(dfq, 64), _unpad_heads(dfk, 64), _unpad_heads(dfv, 64),
        dcq.astype(BF16), dckv.astype(BF16), dkv_in[:, 256:384].astype(BF16), dfl.astype(BF16)], axis=1)
    g_w_in_p = _mm(h1, dproj, "tn", F32, "proj_in_dw", tm=512, tn=640, tk=s)

    g_w_in = _w_in_unpadded(g_w_in_p)
    g_uq = g_w_uq_p.reshape(MLA_Q_RANK, N_HEADS, HEAD_PAD)[:, :, :96].transpose(1, 0, 2)
    g_uk = g_w_kv_comb[:256, :1024].reshape(256, N_HEADS, HEAD_PAD)[:, :, :64]
    g_uv = g_w_kv_comb[:256, 1024:].reshape(256, N_HEADS, HEAD_PAD)[:, :, :64]
    g_ukv = jnp.concatenate([g_uk, g_uv], axis=2).transpose(1, 0, 2)
    g_blocks.update({"w_in": _full_to_cols(g_w_in, 533), "w_uq": g_uq, "w_ukv": g_ukv})
    dh1, *early_recv = _mm(dproj, w_in_p, "nt", F32, "proj_in_dx", tn=512, tk=D_IN_P,
                           scatter=[g_blocks[k].astype(BF16) for k in EARLY])
    grad_x, sums_mix = _rms_mod_bwd(dh1, x, norm_mix_g, sc_m, dx2, "norm_mix_bwd")
    g_big, d_big, nm_big, nv_big = {}, {}, {}, {}
    for k, parts in zip(BIG, list(early_recv) + list(late_recv)):
        g_big[k], d_big[k], nm_big[k], nv_big[k] = [
            t[None] for t in _adamw(parts, big_w[k][0], big_m[k][0], big_v[k][0], "adamw_" + k)]

    dada = jnp.concatenate([sums_mix[0:1], sums_mix[1:2], sums_ffn[3:4], sums_ffn[0:1], sums_ffn[1:2], sums_final[2:3]],
                           axis=1)
    small_part = _pack_small({
        "b_ada": dada, "norm_mix_g": sums_mix[2:3], "q_norm_g": sums_q[2:3], "kv_norm_g": sums_kv[2:3],
        "b_forget": db_f[:, 0, 0], "norm_ffn_g": sums_ffn[2:3],
        "conv_b": jnp.concatenate([s_gate[3:4], s_val[3:4]], axis=1), "norm_final_g": sums_final[0:1],
        "loss": sums_final[1:2, 0:1]})
    (small_all,) = _all_gather([small_part], "gather_small")
    zero1 = jnp.zeros((1,), F32)
    small_w = {"b_ada": b_ada, "norm_mix_g": norm_mix_g, "q_norm_g": q_norm_g, "kv_norm_g": kv_norm_g,
               "b_forget": b_forget, "norm_ffn_g": norm_ffn_g, "conv_b": conv_b, "norm_final_g": norm_final_g,
               "loss": zero1}
    small_m = {"b_ada": m_b_ada, "norm_mix_g": m_norm_mix_g, "q_norm_g": m_q_norm_g, "kv_norm_g": m_kv_norm_g,
               "b_forget": m_b_forget, "norm_ffn_g": m_norm_ffn_g, "conv_b": m_conv_b,
               "norm_final_g": m_norm_final_g, "loss": zero1}
    small_v = {"b_ada": v_b_ada, "norm_mix_g": v_norm_mix_g, "q_norm_g": v_q_norm_g, "kv_norm_g": v_kv_norm_g,
               "b_forget": v_b_forget, "norm_ffn_g": v_norm_ffn_g, "conv_b": v_conv_b,
               "norm_final_g": v_norm_final_g, "loss": zero1}
    g_sm, d_sm, nm_sm, nv_sm = _adamw(small_all, _pack_small(small_w), _pack_small(small_m), _pack_small(small_v),
                                      "adamw_small")
    loss = g_sm[0, SMALL_OFF["loss"]]

    dada_all = small_all[:, 0, SMALL_OFF["b_ada"]:SMALL_OFF["b_ada"] + N_ADA * D_MODEL]
    dada_mine = lax.dynamic_slice(dada_all, (0, me * 768), (N_DEV, 768))
    g_ada_local = _ada_bwd(act_col, dada_mine, "ada_bwd")
    g_ada, d_ada, nm_ada, nv_ada = _adamw(g_ada_local[None], w_ada[0], m_w_ada[0], v_w_ada[0], "adamw_ada")

    def small_out(t, nme, shape):
        real = dict((n_, r_) for n_, r_, _ in SMALL)[nme]
        o = SMALL_OFF[nme]
        return t[0, o:o + real].reshape(shape)

    order = ["w_ada", "b_ada", "norm_mix_g", "w_in", "q_norm_g", "w_uq", "kv_norm_g", "w_ukv", "b_forget",
             "w_o_mla", "w_o_fox", "w_out", "norm_ffn_g", "w_up", "conv_w", "conv_b", "w_down", "norm_final_g"]
    small_shapes = {"b_ada": (1, 6144), "norm_mix_g": (1, 1024), "q_norm_g": (1, 384), "kv_norm_g": (1, 256),
                    "b_forget": (1, 8), "norm_ffn_g": (1, 1024), "conv_b": (1, 5632), "norm_final_g": (1024,)}

    def family(big, small, ada_t):
        out = []
        for nme in order:
            if nme == "w_ada":
                out.append(ada_t[None])
            elif nme in small_shapes:
                out.append(small_out(small, nme, small_shapes[nme]))
            else:
                out.append(big[nme])
        return out

    return (loss, grad_x[None], *family(g_big, g_sm, g_ada), *family(d_big, d_sm, d_ada),
            *family(nm_big, nm_sm, nm_ada), *family(nv_big, nv_sm, nv_ada))
```

```python
import math

import numpy as np
import jax
import jax.numpy as jnp
from jax import lax
from jax.experimental import pallas as pl
from jax.experimental.pallas import tpu as pltpu

F32 = jnp.float32
BF16 = jnp.bfloat16

N_DEV = 8
D_MODEL = 1024
N_HEADS = 8
HEAD_PAD = 128
MLA_Q_RANK = 384
MLA_KV_RANK = 256
MLA_NOPE = 64
MLA_ROPE = 32
MLA_V = 64
FOX_DIM = 64
D_FF = 2816
N_ADA = 6
EPS = 1e-6
ROPE_THETA = 10000.0
MLA_SCALE = 1.0 / math.sqrt(MLA_NOPE + MLA_ROPE)
FOX_SCALE = 1.0 / math.sqrt(FOX_DIM)
IN_SPLITS = (384, 256, 32, 512, 512, 512, 8, 1024, 1024)
D_IN = sum(IN_SPLITS)
IN_OFF = tuple(int(v) for v in np.cumsum((0,) + IN_SPLITS))
P_GM, P_GF, P_FQ, P_FK, P_FV, P_CQ, P_CKV, P_KR, P_FL, D_IN_P = 0, 1024, 2048, 2560, 3072, 3584, 3968, 4224, 4352, 4480

ADAM_LR, ADAM_B1, ADAM_B2, ADAM_EPS, ADAM_WD, ADAM_STEP = 0.001, 0.9, 0.999, 1e-08, 0.01, 10

VMEM_LIMIT_BYTES = 56 * 1024 * 1024
NEG_BIG = -1e30
ATT_T = 512
LOG2E = 1.4426950408889634
SUM_LANE = 64
ROW_T = 512
SEQ_LANES = 128
BF16_ROWS = 16
FWD_HEADS_PER_STEP = 4
BWD_HEADS_PER_STEP = 2


def _params(sem):
    return pltpu.CompilerParams(dimension_semantics=sem, vmem_limit_bytes=VMEM_LIMIT_BYTES)


def _tile(n, target, step=128):
    if n <= target:
        return n
    t = (target // step) * step
    while t >= step:
        if n % t == 0:
            return t
        t -= step
    return n


def _vec_spec(w, nargs):
    if nargs == 1:
        return pl.BlockSpec((1, w), lambda i: (0, 0))
    return pl.BlockSpec((1, w), lambda i, j: (0, 0))


def _comm_call(body, name, ins, out_shapes):
    n = len(ins)
    any_spec = pl.BlockSpec(memory_space=pl.ANY)
    return pl.pallas_call(
        body, name=name, out_shape=tuple(out_shapes),
        in_specs=[any_spec] * n, out_specs=tuple([any_spec] * n),
        scratch_shapes=[pltpu.SemaphoreType.DMA((n, 7)), pltpu.SemaphoreType.DMA((n, 7)),
                        pltpu.SemaphoreType.DMA((n,))],
    )(*ins)


def _all_gather(xs, name):
    n = len(xs)

    def body(*refs):
        x_refs, out_refs = refs[:n], refs[n:2 * n]
        send_sems, recv_sems, local_sems = refs[2 * n:]
        x_, y_, c_ = lax.axis_index("x"), lax.axis_index("y"), lax.axis_index("c")
        me, sibling = (x_, y_, c_), (x_, y_, 1 - c_)
        chips = [(1 - x_, y_), (x_, 1 - y_), (1 - x_, 1 - y_)]

        def slot(a, px, py, pc):
            return out_refs[a].at[4 * px + 2 * py + pc]

        def copy(a, k, block, to, src=None):
            return pltpu.make_async_remote_copy(
                src_ref=slot(a, *block) if src is None else src, dst_ref=slot(a, *block),
                send_sem=send_sems.at[a, k], recv_sem=recv_sems.at[a, k],
                device_id=to, device_id_type=pl.DeviceIdType.MESH)

        mine = [pltpu.make_async_copy(x_refs[a], slot(a, *me), local_sems.at[a]) for a in range(n)]
        for cp in mine:
            cp.start()
        first = []
        for a in range(n):
            first.append(copy(a, 0, me, sibling, src=x_refs[a]))
            first += [copy(a, 1 + j, me, (*chip, c_), src=x_refs[a]) for j, chip in enumerate(chips)]
        for cp in first:
            cp.start()
        passed = []
        for j, chip in enumerate(chips):
            for a in range(n):
                copy(a, 1 + j, (*chip, c_), me).wait_recv()
                passed.append(copy(a, 4 + j, (*chip, c_), sibling))
                passed[-1].start()
        for a in range(n):
            copy(a, 0, sibling, me).wait_recv()
            for j, chip in enumerate(chips):
                copy(a, 4 + j, (*chip, 1 - c_), me).wait_recv()
        for cp in first + passed:
            cp.wait_send()
        for cp in mine:
            cp.wait()

    return _comm_call(body, name, xs, [jax.ShapeDtypeStruct((N_DEV,) + x.shape, x.dtype) for x in xs])


def _direct_exchange(src_refs, out_refs, send_sems, recv_sems, local_sems, scatter):
    n = len(src_refs)
    x_, y_, c_ = lax.axis_index("x"), lax.axis_index("y"), lax.axis_index("c")
    me = 4 * x_ + 2 * y_ + c_

    def peer(k):
        return (x_ ^ ((k >> 2) & 1), y_ ^ ((k >> 1) & 1), c_ ^ (k & 1))

    def src(a, slot):
        return src_refs[a].at[slot] if scatter else src_refs[a]

    def copy(a, k, sending):
        px, py, pc = peer(k)
        theirs = 4 * px + 2 * py + pc
        return pltpu.make_async_remote_copy(
            src_ref=src(a, theirs if sending else me), dst_ref=out_refs[a].at[me if sending else theirs],
            send_sem=send_sems.at[a, k - 1], recv_sem=recv_sems.at[a, k - 1],
            device_id=(px, py, pc), device_id_type=pl.DeviceIdType.MESH)

    mine = [pltpu.make_async_copy(src(a, me), out_refs[a].at[me], local_sems.at[a]) for a in range(n)]
    sends = [copy(a, k, True) for a in range(n) for k in range(1, N_DEV)]

    def start():
        for cp in mine + sends:
            cp.start()

    def wait():
        for a in range(n):
            for k in range(1, N_DEV):
                copy(a, k, False).wait_recv()
        for cp in sends:
            cp.wait_send()
        for cp in mine:
            cp.wait()

    return start, wait


def _exchange_scratch(n):
    return [pltpu.SemaphoreType.DMA((n, 7)), pltpu.SemaphoreType.DMA((n, 7)), pltpu.SemaphoreType.DMA((n,))]


def _mm(a, b, mode, out_dtype, name, res=None, gvec=None, tm=1024, tn=512, tk=1024, rope=None, ones_lane=False,
        scatter=()):
    (k, m) = a.shape if mode == "tn" else a.shape[::-1]
    n = b.shape[0] if mode == "nt" else b.shape[1]
    tm, tn, tk = _tile(m, tm), _tile(n, tn), _tile(k, tk)
    nk = k // tk
    dims = {"nn": (((1,), (0,)), ((), ())), "nt": (((1,), (1,)), ((), ())), "tn": (((0,), (0,)), ((), ()))}[mode]
    has_res, has_g = res is not None, gvec is not None
    fused = has_res and has_g

    n_rope = 2 if rope is not None else 0
    nx = len(scatter)
    assert not nx or nk == 1, "the exchange rides only on a matmul with one K step"
    n_in = 2 + has_res + has_g + n_rope

    def body(*refs):
        acc_ref = refs[-1] if nk > 1 else None
        if nx:
            ins, x_src, outs, x_out, _, x_sems = _split_refs(refs, n_in, 1 + fused, 0, nx)
            refs = list(ins) + list(outs)
            i, j = pl.program_id(0), pl.program_id(1)
            x_start, x_wait = _direct_exchange(x_src, x_out, *x_sems, scatter=True)
            pl.when(jnp.logical_and(i == 0, j == 0))(x_start)
        else:
            refs = list(refs[:n_in + 1 + fused])
        a_ref, b_ref = refs[:2]
        res_ref = refs[2] if has_res else None
        g_ref = refs[2 + has_res] if has_g else None
        o_ref = refs[n_in]
        part = lax.dot_general(a_ref[...], b_ref[...], dims, preferred_element_type=F32)

        def finish(acc):
            if fused:
                refs[-1][...] = acc
            out = g_ref[...] * acc if has_g else acc
            if has_res:
                out = res_ref[...] + out
            if n_rope or ones_lane:
                one = (lax.broadcasted_iota(jnp.int32, (tm, HEAD_PAD), 1) == SUM_LANE).astype(F32)
                for hb in range(tn // HEAD_PAD):
                    lanes = slice(hb * HEAD_PAD, (hb + 1) * HEAD_PAD)
                    seg = out[:, lanes]
                    seg = _rope_block(seg, refs[n_in - 2][...], refs[n_in - 1][...]) if n_rope else seg + one
                    o_ref[:, lanes] = seg.astype(o_ref.dtype)
            else:
                o_ref[...] = out.astype(o_ref.dtype)

        if nk == 1:
            finish(part)
            if nx:
                pl.when(jnp.logical_and(i == m // tm - 1, j == n // tn - 1))(x_wait)
            return
        kk = pl.program_id(2)

        @pl.when(kk == 0)
        def _():
            acc_ref[...] = part

        @pl.when(kk > 0)
        def _():
            acc_ref[...] += part

        @pl.when(kk == nk - 1)
        def _():
            finish(acc_ref[...])

    if mode == "tn":
        a_spec = pl.BlockSpec((tk, tm), lambda i, j, kk: (kk, i))
    else:
        a_spec = pl.BlockSpec((tm, tk), lambda i, j, kk: (i, kk))
    if mode == "nt":
        b_spec = pl.BlockSpec((tn, tk), lambda i, j, kk: (j, kk))
    else:
        b_spec = pl.BlockSpec((tk, tn), lambda i, j, kk: (kk, j))
    o_spec = pl.BlockSpec((tm, tn), lambda i, j, kk: (i, j))
    in_specs, args = [a_spec, b_spec], [a, b]
    out_specs, out_shape = o_spec, jax.ShapeDtypeStruct((m, n), out_dtype)
    if has_res:
        in_specs.append(o_spec)
        args.append(res)
    if has_g:
        in_specs.append(pl.BlockSpec((1, tn), lambda i, j, kk: (0, j)))
        args.append(gvec)
    if n_rope:
        in_specs += [pl.BlockSpec((tm, HEAD_PAD), lambda i, j, kk: (i, 0))] * 2
        args += list(rope)
    if fused:
        out_specs = (o_spec, o_spec)
        out_shape = (out_shape, jax.ShapeDtypeStruct((m, n), F32))
    scratch = [pltpu.VMEM((tm, tn), F32)] if nk > 1 else []
    if nx:
        any_spec = pl.BlockSpec(memory_space=pl.ANY)
        in_specs += [any_spec] * nx
        args += list(scatter)
        out_specs = tuple(out_specs if fused else (out_specs,)) + (any_spec,) * nx
        out_shape = tuple(out_shape if fused else (out_shape,)) + tuple(
            jax.ShapeDtypeStruct(g.shape, g.dtype) for g in scatter)
        scratch += _exchange_scratch(nx)
    return pl.pallas_call(
        body, name=name, grid=(m // tm, n // tn, nk),
        in_specs=in_specs, out_specs=out_specs, out_shape=out_shape,
        scratch_shapes=scratch,
        compiler_params=_params(("arbitrary",) * 3 if nx else ("parallel", "parallel", "arbitrary")),
    )(*args)


def _rms_mod(x, g, sc, sh, name):
    s, w = x.shape
    tm = _tile(s, ROW_T)

    def body(x_ref, g_ref, sc_ref, sh_ref, o_ref):
        xv = x_ref[...]
        r = lax.rsqrt(jnp.mean(xv * xv, axis=-1, keepdims=True) + EPS)
        o_ref[...] = ((xv * r * g_ref[...]) * (1.0 + sc_ref[...]) + sh_ref[...]).astype(o_ref.dtype)

    row = pl.BlockSpec((tm, w), lambda i: (i, 0))
    return pl.pallas_call(
        body, name=name, grid=(s // tm,),
        in_specs=[row, _vec_spec(w, 1), _vec_spec(w, 1), _vec_spec(w, 1)],
        out_specs=row, out_shape=jax.ShapeDtypeStruct((s, w), BF16),
        compiler_params=_params(("parallel",)),
    )(x, g, sc, sh)


def _latent_norm(lat, q_g, kv_g, name):
    s, w = lat.shape
    tm = _tile(s, ROW_T)
    nq_, nkv = MLA_Q_RANK, MLA_KV_RANK

    def norm(xv, gv):
        return xv * lax.rsqrt(jnp.mean(xv * xv, axis=-1, keepdims=True) + EPS) * gv

    def body(lat_ref, qg_ref, kg_ref, qn_ref, kv_ref):
        qn_ref[...] = norm(lat_ref[:, 0:nq_], qg_ref[...]).astype(BF16)
        kv_ref[:, 0:nkv] = norm(lat_ref[:, nq_:nq_ + nkv], kg_ref[...]).astype(BF16)
        kv_ref[:, nkv:nkv + HEAD_PAD] = lat_ref[:, nq_ + nkv:nq_ + nkv + HEAD_PAD].astype(BF16)

    out = lambda n: pl.BlockSpec((tm, n), lambda i: (i, 0))
    return pl.pallas_call(
        body, name=name, grid=(s // tm,),
        in_specs=[out(w), _vec_spec(nq_, 1), _vec_spec(nkv, 1)],
        out_specs=(out(nq_), out(nkv + HEAD_PAD)),
        out_shape=(jax.ShapeDtypeStruct((s, nq_), BF16), jax.ShapeDtypeStruct((s, nkv + HEAD_PAD), BF16)),
        compiler_params=_params(("parallel",)),
    )(lat, q_g, kv_g)


def _rms_mod_bwd(dh, x, g, sc, dres, name, branch=None):
    s, w = x.shape
    tm = _tile(s, ROW_T)
    has_res, has_br = dres is not None, branch is not None

    def body(*refs):
        dh_ref, x_ref, g_ref, sc_ref = refs[:4]
        rest = list(refs[4:])
        dres_ref = rest.pop(0) if has_res else None
        val_ref, bg_ref = (rest.pop(0), rest.pop(0)) if has_br else (None, None)
        dx_ref = rest.pop(0)
        db_ref = rest.pop(0) if has_br else None
        sums_ref = rest.pop(0)
        xv, dhv, gv = x_ref[...], dh_ref[...], g_ref[...]
        r = lax.rsqrt(jnp.mean(xv * xv, axis=-1, keepdims=True) + EPS)
        xhat = xv * r
        dxn = dhv * (1.0 + sc_ref[...])
        dxhat = dxn * gv
        dx = r * (dxhat - xhat * jnp.mean(dxhat * xhat, axis=-1, keepdims=True))
        if has_res:
            dx = dx + dres_ref[...]
        dx_ref[...] = dx

        @pl.when(pl.program_id(0) == 0)
        def _():
            sums_ref[...] = jnp.zeros_like(sums_ref)

        sums_ref[0:1, :] += jnp.sum(dhv, axis=0, keepdims=True)
        sums_ref[1:2, :] += jnp.sum(dhv * (xhat * gv), axis=0, keepdims=True)
        sums_ref[2:3, :] += jnp.sum(dxn * xhat, axis=0, keepdims=True)
        if has_br:
            db_ref[...] = (dx * bg_ref[...]).astype(db_ref.dtype)
            sums_ref[3:4, :] += jnp.sum(dx * val_ref[...], axis=0, keepdims=True)

    row = pl.BlockSpec((tm, w), lambda i: (i, 0))
    in_specs = [row, row, _vec_spec(w, 1), _vec_spec(w, 1)] + ([row] if has_res else [])
    args = [dh, x, g, sc] + ([dres] if has_res else [])
    out_specs, out_shape = [row], [jax.ShapeDtypeStruct((s, w), F32)]
    if has_br:
        in_specs += [row, _vec_spec(w, 1)]
        args += list(branch)
        out_specs.append(row)
        out_shape.append(jax.ShapeDtypeStruct((s, w), BF16))
    out_specs.append(pl.BlockSpec((8, w), lambda i: (0, 0)))
    out_shape.append(jax.ShapeDtypeStruct((8, w), F32))
    return pl.pallas_call(
        body, name=name, grid=(s // tm,),
        in_specs=in_specs, out_specs=tuple(out_specs), out_shape=tuple(out_shape),
        compiler_params=_params(("arbitrary",)),
    )(*args)


def _final_loss(x3, target, g, ffn, gvec, name):
    s, w = x3.shape
    tm = _tile(s, ROW_T)

    def body(x_ref, t_ref, g_ref, ffn_ref, bg_ref, dx_ref, db_ref, sums_ref):
        xv, gv = x_ref[...], g_ref[...]
        r = lax.rsqrt(jnp.mean(xv * xv, axis=-1, keepdims=True) + EPS)
        xhat = xv * r
        err = xhat * gv - t_ref[...]
        dy = err * (1.0 / w)
        dxhat = dy * gv
        dx = r * (dxhat - xhat * jnp.mean(dxhat * xhat, axis=-1, keepdims=True))
        dx_ref[...] = dx
        db_ref[...] = (dx * bg_ref[...]).astype(db_ref.dtype)

        @pl.when(pl.program_id(0) == 0)
        def _():
            sums_ref[...] = jnp.zeros_like(sums_ref)

        sums_ref[0:1, :] += jnp.sum(dy * xhat, axis=0, keepdims=True)
        sums_ref[1:2, :] += jnp.zeros((1, w), F32) + (0.5 / w) * jnp.sum(err * err)
        sums_ref[2:3, :] += jnp.sum(dx * ffn_ref[...], axis=0, keepdims=True)

    row = pl.BlockSpec((tm, w), lambda i: (i, 0))
    return pl.pallas_call(
        body, name=name, grid=(s // tm,),
        in_specs=[row, row, _vec_spec(w, 1), row, _vec_spec(w, 1)],
        out_specs=(row, row, pl.BlockSpec((8, w), lambda i: (0, 0))),
        out_shape=(jax.ShapeDtypeStruct((s, w), F32), jax.ShapeDtypeStruct((s, w), BF16),
                   jax.ShapeDtypeStruct((8, w), F32)),
        compiler_params=_params(("arbitrary",)),
    )(x3, target, g, ffn, gvec)


def _rope_block(seg, cmul, smul):
    lane = lax.broadcasted_iota(jnp.int32, seg.shape, 1)
    swapped = jnp.where(lane < MLA_NOPE + MLA_ROPE // 2,
                        pltpu.roll(seg, HEAD_PAD - MLA_ROPE // 2, 1), pltpu.roll(seg, MLA_ROPE // 2, 1))
    return seg * cmul + swapped * smul


def _rope(t, cmul, smul, name):
    s, w = t.shape
    tm = _tile(s, ROW_T)

    def body(t_ref, c_ref, s_ref, o_ref):
        cv, sv = c_ref[...], s_ref[...]
        for hb in range(w // HEAD_PAD):
            lanes = slice(hb * HEAD_PAD, (hb + 1) * HEAD_PAD)
            o_ref[:, lanes] = _rope_block(t_ref[:, lanes].astype(F32), cv, sv).astype(o_ref.dtype)

    row = pl.BlockSpec((tm, w), lambda i: (i, 0))
    tab = pl.BlockSpec((tm, HEAD_PAD), lambda i: (i, 0))
    return pl.pallas_call(
        body, name=name, grid=(s // tm,),
        in_specs=[row, tab, tab], out_specs=row, out_shape=jax.ShapeDtypeStruct((s, w), BF16),
        compiler_params=_params(("parallel",)),
    )(t, cmul, smul)


def _rope_bwd_kv(dk, dv, cmul, smul, name):
    s, w = dk.shape
    tm = _tile(s, ROW_T)

    def body(dk_ref, dv_ref, c_ref, s_ref, o_ref):
        cv, sv = c_ref[...], s_ref[...]
        for hb in range(N_HEADS):
            lo, hi = hb * HEAD_PAD, (hb + 1) * HEAD_PAD
            o_ref[:, lo:hi] = _rope_block(dk_ref[:, lo:hi].astype(F32), cv, sv).astype(o_ref.dtype)
        o_ref[:, w:2 * w] = dv_ref[...].astype(o_ref.dtype)

    row = pl.BlockSpec((tm, w), lambda i: (i, 0))
    tab = pl.BlockSpec((tm, HEAD_PAD), lambda i: (i, 0))
    return pl.pallas_call(
        body, name=name, grid=(s // tm,),
        in_specs=[row, row, tab, tab],
        out_specs=pl.BlockSpec((tm, 2 * w), lambda i: (i, 0)),
        out_shape=jax.ShapeDtypeStruct((s, 2 * w), BF16),
        compiler_params=_params(("parallel",)),
    )(dk, dv, cmul, smul)


def _lanes(col, width):
    if col.shape[1] == 1:
        col = jnp.broadcast_to(col, (col.shape[0], HEAD_PAD))
    return jnp.tile(col, (1, width // HEAD_PAD))


def _fold_lanes(a):
    out = a[:, 0:HEAD_PAD]
    for g in range(1, a.shape[1] // HEAD_PAD):
        out = out + a[:, g * HEAD_PAD:(g + 1) * HEAD_PAD]
    return out


def _as_row(rep):
    return rep.T[0:1, :]


def _causal(t, rows_are_queries):
    row = lax.broadcasted_iota(jnp.int32, (t, t), 0)
    col = lax.broadcasted_iota(jnp.int32, (t, t), 1)
    return row >= col if rows_are_queries else col >= row


def _split_refs(refs, n_in, n_out, n_scratch, n_x):
    pos = [n_in, n_x, n_out, n_x, n_scratch, 3 if n_x else 0]
    out, at = [], 0
    for cnt in pos:
        out.append(refs[at:at + cnt])
        at += cnt
    return out


def _first_last_step(n0, n1):
    i0, i1 = pl.program_id(0), pl.program_id(1)
    return jnp.logical_and(i0 == 0, i1 == 0), jnp.logical_and(i0 == n0 - 1, i1 == n1 - 1)


def _as_lanes(row):
    return jnp.broadcast_to(row, (HEAD_PAD, row.shape[1])).T


def _attn_fwd(q, k, v, frow, name, gather=()):
    s = q.shape[0]
    t = ATT_T
    nq = s // t
    use_f = frow is not None
    nx = len(gather)

    hpb = FWD_HEADS_PER_STEP

    def body(*refs):
        ins, x_src, outs, x_out, scr, x_sems = _split_refs(refs, 4 if use_f else 3, 3, 2, nx)
        if use_f:
            q_ref, k_ref, v_ref, fr_ref = ins
            fc_b = [_as_lanes(fr_ref[hh, pl.program_id(1)]) for hh in range(hpb)]
        else:
            q_ref, k_ref, v_ref = ins
        o_ref, ob_ref, lse_ref = outs
        m_s, acc_s = scr
        if nx:
            first, last = _first_last_step(N_HEADS // hpb, nq)
            x_start, x_wait = _direct_exchange(x_src, x_out, *x_sems, scatter=False)
            pl.when(first)(x_start)
        qi = pl.program_id(1)
        m_s[...] = jnp.full(m_s.shape, NEG_BIG, F32)
        acc_s[...] = jnp.zeros(acc_s.shape, F32)

        def step(j, masked):
            off = pl.multiple_of(j * t, t)
            for hh in range(hpb):
                lanes = slice(hh * HEAD_PAD, (hh + 1) * HEAD_PAD)
                kv = k_ref[pl.ds(off, t), lanes]
                vv = v_ref[pl.ds(off, t), lanes]
                sc = lax.dot_general(q_ref[:, lanes], kv, (((1,), (1,)), ((), ())), preferred_element_type=F32)
                if use_f:
                    sc = sc + (_lanes(fc_b[hh], t) - fr_ref[hh, j])
                if masked:
                    sc = jnp.where(_causal(t, True), sc, NEG_BIG)
                m_prev = m_s[hh]
                m_new = jnp.maximum(m_prev, jnp.max(sc, axis=-1, keepdims=True))
                p = jnp.exp2(sc - _lanes(m_new, t))
                acc_s[hh] = jnp.exp2(m_prev - m_new) * acc_s[hh] + jnp.dot(p.astype(BF16), vv,
                                                                           preferred_element_type=F32)
                m_s[hh] = m_new

        def loop_body(j, carry):
            step(j, False)
            return carry

        lax.fori_loop(0, qi, loop_body, 0)
        step(qi, True)
        for hh in range(hpb):
            lanes = slice(hh * HEAD_PAD, (hh + 1) * HEAD_PAD)
            acc = acc_s[hh]
            lane = lax.broadcasted_iota(jnp.int32, acc.shape, 1)
            denom = jnp.sum(jnp.where(lane == SUM_LANE, acc, 0.0), axis=-1, keepdims=True)
            o = acc * (1.0 / denom)
            o_ref[:, lanes] = o
            ob_ref[:, lanes] = o.astype(BF16)
            lse_ref[hh, 0] = _as_row(m_s[hh] + jnp.log(denom) * LOG2E)
        if nx:
            pl.when(last)(x_wait)

    w = hpb * HEAD_PAD
    qspec = pl.BlockSpec((t, w), lambda h, i: (i, h))
    kspec = pl.BlockSpec((s, w), lambda h, i: (0, h))
    any_spec = pl.BlockSpec(memory_space=pl.ANY)
    in_specs, args = [qspec, kspec, kspec], [q, k, v]
    if use_f:
        in_specs += [pl.BlockSpec((hpb, nq, 1, t), lambda h, i: (h, 0, 0, 0))]
        args += [frow]
    out_specs = [qspec, qspec, pl.BlockSpec((hpb, 1, 1, t), lambda h, i: (h, i, 0, 0))]
    out_shape = [jax.ShapeDtypeStruct((s, N_HEADS * HEAD_PAD), F32), jax.ShapeDtypeStruct((s, N_HEADS * HEAD_PAD), BF16),
                 jax.ShapeDtypeStruct((N_HEADS, nq, 1, t), F32)]
    scratch = [pltpu.VMEM((hpb, t, HEAD_PAD), F32), pltpu.VMEM((hpb, t, HEAD_PAD), F32)]
    if nx:
        in_specs += [any_spec] * nx
        args += list(gather)
        out_specs += [any_spec] * nx
        out_shape += [jax.ShapeDtypeStruct((N_DEV,) + g.shape, g.dtype) for g in gather]
        scratch += _exchange_scratch(nx)
    return pl.pallas_call(
        body, name=name, grid=(N_HEADS // hpb, nq),
        in_specs=in_specs, out_specs=tuple(out_specs), out_shape=tuple(out_shape),
        scratch_shapes=scratch,
        compiler_params=_params(("arbitrary", "arbitrary") if nx else ("parallel", "arbitrary")),
    )(*args)


def _attn_delta(o, do, name):
    s, w = o.shape
    t = ATT_T

    def body(o_ref, do_ref, d_ref):
        for hb in range(N_HEADS):
            lo, hi = hb * HEAD_PAD, (hb + 1) * HEAD_PAD
            prod = o_ref[:, lo:hi] * do_ref[:, lo:hi].astype(F32)
            d_ref[hb, 0] = jnp.sum(prod.T, axis=0, keepdims=True)

    row = pl.BlockSpec((t, w), lambda i: (i, 0))
    return pl.pallas_call(
        body, name=name, grid=(s // t,),
        in_specs=[row, row],
        out_specs=pl.BlockSpec((N_HEADS, 1, 1, t), lambda i: (0, i, 0, 0)),
        out_shape=jax.ShapeDtypeStruct((N_HEADS, s // t, 1, t), F32),
        compiler_params=_params(("parallel",)),
    )(o, do)


def _attn_bwd(q, k, v, do, lse_row, delta_row, frow, scale_q, scale_k, name, out_dtype, scatter=()):
    s = q.shape[0]
    t = ATT_T
    nq = s // t
    use_f = frow is not None
    nx = len(scatter)
    hpb = BWD_HEADS_PER_STEP

    def body(*refs):
        ins, x_src, outs, x_out, scr, x_sems = _split_refs(refs, 7 if use_f else 6, 5 if use_f else 3,
                                                           5 if use_f else 3, nx)
        if use_f:
            q_ref, k_ref, v_ref, do_ref, lse_ref, dl_ref, fr_ref = ins
            dq_ref, dk_ref, dv_ref, dr_ref, df_ref = outs
            dq_s, dk_s, dv_s, dr_s, df_s = scr
            fc_b = [_as_lanes(fr_ref[hh, pl.program_id(1)]) for hh in range(hpb)]
        else:
            q_ref, k_ref, v_ref, do_ref, lse_ref, dl_ref = ins
            dq_ref, dk_ref, dv_ref = outs
            dq_s, dk_s, dv_s = scr
        if nx:
            first, last = _first_last_step(N_HEADS // hpb, nq)
            x_start, x_wait = _direct_exchange(x_src, x_out, *x_sems, scatter=True)
            pl.when(first)(x_start)
        kj = pl.program_id(1)

        @pl.when(kj == 0)
        def _():
            dq_s[...] = jnp.zeros(dq_s.shape, F32)
            if use_f:
                dr_s[...] = jnp.zeros(dr_s.shape, F32)

        dk_s[...] = jnp.zeros(dk_s.shape, F32)
        dv_s[...] = jnp.zeros(dv_s.shape, F32)
        if use_f:
            df_s[...] = jnp.zeros(df_s.shape, F32)

        def step(i, masked):
            off = pl.multiple_of(i * t, t)
            for hh in range(hpb):
                lanes = slice(hh * HEAD_PAD, (hh + 1) * HEAD_PAD)
                kv, vv = k_ref[:, lanes], v_ref[:, lanes]
                qv = q_ref[pl.ds(off, t), lanes]
                dov = do_ref[pl.ds(off, t), lanes]
                st = lax.dot_general(kv, qv, (((1,), (1,)), ((), ())), preferred_element_type=F32)
                if use_f:
                    st = st + (fr_ref[hh, i] - _lanes(fc_b[hh], t))
                if masked:
                    st = jnp.where(_causal(t, False), st, NEG_BIG)
                pt = jnp.exp2(st - lse_ref[hh, i])
                dv_s[hh] += jnp.dot(pt.astype(BF16), dov, preferred_element_type=F32)
                dpt = lax.dot_general(vv, dov, (((1,), (1,)), ((), ())), preferred_element_type=F32)
                dst = pt * (dpt - dl_ref[hh, i])
                dsb = dst.astype(BF16)
                dk_s[hh] += jnp.dot(dsb, qv, preferred_element_type=F32)
                dq_s[hh, pl.ds(off, t), :] += lax.dot_general(dsb, kv, (((0,), (0,)), ((), ())),
                                                              preferred_element_type=F32)
                if use_f:
                    df_s[hh] -= _fold_lanes(dst)
                    dr_s[hh, i] += jnp.sum(dst, axis=0, keepdims=True)

        step(kj, True)

        def loop_body(i, carry):
            step(i, False)
            return carry

        lax.fori_loop(kj + 1, nq, loop_body, 0)
        for hh in range(hpb):
            lanes = slice(hh * HEAD_PAD, (hh + 1) * HEAD_PAD)
            dk_ref[:, lanes] = (dk_s[hh] * scale_k).astype(dk_ref.dtype)
            dv_ref[:, lanes] = dv_s[hh].astype(dv_ref.dtype)
            if use_f:
                df_ref[hh, 0] = jnp.sum(df_s[hh].T, axis=0, keepdims=True)

        @pl.when(kj == nq - 1)
        def _():
            for hh in range(hpb):
                dq_ref[:, hh * HEAD_PAD:(hh + 1) * HEAD_PAD] = (dq_s[hh] * scale_q).astype(dq_ref.dtype)
            if use_f:
                dr_ref[...] = dr_s[...]

        if nx:
            pl.when(last)(x_wait)

    w = hpb * HEAD_PAD
    kspec = pl.BlockSpec((t, w), lambda h, j: (j, h))
    qspec = pl.BlockSpec((s, w), lambda h, j: (0, h))
    rowspec = pl.BlockSpec((hpb, nq, 1, t), lambda h, j: (h, 0, 0, 0))
    any_spec = pl.BlockSpec(memory_space=pl.ANY)
    in_specs, args = [qspec, kspec, kspec, qspec, rowspec, rowspec], [q, k, v, do, lse_row, delta_row]
    full = jax.ShapeDtypeStruct((s, N_HEADS * HEAD_PAD), out_dtype)
    out_specs, out_shape = [qspec, kspec, kspec], [full, full, full]
    scratch = [pltpu.VMEM((hpb, s, HEAD_PAD), F32), pltpu.VMEM((hpb, t, HEAD_PAD), F32),
               pltpu.VMEM((hpb, t, HEAD_PAD), F32)]
    if use_f:
        in_specs += [rowspec]
        args += [frow]
        out_specs += [rowspec, pl.BlockSpec((hpb, 1, 1, t), lambda h, j: (h, j, 0, 0))]
        out_shape += [jax.ShapeDtypeStruct((N_HEADS, nq, 1, t), F32)] * 2
        scratch += [pltpu.VMEM((hpb, nq, 1, t), F32), pltpu.VMEM((hpb, t, HEAD_PAD), F32)]
    if nx:
        in_specs += [any_spec] * nx
        args += list(scatter)
        out_specs += [any_spec] * nx
        out_shape += [jax.ShapeDtypeStruct(g.shape, g.dtype) for g in scatter]
        scratch += _exchange_scratch(nx)
    return pl.pallas_call(
        body, name=name, grid=(N_HEADS // hpb, nq),
        in_specs=in_specs, out_specs=tuple(out_specs), out_shape=tuple(out_shape),
        scratch_shapes=scratch,
        compiler_params=_params(("arbitrary", "arbitrary") if nx else ("parallel", "arbitrary")),
    )(*args)


def _gate_fwd(pm, pf, gates, name):
    s, w = pm.shape
    tm = _tile(s, ROW_T)

    def body(pm_ref, pf_ref, g_ref, y_ref):
        y = (jax.nn.sigmoid(g_ref[:, 0:w]) * pm_ref[...].astype(F32)
             + jax.nn.sigmoid(g_ref[:, w:2 * w]) * pf_ref[...].astype(F32))
        y_ref[...] = y.astype(y_ref.dtype)

    row = pl.BlockSpec((tm, w), lambda i: (i, 0))
    return pl.pallas_call(
        body, name=name, grid=(s // tm,),
        in_specs=[row, row, pl.BlockSpec((tm, 2 * w), lambda i: (i, 0))],
        out_specs=row, out_shape=jax.ShapeDtypeStruct((s, w), BF16),
        compiler_params=_params(("parallel",)),
    )(pm, pf, gates)


def _gate_bwd(dy, pm, pf, gates, name):
    s, w = pm.shape
    tm = _tile(s, ROW_T)

    def body(dy_ref, pm_ref, pf_ref, g_ref, dpm_ref, dpf_ref, dg_ref):
        dyv = dy_ref[...]
        sm, sf = jax.nn.sigmoid(g_ref[:, 0:w]), jax.nn.sigmoid(g_ref[:, w:2 * w])
        dpm_ref[...] = (dyv * sm).astype(BF16)
        dpf_ref[...] = (dyv * sf).astype(BF16)
        dg_ref[:, 0:w] = (dyv * pm_ref[...].astype(F32) * (sm * (1.0 - sm))).astype(BF16)
        dg_ref[:, w:2 * w] = (dyv * pf_ref[...].astype(F32) * (sf * (1.0 - sf))).astype(BF16)

    row = pl.BlockSpec((tm, w), lambda i: (i, 0))
    wide = pl.BlockSpec((tm, 2 * w), lambda i: (i, 0))
    out = jax.ShapeDtypeStruct((s, w), BF16)
    return pl.pallas_call(
        body, name=name, grid=(s // tm,),
        in_specs=[row, row, row, wide],
        out_specs=(row, row, wide), out_shape=(out, out, jax.ShapeDtypeStruct((s, 2 * w), BF16)),
        compiler_params=_params(("parallel",)),
    )(dy, pm, pf, gates)


CONV_TN = D_FF // 2
CONV_TM = 256
CONV_T_TM = 512
HALO = BF16_ROWS


def _shift_down(u, prev, n):
    rolled = pltpu.roll(u, n, 0)
    prev_rolled = pltpu.roll(prev, n, 0)
    top = jnp.concatenate([prev_rolled, rolled[HALO:]], axis=0)
    row = lax.broadcasted_iota(jnp.int32, u.shape, 0)
    return jnp.where(row < n, top, rolled)


def _conv_tile(u, prev, w_ref, b_ref):
    um1 = _shift_down(u, prev, 1)
    um2 = _shift_down(u, prev, 2)
    uc = b_ref[...] + w_ref[0:1, :] * um2 + w_ref[1:2, :] * um1 + w_ref[2:3, :] * u
    return uc, um1, um2


def _conv_specs(tm, tn, ncol_off):
    blk = lambda off: pl.BlockSpec((tm, tn), lambda j, i: (i, j + off))
    halo = lambda off: pl.BlockSpec((HALO, tn), lambda j, i: (jnp.maximum(i * (tm // HALO) - 1, 0), j + off))
    wsp = lambda off: pl.BlockSpec((3, tn), lambda j, i: (0, j + off))
    bsp = lambda off: pl.BlockSpec((1, tn), lambda j, i: (0, j + off))
    return blk, halo, wsp, bsp


def _convglu_fwd(u, conv_w, conv_b, name):
    s = u.shape[0]
    tm, tn = _tile(s, CONV_TM), CONV_TN
    nj = D_FF // tn
    blk, halo, wsp, bsp = _conv_specs(tm, tn, nj)

    def body(ug_ref, pg_ref, uv_ref, pv_ref, wg_ref, wv_ref, bg_ref, bv_ref, a_ref):
        live = (pl.program_id(1) > 0).astype(F32)
        gate, _, _ = _conv_tile(ug_ref[...].astype(F32), pg_ref[...].astype(F32) * live, wg_ref, bg_ref)
        val, _, _ = _conv_tile(uv_ref[...].astype(F32), pv_ref[...].astype(F32) * live, wv_ref, bv_ref)
        a_ref[...] = (gate * jax.nn.sigmoid(gate) * val).astype(a_ref.dtype)

    return pl.pallas_call(
        body, name=name, grid=(nj, s // tm),
        in_specs=[blk(0), halo(0), blk(nj), halo(nj), wsp(0), wsp(nj), bsp(0), bsp(nj)],
        out_specs=blk(0), out_shape=jax.ShapeDtypeStruct((s, D_FF), BF16),
        compiler_params=_params(("parallel", "arbitrary")),
    )(u, u, u, u, conv_w, conv_w, conv_b, conv_b)


def _convglu_bwd(da, u, conv_w, conv_b, name):
    s = u.shape[0]
    tm, tn = _tile(s, CONV_TM), CONV_TN
    nj = D_FF // tn
    blk, halo, wsp, bsp = _conv_specs(tm, tn, nj)

    def body(da_ref, ug_ref, pg_ref, uv_ref, pv_ref, wg_ref, wv_ref, bg_ref, bv_ref,
             dg_ref, dv_ref, sg_ref, sv_ref):
        live = (pl.program_id(1) > 0).astype(F32)
        ug, uv = ug_ref[...].astype(F32), uv_ref[...].astype(F32)
        gate, ug1, ug2 = _conv_tile(ug, pg_ref[...].astype(F32) * live, wg_ref, bg_ref)
        val, uv1, uv2 = _conv_tile(uv, pv_ref[...].astype(F32) * live, wv_ref, bv_ref)
        dav = da_ref[...].astype(F32)
        sig = jax.nn.sigmoid(gate)
        dgate = dav * val * (sig * (1.0 + gate * (1.0 - sig)))
        dval = dav * (gate * sig)
        dg_ref[...] = dgate.astype(dg_ref.dtype)
        dv_ref[...] = dval.astype(dv_ref.dtype)

        @pl.when(pl.program_id(1) == 0)
        def _():
            sg_ref[...] = jnp.zeros_like(sg_ref)
            sv_ref[...] = jnp.zeros_like(sv_ref)

        for s_ref, d, taps in ((sg_ref, dgate, (ug2, ug1, ug)), (sv_ref, dval, (uv2, uv1, uv))):
            for r, tap in enumerate(taps):
                s_ref[r:r + 1, :] += jnp.sum(d * tap, axis=0, keepdims=True)
            s_ref[3:4, :] += jnp.sum(d, axis=0, keepdims=True)

    sums = lambda off: pl.BlockSpec((8, tn), lambda j, i: (0, j + off))
    return pl.pallas_call(
        body, name=name, grid=(nj, s // tm),
        in_specs=[blk(0), blk(0), halo(0), blk(nj), halo(nj), wsp(0), wsp(nj), bsp(0), bsp(nj)],
        out_specs=(blk(0), blk(0), sums(0), sums(0)),
        out_shape=(jax.ShapeDtypeStruct((s, D_FF), BF16), jax.ShapeDtypeStruct((s, D_FF), BF16),
                   jax.ShapeDtypeStruct((8, D_FF), F32), jax.ShapeDtypeStruct((8, D_FF), F32)),
        compiler_params=_params(("parallel", "arbitrary")),
    )(da, u, u, u, u, conv_w, conv_w, conv_b, conv_b)


def _conv_transpose(d, conv_w, name, col0, into=None):
    s, w = d.shape
    tm, tn = _tile(s, CONV_T_TM), CONV_TN
    last = s // tm - 1
    jo = col0 // tn

    def body(d_ref, nx_ref, w_ref, *rest):
        o_ref = rest[-1]
        dv = d_ref[...].astype(F32)
        nxt = nx_ref[...].astype(F32) * (pl.program_id(1) < last).astype(F32)
        row = lax.broadcasted_iota(jnp.int32, dv.shape, 0)

        def shift_up(n):
            rolled = pltpu.roll(dv, tm - n, 0)
            nxt_rolled = pltpu.roll(nxt, HALO - n, 0)
            bottom = jnp.concatenate([rolled[:tm - HALO], nxt_rolled], axis=0)
            return jnp.where(row >= tm - n, bottom, rolled)

        out = w_ref[2:3, :] * dv + w_ref[1:2, :] * shift_up(1) + w_ref[0:1, :] * shift_up(2)
        o_ref[...] = out.astype(o_ref.dtype)

    blk = pl.BlockSpec((tm, tn), lambda j, i: (i, j))
    nxt_spec = pl.BlockSpec((HALO, tn), lambda j, i: (jnp.minimum((i + 1) * (tm // HALO), s // HALO - 1), j))
    in_specs = [blk, nxt_spec, pl.BlockSpec((3, tn), lambda j, i: (0, j + jo))]
    args = [d, d, conv_w]
    if into is not None:
        in_specs.append(pl.BlockSpec(memory_space=pl.ANY))
        args.append(into)
    return pl.pallas_call(
        body, name=name, grid=(w // tn, s // tm),
        in_specs=in_specs,
        out_specs=pl.BlockSpec((tm, tn), lambda j, i: (i, j + jo)),
        out_shape=jax.ShapeDtypeStruct((s, 2 * D_FF), BF16),
        input_output_aliases={3: 0} if into is not None else {},
        compiler_params=_params(("parallel", "arbitrary")),
    )(*args)


def _split3(a):
    a1 = a.astype(BF16)
    r1 = a - a1.astype(F32)
    a2 = r1.astype(BF16)
    a3 = (r1 - a2.astype(F32)).astype(BF16)
    return a1, a2, a3


def _ones_dot_right(a, mat):
    return sum(jnp.dot(p, mat, preferred_element_type=F32) for p in _split3(a))


def _ones_dot_left(mat, a):
    return sum(jnp.dot(mat, p, preferred_element_type=F32) for p in _split3(a))


def _tri(n, cmp):
    r = lax.broadcasted_iota(jnp.int32, (n, n), 0)
    c = lax.broadcasted_iota(jnp.int32, (n, n), 1)
    return cmp(r, c).astype(BF16)


def _forget_fwd(z, bias, name):
    nh, nr, nl = z.shape

    def body(z_ref, b_ref, f_ref):
        within = _tri(nl, lambda r, c: r <= c)
        before = _tri(nr, lambda r, c: c < r)
        for h in range(nh):
            x = z_ref[h] + b_ref[h]
            lf = jnp.minimum(x, 0.0) - jnp.log(1.0 + jnp.exp(-jnp.abs(x)))
            pre = _ones_dot_right(lf, within)
            tot = jnp.zeros((nr, nl), F32) + jnp.sum(lf, axis=1, keepdims=True)
            f_ref[h] = pre + _ones_dot_left(before, tot)

    return pl.pallas_call(
        body, name=name, out_shape=jax.ShapeDtypeStruct(z.shape, F32),
        compiler_params=pltpu.CompilerParams(vmem_limit_bytes=VMEM_LIMIT_BYTES),
    )(z, bias)


def _forget_bwd(df_rows, df_cols, z, bias, name):
    nh, nr, nl = z.shape

    def body(dfr_ref, dfc_ref, z_ref, b_ref, dz_ref, db_ref):
        within = _tri(nl, lambda r, c: r >= c)
        after = _tri(nr, lambda r, c: c > r)
        for h in range(nh):
            g = dfr_ref[h] + dfc_ref[h]
            suf = _ones_dot_right(g, within)
            tot = jnp.zeros((nr, nl), F32) + jnp.sum(g, axis=1, keepdims=True)
            dlf = suf + _ones_dot_left(after, tot)
            dz = dlf * jax.nn.sigmoid(-(z_ref[h] + b_ref[h]))
            dz_ref[h] = dz
            db_ref[h] = jnp.zeros((1, nl), F32) + jnp.sum(dz)

    return pl.pallas_call(
        body, name=name,
        out_shape=(jax.ShapeDtypeStruct(z.shape, F32), jax.ShapeDtypeStruct(bias.shape, F32)),
        compiler_params=pltpu.CompilerParams(vmem_limit_bytes=VMEM_LIMIT_BYTES),
    )(df_rows, df_cols, z, bias)


def _ada_fwd(c_col, w, b, name):
    kdim, n = w.shape

    def body(c_ref, w_ref, b_ref, ada_ref, act_ref):
        wv = w_ref[...]
        for e in range(N_DEV):
            cv = c_ref[e]
            act = cv * jax.nn.sigmoid(cv)
            act_ref[e] = act
            ada_ref[e:e + 1, :] = jnp.sum(act * wv, axis=0, keepdims=True) + b_ref[...]

    return pl.pallas_call(
        body, name=name,
        out_shape=(jax.ShapeDtypeStruct((N_DEV, n), F32), jax.ShapeDtypeStruct((N_DEV, kdim, 1), F32)),
        compiler_params=pltpu.CompilerParams(vmem_limit_bytes=VMEM_LIMIT_BYTES),
    )(c_col, w, b)


def _ada_bwd(act_col, dada, name):
    kdim = act_col.shape[1]
    n = dada.shape[1]

    def body(act_ref, d_ref, g_ref):
        acc = act_ref[0] * d_ref[0:1, :]
        for e in range(1, N_DEV):
            acc = acc + act_ref[e] * d_ref[e:e + 1, :]
        g_ref[...] = acc

    return pl.pallas_call(
        body, name=name, out_shape=jax.ShapeDtypeStruct((kdim, n), F32),
        compiler_params=pltpu.CompilerParams(vmem_limit_bytes=VMEM_LIMIT_BYTES),
    )(act_col, dada)


def _adamw(parts, w, m, v, name, tr=128):
    npart, r, c = parts.shape
    tr = _tile(r, tr, step=BF16_ROWS) if r % BF16_ROWS == 0 else r

    def body(p_ref, w_ref, m_ref, v_ref, g_ref, d_ref, nm_ref, nv_ref):
        g = p_ref[0].astype(F32)
        for e in range(1, npart):
            g = g + p_ref[e].astype(F32)
        nm = ADAM_B1 * m_ref[...] + (1.0 - ADAM_B1) * g
        nv = ADAM_B2 * v_ref[...] + (1.0 - ADAM_B2) * (g * g)
        m_hat = nm / (1.0 - ADAM_B1 ** ADAM_STEP)
        v_hat = nv / (1.0 - ADAM_B2 ** ADAM_STEP)
        g_ref[...] = g
        d_ref[...] = -ADAM_LR * (m_hat / (jnp.sqrt(v_hat) + ADAM_EPS) + ADAM_WD * w_ref[...])
        nm_ref[...] = nm
        nv_ref[...] = nv

    row = pl.BlockSpec((tr, c), lambda i: (i, 0))
    out = jax.ShapeDtypeStruct((r, c), F32)
    return pl.pallas_call(
        body, name=name, grid=(r // tr,),
        in_specs=[pl.BlockSpec((npart, tr, c), lambda i: (0, i, 0)), row, row, row],
        out_specs=(row, row, row, row), out_shape=(out, out, out, out),
        compiler_params=_params(("parallel",)),
    )(parts, w, m, v)


EARLY = ("w_in", "w_uq", "w_ukv")
LATE = ("w_o_mla", "w_o_fox", "w_out", "w_up", "conv_w", "w_down")
BIG = EARLY + LATE


def _cols_to_full(stack):
    n, r, c = stack.shape
    return stack.transpose(1, 0, 2).reshape(r, n * c)


def _full_to_cols(full, c):
    r = full.shape[0]
    return full.reshape(r, N_DEV, c).transpose(1, 0, 2)


def _pad_heads(a, width):
    r = a.shape[0]
    a = a.reshape(r, N_HEADS, width)
    return jnp.pad(a, ((0, 0), (0, 0), (0, HEAD_PAD - width))).reshape(r, N_HEADS * HEAD_PAD)


def _unpad_heads(a, width):
    s = a.shape[0]
    return a.reshape(s, N_HEADS, HEAD_PAD)[:, :, :width].reshape(s, N_HEADS * width)


def _w_in_padded(w_in):
    seg = [w_in[:, IN_OFF[i]:IN_OFF[i + 1]] for i in range(9)]
    cq, ckv, kr, fq, fk, fv, fl, gm, gf = seg
    padc = lambda a, n: jnp.pad(a, ((0, 0), (0, n - a.shape[1])))
    return jnp.concatenate([gm, gf, fq, fk, fv, cq, ckv, padc(kr, 128), padc(fl, 128)], axis=1)


def _w_in_unpadded(g):
    return jnp.concatenate([
        g[:, P_CQ:P_CQ + 384], g[:, P_CKV:P_CKV + 256], g[:, P_KR:P_KR + 32], g[:, P_FQ:P_FQ + 512],
        g[:, P_FK:P_FK + 512], g[:, P_FV:P_FV + 512], g[:, P_FL:P_FL + 8], g[:, P_GM:P_GM + 1024],
        g[:, P_GF:P_GF + 1024]], axis=1)


SMALL = (("b_ada", 6144, 6144), ("norm_mix_g", 1024, 1024), ("q_norm_g", 384, 384), ("kv_norm_g", 256, 256),
         ("b_forget", 8, 128), ("norm_ffn_g", 1024, 1024), ("conv_b", 5632, 5632), ("norm_final_g", 1024, 1024),
         ("loss", 1, 128))
SMALL_OFF = {}
_o = 0
for _n, _real, _padded in SMALL:
    SMALL_OFF[_n] = _o
    _o += _padded
SMALL_W = _o


def _pack_small(vals):
    parts = []
    for nme, real, padded in SMALL:
        a = vals[nme].reshape(1, real).astype(F32)
        parts.append(jnp.pad(a, ((0, 0), (0, padded - real))))
    return jnp.concatenate(parts, axis=1)


def kernel(x, c, positions, w_ada, b_ada, norm_mix_g, w_in, q_norm_g, w_uq, kv_norm_g, w_ukv, b_forget, w_o_mla, w_o_fox, w_out, norm_ffn_g, w_up, conv_w, conv_b, w_down, norm_final_g, loss_target, m_w_ada, m_b_ada, m_norm_mix_g, m_w_in, m_q_norm_g, m_w_uq, m_kv_norm_g, m_w_ukv, m_b_forget, m_w_o_mla, m_w_o_fox, m_w_out, m_norm_ffn_g, m_w_up, m_conv_w, m_conv_b, m_w_down, m_norm_final_g, v_w_ada, v_b_ada, v_norm_mix_g, v_w_in, v_q_norm_g, v_w_uq, v_kv_norm_g, v_w_ukv, v_b_forget, v_w_o_mla, v_w_o_fox, v_w_out, v_norm_ffn_g, v_w_up, v_conv_w, v_conv_b, v_w_down, v_norm_final_g):
    me = 4 * lax.axis_index("x") + 2 * lax.axis_index("y") + lax.axis_index("c")
    x = x[0]
    target = loss_target[0]
    s = x.shape[0]
    nblk = s // ATT_T
    big_w = {"w_in": w_in, "w_uq": w_uq, "w_ukv": w_ukv, "w_o_mla": w_o_mla, "w_o_fox": w_o_fox,
             "w_out": w_out, "w_up": w_up, "conv_w": conv_w, "w_down": w_down}
    big_m = {"w_in": m_w_in, "w_uq": m_w_uq, "w_ukv": m_w_ukv, "w_o_mla": m_w_o_mla, "w_o_fox": m_w_o_fox,
             "w_out": m_w_out, "w_up": m_w_up, "conv_w": m_conv_w, "w_down": m_w_down}
    big_v = {"w_in": v_w_in, "w_uq": v_w_uq, "w_ukv": v_w_ukv, "w_o_mla": v_w_o_mla, "w_o_fox": v_w_o_fox,
             "w_out": v_w_out, "w_up": v_w_up, "conv_w": v_conv_w, "w_down": v_w_down}

    shard = lambda k: big_w[k][0] if k == "conv_w" else big_w[k][0].astype(BF16)
    st = dict(zip(EARLY, _all_gather([shard(k) for k in EARLY], "gather_weights")))
    w_in_p = _w_in_padded(_cols_to_full(st["w_in"]))
    uq = st["w_uq"]
    w_uq_p = jnp.pad(uq, ((0, 0), (0, 0), (0, HEAD_PAD - 96))).transpose(1, 0, 2).reshape(MLA_Q_RANK, 1024)
    ukv = st["w_ukv"]
    zeros64 = jnp.zeros((N_HEADS, MLA_KV_RANK, 64), BF16)
    w_uk_p = jnp.concatenate([ukv[:, :, :64], zeros64], axis=2).transpose(1, 0, 2).reshape(MLA_KV_RANK, 1024)
    w_uv_p = jnp.concatenate([ukv[:, :, 64:], zeros64], axis=2).transpose(1, 0, 2).reshape(MLA_KV_RANK, 1024)
    place = np.zeros((HEAD_PAD, N_HEADS, HEAD_PAD), np.float32)
    for j in range(MLA_ROPE):
        place[j, :, MLA_NOPE + j] = 1.0
    place = jnp.asarray(place.reshape(HEAD_PAD, 1024), BF16)
    w_kv_comb = jnp.concatenate([
        jnp.concatenate([w_uk_p, w_uv_p], axis=1),
        jnp.concatenate([place, jnp.zeros((HEAD_PAD, 1024), BF16)], axis=1)], axis=0)

    (c_all,) = _all_gather([c], "gather_c")
    b_ada_mine = lax.dynamic_slice(b_ada, (0, me * 768), (1, 768))
    ada_cols, act_col = _ada_fwd(c_all.reshape(N_DEV, D_MODEL, 1), w_ada[0], b_ada_mine, "ada_fwd")
    (ada_all,) = _all_gather([ada_cols], "gather_ada")
    ada = lax.dynamic_slice(ada_all, (0, me, 0), (N_DEV, 1, 768)).reshape(1, N_ADA * D_MODEL)
    sh_m, sc_m, g_m, sh_f, sc_f, g_f = [ada[:, i * D_MODEL:(i + 1) * D_MODEL] for i in range(N_ADA)]

    inv_freq = ROPE_THETA ** (-jnp.arange(0, MLA_ROPE, 2, dtype=F32) / MLA_ROPE)
    ang = positions[0].astype(F32)[:, None] * inv_freq
    cos, sin = jnp.cos(ang), jnp.sin(ang)
    rope_c = jnp.concatenate([jnp.ones((s, 64), F32), cos, cos, jnp.zeros((s, 32), F32)], axis=1)
    rope_s = jnp.concatenate([jnp.zeros((s, 64), F32), -sin, sin, jnp.zeros((s, 32), F32)], axis=1)

    h1 = _rms_mod(x, norm_mix_g, sc_m, sh_m, "norm_mix")
    gates = _mm(h1, w_in_p[:, P_GM:P_FQ], "nn", F32, "proj_gates", tn=1024)
    fq = _mm(h1, _pad_heads(w_in_p[:, P_FQ:P_FK], 64), "nn", BF16, "proj_fq", tn=1024,
             gvec=jnp.full((1, 1024), FOX_SCALE * LOG2E, F32))
    fk = _mm(h1, _pad_heads(w_in_p[:, P_FK:P_FV], 64), "nn", BF16, "proj_fk", tn=1024)
    fv = _mm(h1, _pad_heads(w_in_p[:, P_FV:P_CQ], 64), "nn", BF16, "proj_fv", tn=1024, ones_lane=True)
    lat =_mm(h1, w_in_p[:, P_CQ:], "nn", F32, "proj_latent", tn=D_IN_P - P_CQ)
    cq = lat[:, 0:384]
    ckv = lat[:, P_CKV - P_CQ:P_CKV - P_CQ + 256]
    qn, kv_in = _latent_norm(lat, q_norm_g, kv_norm_g, "latent_norm")
    q_fold = MLA_SCALE * LOG2E
    q_att = _mm(qn, w_uq_p, "nn", BF16, "q_up", rope=(rope_c * q_fold, rope_s * q_fold))
    k_att = _mm(kv_in, w_kv_comb[:, :1024], "nn", BF16, "k_up", rope=(rope_c, rope_s))
    v_att = _mm(kv_in, w_kv_comb[:, 1024:], "nn", BF16, "v_up", ones_lane=True)
    o_mla, o_mla_b, lse_mla, *late = _attn_fwd(q_att, k_att, v_att, None, "mla_fwd",
                                               gather=[shard(k) for k in LATE])
    st.update(zip(LATE, late))
    pad_o = lambda full: jnp.pad(full.reshape(N_HEADS, 64, 1024), ((0, 0), (0, 64), (0, 0))).reshape(1024, 1024)
    w_o_mla_p = pad_o(_cols_to_full(st["w_o_mla"]))
    w_o_fox_p = pad_o(_cols_to_full(st["w_o_fox"]))
    w_out_f = st["w_out"].reshape(1024, 1024)
    w_up_f = _cols_to_full(st["w_up"])
    conv_w_f = _cols_to_full(st["conv_w"])
    w_down_f = st["w_down"].reshape(D_FF, 1024)

    z = lat[:, P_FL - P_CQ:P_FL - P_CQ + 8].T.reshape(N_HEADS, s // SEQ_LANES, SEQ_LANES)
    bias_f = jnp.broadcast_to(b_forget.reshape(N_HEADS, 1, 1), (N_HEADS, 1, SEQ_LANES))
    f_cum = _forget_fwd(z, bias_f, "forget_fwd")
    f_row = (f_cum * LOG2E).reshape(N_HEADS, nblk, 1, ATT_T)
    o_fox, o_fox_b, lse_fox = _attn_fwd(fq, fk, fv, f_row, "fox_fwd")

    pm = _mm(o_mla_b, w_o_mla_p, "nn", BF16, "o_mla_proj")
    pf = _mm(o_fox_b, w_o_fox_p, "nn", BF16, "o_fox_proj")
    y = _gate_fwd(pm, pf, gates, "gate_fwd")
    x2, mix = _mm(y, w_out_f, "nn", F32, "out_proj", res=x, gvec=g_m)

    h2 = _rms_mod(x2, norm_ffn_g, sc_f, sh_f, "norm_ffn")
    u = _mm(h2, w_up_f, "nn", BF16, "ffn_up", tn=D_FF // 2)
    a = _convglu_fwd(u, conv_w_f, conv_b, "convglu_fwd")
    x3, ffn = _mm(a, w_down_f, "nn", F32, "ffn_down", res=x2, gvec=g_f, tk=2816)

    dx3, dffn, sums_final = _final_loss(x3, target, norm_final_g.reshape(1, D_MODEL), ffn, g_f, "final_loss")
    da = _mm(dffn, w_down_f, "nt", BF16, "ffn_down_dx", tn=1408)
    g_w_down = _mm(a, dffn, "tn", F32, "ffn_down_dw", tm=256, tn=1024, tk=s)
    dgate, dval, s_gate, s_val = _convglu_bwd(da, u, conv_w_f, conv_b, "convglu_bwd")
    du = _conv_transpose(dgate, conv_w_f, "conv_t_gate", 0)
    du = _conv_transpose(dval, conv_w_f, "conv_t_val", D_FF, into=du)
    dh2 = _mm(du, w_up_f, "nt", F32, "ffn_up_dx", tn=512, tk=2 * D_FF)
    g_w_up = _mm(h2, du, "tn", F32, "ffn_up_dw", tn=256, tk=s)
    dx2, dmix, sums_ffn = _rms_mod_bwd(dh2, x2, norm_ffn_g, sc_f, dx3, "norm_ffn_bwd", branch=(mix, g_m))

    dy = _mm(dmix, w_out_f, "nt", F32, "out_proj_dx")
    g_w_out = _mm(y, dmix, "tn", F32, "out_proj_dw", tn=256, tk=s)
    dpm, dpf, dgates = _gate_bwd(dy, pm, pf, gates, "gate_bwd")
    do_mla_b = _mm(dpm, w_o_mla_p, "nt", BF16, "o_mla_dx", tn=1024)
    do_fox_b = _mm(dpf, w_o_fox_p, "nt", BF16, "o_fox_dx", tn=1024)
    g_w_o_mla_p = _mm(o_mla_b, dpm, "tn", F32, "o_mla_dw", tn=256, tk=s)
    g_w_o_fox_p = _mm(o_fox_b, dpf, "tn", F32, "o_fox_dw", tn=256, tk=s)

    unpad_o = lambda g: g.reshape(N_HEADS, HEAD_PAD, 1024)[:, :64].reshape(512, 1024)
    g_conv_w = jnp.concatenate([s_gate[0:3], s_val[0:3]], axis=1)
    g_blocks = {
        "w_o_mla": _full_to_cols(unpad_o(g_w_o_mla_p), 128), "w_o_fox": _full_to_cols(unpad_o(g_w_o_fox_p), 128),
        "w_out": g_w_out.reshape(N_DEV, 128, 1024), "w_up": _full_to_cols(g_w_up, 704),
        "conv_w": _full_to_cols(g_conv_w, 704), "w_down": g_w_down.reshape(N_DEV, 352, 1024)}

    delta_mla = _attn_delta(o_mla, do_mla_b, "mla_delta")
    dq_rot, dk_rot, dv_mla, *late_recv = _attn_bwd(
        q_att, k_att, v_att, do_mla_b, lse_mla, delta_mla, None, MLA_SCALE, 1.0 / LOG2E, "mla_bwd", BF16,
        scatter=[g_blocks[k].astype(BF16) for k in LATE])
    dq_pre = _rope(dq_rot, rope_c, -rope_s, "rope_q_bwd")
    dkv_pre = _rope_bwd_kv(dk_rot, dv_mla, rope_c, -rope_s, "rope_kv_bwd")
    dqn = _mm(dq_pre, w_uq_p, "nt", F32, "q_up_dx")
    g_w_uq_p = _mm(qn, dq_pre, "tn", F32, "q_up_dw", tk=s)
    dkv_in = _mm(dkv_pre, w_kv_comb, "nt", F32, "kv_up_dx")
    g_w_kv_comb = _mm(kv_in, dkv_pre, "tn", F32, "kv_up_dw", tk=s)
    dcq, sums_q = _rms_mod_bwd(dqn, cq, q_norm_g, jnp.zeros((1, 384), F32), None, "q_norm_bwd")
    dckv, sums_kv = _rms_mod_bwd(dkv_in[:, :256], ckv, kv_norm_g, jnp.zeros((1, 256), F32), None, "kv_norm_bwd")
    delta_fox = _attn_delta(o_fox, do_fox_b, "fox_delta")
    dfq, dfk, dfv, dfr, dfc = _attn_bwd(fq, fk, fv, do_fox_b, lse_fox, delta_fox, f_row,
                                        FOX_SCALE, 1.0 / LOG2E, "fox_bwd", BF16)
    df_rows = dfr.reshape(N_HEADS, s // SEQ_LANES, SEQ_LANES)
    df_cols = dfc.reshape(N_HEADS, s // SEQ_LANES, SEQ_LANES)
    dz, db_f = _forget_bwd(df_rows, df_cols, z, bias_f, "forget_bwd")
    dfl = jnp.pad(dz.reshape(N_HEADS, s).T, ((0, 0), (0, 128 - N_HEADS)))

    dproj = jnp.concatenate([
        dgates, _unpad_heads(dfq, 64), _unpad_heads(dfk, 64), _unpad_heads(dfv, 64),
        dcq.astype(BF16), dckv.astype(BF16), dkv_in[:, 256:384].astype(BF16), dfl.astype(BF16)], axis=1)
    g_w_in_p = _mm(h1, dproj, "tn", F32, "proj_in_dw", tm=512, tn=640, tk=s)

    g_w_in = _w_in_unpadded(g_w_in_p)
    g_uq = g_w_uq_p.reshape(MLA_Q_RANK, N_HEADS, HEAD_PAD)[:, :, :96].transpose(1, 0, 2)
    g_uk = g_w_kv_comb[:256, :1024].reshape(256, N_HEADS, HEAD_PAD)[:, :, :64]
    g_uv = g_w_kv_comb[:256, 1024:].reshape(256, N_HEADS, HEAD_PAD)[:, :, :64]
    g_ukv = jnp.concatenate([g_uk, g_uv], axis=2).transpose(1, 0, 2)
    g_blocks.update({"w_in": _full_to_cols(g_w_in, 533), "w_uq": g_uq, "w_ukv": g_ukv})
    dh1, *early_recv = _mm(dproj, w_in_p, "nt", F32, "proj_in_dx", tn=512, tk=D_IN_P,
                           scatter=[g_blocks[k].astype(BF16) for k in EARLY])
    grad_x, sums_mix = _rms_mod_bwd(dh1, x, norm_mix_g, sc_m, dx2, "norm_mix_bwd")
    g_big, d_big, nm_big, nv_big = {}, {}, {}, {}
    for k, parts in zip(BIG, list(early_recv) + list(late_recv)):
        g_big[k], d_big[k], nm_big[k], nv_big[k] = [
            t[None] for t in _adamw(parts, big_w[k][0], big_m[k][0], big_v[k][0], "adamw_" + k)]

    dada = jnp.concatenate([sums_mix[0:1], sums_mix[1:2], sums_ffn[3:4], sums_ffn[0:1], sums_ffn[1:2], sums_final[2:3]],
                           axis=1)
    small_part = _pack_small({
        "b_ada": dada, "norm_mix_g": sums_mix[2:3], "q_norm_g": sums_q[2:3], "kv_norm_g": sums_kv[2:3],
        "b_forget": db_f[:, 0, 0], "norm_ffn_g": sums_ffn[2:3],
        "conv_b": jnp.concatenate([s_gate[3:4], s_val[3:4]], axis=1), "norm_final_g": sums_final[0:1],
        "loss": sums_final[1:2, 0:1]})
    (small_all,) = _all_gather([small_part], "gather_small")
    zero1 = jnp.zeros((1,), F32)
    small_w = {"b_ada": b_ada, "norm_mix_g": norm_mix_g, "q_norm_g": q_norm_g, "kv_norm_g": kv_norm_g,
               "b_forget": b_forget, "norm_ffn_g": norm_ffn_g, "conv_b": conv_b, "norm_final_g": norm_final_g,
               "loss": zero1}
    small_m = {"b_ada": m_b_ada, "norm_mix_g": m_norm_mix_g, "q_norm_g": m_q_norm_g, "kv_norm_g": m_kv_norm_g,
               "b_forget": m_b_forget, "norm_ffn_g": m_norm_ffn_g, "conv_b": m_conv_b,
               "norm_final_g": m_norm_final_g, "loss": zero1}
    small_v = {"b_ada": v_b_ada, "norm_mix_g": v_norm_mix_g, "q_norm_g": v_q_norm_g, "kv_norm_g": v_kv_norm_g,
               "b_forget": v_b_forget, "norm_ffn_g": v_norm_ffn_g, "conv_b": v_conv_b,
               "norm_final_g": v_norm_final_g, "loss": zero1}
    g_sm, d_sm, nm_sm, nv_sm = _adamw(small_all, _pack_small(small_w), _pack_small(small_m), _pack_small(small_v),
                                      "adamw_small")
    loss = g_sm[0, SMALL_OFF["loss"]]

    dada_all = small_all[:, 0, SMALL_OFF["b_ada"]:SMALL_OFF["b_ada"] + N_ADA * D_MODEL]
    dada_mine = lax.dynamic_slice(dada_all, (0, me * 768), (N_DEV, 768))
    g_ada_local = _ada_bwd(act_col, dada_mine, "ada_bwd")
    g_ada, d_ada, nm_ada, nv_ada = _adamw(g_ada_local[None], w_ada[0], m_w_ada[0], v_w_ada[0], "adamw_ada")

    def small_out(t, nme, shape):
        real = dict((n_, r_) for n_, r_, _ in SMALL)[nme]
        o = SMALL_OFF[nme]
        return t[0, o:o + real].reshape(shape)

    order = ["w_ada", "b_ada", "norm_mix_g", "w_in", "q_norm_g", "w_uq", "kv_norm_g", "w_ukv", "b_forget",
             "w_o_mla", "w_o_fox", "w_out", "norm_ffn_g", "w_up", "conv_w", "conv_b", "w_down", "norm_final_g"]
    small_shapes = {"b_ada": (1, 6144), "norm_mix_g": (1, 1024), "q_norm_g": (1, 384), "kv_norm_g": (1, 256),
                    "b_forget": (1, 8), "norm_ffn_g": (1, 1024), "conv_b": (1, 5632), "norm_final_g": (1024,)}

    def family(big, small, ada_t):
        out = []
        for nme in order:
            if nme == "w_ada":
                out.append(ada_t[None])
            elif nme in small_shapes:
                out.append(small_out(small, nme, small_shapes[nme]))
            else:
                out.append(big[nme])
        return out

    return (loss, grad_x[None], *family(g_big, g_sm, g_ada), *family(d_big, d_sm, d_ada),
            *family(nm_big, nm_sm, nm_ada), *family(nv_big, nv_sm, nv_ada))
```

```python
import math

import numpy as np
import jax
import jax.numpy as jnp
from jax import lax
from jax.experimental import pallas as pl
from jax.experimental.pallas import tpu as pltpu

F32 = jnp.float32
BF16 = jnp.bfloat16

N_DEV = 8
D_MODEL = 1024
N_HEADS = 8
HEAD_PAD = 128
MLA_Q_RANK = 384
MLA_KV_RANK = 256
MLA_NOPE = 64
MLA_ROPE = 32
MLA_V = 64
FOX_DIM = 64
D_FF = 2816
N_ADA = 6
EPS = 1e-6
ROPE_THETA = 10000.0
MLA_SCALE = 1.0 / math.sqrt(MLA_NOPE + MLA_ROPE)
FOX_SCALE = 1.0 / math.sqrt(FOX_DIM)
IN_SPLITS = (384, 256, 32, 512, 512, 512, 8, 1024, 1024)
D_IN = sum(IN_SPLITS)
IN_OFF = tuple(int(v) for v in np.cumsum((0,) + IN_SPLITS))
P_GM, P_GF, P_FQ, P_FK, P_FV, P_CQ, P_CKV, P_KR, P_FL, D_IN_P = 0, 1024, 2048, 2560, 3072, 3584, 3968, 4224, 4352, 4480

ADAM_LR, ADAM_B1, ADAM_B2, ADAM_EPS, ADAM_WD, ADAM_STEP = 0.001, 0.9, 0.999, 1e-08, 0.01, 10

VMEM_LIMIT_BYTES = 56 * 1024 * 1024
NEG_BIG = -1e30
ATT_T = 512
LOG2E = 1.4426950408889634
SUM_LANE = 64
ROW_T = 512
SEQ_LANES = 128
BF16_ROWS = 16
FWD_HEADS_PER_STEP = 4
BWD_HEADS_PER_STEP = 2


def _params(sem):
    return pltpu.CompilerParams(dimension_semantics=sem, vmem_limit_bytes=VMEM_LIMIT_BYTES)


def _tile(n, target, step=128):
    if n <= target:
        return n
    t = (target // step) * step
    while t >= step:
        if n % t == 0:
            return t
        t -= step
    return n


def _vec_spec(w, nargs):
    if nargs == 1:
        return pl.BlockSpec((1, w), lambda i: (0, 0))
    return pl.BlockSpec((1, w), lambda i, j: (0, 0))


def _comm_call(body, name, ins, out_shapes):
    n = len(ins)
    any_spec = pl.BlockSpec(memory_space=pl.ANY)
    return pl.pallas_call(
        body, name=name, out_shape=tuple(out_shapes),
        in_specs=[any_spec] * n, out_specs=tuple([any_spec] * n),
        scratch_shapes=[pltpu.SemaphoreType.DMA((n, 7)), pltpu.SemaphoreType.DMA((n, 7)),
                        pltpu.SemaphoreType.DMA((n,))],
    )(*ins)


def _all_gather(xs, name):
    n = len(xs)

    def body(*refs):
        x_refs, out_refs = refs[:n], refs[n:2 * n]
        send_sems, recv_sems, local_sems = refs[2 * n:]
        x_, y_, c_ = lax.axis_index("x"), lax.axis_index("y"), lax.axis_index("c")
        me, sibling = (x_, y_, c_), (x_, y_, 1 - c_)
        chips = [(1 - x_, y_), (x_, 1 - y_), (1 - x_, 1 - y_)]

        def slot(a, px, py, pc):
            return out_refs[a].at[4 * px + 2 * py + pc]

        def copy(a, k, block, to, src=None):
            return pltpu.make_async_remote_copy(
                src_ref=slot(a, *block) if src is None else src, dst_ref=slot(a, *block),
                send_sem=send_sems.at[a, k], recv_sem=recv_sems.at[a, k],
                device_id=to, device_id_type=pl.DeviceIdType.MESH)

        mine = [pltpu.make_async_copy(x_refs[a], slot(a, *me), local_sems.at[a]) for a in range(n)]
        for cp in mine:
            cp.start()
        first = []
        for a in range(n):
            first.append(copy(a, 0, me, sibling, src=x_refs[a]))
            first += [copy(a, 1 + j, me, (*chip, c_), src=x_refs[a]) for j, chip in enumerate(chips)]
        for cp in first:
            cp.start()
        passed = []
        for j, chip in enumerate(chips):
            for a in range(n):
                copy(a, 1 + j, (*chip, c_), me).wait_recv()
                passed.append(copy(a, 4 + j, (*chip, c_), sibling))
                passed[-1].start()
        for a in range(n):
            copy(a, 0, sibling, me).wait_recv()
            for j, chip in enumerate(chips):
                copy(a, 4 + j, (*chip, 1 - c_), me).wait_recv()
        for cp in first + passed:
            cp.wait_send()
        for cp in mine:
            cp.wait()

    return _comm_call(body, name, xs, [jax.ShapeDtypeStruct((N_DEV,) + x.shape, x.dtype) for x in xs])


def _direct_exchange(src_refs, out_refs, send_sems, recv_sems, local_sems, scatter):
    n = len(src_refs)
    x_, y_, c_ = lax.axis_index("x"), lax.axis_index("y"), lax.axis_index("c")
    me = 4 * x_ + 2 * y_ + c_

    def peer(k):
        return (x_ ^ ((k >> 2) & 1), y_ ^ ((k >> 1) & 1), c_ ^ (k & 1))

    def src(a, slot):
        return src_refs[a].at[slot] if scatter else src_refs[a]

    def copy(a, k, sending):
        px, py, pc = peer(k)
        theirs = 4 * px + 2 * py + pc
        return pltpu.make_async_remote_copy(
            src_ref=src(a, theirs if sending else me), dst_ref=out_refs[a].at[me if sending else theirs],
            send_sem=send_sems.at[a, k - 1], recv_sem=recv_sems.at[a, k - 1],
            device_id=(px, py, pc), device_id_type=pl.DeviceIdType.MESH)

    mine = [pltpu.make_async_copy(src(a, me), out_refs[a].at[me], local_sems.at[a]) for a in range(n)]
    sends = [copy(a, k, True) for a in range(n) for k in range(1, N_DEV)]

    def start():
        for cp in mine + sends:
            cp.start()

    def wait():
        for a in range(n):
            for k in range(1, N_DEV):
                copy(a, k, False).wait_recv()
        for cp in sends:
            cp.wait_send()
        for cp in mine:
            cp.wait()

    return start, wait


def _exchange_scratch(n):
    return [pltpu.SemaphoreType.DMA((n, 7)), pltpu.SemaphoreType.DMA((n, 7)), pltpu.SemaphoreType.DMA((n,))]


def _mm(a, b, mode, out_dtype, name, res=None, gvec=None, tm=1024, tn=512, tk=1024, rope=None, ones_lane=False,
        scatter=()):
    (k, m) = a.shape if mode == "tn" else a.shape[::-1]
    n = b.shape[0] if mode == "nt" else b.shape[1]
    tm, tn, tk = _tile(m, tm), _tile(n, tn), _tile(k, tk)
    nk = k // tk
    dims = {"nn": (((1,), (0,)), ((), ())), "nt": (((1,), (1,)), ((), ())), "tn": (((0,), (0,)), ((), ()))}[mode]
    has_res, has_g = res is not None, gvec is not None
    fused = has_res and has_g

    n_rope = 2 if rope is not None else 0
    nx = len(scatter)
    assert not nx or nk == 1, "the exchange rides only on a matmul with one K step"
    n_in = 2 + has_res + has_g + n_rope

    def body(*refs):
        acc_ref = refs[-1] if nk > 1 else None
        if nx:
            ins, x_src, outs, x_out, _, x_sems = _split_refs(refs, n_in, 1 + fused, 0, nx)
            refs = list(ins) + list(outs)
            i, j = pl.program_id(0), pl.program_id(1)
            x_start, x_wait = _direct_exchange(x_src, x_out, *x_sems, scatter=True)
            pl.when(jnp.logical_and(i == 0, j == 0))(x_start)
        else:
            refs = list(refs[:n_in + 1 + fused])
        a_ref, b_ref = refs[:2]
        res_ref = refs[2] if has_res else None
        g_ref = refs[2 + has_res] if has_g else None
        o_ref = refs[n_in]
        part = lax.dot_general(a_ref[...], b_ref[...], dims, preferred_element_type=F32)

        def finish(acc):
            if fused:
                refs[-1][...] = acc
            out = g_ref[...] * acc if has_g else acc
            if has_res:
                out = res_ref[...] + out
            if n_rope or ones_lane:
                one = (lax.broadcasted_iota(jnp.int32, (tm, HEAD_PAD), 1) == SUM_LANE).astype(F32)
                for hb in range(tn // HEAD_PAD):
                    lanes = slice(hb * HEAD_PAD, (hb + 1) * HEAD_PAD)
                    seg = out[:, lanes]
                    seg = _rope_block(seg, refs[n_in - 2][...], refs[n_in - 1][...]) if n_rope else seg + one
                    o_ref[:, lanes] = seg.astype(o_ref.dtype)
            else:
                o_ref[...] = out.astype(o_ref.dtype)

        if nk == 1:
            finish(part)
            if nx:
                pl.when(jnp.logical_and(i == m // tm - 1, j == n // tn - 1))(x_wait)
            return
        kk = pl.program_id(2)

        @pl.when(kk == 0)
        def _():
            acc_ref[...] = part

        @pl.when(kk > 0)
        def _():
            acc_ref[...] += part

        @pl.when(kk == nk - 1)
        def _():
            finish(acc_ref[...])

    if mode == "tn":
        a_spec = pl.BlockSpec((tk, tm), lambda i, j, kk: (kk, i))
    else:
        a_spec = pl.BlockSpec((tm, tk), lambda i, j, kk: (i, kk))
    if mode == "nt":
        b_spec = pl.BlockSpec((tn, tk), lambda i, j, kk: (j, kk))
    else:
        b_spec = pl.BlockSpec((tk, tn), lambda i, j, kk: (kk, j))
    o_spec = pl.BlockSpec((tm, tn), lambda i, j, kk: (i, j))
    in_specs, args = [a_spec, b_spec], [a, b]
    out_specs, out_shape = o_spec, jax.ShapeDtypeStruct((m, n), out_dtype)
    if has_res:
        in_specs.append(o_spec)
        args.append(res)
    if has_g:
        in_specs.append(pl.BlockSpec((1, tn), lambda i, j, kk: (0, j)))
        args.append(gvec)
    if n_rope:
        in_specs += [pl.BlockSpec((tm, HEAD_PAD), lambda i, j, kk: (i, 0))] * 2
        args += list(rope)
    if fused:
        out_specs = (o_spec, o_spec)
        out_shape = (out_shape, jax.ShapeDtypeStruct((m, n), F32))
    scratch = [pltpu.VMEM((tm, tn), F32)] if nk > 1 else []
    if nx:
        any_spec = pl.BlockSpec(memory_space=pl.ANY)
        in_specs += [any_spec] * nx
        args += list(scatter)
        out_specs = tuple(out_specs if fused else (out_specs,)) + (any_spec,) * nx
        out_shape = tuple(out_shape if fused else (out_shape,)) + tuple(
            jax.ShapeDtypeStruct(g.shape, g.dtype) for g in scatter)
        scratch += _exchange_scratch(nx)
    return pl.pallas_call(
        body, name=name, grid=(m // tm, n // tn, nk),
        in_specs=in_specs, out_specs=out_specs, out_shape=out_shape,
        scratch_shapes=scratch,
        compiler_params=_params(("arbitrary",) * 3 if nx else ("parallel", "parallel", "arbitrary")),
    )(*args)


def _rms_mod(x, g, sc, sh, name):
    s, w = x.shape
    tm = _tile(s, ROW_T)

    def body(x_ref, g_ref, sc_ref, sh_ref, o_ref):
        xv = x_ref[...]
        r = lax.rsqrt(jnp.mean(xv * xv, axis=-1, keepdims=True) + EPS)
        o_ref[...] = ((xv * r * g_ref[...]) * (1.0 + sc_ref[...]) + sh_ref[...]).astype(o_ref.dtype)

    row = pl.BlockSpec((tm, w), lambda i: (i, 0))
    return pl.pallas_call(
        body, name=name, grid=(s // tm,),
        in_specs=[row, _vec_spec(w, 1), _vec_spec(w, 1), _vec_spec(w, 1)],
        out_specs=row, out_shape=jax.ShapeDtypeStruct((s, w), BF16),
        compiler_params=_params(("parallel",)),
    )(x, g, sc, sh)


def _latent_norm(lat, q_g, kv_g, name):
    s, w = lat.shape
    tm = _tile(s, ROW_T)
    nq_, nkv = MLA_Q_RANK, MLA_KV_RANK

    def norm(xv, gv):
        return xv * lax.rsqrt(jnp.mean(xv * xv, axis=-1, keepdims=True) + EPS) * gv

    def body(lat_ref, qg_ref, kg_ref, qn_ref, kv_ref):
        qn_ref[...] = norm(lat_ref[:, 0:nq_], qg_ref[...]).astype(BF16)
        kv_ref[:, 0:nkv] = norm(lat_ref[:, nq_:nq_ + nkv], kg_ref[...]).astype(BF16)
        kv_ref[:, nkv:nkv + HEAD_PAD] = lat_ref[:, nq_ + nkv:nq_ + nkv + HEAD_PAD].astype(BF16)

    out = lambda n: pl.BlockSpec((tm, n), lambda i: (i, 0))
    return pl.pallas_call(
        body, name=name, grid=(s // tm,),
        in_specs=[out(w), _vec_spec(nq_, 1), _vec_spec(nkv, 1)],
        out_specs=(out(nq_), out(nkv + HEAD_PAD)),
        out_shape=(jax.ShapeDtypeStruct((s, nq_), BF16), jax.ShapeDtypeStruct((s, nkv + HEAD_PAD), BF16)),
        compiler_params=_params(("parallel",)),
    )(lat, q_g, kv_g)


def _rms_mod_bwd(dh, x, g, sc, dres, name, branch=None):
    s, w = x.shape
    tm = _tile(s, ROW_T)
    has_res, has_br = dres is not None, branch is not None

    def body(*refs):
        dh_ref, x_ref, g_ref, sc_ref = refs[:4]
        rest = list(refs[4:])
        dres_ref = rest.pop(0) if has_res else None
        val_ref, bg_ref = (rest.pop(0), rest.pop(0)) if has_br else (None, None)
        dx_ref = rest.pop(0)
        db_ref = rest.pop(0) if has_br else None
        sums_ref = rest.pop(0)
        xv, dhv, gv = x_ref[...], dh_ref[...], g_ref[...]
        r = lax.rsqrt(jnp.mean(xv * xv, axis=-1, keepdims=True) + EPS)
        xhat = xv * r
        dxn = dhv * (1.0 + sc_ref[...])
        dxhat = dxn * gv
        dx = r * (dxhat - xhat * jnp.mean(dxhat * xhat, axis=-1, keepdims=True))
        if has_res:
            dx = dx + dres_ref[...]
        dx_ref[...] = dx

        @pl.when(pl.program_id(0) == 0)
        def _():
            sums_ref[...] = jnp.zeros_like(sums_ref)

        sums_ref[0:1, :] += jnp.sum(dhv, axis=0, keepdims=True)
        sums_ref[1:2, :] += jnp.sum(dhv * (xhat * gv), axis=0, keepdims=True)
        sums_ref[2:3, :] += jnp.sum(dxn * xhat, axis=0, keepdims=True)
        if has_br:
            db_ref[...] = (dx * bg_ref[...]).astype(db_ref.dtype)
            sums_ref[3:4, :] += jnp.sum(dx * val_ref[...], axis=0, keepdims=True)

    row = pl.BlockSpec((tm, w), lambda i: (i, 0))
    in_specs = [row, row, _vec_spec(w, 1), _vec_spec(w, 1)] + ([row] if has_res else [])
    args = [dh, x, g, sc] + ([dres] if has_res else [])
    out_specs, out_shape = [row], [jax.ShapeDtypeStruct((s, w), F32)]
    if has_br:
        in_specs += [row, _vec_spec(w, 1)]
        args += list(branch)
        out_specs.append(row)
        out_shape.append(jax.ShapeDtypeStruct((s, w), BF16))
    out_specs.append(pl.BlockSpec((8, w), lambda i: (0, 0)))
    out_shape.append(jax.ShapeDtypeStruct((8, w), F32))
    return pl.pallas_call(
        body, name=name, grid=(s // tm,),
        in_specs=in_specs, out_specs=tuple(out_specs), out_shape=tuple(out_shape),
        compiler_params=_params(("arbitrary",)),
    )(*args)


def _final_loss(x3, target, g, ffn, gvec, name):
    s, w = x3.shape
    tm = _tile(s, ROW_T)

    def body(x_ref, t_ref, g_ref, ffn_ref, bg_ref, dx_ref, db_ref, sums_ref):
        xv, gv = x_ref[...], g_ref[...]
        r = lax.rsqrt(jnp.mean(xv * xv, axis=-1, keepdims=True) + EPS)
        xhat = xv * r
        err = xhat * gv - t_ref[...]
        dy = err * (1.0 / w)
        dxhat = dy * gv
        dx = r * (dxhat - xhat * jnp.mean(dxhat * xhat, axis=-1, keepdims=True))
        dx_ref[...] = dx
        db_ref[...] = (dx * bg_ref[...]).astype(db_ref.dtype)

        @pl.when(pl.program_id(0) == 0)
        def _():
            sums_ref[...] = jnp.zeros_like(sums_ref)

        sums_ref[0:1, :] += jnp.sum(dy * xhat, axis=0, keepdims=True)
        sums_ref[1:2, :] += jnp.zeros((1, w), F32) + (0.5 / w) * jnp.sum(err * err)
        sums_ref[2:3, :] += jnp.sum(dx * ffn_ref[...], axis=0, keepdims=True)

    row = pl.BlockSpec((tm, w), lambda i: (i, 0))
    return pl.pallas_call(
        body, name=name, grid=(s // tm,),
        in_specs=[row, row, _vec_spec(w, 1), row, _vec_spec(w, 1)],
        out_specs=(row, row, pl.BlockSpec((8, w), lambda i: (0, 0))),
        out_shape=(jax.ShapeDtypeStruct((s, w), F32), jax.ShapeDtypeStruct((s, w), BF16),
                   jax.ShapeDtypeStruct((8, w), F32)),
        compiler_params=_params(("arbitrary",)),
    )(x3, target, g, ffn, gvec)


def _rope_block(seg, cmul, smul):
    lane = lax.broadcasted_iota(jnp.int32, seg.shape, 1)
    swapped = jnp.where(lane < MLA_NOPE + MLA_ROPE // 2,
                        pltpu.roll(seg, HEAD_PAD - MLA_ROPE // 2, 1), pltpu.roll(seg, MLA_ROPE // 2, 1))
    return seg * cmul + swapped * smul


def _rope(t, cmul, smul, name):
    s, w = t.shape
    tm = _tile(s, ROW_T)

    def body(t_ref, c_ref, s_ref, o_ref):
        cv, sv = c_ref[...], s_ref[...]
        for hb in range(w // HEAD_PAD):
            lanes = slice(hb * HEAD_PAD, (hb + 1) * HEAD_PAD)
            o_ref[:, lanes] = _rope_block(t_ref[:, lanes].astype(F32), cv, sv).astype(o_ref.dtype)

    row = pl.BlockSpec((tm, w), lambda i: (i, 0))
    tab = pl.BlockSpec((tm, HEAD_PAD), lambda i: (i, 0))
    return pl.pallas_call(
        body, name=name, grid=(s // tm,),
        in_specs=[row, tab, tab], out_specs=row, out_shape=jax.ShapeDtypeStruct((s, w), BF16),
        compiler_params=_params(("parallel",)),
    )(t, cmul, smul)


def _rope_bwd_kv(dk, dv, cmul, smul, name):
    s, w = dk.shape
    tm = _tile(s, ROW_T)

    def body(dk_ref, dv_ref, c_ref, s_ref, o_ref):
        cv, sv = c_ref[...], s_ref[...]
        for hb in range(N_HEADS):
            lo, hi = hb * HEAD_PAD, (hb + 1) * HEAD_PAD
            o_ref[:, lo:hi] = _rope_block(dk_ref[:, lo:hi].astype(F32), cv, sv).astype(o_ref.dtype)
        o_ref[:, w:2 * w] = dv_ref[...].astype(o_ref.dtype)

    row = pl.BlockSpec((tm, w), lambda i: (i, 0))
    tab = pl.BlockSpec((tm, HEAD_PAD), lambda i: (i, 0))
    return pl.pallas_call(
        body, name=name, grid=(s // tm,),
        in_specs=[row, row, tab, tab],
        out_specs=pl.BlockSpec((tm, 2 * w), lambda i: (i, 0)),
        out_shape=jax.ShapeDtypeStruct((s, 2 * w), BF16),
        compiler_params=_params(("parallel",)),
    )(dk, dv, cmul, smul)


def _lanes(col, width):
    if col.shape[1] == 1:
        col = jnp.broadcast_to(col, (col.shape[0], HEAD_PAD))
    return jnp.tile(col, (1, width // HEAD_PAD))


def _fold_lanes(a):
    out = a[:, 0:HEAD_PAD]
    for g in range(1, a.shape[1] // HEAD_PAD):
        out = out + a[:, g * HEAD_PAD:(g + 1) * HEAD_PAD]
    return out


def _as_row(rep):
    return rep.T[0:1, :]


def _causal(t, rows_are_queries):
    row = lax.broadcasted_iota(jnp.int32, (t, t), 0)
    col = lax.broadcasted_iota(jnp.int32, (t, t), 1)
    return row >= col if rows_are_queries else col >= row


def _split_refs(refs, n_in, n_out, n_scratch, n_x):
    pos = [n_in, n_x, n_out, n_x, n_scratch, 3 if n_x else 0]
    out, at = [], 0
    for cnt in pos:
        out.append(refs[at:at + cnt])
        at += cnt
    return out


def _first_last_step(n0, n1):
    i0, i1 = pl.program_id(0), pl.program_id(1)
    return jnp.logical_and(i0 == 0, i1 == 0), jnp.logical_and(i0 == n0 - 1, i1 == n1 - 1)


def _as_lanes(row):
    return jnp.broadcast_to(row, (HEAD_PAD, row.shape[1])).T


def _attn_fwd(q, k, v, frow, name, gather=()):
    s = q.shape[0]
    t = ATT_T
    nq = s // t
    use_f = frow is not None
    nx = len(gather)

    hpb = FWD_HEADS_PER_STEP

    def body(*refs):
        ins, x_src, outs, x_out, scr, x_sems = _split_refs(refs, 4 if use_f else 3, 3, 2, nx)
        if use_f:
            q_ref, k_ref, v_ref, fr_ref = ins
            fc_b = [_as_lanes(fr_ref[hh, pl.program_id(1)]) for hh in range(hpb)]
        else:
            q_ref, k_ref, v_ref = ins
        o_ref, ob_ref, lse_ref = outs
        m_s, acc_s = scr
        if nx:
            first, last = _first_last_step(N_HEADS // hpb, nq)
            x_start, x_wait = _direct_exchange(x_src, x_out, *x_sems, scatter=False)
            pl.when(first)(x_start)
        qi = pl.program_id(1)
        m_s[...] = jnp.full(m_s.shape, NEG_BIG, F32)
        acc_s[...] = jnp.zeros(acc_s.shape, F32)

        def step(j, masked):
            off = pl.multiple_of(j * t, t)
            for hh in range(hpb):
                lanes = slice(hh * HEAD_PAD, (hh + 1) * HEAD_PAD)
                kv = k_ref[pl.ds(off, t), lanes]
                vv = v_ref[pl.ds(off, t), lanes]
                sc = lax.dot_general(q_ref[:, lanes], kv, (((1,), (1,)), ((), ())), preferred_element_type=F32)
                if use_f:
                    sc = sc + (_lanes(fc_b[hh], t) - fr_ref[hh, j])
                if masked:
                    sc = jnp.where(_causal(t, True), sc, NEG_BIG)
                m_prev = m_s[hh]
                m_new = jnp.maximum(m_prev, jnp.max(sc, axis=-1, keepdims=True))
                p = jnp.exp2(sc - _lanes(m_new, t))
                acc_s[hh] = jnp.exp2(m_prev - m_new) * acc_s[hh] + jnp.dot(p.astype(BF16), vv,
                                                                           preferred_element_type=F32)
                m_s[hh] = m_new

        def loop_body(j, carry):
            step(j, False)
            return carry

        lax.fori_loop(0, qi, loop_body, 0)
        step(qi, True)
        for hh in range(hpb):
            lanes = slice(hh * HEAD_PAD, (hh + 1) * HEAD_PAD)
            acc = acc_s[hh]
            lane = lax.broadcasted_iota(jnp.int32, acc.shape, 1)
            denom = jnp.sum(jnp.where(lane == SUM_LANE, acc, 0.0), axis=-1, keepdims=True)
            o = acc * (1.0 / denom)
            o_ref[:, lanes] = o
            ob_ref[:, lanes] = o.astype(BF16)
            lse_ref[hh, 0] = _as_row(m_s[hh] + jnp.log(denom) * LOG2E)
        if nx:
            pl.when(last)(x_wait)

    w = hpb * HEAD_PAD
    qspec = pl.BlockSpec((t, w), lambda h, i: (i, h))
    kspec = pl.BlockSpec((s, w), lambda h, i: (0, h))
    any_spec = pl.BlockSpec(memory_space=pl.ANY)
    in_specs, args = [qspec, kspec, kspec], [q, k, v]
    if use_f:
        in_specs += [pl.BlockSpec((hpb, nq, 1, t), lambda h, i: (h, 0, 0, 0))]
        args += [frow]
    out_specs = [qspec, qspec, pl.BlockSpec((hpb, 1, 1, t), lambda h, i: (h, i, 0, 0))]
    out_shape = [jax.ShapeDtypeStruct((s, N_HEADS * HEAD_PAD), F32), jax.ShapeDtypeStruct((s, N_HEADS * HEAD_PAD), BF16),
                 jax.ShapeDtypeStruct((N_HEADS, nq, 1, t), F32)]
    scratch = [pltpu.VMEM((hpb, t, HEAD_PAD), F32), pltpu.VMEM((hpb, t, HEAD_PAD), F32)]
    if nx:
        in_specs += [any_spec] * nx
        args += list(gather)
        out_specs += [any_spec] * nx
        out_shape += [jax.ShapeDtypeStruct((N_DEV,) + g.shape, g.dtype) for g in gather]
        scratch += _exchange_scratch(nx)
    return pl.pallas_call(
        body, name=name, grid=(N_HEADS // hpb, nq),
        in_specs=in_specs, out_specs=tuple(out_specs), out_shape=tuple(out_shape),
        scratch_shapes=scratch,
        compiler_params=_params(("arbitrary", "arbitrary") if nx else ("parallel", "arbitrary")),
    )(*args)


def _attn_delta(o, do, name):
    s, w = o.shape
    t = ATT_T

    def body(o_ref, do_ref, d_ref):
        for hb in range(N_HEADS):
            lo, hi = hb * HEAD_PAD, (hb + 1) * HEAD_PAD
            prod = o_ref[:, lo:hi] * do_ref[:, lo:hi].astype(F32)
            d_ref[hb, 0] = jnp.sum(prod.T, axis=0, keepdims=True)

    row = pl.BlockSpec((t, w), lambda i: (i, 0))
    return pl.pallas_call(
        body, name=name, grid=(s // t,),
        in_specs=[row, row],
        out_specs=pl.BlockSpec((N_HEADS, 1, 1, t), lambda i: (0, i, 0, 0)),
        out_shape=jax.ShapeDtypeStruct((N_HEADS, s // t, 1, t), F32),
        compiler_params=_params(("parallel",)),
    )(o, do)


def _attn_bwd(q, k, v, do, lse_row, delta_row, frow, scale_q, scale_k, name, out_dtype, scatter=()):
    s = q.shape[0]
    t = ATT_T
    nq = s // t
    use_f = frow is not None
    nx = len(scatter)
    hpb = BWD_HEADS_PER_STEP

    def body(*refs):
        ins, x_src, outs, x_out, scr, x_sems = _split_refs(refs, 7 if use_f else 6, 5 if use_f else 3,
                                                           5 if use_f else 3, nx)
        if use_f:
            q_ref, k_ref, v_ref, do_ref, lse_ref, dl_ref, fr_ref = ins
            dq_ref, dk_ref, dv_ref, dr_ref, df_ref = outs
            dq_s, dk_s, dv_s, dr_s, df_s = scr
            fc_b = [_as_lanes(fr_ref[hh, pl.program_id(1)]) for hh in range(hpb)]
        else:
            q_ref, k_ref, v_ref, do_ref, lse_ref, dl_ref = ins
            dq_ref, dk_ref, dv_ref = outs
            dq_s, dk_s, dv_s = scr
        if nx:
            first, last = _first_last_step(N_HEADS // hpb, nq)
            x_start, x_wait = _direct_exchange(x_src, x_out, *x_sems, scatter=True)
            pl.when(first)(x_start)
        kj = pl.program_id(1)

        @pl.when(kj == 0)
        def _():
            dq_s[...] = jnp.zeros(dq_s.shape, F32)
            if use_f:
                dr_s[...] = jnp.zeros(dr_s.shape, F32)

        dk_s[...] = jnp.zeros(dk_s.shape, F32)
        dv_s[...] = jnp.zeros(dv_s.shape, F32)
        if use_f:
            df_s[...] = jnp.zeros(df_s.shape, F32)

        def step(i, masked):
            off = pl.multiple_of(i * t, t)
            for hh in range(hpb):
                lanes = slice(hh * HEAD_PAD, (hh + 1) * HEAD_PAD)
                kv, vv = k_ref[:, lanes], v_ref[:, lanes]
                qv = q_ref[pl.ds(off, t), lanes]
                dov = do_ref[pl.ds(off, t), lanes]
                st = lax.dot_general(kv, qv, (((1,), (1,)), ((), ())), preferred_element_type=F32)
                if use_f:
                    st = st + (fr_ref[hh, i] - _lanes(fc_b[hh], t))
                if masked:
                    st = jnp.where(_causal(t, False), st, NEG_BIG)
                pt = jnp.exp2(st - lse_ref[hh, i])
                dv_s[hh] += jnp.dot(pt.astype(BF16), dov, preferred_element_type=F32)
                dpt = lax.dot_general(vv, dov, (((1,), (1,)), ((), ())), preferred_element_type=F32)
                dst = pt * (dpt - dl_ref[hh, i])
                dsb = dst.astype(BF16)
                dk_s[hh] += jnp.dot(dsb, qv, preferred_element_type=F32)
                dq_s[hh, pl.ds(off, t), :] += lax.dot_general(dsb, kv, (((0,), (0,)), ((), ())),
                                                              preferred_element_type=F32)
                if use_f:
                    df_s[hh] -= _fold_lanes(dst)
                    dr_s[hh, i] += jnp.sum(dst, axis=0, keepdims=True)

        step(kj, True)

        def loop_body(i, carry):
            step(i, False)
            return carry

        lax.fori_loop(kj + 1, nq, loop_body, 0)
        for hh in range(hpb):
            lanes = slice(hh * HEAD_PAD, (hh + 1) * HEAD_PAD)
            dk_ref[:, lanes] = (dk_s[hh] * scale_k).astype(dk_ref.dtype)
            dv_ref[:, lanes] = dv_s[hh].astype(dv_ref.dtype)
            if use_f:
                df_ref[hh, 0] = jnp.sum(df_s[hh].T, axis=0, keepdims=True)

        @pl.when(kj == nq - 1)
        def _():
            for hh in range(hpb):
                dq_ref[:, hh * HEAD_PAD:(hh + 1) * HEAD_PAD] = (dq_s[hh] * scale_q).astype(dq_ref.dtype)
            if use_f:
                dr_ref[...] = dr_s[...]

        if nx:
            pl.when(last)(x_wait)

    w = hpb * HEAD_PAD
    kspec = pl.BlockSpec((t, w), lambda h, j: (j, h))
    qspec = pl.BlockSpec((s, w), lambda h, j: (0, h))
    rowspec = pl.BlockSpec((hpb, nq, 1, t), lambda h, j: (h, 0, 0, 0))
    any_spec = pl.BlockSpec(memory_space=pl.ANY)
    in_specs, args = [qspec, kspec, kspec, qspec, rowspec, rowspec], [q, k, v, do, lse_row, delta_row]
    full = jax.ShapeDtypeStruct((s, N_HEADS * HEAD_PAD), out_dtype)
    out_specs, out_shape = [qspec, kspec, kspec], [full, full, full]
    scratch = [pltpu.VMEM((hpb, s, HEAD_PAD), F32), pltpu.VMEM((hpb, t, HEAD_PAD), F32),
               pltpu.VMEM((hpb, t, HEAD_PAD), F32)]
    if use_f:
        in_specs += [rowspec]
        args += [frow]
        out_specs += [rowspec, pl.BlockSpec((hpb, 1, 1, t), lambda h, j: (h, j, 0, 0))]
        out_shape += [jax.ShapeDtypeStruct((N_HEADS, nq, 1, t), F32)] * 2
        scratch += [pltpu.VMEM((hpb, nq, 1, t), F32), pltpu.VMEM((hpb, t, HEAD_PAD), F32)]
    if nx:
        in_specs += [any_spec] * nx
        args += list(scatter)
        out_specs += [any_spec] * nx
        out_shape += [jax.ShapeDtypeStruct(g.shape, g.dtype) for g in scatter]
        scratch += _exchange_scratch(nx)
    return pl.pallas_call(
        body, name=name, grid=(N_HEADS // hpb, nq),
        in_specs=in_specs, out_specs=tuple(out_specs), out_shape=tuple(out_shape),
        scratch_shapes=scratch,
        compiler_params=_params(("arbitrary", "arbitrary") if nx else ("parallel", "arbitrary")),
    )(*args)


def _gate_fwd(pm, pf, gates, name):
    s, w = pm.shape
    tm = _tile(s, ROW_T)

    def body(pm_ref, pf_ref, g_ref, y_ref):
        y = (jax.nn.sigmoid(g_ref[:, 0:w]) * pm_ref[...].astype(F32)
             + jax.nn.sigmoid(g_ref[:, w:2 * w]) * pf_ref[...].astype(F32))
        y_ref[...] = y.astype(y_ref.dtype)

    row = pl.BlockSpec((tm, w), lambda i: (i, 0))
    return pl.pallas_call(
        body, name=name, grid=(s // tm,),
        in_specs=[row, row, pl.BlockSpec((tm, 2 * w), lambda i: (i, 0))],
        out_specs=row, out_shape=jax.ShapeDtypeStruct((s, w), BF16),
        compiler_params=_params(("parallel",)),
    )(pm, pf, gates)


def _gate_bwd(dy, pm, pf, gates, name):
    s, w = pm.shape
    tm = _tile(s, ROW_T)

    def body(dy_ref, pm_ref, pf_ref, g_ref, dpm_ref, dpf_ref, dg_ref):
        dyv = dy_ref[...]
        sm, sf = jax.nn.sigmoid(g_ref[:, 0:w]), jax.nn.sigmoid(g_ref[:, w:2 * w])
        dpm_ref[...] = (dyv * sm).astype(BF16)
        dpf_ref[...] = (dyv * sf).astype(BF16)
        dg_ref[:, 0:w] = (dyv * pm_ref[...].astype(F32) * (sm * (1.0 - sm))).astype(BF16)
        dg_ref[:, w:2 * w] = (dyv * pf_ref[...].astype(F32) * (sf * (1.0 - sf))).astype(BF16)

    row = pl.BlockSpec((tm, w), lambda i: (i, 0))
    wide = pl.BlockSpec((tm, 2 * w), lambda i: (i, 0))
    out = jax.ShapeDtypeStruct((s, w), BF16)
    return pl.pallas_call(
        body, name=name, grid=(s // tm,),
        in_specs=[row, row, row, wide],
        out_specs=(row, row, wide), out_shape=(out, out, jax.ShapeDtypeStruct((s, 2 * w), BF16)),
        compiler_params=_params(("parallel",)),
    )(dy, pm, pf, gates)


CONV_TN = D_FF // 2
CONV_TM = 256
CONV_T_TM = 512
HALO = BF16_ROWS


def _shift_down(u, prev, n):
    rolled = pltpu.roll(u, n, 0)
    prev_rolled = pltpu.roll(prev, n, 0)
    top = jnp.concatenate([prev_rolled, rolled[HALO:]], axis=0)
    row = lax.broadcasted_iota(jnp.int32, u.shape, 0)
    return jnp.where(row < n, top, rolled)


def _conv_tile(u, prev, w_ref, b_ref):
    um1 = _shift_down(u, prev, 1)
    um2 = _shift_down(u, prev, 2)
    uc = b_ref[...] + w_ref[0:1, :] * um2 + w_ref[1:2, :] * um1 + w_ref[2:3, :] * u
    return uc, um1, um2


def _conv_specs(tm, tn, ncol_off):
    blk = lambda off: pl.BlockSpec((tm, tn), lambda j, i: (i, j + off))
    halo = lambda off: pl.BlockSpec((HALO, tn), lambda j, i: (jnp.maximum(i * (tm // HALO) - 1, 0), j + off))
    wsp = lambda off: pl.BlockSpec((3, tn), lambda j, i: (0, j + off))
    bsp = lambda off: pl.BlockSpec((1, tn), lambda j, i: (0, j + off))
    return blk, halo, wsp, bsp


def _convglu_fwd(u, conv_w, conv_b, name):
    s = u.shape[0]
    tm, tn = _tile(s, CONV_TM), CONV_TN
    nj = D_FF // tn
    blk, halo, wsp, bsp = _conv_specs(tm, tn, nj)

    def body(ug_ref, pg_ref, uv_ref, pv_ref, wg_ref, wv_ref, bg_ref, bv_ref, a_ref):
        live = (pl.program_id(1) > 0).astype(F32)
        gate, _, _ = _conv_tile(ug_ref[...].astype(F32), pg_ref[...].astype(F32) * live, wg_ref, bg_ref)
        val, _, _ = _conv_tile(uv_ref[...].astype(F32), pv_ref[...].astype(F32) * live, wv_ref, bv_ref)
        a_ref[...] = (gate * jax.nn.sigmoid(gate) * val).astype(a_ref.dtype)

    return pl.pallas_call(
        body, name=name, grid=(nj, s // tm),
        in_specs=[blk(0), halo(0), blk(nj), halo(nj), wsp(0), wsp(nj), bsp(0), bsp(nj)],
        out_specs=blk(0), out_shape=jax.ShapeDtypeStruct((s, D_FF), BF16),
        compiler_params=_params(("parallel", "arbitrary")),
    )(u, u, u, u, conv_w, conv_w, conv_b, conv_b)


def _convglu_bwd(da, u, conv_w, conv_b, name):
    s = u.shape[0]
    tm, tn = _tile(s, CONV_TM), CONV_TN
    nj = D_FF // tn
    blk, halo, wsp, bsp = _conv_specs(tm, tn, nj)

    def body(da_ref, ug_ref, pg_ref, uv_ref, pv_ref, wg_ref, wv_ref, bg_ref, bv_ref,
             dg_ref, dv_ref, sg_ref, sv_ref):
        live = (pl.program_id(1) > 0).astype(F32)
        ug, uv = ug_ref[...].astype(F32), uv_ref[...].astype(F32)
        gate, ug1, ug2 = _conv_tile(ug, pg_ref[...].astype(F32) * live, wg_ref, bg_ref)
        val, uv1, uv2 = _conv_tile(uv, pv_ref[...].astype(F32) * live, wv_ref, bv_ref)
        dav = da_ref[...].astype(F32)
        sig = jax.nn.sigmoid(gate)
        dgate = dav * val * (sig * (1.0 + gate * (1.0 - sig)))
        dval = dav * (gate * sig)
        dg_ref[...] = dgate.astype(dg_ref.dtype)
        dv_ref[...] = dval.astype(dv_ref.dtype)

        @pl.when(pl.program_id(1) == 0)
        def _():
            sg_ref[...] = jnp.zeros_like(sg_ref)
            sv_ref[...] = jnp.zeros_like(sv_ref)

        for s_ref, d, taps in ((sg_ref, dgate, (ug2, ug1, ug)), (sv_ref, dval, (uv2, uv1, uv))):
            for r, tap in enumerate(taps):
                s_ref[r:r + 1, :] += jnp.sum(d * tap, axis=0, keepdims=True)
            s_ref[3:4, :] += jnp.sum(d, axis=0, keepdims=True)

    sums = lambda off: pl.BlockSpec((8, tn), lambda j, i: (0, j + off))
    return pl.pallas_call(
        body, name=name, grid=(nj, s // tm),
        in_specs=[blk(0), blk(0), halo(0), blk(nj), halo(nj), wsp(0), wsp(nj), bsp(0), bsp(nj)],
        out_specs=(blk(0), blk(0), sums(0), sums(0)),
        out_shape=(jax.ShapeDtypeStruct((s, D_FF), BF16), jax.ShapeDtypeStruct((s, D_FF), BF16),
                   jax.ShapeDtypeStruct((8, D_FF), F32), jax.ShapeDtypeStruct((8, D_FF), F32)),
        compiler_params=_params(("parallel", "arbitrary")),
    )(da, u, u, u, u, conv_w, conv_w, conv_b, conv_b)


def _conv_transpose(d, conv_w, name, col0, into=None):
    s, w = d.shape
    tm, tn = _tile(s, CONV_T_TM), CONV_TN
    last = s // tm - 1
    jo = col0 // tn

    def body(d_ref, nx_ref, w_ref, *rest):
        o_ref = rest[-1]
        dv = d_ref[...].astype(F32)
        nxt = nx_ref[...].astype(F32) * (pl.program_id(1) < last).astype(F32)
        row = lax.broadcasted_iota(jnp.int32, dv.shape, 0)

        def shift_up(n):
            rolled = pltpu.roll(dv, tm - n, 0)
            nxt_rolled = pltpu.roll(nxt, HALO - n, 0)
            bottom = jnp.concatenate([rolled[:tm - HALO], nxt_rolled], axis=0)
            return jnp.where(row >= tm - n, bottom, rolled)

        out = w_ref[2:3, :] * dv + w_ref[1:2, :] * shift_up(1) + w_ref[0:1, :] * shift_up(2)
        o_ref[...] = out.astype(o_ref.dtype)

    blk = pl.BlockSpec((tm, tn), lambda j, i: (i, j))
    nxt_spec = pl.BlockSpec((HALO, tn), lambda j, i: (jnp.minimum((i + 1) * (tm // HALO), s // HALO - 1), j))
    in_specs = [blk, nxt_spec, pl.BlockSpec((3, tn), lambda j, i: (0, j + jo))]
    args = [d, d, conv_w]
    if into is not None:
        in_specs.append(pl.BlockSpec(memory_space=pl.ANY))
        args.append(into)
    return pl.pallas_call(
        body, name=name, grid=(w // tn, s // tm),
        in_specs=in_specs,
        out_specs=pl.BlockSpec((tm, tn), lambda j, i: (i, j + jo)),
        out_shape=jax.ShapeDtypeStruct((s, 2 * D_FF), BF16),
        input_output_aliases={3: 0} if into is not None else {},
        compiler_params=_params(("parallel", "arbitrary")),
    )(*args)


def _split3(a):
    a1 = a.astype(BF16)
    r1 = a - a1.astype(F32)
    a2 = r1.astype(BF16)
    a3 = (r1 - a2.astype(F32)).astype(BF16)
    return a1, a2, a3


def _ones_dot_right(a, mat):
    return sum(jnp.dot(p, mat, preferred_element_type=F32) for p in _split3(a))


def _ones_dot_left(mat, a):
    return sum(jnp.dot(mat, p, preferred_element_type=F32) for p in _split3(a))


def _tri(n, cmp):
    r = lax.broadcasted_iota(jnp.int32, (n, n), 0)
    c = lax.broadcasted_iota(jnp.int32, (n, n), 1)
    return cmp(r, c).astype(BF16)


def _forget_fwd(z, bias, name):
    nh, nr, nl = z.shape

    def body(z_ref, b_ref, f_ref):
        within = _tri(nl, lambda r, c: r <= c)
        before = _tri(nr, lambda r, c: c < r)
        for h in range(nh):
            x = z_ref[h] + b_ref[h]
            lf = jnp.minimum(x, 0.0) - jnp.log(1.0 + jnp.exp(-jnp.abs(x)))
            pre = _ones_dot_right(lf, within)
            tot = jnp.zeros((nr, nl), F32) + jnp.sum(lf, axis=1, keepdims=True)
            f_ref[h] = pre + _ones_dot_left(before, tot)

    return pl.pallas_call(
        body, name=name, out_shape=jax.ShapeDtypeStruct(z.shape, F32),
        compiler_params=pltpu.CompilerParams(vmem_limit_bytes=VMEM_LIMIT_BYTES),
    )(z, bias)


def _forget_bwd(df_rows, df_cols, z, bias, name):
    nh, nr, nl = z.shape

    def body(dfr_ref, dfc_ref, z_ref, b_ref, dz_ref, db_ref):
        within = _tri(nl, lambda r, c: r >= c)
        after = _tri(nr, lambda r, c: c > r)
        for h in range(nh):
            g = dfr_ref[h] + dfc_ref[h]
            suf = _ones_dot_right(g, within)
            tot = jnp.zeros((nr, nl), F32) + jnp.sum(g, axis=1, keepdims=True)
            dlf = suf + _ones_dot_left(after, tot)
            dz = dlf * jax.nn.sigmoid(-(z_ref[h] + b_ref[h]))
            dz_ref[h] = dz
            db_ref[h] = jnp.zeros((1, nl), F32) + jnp.sum(dz)

    return pl.pallas_call(
        body, name=name,
        out_shape=(jax.ShapeDtypeStruct(z.shape, F32), jax.ShapeDtypeStruct(bias.shape, F32)),
        compiler_params=pltpu.CompilerParams(vmem_limit_bytes=VMEM_LIMIT_BYTES),
    )(df_rows, df_cols, z, bias)


def _ada_fwd(c_col, w, b, name):
    kdim, n = w.shape

    def body(c_ref, w_ref, b_ref, ada_ref, act_ref):
        wv = w_ref[...]
        for e in range(N_DEV):
            cv = c_ref[e]
            act = cv * jax.nn.sigmoid(cv)
            act_ref[e] = act
            ada_ref[e:e + 1, :] = jnp.sum(act * wv, axis=0, keepdims=True) + b_ref[...]

    return pl.pallas_call(
        body, name=name,
        out_shape=(jax.ShapeDtypeStruct((N_DEV, n), F32), jax.ShapeDtypeStruct((N_DEV, kdim, 1), F32)),
        compiler_params=pltpu.CompilerParams(vmem_limit_bytes=VMEM_LIMIT_BYTES),
    )(c_col, w, b)


def _ada_bwd(act_col, dada, name):
    kdim = act_col.shape[1]
    n = dada.shape[1]

    def body(act_ref, d_ref, g_ref):
        acc = act_ref[0] * d_ref[0:1, :]
        for e in range(1, N_DEV):
            acc = acc + act_ref[e] * d_ref[e:e + 1, :]
        g_ref[...] = acc

    return pl.pallas_call(
        body, name=name, out_shape=jax.ShapeDtypeStruct((kdim, n), F32),
        compiler_params=pltpu.CompilerParams(vmem_limit_bytes=VMEM_LIMIT_BYTES),
    )(act_col, dada)


def _adamw(parts, w, m, v, name, tr=128):
    npart, r, c = parts.shape
    tr = _tile(r, tr, step=BF16_ROWS) if r % BF16_ROWS == 0 else r

    def body(p_ref, w_ref, m_ref, v_ref, g_ref, d_ref, nm_ref, nv_ref):
        g = p_ref[0].astype(F32)
        for e in range(1, npart):
            g = g + p_ref[e].astype(F32)
        nm = ADAM_B1 * m_ref[...] + (1.0 - ADAM_B1) * g
        nv = ADAM_B2 * v_ref[...] + (1.0 - ADAM_B2) * (g * g)
        m_hat = nm / (1.0 - ADAM_B1 ** ADAM_STEP)
        v_hat = nv / (1.0 - ADAM_B2 ** ADAM_STEP)
        g_ref[...] = g
        d_ref[...] = -ADAM_LR * (m_hat / (jnp.sqrt(v_hat) + ADAM_EPS) + ADAM_WD * w_ref[...])
        nm_ref[...] = nm
        nv_ref[...] = nv

    row = pl.BlockSpec((tr, c), lambda i: (i, 0))
    out = jax.ShapeDtypeStruct((r, c), F32)
    return pl.pallas_call(
        body, name=name, grid=(r // tr,),
        in_specs=[pl.BlockSpec((npart, tr, c), lambda i: (0, i, 0)), row, row, row],
        out_specs=(row, row, row, row), out_shape=(out, out, out, out),
        compiler_params=_params(("parallel",)),
    )(parts, w, m, v)


EARLY = ("w_in", "w_uq", "w_ukv")
LATE = ("w_o_mla", "w_o_fox", "w_out", "w_up", "conv_w", "w_down")
BIG = EARLY + LATE


def _cols_to_full(stack):
    n, r, c = stack.shape
    return stack.transpose(1, 0, 2).reshape(r, n * c)


def _full_to_cols(full, c):
    r = full.shape[0]
    return full.reshape(r, N_DEV, c).transpose(1, 0, 2)


def _pad_heads(a, width):
    r = a.shape[0]
    a = a.reshape(r, N_HEADS, width)
    return jnp.pad(a, ((0, 0), (0, 0), (0, HEAD_PAD - width))).reshape(r, N_HEADS * HEAD_PAD)


def _unpad_heads(a, width):
    s = a.shape[0]
    return a.reshape(s, N_HEADS, HEAD_PAD)[:, :, :width].reshape(s, N_HEADS * width)


def _w_in_padded(w_in):
    seg = [w_in[:, IN_OFF[i]:IN_OFF[i + 1]] for i in range(9)]
    cq, ckv, kr, fq, fk, fv, fl, gm, gf = seg
    padc = lambda a, n: jnp.pad(a, ((0, 0), (0, n - a.shape[1])))
    return jnp.concatenate([gm, gf, fq, fk, fv, cq, ckv, padc(kr, 128), padc(fl, 128)], axis=1)


def _w_in_unpadded(g):
    return jnp.concatenate([
        g[:, P_CQ:P_CQ + 384], g[:, P_CKV:P_CKV + 256], g[:, P_KR:P_KR + 32], g[:, P_FQ:P_FQ + 512],
        g[:, P_FK:P_FK + 512], g[:, P_FV:P_FV + 512], g[:, P_FL:P_FL + 8], g[:, P_GM:P_GM + 1024],
        g[:, P_GF:P_GF + 1024]], axis=1)


SMALL = (("b_ada", 6144, 6144), ("norm_mix_g", 1024, 1024), ("q_norm_g", 384, 384), ("kv_norm_g", 256, 256),
         ("b_forget", 8, 128), ("norm_ffn_g", 1024, 1024), ("conv_b", 5632, 5632), ("norm_final_g", 1024, 1024),
         ("loss", 1, 128))
SMALL_OFF = {}
_o = 0
for _n, _real, _padded in SMALL:
    SMALL_OFF[_n] = _o
    _o += _padded
SMALL_W = _o


def _pack_small(vals):
    parts = []
    for nme, real, padded in SMALL:
        a = vals[nme].reshape(1, real).astype(F32)
        parts.append(jnp.pad(a, ((0, 0), (0, padded - real))))
    return jnp.concatenate(parts, axis=1)


def kernel(x, c, positions, w_ada, b_ada, norm_mix_g, w_in, q_norm_g, w_uq, kv_norm_g, w_ukv, b_forget, w_o_mla, w_o_fox, w_out, norm_ffn_g, w_up, conv_w, conv_b, w_down, norm_final_g, loss_target, m_w_ada, m_b_ada, m_norm_mix_g, m_w_in, m_q_norm_g, m_w_uq, m_kv_norm_g, m_w_ukv, m_b_forget, m_w_o_mla, m_w_o_fox, m_w_out, m_norm_ffn_g, m_w_up, m_conv_w, m_conv_b, m_w_down, m_norm_final_g, v_w_ada, v_b_ada, v_norm_mix_g, v_w_in, v_q_norm_g, v_w_uq, v_kv_norm_g, v_w_ukv, v_b_forget, v_w_o_mla, v_w_o_fox, v_w_out, v_norm_ffn_g, v_w_up, v_conv_w, v_conv_b, v_w_down, v_norm_final_g):
    me = 4 * lax.axis_index("x") + 2 * lax.axis_index("y") + lax.axis_index("c")
    x = x[0]
    target = loss_target[0]
    s = x.shape[0]
    nblk = s // ATT_T
    big_w = {"w_in": w_in, "w_uq": w_uq, "w_ukv": w_ukv, "w_o_mla": w_o_mla, "w_o_fox": w_o_fox,
             "w_out": w_out, "w_up": w_up, "conv_w": conv_w, "w_down": w_down}
    big_m = {"w_in": m_w_in, "w_uq": m_w_uq, "w_ukv": m_w_ukv, "w_o_mla": m_w_o_mla, "w_o_fox": m_w_o_fox,
             "w_out": m_w_out, "w_up": m_w_up, "conv_w": m_conv_w, "w_down": m_w_down}
    big_v = {"w_in": v_w_in, "w_uq": v_w_uq, "w_ukv": v_w_ukv, "w_o_mla": v_w_o_mla, "w_o_fox": v_w_o_fox,
             "w_out": v_w_out, "w_up": v_w_up, "conv_w": v_conv_w, "w_down": v_w_down}

    shard = lambda k: big_w[k][0] if k == "conv_w" else big_w[k][0].astype(BF16)
    c_all, *early = _all_gather([c] + [shard(k) for k in EARLY], "gather_weights")
    st = dict(zip(EARLY, early))
    w_in_p = _w_in_padded(_cols_to_full(st["w_in"]))
    uq = st["w_uq"]
    w_uq_p = jnp.pad(uq, ((0, 0), (0, 0), (0, HEAD_PAD - 96))).transpose(1, 0, 2).reshape(MLA_Q_RANK, 1024)
    ukv = st["w_ukv"]
    zeros64 = jnp.zeros((N_HEADS, MLA_KV_RANK, 64), BF16)
    w_uk_p = jnp.concatenate([ukv[:, :, :64], zeros64], axis=2).transpose(1, 0, 2).reshape(MLA_KV_RANK, 1024)
    w_uv_p = jnp.concatenate([ukv[:, :, 64:], zeros64], axis=2).transpose(1, 0, 2).reshape(MLA_KV_RANK, 1024)
    place = np.zeros((HEAD_PAD, N_HEADS, HEAD_PAD), np.float32)
    for j in range(MLA_ROPE):
        place[j, :, MLA_NOPE + j] = 1.0
    place = jnp.asarray(place.reshape(HEAD_PAD, 1024), BF16)
    w_kv_comb = jnp.concatenate([
        jnp.concatenate([w_uk_p, w_uv_p], axis=1),
        jnp.concatenate([place, jnp.zeros((HEAD_PAD, 1024), BF16)], axis=1)], axis=0)

    b_ada_mine = lax.dynamic_slice(b_ada, (0, me * 768), (1, 768))
    ada_cols, act_col = _ada_fwd(c_all.reshape(N_DEV, D_MODEL, 1), w_ada[0], b_ada_mine, "ada_fwd")
    (ada_all,) = _all_gather([ada_cols], "gather_ada")
    ada = lax.dynamic_slice(ada_all, (0, me, 0), (N_DEV, 1, 768)).reshape(1, N_ADA * D_MODEL)
    sh_m, sc_m, g_m, sh_f, sc_f, g_f = [ada[:, i * D_MODEL:(i + 1) * D_MODEL] for i in range(N_ADA)]

    inv_freq = ROPE_THETA ** (-jnp.arange(0, MLA_ROPE, 2, dtype=F32) / MLA_ROPE)
    ang = positions[0].astype(F32)[:, None] * inv_freq
    cos, sin = jnp.cos(ang), jnp.sin(ang)
    rope_c = jnp.concatenate([jnp.ones((s, 64), F32), cos, cos, jnp.zeros((s, 32), F32)], axis=1)
    rope_s = jnp.concatenate([jnp.zeros((s, 64), F32), -sin, sin, jnp.zeros((s, 32), F32)], axis=1)

    h1 = _rms_mod(x, norm_mix_g, sc_m, sh_m, "norm_mix")
    gates = _mm(h1, w_in_p[:, P_GM:P_FQ], "nn", F32, "proj_gates", tn=1024)
    fq = _mm(h1, _pad_heads(w_in_p[:, P_FQ:P_FK], 64), "nn", BF16, "proj_fq", tn=1024,
             gvec=jnp.full((1, 1024), FOX_SCALE * LOG2E, F32))
    fk = _mm(h1, _pad_heads(w_in_p[:, P_FK:P_FV], 64), "nn", BF16, "proj_fk", tn=1024)
    fv = _mm(h1, _pad_heads(w_in_p[:, P_FV:P_CQ], 64), "nn", BF16, "proj_fv", tn=1024, ones_lane=True)
    lat = _mm(h1, w_in_p[:, P_CQ:], "nn", F32, "proj_latent", tn=D_IN_P - P_CQ)
    cq = lat[:, 0:384]
    ckv = lat[:, P_CKV - P_CQ:P_CKV - P_CQ + 256]
    qn, kv_in = _latent_norm(lat, q_norm_g, kv_norm_g, "latent_norm")
    q_fold = MLA_SCALE * LOG2E
    q_att = _mm(qn, w_uq_p, "nn", BF16, "q_up", rope=(rope_c * q_fold, rope_s * q_fold))
    k_att = _mm(kv_in, w_kv_comb[:, :1024], "nn", BF16, "k_up", rope=(rope_c, rope_s))
    v_att = _mm(kv_in, w_kv_comb[:, 1024:], "nn", BF16, "v_up", ones_lane=True)
    o_mla, o_mla_b, lse_mla, *late = _attn_fwd(q_att, k_att, v_att, None, "mla_fwd",
                                               gather=[shard(k) for k in LATE])
    st.update(zip(LATE, late))
    pad_o = lambda full: jnp.pad(full.reshape(N_HEADS, 64, 1024), ((0, 0), (0, 64), (0, 0))).reshape(1024, 1024)
    w_o_mla_p = pad_o(_cols_to_full(st["w_o_mla"]))
    w_o_fox_p = pad_o(_cols_to_full(st["w_o_fox"]))
    w_out_f = st["w_out"].reshape(1024, 1024)
    w_up_f = _cols_to_full(st["w_up"])
    conv_w_f = _cols_to_full(st["conv_w"])
    w_down_f = st["w_down"].reshape(D_FF, 1024)

    z = lat[:, P_FL - P_CQ:P_FL - P_CQ + 8].T.reshape(N_HEADS, s // SEQ_LANES, SEQ_LANES)
    bias_f = jnp.broadcast_to(b_forget.reshape(N_HEADS, 1, 1), (N_HEADS, 1, SEQ_LANES))
    f_cum = _forget_fwd(z, bias_f, "forget_fwd")
    f_row = (f_cum * LOG2E).reshape(N_HEADS, nblk, 1, ATT_T)
    o_fox, o_fox_b, lse_fox = _attn_fwd(fq, fk, fv, f_row, "fox_fwd")

    pm = _mm(o_mla_b, w_o_mla_p, "nn", BF16, "o_mla_proj")
    pf = _mm(o_fox_b, w_o_fox_p, "nn", BF16, "o_fox_proj")
    y = _gate_fwd(pm, pf, gates, "gate_fwd")
    x2, mix = _mm(y, w_out_f, "nn", F32, "out_proj", res=x, gvec=g_m)

    h2 = _rms_mod(x2, norm_ffn_g, sc_f, sh_f, "norm_ffn")
    u = _mm(h2, w_up_f, "nn", BF16, "ffn_up", tn=D_FF // 2)
    a = _convglu_fwd(u, conv_w_f, conv_b, "convglu_fwd")
    x3, ffn = _mm(a, w_down_f, "nn", F32, "ffn_down", res=x2, gvec=g_f, tk=2816)

    dx3, dffn, sums_final = _final_loss(x3, target, norm_final_g.reshape(1, D_MODEL), ffn, g_f, "final_loss")
    da = _mm(dffn, w_down_f, "nt", BF16, "ffn_down_dx", tn=1408)
    g_w_down = _mm(a, dffn, "tn", F32, "ffn_down_dw", tm=256, tn=1024, tk=s)
    dgate, dval, s_gate, s_val = _convglu_bwd(da, u, conv_w_f, conv_b, "convglu_bwd")
    du = _conv_transpose(dgate, conv_w_f, "conv_t_gate", 0)
    du = _conv_transpose(dval, conv_w_f, "conv_t_val", D_FF, into=du)
    dh2 = _mm(du, w_up_f, "nt", F32, "ffn_up_dx", tn=512, tk=2 * D_FF)
    g_w_up = _mm(h2, du, "tn", F32, "ffn_up_dw", tn=256, tk=s)
    dx2, dmix, sums_ffn = _rms_mod_bwd(dh2, x2, norm_ffn_g, sc_f, dx3, "norm_ffn_bwd", branch=(mix, g_m))

    dy = _mm(dmix, w_out_f, "nt", F32, "out_proj_dx")
    g_w_out = _mm(y, dmix, "tn", F32, "out_proj_dw", tn=256, tk=s)
    dpm, dpf, dgates = _gate_bwd(dy, pm, pf, gates, "gate_bwd")
    do_mla_b = _mm(dpm, w_o_mla_p, "nt", BF16, "o_mla_dx", tn=1024)
    do_fox_b = _mm(dpf, w_o_fox_p, "nt", BF16, "o_fox_dx", tn=1024)
    g_w_o_mla_p = _mm(o_mla_b, dpm, "tn", F32, "o_mla_dw", tn=256, tk=s)
    g_w_o_fox_p = _mm(o_fox_b, dpf, "tn", F32, "o_fox_dw", tn=256, tk=s)

    unpad_o = lambda g: g.reshape(N_HEADS, HEAD_PAD, 1024)[:, :64].reshape(512, 1024)
    g_conv_w = jnp.concatenate([s_gate[0:3], s_val[0:3]], axis=1)
    g_blocks = {
        "w_o_mla": _full_to_cols(unpad_o(g_w_o_mla_p), 128), "w_o_fox": _full_to_cols(unpad_o(g_w_o_fox_p), 128),
        "w_out": g_w_out.reshape(N_DEV, 128, 1024), "w_up": _full_to_cols(g_w_up, 704),
        "conv_w": _full_to_cols(g_conv_w, 704), "w_down": g_w_down.reshape(N_DEV, 352, 1024)}

    delta_mla = _attn_delta(o_mla, do_mla_b, "mla_delta")
    dq_rot, dk_rot, dv_mla, *late_recv = _attn_bwd(
        q_att, k_att, v_att, do_mla_b, lse_mla, delta_mla, None, MLA_SCALE, 1.0 / LOG2E, "mla_bwd", BF16,
        scatter=[g_blocks[k].astype(BF16) for k in LATE])
    dq_pre = _rope(dq_rot, rope_c, -rope_s, "rope_q_bwd")
    dkv_pre = _rope_bwd_kv(dk_rot, dv_mla, rope_c, -rope_s, "rope_kv_bwd")
    dqn = _mm(dq_pre, w_uq_p, "nt", F32, "q_up_dx")
    g_w_uq_p = _mm(qn, dq_pre, "tn", F32, "q_up_dw", tk=s)
    dkv_in = _mm(dkv_pre, w_kv_comb, "nt", F32, "kv_up_dx")
    g_w_kv_comb = _mm(kv_in, dkv_pre, "tn", F32, "kv_up_dw", tk=s)
    dcq, sums_q = _rms_mod_bwd(dqn, cq, q_norm_g, jnp.zeros((1, 384), F32), None, "q_norm_bwd")
    dckv, sums_kv = _rms_mod_bwd(dkv_in[:, :256], ckv, kv_norm_g, jnp.zeros((1, 256), F32), None, "kv_norm_bwd")
    delta_fox = _attn_delta(o_fox, do_fox_b, "fox_delta")
    dfq, dfk, dfv, dfr, dfc = _attn_bwd(fq, fk, fv, do_fox_b, lse_fox, delta_fox, f_row,
                                        FOX_SCALE, 1.0 / LOG2E, "fox_bwd", BF16)
    df_rows = dfr.reshape(N_HEADS, s // SEQ_LANES, SEQ_LANES)
    df_cols = dfc.reshape(N_HEADS, s // SEQ_LANES, SEQ_LANES)
    dz, db_f = _forget_bwd(df_rows, df_cols, z, bias_f, "forget_bwd")
    dfl = jnp.pad(dz.reshape(N_HEADS, s).T, ((0, 0), (0, 128 - N_HEADS)))

    dproj = jnp.concatenate([
        dgates, _unpad_heads(dfq, 64), _unpad_heads(dfk, 64), _unpad_heads(dfv, 64),
        dcq.astype(BF16), dckv.astype(BF16), dkv_in[:, 256:384].astype(BF16), dfl.astype(BF16)], axis=1)
    g_w_in_p = _mm(h1, dproj, "tn", F32, "proj_in_dw", tm=512, tn=640, tk=s)

    g_w_in = _w_in_unpadded(g_w_in_p)
    g_uq = g_w_uq_p.reshape(MLA_Q_RANK, N_HEADS, HEAD_PAD)[:, :, :96].transpose(1, 0, 2)
    g_uk = g_w_kv_comb[:256, :1024].reshape(256, N_HEADS, HEAD_PAD)[:, :, :64]
    g_uv = g_w_kv_comb[:256, 1024:].reshape(256, N_HEADS, HEAD_PAD)[:, :, :64]
    g_ukv = jnp.concatenate([g_uk, g_uv], axis=2).transpose(1, 0, 2)
    g_blocks.update({"w_in": _full_to_cols(g_w_in, 533), "w_uq": g_uq, "w_ukv": g_ukv})
    dh1, *early_recv = _mm(dproj, w_in_p, "nt", F32, "proj_in_dx", tn=512, tk=D_IN_P,
                           scatter=[g_blocks[k].astype(BF16) for k in EARLY])
    grad_x, sums_mix = _rms_mod_bwd(dh1, x, norm_mix_g, sc_m, dx2, "norm_mix_bwd")
    g_big, d_big, nm_big, nv_big = {}, {}, {}, {}
    for k, parts in zip(BIG, list(early_recv) + list(late_recv)):
        g_big[k], d_big[k], nm_big[k], nv_big[k] = [
            t[None] for t in _adamw(parts, big_w[k][0], big_m[k][0], big_v[k][0], "adamw_" + k)]

    dada = jnp.concatenate([sums_mix[0:1], sums_mix[1:2], sums_ffn[3:4], sums_ffn[0:1], sums_ffn[1:2], sums_final[2:3]],
                           axis=1)
    small_part = _pack_small({
        "b_ada": dada, "norm_mix_g": sums_mix[2:3], "q_norm_g": sums_q[2:3], "kv_norm_g": sums_kv[2:3],
        "b_forget": db_f[:, 0, 0], "norm_ffn_g": sums_ffn[2:3],
        "conv_b": jnp.concatenate([s_gate[3:4], s_val[3:4]], axis=1), "norm_final_g": sums_final[0:1],
        "loss": sums_final[1:2, 0:1]})
    (small_all,) = _all_gather([small_part], "gather_small")
    zero1 = jnp.zeros((1,), F32)
    small_w = {"b_ada": b_ada, "norm_mix_g": norm_mix_g, "q_norm_g": q_norm_g, "kv_norm_g": kv_norm_g,
               "b_forget": b_forget, "norm_ffn_g": norm_ffn_g, "conv_b": conv_b, "norm_final_g": norm_final_g,
               "loss": zero1}
    small_m = {"b_ada": m_b_ada, "norm_mix_g": m_norm_mix_g, "q_norm_g": m_q_norm_g, "kv_norm_g": m_kv_norm_g,
               "b_forget": m_b_forget, "norm_ffn_g": m_norm_ffn_g, "conv_b": m_conv_b,
               "norm_final_g": m_norm_final_g, "loss": zero1}
    small_v = {"b_ada": v_b_ada, "norm_mix_g": v_norm_mix_g, "q_norm_g": v_q_norm_g, "kv_norm_g": v_kv_norm_g,
               "b_forget": v_b_forget, "norm_ffn_g": v_norm_ffn_g, "conv_b": v_conv_b,
               "norm_final_g": v_norm_final_g, "loss": zero1}
    g_sm, d_sm, nm_sm, nv_sm = _adamw(small_all, _pack_small(small_w), _pack_small(small_m), _pack_small(small_v),
                                      "adamw_small")
    loss = g_sm[0, SMALL_OFF["loss"]]

    dada_all = small_all[:, 0, SMALL_OFF["b_ada"]:SMALL_OFF["b_ada"] + N_ADA * D_MODEL]
    dada_mine = lax.dynamic_slice(dada_all, (0, me * 768), (N_DEV, 768))
    g_ada_local = _ada_bwd(act_col, dada_mine, "ada_bwd")
    g_ada, d_ada, nm_ada, nv_ada = _adamw(g_ada_local[None], w_ada[0], m_w_ada[0], v_w_ada[0], "adamw_ada")

    def small_out(t, nme, shape):
        real = dict((n_, r_) for n_, r_, _ in SMALL)[nme]
        o = SMALL_OFF[nme]
        return t[0, o:o + real].reshape(shape)

    order = ["w_ada", "b_ada", "norm_mix_g", "w_in", "q_norm_g", "w_uq", "kv_norm_g", "w_ukv", "b_forget",
             "w_o_mla", "w_o_fox", "w_out", "norm_ffn_g", "w_up", "conv_w", "conv_b", "w_down", "norm_final_g"]
    small_shapes = {"b_ada": (1, 6144), "norm_mix_g": (1, 1024), "q_norm_g": (1, 384), "kv_norm_g": (1, 256),
                    "b_forget": (1, 8), "norm_ffn_g": (1, 1024), "conv_b": (1, 5632), "norm_final_g": (1024,)}

    def family(big, small, ada_t):
        out = []
        for nme in order:
            if nme == "w_ada":
                out.append(ada_t[None])
            elif nme in small_shapes:
                out.append(small_out(small, nme, small_shapes[nme]))
            else:
                out.append(big[nme])
        return out

    return (loss, grad_x[None], *family(g_big, g_sm, g_ada), *family(d_big, d_sm, d_ada),
            *family(nm_big, nm_sm, nm_ada), *family(nv_big, nv_sm, nv_ada))
```

```python
import math

import numpy as np
import jax
import jax.numpy as jnp
from jax import lax
from jax.experimental import pallas as pl
from jax.experimental.pallas import tpu as pltpu

F32 = jnp.float32
BF16 = jnp.bfloat16

N_DEV = 8
D_MODEL = 1024
N_HEADS = 8
HEAD_PAD = 128
MLA_Q_RANK = 384
MLA_KV_RANK = 256
MLA_NOPE = 64
MLA_ROPE = 32
MLA_V = 64
FOX_DIM = 64
D_FF = 2816
N_ADA = 6
EPS = 1e-6
ROPE_THETA = 10000.0
MLA_SCALE = 1.0 / math.sqrt(MLA_NOPE + MLA_ROPE)
FOX_SCALE = 1.0 / math.sqrt(FOX_DIM)
IN_SPLITS = (384, 256, 32, 512, 512, 512, 8, 1024, 1024)
D_IN = sum(IN_SPLITS)
IN_OFF = tuple(int(v) for v in np.cumsum((0,) + IN_SPLITS))
P_GM, P_GF, P_FQ, P_FK, P_FV, P_CQ, P_CKV, P_KR, P_FL, D_IN_P = 0, 1024, 2048, 2560, 3072, 3584, 3968, 4224, 4352, 4480

ADAM_LR, ADAM_B1, ADAM_B2, ADAM_EPS, ADAM_WD, ADAM_STEP = 0.001, 0.9, 0.999, 1e-08, 0.01, 10

VMEM_LIMIT_BYTES = 56 * 1024 * 1024
NEG_BIG = -1e30
ATT_T = 512
LOG2E = 1.4426950408889634
SUM_LANE = 64
ROW_T = 512
SEQ_LANES = 128
BF16_ROWS = 16
FWD_HEADS_PER_STEP = 4
BWD_HEADS_PER_STEP = 2


def _params(sem):
    return pltpu.CompilerParams(dimension_semantics=sem, vmem_limit_bytes=VMEM_LIMIT_BYTES)


def _tile(n, target, step=128):
    if n <= target:
        return n
    t = (target // step) * step
    while t >= step:
        if n % t == 0:
            return t
        t -= step
    return n


def _vec_spec(w, nargs):
    if nargs == 1:
        return pl.BlockSpec((1, w), lambda i: (0, 0))
    return pl.BlockSpec((1, w), lambda i, j: (0, 0))


def _comm_call(body, name, ins, out_shapes):
    n = len(ins)
    any_spec = pl.BlockSpec(memory_space=pl.ANY)
    return pl.pallas_call(
        body, name=name, out_shape=tuple(out_shapes),
        in_specs=[any_spec] * n, out_specs=tuple([any_spec] * n),
        scratch_shapes=[pltpu.SemaphoreType.DMA((n, 7)), pltpu.SemaphoreType.DMA((n, 7)),
                        pltpu.SemaphoreType.DMA((n,))],
    )(*ins)


def _all_gather(xs, name):
    n = len(xs)

    def body(*refs):
        x_refs, out_refs = refs[:n], refs[n:2 * n]
        send_sems, recv_sems, local_sems = refs[2 * n:]
        x_, y_, c_ = lax.axis_index("x"), lax.axis_index("y"), lax.axis_index("c")
        me, sibling = (x_, y_, c_), (x_, y_, 1 - c_)
        chips = [(1 - x_, y_), (x_, 1 - y_), (1 - x_, 1 - y_)]

        def slot(a, px, py, pc):
            return out_refs[a].at[4 * px + 2 * py + pc]

        def copy(a, k, block, to, src=None):
            return pltpu.make_async_remote_copy(
                src_ref=slot(a, *block) if src is None else src, dst_ref=slot(a, *block),
                send_sem=send_sems.at[a, k], recv_sem=recv_sems.at[a, k],
                device_id=to, device_id_type=pl.DeviceIdType.MESH)

        mine = [pltpu.make_async_copy(x_refs[a], slot(a, *me), local_sems.at[a]) for a in range(n)]
        for cp in mine:
            cp.start()
        first = []
        for a in range(n):
            first.append(copy(a, 0, me, sibling, src=x_refs[a]))
            first += [copy(a, 1 + j, me, (*chip, c_), src=x_refs[a]) for j, chip in enumerate(chips)]
        for cp in first:
            cp.start()
        passed = []
        for j, chip in enumerate(chips):
            for a in range(n):
                copy(a, 1 + j, (*chip, c_), me).wait_recv()
                passed.append(copy(a, 4 + j, (*chip, c_), sibling))
                passed[-1].start()
        for a in range(n):
            copy(a, 0, sibling, me).wait_recv()
            for j, chip in enumerate(chips):
                copy(a, 4 + j, (*chip, 1 - c_), me).wait_recv()
        for cp in first + passed:
            cp.wait_send()
        for cp in mine:
            cp.wait()

    return _comm_call(body, name, xs, [jax.ShapeDtypeStruct((N_DEV,) + x.shape, x.dtype) for x in xs])


def _direct_exchange(src_refs, out_refs, send_sems, recv_sems, local_sems, scatter):
    n = len(src_refs)
    x_, y_, c_ = lax.axis_index("x"), lax.axis_index("y"), lax.axis_index("c")
    me = 4 * x_ + 2 * y_ + c_

    def peer(k):
        return (x_ ^ ((k >> 2) & 1), y_ ^ ((k >> 1) & 1), c_ ^ (k & 1))

    def src(a, slot):
        return src_refs[a].at[slot] if scatter else src_refs[a]

    def copy(a, k, sending):
        px, py, pc = peer(k)
        theirs = 4 * px + 2 * py + pc
        return pltpu.make_async_remote_copy(
            src_ref=src(a, theirs if sending else me), dst_ref=out_refs[a].at[me if sending else theirs],
            send_sem=send_sems.at[a, k - 1], recv_sem=recv_sems.at[a, k - 1],
            device_id=(px, py, pc), device_id_type=pl.DeviceIdType.MESH)

    mine = [pltpu.make_async_copy(src(a, me), out_refs[a].at[me], local_sems.at[a]) for a in range(n)]
    sends = [copy(a, k, True) for a in range(n) for k in range(1, N_DEV)]

    def start():
        for cp in mine + sends:
            cp.start()

    def wait():
        for a in range(n):
            for k in range(1, N_DEV):
                copy(a, k, False).wait_recv()
        for cp in sends:
            cp.wait_send()
        for cp in mine:
            cp.wait()

    return start, wait


def _exchange_scratch(n):
    return [pltpu.SemaphoreType.DMA((n, 7)), pltpu.SemaphoreType.DMA((n, 7)), pltpu.SemaphoreType.DMA((n,))]


def _mm(a, b, mode, out_dtype, name, res=None, gvec=None, tm=1024, tn=512, tk=1024, rope=None, ones_lane=False,
        scatter=()):
    (k, m) = a.shape if mode == "tn" else a.shape[::-1]
    n = b.shape[0] if mode == "nt" else b.shape[1]
    tm, tn, tk = _tile(m, tm), _tile(n, tn), _tile(k, tk)
    nk = k // tk
    dims = {"nn": (((1,), (0,)), ((), ())), "nt": (((1,), (1,)), ((), ())), "tn": (((0,), (0,)), ((), ()))}[mode]
    has_res, has_g = res is not None, gvec is not None
    fused = has_res and has_g

    n_rope = 2 if rope is not None else 0
    nx = len(scatter)
    assert not nx or nk == 1, "the exchange rides only on a matmul with one K step"
    n_in = 2 + has_res + has_g + n_rope

    def body(*refs):
        acc_ref = refs[-1] if nk > 1 else None
        if nx:
            ins, x_src, outs, x_out, _, x_sems = _split_refs(refs, n_in, 1 + fused, 0, nx)
            refs = list(ins) + list(outs)
            i, j = pl.program_id(0), pl.program_id(1)
            x_start, x_wait = _direct_exchange(x_src, x_out, *x_sems, scatter=True)
            pl.when(jnp.logical_and(i == 0, j == 0))(x_start)
        else:
            refs = list(refs[:n_in + 1 + fused])
        a_ref, b_ref = refs[:2]
        res_ref = refs[2] if has_res else None
        g_ref = refs[2 + has_res] if has_g else None
        o_ref = refs[n_in]
        part = lax.dot_general(a_ref[...], b_ref[...], dims, preferred_element_type=F32)

        def finish(acc):
            if fused:
                refs[-1][...] = acc.astype(refs[-1].dtype)
            out = g_ref[...] * acc if has_g else acc
            if has_res:
                out = res_ref[...] + out
            if n_rope or ones_lane:
                one = (lax.broadcasted_iota(jnp.int32, (tm, HEAD_PAD), 1) == SUM_LANE).astype(F32)
                for hb in range(tn // HEAD_PAD):
                    lanes = slice(hb * HEAD_PAD, (hb + 1) * HEAD_PAD)
                    seg = out[:, lanes]
                    seg = _rope_block(seg, refs[n_in - 2][...], refs[n_in - 1][...]) if n_rope else seg + one
                    o_ref[:, lanes] = seg.astype(o_ref.dtype)
            else:
                o_ref[...] = out.astype(o_ref.dtype)

        if nk == 1:
            finish(part)
            if nx:
                pl.when(jnp.logical_and(i == m // tm - 1, j == n // tn - 1))(x_wait)
            return
        kk = pl.program_id(2)

        @pl.when(kk == 0)
        def _():
            acc_ref[...] = part

        @pl.when(kk > 0)
        def _():
            acc_ref[...] += part

        @pl.when(kk == nk - 1)
        def _():
            finish(acc_ref[...])

    if mode == "tn":
        a_spec = pl.BlockSpec((tk, tm), lambda i, j, kk: (kk, i))
    else:
        a_spec = pl.BlockSpec((tm, tk), lambda i, j, kk: (i, kk))
    if mode == "nt":
        b_spec = pl.BlockSpec((tn, tk), lambda i, j, kk: (j, kk))
    else:
        b_spec = pl.BlockSpec((tk, tn), lambda i, j, kk: (kk, j))
    o_spec = pl.BlockSpec((tm, tn), lambda i, j, kk: (i, j))
    in_specs, args = [a_spec, b_spec], [a, b]
    out_specs, out_shape = o_spec, jax.ShapeDtypeStruct((m, n), out_dtype)
    if has_res:
        in_specs.append(o_spec)
        args.append(res)
    if has_g:
        in_specs.append(pl.BlockSpec((1, tn), lambda i, j, kk: (0, j)))
        args.append(gvec)
    if n_rope:
        in_specs += [pl.BlockSpec((tm, HEAD_PAD), lambda i, j, kk: (i, 0))] * 2
        args += list(rope)
    if fused:
        out_specs = (o_spec, o_spec)
        out_shape = (out_shape, jax.ShapeDtypeStruct((m, n), BF16))
    scratch = [pltpu.VMEM((tm, tn), F32)] if nk > 1 else []
    if nx:
        any_spec = pl.BlockSpec(memory_space=pl.ANY)
        in_specs += [any_spec] * nx
        args += list(scatter)
        out_specs = tuple(out_specs if fused else (out_specs,)) + (any_spec,) * nx
        out_shape = tuple(out_shape if fused else (out_shape,)) + tuple(
            jax.ShapeDtypeStruct(g.shape, g.dtype) for g in scatter)
        scratch += _exchange_scratch(nx)
    return pl.pallas_call(
        body, name=name, grid=(m // tm, n // tn, nk),
        in_specs=in_specs, out_specs=out_specs, out_shape=out_shape,
        scratch_shapes=scratch,
        compiler_params=_params(("arbitrary",) * 3 if nx else ("parallel", "parallel", "arbitrary")),
    )(*args)


def _rms_mod(x, g, sc, sh, name):
    s, w = x.shape
    tm = _tile(s, ROW_T)

    def body(x_ref, g_ref, sc_ref, sh_ref, o_ref):
        xv = x_ref[...]
        r = lax.rsqrt(jnp.mean(xv * xv, axis=-1, keepdims=True) + EPS)
        o_ref[...] = ((xv * r * g_ref[...]) * (1.0 + sc_ref[...]) + sh_ref[...]).astype(o_ref.dtype)

    row = pl.BlockSpec((tm, w), lambda i: (i, 0))
    return pl.pallas_call(
        body, name=name, grid=(s // tm,),
        in_specs=[row, _vec_spec(w, 1), _vec_spec(w, 1), _vec_spec(w, 1)],
        out_specs=row, out_shape=jax.ShapeDtypeStruct((s, w), BF16),
        compiler_params=_params(("parallel",)),
    )(x, g, sc, sh)


def _latent_norm(lat, q_g, kv_g, name):
    s, w = lat.shape
    tm = _tile(s, ROW_T)
    nq_, nkv = MLA_Q_RANK, MLA_KV_RANK

    def norm(xv, gv):
        return xv * lax.rsqrt(jnp.mean(xv * xv, axis=-1, keepdims=True) + EPS) * gv

    def body(lat_ref, qg_ref, kg_ref, qn_ref, kv_ref):
        qn_ref[...] = norm(lat_ref[:, 0:nq_], qg_ref[...]).astype(BF16)
        kv_ref[:, 0:nkv] = norm(lat_ref[:, nq_:nq_ + nkv], kg_ref[...]).astype(BF16)
        kv_ref[:, nkv:nkv + HEAD_PAD] = lat_ref[:, nq_ + nkv:nq_ + nkv + HEAD_PAD].astype(BF16)

    out = lambda n: pl.BlockSpec((tm, n), lambda i: (i, 0))
    return pl.pallas_call(
        body, name=name, grid=(s // tm,),
        in_specs=[out(w), _vec_spec(nq_, 1), _vec_spec(nkv, 1)],
        out_specs=(out(nq_), out(nkv + HEAD_PAD)),
        out_shape=(jax.ShapeDtypeStruct((s, nq_), BF16), jax.ShapeDtypeStruct((s, nkv + HEAD_PAD), BF16)),
        compiler_params=_params(("parallel",)),
    )(lat, q_g, kv_g)


def _rms_mod_bwd(dh, x, g, sc, dres, name, branch=None):
    s, w = x.shape
    tm = _tile(s, ROW_T)
    has_res, has_br = dres is not None, branch is not None

    def body(*refs):
        dh_ref, x_ref, g_ref, sc_ref = refs[:4]
        rest = list(refs[4:])
        dres_ref = rest.pop(0) if has_res else None
        val_ref, bg_ref = (rest.pop(0), rest.pop(0)) if has_br else (None, None)
        dx_ref = rest.pop(0)
        db_ref = rest.pop(0) if has_br else None
        sums_ref = rest.pop(0)
        xv, dhv, gv = x_ref[...], dh_ref[...], g_ref[...]
        r = lax.rsqrt(jnp.mean(xv * xv, axis=-1, keepdims=True) + EPS)
        xhat = xv * r
        dxn = dhv * (1.0 + sc_ref[...])
        dxhat = dxn * gv
        dx = r * (dxhat - xhat * jnp.mean(dxhat * xhat, axis=-1, keepdims=True))
        if has_res:
            dx = dx + dres_ref[...]
        dx_ref[...] = dx

        @pl.when(pl.program_id(0) == 0)
        def _():
            sums_ref[...] = jnp.zeros_like(sums_ref)

        sums_ref[0:1, :] += jnp.sum(dhv, axis=0, keepdims=True)
        sums_ref[1:2, :] += jnp.sum(dhv * (xhat * gv), axis=0, keepdims=True)
        sums_ref[2:3, :] += jnp.sum(dxn * xhat, axis=0, keepdims=True)
        if has_br:
            db_ref[...] = (dx * bg_ref[...]).astype(db_ref.dtype)
            sums_ref[3:4, :] += jnp.sum(dx * val_ref[...].astype(F32), axis=0, keepdims=True)

    row = pl.BlockSpec((tm, w), lambda i: (i, 0))
    in_specs = [row, row, _vec_spec(w, 1), _vec_spec(w, 1)] + ([row] if has_res else [])
    args = [dh, x, g, sc] + ([dres] if has_res else [])
    out_specs, out_shape = [row], [jax.ShapeDtypeStruct((s, w), F32)]
    if has_br:
        in_specs += [row, _vec_spec(w, 1)]
        args += list(branch)
        out_specs.append(row)
        out_shape.append(jax.ShapeDtypeStruct((s, w), BF16))
    out_specs.append(pl.BlockSpec((8, w), lambda i: (0, 0)))
    out_shape.append(jax.ShapeDtypeStruct((8, w), F32))
    return pl.pallas_call(
        body, name=name, grid=(s // tm,),
        in_specs=in_specs, out_specs=tuple(out_specs), out_shape=tuple(out_shape),
        compiler_params=_params(("arbitrary",)),
    )(*args)


def _final_loss(x3, target, g, ffn, gvec, name):
    s, w = x3.shape
    tm = _tile(s, ROW_T)

    def body(x_ref, t_ref, g_ref, ffn_ref, bg_ref, dx_ref, db_ref, sums_ref):
        xv, gv = x_ref[...], g_ref[...]
        r = lax.rsqrt(jnp.mean(xv * xv, axis=-1, keepdims=True) + EPS)
        xhat = xv * r
        err = xhat * gv - t_ref[...]
        dy = err * (1.0 / w)
        dxhat = dy * gv
        dx = r * (dxhat - xhat * jnp.mean(dxhat * xhat, axis=-1, keepdims=True))
        dx_ref[...] = dx
        db_ref[...] = (dx * bg_ref[...]).astype(db_ref.dtype)

        @pl.when(pl.program_id(0) == 0)
        def _():
            sums_ref[...] = jnp.zeros_like(sums_ref)

        sums_ref[0:1, :] += jnp.sum(dy * xhat, axis=0, keepdims=True)
        sums_ref[1:2, :] += jnp.zeros((1, w), F32) + (0.5 / w) * jnp.sum(err * err)
        sums_ref[2:3, :] += jnp.sum(dx * ffn_ref[...].astype(F32), axis=0, keepdims=True)

    row = pl.BlockSpec((tm, w), lambda i: (i, 0))
    return pl.pallas_call(
        body, name=name, grid=(s // tm,),
        in_specs=[row, row, _vec_spec(w, 1), row, _vec_spec(w, 1)],
        out_specs=(row, row, pl.BlockSpec((8, w), lambda i: (0, 0))),
        out_shape=(jax.ShapeDtypeStruct((s, w), F32), jax.ShapeDtypeStruct((s, w), BF16),
                   jax.ShapeDtypeStruct((8, w), F32)),
        compiler_params=_params(("arbitrary",)),
    )(x3, target, g, ffn, gvec)


def _rope_block(seg, cmul, smul):
    lane = lax.broadcasted_iota(jnp.int32, seg.shape, 1)
    swapped = jnp.where(lane < MLA_NOPE + MLA_ROPE // 2,
                        pltpu.roll(seg, HEAD_PAD - MLA_ROPE // 2, 1), pltpu.roll(seg, MLA_ROPE // 2, 1))
    return seg * cmul + swapped * smul


def _rope(t, cmul, smul, name):
    s, w = t.shape
    tm = _tile(s, ROW_T)

    def body(t_ref, c_ref, s_ref, o_ref):
        cv, sv = c_ref[...], s_ref[...]
        for hb in range(w // HEAD_PAD):
            lanes = slice(hb * HEAD_PAD, (hb + 1) * HEAD_PAD)
            o_ref[:, lanes] = _rope_block(t_ref[:, lanes].astype(F32), cv, sv).astype(o_ref.dtype)

    row = pl.BlockSpec((tm, w), lambda i: (i, 0))
    tab = pl.BlockSpec((tm, HEAD_PAD), lambda i: (i, 0))
    return pl.pallas_call(
        body, name=name, grid=(s // tm,),
        in_specs=[row, tab, tab], out_specs=row, out_shape=jax.ShapeDtypeStruct((s, w), BF16),
        compiler_params=_params(("parallel",)),
    )(t, cmul, smul)


def _rope_bwd_kv(dk, dv, cmul, smul, name):
    s, w = dk.shape
    tm = _tile(s, ROW_T)

    def body(dk_ref, dv_ref, c_ref, s_ref, o_ref):
        cv, sv = c_ref[...], s_ref[...]
        for hb in range(N_HEADS):
            lo, hi = hb * HEAD_PAD, (hb + 1) * HEAD_PAD
            o_ref[:, lo:hi] = _rope_block(dk_ref[:, lo:hi].astype(F32), cv, sv).astype(o_ref.dtype)
        o_ref[:, w:2 * w] = dv_ref[...].astype(o_ref.dtype)

    row = pl.BlockSpec((tm, w), lambda i: (i, 0))
    tab = pl.BlockSpec((tm, HEAD_PAD), lambda i: (i, 0))
    return pl.pallas_call(
        body, name=name, grid=(s // tm,),
        in_specs=[row, row, tab, tab],
        out_specs=pl.BlockSpec((tm, 2 * w), lambda i: (i, 0)),
        out_shape=jax.ShapeDtypeStruct((s, 2 * w), BF16),
        compiler_params=_params(("parallel",)),
    )(dk, dv, cmul, smul)


def _lanes(col, width):
    if col.shape[1] == 1:
        col = jnp.broadcast_to(col, (col.shape[0], HEAD_PAD))
    return jnp.tile(col, (1, width // HEAD_PAD))


def _fold_lanes(a):
    out = a[:, 0:HEAD_PAD]
    for g in range(1, a.shape[1] // HEAD_PAD):
        out = out + a[:, g * HEAD_PAD:(g + 1) * HEAD_PAD]
    return out


def _as_row(rep):
    return rep.T[0:1, :]


def _causal(t, rows_are_queries):
    row = lax.broadcasted_iota(jnp.int32, (t, t), 0)
    col = lax.broadcasted_iota(jnp.int32, (t, t), 1)
    return row >= col if rows_are_queries else col >= row


def _split_refs(refs, n_in, n_out, n_scratch, n_x):
    pos = [n_in, n_x, n_out, n_x, n_scratch, 3 if n_x else 0]
    out, at = [], 0
    for cnt in pos:
        out.append(refs[at:at + cnt])
        at += cnt
    return out


def _first_last_step(n0, n1):
    i0, i1 = pl.program_id(0), pl.program_id(1)
    return jnp.logical_and(i0 == 0, i1 == 0), jnp.logical_and(i0 == n0 - 1, i1 == n1 - 1)


def _as_lanes(row):
    return jnp.broadcast_to(row, (HEAD_PAD, row.shape[1])).T


def _attn_fwd(q, k, v, frow, name, gather=()):
    s = q.shape[0]
    t = ATT_T
    nq = s // t
    use_f = frow is not None
    nx = len(gather)

    hpb = FWD_HEADS_PER_STEP

    def body(*refs):
        ins, x_src, outs, x_out, scr, x_sems = _split_refs(refs, 4 if use_f else 3, 3, 2, nx)
        if use_f:
            q_ref, k_ref, v_ref, fr_ref = ins
            fc_b = [_as_lanes(fr_ref[hh, pl.program_id(1)]) for hh in range(hpb)]
        else:
            q_ref, k_ref, v_ref = ins
        o_ref, ob_ref, lse_ref = outs
        m_s, acc_s = scr
        if nx:
            first, last = _first_last_step(N_HEADS // hpb, nq)
            x_start, x_wait = _direct_exchange(x_src, x_out, *x_sems, scatter=False)
            pl.when(first)(x_start)
        qi = pl.program_id(1)
        m_s[...] = jnp.full(m_s.shape, NEG_BIG, F32)
        acc_s[...] = jnp.zeros(acc_s.shape, F32)

        def step(j, masked):
            off = pl.multiple_of(j * t, t)
            for hh in range(hpb):
                lanes = slice(hh * HEAD_PAD, (hh + 1) * HEAD_PAD)
                kv = k_ref[pl.ds(off, t), lanes]
                vv = v_ref[pl.ds(off, t), lanes]
                sc = lax.dot_general(q_ref[:, lanes], kv, (((1,), (1,)), ((), ())), preferred_element_type=F32)
                if use_f:
                    sc = sc + (_lanes(fc_b[hh], t) - fr_ref[hh, j])
                if masked:
                    sc = jnp.where(_causal(t, True), sc, NEG_BIG)
                m_prev = m_s[hh]
                m_new = jnp.maximum(m_prev, jnp.max(sc, axis=-1, keepdims=True))
                p = jnp.exp2(sc - _lanes(m_new, t))
                acc_s[hh] = jnp.exp2(m_prev - m_new) * acc_s[hh] + jnp.dot(p.astype(BF16), vv,
                                                                           preferred_element_type=F32)
                m_s[hh] = m_new

        def loop_body(j, carry):
            step(j, False)
            return carry

        lax.fori_loop(0, qi, loop_body, 0)
        step(qi, True)
        for hh in range(hpb):
            lanes = slice(hh * HEAD_PAD, (hh + 1) * HEAD_PAD)
            acc = acc_s[hh]
            lane = lax.broadcasted_iota(jnp.int32, acc.shape, 1)
            denom = jnp.sum(jnp.where(lane == SUM_LANE, acc, 0.0), axis=-1, keepdims=True)
            o = acc * (1.0 / denom)
            o_ref[:, lanes] = o
            ob_ref[:, lanes] = o.astype(BF16)
            lse_ref[hh, 0] = _as_row(m_s[hh] + jnp.log(denom) * LOG2E)
        if nx:
            pl.when(last)(x_wait)

    w = hpb * HEAD_PAD
    qspec = pl.BlockSpec((t, w), lambda h, i: (i, h))
    kspec = pl.BlockSpec((s, w), lambda h, i: (0, h))
    any_spec = pl.BlockSpec(memory_space=pl.ANY)
    in_specs, args = [qspec, kspec, kspec], [q, k, v]
    if use_f:
        in_specs += [pl.BlockSpec((hpb, nq, 1, t), lambda h, i: (h, 0, 0, 0))]
        args += [frow]
    out_specs = [qspec, qspec, pl.BlockSpec((hpb, 1, 1, t), lambda h, i: (h, i, 0, 0))]
    out_shape = [jax.ShapeDtypeStruct((s, N_HEADS * HEAD_PAD), F32), jax.ShapeDtypeStruct((s, N_HEADS * HEAD_PAD), BF16),
                 jax.ShapeDtypeStruct((N_HEADS, nq, 1, t), F32)]
    scratch = [pltpu.VMEM((hpb, t, HEAD_PAD), F32), pltpu.VMEM((hpb, t, HEAD_PAD), F32)]
    if nx:
        in_specs += [any_spec] * nx
        args += list(gather)
        out_specs += [any_spec] * nx
        out_shape += [jax.ShapeDtypeStruct((N_DEV,) + g.shape, g.dtype) for g in gather]
        scratch += _exchange_scratch(nx)
    return pl.pallas_call(
        body, name=name, grid=(N_HEADS // hpb, nq),
        in_specs=in_specs, out_specs=tuple(out_specs), out_shape=tuple(out_shape),
        scratch_shapes=scratch,
        compiler_params=_params(("arbitrary", "arbitrary") if nx else ("parallel", "arbitrary")),
    )(*args)


def _attn_delta(o, do, name):
    s, w = o.shape
    t = ATT_T

    def body(o_ref, do_ref, d_ref):
        for hb in range(N_HEADS):
            lo, hi = hb * HEAD_PAD, (hb + 1) * HEAD_PAD
            prod = o_ref[:, lo:hi] * do_ref[:, lo:hi].astype(F32)
            d_ref[hb, 0] = jnp.sum(prod.T, axis=0, keepdims=True)

    row = pl.BlockSpec((t, w), lambda i: (i, 0))
    return pl.pallas_call(
        body, name=name, grid=(s // t,),
        in_specs=[row, row],
        out_specs=pl.BlockSpec((N_HEADS, 1, 1, t), lambda i: (0, i, 0, 0)),
        out_shape=jax.ShapeDtypeStruct((N_HEADS, s // t, 1, t), F32),
        compiler_params=_params(("parallel",)),
    )(o, do)


def _attn_bwd(q, k, v, do, lse_row, delta_row, frow, scale_q, scale_k, name, out_dtype, scatter=()):
    s = q.shape[0]
    t = ATT_T
    nq = s // t
    use_f = frow is not None
    nx = len(scatter)
    hpb = BWD_HEADS_PER_STEP

    def body(*refs):
        ins, x_src, outs, x_out, scr, x_sems = _split_refs(refs, 7 if use_f else 6, 5 if use_f else 3,
                                                           5 if use_f else 3, nx)
        if use_f:
            q_ref, k_ref, v_ref, do_ref, lse_ref, dl_ref, fr_ref = ins
            dq_ref, dk_ref, dv_ref, dr_ref, df_ref = outs
            dq_s, dk_s, dv_s, dr_s, df_s = scr
            fc_b = [_as_lanes(fr_ref[hh, pl.program_id(1)]) for hh in range(hpb)]
        else:
            q_ref, k_ref, v_ref, do_ref, lse_ref, dl_ref = ins
            dq_ref, dk_ref, dv_ref = outs
            dq_s, dk_s, dv_s = scr
        if nx:
            first, last = _first_last_step(N_HEADS // hpb, nq)
            x_start, x_wait = _direct_exchange(x_src, x_out, *x_sems, scatter=True)
            pl.when(first)(x_start)
        kj = pl.program_id(1)

        @pl.when(kj == 0)
        def _():
            dq_s[...] = jnp.zeros(dq_s.shape, F32)
            if use_f:
                dr_s[...] = jnp.zeros(dr_s.shape, F32)

        dk_s[...] = jnp.zeros(dk_s.shape, F32)
        dv_s[...] = jnp.zeros(dv_s.shape, F32)
        if use_f:
            df_s[...] = jnp.zeros(df_s.shape, F32)

        def step(i, masked):
            off = pl.multiple_of(i * t, t)
            for hh in range(hpb):
                lanes = slice(hh * HEAD_PAD, (hh + 1) * HEAD_PAD)
                kv, vv = k_ref[:, lanes], v_ref[:, lanes]
                qv = q_ref[pl.ds(off, t), lanes]
                dov = do_ref[pl.ds(off, t), lanes]
                st = lax.dot_general(kv, qv, (((1,), (1,)), ((), ())), preferred_element_type=F32)
                if use_f:
                    st = st + (fr_ref[hh, i] - _lanes(fc_b[hh], t))
                if masked:
                    st = jnp.where(_causal(t, False), st, NEG_BIG)
                pt = jnp.exp2(st - lse_ref[hh, i])
                dv_s[hh] += jnp.dot(pt.astype(BF16), dov, preferred_element_type=F32)
                dpt = lax.dot_general(vv, dov, (((1,), (1,)), ((), ())), preferred_element_type=F32)
                dst = pt * (dpt - dl_ref[hh, i])
                dsb = dst.astype(BF16)
                dk_s[hh] += jnp.dot(dsb, qv, preferred_element_type=F32)
                dq_s[hh, pl.ds(off, t), :] += lax.dot_general(dsb, kv, (((0,), (0,)), ((), ())),
                                                              preferred_element_type=F32)
                if use_f:
                    df_s[hh] -= _fold_lanes(dst)
                    dr_s[hh, i] += jnp.sum(dst, axis=0, keepdims=True)

        step(kj, True)

        def loop_body(i, carry):
            step(i, False)
            return carry

        lax.fori_loop(kj + 1, nq, loop_body, 0)
        for hh in range(hpb):
            lanes = slice(hh * HEAD_PAD, (hh + 1) * HEAD_PAD)
            dk_ref[:, lanes] = (dk_s[hh] * scale_k).astype(dk_ref.dtype)
            dv_ref[:, lanes] = dv_s[hh].astype(dv_ref.dtype)
            if use_f:
                df_ref[hh, 0] = jnp.sum(df_s[hh].T, axis=0, keepdims=True)

        @pl.when(kj == nq - 1)
        def _():
            for hh in range(hpb):
                dq_ref[:, hh * HEAD_PAD:(hh + 1) * HEAD_PAD] = (dq_s[hh] * scale_q).astype(dq_ref.dtype)
            if use_f:
                dr_ref[...] = dr_s[...]

        if nx:
            pl.when(last)(x_wait)

    w = hpb * HEAD_PAD
    kspec = pl.BlockSpec((t, w), lambda h, j: (j, h))
    qspec = pl.BlockSpec((s, w), lambda h, j: (0, h))
    rowspec = pl.BlockSpec((hpb, nq, 1, t), lambda h, j: (h, 0, 0, 0))
    any_spec = pl.BlockSpec(memory_space=pl.ANY)
    in_specs, args = [qspec, kspec, kspec, qspec, rowspec, rowspec], [q, k, v, do, lse_row, delta_row]
    full = jax.ShapeDtypeStruct((s, N_HEADS * HEAD_PAD), out_dtype)
    out_specs, out_shape = [qspec, kspec, kspec], [full, full, full]
    scratch = [pltpu.VMEM((hpb, s, HEAD_PAD), F32), pltpu.VMEM((hpb, t, HEAD_PAD), F32),
               pltpu.VMEM((hpb, t, HEAD_PAD), F32)]
    if use_f:
        in_specs += [rowspec]
        args += [frow]
        out_specs += [rowspec, pl.BlockSpec((hpb, 1, 1, t), lambda h, j: (h, j, 0, 0))]
        out_shape += [jax.ShapeDtypeStruct((N_HEADS, nq, 1, t), F32)] * 2
        scratch += [pltpu.VMEM((hpb, nq, 1, t), F32), pltpu.VMEM((hpb, t, HEAD_PAD), F32)]
    if nx:
        in_specs += [any_spec] * nx
        args += list(scatter)
        out_specs += [any_spec] * nx
        out_shape += [jax.ShapeDtypeStruct(g.shape, g.dtype) for g in scatter]
        scratch += _exchange_scratch(nx)
    return pl.pallas_call(
        body, name=name, grid=(N_HEADS // hpb, nq),
        in_specs=in_specs, out_specs=tuple(out_specs), out_shape=tuple(out_shape),
        scratch_shapes=scratch,
        compiler_params=_params(("arbitrary", "arbitrary") if nx else ("parallel", "arbitrary")),
    )(*args)


def _gate_fwd(pm, pf, gates, name):
    s, w = pm.shape
    tm = _tile(s, ROW_T)

    def body(pm_ref, pf_ref, g_ref, y_ref):
        y = (jax.nn.sigmoid(g_ref[:, 0:w].astype(F32)) * pm_ref[...].astype(F32)
             + jax.nn.sigmoid(g_ref[:, w:2 * w].astype(F32)) * pf_ref[...].astype(F32))
        y_ref[...] = y.astype(y_ref.dtype)

    row = pl.BlockSpec((tm, w), lambda i: (i, 0))
    return pl.pallas_call(
        body, name=name, grid=(s // tm,),
        in_specs=[row, row, pl.BlockSpec((tm, 2 * w), lambda i: (i, 0))],
        out_specs=row, out_shape=jax.ShapeDtypeStruct((s, w), BF16),
        compiler_params=_params(("parallel",)),
    )(pm, pf, gates)


def _gate_bwd(dy, pm, pf, gates, name):
    s, w = pm.shape
    tm = _tile(s, ROW_T)

    def body(dy_ref, pm_ref, pf_ref, g_ref, dpm_ref, dpf_ref, dg_ref):
        dyv = dy_ref[...].astype(F32)
        sm, sf = jax.nn.sigmoid(g_ref[:, 0:w].astype(F32)), jax.nn.sigmoid(g_ref[:, w:2 * w].astype(F32))
        dpm_ref[...] = (dyv * sm).astype(BF16)
        dpf_ref[...] = (dyv * sf).astype(BF16)
        dg_ref[:, 0:w] = (dyv * pm_ref[...].astype(F32) * (sm * (1.0 - sm))).astype(BF16)
        dg_ref[:, w:2 * w] = (dyv * pf_ref[...].astype(F32) * (sf * (1.0 - sf))).astype(BF16)

    row = pl.BlockSpec((tm, w), lambda i: (i, 0))
    wide = pl.BlockSpec((tm, 2 * w), lambda i: (i, 0))
    out = jax.ShapeDtypeStruct((s, w), BF16)
    return pl.pallas_call(
        body, name=name, grid=(s // tm,),
        in_specs=[row, row, row, wide],
        out_specs=(row, row, wide), out_shape=(out, out, jax.ShapeDtypeStruct((s, 2 * w), BF16)),
        compiler_params=_params(("parallel",)),
    )(dy, pm, pf, gates)


CONV_TN = D_FF // 2
CONV_TM = 256
CONV_T_TM = 512
HALO = BF16_ROWS


def _shift_down(u, prev, n):
    rolled = pltpu.roll(u, n, 0)
    prev_rolled = pltpu.roll(prev, n, 0)
    top = jnp.concatenate([prev_rolled, rolled[HALO:]], axis=0)
    row = lax.broadcasted_iota(jnp.int32, u.shape, 0)
    return jnp.where(row < n, top, rolled)


def _conv_tile(u, prev, w_ref, b_ref):
    um1 = _shift_down(u, prev, 1)
    um2 = _shift_down(u, prev, 2)
    uc = b_ref[...] + w_ref[0:1, :] * um2 + w_ref[1:2, :] * um1 + w_ref[2:3, :] * u
    return uc, um1, um2


def _conv_specs(tm, tn, ncol_off):
    blk = lambda off: pl.BlockSpec((tm, tn), lambda j, i: (i, j + off))
    halo = lambda off: pl.BlockSpec((HALO, tn), lambda j, i: (jnp.maximum(i * (tm // HALO) - 1, 0), j + off))
    wsp = lambda off: pl.BlockSpec((3, tn), lambda j, i: (0, j + off))
    bsp = lambda off: pl.BlockSpec((1, tn), lambda j, i: (0, j + off))
    return blk, halo, wsp, bsp


def _convglu_fwd(u, conv_w, conv_b, name):
    s = u.shape[0]
    tm, tn = _tile(s, CONV_TM), CONV_TN
    nj = D_FF // tn
    blk, halo, wsp, bsp = _conv_specs(tm, tn, nj)

    def body(ug_ref, pg_ref, uv_ref, pv_ref, wg_ref, wv_ref, bg_ref, bv_ref, a_ref):
        live = (pl.program_id(1) > 0).astype(F32)
        gate, _, _ = _conv_tile(ug_ref[...].astype(F32), pg_ref[...].astype(F32) * live, wg_ref, bg_ref)
        val, _, _ = _conv_tile(uv_ref[...].astype(F32), pv_ref[...].astype(F32) * live, wv_ref, bv_ref)
        a_ref[...] = (gate * jax.nn.sigmoid(gate) * val).astype(a_ref.dtype)

    return pl.pallas_call(
        body, name=name, grid=(nj, s // tm),
        in_specs=[blk(0), halo(0), blk(nj), halo(nj), wsp(0), wsp(nj), bsp(0), bsp(nj)],
        out_specs=blk(0), out_shape=jax.ShapeDtypeStruct((s, D_FF), BF16),
        compiler_params=_params(("parallel", "arbitrary")),
    )(u, u, u, u, conv_w, conv_w, conv_b, conv_b)


def _convglu_bwd(da, u, conv_w, conv_b, name):
    s = u.shape[0]
    tm, tn = _tile(s, CONV_TM), CONV_TN
    nj = D_FF // tn
    blk, halo, wsp, bsp = _conv_specs(tm, tn, nj)

    def body(da_ref, ug_ref, pg_ref, uv_ref, pv_ref, wg_ref, wv_ref, bg_ref, bv_ref,
             dg_ref, dv_ref, sg_ref, sv_ref):
        live = (pl.program_id(1) > 0).astype(F32)
        ug, uv = ug_ref[...].astype(F32), uv_ref[...].astype(F32)
        gate, ug1, ug2 = _conv_tile(ug, pg_ref[...].astype(F32) * live, wg_ref, bg_ref)
        val, uv1, uv2 = _conv_tile(uv, pv_ref[...].astype(F32) * live, wv_ref, bv_ref)
        dav = da_ref[...].astype(F32)
        sig = jax.nn.sigmoid(gate)
        dgate = dav * val * (sig * (1.0 + gate * (1.0 - sig)))
        dval = dav * (gate * sig)
        dg_ref[...] = dgate.astype(dg_ref.dtype)
        dv_ref[...] = dval.astype(dv_ref.dtype)

        @pl.when(pl.program_id(1) == 0)
        def _():
            sg_ref[...] = jnp.zeros_like(sg_ref)
            sv_ref[...] = jnp.zeros_like(sv_ref)

        for s_ref, d, taps in ((sg_ref, dgate, (ug2, ug1, ug)), (sv_ref, dval, (uv2, uv1, uv))):
            for r, tap in enumerate(taps):
                s_ref[r:r + 1, :] += jnp.sum(d * tap, axis=0, keepdims=True)
            s_ref[3:4, :] += jnp.sum(d, axis=0, keepdims=True)

    sums = lambda off: pl.BlockSpec((8, tn), lambda j, i: (0, j + off))
    return pl.pallas_call(
        body, name=name, grid=(nj, s // tm),
        in_specs=[blk(0), blk(0), halo(0), blk(nj), halo(nj), wsp(0), wsp(nj), bsp(0), bsp(nj)],
        out_specs=(blk(0), blk(0), sums(0), sums(0)),
        out_shape=(jax.ShapeDtypeStruct((s, D_FF), BF16), jax.ShapeDtypeStruct((s, D_FF), BF16),
                   jax.ShapeDtypeStruct((8, D_FF), F32), jax.ShapeDtypeStruct((8, D_FF), F32)),
        compiler_params=_params(("parallel", "arbitrary")),
    )(da, u, u, u, u, conv_w, conv_w, conv_b, conv_b)


def _conv_transpose(d, conv_w, name, col0, into=None):
    s, w = d.shape
    tm, tn = _tile(s, CONV_T_TM), CONV_TN
    last = s // tm - 1
    jo = col0 // tn

    def body(d_ref, nx_ref, w_ref, *rest):
        o_ref = rest[-1]
        dv = d_ref[...].astype(F32)
        nxt = nx_ref[...].astype(F32) * (pl.program_id(1) < last).astype(F32)
        row = lax.broadcasted_iota(jnp.int32, dv.shape, 0)

        def shift_up(n):
            rolled = pltpu.roll(dv, tm - n, 0)
            nxt_rolled = pltpu.roll(nxt, HALO - n, 0)
            bottom = jnp.concatenate([rolled[:tm - HALO], nxt_rolled], axis=0)
            return jnp.where(row >= tm - n, bottom, rolled)

        out = w_ref[2:3, :] * dv + w_ref[1:2, :] * shift_up(1) + w_ref[0:1, :] * shift_up(2)
        o_ref[...] = out.astype(o_ref.dtype)

    blk = pl.BlockSpec((tm, tn), lambda j, i: (i, j))
    nxt_spec = pl.BlockSpec((HALO, tn), lambda j, i: (jnp.minimum((i + 1) * (tm // HALO), s // HALO - 1), j))
    in_specs = [blk, nxt_spec, pl.BlockSpec((3, tn), lambda j, i: (0, j + jo))]
    args = [d, d, conv_w]
    if into is not None:
        in_specs.append(pl.BlockSpec(memory_space=pl.ANY))
        args.append(into)
    return pl.pallas_call(
        body, name=name, grid=(w // tn, s // tm),
        in_specs=in_specs,
        out_specs=pl.BlockSpec((tm, tn), lambda j, i: (i, j + jo)),
        out_shape=jax.ShapeDtypeStruct((s, 2 * D_FF), BF16),
        input_output_aliases={3: 0} if into is not None else {},
        compiler_params=_params(("parallel", "arbitrary")),
    )(*args)


def _split3(a):
    a1 = a.astype(BF16)
    r1 = a - a1.astype(F32)
    a2 = r1.astype(BF16)
    a3 = (r1 - a2.astype(F32)).astype(BF16)
    return a1, a2, a3


def _ones_dot_right(a, mat):
    return sum(jnp.dot(p, mat, preferred_element_type=F32) for p in _split3(a))


def _ones_dot_left(mat, a):
    return sum(jnp.dot(mat, p, preferred_element_type=F32) for p in _split3(a))


def _tri(n, cmp):
    r = lax.broadcasted_iota(jnp.int32, (n, n), 0)
    c = lax.broadcasted_iota(jnp.int32, (n, n), 1)
    return cmp(r, c).astype(BF16)


def _forget_fwd(z, bias, name):
    nh, nr, nl = z.shape

    def body(z_ref, b_ref, f_ref):
        within = _tri(nl, lambda r, c: r <= c)
        before = _tri(nr, lambda r, c: c < r)
        for h in range(nh):
            x = z_ref[h] + b_ref[h]
            lf = jnp.minimum(x, 0.0) - jnp.log(1.0 + jnp.exp(-jnp.abs(x)))
            pre = _ones_dot_right(lf, within)
            tot = jnp.zeros((nr, nl), F32) + jnp.sum(lf, axis=1, keepdims=True)
            f_ref[h] = pre + _ones_dot_left(before, tot)

    return pl.pallas_call(
        body, name=name, out_shape=jax.ShapeDtypeStruct(z.shape, F32),
        compiler_params=pltpu.CompilerParams(vmem_limit_bytes=VMEM_LIMIT_BYTES),
    )(z, bias)


def _forget_bwd(df_rows, df_cols, z, bias, name):
    nh, nr, nl = z.shape

    def body(dfr_ref, dfc_ref, z_ref, b_ref, dz_ref, db_ref):
        within = _tri(nl, lambda r, c: r >= c)
        after = _tri(nr, lambda r, c: c > r)
        for h in range(nh):
            g = dfr_ref[h] + dfc_ref[h]
            suf = _ones_dot_right(g, within)
            tot = jnp.zeros((nr, nl), F32) + jnp.sum(g, axis=1, keepdims=True)
            dlf = suf + _ones_dot_left(after, tot)
            dz = dlf * jax.nn.sigmoid(-(z_ref[h] + b_ref[h]))
            dz_ref[h] = dz
            db_ref[h] = jnp.zeros((1, nl), F32) + jnp.sum(dz)

    return pl.pallas_call(
        body, name=name,
        out_shape=(jax.ShapeDtypeStruct(z.shape, F32), jax.ShapeDtypeStruct(bias.shape, F32)),
        compiler_params=pltpu.CompilerParams(vmem_limit_bytes=VMEM_LIMIT_BYTES),
    )(df_rows, df_cols, z, bias)


def _ada_fwd(c_col, w, b, name):
    kdim, n = w.shape

    def body(c_ref, w_ref, b_ref, ada_ref, act_ref):
        wv = w_ref[...]
        for e in range(N_DEV):
            cv = c_ref[e]
            act = cv * jax.nn.sigmoid(cv)
            act_ref[e] = act
            ada_ref[e:e + 1, :] = jnp.sum(act * wv, axis=0, keepdims=True) + b_ref[...]

    return pl.pallas_call(
        body, name=name,
        out_shape=(jax.ShapeDtypeStruct((N_DEV, n), F32), jax.ShapeDtypeStruct((N_DEV, kdim, 1), F32)),
        compiler_params=pltpu.CompilerParams(vmem_limit_bytes=VMEM_LIMIT_BYTES),
    )(c_col, w, b)


def _ada_bwd(act_col, dada, name):
    kdim = act_col.shape[1]
    n = dada.shape[1]

    def body(act_ref, d_ref, g_ref):
        acc = act_ref[0] * d_ref[0:1, :]
        for e in range(1, N_DEV):
            acc = acc + act_ref[e] * d_ref[e:e + 1, :]
        g_ref[...] = acc

    return pl.pallas_call(
        body, name=name, out_shape=jax.ShapeDtypeStruct((kdim, n), F32),
        compiler_params=pltpu.CompilerParams(vmem_limit_bytes=VMEM_LIMIT_BYTES),
    )(act_col, dada)


def _adamw(parts, w, m, v, name, tr=128):
    npart, r, c = parts.shape
    tr = _tile(r, tr, step=BF16_ROWS) if r % BF16_ROWS == 0 else r

    def body(p_ref, w_ref, m_ref, v_ref, g_ref, d_ref, nm_ref, nv_ref):
        g = p_ref[0].astype(F32)
        for e in range(1, npart):
            g = g + p_ref[e].astype(F32)
        nm = ADAM_B1 * m_ref[...] + (1.0 - ADAM_B1) * g
        nv = ADAM_B2 * v_ref[...] + (1.0 - ADAM_B2) * (g * g)
        m_hat = nm / (1.0 - ADAM_B1 ** ADAM_STEP)
        v_hat = nv / (1.0 - ADAM_B2 ** ADAM_STEP)
        g_ref[...] = g
        d_ref[...] = -ADAM_LR * (m_hat / (jnp.sqrt(v_hat) + ADAM_EPS) + ADAM_WD * w_ref[...])
        nm_ref[...] = nm
        nv_ref[...] = nv

    row = pl.BlockSpec((tr, c), lambda i: (i, 0))
    out = jax.ShapeDtypeStruct((r, c), F32)
    return pl.pallas_call(
        body, name=name, grid=(r // tr,),
        in_specs=[pl.BlockSpec((npart, tr, c), lambda i: (0, i, 0)), row, row, row],
        out_specs=(row, row, row, row), out_shape=(out, out, out, out),
        compiler_params=_params(("parallel",)),
    )(parts, w, m, v)


EARLY = ("w_in", "w_uq", "w_ukv")
LATE = ("w_o_mla", "w_o_fox", "w_out", "w_up", "conv_w", "w_down")
BIG = EARLY + LATE


def _cols_to_full(stack):
    n, r, c = stack.shape
    return stack.transpose(1, 0, 2).reshape(r, n * c)


def _full_to_cols(full, c):
    r = full.shape[0]
    return full.reshape(r, N_DEV, c).transpose(1, 0, 2)


def _pad_heads(a, width):
    r = a.shape[0]
    a = a.reshape(r, N_HEADS, width)
    return jnp.pad(a, ((0, 0), (0, 0), (0, HEAD_PAD - width))).reshape(r, N_HEADS * HEAD_PAD)


def _unpad_heads(a, width):
    s = a.shape[0]
    return a.reshape(s, N_HEADS, HEAD_PAD)[:, :, :width].reshape(s, N_HEADS * width)


def _w_in_padded(w_in):
    seg = [w_in[:, IN_OFF[i]:IN_OFF[i + 1]] for i in range(9)]
    cq, ckv, kr, fq, fk, fv, fl, gm, gf = seg
    padc = lambda a, n: jnp.pad(a, ((0, 0), (0, n - a.shape[1])))
    return jnp.concatenate([gm, gf, fq, fk, fv, cq, ckv, padc(kr, 128), padc(fl, 128)], axis=1)


def _w_in_unpadded(g):
    return jnp.concatenate([
        g[:, P_CQ:P_CQ + 384], g[:, P_CKV:P_CKV + 256], g[:, P_KR:P_KR + 32], g[:, P_FQ:P_FQ + 512],
        g[:, P_FK:P_FK + 512], g[:, P_FV:P_FV + 512], g[:, P_FL:P_FL + 8], g[:, P_GM:P_GM + 1024],
        g[:, P_GF:P_GF + 1024]], axis=1)


SMALL = (("b_ada", 6144, 6144), ("norm_mix_g", 1024, 1024), ("q_norm_g", 384, 384), ("kv_norm_g", 256, 256),
         ("b_forget", 8, 128), ("norm_ffn_g", 1024, 1024), ("conv_b", 5632, 5632), ("norm_final_g", 1024, 1024),
         ("loss", 1, 128))
SMALL_OFF = {}
_o = 0
for _n, _real, _padded in SMALL:
    SMALL_OFF[_n] = _o
    _o += _padded
SMALL_W = _o


def _pack_small(vals):
    parts = []
    for nme, real, padded in SMALL:
        a = vals[nme].reshape(1, real).astype(F32)
        parts.append(jnp.pad(a, ((0, 0), (0, padded - real))))
    return jnp.concatenate(parts, axis=1)


def kernel(x, c, positions, w_ada, b_ada, norm_mix_g, w_in, q_norm_g, w_uq, kv_norm_g, w_ukv, b_forget, w_o_mla, w_o_fox, w_out, norm_ffn_g, w_up, conv_w, conv_b, w_down, norm_final_g, loss_target, m_w_ada, m_b_ada, m_norm_mix_g, m_w_in, m_q_norm_g, m_w_uq, m_kv_norm_g, m_w_ukv, m_b_forget, m_w_o_mla, m_w_o_fox, m_w_out, m_norm_ffn_g, m_w_up, m_conv_w, m_conv_b, m_w_down, m_norm_final_g, v_w_ada, v_b_ada, v_norm_mix_g, v_w_in, v_q_norm_g, v_w_uq, v_kv_norm_g, v_w_ukv, v_b_forget, v_w_o_mla, v_w_o_fox, v_w_out, v_norm_ffn_g, v_w_up, v_conv_w, v_conv_b, v_w_down, v_norm_final_g):
    me = 4 * lax.axis_index("x") + 2 * lax.axis_index("y") + lax.axis_index("c")
    x = x[0]
    target = loss_target[0]
    s = x.shape[0]
    nblk = s // ATT_T
    big_w = {"w_in": w_in, "w_uq": w_uq, "w_ukv": w_ukv, "w_o_mla": w_o_mla, "w_o_fox": w_o_fox,
             "w_out": w_out, "w_up": w_up, "conv_w": conv_w, "w_down": w_down}
    big_m = {"w_in": m_w_in, "w_uq": m_w_uq, "w_ukv": m_w_ukv, "w_o_mla": m_w_o_mla, "w_o_fox": m_w_o_fox,
             "w_out": m_w_out, "w_up": m_w_up, "conv_w": m_conv_w, "w_down": m_w_down}
    big_v = {"w_in": v_w_in, "w_uq": v_w_uq, "w_ukv": v_w_ukv, "w_o_mla": v_w_o_mla, "w_o_fox": v_w_o_fox,
             "w_out": v_w_out, "w_up": v_w_up, "conv_w": v_conv_w, "w_down": v_w_down}

    shard = lambda k: big_w[k][0] if k == "conv_w" else big_w[k][0].astype(BF16)
    c_all, *early = _all_gather([c] + [shard(k) for k in EARLY], "gather_weights")
    st = dict(zip(EARLY, early))
    w_in_p = _w_in_padded(_cols_to_full(st["w_in"]))
    uq = st["w_uq"]
    w_uq_p = jnp.pad(uq, ((0, 0), (0, 0), (0, HEAD_PAD - 96))).transpose(1, 0, 2).reshape(MLA_Q_RANK, 1024)
    ukv = st["w_ukv"]
    zeros64 = jnp.zeros((N_HEADS, MLA_KV_RANK, 64), BF16)
    w_uk_p = jnp.concatenate([ukv[:, :, :64], zeros64], axis=2).transpose(1, 0, 2).reshape(MLA_KV_RANK, 1024)
    w_uv_p = jnp.concatenate([ukv[:, :, 64:], zeros64], axis=2).transpose(1, 0, 2).reshape(MLA_KV_RANK, 1024)
    place = np.zeros((HEAD_PAD, N_HEADS, HEAD_PAD), np.float32)
    for j in range(MLA_ROPE):
        place[j, :, MLA_NOPE + j] = 1.0
    place = jnp.asarray(place.reshape(HEAD_PAD, 1024), BF16)
    w_kv_comb = jnp.concatenate([
        jnp.concatenate([w_uk_p, w_uv_p], axis=1),
        jnp.concatenate([place, jnp.zeros((HEAD_PAD, 1024), BF16)], axis=1)], axis=0)

    b_ada_mine = lax.dynamic_slice(b_ada, (0, me * 768), (1, 768))
    ada_cols, act_col = _ada_fwd(c_all.reshape(N_DEV, D_MODEL, 1), w_ada[0], b_ada_mine, "ada_fwd")
    (ada_all,) = _all_gather([ada_cols], "gather_ada")
    ada = lax.dynamic_slice(ada_all, (0, me, 0), (N_DEV, 1, 768)).reshape(1, N_ADA * D_MODEL)
    sh_m, sc_m, g_m, sh_f, sc_f, g_f = [ada[:, i * D_MODEL:(i + 1) * D_MODEL] for i in range(N_ADA)]

    inv_freq = ROPE_THETA ** (-jnp.arange(0, MLA_ROPE, 2, dtype=F32) / MLA_ROPE)
    ang = positions[0].astype(F32)[:, None] * inv_freq
    cos, sin = jnp.cos(ang), jnp.sin(ang)
    rope_c = jnp.concatenate([jnp.ones((s, 64), F32), cos, cos, jnp.zeros((s, 32), F32)], axis=1)
    rope_s = jnp.concatenate([jnp.zeros((s, 64), F32), -sin, sin, jnp.zeros((s, 32), F32)], axis=1)

    h1 = _rms_mod(x, norm_mix_g, sc_m, sh_m, "norm_mix")
    gates = _mm(h1, w_in_p[:, P_GM:P_FQ], "nn", BF16, "proj_gates", tn=1024)
    fq = _mm(h1, _pad_heads(w_in_p[:, P_FQ:P_FK], 64), "nn", BF16, "proj_fq", tn=1024,
             gvec=jnp.full((1, 1024), FOX_SCALE * LOG2E, F32))
    fk = _mm(h1, _pad_heads(w_in_p[:, P_FK:P_FV], 64), "nn", BF16, "proj_fk", tn=1024)
    fv = _mm(h1, _pad_heads(w_in_p[:, P_FV:P_CQ], 64), "nn", BF16, "proj_fv", tn=1024, ones_lane=True)
    lat = _mm(h1, w_in_p[:, P_CQ:], "nn", F32, "proj_latent", tn=D_IN_P - P_CQ)
    cq = lat[:, 0:384]
    ckv = lat[:, P_CKV - P_CQ:P_CKV - P_CQ + 256]
    qn, kv_in = _latent_norm(lat, q_norm_g, kv_norm_g, "latent_norm")
    q_fold = MLA_SCALE * LOG2E
    q_att = _mm(qn, w_uq_p, "nn", BF16, "q_up", rope=(rope_c * q_fold, rope_s * q_fold))
    k_att = _mm(kv_in, w_kv_comb[:, :1024], "nn", BF16, "k_up", rope=(rope_c, rope_s))
    v_att = _mm(kv_in, w_kv_comb[:, 1024:], "nn", BF16, "v_up", ones_lane=True)
    o_mla, o_mla_b, lse_mla, *late = _attn_fwd(q_att, k_att, v_att, None, "mla_fwd",
                                               gather=[shard(k) for k in LATE])
    st.update(zip(LATE, late))
    pad_o = lambda full: jnp.pad(full.reshape(N_HEADS, 64, 1024), ((0, 0), (0, 64), (0, 0))).reshape(1024, 1024)
    w_o_mla_p = pad_o(_cols_to_full(st["w_o_mla"]))
    w_o_fox_p = pad_o(_cols_to_full(st["w_o_fox"]))
    w_out_f = st["w_out"].reshape(1024, 1024)
    w_up_f = _cols_to_full(st["w_up"])
    conv_w_f = _cols_to_full(st["conv_w"])
    w_down_f = st["w_down"].reshape(D_FF, 1024)

    z = lat[:, P_FL - P_CQ:P_FL - P_CQ + 8].T.reshape(N_HEADS, s // SEQ_LANES, SEQ_LANES)
    bias_f = jnp.broadcast_to(b_forget.reshape(N_HEADS, 1, 1), (N_HEADS, 1, SEQ_LANES))
    f_cum = _forget_fwd(z, bias_f, "forget_fwd")
    f_row = (f_cum * LOG2E).reshape(N_HEADS, nblk, 1, ATT_T)
    o_fox, o_fox_b, lse_fox = _attn_fwd(fq, fk, fv, f_row, "fox_fwd")

    pm = _mm(o_mla_b, w_o_mla_p, "nn", BF16, "o_mla_proj")
    pf = _mm(o_fox_b, w_o_fox_p, "nn", BF16, "o_fox_proj")
    y = _gate_fwd(pm, pf, gates, "gate_fwd")
    x2, mix = _mm(y, w_out_f, "nn", F32, "out_proj", res=x, gvec=g_m)

    h2 = _rms_mod(x2, norm_ffn_g, sc_f, sh_f, "norm_ffn")
    u = _mm(h2, w_up_f, "nn", BF16, "ffn_up", tn=D_FF // 2)
    a = _convglu_fwd(u, conv_w_f, conv_b, "convglu_fwd")
    x3, ffn = _mm(a, w_down_f, "nn", F32, "ffn_down", res=x2, gvec=g_f, tk=2816)

    dx3, dffn, sums_final = _final_loss(x3, target, norm_final_g.reshape(1, D_MODEL), ffn, g_f, "final_loss")
    da = _mm(dffn, w_down_f, "nt", BF16, "ffn_down_dx", tn=1408)
    g_w_down = _mm(a, dffn, "tn", F32, "ffn_down_dw", tm=256, tn=1024, tk=s)
    dgate, dval, s_gate, s_val = _convglu_bwd(da, u, conv_w_f, conv_b, "convglu_bwd")
    du = _conv_transpose(dgate, conv_w_f, "conv_t_gate", 0)
    du = _conv_transpose(dval, conv_w_f, "conv_t_val", D_FF, into=du)
    dh2 = _mm(du, w_up_f, "nt", F32, "ffn_up_dx", tn=512, tk=2 * D_FF)
    g_w_up = _mm(h2, du, "tn", F32, "ffn_up_dw", tn=256, tk=s)
    dx2, dmix, sums_ffn = _rms_mod_bwd(dh2, x2, norm_ffn_g, sc_f, dx3, "norm_ffn_bwd", branch=(mix, g_m))

    dy = _mm(dmix, w_out_f, "nt", BF16, "out_proj_dx")
    g_w_out = _mm(y, dmix, "tn", F32, "out_proj_dw", tn=256, tk=s)
    dpm, dpf, dgates = _gate_bwd(dy, pm, pf, gates, "gate_bwd")
    do_mla_b = _mm(dpm, w_o_mla_p, "nt", BF16, "o_mla_dx", tn=1024)
    do_fox_b = _mm(dpf, w_o_fox_p, "nt", BF16, "o_fox_dx", tn=1024)
    g_w_o_mla_p = _mm(o_mla_b, dpm, "tn", F32, "o_mla_dw", tn=256, tk=s)
    g_w_o_fox_p = _mm(o_fox_b, dpf, "tn", F32, "o_fox_dw", tn=256, tk=s)

    unpad_o = lambda g: g.reshape(N_HEADS, HEAD_PAD, 1024)[:, :64].reshape(512, 1024)
    g_conv_w = jnp.concatenate([s_gate[0:3], s_val[0:3]], axis=1)
    g_blocks = {
        "w_o_mla": _full_to_cols(unpad_o(g_w_o_mla_p), 128), "w_o_fox": _full_to_cols(unpad_o(g_w_o_fox_p), 128),
        "w_out": g_w_out.reshape(N_DEV, 128, 1024), "w_up": _full_to_cols(g_w_up, 704),
        "conv_w": _full_to_cols(g_conv_w, 704), "w_down": g_w_down.reshape(N_DEV, 352, 1024)}

    delta_mla = _attn_delta(o_mla, do_mla_b, "mla_delta")
    dq_rot, dk_rot, dv_mla, *late_recv = _attn_bwd(
        q_att, k_att, v_att, do_mla_b, lse_mla, delta_mla, None, MLA_SCALE, 1.0 / LOG2E, "mla_bwd", BF16,
        scatter=[g_blocks[k].astype(BF16) for k in LATE])
    dq_pre = _rope(dq_rot, rope_c, -rope_s, "rope_q_bwd")
    dkv_pre = _rope_bwd_kv(dk_rot, dv_mla, rope_c, -rope_s, "rope_kv_bwd")
    dqn = _mm(dq_pre, w_uq_p, "nt", F32, "q_up_dx")
    g_w_uq_p = _mm(qn, dq_pre, "tn", F32, "q_up_dw", tk=s)
    dkv_in = _mm(dkv_pre, w_kv_comb, "nt", F32, "kv_up_dx")
    g_w_kv_comb = _mm(kv_in, dkv_pre, "tn", F32, "kv_up_dw", tk=s)
    dcq, sums_q = _rms_mod_bwd(dqn, cq, q_norm_g, jnp.zeros((1, 384), F32), None, "q_norm_bwd")
    dckv, sums_kv = _rms_mod_bwd(dkv_in[:, :256], ckv, kv_norm_g, jnp.zeros((1, 256), F32), None, "kv_norm_bwd")
    delta_fox = _attn_delta(o_fox, do_fox_b, "fox_delta")
    dfq, dfk, dfv, dfr, dfc = _attn_bwd(fq, fk, fv, do_fox_b, lse_fox, delta_fox, f_row,
                                        FOX_SCALE, 1.0 / LOG2E, "fox_bwd", BF16)
    df_rows = dfr.reshape(N_HEADS, s // SEQ_LANES, SEQ_LANES)
    df_cols = dfc.reshape(N_HEADS, s // SEQ_LANES, SEQ_LANES)
    dz, db_f = _forget_bwd(df_rows, df_cols, z, bias_f, "forget_bwd")
    dfl = jnp.pad(dz.reshape(N_HEADS, s).T, ((0, 0), (0, 128 - N_HEADS)))

    dproj = jnp.concatenate([
        dgates, _unpad_heads(dfq, 64), _unpad_heads(dfk, 64), _unpad_heads(dfv, 64),
        dcq.astype(BF16), dckv.astype(BF16), dkv_in[:, 256:384].astype(BF16), dfl.astype(BF16)], axis=1)
    g_w_in_p = _mm(h1, dproj, "tn", F32, "proj_in_dw", tm=512, tn=640, tk=s)

    g_w_in = _w_in_unpadded(g_w_in_p)
    g_uq = g_w_uq_p.reshape(MLA_Q_RANK, N_HEADS, HEAD_PAD)[:, :, :96].transpose(1, 0, 2)
    g_uk = g_w_kv_comb[:256, :1024].reshape(256, N_HEADS, HEAD_PAD)[:, :, :64]
    g_uv = g_w_kv_comb[:256, 1024:].reshape(256, N_HEADS, HEAD_PAD)[:, :, :64]
    g_ukv = jnp.concatenate([g_uk, g_uv], axis=2).transpose(1, 0, 2)
    g_blocks.update({"w_in": _full_to_cols(g_w_in, 533), "w_uq": g_uq, "w_ukv": g_ukv})
    dh1, *early_recv = _mm(dproj, w_in_p, "nt", F32, "proj_in_dx", tn=512, tk=D_IN_P,
                           scatter=[g_blocks[k].astype(BF16) for k in EARLY])
    grad_x, sums_mix = _rms_mod_bwd(dh1, x, norm_mix_g, sc_m, dx2, "norm_mix_bwd")
    g_big, d_big, nm_big, nv_big = {}, {}, {}, {}
    for k, parts in zip(BIG, list(early_recv) + list(late_recv)):
        g_big[k], d_big[k], nm_big[k], nv_big[k] = [
            t[None] for t in _adamw(parts, big_w[k][0], big_m[k][0], big_v[k][0], "adamw_" + k)]

    dada = jnp.concatenate([sums_mix[0:1], sums_mix[1:2], sums_ffn[3:4], sums_ffn[0:1], sums_ffn[1:2], sums_final[2:3]],
                           axis=1)
    small_part = _pack_small({
        "b_ada": dada, "norm_mix_g": sums_mix[2:3], "q_norm_g": sums_q[2:3], "kv_norm_g": sums_kv[2:3],
        "b_forget": db_f[:, 0, 0], "norm_ffn_g": sums_ffn[2:3],
        "conv_b": jnp.concatenate([s_gate[3:4], s_val[3:4]], axis=1), "norm_final_g": sums_final[0:1],
        "loss": sums_final[1:2, 0:1]})
    (small_all,) = _all_gather([small_part], "gather_small")
    zero1 = jnp.zeros((1,), F32)
    small_w = {"b_ada": b_ada, "norm_mix_g": norm_mix_g, "q_norm_g": q_norm_g, "kv_norm_g": kv_norm_g,
               "b_forget": b_forget, "norm_ffn_g": norm_ffn_g, "conv_b": conv_b, "norm_final_g": norm_final_g,
               "loss": zero1}
    small_m = {"b_ada": m_b_ada, "norm_mix_g": m_norm_mix_g, "q_norm_g": m_q_norm_g, "kv_norm_g": m_kv_norm_g,
               "b_forget": m_b_forget, "norm_ffn_g": m_norm_ffn_g, "conv_b": m_conv_b,
               "norm_final_g": m_norm_final_g, "loss": zero1}
    small_v = {"b_ada": v_b_ada, "norm_mix_g": v_norm_mix_g, "q_norm_g": v_q_norm_g, "kv_norm_g": v_kv_norm_g,
               "b_forget": v_b_forget, "norm_ffn_g": v_norm_ffn_g, "conv_b": v_conv_b,
               "norm_final_g": v_norm_final_g, "loss": zero1}
    g_sm, d_sm, nm_sm, nv_sm = _adamw(small_all, _pack_small(small_w), _pack_small(small_m), _pack_small(small_v),
                                      "adamw_small")
    loss = g_sm[0, SMALL_OFF["loss"]]

    dada_all = small_all[:, 0, SMALL_OFF["b_ada"]:SMALL_OFF["b_ada"] + N_ADA * D_MODEL]
    dada_mine = lax.dynamic_slice(dada_all, (0, me * 768), (N_DEV, 768))
    g_ada_local = _ada_bwd(act_col, dada_mine, "ada_bwd")
    g_ada, d_ada, nm_ada, nv_ada = _adamw(g_ada_local[None], w_ada[0], m_w_ada[0], v_w_ada[0], "adamw_ada")

    def small_out(t, nme, shape):
        real = dict((n_, r_) for n_, r_, _ in SMALL)[nme]
        o = SMALL_OFF[nme]
        return t[0, o:o + real].reshape(shape)

    order = ["w_ada", "b_ada", "norm_mix_g", "w_in", "q_norm_g", "w_uq", "kv_norm_g", "w_ukv", "b_forget",
             "w_o_mla", "w_o_fox", "w_out", "norm_ffn_g", "w_up", "conv_w", "conv_b", "w_down", "norm_final_g"]
    small_shapes = {"b_ada": (1, 6144), "norm_mix_g": (1, 1024), "q_norm_g": (1, 384), "kv_norm_g": (1, 256),
                    "b_forget": (1, 8), "norm_ffn_g": (1, 1024), "conv_b": (1, 5632), "norm_final_g": (1024,)}

    def family(big, small, ada_t):
        out = []
        for nme in order:
            if nme == "w_ada":
                out.append(ada_t[None])
            elif nme in small_shapes:
                out.append(small_out(small, nme, small_shapes[nme]))
            else:
                out.append(big[nme])
        return out

    return (loss, grad_x[None], *family(g_big, g_sm, g_ada), *family(d_big, d_sm, d_ada),
            *family(nm_big, nm_sm, nm_ada), *family(nv_big, nv_sm, nv_ada))
```

```python
import math

import numpy as np
import jax
import jax.numpy as jnp
from jax import lax
from jax.experimental import pallas as pl
from jax.experimental.pallas import tpu as pltpu

F32 = jnp.float32
BF16 = jnp.bfloat16

N_DEV = 8
D_MODEL = 1024
N_HEADS = 8
HEAD_PAD = 128
MLA_Q_RANK = 384
MLA_KV_RANK = 256
MLA_NOPE = 64
MLA_ROPE = 32
MLA_V = 64
FOX_DIM = 64
D_FF = 2816
N_ADA = 6
EPS = 1e-6
ROPE_THETA = 10000.0
MLA_SCALE = 1.0 / math.sqrt(MLA_NOPE + MLA_ROPE)
FOX_SCALE = 1.0 / math.sqrt(FOX_DIM)
IN_SPLITS = (384, 256, 32, 512, 512, 512, 8, 1024, 1024)
D_IN = sum(IN_SPLITS)
IN_OFF = tuple(int(v) for v in np.cumsum((0,) + IN_SPLITS))
P_GM, P_GF, P_FQ, P_FK, P_FV, P_CQ, P_CKV, P_KR, P_FL, D_IN_P = 0, 1024, 2048, 2560, 3072, 3584, 3968, 4224, 4352, 4480

ADAM_LR, ADAM_B1, ADAM_B2, ADAM_EPS, ADAM_WD, ADAM_STEP = 0.001, 0.9, 0.999, 1e-08, 0.01, 10

VMEM_LIMIT_BYTES = 56 * 1024 * 1024
NEG_BIG = -1e30
ATT_T = 512
LOG2E = 1.4426950408889634
SUM_LANE = 64
ROW_T = 512
SEQ_LANES = 128
BF16_ROWS = 16
FWD_HEADS_PER_STEP = 4
BWD_HEADS_PER_STEP = 2


def _params(sem):
    return pltpu.CompilerParams(dimension_semantics=sem, vmem_limit_bytes=VMEM_LIMIT_BYTES)


def _tile(n, target, step=128):
    if n <= target:
        return n
    t = (target // step) * step
    while t >= step:
        if n % t == 0:
            return t
        t -= step
    return n


def _vec_spec(w, nargs):
    if nargs == 1:
        return pl.BlockSpec((1, w), lambda i: (0, 0))
    return pl.BlockSpec((1, w), lambda i, j: (0, 0))


def _comm_call(body, name, ins, out_shapes):
    n = len(ins)
    any_spec = pl.BlockSpec(memory_space=pl.ANY)
    return pl.pallas_call(
        body, name=name, out_shape=tuple(out_shapes),
        in_specs=[any_spec] * n, out_specs=tuple([any_spec] * n),
        scratch_shapes=[pltpu.SemaphoreType.DMA((n, 7)), pltpu.SemaphoreType.DMA((n, 7)),
                        pltpu.SemaphoreType.DMA((n,))],
    )(*ins)


def _all_gather(xs, name):
    n = len(xs)

    def body(*refs):
        x_refs, out_refs = refs[:n], refs[n:2 * n]
        send_sems, recv_sems, local_sems = refs[2 * n:]
        x_, y_, c_ = lax.axis_index("x"), lax.axis_index("y"), lax.axis_index("c")
        me, sibling = (x_, y_, c_), (x_, y_, 1 - c_)
        chips = [(1 - x_, y_), (x_, 1 - y_), (1 - x_, 1 - y_)]

        def slot(a, px, py, pc):
            return out_refs[a].at[4 * px + 2 * py + pc]

        def copy(a, k, block, to, src=None):
            return pltpu.make_async_remote_copy(
                src_ref=slot(a, *block) if src is None else src, dst_ref=slot(a, *block),
                send_sem=send_sems.at[a, k], recv_sem=recv_sems.at[a, k],
                device_id=to, device_id_type=pl.DeviceIdType.MESH)

        mine = [pltpu.make_async_copy(x_refs[a], slot(a, *me), local_sems.at[a]) for a in range(n)]
        for cp in mine:
            cp.start()
        first = []
        for a in range(n):
            first.append(copy(a, 0, me, sibling, src=x_refs[a]))
            first += [copy(a, 1 + j, me, (*chip, c_), src=x_refs[a]) for j, chip in enumerate(chips)]
        for cp in first:
            cp.start()
        passed = []
        for j, chip in enumerate(chips):
            for a in range(n):
                copy(a, 1 + j, (*chip, c_), me).wait_recv()
                passed.append(copy(a, 4 + j, (*chip, c_), sibling))
                passed[-1].start()
        for a in range(n):
            copy(a, 0, sibling, me).wait_recv()
            for j, chip in enumerate(chips):
                copy(a, 4 + j, (*chip, 1 - c_), me).wait_recv()
        for cp in first + passed:
            cp.wait_send()
        for cp in mine:
            cp.wait()

    return _comm_call(body, name, xs, [jax.ShapeDtypeStruct((N_DEV,) + x.shape, x.dtype) for x in xs])


def _direct_exchange(src_refs, out_refs, send_sems, recv_sems, local_sems, scatter):
    n = len(src_refs)
    x_, y_, c_ = lax.axis_index("x"), lax.axis_index("y"), lax.axis_index("c")
    me = 4 * x_ + 2 * y_ + c_

    def peer(k):
        return (x_ ^ ((k >> 2) & 1), y_ ^ ((k >> 1) & 1), c_ ^ (k & 1))

    def src(a, slot):
        return src_refs[a].at[slot] if scatter else src_refs[a]

    def copy(a, k, sending):
        px, py, pc = peer(k)
        theirs = 4 * px + 2 * py + pc
        return pltpu.make_async_remote_copy(
            src_ref=src(a, theirs if sending else me), dst_ref=out_refs[a].at[me if sending else theirs],
            send_sem=send_sems.at[a, k - 1], recv_sem=recv_sems.at[a, k - 1],
            device_id=(px, py, pc), device_id_type=pl.DeviceIdType.MESH)

    mine = [pltpu.make_async_copy(src(a, me), out_refs[a].at[me], local_sems.at[a]) for a in range(n)]
    sends = [copy(a, k, True) for a in range(n) for k in range(1, N_DEV)]

    def start():
        for cp in mine + sends:
            cp.start()

    def wait():
        for a in range(n):
            for k in range(1, N_DEV):
                copy(a, k, False).wait_recv()
        for cp in sends:
            cp.wait_send()
        for cp in mine:
            cp.wait()

    return start, wait


def _exchange_scratch(n):
    return [pltpu.SemaphoreType.DMA((n, 7)), pltpu.SemaphoreType.DMA((n, 7)), pltpu.SemaphoreType.DMA((n,))]


def _mm(a, b, mode, out_dtype, name, res=None, gvec=None, tm=1024, tn=512, tk=1024, rope=None, ones_lane=False,
        scatter=()):
    (k, m) = a.shape if mode == "tn" else a.shape[::-1]
    n = b.shape[0] if mode == "nt" else b.shape[1]
    tm, tn, tk = _tile(m, tm), _tile(n, tn), _tile(k, tk)
    nk = k // tk
    dims = {"nn": (((1,), (0,)), ((), ())), "nt": (((1,), (1,)), ((), ())), "tn": (((0,), (0,)), ((), ()))}[mode]
    has_res, has_g = res is not None, gvec is not None
    fused = has_res and has_g

    n_rope = 2 if rope is not None else 0
    nx = len(scatter)
    assert not nx or nk == 1, "the exchange rides only on a matmul with one K step"
    n_in = 2 + has_res + has_g + n_rope

    def body(*refs):
        acc_ref = refs[-1] if nk > 1 else None
        if nx:
            ins, x_src, outs, x_out, _, x_sems = _split_refs(refs, n_in, 1 + fused, 0, nx)
            refs = list(ins) + list(outs)
            i, j = pl.program_id(0), pl.program_id(1)
            x_start, x_wait = _direct_exchange(x_src, x_out, *x_sems, scatter=True)
            pl.when(jnp.logical_and(i == 0, j == 0))(x_start)
        else:
            refs = list(refs[:n_in + 1 + fused])
        a_ref, b_ref = refs[:2]
        res_ref = refs[2] if has_res else None
        g_ref = refs[2 + has_res] if has_g else None
        o_ref = refs[n_in]
        part = lax.dot_general(a_ref[...], b_ref[...], dims, preferred_element_type=F32)

        def finish(acc):
            if fused:
                refs[-1][...] = acc.astype(refs[-1].dtype)
            out = g_ref[...] * acc if has_g else acc
            if has_res:
                out = res_ref[...] + out
            if n_rope or ones_lane:
                one = (lax.broadcasted_iota(jnp.int32, (tm, HEAD_PAD), 1) == SUM_LANE).astype(F32)
                for hb in range(tn // HEAD_PAD):
                    lanes = slice(hb * HEAD_PAD, (hb + 1) * HEAD_PAD)
                    seg = out[:, lanes]
                    seg = _rope_block(seg, refs[n_in - 2][...], refs[n_in - 1][...]) if n_rope else seg + one
                    o_ref[:, lanes] = seg.astype(o_ref.dtype)
            else:
                o_ref[...] = out.astype(o_ref.dtype)

        if nk == 1:
            finish(part)
            if nx:
                pl.when(jnp.logical_and(i == m // tm - 1, j == n // tn - 1))(x_wait)
            return
        kk = pl.program_id(2)

        @pl.when(kk == 0)
        def _():
            acc_ref[...] = part

        @pl.when(kk > 0)
        def _():
            acc_ref[...] += part

        @pl.when(kk == nk - 1)
        def _():
            finish(acc_ref[...])

    if mode == "tn":
        a_spec = pl.BlockSpec((tk, tm), lambda i, j, kk: (kk, i))
    else:
        a_spec = pl.BlockSpec((tm, tk), lambda i, j, kk: (i, kk))
    if mode == "nt":
        b_spec = pl.BlockSpec((tn, tk), lambda i, j, kk: (j, kk))
    else:
        b_spec = pl.BlockSpec((tk, tn), lambda i, j, kk: (kk, j))
    o_spec = pl.BlockSpec((tm, tn), lambda i, j, kk: (i, j))
    in_specs, args = [a_spec, b_spec], [a, b]
    out_specs, out_shape = o_spec, jax.ShapeDtypeStruct((m, n), out_dtype)
    if has_res:
        in_specs.append(o_spec)
        args.append(res)
    if has_g:
        in_specs.append(pl.BlockSpec((1, tn), lambda i, j, kk: (0, j)))
        args.append(gvec)
    if n_rope:
        in_specs += [pl.BlockSpec((tm, HEAD_PAD), lambda i, j, kk: (i, 0))] * 2
        args += list(rope)
    if fused:
        out_specs = (o_spec, o_spec)
        out_shape = (out_shape, jax.ShapeDtypeStruct((m, n), BF16))
    scratch = [pltpu.VMEM((tm, tn), F32)] if nk > 1 else []
    if nx:
        any_spec = pl.BlockSpec(memory_space=pl.ANY)
        in_specs += [any_spec] * nx
        args += list(scatter)
        out_specs = tuple(out_specs if fused else (out_specs,)) + (any_spec,) * nx
        out_shape = tuple(out_shape if fused else (out_shape,)) + tuple(
            jax.ShapeDtypeStruct(g.shape, g.dtype) for g in scatter)
        scratch += _exchange_scratch(nx)
    return pl.pallas_call(
        body, name=name, grid=(m // tm, n // tn, nk),
        in_specs=in_specs, out_specs=out_specs, out_shape=out_shape,
        scratch_shapes=scratch,
        compiler_params=_params(("arbitrary",) * 3 if nx else ("parallel", "parallel", "arbitrary")),
    )(*args)


def _rms_mod(x, g, sc, sh, name):
    s, w = x.shape
    tm = _tile(s, ROW_T)

    def body(x_ref, g_ref, sc_ref, sh_ref, o_ref):
        xv = x_ref[...]
        r = lax.rsqrt(jnp.mean(xv * xv, axis=-1, keepdims=True) + EPS)
        o_ref[...] = ((xv * r * g_ref[...]) * (1.0 + sc_ref[...]) + sh_ref[...]).astype(o_ref.dtype)

    row = pl.BlockSpec((tm, w), lambda i: (i, 0))
    return pl.pallas_call(
        body, name=name, grid=(s // tm,),
        in_specs=[row, _vec_spec(w, 1), _vec_spec(w, 1), _vec_spec(w, 1)],
        out_specs=row, out_shape=jax.ShapeDtypeStruct((s, w), BF16),
        compiler_params=_params(("parallel",)),
    )(x, g, sc, sh)


def _latent_norm(lat, q_g, kv_g, name):
    s, w = lat.shape
    tm = _tile(s, ROW_T)
    nq_, nkv = MLA_Q_RANK, MLA_KV_RANK

    def norm(xv, gv):
        return xv * lax.rsqrt(jnp.mean(xv * xv, axis=-1, keepdims=True) + EPS) * gv

    def body(lat_ref, qg_ref, kg_ref, qn_ref, kv_ref):
        qn_ref[...] = norm(lat_ref[:, 0:nq_], qg_ref[...]).astype(BF16)
        kv_ref[:, 0:nkv] = norm(lat_ref[:, nq_:nq_ + nkv], kg_ref[...]).astype(BF16)
        kv_ref[:, nkv:nkv + HEAD_PAD] = lat_ref[:, nq_ + nkv:nq_ + nkv + HEAD_PAD].astype(BF16)

    out = lambda n: pl.BlockSpec((tm, n), lambda i: (i, 0))
    return pl.pallas_call(
        body, name=name, grid=(s // tm,),
        in_specs=[out(w), _vec_spec(nq_, 1), _vec_spec(nkv, 1)],
        out_specs=(out(nq_), out(nkv + HEAD_PAD)),
        out_shape=(jax.ShapeDtypeStruct((s, nq_), BF16), jax.ShapeDtypeStruct((s, nkv + HEAD_PAD), BF16)),
        compiler_params=_params(("parallel",)),
    )(lat, q_g, kv_g)


def _rms_mod_bwd(dh, x, g, sc, dres, name, branch=None):
    s, w = x.shape
    tm = _tile(s, ROW_T)
    has_res, has_br = dres is not None, branch is not None

    def body(*refs):
        dh_ref, x_ref, g_ref, sc_ref = refs[:4]
        rest = list(refs[4:])
        dres_ref = rest.pop(0) if has_res else None
        val_ref, bg_ref = (rest.pop(0), rest.pop(0)) if has_br else (None, None)
        dx_ref = rest.pop(0)
        db_ref = rest.pop(0) if has_br else None
        sums_ref = rest.pop(0)
        xv, dhv, gv = x_ref[...], dh_ref[...], g_ref[...]
        r = lax.rsqrt(jnp.mean(xv * xv, axis=-1, keepdims=True) + EPS)
        xhat = xv * r
        dxn = dhv * (1.0 + sc_ref[...])
        dxhat = dxn * gv
        dx = r * (dxhat - xhat * jnp.mean(dxhat * xhat, axis=-1, keepdims=True))
        if has_res:
            dx = dx + dres_ref[...]
        dx_ref[...] = dx

        @pl.when(pl.program_id(0) == 0)
        def _():
            sums_ref[...] = jnp.zeros_like(sums_ref)

        sums_ref[0:1, :] += jnp.sum(dhv, axis=0, keepdims=True)
        sums_ref[1:2, :] += jnp.sum(dhv * (xhat * gv), axis=0, keepdims=True)
        sums_ref[2:3, :] += jnp.sum(dxn * xhat, axis=0, keepdims=True)
        if has_br:
            db_ref[...] = (dx * bg_ref[...]).astype(db_ref.dtype)
            sums_ref[3:4, :] += jnp.sum(dx * val_ref[...].astype(F32), axis=0, keepdims=True)

    row = pl.BlockSpec((tm, w), lambda i: (i, 0))
    in_specs = [row, row, _vec_spec(w, 1), _vec_spec(w, 1)] + ([row] if has_res else [])
    args = [dh, x, g, sc] + ([dres] if has_res else [])
    out_specs, out_shape = [row], [jax.ShapeDtypeStruct((s, w), F32)]
    if has_br:
        in_specs += [row, _vec_spec(w, 1)]
        args += list(branch)
        out_specs.append(row)
        out_shape.append(jax.ShapeDtypeStruct((s, w), BF16))
    out_specs.append(pl.BlockSpec((8, w), lambda i: (0, 0)))
    out_shape.append(jax.ShapeDtypeStruct((8, w), F32))
    return pl.pallas_call(
        body, name=name, grid=(s // tm,),
        in_specs=in_specs, out_specs=tuple(out_specs), out_shape=tuple(out_shape),
        compiler_params=_params(("arbitrary",)),
    )(*args)


def _final_loss(x3, target, g, ffn, gvec, name):
    s, w = x3.shape
    tm = _tile(s, ROW_T)

    def body(x_ref, t_ref, g_ref, ffn_ref, bg_ref, dx_ref, db_ref, sums_ref):
        xv, gv = x_ref[...], g_ref[...]
        r = lax.rsqrt(jnp.mean(xv * xv, axis=-1, keepdims=True) + EPS)
        xhat = xv * r
        err = xhat * gv - t_ref[...]
        dy = err * (1.0 / w)
        dxhat = dy * gv
        dx = r * (dxhat - xhat * jnp.mean(dxhat * xhat, axis=-1, keepdims=True))
        dx_ref[...] = dx
        db_ref[...] = (dx * bg_ref[...]).astype(db_ref.dtype)

        @pl.when(pl.program_id(0) == 0)
        def _():
            sums_ref[...] = jnp.zeros_like(sums_ref)

        sums_ref[0:1, :] += jnp.sum(dy * xhat, axis=0, keepdims=True)
        sums_ref[1:2, :] += jnp.zeros((1, w), F32) + (0.5 / w) * jnp.sum(err * err)
        sums_ref[2:3, :] += jnp.sum(dx * ffn_ref[...].astype(F32), axis=0, keepdims=True)

    row = pl.BlockSpec((tm, w), lambda i: (i, 0))
    return pl.pallas_call(
        body, name=name, grid=(s // tm,),
        in_specs=[row, row, _vec_spec(w, 1), row, _vec_spec(w, 1)],
        out_specs=(row, row, pl.BlockSpec((8, w), lambda i: (0, 0))),
        out_shape=(jax.ShapeDtypeStruct((s, w), F32), jax.ShapeDtypeStruct((s, w), BF16),
                   jax.ShapeDtypeStruct((8, w), F32)),
        compiler_params=_params(("arbitrary",)),
    )(x3, target, g, ffn, gvec)


def _rope_block(seg, cmul, smul):
    lane = lax.broadcasted_iota(jnp.int32, seg.shape, 1)
    swapped = jnp.where(lane < MLA_NOPE + MLA_ROPE // 2,
                        pltpu.roll(seg, HEAD_PAD - MLA_ROPE // 2, 1), pltpu.roll(seg, MLA_ROPE // 2, 1))
    return seg * cmul + swapped * smul


def _rope(t, cmul, smul, name):
    s, w = t.shape
    tm = _tile(s, ROW_T)

    def body(t_ref, c_ref, s_ref, o_ref):
        cv, sv = c_ref[...], s_ref[...]
        for hb in range(w // HEAD_PAD):
            lanes = slice(hb * HEAD_PAD, (hb + 1) * HEAD_PAD)
            o_ref[:, lanes] = _rope_block(t_ref[:, lanes].astype(F32), cv, sv).astype(o_ref.dtype)

    row = pl.BlockSpec((tm, w), lambda i: (i, 0))
    tab = pl.BlockSpec((tm, HEAD_PAD), lambda i: (i, 0))
    return pl.pallas_call(
        body, name=name, grid=(s // tm,),
        in_specs=[row, tab, tab], out_specs=row, out_shape=jax.ShapeDtypeStruct((s, w), BF16),
        compiler_params=_params(("parallel",)),
    )(t, cmul, smul)


def _rope_bwd_kv(dk, dv, cmul, smul, name):
    s, w = dk.shape
    tm = _tile(s, ROW_T)

    def body(dk_ref, dv_ref, c_ref, s_ref, o_ref):
        cv, sv = c_ref[...], s_ref[...]
        for hb in range(N_HEADS):
            lo, hi = hb * HEAD_PAD, (hb + 1) * HEAD_PAD
            o_ref[:, lo:hi] = _rope_block(dk_ref[:, lo:hi].astype(F32), cv, sv).astype(o_ref.dtype)
        o_ref[:, w:2 * w] = dv_ref[...].astype(o_ref.dtype)

    row = pl.BlockSpec((tm, w), lambda i: (i, 0))
    tab = pl.BlockSpec((tm, HEAD_PAD), lambda i: (i, 0))
    return pl.pallas_call(
        body, name=name, grid=(s // tm,),
        in_specs=[row, row, tab, tab],
        out_specs=pl.BlockSpec((tm, 2 * w), lambda i: (i, 0)),
        out_shape=jax.ShapeDtypeStruct((s, 2 * w), BF16),
        compiler_params=_params(("parallel",)),
    )(dk, dv, cmul, smul)


def _lanes(col, width):
    if col.shape[1] == 1:
        col = jnp.broadcast_to(col, (col.shape[0], HEAD_PAD))
    return jnp.tile(col, (1, width // HEAD_PAD))


def _fold_lanes(a):
    out = a[:, 0:HEAD_PAD]
    for g in range(1, a.shape[1] // HEAD_PAD):
        out = out + a[:, g * HEAD_PAD:(g + 1) * HEAD_PAD]
    return out


def _as_row(rep):
    return rep.T[0:1, :]


def _causal(t, rows_are_queries):
    row = lax.broadcasted_iota(jnp.int32, (t, t), 0)
    col = lax.broadcasted_iota(jnp.int32, (t, t), 1)
    return row >= col if rows_are_queries else col >= row


def _split_refs(refs, n_in, n_out, n_scratch, n_x):
    pos = [n_in, n_x, n_out, n_x, n_scratch, 3 if n_x else 0]
    out, at = [], 0
    for cnt in pos:
        out.append(refs[at:at + cnt])
        at += cnt
    return out


def _first_last_step(n0, n1):
    i0, i1 = pl.program_id(0), pl.program_id(1)
    return jnp.logical_and(i0 == 0, i1 == 0), jnp.logical_and(i0 == n0 - 1, i1 == n1 - 1)


def _as_lanes(row):
    return jnp.broadcast_to(row, (HEAD_PAD, row.shape[1])).T


def _attn_fwd(q, k, v, frow, name, gather=()):
    s = q.shape[0]
    t = ATT_T
    nq = s // t
    use_f = frow is not None
    nx = len(gather)

    hpb = FWD_HEADS_PER_STEP

    def body(*refs):
        ins, x_src, outs, x_out, scr, x_sems = _split_refs(refs, 4 if use_f else 3, 3, 2, nx)
        if use_f:
            q_ref, k_ref, v_ref, fr_ref = ins
            fc_b = [_as_lanes(fr_ref[hh, pl.program_id(1)]) for hh in range(hpb)]
        else:
            q_ref, k_ref, v_ref = ins
        o_ref, ob_ref, lse_ref = outs
        m_s, acc_s = scr
        if nx:
            first, last = _first_last_step(N_HEADS // hpb, nq)
            x_start, x_wait = _direct_exchange(x_src, x_out, *x_sems, scatter=False)
            pl.when(first)(x_start)
        qi = pl.program_id(1)
        m_s[...] = jnp.full(m_s.shape, NEG_BIG, F32)
        acc_s[...] = jnp.zeros(acc_s.shape, F32)

        def step(j, masked):
            off = pl.multiple_of(j * t, t)
            for hh in range(hpb):
                lanes = slice(hh * HEAD_PAD, (hh + 1) * HEAD_PAD)
                kv = k_ref[pl.ds(off, t), lanes]
                vv = v_ref[pl.ds(off, t), lanes]
                sc = lax.dot_general(q_ref[:, lanes], kv, (((1,), (1,)), ((), ())), preferred_element_type=F32)
                if use_f:
                    sc = sc + (_lanes(fc_b[hh], t) - fr_ref[hh, j])
                if masked:
                    sc = jnp.where(_causal(t, True), sc, NEG_BIG)
                m_prev = m_s[hh]
                m_new = jnp.maximum(m_prev, jnp.max(sc, axis=-1, keepdims=True))
                p = jnp.exp2(sc - _lanes(m_new, t))
                acc_s[hh] = jnp.exp2(m_prev - m_new) * acc_s[hh] + jnp.dot(p.astype(BF16), vv,
                                                                           preferred_element_type=F32)
                m_s[hh] = m_new

        def loop_body(j, carry):
            step(j, False)
            return carry

        lax.fori_loop(0, qi, loop_body, 0)
        step(qi, True)
        for hh in range(hpb):
            lanes = slice(hh * HEAD_PAD, (hh + 1) * HEAD_PAD)
            acc = acc_s[hh]
            lane = lax.broadcasted_iota(jnp.int32, acc.shape, 1)
            denom = jnp.sum(jnp.where(lane == SUM_LANE, acc, 0.0), axis=-1, keepdims=True)
            o = acc * (1.0 / denom)
            o_ref[:, lanes] = o
            ob_ref[:, lanes] = o.astype(BF16)
            lse_ref[hh, 0] = _as_row(m_s[hh] + jnp.log(denom) * LOG2E)
        if nx:
            pl.when(last)(x_wait)

    w = hpb * HEAD_PAD
    qspec = pl.BlockSpec((t, w), lambda h, i: (i, h))
    kspec = pl.BlockSpec((s, w), lambda h, i: (0, h))
    any_spec = pl.BlockSpec(memory_space=pl.ANY)
    in_specs, args = [qspec, kspec, kspec], [q, k, v]
    if use_f:
        in_specs += [pl.BlockSpec((hpb, nq, 1, t), lambda h, i: (h, 0, 0, 0))]
        args += [frow]
    out_specs = [qspec, qspec, pl.BlockSpec((hpb, 1, 1, t), lambda h, i: (h, i, 0, 0))]
    out_shape = [jax.ShapeDtypeStruct((s, N_HEADS * HEAD_PAD), F32), jax.ShapeDtypeStruct((s, N_HEADS * HEAD_PAD), BF16),
                 jax.ShapeDtypeStruct((N_HEADS, nq, 1, t), F32)]
    scratch = [pltpu.VMEM((hpb, t, HEAD_PAD), F32), pltpu.VMEM((hpb, t, HEAD_PAD), F32)]
    if nx:
        in_specs += [any_spec] * nx
        args += list(gather)
        out_specs += [any_spec] * nx
        out_shape += [jax.ShapeDtypeStruct((N_DEV,) + g.shape, g.dtype) for g in gather]
        scratch += _exchange_scratch(nx)
    return pl.pallas_call(
        body, name=name, grid=(N_HEADS // hpb, nq),
        in_specs=in_specs, out_specs=tuple(out_specs), out_shape=tuple(out_shape),
        scratch_shapes=scratch,
        compiler_params=_params(("arbitrary", "arbitrary") if nx else ("parallel", "arbitrary")),
    )(*args)


def _attn_delta(o, do, name):
    s, w = o.shape
    t = ATT_T

    def body(o_ref, do_ref, d_ref):
        for hb in range(N_HEADS):
            lo, hi = hb * HEAD_PAD, (hb + 1) * HEAD_PAD
            prod = o_ref[:, lo:hi] * do_ref[:, lo:hi].astype(F32)
            d_ref[hb, 0] = jnp.sum(prod.T, axis=0, keepdims=True)

    row = pl.BlockSpec((t, w), lambda i: (i, 0))
    return pl.pallas_call(
        body, name=name, grid=(s // t,),
        in_specs=[row, row],
        out_specs=pl.BlockSpec((N_HEADS, 1, 1, t), lambda i: (0, i, 0, 0)),
        out_shape=jax.ShapeDtypeStruct((N_HEADS, s // t, 1, t), F32),
        compiler_params=_params(("parallel",)),
    )(o, do)


def _attn_bwd(q, k, v, do, lse_row, delta_row, frow, scale_q, scale_k, name, out_dtype, scatter=()):
    s = q.shape[0]
    t = ATT_T
    nq = s // t
    use_f = frow is not None
    nx = len(scatter)
    hpb = BWD_HEADS_PER_STEP

    def body(*refs):
        ins, x_src, outs, x_out, scr, x_sems = _split_refs(refs, 7 if use_f else 6, 5 if use_f else 3,
                                                           5 if use_f else 3, nx)
        if use_f:
            q_ref, k_ref, v_ref, do_ref, lse_ref, dl_ref, fr_ref = ins
            dq_ref, dk_ref, dv_ref, dr_ref, df_ref = outs
            dq_s, dk_s, dv_s, dr_s, df_s = scr
            fc_b = [_as_lanes(fr_ref[hh, pl.program_id(1)]) for hh in range(hpb)]
        else:
            q_ref, k_ref, v_ref, do_ref, lse_ref, dl_ref = ins
            dq_ref, dk_ref, dv_ref = outs
            dq_s, dk_s, dv_s = scr
        if nx:
            first, last = _first_last_step(N_HEADS // hpb, nq)
            x_start, x_wait = _direct_exchange(x_src, x_out, *x_sems, scatter=True)
            pl.when(first)(x_start)
        kj = pl.program_id(1)

        @pl.when(kj == 0)
        def _():
            dq_s[...] = jnp.zeros(dq_s.shape, F32)
            if use_f:
                dr_s[...] = jnp.zeros(dr_s.shape, F32)

        dk_s[...] = jnp.zeros(dk_s.shape, F32)
        dv_s[...] = jnp.zeros(dv_s.shape, F32)
        if use_f:
            df_s[...] = jnp.zeros(df_s.shape, F32)
        k_tiles = [k_ref[:, hh * HEAD_PAD:(hh + 1) * HEAD_PAD] for hh in range(hpb)]
        v_tiles = [v_ref[:, hh * HEAD_PAD:(hh + 1) * HEAD_PAD] for hh in range(hpb)]

        def step(i, masked):
            off = pl.multiple_of(i * t, t)
            for hh in range(hpb):
                lanes = slice(hh * HEAD_PAD, (hh + 1) * HEAD_PAD)
                kv, vv = k_tiles[hh], v_tiles[hh]
                qv = q_ref[pl.ds(off, t), lanes]
                dov = do_ref[pl.ds(off, t), lanes]
                st = lax.dot_general(kv, qv, (((1,), (1,)), ((), ())), preferred_element_type=F32)
                if use_f:
                    st = st + (fr_ref[hh, i] - _lanes(fc_b[hh], t))
                if masked:
                    st = jnp.where(_causal(t, False), st, NEG_BIG)
                pt = jnp.exp2(st - lse_ref[hh, i])
                dv_s[hh] += jnp.dot(pt.astype(BF16), dov, preferred_element_type=F32)
                dpt = lax.dot_general(vv, dov, (((1,), (1,)), ((), ())), preferred_element_type=F32)
                dst = pt * (dpt - dl_ref[hh, i])
                dsb = dst.astype(BF16)
                dk_s[hh] += jnp.dot(dsb, qv, preferred_element_type=F32)
                dq_s[hh, pl.ds(off, t), :] += lax.dot_general(dsb, kv, (((0,), (0,)), ((), ())),
                                                              preferred_element_type=F32)
                if use_f:
                    df_s[hh] -= _fold_lanes(dst)
                    dr_s[hh, i] += jnp.sum(dst, axis=0, keepdims=True)

        step(kj, True)

        def loop_body(i, carry):
            step(i, False)
            return carry

        lax.fori_loop(kj + 1, nq, loop_body, 0)
        for hh in range(hpb):
            lanes = slice(hh * HEAD_PAD, (hh + 1) * HEAD_PAD)
            dk_ref[:, lanes] = (dk_s[hh] * scale_k).astype(dk_ref.dtype)
            dv_ref[:, lanes] = dv_s[hh].astype(dv_ref.dtype)
            if use_f:
                df_ref[hh, 0] = jnp.sum(df_s[hh].T, axis=0, keepdims=True)

        @pl.when(kj == nq - 1)
        def _():
            for hh in range(hpb):
                dq_ref[:, hh * HEAD_PAD:(hh + 1) * HEAD_PAD] = (dq_s[hh] * scale_q).astype(dq_ref.dtype)
            if use_f:
                dr_ref[...] = dr_s[...]

        if nx:
            pl.when(last)(x_wait)

    w = hpb * HEAD_PAD
    kspec = pl.BlockSpec((t, w), lambda h, j: (j, h))
    qspec = pl.BlockSpec((s, w), lambda h, j: (0, h))
    rowspec = pl.BlockSpec((hpb, nq, 1, t), lambda h, j: (h, 0, 0, 0))
    any_spec = pl.BlockSpec(memory_space=pl.ANY)
    in_specs, args = [qspec, kspec, kspec, qspec, rowspec, rowspec], [q, k, v, do, lse_row, delta_row]
    full = jax.ShapeDtypeStruct((s, N_HEADS * HEAD_PAD), out_dtype)
    out_specs, out_shape = [qspec, kspec, kspec], [full, full, full]
    scratch = [pltpu.VMEM((hpb, s, HEAD_PAD), F32), pltpu.VMEM((hpb, t, HEAD_PAD), F32),
               pltpu.VMEM((hpb, t, HEAD_PAD), F32)]
    if use_f:
        in_specs += [rowspec]
        args += [frow]
        out_specs += [rowspec, pl.BlockSpec((hpb, 1, 1, t), lambda h, j: (h, j, 0, 0))]
        out_shape += [jax.ShapeDtypeStruct((N_HEADS, nq, 1, t), F32)] * 2
        scratch += [pltpu.VMEM((hpb, nq, 1, t), F32), pltpu.VMEM((hpb, t, HEAD_PAD), F32)]
    if nx:
        in_specs += [any_spec] * nx
        args += list(scatter)
        out_specs += [any_spec] * nx
        out_shape += [jax.ShapeDtypeStruct(g.shape, g.dtype) for g in scatter]
        scratch += _exchange_scratch(nx)
    return pl.pallas_call(
        body, name=name, grid=(N_HEADS // hpb, nq),
        in_specs=in_specs, out_specs=tuple(out_specs), out_shape=tuple(out_shape),
        scratch_shapes=scratch,
        compiler_params=_params(("arbitrary", "arbitrary") if nx else ("parallel", "arbitrary")),
    )(*args)


def _gate_fwd(pm, pf, gates, name):
    s, w = pm.shape
    tm = _tile(s, ROW_T)

    def body(pm_ref, pf_ref, g_ref, y_ref):
        y = (jax.nn.sigmoid(g_ref[:, 0:w].astype(F32)) * pm_ref[...].astype(F32)
             + jax.nn.sigmoid(g_ref[:, w:2 * w].astype(F32)) * pf_ref[...].astype(F32))
        y_ref[...] = y.astype(y_ref.dtype)

    row = pl.BlockSpec((tm, w), lambda i: (i, 0))
    return pl.pallas_call(
        body, name=name, grid=(s // tm,),
        in_specs=[row, row, pl.BlockSpec((tm, 2 * w), lambda i: (i, 0))],
        out_specs=row, out_shape=jax.ShapeDtypeStruct((s, w), BF16),
        compiler_params=_params(("parallel",)),
    )(pm, pf, gates)


def _gate_bwd(dy, pm, pf, gates, name):
    s, w = pm.shape
    tm = _tile(s, ROW_T)

    def body(dy_ref, pm_ref, pf_ref, g_ref, dpm_ref, dpf_ref, dg_ref):
        dyv = dy_ref[...].astype(F32)
        sm, sf = jax.nn.sigmoid(g_ref[:, 0:w].astype(F32)), jax.nn.sigmoid(g_ref[:, w:2 * w].astype(F32))
        dpm_ref[...] = (dyv * sm).astype(BF16)
        dpf_ref[...] = (dyv * sf).astype(BF16)
        dg_ref[:, 0:w] = (dyv * pm_ref[...].astype(F32) * (sm * (1.0 - sm))).astype(BF16)
        dg_ref[:, w:2 * w] = (dyv * pf_ref[...].astype(F32) * (sf * (1.0 - sf))).astype(BF16)

    row = pl.BlockSpec((tm, w), lambda i: (i, 0))
    wide = pl.BlockSpec((tm, 2 * w), lambda i: (i, 0))
    out = jax.ShapeDtypeStruct((s, w), BF16)
    return pl.pallas_call(
        body, name=name, grid=(s // tm,),
        in_specs=[row, row, row, wide],
        out_specs=(row, row, wide), out_shape=(out, out, jax.ShapeDtypeStruct((s, 2 * w), BF16)),
        compiler_params=_params(("parallel",)),
    )(dy, pm, pf, gates)


CONV_TN = D_FF // 2
CONV_TM = 256
CONV_T_TM = 512
HALO = BF16_ROWS


def _shift_down(u, prev, n):
    rolled = pltpu.roll(u, n, 0)
    prev_rolled = pltpu.roll(prev, n, 0)
    top = jnp.concatenate([prev_rolled, rolled[HALO:]], axis=0)
    row = lax.broadcasted_iota(jnp.int32, u.shape, 0)
    return jnp.where(row < n, top, rolled)


def _conv_tile(u, prev, w_ref, b_ref):
    um1 = _shift_down(u, prev, 1)
    um2 = _shift_down(u, prev, 2)
    uc = b_ref[...] + w_ref[0:1, :] * um2 + w_ref[1:2, :] * um1 + w_ref[2:3, :] * u
    return uc, um1, um2


def _conv_specs(tm, tn, ncol_off):
    blk = lambda off: pl.BlockSpec((tm, tn), lambda j, i: (i, j + off))
    halo = lambda off: pl.BlockSpec((HALO, tn), lambda j, i: (jnp.maximum(i * (tm // HALO) - 1, 0), j + off))
    wsp = lambda off: pl.BlockSpec((3, tn), lambda j, i: (0, j + off))
    bsp = lambda off: pl.BlockSpec((1, tn), lambda j, i: (0, j + off))
    return blk, halo, wsp, bsp


def _convglu_fwd(u, conv_w, conv_b, name):
    s = u.shape[0]
    tm, tn = _tile(s, CONV_TM), CONV_TN
    nj = D_FF // tn
    blk, halo, wsp, bsp = _conv_specs(tm, tn, nj)

    def body(ug_ref, pg_ref, uv_ref, pv_ref, wg_ref, wv_ref, bg_ref, bv_ref, a_ref):
        live = (pl.program_id(1) > 0).astype(F32)
        gate, _, _ = _conv_tile(ug_ref[...].astype(F32), pg_ref[...].astype(F32) * live, wg_ref, bg_ref)
        val, _, _ = _conv_tile(uv_ref[...].astype(F32), pv_ref[...].astype(F32) * live, wv_ref, bv_ref)
        a_ref[...] = (gate * jax.nn.sigmoid(gate) * val).astype(a_ref.dtype)

    return pl.pallas_call(
        body, name=name, grid=(nj, s // tm),
        in_specs=[blk(0), halo(0), blk(nj), halo(nj), wsp(0), wsp(nj), bsp(0), bsp(nj)],
        out_specs=blk(0), out_shape=jax.ShapeDtypeStruct((s, D_FF), BF16),
        compiler_params=_params(("parallel", "arbitrary")),
    )(u, u, u, u, conv_w, conv_w, conv_b, conv_b)


def _convglu_bwd(da, u, conv_w, conv_b, name):
    s = u.shape[0]
    tm, tn = _tile(s, CONV_TM), CONV_TN
    nj = D_FF // tn
    blk, halo, wsp, bsp = _conv_specs(tm, tn, nj)

    def body(da_ref, ug_ref, pg_ref, uv_ref, pv_ref, wg_ref, wv_ref, bg_ref, bv_ref,
             dg_ref, dv_ref, sg_ref, sv_ref):
        live = (pl.program_id(1) > 0).astype(F32)
        ug, uv = ug_ref[...].astype(F32), uv_ref[...].astype(F32)
        gate, ug1, ug2 = _conv_tile(ug, pg_ref[...].astype(F32) * live, wg_ref, bg_ref)
        val, uv1, uv2 = _conv_tile(uv, pv_ref[...].astype(F32) * live, wv_ref, bv_ref)
        dav = da_ref[...].astype(F32)
        sig = jax.nn.sigmoid(gate)
        dgate = dav * val * (sig * (1.0 + gate * (1.0 - sig)))
        dval = dav * (gate * sig)
        dg_ref[...] = dgate.astype(dg_ref.dtype)
        dv_ref[...] = dval.astype(dv_ref.dtype)

        @pl.when(pl.program_id(1) == 0)
        def _():
            sg_ref[...] = jnp.zeros_like(sg_ref)
            sv_ref[...] = jnp.zeros_like(sv_ref)

        for s_ref, d, taps in ((sg_ref, dgate, (ug2, ug1, ug)), (sv_ref, dval, (uv2, uv1, uv))):
            for r, tap in enumerate(taps):
                s_ref[r:r + 1, :] += jnp.sum(d * tap, axis=0, keepdims=True)
            s_ref[3:4, :] += jnp.sum(d, axis=0, keepdims=True)

    sums = lambda off: pl.BlockSpec((8, tn), lambda j, i: (0, j + off))
    return pl.pallas_call(
        body, name=name, grid=(nj, s // tm),
        in_specs=[blk(0), blk(0), halo(0), blk(nj), halo(nj), wsp(0), wsp(nj), bsp(0), bsp(nj)],
        out_specs=(blk(0), blk(0), sums(0), sums(0)),
        out_shape=(jax.ShapeDtypeStruct((s, D_FF), BF16), jax.ShapeDtypeStruct((s, D_FF), BF16),
                   jax.ShapeDtypeStruct((8, D_FF), F32), jax.ShapeDtypeStruct((8, D_FF), F32)),
        compiler_params=_params(("parallel", "arbitrary")),
    )(da, u, u, u, u, conv_w, conv_w, conv_b, conv_b)


def _conv_transpose(d, conv_w, name, col0, into=None):
    s, w = d.shape
    tm, tn = _tile(s, CONV_T_TM), CONV_TN
    last = s // tm - 1
    jo = col0 // tn

    def body(d_ref, nx_ref, w_ref, *rest):
        o_ref = rest[-1]
        dv = d_ref[...].astype(F32)
        nxt = nx_ref[...].astype(F32) * (pl.program_id(1) < last).astype(F32)
        row = lax.broadcasted_iota(jnp.int32, dv.shape, 0)

        def shift_up(n):
            rolled = pltpu.roll(dv, tm - n, 0)
            nxt_rolled = pltpu.roll(nxt, HALO - n, 0)
            bottom = jnp.concatenate([rolled[:tm - HALO], nxt_rolled], axis=0)
            return jnp.where(row >= tm - n, bottom, rolled)

        out = w_ref[2:3, :] * dv + w_ref[1:2, :] * shift_up(1) + w_ref[0:1, :] * shift_up(2)
        o_ref[...] = out.astype(o_ref.dtype)

    blk = pl.BlockSpec((tm, tn), lambda j, i: (i, j))
    nxt_spec = pl.BlockSpec((HALO, tn), lambda j, i: (jnp.minimum((i + 1) * (tm // HALO), s // HALO - 1), j))
    in_specs = [blk, nxt_spec, pl.BlockSpec((3, tn), lambda j, i: (0, j + jo))]
    args = [d, d, conv_w]
    if into is not None:
        in_specs.append(pl.BlockSpec(memory_space=pl.ANY))
        args.append(into)
    return pl.pallas_call(
        body, name=name, grid=(w // tn, s // tm),
        in_specs=in_specs,
        out_specs=pl.BlockSpec((tm, tn), lambda j, i: (i, j + jo)),
        out_shape=jax.ShapeDtypeStruct((s, 2 * D_FF), BF16),
        input_output_aliases={3: 0} if into is not None else {},
        compiler_params=_params(("parallel", "arbitrary")),
    )(*args)


def _split3(a):
    a1 = a.astype(BF16)
    r1 = a - a1.astype(F32)
    a2 = r1.astype(BF16)
    a3 = (r1 - a2.astype(F32)).astype(BF16)
    return a1, a2, a3


def _ones_dot_right(a, mat):
    return sum(jnp.dot(p, mat, preferred_element_type=F32) for p in _split3(a))


def _ones_dot_left(mat, a):
    return sum(jnp.dot(mat, p, preferred_element_type=F32) for p in _split3(a))


def _tri(n, cmp):
    r = lax.broadcasted_iota(jnp.int32, (n, n), 0)
    c = lax.broadcasted_iota(jnp.int32, (n, n), 1)
    return cmp(r, c).astype(BF16)


def _forget_fwd(z, bias, name):
    nh, nr, nl = z.shape

    def body(z_ref, b_ref, f_ref):
        within = _tri(nl, lambda r, c: r <= c)
        before = _tri(nr, lambda r, c: c < r)
        for h in range(nh):
            x = z_ref[h] + b_ref[h]
            lf = jnp.minimum(x, 0.0) - jnp.log(1.0 + jnp.exp(-jnp.abs(x)))
            pre = _ones_dot_right(lf, within)
            tot = jnp.zeros((nr, nl), F32) + jnp.sum(lf, axis=1, keepdims=True)
            f_ref[h] = pre + _ones_dot_left(before, tot)

    return pl.pallas_call(
        body, name=name, out_shape=jax.ShapeDtypeStruct(z.shape, F32),
        compiler_params=pltpu.CompilerParams(vmem_limit_bytes=VMEM_LIMIT_BYTES),
    )(z, bias)


def _forget_bwd(df_rows, df_cols, z, bias, name):
    nh, nr, nl = z.shape

    def body(dfr_ref, dfc_ref, z_ref, b_ref, dz_ref, db_ref):
        within = _tri(nl, lambda r, c: r >= c)
        after = _tri(nr, lambda r, c: c > r)
        for h in range(nh):
            g = dfr_ref[h] + dfc_ref[h]
            suf = _ones_dot_right(g, within)
            tot = jnp.zeros((nr, nl), F32) + jnp.sum(g, axis=1, keepdims=True)
            dlf = suf + _ones_dot_left(after, tot)
            dz = dlf * jax.nn.sigmoid(-(z_ref[h] + b_ref[h]))
            dz_ref[h] = dz
            db_ref[h] = jnp.zeros((1, nl), F32) + jnp.sum(dz)

    return pl.pallas_call(
        body, name=name,
        out_shape=(jax.ShapeDtypeStruct(z.shape, F32), jax.ShapeDtypeStruct(bias.shape, F32)),
        compiler_params=pltpu.CompilerParams(vmem_limit_bytes=VMEM_LIMIT_BYTES),
    )(df_rows, df_cols, z, bias)


def _ada_fwd(c_col, w, b, name):
    kdim, n = w.shape

    def body(c_ref, w_ref, b_ref, ada_ref, act_ref):
        wv = w_ref[...]
        for e in range(N_DEV):
            cv = c_ref[e]
            act = cv * jax.nn.sigmoid(cv)
            act_ref[e] = act
            ada_ref[e:e + 1, :] = jnp.sum(act * wv, axis=0, keepdims=True) + b_ref[...]

    return pl.pallas_call(
        body, name=name,
        out_shape=(jax.ShapeDtypeStruct((N_DEV, n), F32), jax.ShapeDtypeStruct((N_DEV, kdim, 1), F32)),
        compiler_params=pltpu.CompilerParams(vmem_limit_bytes=VMEM_LIMIT_BYTES),
    )(c_col, w, b)


def _ada_bwd(act_col, dada, name):
    kdim = act_col.shape[1]
    n = dada.shape[1]

    def body(act_ref, d_ref, g_ref):
        acc = act_ref[0] * d_ref[0:1, :]
        for e in range(1, N_DEV):
            acc = acc + act_ref[e] * d_ref[e:e + 1, :]
        g_ref[...] = acc

    return pl.pallas_call(
        body, name=name, out_shape=jax.ShapeDtypeStruct((kdim, n), F32),
        compiler_params=pltpu.CompilerParams(vmem_limit_bytes=VMEM_LIMIT_BYTES),
    )(act_col, dada)


def _adamw(parts, w, m, v, name, tr=128):
    npart, r, c = parts.shape
    tr = _tile(r, tr, step=BF16_ROWS) if r % BF16_ROWS == 0 else r

    def body(p_ref, w_ref, m_ref, v_ref, g_ref, d_ref, nm_ref, nv_ref):
        g = p_ref[0].astype(F32)
        for e in range(1, npart):
            g = g + p_ref[e].astype(F32)
        nm = ADAM_B1 * m_ref[...] + (1.0 - ADAM_B1) * g
        nv = ADAM_B2 * v_ref[...] + (1.0 - ADAM_B2) * (g * g)
        m_hat = nm / (1.0 - ADAM_B1 ** ADAM_STEP)
        v_hat = nv / (1.0 - ADAM_B2 ** ADAM_STEP)
        g_ref[...] = g
        d_ref[...] = -ADAM_LR * (m_hat / (jnp.sqrt(v_hat) + ADAM_EPS) + ADAM_WD * w_ref[...])
        nm_ref[...] = nm
        nv_ref[...] = nv

    row = pl.BlockSpec((tr, c), lambda i: (i, 0))
    out = jax.ShapeDtypeStruct((r, c), F32)
    return pl.pallas_call(
        body, name=name, grid=(r // tr,),
        in_specs=[pl.BlockSpec((npart, tr, c), lambda i: (0, i, 0)), row, row, row],
        out_specs=(row, row, row, row), out_shape=(out, out, out, out),
        compiler_params=_params(("parallel",)),
    )(parts, w, m, v)


EARLY = ("w_in", "w_uq", "w_ukv")
LATE = ("w_o_mla", "w_o_fox", "w_out", "w_up", "conv_w", "w_down")
BIG = EARLY + LATE


def _cols_to_full(stack):
    n, r, c = stack.shape
    return stack.transpose(1, 0, 2).reshape(r, n * c)


def _full_to_cols(full, c):
    r = full.shape[0]
    return full.reshape(r, N_DEV, c).transpose(1, 0, 2)


def _pad_heads(a, width):
    r = a.shape[0]
    a = a.reshape(r, N_HEADS, width)
    return jnp.pad(a, ((0, 0), (0, 0), (0, HEAD_PAD - width))).reshape(r, N_HEADS * HEAD_PAD)


def _unpad_heads(a, width):
    s = a.shape[0]
    return a.reshape(s, N_HEADS, HEAD_PAD)[:, :, :width].reshape(s, N_HEADS * width)


def _w_in_padded(w_in):
    seg = [w_in[:, IN_OFF[i]:IN_OFF[i + 1]] for i in range(9)]
    cq, ckv, kr, fq, fk, fv, fl, gm, gf = seg
    padc = lambda a, n: jnp.pad(a, ((0, 0), (0, n - a.shape[1])))
    return jnp.concatenate([gm, gf, fq, fk, fv, cq, ckv, padc(kr, 128), padc(fl, 128)], axis=1)


def _w_in_unpadded(g):
    return jnp.concatenate([
        g[:, P_CQ:P_CQ + 384], g[:, P_CKV:P_CKV + 256], g[:, P_KR:P_KR + 32], g[:, P_FQ:P_FQ + 512],
        g[:, P_FK:P_FK + 512], g[:, P_FV:P_FV + 512], g[:, P_FL:P_FL + 8], g[:, P_GM:P_GM + 1024],
        g[:, P_GF:P_GF + 1024]], axis=1)


SMALL = (("b_ada", 6144, 6144), ("norm_mix_g", 1024, 1024), ("q_norm_g", 384, 384), ("kv_norm_g", 256, 256),
         ("b_forget", 8, 128), ("norm_ffn_g", 1024, 1024), ("conv_b", 5632, 5632), ("norm_final_g", 1024, 1024),
         ("loss", 1, 128))
SMALL_OFF = {}
_o = 0
for _n, _real, _padded in SMALL:
    SMALL_OFF[_n] = _o
    _o += _padded
SMALL_W = _o


def _pack_small(vals):
    parts = []
    for nme, real, padded in SMALL:
        a = vals[nme].reshape(1, real).astype(F32)
        parts.append(jnp.pad(a, ((0, 0), (0, padded - real))))
    return jnp.concatenate(parts, axis=1)


def kernel(x, c, positions, w_ada, b_ada, norm_mix_g, w_in, q_norm_g, w_uq, kv_norm_g, w_ukv, b_forget, w_o_mla, w_o_fox, w_out, norm_ffn_g, w_up, conv_w, conv_b, w_down, norm_final_g, loss_target, m_w_ada, m_b_ada, m_norm_mix_g, m_w_in, m_q_norm_g, m_w_uq, m_kv_norm_g, m_w_ukv, m_b_forget, m_w_o_mla, m_w_o_fox, m_w_out, m_norm_ffn_g, m_w_up, m_conv_w, m_conv_b, m_w_down, m_norm_final_g, v_w_ada, v_b_ada, v_norm_mix_g, v_w_in, v_q_norm_g, v_w_uq, v_kv_norm_g, v_w_ukv, v_b_forget, v_w_o_mla, v_w_o_fox, v_w_out, v_norm_ffn_g, v_w_up, v_conv_w, v_conv_b, v_w_down, v_norm_final_g):
    me = 4 * lax.axis_index("x") + 2 * lax.axis_index("y") + lax.axis_index("c")
    x = x[0]
    target = loss_target[0]
    s = x.shape[0]
    nblk = s // ATT_T
    big_w = {"w_in": w_in, "w_uq": w_uq, "w_ukv": w_ukv, "w_o_mla": w_o_mla, "w_o_fox": w_o_fox,
             "w_out": w_out, "w_up": w_up, "conv_w": conv_w, "w_down": w_down}
    big_m = {"w_in": m_w_in, "w_uq": m_w_uq, "w_ukv": m_w_ukv, "w_o_mla": m_w_o_mla, "w_o_fox": m_w_o_fox,
             "w_out": m_w_out, "w_up": m_w_up, "conv_w": m_conv_w, "w_down": m_w_down}
    big_v = {"w_in": v_w_in, "w_uq": v_w_uq, "w_ukv": v_w_ukv, "w_o_mla": v_w_o_mla, "w_o_fox": v_w_o_fox,
             "w_out": v_w_out, "w_up": v_w_up, "conv_w": v_conv_w, "w_down": v_w_down}

    shard = lambda k: big_w[k][0] if k == "conv_w" else big_w[k][0].astype(BF16)
    c_all, *early = _all_gather([c] + [shard(k) for k in EARLY], "gather_weights")
    st = dict(zip(EARLY, early))
    w_in_p = _w_in_padded(_cols_to_full(st["w_in"]))
    uq = st["w_uq"]
    w_uq_p = jnp.pad(uq, ((0, 0), (0, 0), (0, HEAD_PAD - 96))).transpose(1, 0, 2).reshape(MLA_Q_RANK, 1024)
    ukv = st["w_ukv"]
    zeros64 = jnp.zeros((N_HEADS, MLA_KV_RANK, 64), BF16)
    w_uk_p = jnp.concatenate([ukv[:, :, :64], zeros64], axis=2).transpose(1, 0, 2).reshape(MLA_KV_RANK, 1024)
    w_uv_p = jnp.concatenate([ukv[:, :, 64:], zeros64], axis=2).transpose(1, 0, 2).reshape(MLA_KV_RANK, 1024)
    place = np.zeros((HEAD_PAD, N_HEADS, HEAD_PAD), np.float32)
    for j in range(MLA_ROPE):
        place[j, :, MLA_NOPE + j] = 1.0
    place = jnp.asarray(place.reshape(HEAD_PAD, 1024), BF16)
    w_kv_comb = jnp.concatenate([
        jnp.concatenate([w_uk_p, w_uv_p], axis=1),
        jnp.concatenate([place, jnp.zeros((HEAD_PAD, 1024), BF16)], axis=1)], axis=0)

    b_ada_mine = lax.dynamic_slice(b_ada, (0, me * 768), (1, 768))
    ada_cols, act_col = _ada_fwd(c_all.reshape(N_DEV, D_MODEL, 1), w_ada[0], b_ada_mine, "ada_fwd")
    (ada_all,) = _all_gather([ada_cols], "gather_ada")
    ada = lax.dynamic_slice(ada_all, (0, me, 0), (N_DEV, 1, 768)).reshape(1, N_ADA * D_MODEL)
    sh_m, sc_m, g_m, sh_f, sc_f, g_f = [ada[:, i * D_MODEL:(i + 1) * D_MODEL] for i in range(N_ADA)]

    inv_freq = ROPE_THETA ** (-jnp.arange(0, MLA_ROPE, 2, dtype=F32) / MLA_ROPE)
    ang = positions[0].astype(F32)[:, None] * inv_freq
    cos, sin = jnp.cos(ang), jnp.sin(ang)
    rope_c = jnp.concatenate([jnp.ones((s, 64), F32), cos, cos, jnp.zeros((s, 32), F32)], axis=1)
    rope_s = jnp.concatenate([jnp.zeros((s, 64), F32), -sin, sin, jnp.zeros((s, 32), F32)], axis=1)

    h1 = _rms_mod(x, norm_mix_g, sc_m, sh_m, "norm_mix")
    gates = _mm(h1, w_in_p[:, P_GM:P_FQ], "nn", BF16, "proj_gates", tn=1024)
    fq = _mm(h1, _pad_heads(w_in_p[:, P_FQ:P_FK], 64), "nn", BF16, "proj_fq", tn=1024,
             gvec=jnp.full((1, 1024), FOX_SCALE * LOG2E, F32))
    fk = _mm(h1, _pad_heads(w_in_p[:, P_FK:P_FV], 64), "nn", BF16, "proj_fk", tn=1024)
    fv = _mm(h1, _pad_heads(w_in_p[:, P_FV:P_CQ], 64), "nn", BF16, "proj_fv", tn=1024, ones_lane=True)
    lat = _mm(h1, w_in_p[:, P_CQ:], "nn", F32, "proj_latent", tn=D_IN_P - P_CQ)
    cq = lat[:, 0:384]
    ckv = lat[:, P_CKV - P_CQ:P_CKV - P_CQ + 256]
    qn, kv_in = _latent_norm(lat, q_norm_g, kv_norm_g, "latent_norm")
    q_fold = MLA_SCALE * LOG2E
    q_att = _mm(qn, w_uq_p, "nn", BF16, "q_up", rope=(rope_c * q_fold, rope_s * q_fold))
    k_att = _mm(kv_in, w_kv_comb[:, :1024], "nn", BF16, "k_up", rope=(rope_c, rope_s))
    v_att = _mm(kv_in, w_kv_comb[:, 1024:], "nn", BF16, "v_up", ones_lane=True)
    o_mla, o_mla_b, lse_mla, *late = _attn_fwd(q_att, k_att, v_att, None, "mla_fwd",
                                               gather=[shard(k) for k in LATE])
    st.update(zip(LATE, late))
    pad_o = lambda full: jnp.pad(full.reshape(N_HEADS, 64, 1024), ((0, 0), (0, 64), (0, 0))).reshape(1024, 1024)
    w_o_mla_p = pad_o(_cols_to_full(st["w_o_mla"]))
    w_o_fox_p = pad_o(_cols_to_full(st["w_o_fox"]))
    w_out_f = st["w_out"].reshape(1024, 1024)
    w_up_f = _cols_to_full(st["w_up"])
    conv_w_f = _cols_to_full(st["conv_w"])
    w_down_f = st["w_down"].reshape(D_FF, 1024)

    z = lat[:, P_FL - P_CQ:P_FL - P_CQ + 8].T.reshape(N_HEADS, s // SEQ_LANES, SEQ_LANES)
    bias_f = jnp.broadcast_to(b_forget.reshape(N_HEADS, 1, 1), (N_HEADS, 1, SEQ_LANES))
    f_cum = _forget_fwd(z, bias_f, "forget_fwd")
    f_row = (f_cum * LOG2E).reshape(N_HEADS, nblk, 1, ATT_T)
    o_fox, o_fox_b, lse_fox = _attn_fwd(fq, fk, fv, f_row, "fox_fwd")

    pm = _mm(o_mla_b, w_o_mla_p, "nn", BF16, "o_mla_proj")
    pf = _mm(o_fox_b, w_o_fox_p, "nn", BF16, "o_fox_proj")
    y = _gate_fwd(pm, pf, gates, "gate_fwd")
    x2, mix = _mm(y, w_out_f, "nn", F32, "out_proj", res=x, gvec=g_m)

    h2 = _rms_mod(x2, norm_ffn_g, sc_f, sh_f, "norm_ffn")
    u = _mm(h2, w_up_f, "nn", BF16, "ffn_up", tn=D_FF // 2)
    a = _convglu_fwd(u, conv_w_f, conv_b, "convglu_fwd")
    x3, ffn = _mm(a, w_down_f, "nn", F32, "ffn_down", res=x2, gvec=g_f, tk=2816)

    dx3, dffn, sums_final = _final_loss(x3, target, norm_final_g.reshape(1, D_MODEL), ffn, g_f, "final_loss")
    da = _mm(dffn, w_down_f, "nt", BF16, "ffn_down_dx", tn=1408)
    g_w_down = _mm(a, dffn, "tn", F32, "ffn_down_dw", tm=256, tn=1024, tk=s)
    dgate, dval, s_gate, s_val = _convglu_bwd(da, u, conv_w_f, conv_b, "convglu_bwd")
    du = _conv_transpose(dgate, conv_w_f, "conv_t_gate", 0)
    du = _conv_transpose(dval, conv_w_f, "conv_t_val", D_FF, into=du)
    dh2 = _mm(du, w_up_f, "nt", F32, "ffn_up_dx", tn=512, tk=2 * D_FF)
    g_w_up = _mm(h2, du, "tn", F32, "ffn_up_dw", tn=256, tk=s)
    dx2, dmix, sums_ffn = _rms_mod_bwd(dh2, x2, norm_ffn_g, sc_f, dx3, "norm_ffn_bwd", branch=(mix, g_m))

    dy = _mm(dmix, w_out_f, "nt", BF16, "out_proj_dx")
    g_w_out = _mm(y, dmix, "tn", F32, "out_proj_dw", tn=256, tk=s)
    dpm, dpf, dgates = _gate_bwd(dy, pm, pf, gates, "gate_bwd")
    do_mla_b = _mm(dpm, w_o_mla_p, "nt", BF16, "o_mla_dx", tn=1024)
    do_fox_b = _mm(dpf, w_o_fox_p, "nt", BF16, "o_fox_dx", tn=1024)
    g_w_o_mla_p = _mm(o_mla_b, dpm, "tn", F32, "o_mla_dw", tn=256, tk=s)
    g_w_o_fox_p = _mm(o_fox_b, dpf, "tn", F32, "o_fox_dw", tn=256, tk=s)

    unpad_o = lambda g: g.reshape(N_HEADS, HEAD_PAD, 1024)[:, :64].reshape(512, 1024)
    g_conv_w = jnp.concatenate([s_gate[0:3], s_val[0:3]], axis=1)
    g_blocks = {
        "w_o_mla": _full_to_cols(unpad_o(g_w_o_mla_p), 128), "w_o_fox": _full_to_cols(unpad_o(g_w_o_fox_p), 128),
        "w_out": g_w_out.reshape(N_DEV, 128, 1024), "w_up": _full_to_cols(g_w_up, 704),
        "conv_w": _full_to_cols(g_conv_w, 704), "w_down": g_w_down.reshape(N_DEV, 352, 1024)}

    delta_mla = _attn_delta(o_mla, do_mla_b, "mla_delta")
    dq_rot, dk_rot, dv_mla, *late_recv = _attn_bwd(
        q_att, k_att, v_att, do_mla_b, lse_mla, delta_mla, None, MLA_SCALE, 1.0 / LOG2E, "mla_bwd", BF16,
        scatter=[g_blocks[k].astype(BF16) for k in LATE])
    dq_pre = _rope(dq_rot, rope_c, -rope_s, "rope_q_bwd")
    dkv_pre = _rope_bwd_kv(dk_rot, dv_mla, rope_c, -rope_s, "rope_kv_bwd")
    dqn = _mm(dq_pre, w_uq_p, "nt", F32, "q_up_dx")
    g_w_uq_p = _mm(qn, dq_pre, "tn", F32, "q_up_dw", tk=s)
    dkv_in = _mm(dkv_pre, w_kv_comb, "nt", F32, "kv_up_dx")
    g_w_kv_comb = _mm(kv_in, dkv_pre, "tn", F32, "kv_up_dw", tk=s)
    dcq, sums_q = _rms_mod_bwd(dqn, cq, q_norm_g, jnp.zeros((1, 384), F32), None, "q_norm_bwd")
    dckv, sums_kv = _rms_mod_bwd(dkv_in[:, :256], ckv, kv_norm_g, jnp.zeros((1, 256), F32), None, "kv_norm_bwd")
    delta_fox = _attn_delta(o_fox, do_fox_b, "fox_delta")
    dfq, dfk, dfv, dfr, dfc = _attn_bwd(fq, fk, fv, do_fox_b, lse_fox, delta_fox, f_row,
                                        FOX_SCALE, 1.0 / LOG2E, "fox_bwd", BF16)
    df_rows = dfr.reshape(N_HEADS, s // SEQ_LANES, SEQ_LANES)
    df_cols = dfc.reshape(N_HEADS, s // SEQ_LANES, SEQ_LANES)
    dz, db_f = _forget_bwd(df_rows, df_cols, z, bias_f, "forget_bwd")
    dfl = jnp.pad(dz.reshape(N_HEADS, s).T, ((0, 0), (0, 128 - N_HEADS)))

    dproj = jnp.concatenate([
        dgates, _unpad_heads(dfq, 64), _unpad_heads(dfk, 64), _unpad_heads(dfv, 64),
        dcq.astype(BF16), dckv.astype(BF16), dkv_in[:, 256:384].astype(BF16), dfl.astype(BF16)], axis=1)
    g_w_in_p = _mm(h1, dproj, "tn", F32, "proj_in_dw", tm=512, tn=640, tk=s)

    g_w_in = _w_in_unpadded(g_w_in_p)
    g_uq = g_w_uq_p.reshape(MLA_Q_RANK, N_HEADS, HEAD_PAD)[:, :, :96].transpose(1, 0, 2)
    g_uk = g_w_kv_comb[:256, :1024].reshape(256, N_HEADS, HEAD_PAD)[:, :, :64]
    g_uv = g_w_kv_comb[:256, 1024:].reshape(256, N_HEADS, HEAD_PAD)[:, :, :64]
    g_ukv = jnp.concatenate([g_uk, g_uv], axis=2).transpose(1, 0, 2)
    g_blocks.update({"w_in": _full_to_cols(g_w_in, 533), "w_uq": g_uq, "w_ukv": g_ukv})
    dh1, *early_recv = _mm(dproj, w_in_p, "nt", F32, "proj_in_dx", tn=512, tk=D_IN_P,
                           scatter=[g_blocks[k].astype(BF16) for k in EARLY])
    grad_x, sums_mix = _rms_mod_bwd(dh1, x, norm_mix_g, sc_m, dx2, "norm_mix_bwd")
    g_big, d_big, nm_big, nv_big = {}, {}, {}, {}
    for k, parts in zip(BIG, list(early_recv) + list(late_recv)):
        g_big[k], d_big[k], nm_big[k], nv_big[k] = [
            t[None] for t in _adamw(parts, big_w[k][0], big_m[k][0], big_v[k][0], "adamw_" + k)]

    dada = jnp.concatenate([sums_mix[0:1], sums_mix[1:2], sums_ffn[3:4], sums_ffn[0:1], sums_ffn[1:2], sums_final[2:3]],
                           axis=1)
    small_part = _pack_small({
        "b_ada": dada, "norm_mix_g": sums_mix[2:3], "q_norm_g": sums_q[2:3], "kv_norm_g": sums_kv[2:3],
        "b_forget": db_f[:, 0, 0], "norm_ffn_g": sums_ffn[2:3],
        "conv_b": jnp.concatenate([s_gate[3:4], s_val[3:4]], axis=1), "norm_final_g": sums_final[0:1],
        "loss": sums_final[1:2, 0:1]})
    (small_all,) = _all_gather([small_part], "gather_small")
    zero1 = jnp.zeros((1,), F32)
    small_w = {"b_ada": b_ada, "norm_mix_g": norm_mix_g, "q_norm_g": q_norm_g, "kv_norm_g": kv_norm_g,
               "b_forget": b_forget, "norm_ffn_g": norm_ffn_g, "conv_b": conv_b, "norm_final_g": norm_final_g,
               "loss": zero1}
    small_m = {"b_ada": m_b_ada, "norm_mix_g": m_norm_mix_g, "q_norm_g": m_q_norm_g, "kv_norm_g": m_kv_norm_g,
               "b_forget": m_b_forget, "norm_ffn_g": m_norm_ffn_g, "conv_b": m_conv_b,
               "norm_final_g": m_norm_final_g, "loss": zero1}
    small_v = {"b_ada": v_b_ada, "norm_mix_g": v_norm_mix_g, "q_norm_g": v_q_norm_g, "kv_norm_g": v_kv_norm_g,
               "b_forget": v_b_forget, "norm_ffn_g": v_norm_ffn_g, "conv_b": v_conv_b,
               "norm_final_g": v_norm_final_g, "loss": zero1}
    g_sm, d_sm, nm_sm, nv_sm = _adamw(small_all, _pack_small(small_w), _pack_small(small_m), _pack_small(small_v),
                                      "adamw_small")
    loss = g_sm[0, SMALL_OFF["loss"]]

    dada_all = small_all[:, 0, SMALL_OFF["b_ada"]:SMALL_OFF["b_ada"] + N_ADA * D_MODEL]
    dada_mine = lax.dynamic_slice(dada_all, (0, me * 768), (N_DEV, 768))
    g_ada_local = _ada_bwd(act_col, dada_mine, "ada_bwd")
    g_ada, d_ada, nm_ada, nv_ada = _adamw(g_ada_local[None], w_ada[0], m_w_ada[0], v_w_ada[0], "adamw_ada")

    def small_out(t, nme, shape):
        real = dict((n_, r_) for n_, r_, _ in SMALL)[nme]
        o = SMALL_OFF[nme]
        return t[0, o:o + real].reshape(shape)

    order = ["w_ada", "b_ada", "norm_mix_g", "w_in", "q_norm_g", "w_uq", "kv_norm_g", "w_ukv", "b_forget",
             "w_o_mla", "w_o_fox", "w_out", "norm_ffn_g", "w_up", "conv_w", "conv_b", "w_down", "norm_final_g"]
    small_shapes = {"b_ada": (1, 6144), "norm_mix_g": (1, 1024), "q_norm_g": (1, 384), "kv_norm_g": (1, 256),
                    "b_forget": (1, 8), "norm_ffn_g": (1, 1024), "conv_b": (1, 5632), "norm_final_g": (1024,)}

    def family(big, small, ada_t):
        out = []
        for nme in order:
            if nme == "w_ada":
                out.append(ada_t[None])
            elif nme in small_shapes:
                out.append(small_out(small, nme, small_shapes[nme]))
            else:
                out.append(big[nme])
        return out

    return (loss, grad_x[None], *family(g_big, g_sm, g_ada), *family(d_big, d_sm, d_ada),
            *family(nm_big, nm_sm, nm_ada), *family(nv_big, nv_sm, nv_ada))
```
